```python
import math
import jax, jax.numpy as jnp
from jax import lax
import numpy as np

D_MODEL = 2048
BATCH = 16
SEQ = 2048
DEPTH = 1

HEAD_DIM = 64
N_SWA_HEADS = 16
N_SWA_KV = 4
N_SB_HEADS = 16
WINDOW = 128
BLOCK = 128
D_FF = 5504
EPS = 1e-6

SWA_Q = N_SWA_HEADS * HEAD_DIM
SWA_KV = N_SWA_KV * HEAD_DIM
SB_W = N_SB_HEADS * HEAD_DIM
MIX_W = SWA_Q + SB_W
IN_W = SWA_Q + 2 * SWA_KV + 3 * SB_W

kernel_name = "hybrid_swa_sink_stickbreak_macaron"


def rmsnorm(x, g):
    xf = x.astype(jnp.float32)
    y = xf * lax.rsqrt(jnp.mean(xf * xf, axis=-1, keepdims=True) + EPS)
    return (y * g.astype(jnp.float32)).astype(x.dtype)


def swiglu(h, w_gate, w_up, w_down):
    return (jax.nn.silu(h @ w_gate) * (h @ w_up)) @ w_down


def alibi_slopes(n_heads):
    i = jnp.arange(1, n_heads + 1, dtype=jnp.float32)
    return jnp.exp2(-8.0 * i / n_heads)


def swa_sink_attention(q, k, v, sinks):
    B, S, H, D = q.shape
    Hkv = k.shape[2]
    G = H // Hkv
    nb = S // BLOCK
    qb = q.reshape(B, nb, BLOCK, Hkv, G, D)
    kb = k.reshape(B, nb, BLOCK, Hkv, D)
    vb = v.reshape(B, nb, BLOCK, Hkv, D)
    pad = ((0, 0), (1, 0), (0, 0), (0, 0), (0, 0))
    kk = jnp.concatenate([jnp.pad(kb[:, :-1], pad), kb], axis=2)
    vv = jnp.concatenate([jnp.pad(vb[:, :-1], pad), vb], axis=2)
    s = jnp.einsum('bnqhgd,bnkhd->bnhgqk', qb, kk).astype(jnp.float32) * (D ** -0.5)
    q_pos = jnp.arange(BLOCK)[:, None] + BLOCK
    k_pos = jnp.arange(2 * BLOCK)[None, :]
    dist = q_pos - k_pos
    valid = (dist >= 0) & (dist < WINDOW)
    blk = jnp.arange(nb)[:, None, None]
    valid = valid[None] & ((blk > 0) | (k_pos[None] >= BLOCK))
    slopes = alibi_slopes(H).reshape(Hkv, G)
    s = s - slopes[:, :, None, None] * dist.astype(jnp.float32)
    s = jnp.where(valid[None, :, None, None], s, -jnp.inf)
    sink = jnp.broadcast_to(sinks.astype(jnp.float32).reshape(1, 1, Hkv, G, 1, 1),
                            s.shape[:-1] + (1,))
    p = jax.nn.softmax(jnp.concatenate([s, sink], axis=-1), axis=-1)[..., :-1]
    o = jnp.einsum('bnhgqk,bnkhd->bnqhgd', p.astype(v.dtype), vv)
    return o.reshape(B, S, H * D)


def stick_breaking_attention(q, k, v):
    B, S, H, D = q.shape
    nb = S // BLOCK
    qb = q.reshape(B, nb, BLOCK, H, D).transpose(1, 0, 2, 3, 4)
    k_pos = jnp.arange(S)

    def one_block(args):
        q_blk, n = args
        z = jnp.einsum('bqhd,bkhd->bhqk', q_blk, k).astype(jnp.float32) * (D ** -0.5)
        q_pos = n * BLOCK + jnp.arange(BLOCK)
        mask = k_pos[None, :] < q_pos[:, None]
        log_beta = jax.nn.log_sigmoid(z)
        log_1m = jnp.where(mask, jax.nn.log_sigmoid(-z), 0.0)
        after = lax.cumsum(log_1m, axis=3, reverse=True) - log_1m
        a = jnp.where(mask, jnp.exp(log_beta + after), 0.0)
        return jnp.einsum('bhqk,bkhd->bqhd', a.astype(v.dtype), v)

    o = lax.map(one_block, (qb, jnp.arange(nb)))
    return o.transpose(1, 0, 2, 3, 4).reshape(B, S, H * D)


def _fwd_setup_inputs(seed: int = 0) -> dict:
    key = jax.random.key(seed)
    ks = jax.random.split(key, 16)
    L = DEPTH

    def w(k, shape, fan_in):
        return jax.random.normal(k, shape, jnp.float32) * (fan_in ** -0.5)

    def gain(k, shape):
        return 1.0 + 0.02 * jax.random.normal(k, shape, jnp.float32)

    return {
        "x": jax.random.normal(ks[0], (BATCH, SEQ, D_MODEL), jnp.float32),
        "ffn1_norm": gain(ks[1], (L, D_MODEL)),
        "ffn1_w_gate": w(ks[2], (L, D_MODEL, D_FF), D_MODEL),
        "ffn1_w_up": w(ks[3], (L, D_MODEL, D_FF), D_MODEL),
        "ffn1_w_down": w(ks[4], (L, D_FF, D_MODEL), D_FF),
        "mix_norm": gain(ks[5], (L, D_MODEL)),
        "w_in": w(ks[6], (L, D_MODEL, IN_W), D_MODEL),
        "swa_sinks": jax.random.normal(ks[7], (L, N_SWA_HEADS), jnp.float32),
        "swa_out_norm": gain(ks[8], (L, SWA_Q)),
        "sb_out_norm": gain(ks[9], (L, SB_W)),
        "w_out": w(ks[10], (L, MIX_W, D_MODEL), MIX_W),
        "ffn2_norm": gain(ks[11], (L, D_MODEL)),
        "ffn2_w_gate": w(ks[12], (L, D_MODEL, D_FF), D_MODEL),
        "ffn2_w_up": w(ks[13], (L, D_MODEL, D_FF), D_MODEL),
        "ffn2_w_down": w(ks[14], (L, D_FF, D_MODEL), D_FF),
        "final_norm": gain(ks[15], (D_MODEL,)),
    }


def _fwd_reference(x, ffn1_norm, ffn1_w_gate, ffn1_w_up, ffn1_w_down, mix_norm, w_in,
              swa_sinks, swa_out_norm, sb_out_norm, w_out, ffn2_norm, ffn2_w_gate,
              ffn2_w_up, ffn2_w_down, final_norm):
    B, S, _ = x.shape
    for l in range(DEPTH):
        x = x + 0.5 * swiglu(rmsnorm(x, ffn1_norm[l]), ffn1_w_gate[l], ffn1_w_up[l], ffn1_w_down[l])

        h = rmsnorm(x, mix_norm[l])
        proj = h @ w_in[l]
        o1 = SWA_Q
        o2 = o1 + SWA_KV
        o3 = o2 + SWA_KV
        o4 = o3 + SB_W
        o5 = o4 + SB_W
        qa = proj[..., :o1].reshape(B, S, N_SWA_HEADS, HEAD_DIM)
        ka = proj[..., o1:o2].reshape(B, S, N_SWA_KV, HEAD_DIM)
        va = proj[..., o2:o3].reshape(B, S, N_SWA_KV, HEAD_DIM)
        qb = proj[..., o3:o4].reshape(B, S, N_SB_HEADS, HEAD_DIM)
        kb = proj[..., o4:o5].reshape(B, S, N_SB_HEADS, HEAD_DIM)
        vb = proj[..., o5:].reshape(B, S, N_SB_HEADS, HEAD_DIM)

        ya = swa_sink_attention(qa, ka, va, swa_sinks[l])
        yb = stick_breaking_attention(qb, kb, vb)

        y = jnp.concatenate([rmsnorm(ya, swa_out_norm[l]), rmsnorm(yb, sb_out_norm[l])], axis=-1)
        x = x + y @ w_out[l]

        x = x + 0.5 * swiglu(rmsnorm(x, ffn2_norm[l]), ffn2_w_gate[l], ffn2_w_up[l], ffn2_w_down[l])
    return rmsnorm(x, final_norm)


import jax as _jax
import jax.numpy as _jnp

TWIN_FORMAT = 'train_step'
FWD_PARAMS = ['x', 'ffn1_norm', 'ffn1_w_gate', 'ffn1_w_up', 'ffn1_w_down', 'mix_norm', 'w_in', 'swa_sinks', 'swa_out_norm', 'sb_out_norm', 'w_out', 'ffn2_norm', 'ffn2_w_gate', 'ffn2_w_up', 'ffn2_w_down', 'final_norm']
TWIN_WEIGHTS = ['ffn1_norm', 'ffn1_w_gate', 'ffn1_w_up', 'ffn1_w_down', 'mix_norm', 'w_in', 'swa_sinks', 'swa_out_norm', 'sb_out_norm', 'w_out', 'ffn2_norm', 'ffn2_w_gate', 'ffn2_w_up', 'ffn2_w_down', 'final_norm']
TWIN_DIFF_INPUT = 'x'
TWIN_INPUTS = ['x', 'ffn1_norm', 'ffn1_w_gate', 'ffn1_w_up', 'ffn1_w_down', 'mix_norm', 'w_in', 'swa_sinks', 'swa_out_norm', 'sb_out_norm', 'w_out', 'ffn2_norm', 'ffn2_w_gate', 'ffn2_w_up', 'ffn2_w_down', 'final_norm', 'loss_target', 'm_ffn1_norm', 'm_ffn1_w_gate', 'm_ffn1_w_up', 'm_ffn1_w_down', 'm_mix_norm', 'm_w_in', 'm_swa_sinks', 'm_swa_out_norm', 'm_sb_out_norm', 'm_w_out', 'm_ffn2_norm', 'm_ffn2_w_gate', 'm_ffn2_w_up', 'm_ffn2_w_down', 'm_final_norm', 'v_ffn1_norm', 'v_ffn1_w_gate', 'v_ffn1_w_up', 'v_ffn1_w_down', 'v_mix_norm', 'v_w_in', 'v_swa_sinks', 'v_swa_out_norm', 'v_sb_out_norm', 'v_w_out', 'v_ffn2_norm', 'v_ffn2_w_gate', 'v_ffn2_w_up', 'v_ffn2_w_down', 'v_final_norm']
TWIN_OUTPUTS = ['loss', 'grad_x', 'grad_ffn1_norm', 'grad_ffn1_w_gate', 'grad_ffn1_w_up', 'grad_ffn1_w_down', 'grad_mix_norm', 'grad_w_in', 'grad_swa_sinks', 'grad_swa_out_norm', 'grad_sb_out_norm', 'grad_w_out', 'grad_ffn2_norm', 'grad_ffn2_w_gate', 'grad_ffn2_w_up', 'grad_ffn2_w_down', 'grad_final_norm', 'delta_ffn1_norm', 'delta_ffn1_w_gate', 'delta_ffn1_w_up', 'delta_ffn1_w_down', 'delta_mix_norm', 'delta_w_in', 'delta_swa_sinks', 'delta_swa_out_norm', 'delta_sb_out_norm', 'delta_w_out', 'delta_ffn2_norm', 'delta_ffn2_w_gate', 'delta_ffn2_w_up', 'delta_ffn2_w_down', 'delta_final_norm', 'new_m_ffn1_norm', 'new_m_ffn1_w_gate', 'new_m_ffn1_w_up', 'new_m_ffn1_w_down', 'new_m_mix_norm', 'new_m_w_in', 'new_m_swa_sinks', 'new_m_swa_out_norm', 'new_m_sb_out_norm', 'new_m_w_out', 'new_m_ffn2_norm', 'new_m_ffn2_w_gate', 'new_m_ffn2_w_up', 'new_m_ffn2_w_down', 'new_m_final_norm', 'new_v_ffn1_norm', 'new_v_ffn1_w_gate', 'new_v_ffn1_w_up', 'new_v_ffn1_w_down', 'new_v_mix_norm', 'new_v_w_in', 'new_v_swa_sinks', 'new_v_swa_out_norm', 'new_v_sb_out_norm', 'new_v_w_out', 'new_v_ffn2_norm', 'new_v_ffn2_w_gate', 'new_v_ffn2_w_up', 'new_v_ffn2_w_down', 'new_v_final_norm']
TWIN_LEAF_KINDS = {'loss': 'loss', 'grad_x': 'grad_x', 'grad_ffn1_norm': 'grad_w', 'grad_ffn1_w_gate': 'grad_w', 'grad_ffn1_w_up': 'grad_w', 'grad_ffn1_w_down': 'grad_w', 'grad_mix_norm': 'grad_w', 'grad_w_in': 'grad_w', 'grad_swa_sinks': 'grad_w', 'grad_swa_out_norm': 'grad_w', 'grad_sb_out_norm': 'grad_w', 'grad_w_out': 'grad_w', 'grad_ffn2_norm': 'grad_w', 'grad_ffn2_w_gate': 'grad_w', 'grad_ffn2_w_up': 'grad_w', 'grad_ffn2_w_down': 'grad_w', 'grad_final_norm': 'grad_w', 'delta_ffn1_norm': 'delta_w', 'delta_ffn1_w_gate': 'delta_w', 'delta_ffn1_w_up': 'delta_w', 'delta_ffn1_w_down': 'delta_w', 'delta_mix_norm': 'delta_w', 'delta_w_in': 'delta_w', 'delta_swa_sinks': 'delta_w', 'delta_swa_out_norm': 'delta_w', 'delta_sb_out_norm': 'delta_w', 'delta_w_out': 'delta_w', 'delta_ffn2_norm': 'delta_w', 'delta_ffn2_w_gate': 'delta_w', 'delta_ffn2_w_up': 'delta_w', 'delta_ffn2_w_down': 'delta_w', 'delta_final_norm': 'delta_w', 'new_m_ffn1_norm': 'new_m', 'new_m_ffn1_w_gate': 'new_m', 'new_m_ffn1_w_up': 'new_m', 'new_m_ffn1_w_down': 'new_m', 'new_m_mix_norm': 'new_m', 'new_m_w_in': 'new_m', 'new_m_swa_sinks': 'new_m', 'new_m_swa_out_norm': 'new_m', 'new_m_sb_out_norm': 'new_m', 'new_m_w_out': 'new_m', 'new_m_ffn2_norm': 'new_m', 'new_m_ffn2_w_gate': 'new_m', 'new_m_ffn2_w_up': 'new_m', 'new_m_ffn2_w_down': 'new_m', 'new_m_final_norm': 'new_m', 'new_v_ffn1_norm': 'new_v', 'new_v_ffn1_w_gate': 'new_v', 'new_v_ffn1_w_up': 'new_v', 'new_v_ffn1_w_down': 'new_v', 'new_v_mix_norm': 'new_v', 'new_v_w_in': 'new_v', 'new_v_swa_sinks': 'new_v', 'new_v_swa_out_norm': 'new_v', 'new_v_sb_out_norm': 'new_v', 'new_v_w_out': 'new_v', 'new_v_ffn2_norm': 'new_v', 'new_v_ffn2_w_gate': 'new_v', 'new_v_ffn2_w_up': 'new_v', 'new_v_ffn2_w_down': 'new_v', 'new_v_final_norm': 'new_v'}


def _forward(args):
    return _fwd_reference(*[args[k] for k in FWD_PARAMS])


def _output_shape():
    out = _jax.eval_shape(lambda: _forward(_fwd_setup_inputs(0)))
    return out.shape, out.dtype

N_MICROBATCH = 1
ADAM_LR = 0.001
ADAM_B1 = 0.9
ADAM_B2 = 0.999
ADAM_EPS = 1e-08
ADAM_WD = 0.01
ADAM_STEP = 10
PER_EXAMPLE_BATCH_AXIS = {'x': 0, 'loss_target': 0}
SHARED_INPUTS = []
_WEIGHT_DTYPES = {'ffn1_norm': _jnp.float32, 'ffn1_w_gate': _jnp.float32, 'ffn1_w_up': _jnp.float32, 'ffn1_w_down': _jnp.float32, 'mix_norm': _jnp.float32, 'w_in': _jnp.float32, 'swa_sinks': _jnp.float32, 'swa_out_norm': _jnp.float32, 'sb_out_norm': _jnp.float32, 'w_out': _jnp.float32, 'ffn2_norm': _jnp.float32, 'ffn2_w_gate': _jnp.float32, 'ffn2_w_up': _jnp.float32, 'ffn2_w_down': _jnp.float32, 'final_norm': _jnp.float32}
MOMENT_SCALE = {'ffn1_norm': 4.414385e-02, 'ffn1_w_gate': 1.916255e-02, 'ffn1_w_up': 1.861748e-02, 'ffn1_w_down': 3.043958e-02, 'mix_norm': 9.044569e-02, 'w_in': 5.605348e-02, 'swa_sinks': 5.880264e-02, 'swa_out_norm': 6.518954e-02, 'sb_out_norm': 6.358668e-02, 'w_out': 6.273986e-02, 'ffn2_norm': 2.589526e-02, 'ffn2_w_gate': 1.126685e-02, 'ffn2_w_up': 1.091767e-02, 'ffn2_w_down': 1.787719e-02, 'final_norm': 1.599206e+01}


def _to_microbatches(a, axis):
    t = _jnp.moveaxis(a, axis, 0)
    t = t.reshape((N_MICROBATCH, t.shape[0] // N_MICROBATCH) + t.shape[1:])
    return _jnp.moveaxis(t, 1, axis + 1)


def setup_inputs(seed: int = 0) -> dict:
    inp = _fwd_setup_inputs(seed)
    key = _jax.random.fold_in(_jax.random.key(seed), 7919)
    shape, _ = _output_shape()
    out = dict(inp)
    out["loss_target"] = _jax.random.normal(_jax.random.fold_in(key, 0), shape, _jnp.float32)
    for i, name in enumerate(TWIN_WEIGHTS):
        w = inp[name].astype(_jnp.float32)
        if MOMENT_SCALE is None:
            s = _jnp.sqrt(_jnp.mean(_jnp.square(w)) + 1e-30)
        else:
            s = MOMENT_SCALE[name]
        km, kv = _jax.random.split(_jax.random.fold_in(key, i + 1))
        out[name] = w
        out["m_" + name] = s * _jax.random.normal(km, w.shape, _jnp.float32)
        out["v_" + name] = (s * s) * _jax.random.uniform(kv, w.shape, _jnp.float32, 0.5, 1.5)
    if N_MICROBATCH > 1:
        for name, axis in PER_EXAMPLE_BATCH_AXIS.items():
            out[name] = _to_microbatches(out[name], axis)
    return {'x': out['x'], 'ffn1_norm': out['ffn1_norm'], 'ffn1_w_gate': out['ffn1_w_gate'], 'ffn1_w_up': out['ffn1_w_up'], 'ffn1_w_down': out['ffn1_w_down'], 'mix_norm': out['mix_norm'], 'w_in': out['w_in'], 'swa_sinks': out['swa_sinks'], 'swa_out_norm': out['swa_out_norm'], 'sb_out_norm': out['sb_out_norm'], 'w_out': out['w_out'], 'ffn2_norm': out['ffn2_norm'], 'ffn2_w_gate': out['ffn2_w_gate'], 'ffn2_w_up': out['ffn2_w_up'], 'ffn2_w_down': out['ffn2_w_down'], 'final_norm': out['final_norm'], 'loss_target': out['loss_target'], 'm_ffn1_norm': out['m_ffn1_norm'], 'm_ffn1_w_gate': out['m_ffn1_w_gate'], 'm_ffn1_w_up': out['m_ffn1_w_up'], 'm_ffn1_w_down': out['m_ffn1_w_down'], 'm_mix_norm': out['m_mix_norm'], 'm_w_in': out['m_w_in'], 'm_swa_sinks': out['m_swa_sinks'], 'm_swa_out_norm': out['m_swa_out_norm'], 'm_sb_out_norm': out['m_sb_out_norm'], 'm_w_out': out['m_w_out'], 'm_ffn2_norm': out['m_ffn2_norm'], 'm_ffn2_w_gate': out['m_ffn2_w_gate'], 'm_ffn2_w_up': out['m_ffn2_w_up'], 'm_ffn2_w_down': out['m_ffn2_w_down'], 'm_final_norm': out['m_final_norm'], 'v_ffn1_norm': out['v_ffn1_norm'], 'v_ffn1_w_gate': out['v_ffn1_w_gate'], 'v_ffn1_w_up': out['v_ffn1_w_up'], 'v_ffn1_w_down': out['v_ffn1_w_down'], 'v_mix_norm': out['v_mix_norm'], 'v_w_in': out['v_w_in'], 'v_swa_sinks': out['v_swa_sinks'], 'v_swa_out_norm': out['v_swa_out_norm'], 'v_sb_out_norm': out['v_sb_out_norm'], 'v_w_out': out['v_w_out'], 'v_ffn2_norm': out['v_ffn2_norm'], 'v_ffn2_w_gate': out['v_ffn2_w_gate'], 'v_ffn2_w_up': out['v_ffn2_w_up'], 'v_ffn2_w_down': out['v_ffn2_w_down'], 'v_final_norm': out['v_final_norm']}


def _loss(weights, diff, rest, loss_target):
    with _jax.named_scope("forward"):
        args = {**rest, TWIN_DIFF_INPUT: diff, **{k: w.astype(_WEIGHT_DTYPES[k]) for k, w in weights.items()}}
        y = _forward(args)
    with _jax.named_scope("loss_head"):
        err = _jnp.square(y.astype(_jnp.float32) - loss_target)
        return 0.5 * _jnp.sum(_jnp.mean(err, axis=-1)) if err.ndim else 0.5 * err


def _adamw(w, g, m, v):
    m = ADAM_B1 * m + (1.0 - ADAM_B1) * g
    v = ADAM_B2 * v + (1.0 - ADAM_B2) * _jnp.square(g)
    m_hat = m / (1.0 - ADAM_B1 ** ADAM_STEP)
    v_hat = v / (1.0 - ADAM_B2 ** ADAM_STEP)
    delta = -ADAM_LR * (m_hat / (_jnp.sqrt(v_hat) + ADAM_EPS) + ADAM_WD * w)
    return delta, m, v


def reference(x, ffn1_norm, ffn1_w_gate, ffn1_w_up, ffn1_w_down, mix_norm, w_in, swa_sinks, swa_out_norm, sb_out_norm, w_out, ffn2_norm, ffn2_w_gate, ffn2_w_up, ffn2_w_down, final_norm, loss_target, m_ffn1_norm, m_ffn1_w_gate, m_ffn1_w_up, m_ffn1_w_down, m_mix_norm, m_w_in, m_swa_sinks, m_swa_out_norm, m_sb_out_norm, m_w_out, m_ffn2_norm, m_ffn2_w_gate, m_ffn2_w_up, m_ffn2_w_down, m_final_norm, v_ffn1_norm, v_ffn1_w_gate, v_ffn1_w_up, v_ffn1_w_down, v_mix_norm, v_w_in, v_swa_sinks, v_swa_out_norm, v_sb_out_norm, v_w_out, v_ffn2_norm, v_ffn2_w_gate, v_ffn2_w_up, v_ffn2_w_down, v_final_norm):
    given = dict(x=x, ffn1_norm=ffn1_norm, ffn1_w_gate=ffn1_w_gate, ffn1_w_up=ffn1_w_up, ffn1_w_down=ffn1_w_down, mix_norm=mix_norm, w_in=w_in, swa_sinks=swa_sinks, swa_out_norm=swa_out_norm, sb_out_norm=sb_out_norm, w_out=w_out, ffn2_norm=ffn2_norm, ffn2_w_gate=ffn2_w_gate, ffn2_w_up=ffn2_w_up, ffn2_w_down=ffn2_w_down, final_norm=final_norm, loss_target=loss_target, m_ffn1_norm=m_ffn1_norm, m_ffn1_w_gate=m_ffn1_w_gate, m_ffn1_w_up=m_ffn1_w_up, m_ffn1_w_down=m_ffn1_w_down, m_mix_norm=m_mix_norm, m_w_in=m_w_in, m_swa_sinks=m_swa_sinks, m_swa_out_norm=m_swa_out_norm, m_sb_out_norm=m_sb_out_norm, m_w_out=m_w_out, m_ffn2_norm=m_ffn2_norm, m_ffn2_w_gate=m_ffn2_w_gate, m_ffn2_w_up=m_ffn2_w_up, m_ffn2_w_down=m_ffn2_w_down, m_final_norm=m_final_norm, v_ffn1_norm=v_ffn1_norm, v_ffn1_w_gate=v_ffn1_w_gate, v_ffn1_w_up=v_ffn1_w_up, v_ffn1_w_down=v_ffn1_w_down, v_mix_norm=v_mix_norm, v_w_in=v_w_in, v_swa_sinks=v_swa_sinks, v_swa_out_norm=v_swa_out_norm, v_sb_out_norm=v_sb_out_norm, v_w_out=v_w_out, v_ffn2_norm=v_ffn2_norm, v_ffn2_w_gate=v_ffn2_w_gate, v_ffn2_w_up=v_ffn2_w_up, v_ffn2_w_down=v_ffn2_w_down, v_final_norm=v_final_norm)
    weights = {n: given[n] for n in TWIN_WEIGHTS}
    shared = {n: given[n] for n in SHARED_INPUTS}
    per_example = {n: given[n] for n in ['x']}
    grad_fn = _jax.value_and_grad(_loss, argnums=(0, 1))

    def one_microbatch(ex, loss_target):
        ex = dict(ex)
        diff = ex.pop(TWIN_DIFF_INPUT)
        return grad_fn(weights, diff, {**shared, **ex}, loss_target)

    if N_MICROBATCH == 1:
        loss, (grad_w, grad_x) = one_microbatch(per_example, given["loss_target"])
    else:
        def body(carry, xs):
            loss_sum, grad_sum = carry
            l_k, (gw_k, gx_k) = one_microbatch(xs[0], xs[1])
            with _jax.named_scope("update"):
                return (loss_sum + l_k, _jax.tree.map(_jnp.add, grad_sum, gw_k)), gx_k

        init = (_jnp.zeros((), _jnp.float32), _jax.tree.map(_jnp.zeros_like, weights))
        (loss, grad_w), grad_x = _jax.lax.scan(body, init, (per_example, given["loss_target"]))
    with _jax.named_scope("update"):
        delta_w, new_m, new_v = {}, {}, {}
        for n in TWIN_WEIGHTS:
            delta_w[n], new_m[n], new_v[n] = _adamw(weights[n], grad_w[n], given["m_" + n], given["v_" + n])
    return (loss, grad_x, *[grad_w[n] for n in TWIN_WEIGHTS], *[delta_w[n] for n in TWIN_WEIGHTS],
            *[new_m[n] for n in TWIN_WEIGHTS], *[new_v[n] for n in TWIN_WEIGHTS])
```

```python
import functools

import jax
import jax.numpy as jnp
from jax import lax
from jax.experimental import pallas as pl
from jax.experimental.pallas import tpu as pltpu

F32 = jnp.float32
_MXU_DTYPE = jnp.bfloat16
_WIRE_DTYPE = jnp.bfloat16

EPS = 1e-6
HEAD_DIM = 64
N_SWA_HEADS = 16
N_SWA_KV = 4
N_SB_HEADS = 16
WINDOW = 128
SWA_Q = N_SWA_HEADS * HEAD_DIM
SWA_KV = N_SWA_KV * HEAD_DIM
SB_W = N_SB_HEADS * HEAD_DIM
IN_W = SWA_Q + 2 * SWA_KV + 3 * SB_W
LANES = 128
ATT_SCALE = HEAD_DIM ** -0.5

ADAM_LR = 0.001
ADAM_B1 = 0.9
ADAM_B2 = 0.999
ADAM_EPS = 1e-08
ADAM_WD = 0.01
ADAM_STEP = 10

N_DEV = 8
_VMEM_LIMIT_BYTES = 48 * 1024 * 1024
_F_TILE = 512


def _params(*semantics):
    return pltpu.CompilerParams(dimension_semantics=semantics, vmem_limit_bytes=_VMEM_LIMIT_BYTES)


def _tile(n, pref, align):
    t = min(n, pref)
    t -= t % align
    while t >= align:
        if n % t == 0:
            return t
        t -= align
    return n


def _dot(a, b):
    return lax.dot_general(a, b, (((1,), (0,)), ((), ())), preferred_element_type=F32)


def _dot_nt(a, b):
    return lax.dot_general(a, b, (((1,), (1,)), ((), ())), preferred_element_type=F32)


def _dot_tn(a, b):
    return lax.dot_general(a, b, (((0,), (0,)), ((), ())), preferred_element_type=F32)


def _rms_fwd(x, g, name):
    T, D = x.shape
    tm = _tile(T, 512, 16)

    def body(x_ref, g_ref, o_ref):
        xv = x_ref[...]
        r = lax.rsqrt(jnp.mean(xv * xv, axis=-1, keepdims=True) + EPS)
        o_ref[...] = (xv * r * g_ref[...]).astype(o_ref.dtype)

    return pl.pallas_call(
        body, name=name, grid=(T // tm,),
        out_shape=jax.ShapeDtypeStruct((T, D), _MXU_DTYPE),
        in_specs=[pl.BlockSpec((tm, D), lambda i: (i, 0)), pl.BlockSpec((1, D), lambda i: (0, 0))],
        out_specs=pl.BlockSpec((tm, D), lambda i: (i, 0)),
        compiler_params=_params("parallel"),
    )(x, g)


def _rms_bwd_rows(dh, xv, g):
    r = lax.rsqrt(jnp.mean(xv * xv, axis=-1, keepdims=True) + EPS)
    xhat = xv * r
    u = dh * g
    dx = r * (u - xhat * jnp.mean(u * xhat, axis=-1, keepdims=True))
    return dx, dh * xhat


def _rms_bwd(dh, x, g, dres, name):
    T, D = x.shape
    tm = _tile(T, 256, 16)

    def body(dh_ref, x_ref, g_ref, dres_ref, dx_ref, dxb_ref, dg_ref):
        @pl.when(pl.program_id(0) == 0)
        def _():
            dg_ref[...] = jnp.zeros_like(dg_ref)

        dx, dgr = _rms_bwd_rows(dh_ref[...], x_ref[...], g_ref[...])
        dx = dres_ref[...] + dx
        dx_ref[...] = dx
        dxb_ref[...] = dx.astype(dxb_ref.dtype)
        dg_ref[...] += jnp.sum(dgr, axis=0, keepdims=True)

    row = pl.BlockSpec((tm, D), lambda i: (i, 0))
    vec = pl.BlockSpec((1, D), lambda i: (0, 0))
    return pl.pallas_call(
        body, name=name, grid=(T // tm,),
        out_shape=(jax.ShapeDtypeStruct((T, D), F32), jax.ShapeDtypeStruct((T, D), _MXU_DTYPE),
                   jax.ShapeDtypeStruct((1, D), F32)),
        in_specs=[row, row, vec, row], out_specs=(row, row, vec),
        compiler_params=_params("arbitrary"),
    )(dh, x, g, dres)


def _loss_head(x, g, tgt, name):
    T, D = x.shape
    tm = _tile(T, 256, 16)

    def body(x_ref, g_ref, t_ref, dx_ref, dxb_ref, dg_ref, loss_ref):
        @pl.when(pl.program_id(0) == 0)
        def _():
            dg_ref[...] = jnp.zeros_like(dg_ref)
            loss_ref[...] = jnp.zeros_like(loss_ref)

        xv = x_ref[...]
        gv = g_ref[...]
        r = lax.rsqrt(jnp.mean(xv * xv, axis=-1, keepdims=True) + EPS)
        xhat = xv * r
        diff = xhat * gv - t_ref[...]
        tok = jnp.mean(diff * diff, axis=-1, keepdims=True)
        loss_ref[...] += 0.5 * jnp.sum(tok, axis=0, keepdims=True)
        dy = diff / D
        u = dy * gv
        dx = r * (u - xhat * jnp.mean(u * xhat, axis=-1, keepdims=True))
        dx_ref[...] = dx
        dxb_ref[...] = dx.astype(dxb_ref.dtype)
        dg_ref[...] += jnp.sum(dy * xhat, axis=0, keepdims=True)

    row = pl.BlockSpec((tm, D), lambda i: (i, 0))
    vec = pl.BlockSpec((1, D), lambda i: (0, 0))
    return pl.pallas_call(
        body, name=name, grid=(T // tm,),
        out_shape=(jax.ShapeDtypeStruct((T, D), F32), jax.ShapeDtypeStruct((T, D), _MXU_DTYPE),
                   jax.ShapeDtypeStruct((1, D), F32), jax.ShapeDtypeStruct((1, LANES), F32)),
        in_specs=[row, vec, row],
        out_specs=(row, row, vec, pl.BlockSpec((1, LANES), lambda i: (0, 0))),
        compiler_params=_params("arbitrary"),
    )(x, g, tgt)


def _outnorm_fwd(ya, yb, ga, gb, name):
    T, W = ya.shape
    tm = _tile(T, 512, 16)

    def body(ya_ref, yb_ref, ga_ref, gb_ref, o_ref):
        for k, (y_ref, g_ref) in enumerate(((ya_ref, ga_ref), (yb_ref, gb_ref))):
            yv = y_ref[...]
            r = lax.rsqrt(jnp.mean(yv * yv, axis=-1, keepdims=True) + EPS)
            o_ref[:, k * W:(k + 1) * W] = (yv * r * g_ref[...]).astype(o_ref.dtype)

    row = pl.BlockSpec((tm, W), lambda i: (i, 0))
    vec = pl.BlockSpec((1, W), lambda i: (0, 0))
    return pl.pallas_call(
        body, name=name, grid=(T // tm,),
        out_shape=jax.ShapeDtypeStruct((T, 2 * W), _MXU_DTYPE),
        in_specs=[row, row, vec, vec], out_specs=pl.BlockSpec((tm, 2 * W), lambda i: (i, 0)),
        compiler_params=_params("parallel"),
    )(ya, yb, ga, gb)


def _outnorm_bwd(dyn, ya, yb, ga, gb, name):
    T, W = ya.shape
    tm = _tile(T, 256, 16)

    def body(d_ref, ya_ref, yb_ref, ga_ref, gb_ref, dya_ref, dyb_ref, dga_ref, dgb_ref):
        @pl.when(pl.program_id(0) == 0)
        def _():
            dga_ref[...] = jnp.zeros_like(dga_ref)
            dgb_ref[...] = jnp.zeros_like(dgb_ref)

        for k, (y_ref, g_ref, dy_ref, dg_ref) in enumerate(
                ((ya_ref, ga_ref, dya_ref, dga_ref), (yb_ref, gb_ref, dyb_ref, dgb_ref))):
            dy, dgr = _rms_bwd_rows(d_ref[:, k * W:(k + 1) * W], y_ref[...], g_ref[...])
            dy_ref[...] = dy.astype(dy_ref.dtype)
            dg_ref[...] += jnp.sum(dgr, axis=0, keepdims=True)

    row = pl.BlockSpec((tm, W), lambda i: (i, 0))
    vec = pl.BlockSpec((1, W), lambda i: (0, 0))
    return pl.pallas_call(
        body, name=name, grid=(T // tm,),
        out_shape=(jax.ShapeDtypeStruct((T, W), _MXU_DTYPE), jax.ShapeDtypeStruct((T, W), _MXU_DTYPE),
                   jax.ShapeDtypeStruct((1, W), F32), jax.ShapeDtypeStruct((1, W), F32)),
        in_specs=[pl.BlockSpec((tm, 2 * W), lambda i: (i, 0)), row, row, vec, vec],
        out_specs=(row, row, vec, vec),
        compiler_params=_params("arbitrary"),
    )(dyn, ya, yb, ga, gb)


def _ffn_gu(h, wg_t, wu_t, name):
    T, D = h.shape
    Fp = wg_t.shape[0]
    tm = _tile(T, 1024, 16)
    tn = _tile(Fp, _F_TILE, LANES)

    def body(h_ref, wg_ref, wu_ref, g_ref, u_ref, a_ref):
        hv = h_ref[...]
        g = _dot_nt(hv, wg_ref[...])
        u = _dot_nt(hv, wu_ref[...])
        g_ref[...] = g.astype(g_ref.dtype)
        u_ref[...] = u.astype(u_ref.dtype)
        a_ref[...] = (g * jax.nn.sigmoid(g) * u).astype(a_ref.dtype)

    act = pl.BlockSpec((tm, tn), lambda n, m: (m, n))
    wsp = pl.BlockSpec((tn, D), lambda n, m: (n, 0))
    out = jax.ShapeDtypeStruct((T, Fp), _MXU_DTYPE)
    return pl.pallas_call(
        body, name=name, grid=(Fp // tn, T // tm), out_shape=(out, out, out),
        in_specs=[pl.BlockSpec((tm, D), lambda n, m: (m, 0)), wsp, wsp],
        out_specs=(act, act, act),
        compiler_params=_params("parallel", "parallel"),
    )(h, wg_t, wu_t)


def _ffn_bwd_act(dxb, wd, G, U, name):
    T, D = dxb.shape
    Fp = wd.shape[0]
    tm = _tile(T, 1024, 16)
    tn = _tile(Fp, _F_TILE, LANES)

    def body(e_ref, wd_ref, g_ref, u_ref, dg_ref, du_ref):
        da = 0.5 * _dot_nt(e_ref[...], wd_ref[...])
        g = g_ref[...].astype(F32)
        u = u_ref[...].astype(F32)
        s = jax.nn.sigmoid(g)
        du_ref[...] = (da * (g * s)).astype(du_ref.dtype)
        dg_ref[...] = (da * u * (s * (1.0 + g * (1.0 - s)))).astype(dg_ref.dtype)

    act = pl.BlockSpec((tm, tn), lambda n, m: (m, n))
    out = jax.ShapeDtypeStruct((T, Fp), _MXU_DTYPE)
    return pl.pallas_call(
        body, name=name, grid=(Fp // tn, T // tm), out_shape=(out, out),
        in_specs=[pl.BlockSpec((tm, D), lambda n, m: (m, 0)), pl.BlockSpec((tn, D), lambda n, m: (n, 0)),
                  act, act],
        out_specs=(act, act),
        compiler_params=_params("parallel", "parallel"),
    )(dxb, wd, G, U)


def _mm_nt(a, b, out_dtype, name):
    M, K = a.shape
    N = b.shape[0]
    tm = _tile(M, 1024, 16)
    tn = _tile(N, 512, LANES)

    def body(a_ref, b_ref, o_ref):
        o_ref[...] = _dot_nt(a_ref[...], b_ref[...]).astype(o_ref.dtype)

    return pl.pallas_call(
        body, name=name, grid=(N // tn, M // tm), out_shape=jax.ShapeDtypeStruct((M, N), out_dtype),
        in_specs=[pl.BlockSpec((tm, K), lambda n, m: (m, 0)), pl.BlockSpec((tn, K), lambda n, m: (n, 0))],
        out_specs=pl.BlockSpec((tm, tn), lambda n, m: (m, n)),
        compiler_params=_params("parallel", "parallel"),
    )(a, b)


def _mm_nn(pairs, res, alpha, out_dtype, name):
    M, K = pairs[0][0].shape
    N = pairs[0][1].shape[1]
    tm = _tile(M, 512, 16)
    tn = _tile(N, 2048, LANES)
    tk = _tile(K, 512, LANES)
    nk = K // tk
    n_pairs = len(pairs)

    def body(*refs):
        ab = refs[:2 * n_pairs]
        res_ref = refs[2 * n_pairs] if res is not None else None
        o_ref, acc_ref = refs[-2], refs[-1]
        k = pl.program_id(2)

        @pl.when(k == 0)
        def _():
            acc_ref[...] = jnp.zeros_like(acc_ref)

        part = _dot(ab[0][...], ab[1][...])
        for i in range(1, n_pairs):
            part = part + _dot(ab[2 * i][...], ab[2 * i + 1][...])
        acc_ref[...] += part

        @pl.when(k == nk - 1)
        def _():
            out = alpha * acc_ref[...]
            if res_ref is not None:
                out = res_ref[...] + out
            o_ref[...] = out.astype(o_ref.dtype)

    in_specs, args = [], []
    for a, b in pairs:
        in_specs += [pl.BlockSpec((tm, tk), lambda m, n, k: (m, k)), pl.BlockSpec((tk, tn), lambda m, n, k: (k, n))]
        args += [a, b]
    if res is not None:
        in_specs.append(pl.BlockSpec((tm, tn), lambda m, n, k: (m, n)))
        args.append(res)
    return pl.pallas_call(
        body, name=name, grid=(M // tm, N // tn, nk), out_shape=jax.ShapeDtypeStruct((M, N), out_dtype),
        in_specs=in_specs, out_specs=pl.BlockSpec((tm, tn), lambda m, n, k: (m, n)),
        scratch_shapes=[pltpu.VMEM((tm, tn), F32)],
        compiler_params=_params("parallel", "parallel", "arbitrary"),
    )(*args)


def _mm_tn(a, b, alpha, out_dtype, name):
    K, M = a.shape
    N = b.shape[1]
    tm = _tile(M, 512, LANES)
    tn = _tile(N, 2048, LANES)
    tk = _tile(K, 512, 16)
    nk = K // tk

    def body(a_ref, b_ref, o_ref, acc_ref):
        k = pl.program_id(2)

        @pl.when(k == 0)
        def _():
            acc_ref[...] = jnp.zeros_like(acc_ref)

        acc_ref[...] += _dot_tn(a_ref[...], b_ref[...])

        @pl.when(k == nk - 1)
        def _():
            o_ref[...] = (alpha * acc_ref[...]).astype(o_ref.dtype)

    return pl.pallas_call(
        body, name=name, grid=(M // tm, N // tn, nk), out_shape=jax.ShapeDtypeStruct((M, N), out_dtype),
        in_specs=[pl.BlockSpec((tk, tm), lambda m, n, k: (k, m)), pl.BlockSpec((tk, tn), lambda m, n, k: (k, n))],
        out_specs=pl.BlockSpec((tm, tn), lambda m, n, k: (m, n)),
        scratch_shapes=[pltpu.VMEM((tm, tn), F32)],
        compiler_params=_params("parallel", "parallel", "arbitrary"),
    )(a, b)


def _half_masks():
    lane = lax.broadcasted_iota(jnp.int32, (1, LANES), 1)
    return (lane < HEAD_DIM, lane >= HEAD_DIM)


def _swap_halves(v):
    return pltpu.roll(v.astype(F32), HEAD_DIM, 1).astype(v.dtype)


def _swa_geometry(n):
    qi = lax.broadcasted_iota(jnp.int32, (WINDOW, 2 * WINDOW), 0)
    kp = lax.broadcasted_iota(jnp.int32, (WINDOW, 2 * WINDOW), 1)
    dist = (WINDOW + qi) - kp
    valid = (dist >= 0) & (dist < WINDOW) & ((n > 0) | (kp >= WINDOW))
    return dist.astype(F32), valid


def _swa_slope(h):
    return 2.0 ** (-8.0 * (h + 1) / N_SWA_HEADS)


def _swa_probs(qm, kx, sink, slope, distf, valid):
    s = _dot_nt(qm, kx) * ATT_SCALE - slope * distf
    s = jnp.where(valid, s, -1e30)
    m = jnp.maximum(jnp.max(s, axis=1, keepdims=True), sink)
    p = jnp.exp(s - m)
    e_sink = jnp.exp(sink - m)
    den = jnp.sum(p, axis=1, keepdims=True) + e_sink
    return p / den, e_sink / den


def _swa_specs(B, S):
    nb = S // WINDOW
    kcol = SWA_Q // SWA_KV
    cur = lambda b, n: (b * nb + n, kcol)
    prev = lambda b, n: (b * nb + jnp.maximum(n - 1, 0), kcol)
    curv = lambda b, n: (b * nb + n, kcol + 1)
    prevv = lambda b, n: (b * nb + jnp.maximum(n - 1, 0), kcol + 1)
    q_spec = pl.BlockSpec((WINDOW, SWA_Q), lambda b, n: (b * nb + n, 0))
    kv = [pl.BlockSpec((WINDOW, SWA_KV), f) for f in (prev, cur, prevv, curv)]
    sink_spec = pl.BlockSpec(memory_space=pltpu.SMEM)
    return nb, q_spec, kv, sink_spec


def _swa_kv_views(kp_ref, kc_ref, vp_ref, vc_ref, g):
    hm = _half_masks()
    c0 = (g // 2) * LANES
    k_all = jnp.concatenate([kp_ref[:, c0:c0 + LANES], kc_ref[:, c0:c0 + LANES]], axis=0)
    v_all = jnp.concatenate([vp_ref[:, c0:c0 + LANES], vc_ref[:, c0:c0 + LANES]], axis=0)
    b = g % 2
    ks, vs = [None, None], [None, None]
    ks[b], vs[b] = k_all, v_all
    ks[1 - b], vs[1 - b] = _swap_halves(k_all), _swap_halves(v_all)
    ks = [jnp.where(hm[a], ks[a], 0) for a in range(2)]
    vs = [jnp.where(hm[a], vs[a], 0) for a in range(2)]
    return ks, vs


def _swa_fwd(proj, sinks, B, S, name):
    T = B * S
    nb, q_spec, kv_specs, sink_spec = _swa_specs(B, S)

    def body(sink_ref, q_ref, kp_ref, kc_ref, vp_ref, vc_ref, y_ref):
        hm = _half_masks()
        distf, valid = _swa_geometry(pl.program_id(1))
        for g in range(N_SWA_KV):
            ks, vs = _swa_kv_views(kp_ref, kc_ref, vp_ref, vc_ref, g)
            for pp in (2 * g, 2 * g + 1):
                q_pair = q_ref[:, pp * LANES:(pp + 1) * LANES]
                o_pair = jnp.zeros((WINDOW, LANES), F32)
                for a in range(2):
                    h = 2 * pp + a
                    qm = jnp.where(hm[a], q_pair, 0)
                    p, _ = _swa_probs(qm, ks[a], sink_ref[0, h], _swa_slope(h), distf, valid)
                    o_pair = o_pair + _dot(p.astype(_MXU_DTYPE), vs[a])
                y_ref[:, pp * LANES:(pp + 1) * LANES] = o_pair

    return pl.pallas_call(
        body, name=name, grid=(B, nb), out_shape=jax.ShapeDtypeStruct((T, SWA_Q), F32),
        in_specs=[sink_spec, q_spec] + kv_specs,
        out_specs=pl.BlockSpec((WINDOW, SWA_Q), lambda b, n: (b * nb + n, 0)),
        compiler_params=_params("parallel", "parallel"),
    )(sinks, proj, proj, proj, proj, proj)


def _swa_bwd(proj, sinks, dya, B, S, name):
    T = B * S
    nb, q_spec, kv_specs, sink_spec = _swa_specs(B, S)

    def body(sink_ref, q_ref, kp_ref, kc_ref, vp_ref, vc_ref, do_ref,
             dq_ref, dk_ref, dv_ref, dsink_ref, dk_acc, dv_acc):
        b_id, n = pl.program_id(0), pl.program_id(1)
        hm = _half_masks()
        lane = lax.broadcasted_iota(jnp.int32, (1, LANES), 1)

        @pl.when((b_id == 0) & (n == 0))
        def _():
            dsink_ref[...] = jnp.zeros_like(dsink_ref)

        @pl.when(n == 0)
        def _():
            dk_acc[...] = jnp.zeros_like(dk_acc)
            dv_acc[...] = jnp.zeros_like(dv_acc)

        distf, valid = _swa_geometry(n)
        r_prev = pl.multiple_of(jnp.maximum(n - 1, 0) * WINDOW, WINDOW)
        r_cur = pl.multiple_of(n * WINDOW, WINDOW)
        dsink = jnp.zeros((1, LANES), F32)
        for g in range(N_SWA_KV):
            ks, vs = _swa_kv_views(kp_ref, kc_ref, vp_ref, vc_ref, g)
            dk_g = [jnp.zeros((2 * WINDOW, LANES), F32) for _ in range(2)]
            dv_g = [jnp.zeros((2 * WINDOW, LANES), F32) for _ in range(2)]
            for pp in (2 * g, 2 * g + 1):
                q_pair = q_ref[:, pp * LANES:(pp + 1) * LANES]
                do_pair = do_ref[:, pp * LANES:(pp + 1) * LANES]
                dq_pair = jnp.zeros((WINDOW, LANES), F32)
                for a in range(2):
                    h = 2 * pp + a
                    qm = jnp.where(hm[a], q_pair, 0)
                    dom = jnp.where(hm[a], do_pair, 0)
                    p, p_sink = _swa_probs(qm, ks[a], sink_ref[0, h], _swa_slope(h), distf, valid)
                    dp = _dot_nt(dom, vs[a])
                    delta = jnp.sum(p * dp, axis=1, keepdims=True)
                    ds = (p * (dp - delta) * ATT_SCALE).astype(_MXU_DTYPE)
                    dsink = dsink + jnp.where(lane == h, -jnp.sum(p_sink * delta), 0.0)
                    dq_pair = dq_pair + _dot(ds, ks[a])
                    dk_g[a] = dk_g[a] + _dot_tn(ds, qm)
                    dv_g[a] = dv_g[a] + _dot_tn(p.astype(_MXU_DTYPE), dom)
                dq_ref[:, pp * LANES:(pp + 1) * LANES] = dq_pair.astype(dq_ref.dtype)
            bsel = g % 2
            dk_t = dk_g[bsel] + pltpu.roll(dk_g[1 - bsel], HEAD_DIM, 1)
            dv_t = dv_g[bsel] + pltpu.roll(dv_g[1 - bsel], HEAD_DIM, 1)
            c0 = (g // 2) * LANES
            dk_acc[pl.ds(r_prev, WINDOW), c0:c0 + LANES] += dk_t[:WINDOW]
            dk_acc[pl.ds(r_cur, WINDOW), c0:c0 + LANES] += dk_t[WINDOW:]
            dv_acc[pl.ds(r_prev, WINDOW), c0:c0 + LANES] += dv_t[:WINDOW]
            dv_acc[pl.ds(r_cur, WINDOW), c0:c0 + LANES] += dv_t[WINDOW:]
        dsink_ref[...] += dsink

        @pl.when(n == nb - 1)
        def _():
            dk_ref[...] = dk_acc[...].astype(dk_ref.dtype)
            dv_ref[...] = dv_acc[...].astype(dv_ref.dtype)

    seq_kv = pl.BlockSpec((S, SWA_KV), lambda b, n: (b, 0))
    return pl.pallas_call(
        body, name=name, grid=(B, nb),
        out_shape=(jax.ShapeDtypeStruct((T, SWA_Q), _MXU_DTYPE), jax.ShapeDtypeStruct((T, SWA_KV), _MXU_DTYPE),
                   jax.ShapeDtypeStruct((T, SWA_KV), _MXU_DTYPE), jax.ShapeDtypeStruct((1, LANES), F32)),
        in_specs=[sink_spec, q_spec] + kv_specs + [pl.BlockSpec((WINDOW, SWA_Q), lambda b, n: (b * nb + n, 0))],
        out_specs=(pl.BlockSpec((WINDOW, SWA_Q), lambda b, n: (b * nb + n, 0)), seq_kv, seq_kv,
                   pl.BlockSpec((1, LANES), lambda b, n: (0, 0))),
        scratch_shapes=[pltpu.VMEM((S, SWA_KV), F32), pltpu.VMEM((S, SWA_KV), F32)],
        compiler_params=_params("arbitrary", "arbitrary"),
    )(sinks, proj, proj, proj, proj, proj, dya)


def _tri_cumsum(x, tri):
    hi = x.astype(_MXU_DTYPE)
    lo = (x - hi.astype(F32)).astype(_MXU_DTYPE)
    return _dot(hi, tri) + _dot(lo, tri)


def _log_sigmoid(z):
    return jnp.minimum(z, 0.0) - jnp.log(1.0 + jnp.exp(-jnp.abs(z)))


def _sb_specs(B, S):
    qb = (SWA_Q + 2 * SWA_KV) // LANES
    kb = qb + SB_W // LANES
    vb = kb + SB_W // LANES
    return [pl.BlockSpec((S, LANES), functools.partial(lambda b, p, c: (b, c + p), c=c)) for c in (qb, kb, vb)]


def _sb_fwd(proj, B, S, name):
    T = B * S
    tq = _tile(S, 256, LANES)
    nq = S // tq

    def body(q_ref, k_ref, v_ref, y_ref, tot_ref):
        hm = _half_masks()
        ji = lax.broadcasted_iota(jnp.int32, (tq, tq), 0)
        si = lax.broadcasted_iota(jnp.int32, (tq, tq), 1)
        tri_after = (ji > si).astype(_MXU_DTYPE)

        def q_loop(qi, carry):
            r0 = pl.multiple_of(qi * tq, tq)
            q_pair = q_ref[pl.ds(r0, tq), :]
            outs = []
            for a in range(2):
                qm = jnp.where(hm[a], q_pair, 0)

                def k_loop(it, c, qm=qm, a=a):
                    acc, car = c
                    c0 = pl.multiple_of((qi - it) * tq, tq)
                    z = _dot_nt(qm, k_ref[pl.ds(c0, tq), :]) * ATT_SCALE
                    mask = (c0 + si) < (r0 + ji)
                    lb = _log_sigmoid(z)
                    l1m = jnp.where(mask, lb - z, 0.0)
                    after = _tri_cumsum(l1m, tri_after) + car
                    att = jnp.where(mask, jnp.exp(lb + after), 0.0)
                    vm = jnp.where(hm[a], v_ref[pl.ds(c0, tq), :], 0)
                    acc = acc + _dot(att.astype(_MXU_DTYPE), vm)
                    car = car + jnp.sum(l1m, axis=1, keepdims=True)
                    return acc, car

                outs.append(lax.fori_loop(0, qi + 1, k_loop,
                                          (jnp.zeros((tq, LANES), F32), jnp.zeros((tq, 1), F32))))
            y_ref[pl.ds(r0, tq), :] = outs[0][0] + outs[1][0]
            tot_ref[pl.ds(r0, tq), :] = jnp.where(hm[0], outs[0][1], outs[1][1])
            return carry

        lax.fori_loop(0, nq, q_loop, 0)

    out_spec = pl.BlockSpec((S, LANES), lambda b, p: (b, p))
    return pl.pallas_call(
        body, name=name, grid=(B, SB_W // LANES),
        out_shape=(jax.ShapeDtypeStruct((T, SB_W), F32), jax.ShapeDtypeStruct((T, SB_W), F32)),
        in_specs=_sb_specs(B, S), out_specs=(out_spec, out_spec),
        compiler_params=_params("parallel", "parallel"),
    )(proj, proj, proj)


def _sb_bwd(proj, tot, dyb, B, S, name):
    T = B * S
    tq = _tile(S, 256, LANES)
    nq = S // tq

    def body(q_ref, k_ref, v_ref, do_ref, tot_ref, dq_ref, dk_ref, dv_ref, dk_acc, dv_acc):
        hm = _half_masks()
        ji = lax.broadcasted_iota(jnp.int32, (tq, tq), 0)
        si = lax.broadcasted_iota(jnp.int32, (tq, tq), 1)
        tri_incl = (ji <= si).astype(_MXU_DTYPE)
        tri_excl = (ji < si).astype(_MXU_DTYPE)
        dk_acc[...] = jnp.zeros_like(dk_acc)
        dv_acc[...] = jnp.zeros_like(dv_acc)

        def q_loop(qi, carry):
            r0 = pl.multiple_of(qi * tq, tq)
            q_pair = q_ref[pl.ds(r0, tq), :]
            do_pair = do_ref[pl.ds(r0, tq), :]
            tot_pair = tot_ref[pl.ds(r0, tq), :]
            dq_pair = jnp.zeros((tq, LANES), F32)
            for a in range(2):
                qm = jnp.where(hm[a], q_pair, 0)
                dom = jnp.where(hm[a], do_pair, 0)
                total = jnp.max(jnp.where(hm[a], tot_pair, -jnp.inf), axis=1, keepdims=True)

                def k_loop(kj, c, qm=qm, dom=dom, total=total, a=a):
                    dq, cp, cq = c
                    c0 = pl.multiple_of(kj * tq, tq)
                    kk = k_ref[pl.ds(c0, tq), :]
                    z = _dot_nt(qm, kk) * ATT_SCALE
                    mask = (c0 + si) < (r0 + ji)
                    lb = _log_sigmoid(z)
                    sig = jnp.exp(lb)
                    l1m = jnp.where(mask, lb - z, 0.0)
                    upto = _tri_cumsum(l1m, tri_incl) + cp
                    att = jnp.where(mask, jnp.exp(lb + (total - upto)), 0.0)
                    d_att = _dot_nt(dom, v_ref[pl.ds(c0, tq), :])
                    d_log = d_att * att
                    before = _tri_cumsum(d_log, tri_excl) + cq
                    dz = jnp.where(mask, d_log * (1.0 - sig) - sig * before, 0.0) * ATT_SCALE
                    dzb = dz.astype(_MXU_DTYPE)
                    dq = dq + _dot(dzb, jnp.where(hm[a], kk, 0))
                    dk_acc[pl.ds(c0, tq), :] += _dot_tn(dzb, qm)
                    dv_acc[pl.ds(c0, tq), :] += _dot_tn(att.astype(_MXU_DTYPE), dom)
                    cp = cp + jnp.sum(l1m, axis=1, keepdims=True)
                    cq = cq + jnp.sum(d_log, axis=1, keepdims=True)
                    return dq, cp, cq

                zero_col = jnp.zeros((tq, 1), F32)
                dq_a, _, _ = lax.fori_loop(0, qi + 1, k_loop, (jnp.zeros((tq, LANES), F32), zero_col, zero_col))
                dq_pair = dq_pair + dq_a
            dq_ref[pl.ds(r0, tq), :] = dq_pair.astype(dq_ref.dtype)
            return carry

        lax.fori_loop(0, nq, q_loop, 0)
        dk_ref[...] = dk_acc[...].astype(dk_ref.dtype)
        dv_ref[...] = dv_acc[...].astype(dv_ref.dtype)

    pair = pl.BlockSpec((S, LANES), lambda b, p: (b, p))
    out = jax.ShapeDtypeStruct((T, SB_W), _MXU_DTYPE)
    return pl.pallas_call(
        body, name=name, grid=(B, SB_W // LANES), out_shape=(out, out, out),
        in_specs=_sb_specs(B, S) + [pair, pair], out_specs=(pair, pair, pair),
        scratch_shapes=[pltpu.VMEM((S, LANES), F32), pltpu.VMEM((S, LANES), F32)],
        compiler_params=_params("parallel", "parallel"),
    )(proj, proj, proj, dyb, tot)


def _ffn_fwd(x, g, wg_t, wu_t, wd, tag):
    h = _rms_fwd(x, g, tag + "_rms")
    G, U, A = _ffn_gu(h, wg_t, wu_t, tag + "_gate_up")
    y = _mm_nn([(A, wd)], x, 0.5, F32, tag + "_down")
    return y, (h, G, U, A)


def _ffn_bwd(dx, dxb, x, g, wg_t, wu_t, wd, saved, tag):
    h, G, U, A = saved
    dG, dU = _ffn_bwd_act(dxb, wd, G, U, tag + "_bwd_act")
    d_wd = _mm_tn(A, dxb, 0.5, _WIRE_DTYPE, tag + "_dw_down")
    d_wg = _mm_tn(dG, h, 1.0, _WIRE_DTYPE, tag + "_dw_gate")
    d_wu = _mm_tn(dU, h, 1.0, _WIRE_DTYPE, tag + "_dw_up")
    dh = _mm_nn([(dG, wg_t), (dU, wu_t)], None, 1.0, F32, tag + "_dh")
    dx_in, dxb_in, dg = _rms_bwd(dh, x, g, dx, tag + "_rms_bwd")
    return dx_in, dxb_in, dg, d_wg, d_wu, d_wd


def _local_step(x, tgt, B, S, small, big):
    x1, ffn1 = _ffn_fwd(x, small["ffn1_norm"], big["ffn1_gate"], big["ffn1_up"], big["ffn1_down"], "ffn1")
    h2 = _rms_fwd(x1, small["mix_norm"], "mix_rms")
    proj = _mm_nt(h2, big["w_in"], _MXU_DTYPE, "in_proj")
    ya = _swa_fwd(proj, small["swa_sinks"], B, S, "swa_fwd")
    yb, tot = _sb_fwd(proj, B, S, "sb_fwd")
    yn = _outnorm_fwd(ya, yb, small["swa_out_norm"], small["sb_out_norm"], "out_norm")
    x2 = _mm_nn([(yn, big["w_out"])], x1, 1.0, F32, "out_proj")
    x3, ffn2 = _ffn_fwd(x2, small["ffn2_norm"], big["ffn2_gate"], big["ffn2_up"], big["ffn2_down"], "ffn2")

    dx3, dx3b, d_final, loss = _loss_head(x3, small["final_norm"], tgt, "loss_head")

    dx2, dx2b, d_g2, d_wg2, d_wu2, d_wd2 = _ffn_bwd(
        dx3, dx3b, x2, small["ffn2_norm"], big["ffn2_gate"], big["ffn2_up"], big["ffn2_down"], ffn2, "ffn2")

    d_wout = _mm_tn(yn, dx2b, 1.0, _WIRE_DTYPE, "dw_out")
    dyn = _mm_nt(dx2b, big["w_out"], F32, "out_proj_bwd")
    dya, dyb, d_ga, d_gb = _outnorm_bwd(dyn, ya, yb, small["swa_out_norm"], small["sb_out_norm"], "out_norm_bwd")
    dqa, dka, dva, d_sinks = _swa_bwd(proj, small["swa_sinks"], dya, B, S, "swa_bwd")
    dqb, dkb, dvb = _sb_bwd(proj, tot, dyb, B, S, "sb_bwd")
    dproj = jnp.concatenate([dqa, dka, dva, dqb, dkb, dvb], axis=1)
    d_win = _mm_tn(dproj, h2, 1.0, _WIRE_DTYPE, "dw_in")
    dh2 = _mm_nn([(dproj, big["w_in"])], None, 1.0, F32, "in_proj_bwd")
    dx1, dx1b, d_gm = _rms_bwd(dh2, x1, small["mix_norm"], dx2, "mix_rms_bwd")

    gx, _, d_g1, d_wg1, d_wu1, d_wd1 = _ffn_bwd(
        dx1, dx1b, x, small["ffn1_norm"], big["ffn1_gate"], big["ffn1_up"], big["ffn1_down"], ffn1, "ffn1")

    d_small = {"ffn1_norm": d_g1, "mix_norm": d_gm, "swa_sinks": d_sinks[:, :N_SWA_HEADS], "swa_out_norm": d_ga,
               "sb_out_norm": d_gb, "ffn2_norm": d_g2, "final_norm": d_final}
    d_big = {"ffn1_gate": d_wg1, "ffn1_up": d_wu1, "ffn1_down": d_wd1, "w_in": d_win, "w_out": d_wout,
             "ffn2_gate": d_wg2, "ffn2_up": d_wu2, "ffn2_down": d_wd2}
    return loss, gx, d_small, d_big


MESH = pl.DeviceIdType.MESH
BIG_NAMES = ("ffn1_gate", "ffn1_up", "ffn1_down", "w_in", "w_out", "ffn2_gate", "ffn2_up", "ffn2_down")
_ANY = pl.BlockSpec(memory_space=pl.ANY)
_COMM_PARAMS = pltpu.CompilerParams(has_side_effects=True)


def _place():
    x, y, c = lax.axis_index("x"), lax.axis_index("y"), lax.axis_index("c")
    other_chips = [(1 - x, y), (x, 1 - y), (1 - x, 1 - y)]
    return x, y, c, other_chips


def _padded_rows(rows):
    full = N_DEV * rows
    return -(-full // _F_TILE) * _F_TILE


def _all_gather(shards):
    nw = len(shards)
    D = shards[0].shape[1]
    rows_w = [s.shape[0] for s in shards]
    full_w = [_padded_rows(r) for r in rows_w]
    pad_w = [f - N_DEV * r for f, r in zip(full_w, rows_w)]
    max_pad = max(max(pad_w), 16)

    def body(*refs):
        ins, outs = refs[:nw], refs[nw:2 * nw]
        zbuf, send_sems, recv_sems, local_sems, zero_sems = refs[2 * nw:]
        x, y, c, chips = _place()
        me, sibling = (x, y, c), (x, y, 1 - c)

        def rows(w, px, py, pc):
            start = pl.multiple_of((4 * px + 2 * py + pc) * rows_w[w], 16)
            return outs[w].at[pl.ds(start, rows_w[w]), :]

        def copy(w, k, block, to, src=None):
            return pltpu.make_async_remote_copy(
                src_ref=rows(w, *block) if src is None else src, dst_ref=rows(w, *block),
                send_sem=send_sems.at[w, k], recv_sem=recv_sems.at[w, k], device_id=to, device_id_type=MESH)

        zbuf[...] = jnp.zeros_like(zbuf)
        zero_copies = [pltpu.make_async_copy(zbuf.at[pl.ds(0, pad_w[w]), :],
                                             outs[w].at[pl.ds(N_DEV * rows_w[w], pad_w[w]), :], zero_sems.at[w])
                       for w in range(nw) if pad_w[w]]
        for cp in zero_copies:
            cp.start()
        local_copies, sends = list(zero_copies), []
        for w in range(nw):
            mine = pltpu.make_async_copy(ins[w], rows(w, *me), local_sems.at[w])
            mine.start()
            local_copies.append(mine)
            first = [copy(w, 0, me, sibling, src=ins[w])]
            first += [copy(w, 1 + j, me, (*chip, c), src=ins[w]) for j, chip in enumerate(chips)]
            for cp in first:
                cp.start()
            sends += first
        for w in range(nw):
            for j, chip in enumerate(chips):
                copy(w, 1 + j, (*chip, c), me).wait_recv()
                passed = copy(w, 4 + j, (*chip, c), sibling)
                passed.start()
                sends.append(passed)
        for w in range(nw):
            copy(w, 0, sibling, me).wait_recv()
            for j, chip in enumerate(chips):
                copy(w, 4 + j, (*chip, 1 - c), me).wait_recv()
        for cp in sends:
            cp.wait_send()
        for cp in local_copies:
            cp.wait()

    return pl.pallas_call(
        body, name="all_gather_weights",
        out_shape=tuple(jax.ShapeDtypeStruct((f, D), s.dtype) for f, s in zip(full_w, shards)),
        in_specs=[_ANY] * nw, out_specs=tuple([_ANY] * nw),
        scratch_shapes=[pltpu.VMEM((max_pad, D), shards[0].dtype), pltpu.SemaphoreType.DMA((nw, 7)),
                        pltpu.SemaphoreType.DMA((nw, 7)), pltpu.SemaphoreType.DMA((nw,)),
                        pltpu.SemaphoreType.DMA((nw,))],
        compiler_params=_COMM_PARAMS,
    )(*shards)


def _rs_pair(partials, rows_w):
    nw = len(partials)
    D = partials[0].shape[1]

    def body(*refs):
        ins, outs = refs[:nw], refs[nw:2 * nw]
        send_sems, recv_sems = refs[2 * nw:]
        x, y, c, _ = _place()
        sibling = (x, y, 1 - c)
        copies = []
        for w in range(nw):
            r = rows_w[w]
            for q in range(4):
                src = ins[w].at[pl.ds(pl.multiple_of((2 * q + 1 - c) * r, 16), r), :]
                dst = outs[w].at[pl.ds(q * r, r), :]
                cp = pltpu.make_async_remote_copy(src_ref=src, dst_ref=dst, send_sem=send_sems.at[w, q],
                                                  recv_sem=recv_sems.at[w, q], device_id=sibling, device_id_type=MESH)
                cp.start()
                copies.append(cp)
        for cp in copies:
            cp.wait()

    return pl.pallas_call(
        body, name="reduce_scatter_pair",
        out_shape=tuple(jax.ShapeDtypeStruct((4 * r, D), p.dtype) for r, p in zip(rows_w, partials)),
        in_specs=[_ANY] * nw, out_specs=tuple([_ANY] * nw),
        scratch_shapes=[pltpu.SemaphoreType.DMA((nw, 4)), pltpu.SemaphoreType.DMA((nw, 4))],
        compiler_params=_COMM_PARAMS,
    )(*partials)


def _pair_sum(partial, from_sibling, rows, core, name):
    D = partial.shape[1]

    def body(core_ref, p_ref, s_ref, o_ref):
        o_ref[...] = (p_ref[...].astype(F32) + s_ref[...].astype(F32)).astype(o_ref.dtype)

    grid_spec = pltpu.PrefetchScalarGridSpec(
        num_scalar_prefetch=1, grid=(4,),
        in_specs=[pl.BlockSpec((rows, D), lambda q, core_ref: (2 * q + core_ref[0], 0)),
                  pl.BlockSpec((rows, D), lambda q, core_ref: (q, 0))],
        out_specs=pl.BlockSpec((rows, D), lambda q, core_ref: (q, 0)))
    return pl.pallas_call(
        body, name=name, grid_spec=grid_spec, out_shape=jax.ShapeDtypeStruct((4 * rows, D), partial.dtype),
        compiler_params=_params("arbitrary"),
    )(core, partial, from_sibling)


def _rs_chips(chip_sums, rows_w):
    nw = len(chip_sums)
    D = chip_sums[0].shape[1]

    def body(*refs):
        ins, outs = refs[:nw], refs[nw:2 * nw]
        send_sems, recv_sems, local_sems = refs[2 * nw:]
        x, y, c, chips = _place()
        my_chip = 2 * x + y
        copies = []
        for w in range(nw):
            r = rows_w[w]
            mine = pl.ds(pl.multiple_of(my_chip * r, 16), r)
            own = pltpu.make_async_copy(ins[w].at[mine, :], outs[w].at[mine, :], local_sems.at[w])
            own.start()
            copies.append(own)
            for j, (qx, qy) in enumerate(chips):
                src = ins[w].at[pl.ds(pl.multiple_of((2 * qx + qy) * r, 16), r), :]
                cp = pltpu.make_async_remote_copy(src_ref=src, dst_ref=outs[w].at[mine, :], send_sem=send_sems.at[w, j],
                                                  recv_sem=recv_sems.at[w, j], device_id=(qx, qy, c), device_id_type=MESH)
                cp.start()
                copies.append(cp)
        for cp in copies:
            cp.wait()

    return pl.pallas_call(
        body, name="reduce_scatter_chips",
        out_shape=tuple(jax.ShapeDtypeStruct(s.shape, s.dtype) for s in chip_sums),
        in_specs=[_ANY] * nw, out_specs=tuple([_ANY] * nw),
        scratch_shapes=[pltpu.SemaphoreType.DMA((nw, 3)), pltpu.SemaphoreType.DMA((nw, 3)),
                        pltpu.SemaphoreType.DMA((nw,))],
        compiler_params=_COMM_PARAMS,
    )(*chip_sums)


def _chip_sum(slots, rows, name):
    D = slots.shape[1]

    def body(s_ref, o_ref):
        acc = s_ref[0:rows, :].astype(F32)
        for q in range(1, 4):
            acc = acc + s_ref[q * rows:(q + 1) * rows, :].astype(F32)
        o_ref[...] = acc

    tc = _tile(D, 512, LANES)
    return pl.pallas_call(
        body, name=name, grid=(D // tc,), out_shape=jax.ShapeDtypeStruct((rows, D), F32),
        in_specs=[pl.BlockSpec((4 * rows, tc), lambda j: (0, j))], out_specs=pl.BlockSpec((rows, tc), lambda j: (0, j)),
        compiler_params=_params("parallel"),
    )(slots)


SMALL_ROWS = 88


def _small_allreduce(vec):
    def body(v_ref, o_ref, gather, send_sems, recv_sems):
        x, y, c, _ = _place()
        my_id = 4 * x + 2 * y + c
        gather[my_id] = v_ref[...]
        copies = []
        for r in range(1, N_DEV):
            peer = (x ^ (r >> 2), y ^ ((r >> 1) & 1), c ^ (r & 1))
            cp = pltpu.make_async_remote_copy(src_ref=v_ref, dst_ref=gather.at[my_id], send_sem=send_sems.at[r - 1],
                                              recv_sem=recv_sems.at[r - 1], device_id=peer, device_id_type=MESH)
            cp.start()
            copies.append(cp)
        for cp in copies:
            cp.wait()
        acc = gather[0]
        for d in range(1, N_DEV):
            acc = acc + gather[d]
        o_ref[...] = acc

    vm = pl.BlockSpec(memory_space=pltpu.VMEM)
    return pl.pallas_call(
        body, name="small_allreduce", out_shape=jax.ShapeDtypeStruct(vec.shape, F32),
        in_specs=[vm], out_specs=vm,
        scratch_shapes=[pltpu.VMEM((N_DEV,) + vec.shape, F32), pltpu.SemaphoreType.DMA((N_DEV - 1,)),
                        pltpu.SemaphoreType.DMA((N_DEV - 1,))],
        compiler_params=_COMM_PARAMS,
    )(vec)


def _adamw(w, g, m, v, name):
    R, C = w.shape
    tr = _tile(R, 256, 8)

    def body(w_ref, g_ref, m_ref, v_ref, d_ref, nm_ref, nv_ref):
        gv = g_ref[...]
        nm = ADAM_B1 * m_ref[...] + (1.0 - ADAM_B1) * gv
        nv = ADAM_B2 * v_ref[...] + (1.0 - ADAM_B2) * jnp.square(gv)
        m_hat = nm / (1.0 - ADAM_B1 ** ADAM_STEP)
        v_hat = nv / (1.0 - ADAM_B2 ** ADAM_STEP)
        d_ref[...] = -ADAM_LR * (m_hat / (jnp.sqrt(v_hat) + ADAM_EPS) + ADAM_WD * w_ref[...])
        nm_ref[...] = nm
        nv_ref[...] = nv

    spec = pl.BlockSpec((tr, C), lambda i: (i, 0))
    out = jax.ShapeDtypeStruct((R, C), F32)
    return pl.pallas_call(
        body, name=name, grid=(R // tr,), out_shape=(out, out, out),
        in_specs=[spec] * 4, out_specs=(spec, spec, spec),
        compiler_params=_params("parallel"),
    )(w, g, m, v)


WEIGHT_NAMES = ("ffn1_norm", "ffn1_w_gate", "ffn1_w_up", "ffn1_w_down", "mix_norm", "w_in", "swa_sinks",
                "swa_out_norm", "sb_out_norm", "w_out", "ffn2_norm", "ffn2_w_gate", "ffn2_w_up", "ffn2_w_down",
                "final_norm")
SMALL_NAMES = ("ffn1_norm", "mix_norm", "swa_sinks", "swa_out_norm", "sb_out_norm", "ffn2_norm", "final_norm")
BIG_ARGS = {"ffn1_gate": ("ffn1_w_gate", True), "ffn1_up": ("ffn1_w_up", True), "ffn1_down": ("ffn1_w_down", False),
            "w_in": ("w_in", True), "w_out": ("w_out", False), "ffn2_gate": ("ffn2_w_gate", True),
            "ffn2_up": ("ffn2_w_up", True), "ffn2_down": ("ffn2_w_down", False)}


def _pack_small(parts):
    padded = [jnp.pad(p.reshape(1, -1), ((0, 0), (0, -p.size % LANES))) for p in parts]
    flat = jnp.concatenate(padded, axis=1)
    flat = jnp.pad(flat, ((0, 0), (0, SMALL_ROWS * LANES - flat.shape[1])))
    return flat.reshape(SMALL_ROWS, LANES)


def _unpack_small(block, shapes):
    flat = block.reshape(-1)
    out, off = [], 0
    for shp in shapes:
        n = 1
        for s in shp:
            n *= s
        out.append(flat[off:off + n].reshape(shp))
        off += n + (-n % LANES)
    return out


def kernel(x, ffn1_norm, ffn1_w_gate, ffn1_w_up, ffn1_w_down, mix_norm, w_in, swa_sinks, swa_out_norm, sb_out_norm, w_out, ffn2_norm, ffn2_w_gate, ffn2_w_up, ffn2_w_down, final_norm, loss_target, m_ffn1_norm, m_ffn1_w_gate, m_ffn1_w_up, m_ffn1_w_down, m_mix_norm, m_w_in, m_swa_sinks, m_swa_out_norm, m_sb_out_norm, m_w_out, m_ffn2_norm, m_ffn2_w_gate, m_ffn2_w_up, m_ffn2_w_down, m_final_norm, v_ffn1_norm, v_ffn1_w_gate, v_ffn1_w_up, v_ffn1_w_down, v_mix_norm, v_w_in, v_swa_sinks, v_swa_out_norm, v_sb_out_norm, v_w_out, v_ffn2_norm, v_ffn2_w_gate, v_ffn2_w_up, v_ffn2_w_down, v_final_norm):
    args = dict(locals())
    B, S, D = x.shape
    T = B * S
    weights = {n: args[n] for n in WEIGHT_NAMES}
    mom_m = {n: args["m_" + n] for n in WEIGHT_NAMES}
    mom_v = {n: args["v_" + n] for n in WEIGHT_NAMES}

    shards = []
    for name in BIG_NAMES:
        arg, transposed = BIG_ARGS[name]
        w2 = weights[arg][0]
        shards.append((w2.T if transposed else w2).astype(_WIRE_DTYPE))
    rows_w = [s.shape[0] for s in shards]
    big = dict(zip(BIG_NAMES, _all_gather(shards)))
    small = {n: weights[n].reshape(1, -1) for n in SMALL_NAMES}

    loss, gx, d_small, d_big = _local_step(x.reshape(T, D), loss_target.reshape(T, D), B, S, small, big)

    partials = [d_big[n] for n in BIG_NAMES]
    from_sibling = _rs_pair(partials, rows_w)
    core = lax.axis_index("c").astype(jnp.int32).reshape(1)
    chip_sums = [_pair_sum(p, s, r, core, "pair_sum_" + n)
                 for p, s, r, n in zip(partials, from_sibling, rows_w, BIG_NAMES)]
    slots = _rs_chips(chip_sums, rows_w)
    g_big = {n: _chip_sum(s, r, "chip_sum_" + n) for n, s, r in zip(BIG_NAMES, slots, rows_w)}

    small_shapes = [(1, 1)] + [d_small[n].shape for n in SMALL_NAMES]
    reduced = _small_allreduce(_pack_small([loss[:, :1]] + [d_small[n] for n in SMALL_NAMES]))
    red = _unpack_small(reduced, small_shapes)
    loss_out = red[0].reshape(())
    g_small = dict(zip(SMALL_NAMES, red[1:]))

    grads, deltas, new_m, new_v = {}, {}, {}, {}
    for name in BIG_NAMES:
        arg, transposed = BIG_ARGS[name]
        g2 = g_big[name].T if transposed else g_big[name]
        d2, m2, v2 = _adamw(weights[arg][0], g2, mom_m[arg][0], mom_v[arg][0], "adamw_" + name)
        grads[arg], deltas[arg], new_m[arg], new_v[arg] = g2[None], d2[None], m2[None], v2[None]
    shapes1 = [(1, weights[n].size) for n in SMALL_NAMES]
    packed = [_pack_small([t[n].reshape(1, -1) for n in SMALL_NAMES]) for t in (weights, g_small, mom_m, mom_v)]
    upd = _adamw(*packed, "adamw_small")
    for tgt_dict, block in zip((deltas, new_m, new_v), upd):
        for n, val in zip(SMALL_NAMES, _unpack_small(block, shapes1)):
            tgt_dict[n] = val.reshape(weights[n].shape)
    for n in SMALL_NAMES:
        grads[n] = g_small[n].reshape(weights[n].shape)

    return (loss_out, gx.reshape(B, S, D), *[grads[n] for n in WEIGHT_NAMES], *[deltas[n] for n in WEIGHT_NAMES],
            *[new_m[n] for n in WEIGHT_NAMES], *[new_v[n] for n in WEIGHT_NAMES])
```

```python
import functools

import jax
import jax.numpy as jnp
from jax import lax
from jax.experimental import pallas as pl
from jax.experimental.pallas import tpu as pltpu

F32 = jnp.float32
_MXU_DTYPE = jnp.bfloat16
_WIRE_DTYPE = jnp.bfloat16

EPS = 1e-6
HEAD_DIM = 64
N_SWA_HEADS = 16
N_SWA_KV = 4
N_SB_HEADS = 16
WINDOW = 128
SWA_Q = N_SWA_HEADS * HEAD_DIM
SWA_KV = N_SWA_KV * HEAD_DIM
SB_W = N_SB_HEADS * HEAD_DIM
IN_W = SWA_Q + 2 * SWA_KV + 3 * SB_W
LANES = 128
ATT_SCALE = HEAD_DIM ** -0.5

ADAM_LR = 0.001
ADAM_B1 = 0.9
ADAM_B2 = 0.999
ADAM_EPS = 1e-08
ADAM_WD = 0.01
ADAM_STEP = 10

N_DEV = 8
_VMEM_LIMIT_BYTES = 48 * 1024 * 1024
_F_TILE = 512


def _params(*semantics):
    return pltpu.CompilerParams(dimension_semantics=semantics, vmem_limit_bytes=_VMEM_LIMIT_BYTES)


def _tile(n, pref, align):
    t = min(n, pref)
    t -= t % align
    while t >= align:
        if n % t == 0:
            return t
        t -= align
    return n


def _dot(a, b):
    return lax.dot_general(a, b, (((1,), (0,)), ((), ())), preferred_element_type=F32)


def _dot_nt(a, b):
    return lax.dot_general(a, b, (((1,), (1,)), ((), ())), preferred_element_type=F32)


def _dot_tn(a, b):
    return lax.dot_general(a, b, (((0,), (0,)), ((), ())), preferred_element_type=F32)


def _rms_fwd(x, g, name):
    T, D = x.shape
    tm = _tile(T, 512, 16)

    def body(x_ref, g_ref, o_ref):
        xv = x_ref[...]
        r = lax.rsqrt(jnp.mean(xv * xv, axis=-1, keepdims=True) + EPS)
        o_ref[...] = (xv * r * g_ref[...]).astype(o_ref.dtype)

    return pl.pallas_call(
        body, name=name, grid=(T // tm,),
        out_shape=jax.ShapeDtypeStruct((T, D), _MXU_DTYPE),
        in_specs=[pl.BlockSpec((tm, D), lambda i: (i, 0)), pl.BlockSpec((1, D), lambda i: (0, 0))],
        out_specs=pl.BlockSpec((tm, D), lambda i: (i, 0)),
        compiler_params=_params("parallel"),
    )(x, g)


def _rms_bwd_rows(dh, xv, g):
    r = lax.rsqrt(jnp.mean(xv * xv, axis=-1, keepdims=True) + EPS)
    xhat = xv * r
    u = dh * g
    dx = r * (u - xhat * jnp.mean(u * xhat, axis=-1, keepdims=True))
    return dx, dh * xhat


def _rms_bwd(dh, x, g, dres, name):
    T, D = x.shape
    tm = _tile(T, 256, 16)

    def body(dh_ref, x_ref, g_ref, dres_ref, dx_ref, dxb_ref, dg_ref):
        @pl.when(pl.program_id(0) == 0)
        def _():
            dg_ref[...] = jnp.zeros_like(dg_ref)

        dx, dgr = _rms_bwd_rows(dh_ref[...], x_ref[...], g_ref[...])
        dx = dres_ref[...] + dx
        dx_ref[...] = dx
        dxb_ref[...] = dx.astype(dxb_ref.dtype)
        dg_ref[...] += jnp.sum(dgr, axis=0, keepdims=True)

    row = pl.BlockSpec((tm, D), lambda i: (i, 0))
    vec = pl.BlockSpec((1, D), lambda i: (0, 0))
    return pl.pallas_call(
        body, name=name, grid=(T // tm,),
        out_shape=(jax.ShapeDtypeStruct((T, D), F32), jax.ShapeDtypeStruct((T, D), _MXU_DTYPE),
                   jax.ShapeDtypeStruct((1, D), F32)),
        in_specs=[row, row, vec, row], out_specs=(row, row, vec),
        compiler_params=_params("arbitrary"),
    )(dh, x, g, dres)


def _loss_head(x, g, tgt, name):
    T, D = x.shape
    tm = _tile(T, 256, 16)

    def body(x_ref, g_ref, t_ref, dx_ref, dxb_ref, dg_ref, loss_ref):
        @pl.when(pl.program_id(0) == 0)
        def _():
            dg_ref[...] = jnp.zeros_like(dg_ref)
            loss_ref[...] = jnp.zeros_like(loss_ref)

        xv = x_ref[...]
        gv = g_ref[...]
        r = lax.rsqrt(jnp.mean(xv * xv, axis=-1, keepdims=True) + EPS)
        xhat = xv * r
        diff = xhat * gv - t_ref[...]
        tok = jnp.mean(diff * diff, axis=-1, keepdims=True)
        loss_ref[...] += 0.5 * jnp.sum(tok, axis=0, keepdims=True)
        dy = diff / D
        u = dy * gv
        dx = r * (u - xhat * jnp.mean(u * xhat, axis=-1, keepdims=True))
        dx_ref[...] = dx
        dxb_ref[...] = dx.astype(dxb_ref.dtype)
        dg_ref[...] += jnp.sum(dy * xhat, axis=0, keepdims=True)

    row = pl.BlockSpec((tm, D), lambda i: (i, 0))
    vec = pl.BlockSpec((1, D), lambda i: (0, 0))
    return pl.pallas_call(
        body, name=name, grid=(T // tm,),
        out_shape=(jax.ShapeDtypeStruct((T, D), F32), jax.ShapeDtypeStruct((T, D), _MXU_DTYPE),
                   jax.ShapeDtypeStruct((1, D), F32), jax.ShapeDtypeStruct((1, LANES), F32)),
        in_specs=[row, vec, row],
        out_specs=(row, row, vec, pl.BlockSpec((1, LANES), lambda i: (0, 0))),
        compiler_params=_params("arbitrary"),
    )(x, g, tgt)


def _outnorm_fwd(ya, yb, ga, gb, name):
    T, W = ya.shape
    tm = _tile(T, 512, 16)

    def body(ya_ref, yb_ref, ga_ref, gb_ref, o_ref):
        for k, (y_ref, g_ref) in enumerate(((ya_ref, ga_ref), (yb_ref, gb_ref))):
            yv = y_ref[...]
            r = lax.rsqrt(jnp.mean(yv * yv, axis=-1, keepdims=True) + EPS)
            o_ref[:, k * W:(k + 1) * W] = (yv * r * g_ref[...]).astype(o_ref.dtype)

    row = pl.BlockSpec((tm, W), lambda i: (i, 0))
    vec = pl.BlockSpec((1, W), lambda i: (0, 0))
    return pl.pallas_call(
        body, name=name, grid=(T // tm,),
        out_shape=jax.ShapeDtypeStruct((T, 2 * W), _MXU_DTYPE),
        in_specs=[row, row, vec, vec], out_specs=pl.BlockSpec((tm, 2 * W), lambda i: (i, 0)),
        compiler_params=_params("parallel"),
    )(ya, yb, ga, gb)


def _outnorm_bwd(dyn, ya, yb, ga, gb, name):
    T, W = ya.shape
    tm = _tile(T, 256, 16)

    def body(d_ref, ya_ref, yb_ref, ga_ref, gb_ref, dya_ref, dyb_ref, dga_ref, dgb_ref):
        @pl.when(pl.program_id(0) == 0)
        def _():
            dga_ref[...] = jnp.zeros_like(dga_ref)
            dgb_ref[...] = jnp.zeros_like(dgb_ref)

        for k, (y_ref, g_ref, dy_ref, dg_ref) in enumerate(
                ((ya_ref, ga_ref, dya_ref, dga_ref), (yb_ref, gb_ref, dyb_ref, dgb_ref))):
            dy, dgr = _rms_bwd_rows(d_ref[:, k * W:(k + 1) * W], y_ref[...], g_ref[...])
            dy_ref[...] = dy.astype(dy_ref.dtype)
            dg_ref[...] += jnp.sum(dgr, axis=0, keepdims=True)

    row = pl.BlockSpec((tm, W), lambda i: (i, 0))
    vec = pl.BlockSpec((1, W), lambda i: (0, 0))
    return pl.pallas_call(
        body, name=name, grid=(T // tm,),
        out_shape=(jax.ShapeDtypeStruct((T, W), _MXU_DTYPE), jax.ShapeDtypeStruct((T, W), _MXU_DTYPE),
                   jax.ShapeDtypeStruct((1, W), F32), jax.ShapeDtypeStruct((1, W), F32)),
        in_specs=[pl.BlockSpec((tm, 2 * W), lambda i: (i, 0)), row, row, vec, vec],
        out_specs=(row, row, vec, vec),
        compiler_params=_params("arbitrary"),
    )(dyn, ya, yb, ga, gb)


def _ffn_gu(h, wg_t, wu_t, name):
    T, D = h.shape
    Fp = wg_t.shape[0]
    tm = _tile(T, 1024, 16)
    tn = _tile(Fp, _F_TILE, LANES)

    def body(h_ref, wg_ref, wu_ref, g_ref, u_ref, a_ref):
        hv = h_ref[...]
        g = _dot_nt(hv, wg_ref[...])
        u = _dot_nt(hv, wu_ref[...])
        g_ref[...] = g.astype(g_ref.dtype)
        u_ref[...] = u.astype(u_ref.dtype)
        a_ref[...] = (g * jax.nn.sigmoid(g) * u).astype(a_ref.dtype)

    act = pl.BlockSpec((tm, tn), lambda n, m: (m, n))
    wsp = pl.BlockSpec((tn, D), lambda n, m: (n, 0))
    out = jax.ShapeDtypeStruct((T, Fp), _MXU_DTYPE)
    return pl.pallas_call(
        body, name=name, grid=(Fp // tn, T // tm), out_shape=(out, out, out),
        in_specs=[pl.BlockSpec((tm, D), lambda n, m: (m, 0)), wsp, wsp],
        out_specs=(act, act, act),
        compiler_params=_params("parallel", "parallel"),
    )(h, wg_t, wu_t)


def _ffn_bwd_act(dxb, wd, G, U, name):
    T, D = dxb.shape
    Fp = wd.shape[0]
    tm = _tile(T, 1024, 16)
    tn = _tile(Fp, _F_TILE, LANES)

    def body(e_ref, wd_ref, g_ref, u_ref, dg_ref, du_ref):
        da = 0.5 * _dot_nt(e_ref[...], wd_ref[...])
        g = g_ref[...].astype(F32)
        u = u_ref[...].astype(F32)
        s = jax.nn.sigmoid(g)
        du_ref[...] = (da * (g * s)).astype(du_ref.dtype)
        dg_ref[...] = (da * u * (s * (1.0 + g * (1.0 - s)))).astype(dg_ref.dtype)

    act = pl.BlockSpec((tm, tn), lambda n, m: (m, n))
    out = jax.ShapeDtypeStruct((T, Fp), _MXU_DTYPE)
    return pl.pallas_call(
        body, name=name, grid=(Fp // tn, T // tm), out_shape=(out, out),
        in_specs=[pl.BlockSpec((tm, D), lambda n, m: (m, 0)), pl.BlockSpec((tn, D), lambda n, m: (n, 0)),
                  act, act],
        out_specs=(act, act),
        compiler_params=_params("parallel", "parallel"),
    )(dxb, wd, G, U)


def _mm_nt(a, b, out_dtype, name):
    M, K = a.shape
    N = b.shape[0]
    tm = _tile(M, 1024, 16)
    tn = _tile(N, 512, LANES)

    def body(a_ref, b_ref, o_ref):
        o_ref[...] = _dot_nt(a_ref[...], b_ref[...]).astype(o_ref.dtype)

    return pl.pallas_call(
        body, name=name, grid=(N // tn, M // tm), out_shape=jax.ShapeDtypeStruct((M, N), out_dtype),
        in_specs=[pl.BlockSpec((tm, K), lambda n, m: (m, 0)), pl.BlockSpec((tn, K), lambda n, m: (n, 0))],
        out_specs=pl.BlockSpec((tm, tn), lambda n, m: (m, n)),
        compiler_params=_params("parallel", "parallel"),
    )(a, b)


def _mm_nn(pairs, res, alpha, out_dtype, name):
    M, K = pairs[0][0].shape
    N = pairs[0][1].shape[1]
    tm = _tile(M, 512, 16)
    tn = _tile(N, 2048, LANES)
    tk = _tile(K, 512, LANES)
    nk = K // tk
    n_pairs = len(pairs)

    def body(*refs):
        ab = refs[:2 * n_pairs]
        res_ref = refs[2 * n_pairs] if res is not None else None
        o_ref, acc_ref = refs[-2], refs[-1]
        k = pl.program_id(2)

        @pl.when(k == 0)
        def _():
            acc_ref[...] = jnp.zeros_like(acc_ref)

        part = _dot(ab[0][...], ab[1][...])
        for i in range(1, n_pairs):
            part = part + _dot(ab[2 * i][...], ab[2 * i + 1][...])
        acc_ref[...] += part

        @pl.when(k == nk - 1)
        def _():
            out = alpha * acc_ref[...]
            if res_ref is not None:
                out = res_ref[...] + out
            o_ref[...] = out.astype(o_ref.dtype)

    in_specs, args = [], []
    for a, b in pairs:
        in_specs += [pl.BlockSpec((tm, tk), lambda m, n, k: (m, k)), pl.BlockSpec((tk, tn), lambda m, n, k: (k, n))]
        args += [a, b]
    if res is not None:
        in_specs.append(pl.BlockSpec((tm, tn), lambda m, n, k: (m, n)))
        args.append(res)
    return pl.pallas_call(
        body, name=name, grid=(M // tm, N // tn, nk), out_shape=jax.ShapeDtypeStruct((M, N), out_dtype),
        in_specs=in_specs, out_specs=pl.BlockSpec((tm, tn), lambda m, n, k: (m, n)),
        scratch_shapes=[pltpu.VMEM((tm, tn), F32)],
        compiler_params=_params("parallel", "parallel", "arbitrary"),
    )(*args)


def _mm_tn(a, b, alpha, out_dtype, name):
    K, M = a.shape
    N = b.shape[1]
    tm = _tile(M, 512, LANES)
    tn = _tile(N, 2048, LANES)
    tk = _tile(K, 512, 16)
    nk = K // tk

    def body(a_ref, b_ref, o_ref, acc_ref):
        k = pl.program_id(2)

        @pl.when(k == 0)
        def _():
            acc_ref[...] = jnp.zeros_like(acc_ref)

        acc_ref[...] += _dot_tn(a_ref[...], b_ref[...])

        @pl.when(k == nk - 1)
        def _():
            o_ref[...] = (alpha * acc_ref[...]).astype(o_ref.dtype)

    return pl.pallas_call(
        body, name=name, grid=(M // tm, N // tn, nk), out_shape=jax.ShapeDtypeStruct((M, N), out_dtype),
        in_specs=[pl.BlockSpec((tk, tm), lambda m, n, k: (k, m)), pl.BlockSpec((tk, tn), lambda m, n, k: (k, n))],
        out_specs=pl.BlockSpec((tm, tn), lambda m, n, k: (m, n)),
        scratch_shapes=[pltpu.VMEM((tm, tn), F32)],
        compiler_params=_params("parallel", "parallel", "arbitrary"),
    )(a, b)


def _half_masks():
    lane = lax.broadcasted_iota(jnp.int32, (1, LANES), 1)
    return (lane < HEAD_DIM, lane >= HEAD_DIM)


def _swap_halves(v):
    return pltpu.roll(v.astype(F32), HEAD_DIM, 1).astype(v.dtype)


def _swa_geometry(n):
    qi = lax.broadcasted_iota(jnp.int32, (WINDOW, 2 * WINDOW), 0)
    kp = lax.broadcasted_iota(jnp.int32, (WINDOW, 2 * WINDOW), 1)
    dist = (WINDOW + qi) - kp
    valid = (dist >= 0) & (dist < WINDOW) & ((n > 0) | (kp >= WINDOW))
    return dist.astype(F32), valid


def _swa_slope(h):
    return 2.0 ** (-8.0 * (h + 1) / N_SWA_HEADS)


def _swa_probs(qm, kx, sink, slope, distf, valid):
    s = _dot_nt(qm, kx) * ATT_SCALE - slope * distf
    s = jnp.where(valid, s, -1e30)
    m = jnp.maximum(jnp.max(s, axis=1, keepdims=True), sink)
    p = jnp.exp(s - m)
    e_sink = jnp.exp(sink - m)
    den = jnp.sum(p, axis=1, keepdims=True) + e_sink
    return p / den, e_sink / den


def _swa_specs(B, S):
    nb = S // WINDOW
    kcol = SWA_Q // SWA_KV
    cur = lambda b, n: (b * nb + n, kcol)
    prev = lambda b, n: (b * nb + jnp.maximum(n - 1, 0), kcol)
    curv = lambda b, n: (b * nb + n, kcol + 1)
    prevv = lambda b, n: (b * nb + jnp.maximum(n - 1, 0), kcol + 1)
    q_spec = pl.BlockSpec((WINDOW, SWA_Q), lambda b, n: (b * nb + n, 0))
    kv = [pl.BlockSpec((WINDOW, SWA_KV), f) for f in (prev, cur, prevv, curv)]
    sink_spec = pl.BlockSpec(memory_space=pltpu.SMEM)
    return nb, q_spec, kv, sink_spec


def _swa_kv_views(kp_ref, kc_ref, vp_ref, vc_ref, g):
    hm = _half_masks()
    c0 = (g // 2) * LANES
    k_all = jnp.concatenate([kp_ref[:, c0:c0 + LANES], kc_ref[:, c0:c0 + LANES]], axis=0)
    v_all = jnp.concatenate([vp_ref[:, c0:c0 + LANES], vc_ref[:, c0:c0 + LANES]], axis=0)
    b = g % 2
    ks, vs = [None, None], [None, None]
    ks[b], vs[b] = k_all, v_all
    ks[1 - b], vs[1 - b] = _swap_halves(k_all), _swap_halves(v_all)
    ks = [jnp.where(hm[a], ks[a], 0) for a in range(2)]
    vs = [jnp.where(hm[a], vs[a], 0) for a in range(2)]
    return ks, vs


def _swa_fwd(proj, sinks, B, S, name):
    T = B * S
    nb, q_spec, kv_specs, sink_spec = _swa_specs(B, S)

    def body(sink_ref, q_ref, kp_ref, kc_ref, vp_ref, vc_ref, y_ref):
        hm = _half_masks()
        distf, valid = _swa_geometry(pl.program_id(1))
        for g in range(N_SWA_KV):
            ks, vs = _swa_kv_views(kp_ref, kc_ref, vp_ref, vc_ref, g)
            for pp in (2 * g, 2 * g + 1):
                q_pair = q_ref[:, pp * LANES:(pp + 1) * LANES]
                o_pair = jnp.zeros((WINDOW, LANES), F32)
                for a in range(2):
                    h = 2 * pp + a
                    qm = jnp.where(hm[a], q_pair, 0)
                    p, _ = _swa_probs(qm, ks[a], sink_ref[0, h], _swa_slope(h), distf, valid)
                    o_pair = o_pair + _dot(p.astype(_MXU_DTYPE), vs[a])
                y_ref[:, pp * LANES:(pp + 1) * LANES] = o_pair

    return pl.pallas_call(
        body, name=name, grid=(B, nb), out_shape=jax.ShapeDtypeStruct((T, SWA_Q), F32),
        in_specs=[sink_spec, q_spec] + kv_specs,
        out_specs=pl.BlockSpec((WINDOW, SWA_Q), lambda b, n: (b * nb + n, 0)),
        compiler_params=_params("parallel", "parallel"),
    )(sinks, proj, proj, proj, proj, proj)


def _swa_bwd(proj, sinks, dya, B, S, name):
    T = B * S
    nb, q_spec, kv_specs, sink_spec = _swa_specs(B, S)

    def body(sink_ref, q_ref, kp_ref, kc_ref, vp_ref, vc_ref, do_ref,
             dq_ref, dk_ref, dv_ref, dsink_ref, dk_acc, dv_acc):
        b_id, n = pl.program_id(0), pl.program_id(1)
        hm = _half_masks()
        lane = lax.broadcasted_iota(jnp.int32, (1, LANES), 1)

        @pl.when((b_id == 0) & (n == 0))
        def _():
            dsink_ref[...] = jnp.zeros_like(dsink_ref)

        @pl.when(n == 0)
        def _():
            dk_acc[...] = jnp.zeros_like(dk_acc)
            dv_acc[...] = jnp.zeros_like(dv_acc)

        distf, valid = _swa_geometry(n)
        r_prev = pl.multiple_of(jnp.maximum(n - 1, 0) * WINDOW, WINDOW)
        r_cur = pl.multiple_of(n * WINDOW, WINDOW)
        dsink = jnp.zeros((1, LANES), F32)
        for g in range(N_SWA_KV):
            ks, vs = _swa_kv_views(kp_ref, kc_ref, vp_ref, vc_ref, g)
            dk_g = [jnp.zeros((2 * WINDOW, LANES), F32) for _ in range(2)]
            dv_g = [jnp.zeros((2 * WINDOW, LANES), F32) for _ in range(2)]
            for pp in (2 * g, 2 * g + 1):
                q_pair = q_ref[:, pp * LANES:(pp + 1) * LANES]
                do_pair = do_ref[:, pp * LANES:(pp + 1) * LANES]
                dq_pair = jnp.zeros((WINDOW, LANES), F32)
                for a in range(2):
                    h = 2 * pp + a
                    qm = jnp.where(hm[a], q_pair, 0)
                    dom = jnp.where(hm[a], do_pair, 0)
                    p, p_sink = _swa_probs(qm, ks[a], sink_ref[0, h], _swa_slope(h), distf, valid)
                    dp = _dot_nt(dom, vs[a])
                    delta = jnp.sum(p * dp, axis=1, keepdims=True)
                    ds = (p * (dp - delta) * ATT_SCALE).astype(_MXU_DTYPE)
                    dsink = dsink + jnp.where(lane == h, -jnp.sum(p_sink * delta), 0.0)
                    dq_pair = dq_pair + _dot(ds, ks[a])
                    dk_g[a] = dk_g[a] + _dot_tn(ds, qm)
                    dv_g[a] = dv_g[a] + _dot_tn(p.astype(_MXU_DTYPE), dom)
                dq_ref[:, pp * LANES:(pp + 1) * LANES] = dq_pair.astype(dq_ref.dtype)
            bsel = g % 2
            dk_t = dk_g[bsel] + pltpu.roll(dk_g[1 - bsel], HEAD_DIM, 1)
            dv_t = dv_g[bsel] + pltpu.roll(dv_g[1 - bsel], HEAD_DIM, 1)
            c0 = (g // 2) * LANES
            dk_acc[pl.ds(r_prev, WINDOW), c0:c0 + LANES] += dk_t[:WINDOW]
            dk_acc[pl.ds(r_cur, WINDOW), c0:c0 + LANES] += dk_t[WINDOW:]
            dv_acc[pl.ds(r_prev, WINDOW), c0:c0 + LANES] += dv_t[:WINDOW]
            dv_acc[pl.ds(r_cur, WINDOW), c0:c0 + LANES] += dv_t[WINDOW:]
        dsink_ref[...] += dsink

        @pl.when(n == nb - 1)
        def _():
            dk_ref[...] = dk_acc[...].astype(dk_ref.dtype)
            dv_ref[...] = dv_acc[...].astype(dv_ref.dtype)

    seq_kv = pl.BlockSpec((S, SWA_KV), lambda b, n: (b, 0))
    return pl.pallas_call(
        body, name=name, grid=(B, nb),
        out_shape=(jax.ShapeDtypeStruct((T, SWA_Q), _MXU_DTYPE), jax.ShapeDtypeStruct((T, SWA_KV), _MXU_DTYPE),
                   jax.ShapeDtypeStruct((T, SWA_KV), _MXU_DTYPE), jax.ShapeDtypeStruct((1, LANES), F32)),
        in_specs=[sink_spec, q_spec] + kv_specs + [pl.BlockSpec((WINDOW, SWA_Q), lambda b, n: (b * nb + n, 0))],
        out_specs=(pl.BlockSpec((WINDOW, SWA_Q), lambda b, n: (b * nb + n, 0)), seq_kv, seq_kv,
                   pl.BlockSpec((1, LANES), lambda b, n: (0, 0))),
        scratch_shapes=[pltpu.VMEM((S, SWA_KV), F32), pltpu.VMEM((S, SWA_KV), F32)],
        compiler_params=_params("arbitrary", "arbitrary"),
    )(sinks, proj, proj, proj, proj, proj, dya)


SB_TILE = 256
SB_HALF = 128


def _tri2(cond):
    j = lax.broadcasted_iota(jnp.int32, (2 * SB_HALF, SB_HALF), 0) & (SB_HALF - 1)
    s = lax.broadcasted_iota(jnp.int32, (2 * SB_HALF, SB_HALF), 1)
    return cond(j, s).astype(_MXU_DTYPE)


def _half_cumsums(x, tri2):
    out = []
    for h in range(2):
        xh = x[:, h * SB_HALF:(h + 1) * SB_HALF]
        hi = xh.astype(_MXU_DTYPE)
        lo = (xh - hi.astype(F32)).astype(_MXU_DTYPE)
        out.append(_dot(jnp.concatenate([hi, lo], axis=1), tri2))
    return out


def _log_sigmoid(z):
    return jnp.minimum(z, 0.0) - jnp.log(1.0 + jnp.exp(-jnp.abs(z)))


def _sb_specs(B, S):
    qb = (SWA_Q + 2 * SWA_KV) // LANES
    kb = qb + SB_W // LANES
    vb = kb + SB_W // LANES
    return [pl.BlockSpec((S, LANES), functools.partial(lambda b, p, c: (b, c + p), c=c)) for c in (qb, kb, vb)]


def _sb_fwd(proj, B, S, name):
    T = B * S
    tq = SB_TILE
    nq = S // tq

    def body(q_ref, k_ref, v_ref, y_ref, tot_ref):
        hm = _half_masks()
        ji = lax.broadcasted_iota(jnp.int32, (tq, tq), 0)
        si = lax.broadcasted_iota(jnp.int32, (tq, tq), 1)
        tri_after = _tri2(lambda j, s: j > s)
        causal = si < ji

        def q_loop(qi, carry):
            r0 = pl.multiple_of(qi * tq, tq)
            q_pair = q_ref[pl.ds(r0, tq), :] * ATT_SCALE
            qms = [jnp.where(hm[a], q_pair, 0) for a in range(2)]

            def tile(c0, state, diagonal):
                kk = k_ref[pl.ds(c0, tq), :]
                vv = v_ref[pl.ds(c0, tq), :]
                two = range(2)
                z = [_dot_nt(qms[a], kk) for a in two]
                lb = [_log_sigmoid(z[a]) for a in two]
                l1m = [jnp.where(causal, lb[a] - z[a], 0.0) if diagonal else lb[a] - z[a] for a in two]
                cum = [_half_cumsums(l1m[a], tri_after) for a in two]
                tot = [[cum[a][h][:, 0:1] + l1m[a][:, h * SB_HALF:h * SB_HALF + 1] for h in two] for a in two]
                after = [jnp.concatenate([cum[a][0] + (state[a][1] + tot[a][1]), cum[a][1] + state[a][1]], axis=1)
                         for a in two]
                att = [jnp.exp(lb[a] + after[a]) for a in two]
                if diagonal:
                    att = [jnp.where(causal, att[a], 0.0) for a in two]
                acc = [state[a][0] + _dot(att[a].astype(_MXU_DTYPE), jnp.where(hm[a], vv, 0)) for a in two]
                car = [state[a][1] + (tot[a][0] + tot[a][1]) for a in two]
                return tuple((acc[a], car[a]) for a in two)

            zero = (jnp.zeros((tq, LANES), F32), jnp.zeros((tq, 1), F32))
            state = tile(r0, (zero, zero), True)
            state = lax.fori_loop(
                0, qi, lambda it, st: tile(pl.multiple_of((qi - 1 - it) * tq, tq), st, False), state)
            y_ref[pl.ds(r0, tq), :] = state[0][0] + state[1][0]
            tot_ref[pl.ds(r0, tq), :] = jnp.where(hm[0], state[0][1], state[1][1])
            return carry

        lax.fori_loop(0, nq, q_loop, 0)

    out_spec = pl.BlockSpec((S, LANES), lambda b, p: (b, p))
    return pl.pallas_call(
        body, name=name, grid=(B, SB_W // LANES),
        out_shape=(jax.ShapeDtypeStruct((T, SB_W), F32), jax.ShapeDtypeStruct((T, SB_W), F32)),
        in_specs=_sb_specs(B, S), out_specs=(out_spec, out_spec),
        compiler_params=_params("parallel", "parallel"),
    )(proj, proj, proj)


def _sb_bwd(proj, tot, dyb, B, S, name):
    T = B * S
    tq = SB_TILE
    nq = S // tq

    def body(q_ref, k_ref, v_ref, do_ref, tot_ref, dq_ref, dk_ref, dv_ref, dk_acc, dv_acc):
        hm = _half_masks()
        ji = lax.broadcasted_iota(jnp.int32, (tq, tq), 0)
        si = lax.broadcasted_iota(jnp.int32, (tq, tq), 1)
        tri_incl = _tri2(lambda j, s: j <= s)
        tri_excl = _tri2(lambda j, s: j < s)
        causal = si < ji
        dk_acc[...] = jnp.zeros_like(dk_acc)
        dv_acc[...] = jnp.zeros_like(dv_acc)

        def q_loop(qi, carry):
            r0 = pl.multiple_of(qi * tq, tq)
            q_pair = q_ref[pl.ds(r0, tq), :] * ATT_SCALE
            do_pair = do_ref[pl.ds(r0, tq), :]
            tot_pair = tot_ref[pl.ds(r0, tq), :]
            qms = [jnp.where(hm[a], q_pair, 0) for a in range(2)]
            doms = [jnp.where(hm[a], do_pair, 0) for a in range(2)]
            totals = [jnp.max(jnp.where(hm[a], tot_pair, -jnp.inf), axis=1, keepdims=True) for a in range(2)]

            def tile(c0, state, diagonal):
                kk = k_ref[pl.ds(c0, tq), :]
                vv = v_ref[pl.ds(c0, tq), :]
                ks = kk * ATT_SCALE
                two = range(2)
                last = SB_HALF - 1
                z = [_dot_nt(qms[a], kk) for a in two]
                d_att = [_dot_nt(doms[a], vv) for a in two]
                lb = [_log_sigmoid(z[a]) for a in two]
                l1m = [jnp.where(causal, lb[a] - z[a], 0.0) if diagonal else lb[a] - z[a] for a in two]
                cum = [_half_cumsums(l1m[a], tri_incl) for a in two]
                upto = [jnp.concatenate([cum[a][0] + state[a][1],
                                         cum[a][1] + (state[a][1] + cum[a][0][:, last:last + 1])], axis=1) for a in two]
                att = [jnp.exp(lb[a] + (totals[a] - upto[a])) for a in two]
                if diagonal:
                    att = [jnp.where(causal, att[a], 0.0) for a in two]
                d_log = [d_att[a] * att[a] for a in two]
                cumd = [_half_cumsums(d_log[a], tri_excl) for a in two]
                totd = [[cumd[a][h][:, last:last + 1] + d_log[a][:, h * SB_HALF + last:h * SB_HALF + last + 1]
                         for h in two] for a in two]
                before = [jnp.concatenate([cumd[a][0] + state[a][2], cumd[a][1] + (state[a][2] + totd[a][0])], axis=1)
                          for a in two]
                sig = [jnp.exp(lb[a]) for a in two]
                dz = [d_log[a] * (1.0 - sig[a]) - sig[a] * before[a] for a in two]
                if diagonal:
                    dz = [jnp.where(causal, dz[a], 0.0) for a in two]
                dzb = [dz[a].astype(_MXU_DTYPE) for a in two]
                dq = [state[a][0] + _dot(dzb[a], jnp.where(hm[a], ks, 0)) for a in two]
                dk_acc[pl.ds(c0, tq), :] += _dot_tn(dzb[0], qms[0]) + _dot_tn(dzb[1], qms[1])
                dv_acc[pl.ds(c0, tq), :] += (_dot_tn(att[0].astype(_MXU_DTYPE), doms[0])
                                             + _dot_tn(att[1].astype(_MXU_DTYPE), doms[1]))
                cp = [upto[a][:, tq - 1:tq] for a in two]
                cq = [state[a][2] + (totd[a][0] + totd[a][1]) for a in two]
                return tuple((dq[a], cp[a], cq[a]) for a in two)

            zero_col = jnp.zeros((tq, 1), F32)
            zero = (jnp.zeros((tq, LANES), F32), zero_col, zero_col)
            state = lax.fori_loop(0, qi, lambda kj, st: tile(pl.multiple_of(kj * tq, tq), st, False), (zero, zero))
            state = tile(r0, state, True)
            dq_ref[pl.ds(r0, tq), :] = (state[0][0] + state[1][0]).astype(dq_ref.dtype)
            return carry

        lax.fori_loop(0, nq, q_loop, 0)
        dk_ref[...] = dk_acc[...].astype(dk_ref.dtype)
        dv_ref[...] = dv_acc[...].astype(dv_ref.dtype)

    pair = pl.BlockSpec((S, LANES), lambda b, p: (b, p))
    out = jax.ShapeDtypeStruct((T, SB_W), _MXU_DTYPE)
    return pl.pallas_call(
        body, name=name, grid=(B, SB_W // LANES), out_shape=(out, out, out),
        in_specs=_sb_specs(B, S) + [pair, pair], out_specs=(pair, pair, pair),
        scratch_shapes=[pltpu.VMEM((S, LANES), F32), pltpu.VMEM((S, LANES), F32)],
        compiler_params=_params("parallel", "parallel"),
    )(proj, proj, proj, dyb, tot)


def _ffn_fwd(x, g, wg_t, wu_t, wd, tag):
    h = _rms_fwd(x, g, tag + "_rms")
    G, U, A = _ffn_gu(h, wg_t, wu_t, tag + "_gate_up")
    y = _mm_nn([(A, wd)], x, 0.5, F32, tag + "_down")
    return y, (h, G, U, A)


def _ffn_bwd(dx, dxb, x, g, wg_t, wu_t, wd, saved, tag):
    h, G, U, A = saved
    dG, dU = _ffn_bwd_act(dxb, wd, G, U, tag + "_bwd_act")
    d_wd = _mm_tn(A, dxb, 0.5, _WIRE_DTYPE, tag + "_dw_down")
    d_wg = _mm_tn(dG, h, 1.0, _WIRE_DTYPE, tag + "_dw_gate")
    d_wu = _mm_tn(dU, h, 1.0, _WIRE_DTYPE, tag + "_dw_up")
    dh = _mm_nn([(dG, wg_t), (dU, wu_t)], None, 1.0, F32, tag + "_dh")
    dx_in, dxb_in, dg = _rms_bwd(dh, x, g, dx, tag + "_rms_bwd")
    return dx_in, dxb_in, dg, d_wg, d_wu, d_wd


def _local_step(x, tgt, B, S, small, big):
    x1, ffn1 = _ffn_fwd(x, small["ffn1_norm"], big["ffn1_gate"], big["ffn1_up"], big["ffn1_down"], "ffn1")
    h2 = _rms_fwd(x1, small["mix_norm"], "mix_rms")
    proj = _mm_nt(h2, big["w_in"], _MXU_DTYPE, "in_proj")
    ya = _swa_fwd(proj, small["swa_sinks"], B, S, "swa_fwd")
    yb, tot = _sb_fwd(proj, B, S, "sb_fwd")
    yn = _outnorm_fwd(ya, yb, small["swa_out_norm"], small["sb_out_norm"], "out_norm")
    x2 = _mm_nn([(yn, big["w_out"])], x1, 1.0, F32, "out_proj")
    x3, ffn2 = _ffn_fwd(x2, small["ffn2_norm"], big["ffn2_gate"], big["ffn2_up"], big["ffn2_down"], "ffn2")

    dx3, dx3b, d_final, loss = _loss_head(x3, small["final_norm"], tgt, "loss_head")

    dx2, dx2b, d_g2, d_wg2, d_wu2, d_wd2 = _ffn_bwd(
        dx3, dx3b, x2, small["ffn2_norm"], big["ffn2_gate"], big["ffn2_up"], big["ffn2_down"], ffn2, "ffn2")

    d_wout = _mm_tn(yn, dx2b, 1.0, _WIRE_DTYPE, "dw_out")
    dyn = _mm_nt(dx2b, big["w_out"], F32, "out_proj_bwd")
    dya, dyb, d_ga, d_gb = _outnorm_bwd(dyn, ya, yb, small["swa_out_norm"], small["sb_out_norm"], "out_norm_bwd")
    dqa, dka, dva, d_sinks = _swa_bwd(proj, small["swa_sinks"], dya, B, S, "swa_bwd")
    dqb, dkb, dvb = _sb_bwd(proj, tot, dyb, B, S, "sb_bwd")
    dproj = jnp.concatenate([dqa, dka, dva, dqb, dkb, dvb], axis=1)
    d_win = _mm_tn(dproj, h2, 1.0, _WIRE_DTYPE, "dw_in")
    dh2 = _mm_nn([(dproj, big["w_in"])], None, 1.0, F32, "in_proj_bwd")
    dx1, dx1b, d_gm = _rms_bwd(dh2, x1, small["mix_norm"], dx2, "mix_rms_bwd")

    gx, _, d_g1, d_wg1, d_wu1, d_wd1 = _ffn_bwd(
        dx1, dx1b, x, small["ffn1_norm"], big["ffn1_gate"], big["ffn1_up"], big["ffn1_down"], ffn1, "ffn1")

    d_small = {"ffn1_norm": d_g1, "mix_norm": d_gm, "swa_sinks": d_sinks[:, :N_SWA_HEADS], "swa_out_norm": d_ga,
               "sb_out_norm": d_gb, "ffn2_norm": d_g2, "final_norm": d_final}
    d_big = {"ffn1_gate": d_wg1, "ffn1_up": d_wu1, "ffn1_down": d_wd1, "w_in": d_win, "w_out": d_wout,
             "ffn2_gate": d_wg2, "ffn2_up": d_wu2, "ffn2_down": d_wd2}
    return loss, gx, d_small, d_big


MESH = pl.DeviceIdType.MESH
BIG_NAMES = ("ffn1_gate", "ffn1_up", "ffn1_down", "w_in", "w_out", "ffn2_gate", "ffn2_up", "ffn2_down")
_ANY = pl.BlockSpec(memory_space=pl.ANY)
_COMM_PARAMS = pltpu.CompilerParams(has_side_effects=True)


def _place():
    x, y, c = lax.axis_index("x"), lax.axis_index("y"), lax.axis_index("c")
    other_chips = [(1 - x, y), (x, 1 - y), (1 - x, 1 - y)]
    return x, y, c, other_chips


def _padded_rows(rows):
    full = N_DEV * rows
    return -(-full // _F_TILE) * _F_TILE


def _all_gather(shards):
    nw = len(shards)
    D = shards[0].shape[1]
    rows_w = [s.shape[0] for s in shards]
    full_w = [_padded_rows(r) for r in rows_w]
    pad_w = [f - N_DEV * r for f, r in zip(full_w, rows_w)]
    max_pad = max(max(pad_w), 16)

    def body(*refs):
        ins, outs = refs[:nw], refs[nw:2 * nw]
        zbuf, send_sems, recv_sems, local_sems, zero_sems = refs[2 * nw:]
        x, y, c, chips = _place()
        me, sibling = (x, y, c), (x, y, 1 - c)

        def rows(w, px, py, pc):
            start = pl.multiple_of((4 * px + 2 * py + pc) * rows_w[w], 16)
            return outs[w].at[pl.ds(start, rows_w[w]), :]

        def copy(w, k, block, to, src=None):
            return pltpu.make_async_remote_copy(
                src_ref=rows(w, *block) if src is None else src, dst_ref=rows(w, *block),
                send_sem=send_sems.at[w, k], recv_sem=recv_sems.at[w, k], device_id=to, device_id_type=MESH)

        zbuf[...] = jnp.zeros_like(zbuf)
        zero_copies = [pltpu.make_async_copy(zbuf.at[pl.ds(0, pad_w[w]), :],
                                             outs[w].at[pl.ds(N_DEV * rows_w[w], pad_w[w]), :], zero_sems.at[w])
                       for w in range(nw) if pad_w[w]]
        for cp in zero_copies:
            cp.start()
        local_copies, sends = list(zero_copies), []
        for w in range(nw):
            mine = pltpu.make_async_copy(ins[w], rows(w, *me), local_sems.at[w])
            mine.start()
            local_copies.append(mine)
            first = [copy(w, 0, me, sibling, src=ins[w])]
            first += [copy(w, 1 + j, me, (*chip, c), src=ins[w]) for j, chip in enumerate(chips)]
            for cp in first:
                cp.start()
            sends += first
        for w in range(nw):
            for j, chip in enumerate(chips):
                copy(w, 1 + j, (*chip, c), me).wait_recv()
                passed = copy(w, 4 + j, (*chip, c), sibling)
                passed.start()
                sends.append(passed)
        for w in range(nw):
            copy(w, 0, sibling, me).wait_recv()
            for j, chip in enumerate(chips):
                copy(w, 4 + j, (*chip, 1 - c), me).wait_recv()
        for cp in sends:
            cp.wait_send()
        for cp in local_copies:
            cp.wait()

    return pl.pallas_call(
        body, name="all_gather_weights",
        out_shape=tuple(jax.ShapeDtypeStruct((f, D), s.dtype) for f, s in zip(full_w, shards)),
        in_specs=[_ANY] * nw, out_specs=tuple([_ANY] * nw),
        scratch_shapes=[pltpu.VMEM((max_pad, D), shards[0].dtype), pltpu.SemaphoreType.DMA((nw, 7)),
                        pltpu.SemaphoreType.DMA((nw, 7)), pltpu.SemaphoreType.DMA((nw,)),
                        pltpu.SemaphoreType.DMA((nw,))],
        compiler_params=_COMM_PARAMS,
    )(*shards)


def _rs_pair(partials, rows_w):
    nw = len(partials)
    D = partials[0].shape[1]

    def body(*refs):
        ins, outs = refs[:nw], refs[nw:2 * nw]
        send_sems, recv_sems = refs[2 * nw:]
        x, y, c, _ = _place()
        sibling = (x, y, 1 - c)
        copies = []
        for w in range(nw):
            r = rows_w[w]
            for q in range(4):
                src = ins[w].at[pl.ds(pl.multiple_of((2 * q + 1 - c) * r, 16), r), :]
                dst = outs[w].at[pl.ds(q * r, r), :]
                cp = pltpu.make_async_remote_copy(src_ref=src, dst_ref=dst, send_sem=send_sems.at[w, q],
                                                  recv_sem=recv_sems.at[w, q], device_id=sibling, device_id_type=MESH)
                cp.start()
                copies.append(cp)
        for cp in copies:
            cp.wait()

    return pl.pallas_call(
        body, name="reduce_scatter_pair",
        out_shape=tuple(jax.ShapeDtypeStruct((4 * r, D), p.dtype) for r, p in zip(rows_w, partials)),
        in_specs=[_ANY] * nw, out_specs=tuple([_ANY] * nw),
        scratch_shapes=[pltpu.SemaphoreType.DMA((nw, 4)), pltpu.SemaphoreType.DMA((nw, 4))],
        compiler_params=_COMM_PARAMS,
    )(*partials)


def _pair_sum(partial, from_sibling, rows, core, name):
    D = partial.shape[1]

    def body(core_ref, p_ref, s_ref, o_ref):
        o_ref[...] = (p_ref[...].astype(F32) + s_ref[...].astype(F32)).astype(o_ref.dtype)

    grid_spec = pltpu.PrefetchScalarGridSpec(
        num_scalar_prefetch=1, grid=(4,),
        in_specs=[pl.BlockSpec((rows, D), lambda q, core_ref: (2 * q + core_ref[0], 0)),
                  pl.BlockSpec((rows, D), lambda q, core_ref: (q, 0))],
        out_specs=pl.BlockSpec((rows, D), lambda q, core_ref: (q, 0)))
    return pl.pallas_call(
        body, name=name, grid_spec=grid_spec, out_shape=jax.ShapeDtypeStruct((4 * rows, D), partial.dtype),
        compiler_params=_params("arbitrary"),
    )(core, partial, from_sibling)


def _rs_chips(chip_sums, rows_w):
    nw = len(chip_sums)
    D = chip_sums[0].shape[1]

    def body(*refs):
        ins, outs = refs[:nw], refs[nw:2 * nw]
        send_sems, recv_sems, local_sems = refs[2 * nw:]
        x, y, c, chips = _place()
        my_chip = 2 * x + y
        copies = []
        for w in range(nw):
            r = rows_w[w]
            mine = pl.ds(pl.multiple_of(my_chip * r, 16), r)
            own = pltpu.make_async_copy(ins[w].at[mine, :], outs[w].at[mine, :], local_sems.at[w])
            own.start()
            copies.append(own)
            for j, (qx, qy) in enumerate(chips):
                src = ins[w].at[pl.ds(pl.multiple_of((2 * qx + qy) * r, 16), r), :]
                cp = pltpu.make_async_remote_copy(src_ref=src, dst_ref=outs[w].at[mine, :], send_sem=send_sems.at[w, j],
                                                  recv_sem=recv_sems.at[w, j], device_id=(qx, qy, c), device_id_type=MESH)
                cp.start()
                copies.append(cp)
        for cp in copies:
            cp.wait()

    return pl.pallas_call(
        body, name="reduce_scatter_chips",
        out_shape=tuple(jax.ShapeDtypeStruct(s.shape, s.dtype) for s in chip_sums),
        in_specs=[_ANY] * nw, out_specs=tuple([_ANY] * nw),
        scratch_shapes=[pltpu.SemaphoreType.DMA((nw, 3)), pltpu.SemaphoreType.DMA((nw, 3)),
                        pltpu.SemaphoreType.DMA((nw,))],
        compiler_params=_COMM_PARAMS,
    )(*chip_sums)


def _chip_sum(slots, rows, name):
    D = slots.shape[1]

    def body(s_ref, o_ref):
        acc = s_ref[0:rows, :].astype(F32)
        for q in range(1, 4):
            acc = acc + s_ref[q * rows:(q + 1) * rows, :].astype(F32)
        o_ref[...] = acc

    tc = _tile(D, 512, LANES)
    return pl.pallas_call(
        body, name=name, grid=(D // tc,), out_shape=jax.ShapeDtypeStruct((rows, D), F32),
        in_specs=[pl.BlockSpec((4 * rows, tc), lambda j: (0, j))], out_specs=pl.BlockSpec((rows, tc), lambda j: (0, j)),
        compiler_params=_params("parallel"),
    )(slots)


SMALL_ROWS = 88


def _small_allreduce(vec):
    def body(v_ref, o_ref, gather, send_sems, recv_sems):
        x, y, c, _ = _place()
        my_id = 4 * x + 2 * y + c
        gather[my_id] = v_ref[...]
        copies = []
        for r in range(1, N_DEV):
            peer = (x ^ (r >> 2), y ^ ((r >> 1) & 1), c ^ (r & 1))
            cp = pltpu.make_async_remote_copy(src_ref=v_ref, dst_ref=gather.at[my_id], send_sem=send_sems.at[r - 1],
                                              recv_sem=recv_sems.at[r - 1], device_id=peer, device_id_type=MESH)
            cp.start()
            copies.append(cp)
        for cp in copies:
            cp.wait()
        acc = gather[0]
        for d in range(1, N_DEV):
            acc = acc + gather[d]
        o_ref[...] = acc

    vm = pl.BlockSpec(memory_space=pltpu.VMEM)
    return pl.pallas_call(
        body, name="small_allreduce", out_shape=jax.ShapeDtypeStruct(vec.shape, F32),
        in_specs=[vm], out_specs=vm,
        scratch_shapes=[pltpu.VMEM((N_DEV,) + vec.shape, F32), pltpu.SemaphoreType.DMA((N_DEV - 1,)),
                        pltpu.SemaphoreType.DMA((N_DEV - 1,))],
        compiler_params=_COMM_PARAMS,
    )(vec)


def _adamw(w, g, m, v, name):
    R, C = w.shape
    tr = _tile(R, 256, 8)

    def body(w_ref, g_ref, m_ref, v_ref, d_ref, nm_ref, nv_ref):
        gv = g_ref[...]
        nm = ADAM_B1 * m_ref[...] + (1.0 - ADAM_B1) * gv
        nv = ADAM_B2 * v_ref[...] + (1.0 - ADAM_B2) * jnp.square(gv)
        m_hat = nm / (1.0 - ADAM_B1 ** ADAM_STEP)
        v_hat = nv / (1.0 - ADAM_B2 ** ADAM_STEP)
        d_ref[...] = -ADAM_LR * (m_hat / (jnp.sqrt(v_hat) + ADAM_EPS) + ADAM_WD * w_ref[...])
        nm_ref[...] = nm
        nv_ref[...] = nv

    spec = pl.BlockSpec((tr, C), lambda i: (i, 0))
    out = jax.ShapeDtypeStruct((R, C), F32)
    return pl.pallas_call(
        body, name=name, grid=(R // tr,), out_shape=(out, out, out),
        in_specs=[spec] * 4, out_specs=(spec, spec, spec),
        compiler_params=_params("parallel"),
    )(w, g, m, v)


WEIGHT_NAMES = ("ffn1_norm", "ffn1_w_gate", "ffn1_w_up", "ffn1_w_down", "mix_norm", "w_in", "swa_sinks",
                "swa_out_norm", "sb_out_norm", "w_out", "ffn2_norm", "ffn2_w_gate", "ffn2_w_up", "ffn2_w_down",
                "final_norm")
SMALL_NAMES = ("ffn1_norm", "mix_norm", "swa_sinks", "swa_out_norm", "sb_out_norm", "ffn2_norm", "final_norm")
BIG_ARGS = {"ffn1_gate": ("ffn1_w_gate", True), "ffn1_up": ("ffn1_w_up", True), "ffn1_down": ("ffn1_w_down", False),
            "w_in": ("w_in", True), "w_out": ("w_out", False), "ffn2_gate": ("ffn2_w_gate", True),
            "ffn2_up": ("ffn2_w_up", True), "ffn2_down": ("ffn2_w_down", False)}


def _pack_small(parts):
    padded = [jnp.pad(p.reshape(1, -1), ((0, 0), (0, -p.size % LANES))) for p in parts]
    flat = jnp.concatenate(padded, axis=1)
    flat = jnp.pad(flat, ((0, 0), (0, SMALL_ROWS * LANES - flat.shape[1])))
    return flat.reshape(SMALL_ROWS, LANES)


def _unpack_small(block, shapes):
    flat = block.reshape(-1)
    out, off = [], 0
    for shp in shapes:
        n = 1
        for s in shp:
            n *= s
        out.append(flat[off:off + n].reshape(shp))
        off += n + (-n % LANES)
    return out


def kernel(x, ffn1_norm, ffn1_w_gate, ffn1_w_up, ffn1_w_down, mix_norm, w_in, swa_sinks, swa_out_norm, sb_out_norm, w_out, ffn2_norm, ffn2_w_gate, ffn2_w_up, ffn2_w_down, final_norm, loss_target, m_ffn1_norm, m_ffn1_w_gate, m_ffn1_w_up, m_ffn1_w_down, m_mix_norm, m_w_in, m_swa_sinks, m_swa_out_norm, m_sb_out_norm, m_w_out, m_ffn2_norm, m_ffn2_w_gate, m_ffn2_w_up, m_ffn2_w_down, m_final_norm, v_ffn1_norm, v_ffn1_w_gate, v_ffn1_w_up, v_ffn1_w_down, v_mix_norm, v_w_in, v_swa_sinks, v_swa_out_norm, v_sb_out_norm, v_w_out, v_ffn2_norm, v_ffn2_w_gate, v_ffn2_w_up, v_ffn2_w_down, v_final_norm):
    args = dict(locals())
    B, S, D = x.shape
    T = B * S
    weights = {n: args[n] for n in WEIGHT_NAMES}
    mom_m = {n: args["m_" + n] for n in WEIGHT_NAMES}
    mom_v = {n: args["v_" + n] for n in WEIGHT_NAMES}

    shards = []
    for name in BIG_NAMES:
        arg, transposed = BIG_ARGS[name]
        w2 = weights[arg][0]
        shards.append((w2.T if transposed else w2).astype(_WIRE_DTYPE))
    rows_w = [s.shape[0] for s in shards]
    big = dict(zip(BIG_NAMES, _all_gather(shards)))
    small = {n: weights[n].reshape(1, -1) for n in SMALL_NAMES}

    loss, gx, d_small, d_big = _local_step(x.reshape(T, D), loss_target.reshape(T, D), B, S, small, big)

    partials = [d_big[n] for n in BIG_NAMES]
    from_sibling = _rs_pair(partials, rows_w)
    core = lax.axis_index("c").astype(jnp.int32).reshape(1)
    chip_sums = [_pair_sum(p, s, r, core, "pair_sum_" + n)
                 for p, s, r, n in zip(partials, from_sibling, rows_w, BIG_NAMES)]
    slots = _rs_chips(chip_sums, rows_w)
    g_big = {n: _chip_sum(s, r, "chip_sum_" + n) for n, s, r in zip(BIG_NAMES, slots, rows_w)}

    small_shapes = [(1, 1)] + [d_small[n].shape for n in SMALL_NAMES]
    reduced = _small_allreduce(_pack_small([loss[:, :1]] + [d_small[n] for n in SMALL_NAMES]))
    red = _unpack_small(reduced, small_shapes)
    loss_out = red[0].reshape(())
    g_small = dict(zip(SMALL_NAMES, red[1:]))

    grads, deltas, new_m, new_v = {}, {}, {}, {}
    for name in BIG_NAMES:
        arg, transposed = BIG_ARGS[name]
        g2 = g_big[name].T if transposed else g_big[name]
        d2, m2, v2 = _adamw(weights[arg][0], g2, mom_m[arg][0], mom_v[arg][0], "adamw_" + name)
        grads[arg], deltas[arg], new_m[arg], new_v[arg] = g2[None], d2[None], m2[None], v2[None]
    shapes1 = [(1, weights[n].size) for n in SMALL_NAMES]
    packed = [_pack_small([t[n].reshape(1, -1) for n in SMALL_NAMES]) for t in (weights, g_small, mom_m, mom_v)]
    upd = _adamw(*packed, "adamw_small")
    for tgt_dict, block in zip((deltas, new_m, new_v), upd):
        for n, val in zip(SMALL_NAMES, _unpack_small(block, shapes1)):
            tgt_dict[n] = val.reshape(weights[n].shape)
    for n in SMALL_NAMES:
        grads[n] = g_small[n].reshape(weights[n].shape)

    return (loss_out, gx.reshape(B, S, D), *[grads[n] for n in WEIGHT_NAMES], *[deltas[n] for n in WEIGHT_NAMES],
            *[new_m[n] for n in WEIGHT_NAMES], *[new_v[n] for n in WEIGHT_NAMES])
```

```python
import functools

import jax
import jax.numpy as jnp
from jax import lax
from jax.experimental import pallas as pl
from jax.experimental.pallas import tpu as pltpu

F32 = jnp.float32
_MXU_DTYPE = jnp.bfloat16
_WIRE_DTYPE = jnp.bfloat16

EPS = 1e-6
HEAD_DIM = 64
N_SWA_HEADS = 16
N_SWA_KV = 4
N_SB_HEADS = 16
WINDOW = 128
SWA_Q = N_SWA_HEADS * HEAD_DIM
SWA_KV = N_SWA_KV * HEAD_DIM
SB_W = N_SB_HEADS * HEAD_DIM
IN_W = SWA_Q + 2 * SWA_KV + 3 * SB_W
LANES = 128
ATT_SCALE = HEAD_DIM ** -0.5

ADAM_LR = 0.001
ADAM_B1 = 0.9
ADAM_B2 = 0.999
ADAM_EPS = 1e-08
ADAM_WD = 0.01
ADAM_STEP = 10

N_DEV = 8
_VMEM_LIMIT_BYTES = 48 * 1024 * 1024
_F_TILE = 512


def _params(*semantics):
    return pltpu.CompilerParams(dimension_semantics=semantics, vmem_limit_bytes=_VMEM_LIMIT_BYTES)


def _tile(n, pref, align):
    t = min(n, pref)
    t -= t % align
    while t >= align:
        if n % t == 0:
            return t
        t -= align
    return n


def _dot(a, b):
    return lax.dot_general(a, b, (((1,), (0,)), ((), ())), preferred_element_type=F32)


def _dot_nt(a, b):
    return lax.dot_general(a, b, (((1,), (1,)), ((), ())), preferred_element_type=F32)


def _dot_tn(a, b):
    return lax.dot_general(a, b, (((0,), (0,)), ((), ())), preferred_element_type=F32)


class _Job:
    def __init__(self, ins, out_shape, scratch, start, finish, mid=None):
        self.ins, self.out_shape, self.scratch = list(ins), list(out_shape), list(scratch)
        self.start, self.mid, self.finish = start, mid, finish


_JOB_MID_FRACTION = 0.8


def _call(body, *, name, grid, in_specs, out_specs, out_shape, args, semantics, scratch_shapes=(), jobs=()):
    single = not isinstance(out_shape, (tuple, list))
    if not jobs:
        res = pl.pallas_call(body, name=name, grid=grid, in_specs=list(in_specs), out_specs=out_specs,
                             out_shape=out_shape, scratch_shapes=list(scratch_shapes),
                             compiler_params=_params(*semantics))(*args)
        return res, []
    base_out = [out_shape] if single else list(out_shape)
    base_out_specs = [out_specs] if single else list(out_specs)
    n_in, n_out, n_scr = len(args), len(base_out), len(scratch_shapes)
    any_spec = pl.BlockSpec(memory_space=pl.ANY)
    total = 1
    for g in grid:
        total *= g
    mid_step = min(total - 1, int(total * _JOB_MID_FRACTION))

    def wrapped(*refs):
        pos = n_in
        job_ins = []
        for job in jobs:
            job_ins.append(refs[pos:pos + len(job.ins)])
            pos += len(job.ins)
        outs = refs[pos:pos + n_out]
        pos += n_out
        job_outs = []
        for job in jobs:
            job_outs.append(refs[pos:pos + len(job.out_shape)])
            pos += len(job.out_shape)
        scr = refs[pos:pos + n_scr]
        pos += n_scr
        job_scr = []
        for job in jobs:
            job_scr.append(refs[pos:pos + len(job.scratch)])
            pos += len(job.scratch)
        step = pl.program_id(0)
        for d in range(1, len(grid)):
            step = step * grid[d] + pl.program_id(d)

        @pl.when(step == 0)
        def _():
            for job, ji, jo, js in zip(jobs, job_ins, job_outs, job_scr):
                job.start(ji, jo, js)

        @pl.when(step == mid_step)
        def _():
            for job, ji, jo, js in zip(jobs, job_ins, job_outs, job_scr):
                if job.mid is not None:
                    job.mid(ji, jo, js)

        body(*refs[:n_in], *outs, *scr)

        @pl.when(step == total - 1)
        def _():
            for job, ji, jo, js in zip(jobs, job_ins, job_outs, job_scr):
                job.finish(ji, jo, js)

    all_args = list(args) + [a for job in jobs for a in job.ins]
    all_in_specs = list(in_specs) + [any_spec for job in jobs for _ in job.ins]
    all_out_shape = base_out + [s for job in jobs for s in job.out_shape]
    all_out_specs = base_out_specs + [any_spec for job in jobs for _ in job.out_shape]
    all_scratch = list(scratch_shapes) + [s for job in jobs for s in job.scratch]
    res = pl.pallas_call(
        wrapped, name=name, grid=grid, in_specs=all_in_specs, out_specs=tuple(all_out_specs),
        out_shape=tuple(all_out_shape), scratch_shapes=all_scratch,
        compiler_params=pltpu.CompilerParams(dimension_semantics=("arbitrary",) * len(grid),
                                             vmem_limit_bytes=_VMEM_LIMIT_BYTES, has_side_effects=True),
    )(*all_args)
    base = res[0] if single else tuple(res[:n_out])
    job_res, pos = [], n_out
    for job in jobs:
        job_res.append(tuple(res[pos:pos + len(job.out_shape)]))
        pos += len(job.out_shape)
    return base, job_res


def _rms_fwd(x, g, name):
    T, D = x.shape
    tm = _tile(T, 512, 16)

    def body(x_ref, g_ref, o_ref):
        xv = x_ref[...]
        r = lax.rsqrt(jnp.mean(xv * xv, axis=-1, keepdims=True) + EPS)
        o_ref[...] = (xv * r * g_ref[...]).astype(o_ref.dtype)

    return pl.pallas_call(
        body, name=name, grid=(T // tm,),
        out_shape=jax.ShapeDtypeStruct((T, D), _MXU_DTYPE),
        in_specs=[pl.BlockSpec((tm, D), lambda i: (i, 0)), pl.BlockSpec((1, D), lambda i: (0, 0))],
        out_specs=pl.BlockSpec((tm, D), lambda i: (i, 0)),
        compiler_params=_params("parallel"),
    )(x, g)


def _rms_bwd_rows(dh, xv, g):
    r = lax.rsqrt(jnp.mean(xv * xv, axis=-1, keepdims=True) + EPS)
    xhat = xv * r
    u = dh * g
    dx = r * (u - xhat * jnp.mean(u * xhat, axis=-1, keepdims=True))
    return dx, dh * xhat


def _rms_bwd(dh, x, g, dres, name):
    T, D = x.shape
    tm = _tile(T, 256, 16)

    def body(dh_ref, x_ref, g_ref, dres_ref, dx_ref, dxb_ref, dg_ref):
        @pl.when(pl.program_id(0) == 0)
        def _():
            dg_ref[...] = jnp.zeros_like(dg_ref)

        dx, dgr = _rms_bwd_rows(dh_ref[...], x_ref[...], g_ref[...])
        dx = dres_ref[...] + dx
        dx_ref[...] = dx
        dxb_ref[...] = dx.astype(dxb_ref.dtype)
        dg_ref[...] += jnp.sum(dgr, axis=0, keepdims=True)

    row = pl.BlockSpec((tm, D), lambda i: (i, 0))
    vec = pl.BlockSpec((1, D), lambda i: (0, 0))
    return pl.pallas_call(
        body, name=name, grid=(T // tm,),
        out_shape=(jax.ShapeDtypeStruct((T, D), F32), jax.ShapeDtypeStruct((T, D), _MXU_DTYPE),
                   jax.ShapeDtypeStruct((1, D), F32)),
        in_specs=[row, row, vec, row], out_specs=(row, row, vec),
        compiler_params=_params("arbitrary"),
    )(dh, x, g, dres)


def _loss_head(x, g, tgt, name):
    T, D = x.shape
    tm = _tile(T, 256, 16)

    def body(x_ref, g_ref, t_ref, dx_ref, dxb_ref, dg_ref, loss_ref):
        @pl.when(pl.program_id(0) == 0)
        def _():
            dg_ref[...] = jnp.zeros_like(dg_ref)
            loss_ref[...] = jnp.zeros_like(loss_ref)

        xv = x_ref[...]
        gv = g_ref[...]
        r = lax.rsqrt(jnp.mean(xv * xv, axis=-1, keepdims=True) + EPS)
        xhat = xv * r
        diff = xhat * gv - t_ref[...]
        tok = jnp.mean(diff * diff, axis=-1, keepdims=True)
        loss_ref[...] += 0.5 * jnp.sum(tok, axis=0, keepdims=True)
        dy = diff / D
        u = dy * gv
        dx = r * (u - xhat * jnp.mean(u * xhat, axis=-1, keepdims=True))
        dx_ref[...] = dx
        dxb_ref[...] = dx.astype(dxb_ref.dtype)
        dg_ref[...] += jnp.sum(dy * xhat, axis=0, keepdims=True)

    row = pl.BlockSpec((tm, D), lambda i: (i, 0))
    vec = pl.BlockSpec((1, D), lambda i: (0, 0))
    return pl.pallas_call(
        body, name=name, grid=(T // tm,),
        out_shape=(jax.ShapeDtypeStruct((T, D), F32), jax.ShapeDtypeStruct((T, D), _MXU_DTYPE),
                   jax.ShapeDtypeStruct((1, D), F32), jax.ShapeDtypeStruct((1, LANES), F32)),
        in_specs=[row, vec, row],
        out_specs=(row, row, vec, pl.BlockSpec((1, LANES), lambda i: (0, 0))),
        compiler_params=_params("arbitrary"),
    )(x, g, tgt)


def _outnorm_fwd(ya, yb, ga, gb, name):
    T, W = ya.shape
    tm = _tile(T, 512, 16)

    def body(ya_ref, yb_ref, ga_ref, gb_ref, o_ref):
        for k, (y_ref, g_ref) in enumerate(((ya_ref, ga_ref), (yb_ref, gb_ref))):
            yv = y_ref[...]
            r = lax.rsqrt(jnp.mean(yv * yv, axis=-1, keepdims=True) + EPS)
            o_ref[:, k * W:(k + 1) * W] = (yv * r * g_ref[...]).astype(o_ref.dtype)

    row = pl.BlockSpec((tm, W), lambda i: (i, 0))
    vec = pl.BlockSpec((1, W), lambda i: (0, 0))
    return pl.pallas_call(
        body, name=name, grid=(T // tm,),
        out_shape=jax.ShapeDtypeStruct((T, 2 * W), _MXU_DTYPE),
        in_specs=[row, row, vec, vec], out_specs=pl.BlockSpec((tm, 2 * W), lambda i: (i, 0)),
        compiler_params=_params("parallel"),
    )(ya, yb, ga, gb)


def _outnorm_bwd(dyn, ya, yb, ga, gb, name):
    T, W = ya.shape
    tm = _tile(T, 256, 16)

    def body(d_ref, ya_ref, yb_ref, ga_ref, gb_ref, dya_ref, dyb_ref, dga_ref, dgb_ref):
        @pl.when(pl.program_id(0) == 0)
        def _():
            dga_ref[...] = jnp.zeros_like(dga_ref)
            dgb_ref[...] = jnp.zeros_like(dgb_ref)

        for k, (y_ref, g_ref, dy_ref, dg_ref) in enumerate(
                ((ya_ref, ga_ref, dya_ref, dga_ref), (yb_ref, gb_ref, dyb_ref, dgb_ref))):
            dy, dgr = _rms_bwd_rows(d_ref[:, k * W:(k + 1) * W], y_ref[...], g_ref[...])
            dy_ref[...] = dy.astype(dy_ref.dtype)
            dg_ref[...] += jnp.sum(dgr, axis=0, keepdims=True)

    row = pl.BlockSpec((tm, W), lambda i: (i, 0))
    vec = pl.BlockSpec((1, W), lambda i: (0, 0))
    return pl.pallas_call(
        body, name=name, grid=(T // tm,),
        out_shape=(jax.ShapeDtypeStruct((T, W), _MXU_DTYPE), jax.ShapeDtypeStruct((T, W), _MXU_DTYPE),
                   jax.ShapeDtypeStruct((1, W), F32), jax.ShapeDtypeStruct((1, W), F32)),
        in_specs=[pl.BlockSpec((tm, 2 * W), lambda i: (i, 0)), row, row, vec, vec],
        out_specs=(row, row, vec, vec),
        compiler_params=_params("arbitrary"),
    )(dyn, ya, yb, ga, gb)


def _ffn_gu(h, wg_t, wu_t, name, jobs=()):
    T, D = h.shape
    Fp = wg_t.shape[0]
    tm = _tile(T, 1024, 16)
    tn = _tile(Fp, _F_TILE, LANES)

    def body(h_ref, wg_ref, wu_ref, g_ref, u_ref, a_ref):
        hv = h_ref[...]
        g = _dot_nt(hv, wg_ref[...])
        u = _dot_nt(hv, wu_ref[...])
        g_ref[...] = g.astype(g_ref.dtype)
        u_ref[...] = u.astype(u_ref.dtype)
        a_ref[...] = (g * jax.nn.sigmoid(g) * u).astype(a_ref.dtype)

    act = pl.BlockSpec((tm, tn), lambda n, m: (m, n))
    wsp = pl.BlockSpec((tn, D), lambda n, m: (n, 0))
    out = jax.ShapeDtypeStruct((T, Fp), _MXU_DTYPE)
    return _call(
        body, name=name, grid=(Fp // tn, T // tm), out_shape=(out, out, out),
        in_specs=[pl.BlockSpec((tm, D), lambda n, m: (m, 0)), wsp, wsp],
        out_specs=(act, act, act), args=(h, wg_t, wu_t), semantics=("parallel", "parallel"), jobs=jobs)


def _ffn_bwd_act(dxb, wd, G, U, name, jobs=()):
    T, D = dxb.shape
    Fp = wd.shape[0]
    tm = _tile(T, 1024, 16)
    tn = _tile(Fp, _F_TILE, LANES)

    def body(e_ref, wd_ref, g_ref, u_ref, dg_ref, du_ref):
        da = 0.5 * _dot_nt(e_ref[...], wd_ref[...])
        g = g_ref[...].astype(F32)
        u = u_ref[...].astype(F32)
        s = jax.nn.sigmoid(g)
        du_ref[...] = (da * (g * s)).astype(du_ref.dtype)
        dg_ref[...] = (da * u * (s * (1.0 + g * (1.0 - s)))).astype(dg_ref.dtype)

    act = pl.BlockSpec((tm, tn), lambda n, m: (m, n))
    out = jax.ShapeDtypeStruct((T, Fp), _MXU_DTYPE)
    return _call(
        body, name=name, grid=(Fp // tn, T // tm), out_shape=(out, out),
        in_specs=[pl.BlockSpec((tm, D), lambda n, m: (m, 0)), pl.BlockSpec((tn, D), lambda n, m: (n, 0)),
                  act, act],
        out_specs=(act, act), args=(dxb, wd, G, U), semantics=("parallel", "parallel"), jobs=jobs)


def _mm_nt(a, b, out_dtype, name):
    M, K = a.shape
    N = b.shape[0]
    tm = _tile(M, 1024, 16)
    tn = _tile(N, 512, LANES)

    def body(a_ref, b_ref, o_ref):
        o_ref[...] = _dot_nt(a_ref[...], b_ref[...]).astype(o_ref.dtype)

    return pl.pallas_call(
        body, name=name, grid=(N // tn, M // tm), out_shape=jax.ShapeDtypeStruct((M, N), out_dtype),
        in_specs=[pl.BlockSpec((tm, K), lambda n, m: (m, 0)), pl.BlockSpec((tn, K), lambda n, m: (n, 0))],
        out_specs=pl.BlockSpec((tm, tn), lambda n, m: (m, n)),
        compiler_params=_params("parallel", "parallel"),
    )(a, b)


def _mm_nn(pairs, res, alpha, out_dtype, name, jobs=()):
    M, K = pairs[0][0].shape
    N = pairs[0][1].shape[1]
    tm = _tile(M, 512, 16)
    tn = _tile(N, 2048, LANES)
    tk = _tile(K, 512, LANES)
    nk = K // tk
    n_pairs = len(pairs)

    def body(*refs):
        ab = refs[:2 * n_pairs]
        res_ref = refs[2 * n_pairs] if res is not None else None
        o_ref, acc_ref = refs[-2], refs[-1]
        k = pl.program_id(2)

        @pl.when(k == 0)
        def _():
            acc_ref[...] = jnp.zeros_like(acc_ref)

        part = _dot(ab[0][...], ab[1][...])
        for i in range(1, n_pairs):
            part = part + _dot(ab[2 * i][...], ab[2 * i + 1][...])
        acc_ref[...] += part

        @pl.when(k == nk - 1)
        def _():
            out = alpha * acc_ref[...]
            if res_ref is not None:
                out = res_ref[...] + out
            o_ref[...] = out.astype(o_ref.dtype)

    in_specs, args = [], []
    for a, b in pairs:
        in_specs += [pl.BlockSpec((tm, tk), lambda m, n, k: (m, k)), pl.BlockSpec((tk, tn), lambda m, n, k: (k, n))]
        args += [a, b]
    if res is not None:
        in_specs.append(pl.BlockSpec((tm, tn), lambda m, n, k: (m, n)))
        args.append(res)
    return _call(
        body, name=name, grid=(M // tm, N // tn, nk), out_shape=jax.ShapeDtypeStruct((M, N), out_dtype),
        in_specs=in_specs, out_specs=pl.BlockSpec((tm, tn), lambda m, n, k: (m, n)),
        scratch_shapes=[pltpu.VMEM((tm, tn), F32)], args=args,
        semantics=("parallel", "parallel", "arbitrary"), jobs=jobs)


def _mm_tn(a, b, alpha, out_dtype, name, jobs=()):
    K, M = a.shape
    N = b.shape[1]
    tm = _tile(M, 512, LANES)
    tn = _tile(N, 2048, LANES)
    tk = _tile(K, 512, 16)
    nk = K // tk

    def body(a_ref, b_ref, o_ref, acc_ref):
        k = pl.program_id(2)

        @pl.when(k == 0)
        def _():
            acc_ref[...] = jnp.zeros_like(acc_ref)

        acc_ref[...] += _dot_tn(a_ref[...], b_ref[...])

        @pl.when(k == nk - 1)
        def _():
            o_ref[...] = (alpha * acc_ref[...]).astype(o_ref.dtype)

    return _call(
        body, name=name, grid=(M // tm, N // tn, nk), out_shape=jax.ShapeDtypeStruct((M, N), out_dtype),
        in_specs=[pl.BlockSpec((tk, tm), lambda m, n, k: (k, m)), pl.BlockSpec((tk, tn), lambda m, n, k: (k, n))],
        out_specs=pl.BlockSpec((tm, tn), lambda m, n, k: (m, n)),
        scratch_shapes=[pltpu.VMEM((tm, tn), F32)], args=(a, b),
        semantics=("parallel", "parallel", "arbitrary"), jobs=jobs)


def _half_masks():
    lane = lax.broadcasted_iota(jnp.int32, (1, LANES), 1)
    return (lane < HEAD_DIM, lane >= HEAD_DIM)


def _swap_halves(v):
    return pltpu.roll(v.astype(F32), HEAD_DIM, 1).astype(v.dtype)


def _swa_geometry(n):
    qi = lax.broadcasted_iota(jnp.int32, (WINDOW, 2 * WINDOW), 0)
    kp = lax.broadcasted_iota(jnp.int32, (WINDOW, 2 * WINDOW), 1)
    dist = (WINDOW + qi) - kp
    valid = (dist >= 0) & (dist < WINDOW) & ((n > 0) | (kp >= WINDOW))
    return dist.astype(F32), valid


def _swa_slope(h):
    return 2.0 ** (-8.0 * (h + 1) / N_SWA_HEADS)


def _swa_probs(qm, kx, sink, slope, distf, valid):
    s = _dot_nt(qm, kx) * ATT_SCALE - slope * distf
    s = jnp.where(valid, s, -1e30)
    m = jnp.maximum(jnp.max(s, axis=1, keepdims=True), sink)
    p = jnp.exp(s - m)
    e_sink = jnp.exp(sink - m)
    den = jnp.sum(p, axis=1, keepdims=True) + e_sink
    return p / den, e_sink / den


def _swa_specs(B, S):
    nb = S // WINDOW
    kcol = SWA_Q // SWA_KV
    cur = lambda b, n: (b * nb + n, kcol)
    prev = lambda b, n: (b * nb + jnp.maximum(n - 1, 0), kcol)
    curv = lambda b, n: (b * nb + n, kcol + 1)
    prevv = lambda b, n: (b * nb + jnp.maximum(n - 1, 0), kcol + 1)
    q_spec = pl.BlockSpec((WINDOW, SWA_Q), lambda b, n: (b * nb + n, 0))
    kv = [pl.BlockSpec((WINDOW, SWA_KV), f) for f in (prev, cur, prevv, curv)]
    sink_spec = pl.BlockSpec(memory_space=pltpu.SMEM)
    return nb, q_spec, kv, sink_spec


def _swa_kv_views(kp_ref, kc_ref, vp_ref, vc_ref, g):
    hm = _half_masks()
    c0 = (g // 2) * LANES
    k_all = jnp.concatenate([kp_ref[:, c0:c0 + LANES], kc_ref[:, c0:c0 + LANES]], axis=0)
    v_all = jnp.concatenate([vp_ref[:, c0:c0 + LANES], vc_ref[:, c0:c0 + LANES]], axis=0)
    b = g % 2
    ks, vs = [None, None], [None, None]
    ks[b], vs[b] = k_all, v_all
    ks[1 - b], vs[1 - b] = _swap_halves(k_all), _swap_halves(v_all)
    ks = [jnp.where(hm[a], ks[a], 0) for a in range(2)]
    vs = [jnp.where(hm[a], vs[a], 0) for a in range(2)]
    return ks, vs


def _swa_fwd(proj, sinks, B, S, name):
    T = B * S
    nb, q_spec, kv_specs, sink_spec = _swa_specs(B, S)

    def body(sink_ref, q_ref, kp_ref, kc_ref, vp_ref, vc_ref, y_ref):
        hm = _half_masks()
        distf, valid = _swa_geometry(pl.program_id(1))
        for g in range(N_SWA_KV):
            ks, vs = _swa_kv_views(kp_ref, kc_ref, vp_ref, vc_ref, g)
            for pp in (2 * g, 2 * g + 1):
                q_pair = q_ref[:, pp * LANES:(pp + 1) * LANES]
                o_pair = jnp.zeros((WINDOW, LANES), F32)
                for a in range(2):
                    h = 2 * pp + a
                    qm = jnp.where(hm[a], q_pair, 0)
                    p, _ = _swa_probs(qm, ks[a], sink_ref[0, h], _swa_slope(h), distf, valid)
                    o_pair = o_pair + _dot(p.astype(_MXU_DTYPE), vs[a])
                y_ref[:, pp * LANES:(pp + 1) * LANES] = o_pair

    return pl.pallas_call(
        body, name=name, grid=(B, nb), out_shape=jax.ShapeDtypeStruct((T, SWA_Q), F32),
        in_specs=[sink_spec, q_spec] + kv_specs,
        out_specs=pl.BlockSpec((WINDOW, SWA_Q), lambda b, n: (b * nb + n, 0)),
        compiler_params=_params("parallel", "parallel"),
    )(sinks, proj, proj, proj, proj, proj)


def _swa_bwd(proj, sinks, dya, B, S, name):
    T = B * S
    nb, q_spec, kv_specs, sink_spec = _swa_specs(B, S)

    def body(sink_ref, q_ref, kp_ref, kc_ref, vp_ref, vc_ref, do_ref,
             dq_ref, dk_ref, dv_ref, dsink_ref, dk_acc, dv_acc):
        b_id, n = pl.program_id(0), pl.program_id(1)
        hm = _half_masks()
        lane = lax.broadcasted_iota(jnp.int32, (1, LANES), 1)

        @pl.when((b_id == 0) & (n == 0))
        def _():
            dsink_ref[...] = jnp.zeros_like(dsink_ref)

        @pl.when(n == 0)
        def _():
            dk_acc[...] = jnp.zeros_like(dk_acc)
            dv_acc[...] = jnp.zeros_like(dv_acc)

        distf, valid = _swa_geometry(n)
        r_prev = pl.multiple_of(jnp.maximum(n - 1, 0) * WINDOW, WINDOW)
        r_cur = pl.multiple_of(n * WINDOW, WINDOW)
        dsink = jnp.zeros((1, LANES), F32)
        for g in range(N_SWA_KV):
            ks, vs = _swa_kv_views(kp_ref, kc_ref, vp_ref, vc_ref, g)
            dk_g = [jnp.zeros((2 * WINDOW, LANES), F32) for _ in range(2)]
            dv_g = [jnp.zeros((2 * WINDOW, LANES), F32) for _ in range(2)]
            for pp in (2 * g, 2 * g + 1):
                q_pair = q_ref[:, pp * LANES:(pp + 1) * LANES]
                do_pair = do_ref[:, pp * LANES:(pp + 1) * LANES]
                dq_pair = jnp.zeros((WINDOW, LANES), F32)
                for a in range(2):
                    h = 2 * pp + a
                    qm = jnp.where(hm[a], q_pair, 0)
                    dom = jnp.where(hm[a], do_pair, 0)
                    p, p_sink = _swa_probs(qm, ks[a], sink_ref[0, h], _swa_slope(h), distf, valid)
                    dp = _dot_nt(dom, vs[a])
                    delta = jnp.sum(p * dp, axis=1, keepdims=True)
                    ds = (p * (dp - delta) * ATT_SCALE).astype(_MXU_DTYPE)
                    dsink = dsink + jnp.where(lane == h, -jnp.sum(p_sink * delta), 0.0)
                    dq_pair = dq_pair + _dot(ds, ks[a])
                    dk_g[a] = dk_g[a] + _dot_tn(ds, qm)
                    dv_g[a] = dv_g[a] + _dot_tn(p.astype(_MXU_DTYPE), dom)
                dq_ref[:, pp * LANES:(pp + 1) * LANES] = dq_pair.astype(dq_ref.dtype)
            bsel = g % 2
            dk_t = dk_g[bsel] + pltpu.roll(dk_g[1 - bsel], HEAD_DIM, 1)
            dv_t = dv_g[bsel] + pltpu.roll(dv_g[1 - bsel], HEAD_DIM, 1)
            c0 = (g // 2) * LANES
            dk_acc[pl.ds(r_prev, WINDOW), c0:c0 + LANES] += dk_t[:WINDOW]
            dk_acc[pl.ds(r_cur, WINDOW), c0:c0 + LANES] += dk_t[WINDOW:]
            dv_acc[pl.ds(r_prev, WINDOW), c0:c0 + LANES] += dv_t[:WINDOW]
            dv_acc[pl.ds(r_cur, WINDOW), c0:c0 + LANES] += dv_t[WINDOW:]
        dsink_ref[...] += dsink

        @pl.when(n == nb - 1)
        def _():
            dk_ref[...] = dk_acc[...].astype(dk_ref.dtype)
            dv_ref[...] = dv_acc[...].astype(dv_ref.dtype)

    seq_kv = pl.BlockSpec((S, SWA_KV), lambda b, n: (b, 0))
    return pl.pallas_call(
        body, name=name, grid=(B, nb),
        out_shape=(jax.ShapeDtypeStruct((T, SWA_Q), _MXU_DTYPE), jax.ShapeDtypeStruct((T, SWA_KV), _MXU_DTYPE),
                   jax.ShapeDtypeStruct((T, SWA_KV), _MXU_DTYPE), jax.ShapeDtypeStruct((1, LANES), F32)),
        in_specs=[sink_spec, q_spec] + kv_specs + [pl.BlockSpec((WINDOW, SWA_Q), lambda b, n: (b * nb + n, 0))],
        out_specs=(pl.BlockSpec((WINDOW, SWA_Q), lambda b, n: (b * nb + n, 0)), seq_kv, seq_kv,
                   pl.BlockSpec((1, LANES), lambda b, n: (0, 0))),
        scratch_shapes=[pltpu.VMEM((S, SWA_KV), F32), pltpu.VMEM((S, SWA_KV), F32)],
        compiler_params=_params("arbitrary", "arbitrary"),
    )(sinks, proj, proj, proj, proj, proj, dya)


SB_TILE = 256
SB_HALF = 128


def _tri2(cond):
    j = lax.broadcasted_iota(jnp.int32, (2 * SB_HALF, SB_HALF), 0) & (SB_HALF - 1)
    s = lax.broadcasted_iota(jnp.int32, (2 * SB_HALF, SB_HALF), 1)
    return cond(j, s).astype(_MXU_DTYPE)


def _half_cumsums(x, tri2):
    out = []
    for h in range(2):
        xh = x[:, h * SB_HALF:(h + 1) * SB_HALF]
        hi = xh.astype(_MXU_DTYPE)
        lo = (xh - hi.astype(F32)).astype(_MXU_DTYPE)
        out.append(_dot(jnp.concatenate([hi, lo], axis=1), tri2))
    return out


def _log_sigmoid(z):
    return jnp.minimum(z, 0.0) - jnp.log(1.0 + jnp.exp(-jnp.abs(z)))


def _sb_specs(B, S):
    qb = (SWA_Q + 2 * SWA_KV) // LANES
    kb = qb + SB_W // LANES
    vb = kb + SB_W // LANES
    return [pl.BlockSpec((S, LANES), functools.partial(lambda b, p, c: (b, c + p), c=c)) for c in (qb, kb, vb)]


def _sb_fwd(proj, B, S, name, jobs=()):
    T = B * S
    tq = SB_TILE
    nq = S // tq

    def body(q_ref, k_ref, v_ref, y_ref, tot_ref):
        hm = _half_masks()
        ji = lax.broadcasted_iota(jnp.int32, (tq, tq), 0)
        si = lax.broadcasted_iota(jnp.int32, (tq, tq), 1)
        tri_after = _tri2(lambda j, s: j > s)
        causal = si < ji

        def q_loop(qi, carry):
            r0 = pl.multiple_of(qi * tq, tq)
            q_pair = q_ref[pl.ds(r0, tq), :] * ATT_SCALE
            qms = [jnp.where(hm[a], q_pair, 0) for a in range(2)]

            def tile(c0, state, diagonal):
                kk = k_ref[pl.ds(c0, tq), :]
                vv = v_ref[pl.ds(c0, tq), :]
                two = range(2)
                z = [_dot_nt(qms[a], kk) for a in two]
                lb = [_log_sigmoid(z[a]) for a in two]
                l1m = [jnp.where(causal, lb[a] - z[a], 0.0) if diagonal else lb[a] - z[a] for a in two]
                cum = [_half_cumsums(l1m[a], tri_after) for a in two]
                tot = [[cum[a][h][:, 0:1] + l1m[a][:, h * SB_HALF:h * SB_HALF + 1] for h in two] for a in two]
                after = [jnp.concatenate([cum[a][0] + (state[a][1] + tot[a][1]), cum[a][1] + state[a][1]], axis=1)
                         for a in two]
                att = [jnp.exp(lb[a] + after[a]) for a in two]
                if diagonal:
                    att = [jnp.where(causal, att[a], 0.0) for a in two]
                acc = [state[a][0] + _dot(att[a].astype(_MXU_DTYPE), jnp.where(hm[a], vv, 0)) for a in two]
                car = [state[a][1] + (tot[a][0] + tot[a][1]) for a in two]
                return tuple((acc[a], car[a]) for a in two)

            zero = (jnp.zeros((tq, LANES), F32), jnp.zeros((tq, 1), F32))
            state = tile(r0, (zero, zero), True)
            state = lax.fori_loop(
                0, qi, lambda it, st: tile(pl.multiple_of((qi - 1 - it) * tq, tq), st, False), state)
            y_ref[pl.ds(r0, tq), :] = state[0][0] + state[1][0]
            tot_ref[pl.ds(r0, tq), :] = jnp.where(hm[0], state[0][1], state[1][1])
            return carry

        lax.fori_loop(0, nq, q_loop, 0)

    out_spec = pl.BlockSpec((S, LANES), lambda b, p: (b, p))
    return _call(
        body, name=name, grid=(B, SB_W // LANES),
        out_shape=(jax.ShapeDtypeStruct((T, SB_W), F32), jax.ShapeDtypeStruct((T, SB_W), F32)),
        in_specs=_sb_specs(B, S), out_specs=(out_spec, out_spec), args=(proj, proj, proj),
        semantics=("parallel", "parallel"), jobs=jobs)


def _sb_bwd(proj, tot, dyb, B, S, name, jobs=()):
    T = B * S
    tq = SB_TILE
    nq = S // tq

    def body(q_ref, k_ref, v_ref, do_ref, tot_ref, dq_ref, dk_ref, dv_ref, dk_acc, dv_acc):
        hm = _half_masks()
        ji = lax.broadcasted_iota(jnp.int32, (tq, tq), 0)
        si = lax.broadcasted_iota(jnp.int32, (tq, tq), 1)
        tri_incl = _tri2(lambda j, s: j <= s)
        tri_excl = _tri2(lambda j, s: j < s)
        causal = si < ji
        dk_acc[...] = jnp.zeros_like(dk_acc)
        dv_acc[...] = jnp.zeros_like(dv_acc)

        def q_loop(qi, carry):
            r0 = pl.multiple_of(qi * tq, tq)
            q_pair = q_ref[pl.ds(r0, tq), :] * ATT_SCALE
            do_pair = do_ref[pl.ds(r0, tq), :]
            tot_pair = tot_ref[pl.ds(r0, tq), :]
            qms = [jnp.where(hm[a], q_pair, 0) for a in range(2)]
            doms = [jnp.where(hm[a], do_pair, 0) for a in range(2)]
            totals = [jnp.max(jnp.where(hm[a], tot_pair, -jnp.inf), axis=1, keepdims=True) for a in range(2)]

            def tile(c0, state, diagonal):
                kk = k_ref[pl.ds(c0, tq), :]
                vv = v_ref[pl.ds(c0, tq), :]
                ks = kk * ATT_SCALE
                two = range(2)
                last = SB_HALF - 1
                z = [_dot_nt(qms[a], kk) for a in two]
                d_att = [_dot_nt(doms[a], vv) for a in two]
                lb = [_log_sigmoid(z[a]) for a in two]
                l1m = [jnp.where(causal, lb[a] - z[a], 0.0) if diagonal else lb[a] - z[a] for a in two]
                cum = [_half_cumsums(l1m[a], tri_incl) for a in two]
                upto = [jnp.concatenate([cum[a][0] + state[a][1],
                                         cum[a][1] + (state[a][1] + cum[a][0][:, last:last + 1])], axis=1) for a in two]
                att = [jnp.exp(lb[a] + (totals[a] - upto[a])) for a in two]
                if diagonal:
                    att = [jnp.where(causal, att[a], 0.0) for a in two]
                d_log = [d_att[a] * att[a] for a in two]
                cumd = [_half_cumsums(d_log[a], tri_excl) for a in two]
                totd = [[cumd[a][h][:, last:last + 1] + d_log[a][:, h * SB_HALF + last:h * SB_HALF + last + 1]
                         for h in two] for a in two]
                before = [jnp.concatenate([cumd[a][0] + state[a][2], cumd[a][1] + (state[a][2] + totd[a][0])], axis=1)
                          for a in two]
                sig = [jnp.exp(lb[a]) for a in two]
                dz = [d_log[a] * (1.0 - sig[a]) - sig[a] * before[a] for a in two]
                if diagonal:
                    dz = [jnp.where(causal, dz[a], 0.0) for a in two]
                dzb = [dz[a].astype(_MXU_DTYPE) for a in two]
                dq = [state[a][0] + _dot(dzb[a], jnp.where(hm[a], ks, 0)) for a in two]
                dk_acc[pl.ds(c0, tq), :] += _dot_tn(dzb[0], qms[0]) + _dot_tn(dzb[1], qms[1])
                dv_acc[pl.ds(c0, tq), :] += (_dot_tn(att[0].astype(_MXU_DTYPE), doms[0])
                                             + _dot_tn(att[1].astype(_MXU_DTYPE), doms[1]))
                cp = [upto[a][:, tq - 1:tq] for a in two]
                cq = [state[a][2] + (totd[a][0] + totd[a][1]) for a in two]
                return tuple((dq[a], cp[a], cq[a]) for a in two)

            zero_col = jnp.zeros((tq, 1), F32)
            zero = (jnp.zeros((tq, LANES), F32), zero_col, zero_col)
            state = lax.fori_loop(0, qi, lambda kj, st: tile(pl.multiple_of(kj * tq, tq), st, False), (zero, zero))
            state = tile(r0, state, True)
            dq_ref[pl.ds(r0, tq), :] = (state[0][0] + state[1][0]).astype(dq_ref.dtype)
            return carry

        lax.fori_loop(0, nq, q_loop, 0)
        dk_ref[...] = dk_acc[...].astype(dk_ref.dtype)
        dv_ref[...] = dv_acc[...].astype(dv_ref.dtype)

    pair = pl.BlockSpec((S, LANES), lambda b, p: (b, p))
    out = jax.ShapeDtypeStruct((T, SB_W), _MXU_DTYPE)
    return _call(
        body, name=name, grid=(B, SB_W // LANES), out_shape=(out, out, out),
        in_specs=_sb_specs(B, S) + [pair, pair], out_specs=(pair, pair, pair),
        scratch_shapes=[pltpu.VMEM((S, LANES), F32), pltpu.VMEM((S, LANES), F32)],
        args=(proj, proj, proj, dyb, tot), semantics=("parallel", "parallel"), jobs=jobs)


def _layer_step(x, tgt, B, S, small, comm):
    run, big, part = comm.run, comm.big, comm.partial
    ffn1_w, ffn2_w = ("ffn1_down", "ffn1_gate", "ffn1_up"), ("ffn2_down", "ffn2_gate", "ffn2_up")

    run(_idle_host, "all_gather_first", ag=("ffn1_gate", "ffn1_up"))
    h1 = _rms_fwd(x, small["ffn1_norm"], "ffn1_rms")
    G1, U1, A1 = run(_ffn_gu, h1, big["ffn1_gate"], big["ffn1_up"], "ffn1_gate_up", ag=("ffn1_down", "w_in", "w_out"))
    x1 = run(_mm_nn, [(A1, big["ffn1_down"])], x, 0.5, F32, "ffn1_down")
    h2 = _rms_fwd(x1, small["mix_norm"], "mix_rms")
    proj = _mm_nt(h2, big["w_in"], _MXU_DTYPE, "in_proj")
    ya = _swa_fwd(proj, small["swa_sinks"], B, S, "swa_fwd")
    yb, tot = run(_sb_fwd, proj, B, S, "sb_fwd", ag=("ffn2_gate", "ffn2_up", "ffn2_down"))
    yn = _outnorm_fwd(ya, yb, small["swa_out_norm"], small["sb_out_norm"], "out_norm")
    x2 = run(_mm_nn, [(yn, big["w_out"])], x1, 1.0, F32, "out_proj")
    h3 = _rms_fwd(x2, small["ffn2_norm"], "ffn2_rms")
    G2, U2, A2 = run(_ffn_gu, h3, big["ffn2_gate"], big["ffn2_up"], "ffn2_gate_up")
    x3 = run(_mm_nn, [(A2, big["ffn2_down"])], x2, 0.5, F32, "ffn2_down")

    dx3, dx3b, d_final, loss = _loss_head(x3, small["final_norm"], tgt, "loss_head")

    dG2, dU2 = run(_ffn_bwd_act, dx3b, big["ffn2_down"], G2, U2, "ffn2_bwd_act")
    part["ffn2_down"] = run(_mm_tn, A2, dx3b, 0.5, _WIRE_DTYPE, "ffn2_dw_down")
    part["ffn2_gate"] = run(_mm_tn, dG2, h3, 1.0, _WIRE_DTYPE, "ffn2_dw_gate")
    part["ffn2_up"] = run(_mm_tn, dU2, h3, 1.0, _WIRE_DTYPE, "ffn2_dw_up")
    dh3 = run(_mm_nn, [(dG2, big["ffn2_gate"]), (dU2, big["ffn2_up"])], None, 1.0, F32, "ffn2_dh", rs1=ffn2_w)
    dx2, dx2b, d_g2 = _rms_bwd(dh3, x2, small["ffn2_norm"], dx3, "ffn2_rms_bwd")

    part["w_out"] = run(_mm_tn, yn, dx2b, 1.0, _WIRE_DTYPE, "dw_out")
    dyn = _mm_nt(dx2b, big["w_out"], F32, "out_proj_bwd")
    dya, dyb, d_ga, d_gb = _outnorm_bwd(dyn, ya, yb, small["swa_out_norm"], small["sb_out_norm"], "out_norm_bwd")
    dqa, dka, dva, d_sinks = _swa_bwd(proj, small["swa_sinks"], dya, B, S, "swa_bwd")
    dqb, dkb, dvb = run(_sb_bwd, proj, tot, dyb, B, S, "sb_bwd", rs2=ffn2_w)
    dproj = jnp.concatenate([dqa, dka, dva, dqb, dkb, dvb], axis=1)
    part["w_in"] = run(_mm_tn, dproj, h2, 1.0, _WIRE_DTYPE, "dw_in")
    dh2 = run(_mm_nn, [(dproj, big["w_in"])], None, 1.0, F32, "in_proj_bwd", rs1=("w_in", "w_out"))
    dx1, dx1b, d_gm = _rms_bwd(dh2, x1, small["mix_norm"], dx2, "mix_rms_bwd")

    dG1, dU1 = run(_ffn_bwd_act, dx1b, big["ffn1_down"], G1, U1, "ffn1_bwd_act", rs2=("w_in", "w_out"))
    part["ffn1_down"] = run(_mm_tn, A1, dx1b, 0.5, _WIRE_DTYPE, "ffn1_dw_down")
    part["ffn1_gate"] = run(_mm_tn, dG1, h1, 1.0, _WIRE_DTYPE, "ffn1_dw_gate", rs1=("ffn1_down",))
    part["ffn1_up"] = run(_mm_tn, dU1, h1, 1.0, _WIRE_DTYPE, "ffn1_dw_up", rs1=("ffn1_gate",), rs2=("ffn1_down",))
    dh1 = run(_mm_nn, [(dG1, big["ffn1_gate"])], None, 1.0, F32, "ffn1_dh_gate", rs1=("ffn1_up",), rs2=("ffn1_gate",))
    dh1 = run(_mm_nn, [(dU1, big["ffn1_up"])], dh1, 1.0, F32, "ffn1_dh_up", rs2=("ffn1_up",))
    gx, _, d_g1 = _rms_bwd(dh1, x, small["ffn1_norm"], dx1, "ffn1_rms_bwd")

    d_small = {"ffn1_norm": d_g1, "mix_norm": d_gm, "swa_sinks": d_sinks[:, :N_SWA_HEADS], "swa_out_norm": d_ga,
               "sb_out_norm": d_gb, "ffn2_norm": d_g2, "final_norm": d_final}
    return loss, gx, d_small


MESH = pl.DeviceIdType.MESH
BIG_NAMES = ("ffn1_gate", "ffn1_up", "ffn1_down", "w_in", "w_out", "ffn2_gate", "ffn2_up", "ffn2_down")
_ANY = pl.BlockSpec(memory_space=pl.ANY)
_COMM_PARAMS = pltpu.CompilerParams(has_side_effects=True)


def _place():
    x, y, c = lax.axis_index("x"), lax.axis_index("y"), lax.axis_index("c")
    other_chips = [(1 - x, y), (x, 1 - y), (1 - x, 1 - y)]
    return x, y, c, other_chips


def _padded_rows(rows):
    full = N_DEV * rows
    return -(-full // _F_TILE) * _F_TILE


def _ag_job(shards):
    nw = len(shards)
    D = shards[0].shape[1]
    rows_w = [s.shape[0] for s in shards]
    full_w = [_padded_rows(r) for r in rows_w]
    pad_w = [f - N_DEV * r for f, r in zip(full_w, rows_w)]
    max_pad = max(max(pad_w), 16)

    class Plan:
        def __init__(self, ins, outs, scratch):
            zbuf, send_sems, recv_sems, local_sems, zero_sems = scratch
            x, y, c, chips = _place()
            me, sibling = (x, y, c), (x, y, 1 - c)

            def rows(w, px, py, pc):
                start = pl.multiple_of((4 * px + 2 * py + pc) * rows_w[w], 16)
                return outs[w].at[pl.ds(start, rows_w[w]), :]

            def copy(w, k, block, to, src=None):
                return pltpu.make_async_remote_copy(
                    src_ref=rows(w, *block) if src is None else src, dst_ref=rows(w, *block),
                    send_sem=send_sems.at[w, k], recv_sem=recv_sems.at[w, k], device_id=to, device_id_type=MESH)

            self.zbuf = zbuf
            self.local = [pltpu.make_async_copy(zbuf.at[pl.ds(0, pad_w[w]), :],
                                                outs[w].at[pl.ds(N_DEV * rows_w[w], pad_w[w]), :], zero_sems.at[w])
                          for w in range(nw) if pad_w[w]]
            self.local += [pltpu.make_async_copy(ins[w], rows(w, *me), local_sems.at[w]) for w in range(nw)]
            self.first = [[copy(w, 0, me, sibling, src=ins[w])]
                          + [copy(w, 1 + j, me, (*chip, c), src=ins[w]) for j, chip in enumerate(chips)]
                          for w in range(nw)]
            self.arrive = [[copy(w, 1 + j, (*chip, c), me) for j, chip in enumerate(chips)] for w in range(nw)]
            self.passed = [[copy(w, 4 + j, (*chip, c), sibling) for j, chip in enumerate(chips)] for w in range(nw)]
            self.from_sibling = [[copy(w, 0, sibling, me)]
                                 + [copy(w, 4 + j, (*chip, 1 - c), me) for j, chip in enumerate(chips)]
                                 for w in range(nw)]

    def start(ins, outs, scratch):
        plan = Plan(ins, outs, scratch)
        plan.zbuf[...] = jnp.zeros_like(plan.zbuf)
        for cp in plan.local:
            cp.start()
        for w in range(nw):
            for cp in plan.first[w]:
                cp.start()

    def mid(ins, outs, scratch):
        plan = Plan(ins, outs, scratch)
        for w in range(nw):
            for arrived, onward in zip(plan.arrive[w], plan.passed[w]):
                arrived.wait_recv()
                onward.start()

    def finish(ins, outs, scratch):
        plan = Plan(ins, outs, scratch)
        for w in range(nw):
            for cp in plan.from_sibling[w]:
                cp.wait_recv()
        for w in range(nw):
            for cp in plan.first[w] + plan.passed[w]:
                cp.wait_send()
        for cp in plan.local:
            cp.wait()

    return _Job(
        ins=shards, out_shape=[jax.ShapeDtypeStruct((f, D), s.dtype) for f, s in zip(full_w, shards)],
        scratch=[pltpu.VMEM((max_pad, D), shards[0].dtype), pltpu.SemaphoreType.DMA((nw, 7)),
                 pltpu.SemaphoreType.DMA((nw, 7)), pltpu.SemaphoreType.DMA((nw,)), pltpu.SemaphoreType.DMA((nw,))],
        start=start, mid=mid, finish=finish)


def _rs1_job(partials, rows_w):
    nw = len(partials)
    D = partials[0].shape[1]

    def copies(ins, outs, scratch):
        send_sems, recv_sems = scratch
        x, y, c, _ = _place()
        out = []
        for w in range(nw):
            r = rows_w[w]
            for q in range(4):
                src = ins[w].at[pl.ds(pl.multiple_of((2 * q + 1 - c) * r, 16), r), :]
                out.append(pltpu.make_async_remote_copy(
                    src_ref=src, dst_ref=outs[w].at[pl.ds(q * r, r), :], send_sem=send_sems.at[w, q],
                    recv_sem=recv_sems.at[w, q], device_id=(x, y, 1 - c), device_id_type=MESH))
        return out

    def start(ins, outs, scratch):
        for cp in copies(ins, outs, scratch):
            cp.start()

    def finish(ins, outs, scratch):
        for cp in copies(ins, outs, scratch):
            cp.wait()

    return _Job(
        ins=partials, out_shape=[jax.ShapeDtypeStruct((4 * r, D), p.dtype) for r, p in zip(rows_w, partials)],
        scratch=[pltpu.SemaphoreType.DMA((nw, 4)), pltpu.SemaphoreType.DMA((nw, 4))], start=start, finish=finish)


def _pair_sum(partial, from_sibling, rows, core, name):
    D = partial.shape[1]

    def body(core_ref, p_ref, s_ref, o_ref):
        o_ref[...] = (p_ref[...].astype(F32) + s_ref[...].astype(F32)).astype(o_ref.dtype)

    grid_spec = pltpu.PrefetchScalarGridSpec(
        num_scalar_prefetch=1, grid=(4,),
        in_specs=[pl.BlockSpec((rows, D), lambda q, core_ref: (2 * q + core_ref[0], 0)),
                  pl.BlockSpec((rows, D), lambda q, core_ref: (q, 0))],
        out_specs=pl.BlockSpec((rows, D), lambda q, core_ref: (q, 0)))
    return pl.pallas_call(
        body, name=name, grid_spec=grid_spec, out_shape=jax.ShapeDtypeStruct((4 * rows, D), partial.dtype),
        compiler_params=_params("arbitrary"),
    )(core, partial, from_sibling)


def _rs2_job(chip_sums, rows_w):
    nw = len(chip_sums)

    def copies(ins, outs, scratch):
        send_sems, recv_sems, local_sems = scratch
        x, y, c, chips = _place()
        my_chip = 2 * x + y
        out = []
        for w in range(nw):
            r = rows_w[w]
            mine = pl.ds(pl.multiple_of(my_chip * r, 16), r)
            out.append(pltpu.make_async_copy(ins[w].at[mine, :], outs[w].at[mine, :], local_sems.at[w]))
            for j, (qx, qy) in enumerate(chips):
                src = ins[w].at[pl.ds(pl.multiple_of((2 * qx + qy) * r, 16), r), :]
                out.append(pltpu.make_async_remote_copy(
                    src_ref=src, dst_ref=outs[w].at[mine, :], send_sem=send_sems.at[w, j],
                    recv_sem=recv_sems.at[w, j], device_id=(qx, qy, c), device_id_type=MESH))
        return out

    def start(ins, outs, scratch):
        for cp in copies(ins, outs, scratch):
            cp.start()

    def finish(ins, outs, scratch):
        for cp in copies(ins, outs, scratch):
            cp.wait()

    return _Job(
        ins=chip_sums, out_shape=[jax.ShapeDtypeStruct(s.shape, s.dtype) for s in chip_sums],
        scratch=[pltpu.SemaphoreType.DMA((nw, 3)), pltpu.SemaphoreType.DMA((nw, 3)), pltpu.SemaphoreType.DMA((nw,))],
        start=start, finish=finish)


def _idle_host(name, jobs=()):
    def body(o_ref):
        o_ref[...] = jnp.zeros_like(o_ref)

    return _call(body, name=name, grid=(1,), in_specs=[], out_specs=pl.BlockSpec((8, LANES), lambda i: (0, 0)),
                 out_shape=jax.ShapeDtypeStruct((8, LANES), F32), args=(), semantics=("arbitrary",), jobs=jobs)


class _Comm:
    def __init__(self, shards):
        self.shards = shards
        self.rows = {n: s.shape[0] for n, s in shards.items()}
        self.core = lax.axis_index("c").astype(jnp.int32).reshape(1)
        self.big, self.partial, self.chip_sums, self.slots = {}, {}, {}, {}

    def run(self, fn, *args, ag=(), rs1=(), rs2=()):
        jobs = []
        if ag:
            jobs.append(_ag_job([self.shards[n] for n in ag]))
        if rs1:
            jobs.append(_rs1_job([self.partial[n] for n in rs1], [self.rows[n] for n in rs1]))
        if rs2:
            jobs.append(_rs2_job([self.chip_sums[n] for n in rs2], [self.rows[n] for n in rs2]))
        out, job_res = fn(*args, jobs=jobs)
        job_res = iter(job_res)
        if ag:
            self.big.update(zip(ag, next(job_res)))
        if rs1:
            for n, got in zip(rs1, next(job_res)):
                self.chip_sums[n] = _pair_sum(self.partial[n], got, self.rows[n], self.core, "pair_sum_" + n)
        if rs2:
            self.slots.update(zip(rs2, next(job_res)))
        return out


def _chip_sum(slots, rows, name):
    D = slots.shape[1]

    def body(s_ref, o_ref):
        acc = s_ref[0:rows, :].astype(F32)
        for q in range(1, 4):
            acc = acc + s_ref[q * rows:(q + 1) * rows, :].astype(F32)
        o_ref[...] = acc

    tc = _tile(D, 512, LANES)
    return pl.pallas_call(
        body, name=name, grid=(D // tc,), out_shape=jax.ShapeDtypeStruct((rows, D), F32),
        in_specs=[pl.BlockSpec((4 * rows, tc), lambda j: (0, j))], out_specs=pl.BlockSpec((rows, tc), lambda j: (0, j)),
        compiler_params=_params("parallel"),
    )(slots)


SMALL_ROWS = 88


def _small_allreduce(vec):
    def body(v_ref, o_ref, gather, send_sems, recv_sems):
        x, y, c, _ = _place()
        my_id = 4 * x + 2 * y + c
        gather[my_id] = v_ref[...]
        copies = []
        for r in range(1, N_DEV):
            peer = (x ^ (r >> 2), y ^ ((r >> 1) & 1), c ^ (r & 1))
            cp = pltpu.make_async_remote_copy(src_ref=v_ref, dst_ref=gather.at[my_id], send_sem=send_sems.at[r - 1],
                                              recv_sem=recv_sems.at[r - 1], device_id=peer, device_id_type=MESH)
            cp.start()
            copies.append(cp)
        for cp in copies:
            cp.wait()
        acc = gather[0]
        for d in range(1, N_DEV):
            acc = acc + gather[d]
        o_ref[...] = acc

    vm = pl.BlockSpec(memory_space=pltpu.VMEM)
    return pl.pallas_call(
        body, name="small_allreduce", out_shape=jax.ShapeDtypeStruct(vec.shape, F32),
        in_specs=[vm], out_specs=vm,
        scratch_shapes=[pltpu.VMEM((N_DEV,) + vec.shape, F32), pltpu.SemaphoreType.DMA((N_DEV - 1,)),
                        pltpu.SemaphoreType.DMA((N_DEV - 1,))],
        compiler_params=_COMM_PARAMS,
    )(vec)


def _adamw(w, g, m, v, name):
    R, C = w.shape
    tr = _tile(R, 256, 8)

    def body(w_ref, g_ref, m_ref, v_ref, d_ref, nm_ref, nv_ref):
        gv = g_ref[...]
        nm = ADAM_B1 * m_ref[...] + (1.0 - ADAM_B1) * gv
        nv = ADAM_B2 * v_ref[...] + (1.0 - ADAM_B2) * jnp.square(gv)
        m_hat = nm / (1.0 - ADAM_B1 ** ADAM_STEP)
        v_hat = nv / (1.0 - ADAM_B2 ** ADAM_STEP)
        d_ref[...] = -ADAM_LR * (m_hat / (jnp.sqrt(v_hat) + ADAM_EPS) + ADAM_WD * w_ref[...])
        nm_ref[...] = nm
        nv_ref[...] = nv

    spec = pl.BlockSpec((tr, C), lambda i: (i, 0))
    out = jax.ShapeDtypeStruct((R, C), F32)
    return pl.pallas_call(
        body, name=name, grid=(R // tr,), out_shape=(out, out, out),
        in_specs=[spec] * 4, out_specs=(spec, spec, spec),
        compiler_params=_params("parallel"),
    )(w, g, m, v)


WEIGHT_NAMES = ("ffn1_norm", "ffn1_w_gate", "ffn1_w_up", "ffn1_w_down", "mix_norm", "w_in", "swa_sinks",
                "swa_out_norm", "sb_out_norm", "w_out", "ffn2_norm", "ffn2_w_gate", "ffn2_w_up", "ffn2_w_down",
                "final_norm")
SMALL_NAMES = ("ffn1_norm", "mix_norm", "swa_sinks", "swa_out_norm", "sb_out_norm", "ffn2_norm", "final_norm")
BIG_ARGS = {"ffn1_gate": ("ffn1_w_gate", True), "ffn1_up": ("ffn1_w_up", True), "ffn1_down": ("ffn1_w_down", False),
            "w_in": ("w_in", True), "w_out": ("w_out", False), "ffn2_gate": ("ffn2_w_gate", True),
            "ffn2_up": ("ffn2_w_up", True), "ffn2_down": ("ffn2_w_down", False)}


def _pack_small(parts):
    padded = [jnp.pad(p.reshape(1, -1), ((0, 0), (0, -p.size % LANES))) for p in parts]
    flat = jnp.concatenate(padded, axis=1)
    flat = jnp.pad(flat, ((0, 0), (0, SMALL_ROWS * LANES - flat.shape[1])))
    return flat.reshape(SMALL_ROWS, LANES)


def _unpack_small(block, shapes):
    flat = block.reshape(-1)
    out, off = [], 0
    for shp in shapes:
        n = 1
        for s in shp:
            n *= s
        out.append(flat[off:off + n].reshape(shp))
        off += n + (-n % LANES)
    return out


def kernel(x, ffn1_norm, ffn1_w_gate, ffn1_w_up, ffn1_w_down, mix_norm, w_in, swa_sinks, swa_out_norm, sb_out_norm, w_out, ffn2_norm, ffn2_w_gate, ffn2_w_up, ffn2_w_down, final_norm, loss_target, m_ffn1_norm, m_ffn1_w_gate, m_ffn1_w_up, m_ffn1_w_down, m_mix_norm, m_w_in, m_swa_sinks, m_swa_out_norm, m_sb_out_norm, m_w_out, m_ffn2_norm, m_ffn2_w_gate, m_ffn2_w_up, m_ffn2_w_down, m_final_norm, v_ffn1_norm, v_ffn1_w_gate, v_ffn1_w_up, v_ffn1_w_down, v_mix_norm, v_w_in, v_swa_sinks, v_swa_out_norm, v_sb_out_norm, v_w_out, v_ffn2_norm, v_ffn2_w_gate, v_ffn2_w_up, v_ffn2_w_down, v_final_norm):
    args = dict(locals())
    B, S, D = x.shape
    T = B * S
    weights = {n: args[n] for n in WEIGHT_NAMES}
    mom_m = {n: args["m_" + n] for n in WEIGHT_NAMES}
    mom_v = {n: args["v_" + n] for n in WEIGHT_NAMES}

    shards = {}
    for name in BIG_NAMES:
        arg, transposed = BIG_ARGS[name]
        w2 = weights[arg][0]
        shards[name] = (w2.T if transposed else w2).astype(_WIRE_DTYPE)
    comm = _Comm(shards)
    small = {n: weights[n].reshape(1, -1) for n in SMALL_NAMES}

    loss, gx, d_small = _layer_step(x.reshape(T, D), loss_target.reshape(T, D), B, S, small, comm)

    g_big = {n: _chip_sum(comm.slots[n], comm.rows[n], "chip_sum_" + n) for n in BIG_NAMES}

    small_shapes = [(1, 1)] + [d_small[n].shape for n in SMALL_NAMES]
    reduced = _small_allreduce(_pack_small([loss[:, :1]] + [d_small[n] for n in SMALL_NAMES]))
    red = _unpack_small(reduced, small_shapes)
    loss_out = red[0].reshape(())
    g_small = dict(zip(SMALL_NAMES, red[1:]))

    grads, deltas, new_m, new_v = {}, {}, {}, {}
    for name in BIG_NAMES:
        arg, transposed = BIG_ARGS[name]
        g2 = g_big[name].T if transposed else g_big[name]
        d2, m2, v2 = _adamw(weights[arg][0], g2, mom_m[arg][0], mom_v[arg][0], "adamw_" + name)
        grads[arg], deltas[arg], new_m[arg], new_v[arg] = g2[None], d2[None], m2[None], v2[None]
    shapes1 = [(1, weights[n].size) for n in SMALL_NAMES]
    packed = [_pack_small([t[n].reshape(1, -1) for n in SMALL_NAMES]) for t in (weights, g_small, mom_m, mom_v)]
    upd = _adamw(*packed, "adamw_small")
    for tgt_dict, block in zip((deltas, new_m, new_v), upd):
        for n, val in zip(SMALL_NAMES, _unpack_small(block, shapes1)):
            tgt_dict[n] = val.reshape(weights[n].shape)
    for n in SMALL_NAMES:
        grads[n] = g_small[n].reshape(weights[n].shape)

    return (loss_out, gx.reshape(B, S, D), *[grads[n] for n in WEIGHT_NAMES], *[deltas[n] for n in WEIGHT_NAMES],
            *[new_m[n] for n in WEIGHT_NAMES], *[new_v[n] for n in WEIGHT_NAMES])
```

```python
import functools

import jax
import jax.numpy as jnp
from jax import lax
from jax.experimental import pallas as pl
from jax.experimental.pallas import tpu as pltpu

F32 = jnp.float32
_MXU_DTYPE = jnp.bfloat16
_WIRE_DTYPE = jnp.bfloat16

EPS = 1e-6
HEAD_DIM = 64
N_SWA_HEADS = 16
N_SWA_KV = 4
N_SB_HEADS = 16
WINDOW = 128
SWA_Q = N_SWA_HEADS * HEAD_DIM
SWA_KV = N_SWA_KV * HEAD_DIM
SB_W = N_SB_HEADS * HEAD_DIM
IN_W = SWA_Q + 2 * SWA_KV + 3 * SB_W
LANES = 128
ATT_SCALE = HEAD_DIM ** -0.5

ADAM_LR = 0.001
ADAM_B1 = 0.9
ADAM_B2 = 0.999
ADAM_EPS = 1e-08
ADAM_WD = 0.01
ADAM_STEP = 10

N_DEV = 8
_VMEM_LIMIT_BYTES = 48 * 1024 * 1024
_F_TILE = 512


def _params(*semantics):
    return pltpu.CompilerParams(dimension_semantics=semantics, vmem_limit_bytes=_VMEM_LIMIT_BYTES)


def _tile(n, pref, align):
    t = min(n, pref)
    t -= t % align
    while t >= align:
        if n % t == 0:
            return t
        t -= align
    return n


def _dot(a, b):
    return lax.dot_general(a, b, (((1,), (0,)), ((), ())), preferred_element_type=F32)


def _dot_nt(a, b):
    return lax.dot_general(a, b, (((1,), (1,)), ((), ())), preferred_element_type=F32)


def _dot_tn(a, b):
    return lax.dot_general(a, b, (((0,), (0,)), ((), ())), preferred_element_type=F32)


class _Job:
    def __init__(self, ins, out_shape, scratch, start, finish, mid=None):
        self.ins, self.out_shape, self.scratch = list(ins), list(out_shape), list(scratch)
        self.start, self.mid, self.finish = start, mid, finish


_JOB_MID_FRACTION = 0.8


def _call(body, *, name, grid, in_specs, out_specs, out_shape, args, semantics, scratch_shapes=(), jobs=()):
    single = not isinstance(out_shape, (tuple, list))
    if not jobs:
        res = pl.pallas_call(body, name=name, grid=grid, in_specs=list(in_specs), out_specs=out_specs,
                             out_shape=out_shape, scratch_shapes=list(scratch_shapes),
                             compiler_params=_params(*semantics))(*args)
        return res, []
    base_out = [out_shape] if single else list(out_shape)
    base_out_specs = [out_specs] if single else list(out_specs)
    n_in, n_out, n_scr = len(args), len(base_out), len(scratch_shapes)
    any_spec = pl.BlockSpec(memory_space=pl.ANY)
    total = 1
    for g in grid:
        total *= g
    mid_step = min(total - 1, int(total * _JOB_MID_FRACTION))

    def wrapped(*refs):
        pos = n_in
        job_ins = []
        for job in jobs:
            job_ins.append(refs[pos:pos + len(job.ins)])
            pos += len(job.ins)
        outs = refs[pos:pos + n_out]
        pos += n_out
        job_outs = []
        for job in jobs:
            job_outs.append(refs[pos:pos + len(job.out_shape)])
            pos += len(job.out_shape)
        scr = refs[pos:pos + n_scr]
        pos += n_scr
        job_scr = []
        for job in jobs:
            job_scr.append(refs[pos:pos + len(job.scratch)])
            pos += len(job.scratch)
        step = pl.program_id(0)
        for d in range(1, len(grid)):
            step = step * grid[d] + pl.program_id(d)

        @pl.when(step == 0)
        def _():
            for job, ji, jo, js in zip(jobs, job_ins, job_outs, job_scr):
                job.start(ji, jo, js)

        @pl.when(step == mid_step)
        def _():
            for job, ji, jo, js in zip(jobs, job_ins, job_outs, job_scr):
                if job.mid is not None:
                    job.mid(ji, jo, js)

        body(*refs[:n_in], *outs, *scr)

        @pl.when(step == total - 1)
        def _():
            for job, ji, jo, js in zip(jobs, job_ins, job_outs, job_scr):
                job.finish(ji, jo, js)

    all_args = list(args) + [a for job in jobs for a in job.ins]
    all_in_specs = list(in_specs) + [any_spec for job in jobs for _ in job.ins]
    all_out_shape = base_out + [s for job in jobs for s in job.out_shape]
    all_out_specs = base_out_specs + [any_spec for job in jobs for _ in job.out_shape]
    all_scratch = list(scratch_shapes) + [s for job in jobs for s in job.scratch]
    res = pl.pallas_call(
        wrapped, name=name, grid=grid, in_specs=all_in_specs, out_specs=tuple(all_out_specs),
        out_shape=tuple(all_out_shape), scratch_shapes=all_scratch,
        compiler_params=pltpu.CompilerParams(dimension_semantics=("arbitrary",) * len(grid),
                                             vmem_limit_bytes=_VMEM_LIMIT_BYTES, has_side_effects=True),
    )(*all_args)
    base = res[0] if single else tuple(res[:n_out])
    job_res, pos = [], n_out
    for job in jobs:
        job_res.append(tuple(res[pos:pos + len(job.out_shape)]))
        pos += len(job.out_shape)
    return base, job_res


def _rms_fwd(x, g, name, jobs=()):
    T, D = x.shape
    tm = _tile(T, 512, 16)

    def body(x_ref, g_ref, o_ref):
        xv = x_ref[...]
        r = lax.rsqrt(jnp.mean(xv * xv, axis=-1, keepdims=True) + EPS)
        o_ref[...] = (xv * r * g_ref[...]).astype(o_ref.dtype)

    return _call(
        body, name=name, grid=(T // tm,),
        out_shape=jax.ShapeDtypeStruct((T, D), _MXU_DTYPE),
        in_specs=[pl.BlockSpec((tm, D), lambda i: (i, 0)), pl.BlockSpec((1, D), lambda i: (0, 0))],
        out_specs=pl.BlockSpec((tm, D), lambda i: (i, 0)), args=(x, g), semantics=("parallel",), jobs=jobs)


def _rms_bwd_rows(dh, xv, g):
    r = lax.rsqrt(jnp.mean(xv * xv, axis=-1, keepdims=True) + EPS)
    xhat = xv * r
    u = dh * g
    dx = r * (u - xhat * jnp.mean(u * xhat, axis=-1, keepdims=True))
    return dx, dh * xhat


def _rms_bwd(dh, x, g, dres, name):
    T, D = x.shape
    tm = _tile(T, 256, 16)

    def body(dh_ref, x_ref, g_ref, dres_ref, dx_ref, dxb_ref, dg_ref):
        @pl.when(pl.program_id(0) == 0)
        def _():
            dg_ref[...] = jnp.zeros_like(dg_ref)

        dx, dgr = _rms_bwd_rows(dh_ref[...], x_ref[...], g_ref[...])
        dx = dres_ref[...] + dx
        dx_ref[...] = dx
        dxb_ref[...] = dx.astype(dxb_ref.dtype)
        dg_ref[...] += jnp.sum(dgr, axis=0, keepdims=True)

    row = pl.BlockSpec((tm, D), lambda i: (i, 0))
    vec = pl.BlockSpec((1, D), lambda i: (0, 0))
    return pl.pallas_call(
        body, name=name, grid=(T // tm,),
        out_shape=(jax.ShapeDtypeStruct((T, D), F32), jax.ShapeDtypeStruct((T, D), _MXU_DTYPE),
                   jax.ShapeDtypeStruct((1, D), F32)),
        in_specs=[row, row, vec, row], out_specs=(row, row, vec),
        compiler_params=_params("arbitrary"),
    )(dh, x, g, dres)


def _loss_head(x, g, tgt, name):
    T, D = x.shape
    tm = _tile(T, 256, 16)

    def body(x_ref, g_ref, t_ref, dx_ref, dxb_ref, dg_ref, loss_ref):
        @pl.when(pl.program_id(0) == 0)
        def _():
            dg_ref[...] = jnp.zeros_like(dg_ref)
            loss_ref[...] = jnp.zeros_like(loss_ref)

        xv = x_ref[...]
        gv = g_ref[...]
        r = lax.rsqrt(jnp.mean(xv * xv, axis=-1, keepdims=True) + EPS)
        xhat = xv * r
        diff = xhat * gv - t_ref[...]
        tok = jnp.mean(diff * diff, axis=-1, keepdims=True)
        loss_ref[...] += 0.5 * jnp.sum(tok, axis=0, keepdims=True)
        dy = diff / D
        u = dy * gv
        dx = r * (u - xhat * jnp.mean(u * xhat, axis=-1, keepdims=True))
        dx_ref[...] = dx
        dxb_ref[...] = dx.astype(dxb_ref.dtype)
        dg_ref[...] += jnp.sum(dy * xhat, axis=0, keepdims=True)

    row = pl.BlockSpec((tm, D), lambda i: (i, 0))
    vec = pl.BlockSpec((1, D), lambda i: (0, 0))
    return pl.pallas_call(
        body, name=name, grid=(T // tm,),
        out_shape=(jax.ShapeDtypeStruct((T, D), F32), jax.ShapeDtypeStruct((T, D), _MXU_DTYPE),
                   jax.ShapeDtypeStruct((1, D), F32), jax.ShapeDtypeStruct((1, LANES), F32)),
        in_specs=[row, vec, row],
        out_specs=(row, row, vec, pl.BlockSpec((1, LANES), lambda i: (0, 0))),
        compiler_params=_params("arbitrary"),
    )(x, g, tgt)


def _outnorm_fwd(ya, yb, ga, gb, name):
    T, W = ya.shape
    tm = _tile(T, 512, 16)

    def body(ya_ref, yb_ref, ga_ref, gb_ref, o_ref):
        for k, (y_ref, g_ref) in enumerate(((ya_ref, ga_ref), (yb_ref, gb_ref))):
            yv = y_ref[...]
            r = lax.rsqrt(jnp.mean(yv * yv, axis=-1, keepdims=True) + EPS)
            o_ref[:, k * W:(k + 1) * W] = (yv * r * g_ref[...]).astype(o_ref.dtype)

    row = pl.BlockSpec((tm, W), lambda i: (i, 0))
    vec = pl.BlockSpec((1, W), lambda i: (0, 0))
    return pl.pallas_call(
        body, name=name, grid=(T // tm,),
        out_shape=jax.ShapeDtypeStruct((T, 2 * W), _MXU_DTYPE),
        in_specs=[row, row, vec, vec], out_specs=pl.BlockSpec((tm, 2 * W), lambda i: (i, 0)),
        compiler_params=_params("parallel"),
    )(ya, yb, ga, gb)


def _outnorm_bwd(dyn, ya, yb, ga, gb, name):
    T, W = ya.shape
    tm = _tile(T, 256, 16)

    def body(d_ref, ya_ref, yb_ref, ga_ref, gb_ref, dya_ref, dyb_ref, dga_ref, dgb_ref):
        @pl.when(pl.program_id(0) == 0)
        def _():
            dga_ref[...] = jnp.zeros_like(dga_ref)
            dgb_ref[...] = jnp.zeros_like(dgb_ref)

        for k, (y_ref, g_ref, dy_ref, dg_ref) in enumerate(
                ((ya_ref, ga_ref, dya_ref, dga_ref), (yb_ref, gb_ref, dyb_ref, dgb_ref))):
            dy, dgr = _rms_bwd_rows(d_ref[:, k * W:(k + 1) * W], y_ref[...], g_ref[...])
            dy_ref[...] = dy.astype(dy_ref.dtype)
            dg_ref[...] += jnp.sum(dgr, axis=0, keepdims=True)

    row = pl.BlockSpec((tm, W), lambda i: (i, 0))
    vec = pl.BlockSpec((1, W), lambda i: (0, 0))
    return pl.pallas_call(
        body, name=name, grid=(T // tm,),
        out_shape=(jax.ShapeDtypeStruct((T, W), _MXU_DTYPE), jax.ShapeDtypeStruct((T, W), _MXU_DTYPE),
                   jax.ShapeDtypeStruct((1, W), F32), jax.ShapeDtypeStruct((1, W), F32)),
        in_specs=[pl.BlockSpec((tm, 2 * W), lambda i: (i, 0)), row, row, vec, vec],
        out_specs=(row, row, vec, vec),
        compiler_params=_params("arbitrary"),
    )(dyn, ya, yb, ga, gb)


def _ffn_gu(h, wg_t, wu_t, name, jobs=()):
    T, D = h.shape
    Fp = wg_t.shape[0]
    tm = _tile(T, 1024, 16)
    tn = _tile(Fp, _F_TILE, LANES)

    def body(h_ref, wg_ref, wu_ref, g_ref, u_ref, a_ref):
        hv = h_ref[...]
        g = _dot_nt(hv, wg_ref[...])
        u = _dot_nt(hv, wu_ref[...])
        g_ref[...] = g.astype(g_ref.dtype)
        u_ref[...] = u.astype(u_ref.dtype)
        a_ref[...] = (g * jax.nn.sigmoid(g) * u).astype(a_ref.dtype)

    act = pl.BlockSpec((tm, tn), lambda n, m: (m, n))
    wsp = pl.BlockSpec((tn, D), lambda n, m: (n, 0))
    out = jax.ShapeDtypeStruct((T, Fp), _MXU_DTYPE)
    return _call(
        body, name=name, grid=(Fp // tn, T // tm), out_shape=(out, out, out),
        in_specs=[pl.BlockSpec((tm, D), lambda n, m: (m, 0)), wsp, wsp],
        out_specs=(act, act, act), args=(h, wg_t, wu_t), semantics=("parallel", "parallel"), jobs=jobs)


def _ffn_bwd_act(dxb, wd, G, U, name, jobs=()):
    T, D = dxb.shape
    Fp = wd.shape[0]
    tm = _tile(T, 1024, 16)
    tn = _tile(Fp, _F_TILE, LANES)

    def body(e_ref, wd_ref, g_ref, u_ref, dg_ref, du_ref):
        da = 0.5 * _dot_nt(e_ref[...], wd_ref[...])
        g = g_ref[...].astype(F32)
        u = u_ref[...].astype(F32)
        s = jax.nn.sigmoid(g)
        du_ref[...] = (da * (g * s)).astype(du_ref.dtype)
        dg_ref[...] = (da * u * (s * (1.0 + g * (1.0 - s)))).astype(dg_ref.dtype)

    act = pl.BlockSpec((tm, tn), lambda n, m: (m, n))
    out = jax.ShapeDtypeStruct((T, Fp), _MXU_DTYPE)
    return _call(
        body, name=name, grid=(Fp // tn, T // tm), out_shape=(out, out),
        in_specs=[pl.BlockSpec((tm, D), lambda n, m: (m, 0)), pl.BlockSpec((tn, D), lambda n, m: (n, 0)),
                  act, act],
        out_specs=(act, act), args=(dxb, wd, G, U), semantics=("parallel", "parallel"), jobs=jobs)


def _ffn_gate(h, wg_t, U, name, jobs=()):
    T, D = h.shape
    Fp = wg_t.shape[0]
    tm = _tile(T, 1024, 16)
    tn = _tile(Fp, _F_TILE, LANES)

    def body(h_ref, wg_ref, u_ref, g_ref, a_ref):
        g = _dot_nt(h_ref[...], wg_ref[...])
        g_ref[...] = g.astype(g_ref.dtype)
        a_ref[...] = (g * jax.nn.sigmoid(g) * u_ref[...].astype(F32)).astype(a_ref.dtype)

    act = pl.BlockSpec((tm, tn), lambda n, m: (m, n))
    out = jax.ShapeDtypeStruct((T, Fp), _MXU_DTYPE)
    return _call(
        body, name=name, grid=(Fp // tn, T // tm), out_shape=(out, out),
        in_specs=[pl.BlockSpec((tm, D), lambda n, m: (m, 0)), pl.BlockSpec((tn, D), lambda n, m: (n, 0)), act],
        out_specs=(act, act), args=(h, wg_t, U), semantics=("parallel", "parallel"), jobs=jobs)


def _mm_nt(a, b, out_dtype, name, jobs=()):
    M, K = a.shape
    N = b.shape[0]
    tm = _tile(M, 1024, 16)
    tn = _tile(N, 512, LANES)

    def body(a_ref, b_ref, o_ref):
        o_ref[...] = _dot_nt(a_ref[...], b_ref[...]).astype(o_ref.dtype)

    return _call(
        body, name=name, grid=(N // tn, M // tm), out_shape=jax.ShapeDtypeStruct((M, N), out_dtype),
        in_specs=[pl.BlockSpec((tm, K), lambda n, m: (m, 0)), pl.BlockSpec((tn, K), lambda n, m: (n, 0))],
        out_specs=pl.BlockSpec((tm, tn), lambda n, m: (m, n)), args=(a, b),
        semantics=("parallel", "parallel"), jobs=jobs)


_MM_OPERAND_BYTES = 26 * 1024 * 1024


def _k_tile(K, bytes_per_k, align):
    best = align
    for t in range(align, K + 1, align):
        if K % t == 0 and 2 * t * bytes_per_k <= _MM_OPERAND_BYTES:
            best = t
    return best


def _mm_nn(pairs, res, alpha, out_dtype, name, jobs=()):
    M, K = pairs[0][0].shape
    N = pairs[0][1].shape[1]
    n_pairs = len(pairs)
    tm = _tile(M, 512, 16)
    tn = _tile(N, 512, LANES)
    tk = _k_tile(K, n_pairs * (tm + tn) * pairs[0][0].dtype.itemsize, LANES)
    nk = K // tk

    def body(*refs):
        ab = refs[:2 * n_pairs]
        res_ref = refs[2 * n_pairs] if res is not None else None
        o_ref = refs[2 * n_pairs + (res is not None)]

        def finish(acc):
            out = alpha * acc
            if res_ref is not None:
                out = res_ref[...] + out
            o_ref[...] = out.astype(o_ref.dtype)

        part = _dot(ab[0][...], ab[1][...])
        for i in range(1, n_pairs):
            part = part + _dot(ab[2 * i][...], ab[2 * i + 1][...])
        if nk == 1:
            finish(part)
        else:
            acc_ref = refs[-1]
            k = pl.program_id(2)

            @pl.when(k == 0)
            def _():
                acc_ref[...] = part

            @pl.when(k > 0)
            def _():
                acc_ref[...] += part

            @pl.when(k == nk - 1)
            def _():
                finish(acc_ref[...])

    in_specs, args = [], []
    for a, b in pairs:
        in_specs += [pl.BlockSpec((tm, tk), lambda m, n, k: (m, k)), pl.BlockSpec((tk, tn), lambda m, n, k: (k, n))]
        args += [a, b]
    if res is not None:
        in_specs.append(pl.BlockSpec((tm, tn), lambda m, n, k: (m, n)))
        args.append(res)
    return _call(
        body, name=name, grid=(M // tm, N // tn, nk), out_shape=jax.ShapeDtypeStruct((M, N), out_dtype),
        in_specs=in_specs, out_specs=pl.BlockSpec((tm, tn), lambda m, n, k: (m, n)),
        scratch_shapes=[pltpu.VMEM((tm, tn), F32)] if nk > 1 else [], args=args,
        semantics=("parallel", "parallel", "arbitrary"), jobs=jobs)


def _mm_tn(a, b, alpha, out_dtype, name, jobs=()):
    K, M = a.shape
    N = b.shape[1]
    tm = _tile(M, 512, LANES)
    tn = _tile(N, 512, LANES)

    def body(a_ref, b_ref, o_ref):
        o_ref[...] = (alpha * _dot_tn(a_ref[...], b_ref[...])).astype(o_ref.dtype)

    return _call(
        body, name=name, grid=(M // tm, N // tn), out_shape=jax.ShapeDtypeStruct((M, N), out_dtype),
        in_specs=[pl.BlockSpec((K, tm), lambda m, n: (0, m)), pl.BlockSpec((K, tn), lambda m, n: (0, n))],
        out_specs=pl.BlockSpec((tm, tn), lambda m, n: (m, n)), args=(a, b),
        semantics=("parallel", "parallel"), jobs=jobs)


def _half_masks():
    lane = lax.broadcasted_iota(jnp.int32, (1, LANES), 1)
    return (lane < HEAD_DIM, lane >= HEAD_DIM)


def _swap_halves(v):
    return pltpu.roll(v.astype(F32), HEAD_DIM, 1).astype(v.dtype)


def _swa_geometry(n):
    qi = lax.broadcasted_iota(jnp.int32, (WINDOW, 2 * WINDOW), 0)
    kp = lax.broadcasted_iota(jnp.int32, (WINDOW, 2 * WINDOW), 1)
    dist = (WINDOW + qi) - kp
    valid = (dist >= 0) & (dist < WINDOW) & ((n > 0) | (kp >= WINDOW))
    return dist.astype(F32), valid


def _swa_slope(h):
    return 2.0 ** (-8.0 * (h + 1) / N_SWA_HEADS)


def _swa_probs(qm, kx, sink, slope, distf, valid):
    s = _dot_nt(qm, kx) * ATT_SCALE - slope * distf
    s = jnp.where(valid, s, -1e30)
    m = jnp.maximum(jnp.max(s, axis=1, keepdims=True), sink)
    p = jnp.exp(s - m)
    e_sink = jnp.exp(sink - m)
    den = jnp.sum(p, axis=1, keepdims=True) + e_sink
    return p / den, e_sink / den


def _swa_specs(B, S):
    nb = S // WINDOW
    kcol = SWA_Q // SWA_KV
    cur = lambda b, n: (b * nb + n, kcol)
    prev = lambda b, n: (b * nb + jnp.maximum(n - 1, 0), kcol)
    curv = lambda b, n: (b * nb + n, kcol + 1)
    prevv = lambda b, n: (b * nb + jnp.maximum(n - 1, 0), kcol + 1)
    q_spec = pl.BlockSpec((WINDOW, SWA_Q), lambda b, n: (b * nb + n, 0))
    kv = [pl.BlockSpec((WINDOW, SWA_KV), f) for f in (prev, cur, prevv, curv)]
    sink_spec = pl.BlockSpec(memory_space=pltpu.SMEM)
    return nb, q_spec, kv, sink_spec


def _swa_kv_views(kp_ref, kc_ref, vp_ref, vc_ref, g):
    hm = _half_masks()
    c0 = (g // 2) * LANES
    k_all = jnp.concatenate([kp_ref[:, c0:c0 + LANES], kc_ref[:, c0:c0 + LANES]], axis=0)
    v_all = jnp.concatenate([vp_ref[:, c0:c0 + LANES], vc_ref[:, c0:c0 + LANES]], axis=0)
    b = g % 2
    ks, vs = [None, None], [None, None]
    ks[b], vs[b] = k_all, v_all
    ks[1 - b], vs[1 - b] = _swap_halves(k_all), _swap_halves(v_all)
    ks = [jnp.where(hm[a], ks[a], 0) for a in range(2)]
    vs = [jnp.where(hm[a], vs[a], 0) for a in range(2)]
    return ks, vs


def _swa_fwd(proj, sinks, B, S, name):
    T = B * S
    nb, q_spec, kv_specs, sink_spec = _swa_specs(B, S)

    def body(sink_ref, q_ref, kp_ref, kc_ref, vp_ref, vc_ref, y_ref):
        hm = _half_masks()
        distf, valid = _swa_geometry(pl.program_id(1))
        for g in range(N_SWA_KV):
            ks, vs = _swa_kv_views(kp_ref, kc_ref, vp_ref, vc_ref, g)
            for pp in (2 * g, 2 * g + 1):
                q_pair = q_ref[:, pp * LANES:(pp + 1) * LANES]
                o_pair = jnp.zeros((WINDOW, LANES), F32)
                for a in range(2):
                    h = 2 * pp + a
                    qm = jnp.where(hm[a], q_pair, 0)
                    p, _ = _swa_probs(qm, ks[a], sink_ref[0, h], _swa_slope(h), distf, valid)
                    o_pair = o_pair + _dot(p.astype(_MXU_DTYPE), vs[a])
                y_ref[:, pp * LANES:(pp + 1) * LANES] = o_pair

    return pl.pallas_call(
        body, name=name, grid=(B, nb), out_shape=jax.ShapeDtypeStruct((T, SWA_Q), F32),
        in_specs=[sink_spec, q_spec] + kv_specs,
        out_specs=pl.BlockSpec((WINDOW, SWA_Q), lambda b, n: (b * nb + n, 0)),
        compiler_params=_params("parallel", "parallel"),
    )(sinks, proj, proj, proj, proj, proj)


def _swa_bwd(proj, sinks, dya, B, S, name):
    T = B * S
    nb, q_spec, kv_specs, sink_spec = _swa_specs(B, S)

    def body(sink_ref, q_ref, kp_ref, kc_ref, vp_ref, vc_ref, do_ref,
             dq_ref, dk_ref, dv_ref, dsink_ref, dk_acc, dv_acc):
        b_id, n = pl.program_id(0), pl.program_id(1)
        hm = _half_masks()
        lane = lax.broadcasted_iota(jnp.int32, (1, LANES), 1)

        @pl.when((b_id == 0) & (n == 0))
        def _():
            dsink_ref[...] = jnp.zeros_like(dsink_ref)

        @pl.when(n == 0)
        def _():
            dk_acc[...] = jnp.zeros_like(dk_acc)
            dv_acc[...] = jnp.zeros_like(dv_acc)

        distf, valid = _swa_geometry(n)
        r_prev = pl.multiple_of(jnp.maximum(n - 1, 0) * WINDOW, WINDOW)
        r_cur = pl.multiple_of(n * WINDOW, WINDOW)
        dsink = jnp.zeros((1, LANES), F32)
        for g in range(N_SWA_KV):
            ks, vs = _swa_kv_views(kp_ref, kc_ref, vp_ref, vc_ref, g)
            dk_g = [jnp.zeros((2 * WINDOW, LANES), F32) for _ in range(2)]
            dv_g = [jnp.zeros((2 * WINDOW, LANES), F32) for _ in range(2)]
            for pp in (2 * g, 2 * g + 1):
                q_pair = q_ref[:, pp * LANES:(pp + 1) * LANES]
                do_pair = do_ref[:, pp * LANES:(pp + 1) * LANES]
                dq_pair = jnp.zeros((WINDOW, LANES), F32)
                for a in range(2):
                    h = 2 * pp + a
                    qm = jnp.where(hm[a], q_pair, 0)
                    dom = jnp.where(hm[a], do_pair, 0)
                    p, p_sink = _swa_probs(qm, ks[a], sink_ref[0, h], _swa_slope(h), distf, valid)
                    dp = _dot_nt(dom, vs[a])
                    delta = jnp.sum(p * dp, axis=1, keepdims=True)
                    ds = (p * (dp - delta) * ATT_SCALE).astype(_MXU_DTYPE)
                    dsink = dsink + jnp.where(lane == h, -jnp.sum(p_sink * delta), 0.0)
                    dq_pair = dq_pair + _dot(ds, ks[a])
                    dk_g[a] = dk_g[a] + _dot_tn(ds, qm)
                    dv_g[a] = dv_g[a] + _dot_tn(p.astype(_MXU_DTYPE), dom)
                dq_ref[:, pp * LANES:(pp + 1) * LANES] = dq_pair.astype(dq_ref.dtype)
            bsel = g % 2
            dk_t = dk_g[bsel] + pltpu.roll(dk_g[1 - bsel], HEAD_DIM, 1)
            dv_t = dv_g[bsel] + pltpu.roll(dv_g[1 - bsel], HEAD_DIM, 1)
            c0 = (g // 2) * LANES
            dk_acc[pl.ds(r_prev, WINDOW), c0:c0 + LANES] += dk_t[:WINDOW]
            dk_acc[pl.ds(r_cur, WINDOW), c0:c0 + LANES] += dk_t[WINDOW:]
            dv_acc[pl.ds(r_prev, WINDOW), c0:c0 + LANES] += dv_t[:WINDOW]
            dv_acc[pl.ds(r_cur, WINDOW), c0:c0 + LANES] += dv_t[WINDOW:]
        dsink_ref[...] += dsink

        @pl.when(n == nb - 1)
        def _():
            dk_ref[...] = dk_acc[...].astype(dk_ref.dtype)
            dv_ref[...] = dv_acc[...].astype(dv_ref.dtype)

    seq_kv = pl.BlockSpec((S, SWA_KV), lambda b, n: (b, 0))
    return pl.pallas_call(
        body, name=name, grid=(B, nb),
        out_shape=(jax.ShapeDtypeStruct((T, SWA_Q), _MXU_DTYPE), jax.ShapeDtypeStruct((T, SWA_KV), _MXU_DTYPE),
                   jax.ShapeDtypeStruct((T, SWA_KV), _MXU_DTYPE), jax.ShapeDtypeStruct((1, LANES), F32)),
        in_specs=[sink_spec, q_spec] + kv_specs + [pl.BlockSpec((WINDOW, SWA_Q), lambda b, n: (b * nb + n, 0))],
        out_specs=(pl.BlockSpec((WINDOW, SWA_Q), lambda b, n: (b * nb + n, 0)), seq_kv, seq_kv,
                   pl.BlockSpec((1, LANES), lambda b, n: (0, 0))),
        scratch_shapes=[pltpu.VMEM((S, SWA_KV), F32), pltpu.VMEM((S, SWA_KV), F32)],
        compiler_params=_params("arbitrary", "arbitrary"),
    )(sinks, proj, proj, proj, proj, proj, dya)


SB_TILE = 256
SB_HALF = 128


def _tri2(cond):
    j = lax.broadcasted_iota(jnp.int32, (2 * SB_HALF, SB_HALF), 0) & (SB_HALF - 1)
    s = lax.broadcasted_iota(jnp.int32, (2 * SB_HALF, SB_HALF), 1)
    return cond(j, s).astype(_MXU_DTYPE)


def _half_cumsums(x, tri2):
    out = []
    for h in range(2):
        xh = x[:, h * SB_HALF:(h + 1) * SB_HALF]
        hi = xh.astype(_MXU_DTYPE)
        lo = (xh - hi.astype(F32)).astype(_MXU_DTYPE)
        out.append(_dot(jnp.concatenate([hi, lo], axis=1), tri2))
    return out


def _log_sigmoid(z):
    return jnp.minimum(z, 0.0) - jnp.log(1.0 + jnp.exp(-jnp.abs(z)))


def _sb_specs(B, S):
    qb = (SWA_Q + 2 * SWA_KV) // LANES
    kb = qb + SB_W // LANES
    vb = kb + SB_W // LANES
    return [pl.BlockSpec((S, LANES), functools.partial(lambda b, p, c: (b, c + p), c=c)) for c in (qb, kb, vb)]


def _sb_fwd(proj, B, S, name, jobs=()):
    T = B * S
    tq = SB_TILE
    nq = S // tq

    def body(q_ref, k_ref, v_ref, y_ref, tot_ref):
        hm = _half_masks()
        ji = lax.broadcasted_iota(jnp.int32, (tq, tq), 0)
        si = lax.broadcasted_iota(jnp.int32, (tq, tq), 1)
        tri_after = _tri2(lambda j, s: j > s)
        causal = si < ji

        def q_loop(qi, carry):
            r0 = pl.multiple_of(qi * tq, tq)
            q_pair = q_ref[pl.ds(r0, tq), :] * ATT_SCALE
            qms = [jnp.where(hm[a], q_pair, 0) for a in range(2)]

            def tile(c0, state, diagonal):
                kk = k_ref[pl.ds(c0, tq), :]
                vv = v_ref[pl.ds(c0, tq), :]
                two = range(2)
                z = [_dot_nt(qms[a], kk) for a in two]
                lb = [_log_sigmoid(z[a]) for a in two]
                l1m = [jnp.where(causal, lb[a] - z[a], 0.0) if diagonal else lb[a] - z[a] for a in two]
                cum = [_half_cumsums(l1m[a], tri_after) for a in two]
                tot = [[cum[a][h][:, 0:1] + l1m[a][:, h * SB_HALF:h * SB_HALF + 1] for h in two] for a in two]
                after = [jnp.concatenate([cum[a][0] + (state[a][1] + tot[a][1]), cum[a][1] + state[a][1]], axis=1)
                         for a in two]
                att = [jnp.exp(lb[a] + after[a]) for a in two]
                if diagonal:
                    att = [jnp.where(causal, att[a], 0.0) for a in two]
                acc = [state[a][0] + _dot(att[a].astype(_MXU_DTYPE), jnp.where(hm[a], vv, 0)) for a in two]
                car = [state[a][1] + (tot[a][0] + tot[a][1]) for a in two]
                return tuple((acc[a], car[a]) for a in two)

            zero = (jnp.zeros((tq, LANES), F32), jnp.zeros((tq, 1), F32))
            state = tile(r0, (zero, zero), True)
            state = lax.fori_loop(
                0, qi, lambda it, st: tile(pl.multiple_of((qi - 1 - it) * tq, tq), st, False), state)
            y_ref[pl.ds(r0, tq), :] = state[0][0] + state[1][0]
            tot_ref[pl.ds(r0, tq), :] = jnp.where(hm[0], state[0][1], state[1][1])
            return carry

        lax.fori_loop(0, nq, q_loop, 0)

    out_spec = pl.BlockSpec((S, LANES), lambda b, p: (b, p))
    return _call(
        body, name=name, grid=(B, SB_W // LANES),
        out_shape=(jax.ShapeDtypeStruct((T, SB_W), F32), jax.ShapeDtypeStruct((T, SB_W), F32)),
        in_specs=_sb_specs(B, S), out_specs=(out_spec, out_spec), args=(proj, proj, proj),
        semantics=("parallel", "parallel"), jobs=jobs)


def _sb_bwd(proj, tot, dyb, B, S, name, jobs=()):
    T = B * S
    tq = SB_TILE
    nq = S // tq

    def body(q_ref, k_ref, v_ref, do_ref, tot_ref, dq_ref, dk_ref, dv_ref, dk_acc, dv_acc):
        hm = _half_masks()
        ji = lax.broadcasted_iota(jnp.int32, (tq, tq), 0)
        si = lax.broadcasted_iota(jnp.int32, (tq, tq), 1)
        tri_incl = _tri2(lambda j, s: j <= s)
        tri_excl = _tri2(lambda j, s: j < s)
        causal = si < ji
        dk_acc[...] = jnp.zeros_like(dk_acc)
        dv_acc[...] = jnp.zeros_like(dv_acc)

        def q_loop(qi, carry):
            r0 = pl.multiple_of(qi * tq, tq)
            q_pair = q_ref[pl.ds(r0, tq), :] * ATT_SCALE
            do_pair = do_ref[pl.ds(r0, tq), :]
            tot_pair = tot_ref[pl.ds(r0, tq), :]
            qms = [jnp.where(hm[a], q_pair, 0) for a in range(2)]
            doms = [jnp.where(hm[a], do_pair, 0) for a in range(2)]
            totals = [jnp.max(jnp.where(hm[a], tot_pair, -jnp.inf), axis=1, keepdims=True) for a in range(2)]

            def tile(c0, state, diagonal):
                kk = k_ref[pl.ds(c0, tq), :]
                vv = v_ref[pl.ds(c0, tq), :]
                ks = kk * ATT_SCALE
                two = range(2)
                last = SB_HALF - 1
                z = [_dot_nt(qms[a], kk) for a in two]
                d_att = [_dot_nt(doms[a], vv) for a in two]
                lb = [_log_sigmoid(z[a]) for a in two]
                l1m = [jnp.where(causal, lb[a] - z[a], 0.0) if diagonal else lb[a] - z[a] for a in two]
                cum = [_half_cumsums(l1m[a], tri_incl) for a in two]
                upto = [jnp.concatenate([cum[a][0] + state[a][1],
                                         cum[a][1] + (state[a][1] + cum[a][0][:, last:last + 1])], axis=1) for a in two]
                att = [jnp.exp(lb[a] + (totals[a] - upto[a])) for a in two]
                if diagonal:
                    att = [jnp.where(causal, att[a], 0.0) for a in two]
                d_log = [d_att[a] * att[a] for a in two]
                cumd = [_half_cumsums(d_log[a], tri_excl) for a in two]
                totd = [[cumd[a][h][:, last:last + 1] + d_log[a][:, h * SB_HALF + last:h * SB_HALF + last + 1]
                         for h in two] for a in two]
                before = [jnp.concatenate([cumd[a][0] + state[a][2], cumd[a][1] + (state[a][2] + totd[a][0])], axis=1)
                          for a in two]
                sig = [jnp.exp(lb[a]) for a in two]
                dz = [d_log[a] * (1.0 - sig[a]) - sig[a] * before[a] for a in two]
                if diagonal:
                    dz = [jnp.where(causal, dz[a], 0.0) for a in two]
                dzb = [dz[a].astype(_MXU_DTYPE) for a in two]
                dq = [state[a][0] + _dot(dzb[a], jnp.where(hm[a], ks, 0)) for a in two]
                dk_acc[pl.ds(c0, tq), :] += _dot_tn(dzb[0], qms[0]) + _dot_tn(dzb[1], qms[1])
                dv_acc[pl.ds(c0, tq), :] += (_dot_tn(att[0].astype(_MXU_DTYPE), doms[0])
                                             + _dot_tn(att[1].astype(_MXU_DTYPE), doms[1]))
                cp = [upto[a][:, tq - 1:tq] for a in two]
                cq = [state[a][2] + (totd[a][0] + totd[a][1]) for a in two]
                return tuple((dq[a], cp[a], cq[a]) for a in two)

            zero_col = jnp.zeros((tq, 1), F32)
            zero = (jnp.zeros((tq, LANES), F32), zero_col, zero_col)
            state = lax.fori_loop(0, qi, lambda kj, st: tile(pl.multiple_of(kj * tq, tq), st, False), (zero, zero))
            state = tile(r0, state, True)
            dq_ref[pl.ds(r0, tq), :] = (state[0][0] + state[1][0]).astype(dq_ref.dtype)
            return carry

        lax.fori_loop(0, nq, q_loop, 0)
        dk_ref[...] = dk_acc[...].astype(dk_ref.dtype)
        dv_ref[...] = dv_acc[...].astype(dv_ref.dtype)

    pair = pl.BlockSpec((S, LANES), lambda b, p: (b, p))
    out = jax.ShapeDtypeStruct((T, SB_W), _MXU_DTYPE)
    return _call(
        body, name=name, grid=(B, SB_W // LANES), out_shape=(out, out, out),
        in_specs=_sb_specs(B, S) + [pair, pair], out_specs=(pair, pair, pair),
        scratch_shapes=[pltpu.VMEM((S, LANES), F32), pltpu.VMEM((S, LANES), F32)],
        args=(proj, proj, proj, dyb, tot), semantics=("parallel", "parallel"), jobs=jobs)


def _layer_step(x, tgt, B, S, small, comm):
    run, big, part = comm.run, comm.big, comm.partial
    ffn1_w, ffn2_w = ("ffn1_down", "ffn1_gate", "ffn1_up"), ("ffn2_down", "ffn2_gate", "ffn2_up")

    h1 = run(_rms_fwd, x, small["ffn1_norm"], "ffn1_rms", ag=("ffn1_up",))
    U1 = run(_mm_nt, h1, big["ffn1_up"], _MXU_DTYPE, "ffn1_up", ag=("ffn1_gate",))
    G1, A1 = run(_ffn_gate, h1, big["ffn1_gate"], U1, "ffn1_gate", ag=("ffn1_down",))
    x1 = run(_mm_nn, [(A1, big["ffn1_down"])], x, 0.5, F32, "ffn1_down", ag=("w_in", "w_out"))
    h2 = run(_rms_fwd, x1, small["mix_norm"], "mix_rms")
    proj = run(_mm_nt, h2, big["w_in"], _MXU_DTYPE, "in_proj")
    ya = _swa_fwd(proj, small["swa_sinks"], B, S, "swa_fwd")
    yb, tot = run(_sb_fwd, proj, B, S, "sb_fwd", ag=("ffn2_gate", "ffn2_up", "ffn2_down"))
    yn = _outnorm_fwd(ya, yb, small["swa_out_norm"], small["sb_out_norm"], "out_norm")
    x2 = run(_mm_nn, [(yn, big["w_out"])], x1, 1.0, F32, "out_proj")
    h3 = run(_rms_fwd, x2, small["ffn2_norm"], "ffn2_rms")
    G2, U2, A2 = run(_ffn_gu, h3, big["ffn2_gate"], big["ffn2_up"], "ffn2_gate_up")
    x3 = run(_mm_nn, [(A2, big["ffn2_down"])], x2, 0.5, F32, "ffn2_down")

    dx3, dx3b, d_final, loss = _loss_head(x3, small["final_norm"], tgt, "loss_head")

    dG2, dU2 = run(_ffn_bwd_act, dx3b, big["ffn2_down"], G2, U2, "ffn2_bwd_act")
    part["ffn2_down"] = run(_mm_tn, A2, dx3b, 0.5, _WIRE_DTYPE, "ffn2_dw_down")
    part["ffn2_gate"] = run(_mm_tn, dG2, h3, 1.0, _WIRE_DTYPE, "ffn2_dw_gate")
    part["ffn2_up"] = run(_mm_tn, dU2, h3, 1.0, _WIRE_DTYPE, "ffn2_dw_up")
    dh3 = run(_mm_nn, [(dG2, big["ffn2_gate"]), (dU2, big["ffn2_up"])], None, 1.0, F32, "ffn2_dh", rs1=ffn2_w)
    dx2, dx2b, d_g2 = _rms_bwd(dh3, x2, small["ffn2_norm"], dx3, "ffn2_rms_bwd")

    part["w_out"] = run(_mm_tn, yn, dx2b, 1.0, _WIRE_DTYPE, "dw_out")
    dyn = run(_mm_nt, dx2b, big["w_out"], F32, "out_proj_bwd")
    dya, dyb, d_ga, d_gb = _outnorm_bwd(dyn, ya, yb, small["swa_out_norm"], small["sb_out_norm"], "out_norm_bwd")
    dqa, dka, dva, d_sinks = _swa_bwd(proj, small["swa_sinks"], dya, B, S, "swa_bwd")
    dqb, dkb, dvb = run(_sb_bwd, proj, tot, dyb, B, S, "sb_bwd", rs2=ffn2_w)
    dproj = jnp.concatenate([dqa, dka, dva, dqb, dkb, dvb], axis=1)
    part["w_in"] = run(_mm_tn, dproj, h2, 1.0, _WIRE_DTYPE, "dw_in")
    dh2 = run(_mm_nn, [(dproj, big["w_in"])], None, 1.0, F32, "in_proj_bwd", rs1=("w_in", "w_out"))
    dx1, dx1b, d_gm = _rms_bwd(dh2, x1, small["mix_norm"], dx2, "mix_rms_bwd")

    dG1, dU1 = run(_ffn_bwd_act, dx1b, big["ffn1_down"], G1, U1, "ffn1_bwd_act", rs2=("w_in", "w_out"))
    part["ffn1_down"] = run(_mm_tn, A1, dx1b, 0.5, _WIRE_DTYPE, "ffn1_dw_down")
    part["ffn1_gate"] = run(_mm_tn, dG1, h1, 1.0, _WIRE_DTYPE, "ffn1_dw_gate", rs1=("ffn1_down",))
    part["ffn1_up"] = run(_mm_tn, dU1, h1, 1.0, _WIRE_DTYPE, "ffn1_dw_up", rs1=("ffn1_gate",), rs2=("ffn1_down",))
    dh1 = run(_mm_nn, [(dG1, big["ffn1_gate"])], None, 1.0, F32, "ffn1_dh_gate", rs1=("ffn1_up",), rs2=("ffn1_gate",))
    dh1 = run(_mm_nn, [(dU1, big["ffn1_up"])], dh1, 1.0, F32, "ffn1_dh_up", rs2=("ffn1_up",))
    gx, _, d_g1 = _rms_bwd(dh1, x, small["ffn1_norm"], dx1, "ffn1_rms_bwd")

    d_small = {"ffn1_norm": d_g1, "mix_norm": d_gm, "swa_sinks": d_sinks[:, :N_SWA_HEADS], "swa_out_norm": d_ga,
               "sb_out_norm": d_gb, "ffn2_norm": d_g2, "final_norm": d_final}
    return loss, gx, d_small


MESH = pl.DeviceIdType.MESH
BIG_NAMES = ("ffn1_gate", "ffn1_up", "ffn1_down", "w_in", "w_out", "ffn2_gate", "ffn2_up", "ffn2_down")
_COMM_PARAMS = pltpu.CompilerParams(has_side_effects=True)


def _place():
    x, y, c = lax.axis_index("x"), lax.axis_index("y"), lax.axis_index("c")
    other_chips = [(1 - x, y), (x, 1 - y), (1 - x, 1 - y)]
    return x, y, c, other_chips


def _padded_rows(rows):
    full = N_DEV * rows
    return -(-full // _F_TILE) * _F_TILE


def _ag_job(shards):
    nw = len(shards)
    D = shards[0].shape[1]
    rows_w = [s.shape[0] for s in shards]
    full_w = [_padded_rows(r) for r in rows_w]
    pad_w = [f - N_DEV * r for f, r in zip(full_w, rows_w)]
    max_pad = max(max(pad_w), 16)

    class Plan:
        def __init__(self, ins, outs, scratch):
            zbuf, send_sems, recv_sems, local_sems, zero_sems = scratch
            x, y, c, chips = _place()
            me, sibling = (x, y, c), (x, y, 1 - c)

            def rows(w, px, py, pc):
                start = pl.multiple_of((4 * px + 2 * py + pc) * rows_w[w], 16)
                return outs[w].at[pl.ds(start, rows_w[w]), :]

            def copy(w, k, block, to, src=None):
                return pltpu.make_async_remote_copy(
                    src_ref=rows(w, *block) if src is None else src, dst_ref=rows(w, *block),
                    send_sem=send_sems.at[w, k], recv_sem=recv_sems.at[w, k], device_id=to, device_id_type=MESH)

            self.zbuf = zbuf
            self.local = [pltpu.make_async_copy(zbuf.at[pl.ds(0, pad_w[w]), :],
                                                outs[w].at[pl.ds(N_DEV * rows_w[w], pad_w[w]), :], zero_sems.at[w])
                          for w in range(nw) if pad_w[w]]
            self.local += [pltpu.make_async_copy(ins[w], rows(w, *me), local_sems.at[w]) for w in range(nw)]
            self.first = [[copy(w, 0, me, sibling, src=ins[w])]
                          + [copy(w, 1 + j, me, (*chip, c), src=ins[w]) for j, chip in enumerate(chips)]
                          for w in range(nw)]
            self.arrive = [[copy(w, 1 + j, (*chip, c), me) for j, chip in enumerate(chips)] for w in range(nw)]
            self.passed = [[copy(w, 4 + j, (*chip, c), sibling) for j, chip in enumerate(chips)] for w in range(nw)]
            self.from_sibling = [[copy(w, 0, sibling, me)]
                                 + [copy(w, 4 + j, (*chip, 1 - c), me) for j, chip in enumerate(chips)]
                                 for w in range(nw)]

    def start(ins, outs, scratch):
        plan = Plan(ins, outs, scratch)
        plan.zbuf[...] = jnp.zeros_like(plan.zbuf)
        for cp in plan.local:
            cp.start()
        for w in range(nw):
            for cp in plan.first[w]:
                cp.start()

    def mid(ins, outs, scratch):
        plan = Plan(ins, outs, scratch)
        for w in range(nw):
            for arrived, onward in zip(plan.arrive[w], plan.passed[w]):
                arrived.wait_recv()
                onward.start()

    def finish(ins, outs, scratch):
        plan = Plan(ins, outs, scratch)
        for w in range(nw):
            for cp in plan.from_sibling[w]:
                cp.wait_recv()
        for w in range(nw):
            for cp in plan.first[w] + plan.passed[w]:
                cp.wait_send()
        for cp in plan.local:
            cp.wait()

    return _Job(
        ins=shards, out_shape=[jax.ShapeDtypeStruct((f, D), s.dtype) for f, s in zip(full_w, shards)],
        scratch=[pltpu.VMEM((max_pad, D), shards[0].dtype), pltpu.SemaphoreType.DMA((nw, 7)),
                 pltpu.SemaphoreType.DMA((nw, 7)), pltpu.SemaphoreType.DMA((nw,)), pltpu.SemaphoreType.DMA((nw,))],
        start=start, mid=mid, finish=finish)


def _rs1_job(partials, rows_w):
    nw = len(partials)
    D = partials[0].shape[1]

    def copies(ins, outs, scratch):
        send_sems, recv_sems = scratch
        x, y, c, _ = _place()
        out = []
        for w in range(nw):
            r = rows_w[w]
            for q in range(4):
                src = ins[w].at[pl.ds(pl.multiple_of((2 * q + 1 - c) * r, 16), r), :]
                out.append(pltpu.make_async_remote_copy(
                    src_ref=src, dst_ref=outs[w].at[pl.ds(q * r, r), :], send_sem=send_sems.at[w, q],
                    recv_sem=recv_sems.at[w, q], device_id=(x, y, 1 - c), device_id_type=MESH))
        return out

    def start(ins, outs, scratch):
        for cp in copies(ins, outs, scratch):
            cp.start()

    def finish(ins, outs, scratch):
        for cp in copies(ins, outs, scratch):
            cp.wait()

    return _Job(
        ins=partials, out_shape=[jax.ShapeDtypeStruct((4 * r, D), p.dtype) for r, p in zip(rows_w, partials)],
        scratch=[pltpu.SemaphoreType.DMA((nw, 4)), pltpu.SemaphoreType.DMA((nw, 4))], start=start, finish=finish)


def _pair_sum(partial, from_sibling, rows, core, name):
    D = partial.shape[1]

    def body(core_ref, p_ref, s_ref, o_ref):
        o_ref[...] = (p_ref[...].astype(F32) + s_ref[...].astype(F32)).astype(o_ref.dtype)

    grid_spec = pltpu.PrefetchScalarGridSpec(
        num_scalar_prefetch=1, grid=(4,),
        in_specs=[pl.BlockSpec((rows, D), lambda q, core_ref: (2 * q + core_ref[0], 0)),
                  pl.BlockSpec((rows, D), lambda q, core_ref: (q, 0))],
        out_specs=pl.BlockSpec((rows, D), lambda q, core_ref: (q, 0)))
    return pl.pallas_call(
        body, name=name, grid_spec=grid_spec, out_shape=jax.ShapeDtypeStruct((4 * rows, D), partial.dtype),
        compiler_params=_params("arbitrary"),
    )(core, partial, from_sibling)


def _rs2_job(chip_sums, rows_w):
    nw = len(chip_sums)

    def copies(ins, outs, scratch):
        send_sems, recv_sems, local_sems = scratch
        x, y, c, chips = _place()
        my_chip = 2 * x + y
        out = []
        for w in range(nw):
            r = rows_w[w]
            mine = pl.ds(pl.multiple_of(my_chip * r, 16), r)
            out.append(pltpu.make_async_copy(ins[w].at[mine, :], outs[w].at[mine, :], local_sems.at[w]))
            for j, (qx, qy) in enumerate(chips):
                src = ins[w].at[pl.ds(pl.multiple_of((2 * qx + qy) * r, 16), r), :]
                out.append(pltpu.make_async_remote_copy(
                    src_ref=src, dst_ref=outs[w].at[mine, :], send_sem=send_sems.at[w, j],
                    recv_sem=recv_sems.at[w, j], device_id=(qx, qy, c), device_id_type=MESH))
        return out

    def start(ins, outs, scratch):
        for cp in copies(ins, outs, scratch):
            cp.start()

    def finish(ins, outs, scratch):
        for cp in copies(ins, outs, scratch):
            cp.wait()

    return _Job(
        ins=chip_sums, out_shape=[jax.ShapeDtypeStruct(s.shape, s.dtype) for s in chip_sums],
        scratch=[pltpu.SemaphoreType.DMA((nw, 3)), pltpu.SemaphoreType.DMA((nw, 3)), pltpu.SemaphoreType.DMA((nw,))],
        start=start, finish=finish)


class _Comm:
    def __init__(self, shards):
        self.shards = shards
        self.rows = {n: s.shape[0] for n, s in shards.items()}
        self.core = lax.axis_index("c").astype(jnp.int32).reshape(1)
        self.big, self.partial, self.chip_sums, self.slots = {}, {}, {}, {}

    def run(self, fn, *args, ag=(), rs1=(), rs2=()):
        jobs = []
        if ag:
            jobs.append(_ag_job([self.shards[n] for n in ag]))
        if rs1:
            jobs.append(_rs1_job([self.partial[n] for n in rs1], [self.rows[n] for n in rs1]))
        if rs2:
            jobs.append(_rs2_job([self.chip_sums[n] for n in rs2], [self.rows[n] for n in rs2]))
        out, job_res = fn(*args, jobs=jobs)
        job_res = iter(job_res)
        if ag:
            self.big.update(zip(ag, next(job_res)))
        if rs1:
            for n, got in zip(rs1, next(job_res)):
                self.chip_sums[n] = _pair_sum(self.partial[n], got, self.rows[n], self.core, "pair_sum_" + n)
        if rs2:
            self.slots.update(zip(rs2, next(job_res)))
        return out


def _chip_sum(slots, rows, name):
    D = slots.shape[1]

    def body(s_ref, o_ref):
        acc = s_ref[0:rows, :].astype(F32)
        for q in range(1, 4):
            acc = acc + s_ref[q * rows:(q + 1) * rows, :].astype(F32)
        o_ref[...] = acc

    tc = _tile(D, 512, LANES)
    return pl.pallas_call(
        body, name=name, grid=(D // tc,), out_shape=jax.ShapeDtypeStruct((rows, D), F32),
        in_specs=[pl.BlockSpec((4 * rows, tc), lambda j: (0, j))], out_specs=pl.BlockSpec((rows, tc), lambda j: (0, j)),
        compiler_params=_params("parallel"),
    )(slots)


SMALL_ROWS = 88


def _small_allreduce(vec):
    def body(v_ref, o_ref, gather, send_sems, recv_sems):
        x, y, c, _ = _place()
        my_id = 4 * x + 2 * y + c
        gather[my_id] = v_ref[...]
        copies = []
        for r in range(1, N_DEV):
            peer = (x ^ (r >> 2), y ^ ((r >> 1) & 1), c ^ (r & 1))
            cp = pltpu.make_async_remote_copy(src_ref=v_ref, dst_ref=gather.at[my_id], send_sem=send_sems.at[r - 1],
                                              recv_sem=recv_sems.at[r - 1], device_id=peer, device_id_type=MESH)
            cp.start()
            copies.append(cp)
        for cp in copies:
            cp.wait()
        acc = gather[0]
        for d in range(1, N_DEV):
            acc = acc + gather[d]
        o_ref[...] = acc

    vm = pl.BlockSpec(memory_space=pltpu.VMEM)
    return pl.pallas_call(
        body, name="small_allreduce", out_shape=jax.ShapeDtypeStruct(vec.shape, F32),
        in_specs=[vm], out_specs=vm,
        scratch_shapes=[pltpu.VMEM((N_DEV,) + vec.shape, F32), pltpu.SemaphoreType.DMA((N_DEV - 1,)),
                        pltpu.SemaphoreType.DMA((N_DEV - 1,))],
        compiler_params=_COMM_PARAMS,
    )(vec)


def _adamw(w, g, m, v, name):
    R, C = w.shape
    tr = _tile(R, 256, 8)

    def body(w_ref, g_ref, m_ref, v_ref, d_ref, nm_ref, nv_ref):
        gv = g_ref[...]
        nm = ADAM_B1 * m_ref[...] + (1.0 - ADAM_B1) * gv
        nv = ADAM_B2 * v_ref[...] + (1.0 - ADAM_B2) * jnp.square(gv)
        m_hat = nm / (1.0 - ADAM_B1 ** ADAM_STEP)
        v_hat = nv / (1.0 - ADAM_B2 ** ADAM_STEP)
        d_ref[...] = -ADAM_LR * (m_hat / (jnp.sqrt(v_hat) + ADAM_EPS) + ADAM_WD * w_ref[...])
        nm_ref[...] = nm
        nv_ref[...] = nv

    spec = pl.BlockSpec((tr, C), lambda i: (i, 0))
    out = jax.ShapeDtypeStruct((R, C), F32)
    return pl.pallas_call(
        body, name=name, grid=(R // tr,), out_shape=(out, out, out),
        in_specs=[spec] * 4, out_specs=(spec, spec, spec),
        compiler_params=_params("parallel"),
    )(w, g, m, v)


WEIGHT_NAMES = ("ffn1_norm", "ffn1_w_gate", "ffn1_w_up", "ffn1_w_down", "mix_norm", "w_in", "swa_sinks",
                "swa_out_norm", "sb_out_norm", "w_out", "ffn2_norm", "ffn2_w_gate", "ffn2_w_up", "ffn2_w_down",
                "final_norm")
SMALL_NAMES = ("ffn1_norm", "mix_norm", "swa_sinks", "swa_out_norm", "sb_out_norm", "ffn2_norm", "final_norm")
BIG_ARGS = {"ffn1_gate": ("ffn1_w_gate", True), "ffn1_up": ("ffn1_w_up", True), "ffn1_down": ("ffn1_w_down", False),
            "w_in": ("w_in", True), "w_out": ("w_out", False), "ffn2_gate": ("ffn2_w_gate", True),
            "ffn2_up": ("ffn2_w_up", True), "ffn2_down": ("ffn2_w_down", False)}


def _pack_small(parts):
    padded = [jnp.pad(p.reshape(1, -1), ((0, 0), (0, -p.size % LANES))) for p in parts]
    flat = jnp.concatenate(padded, axis=1)
    flat = jnp.pad(flat, ((0, 0), (0, SMALL_ROWS * LANES - flat.shape[1])))
    return flat.reshape(SMALL_ROWS, LANES)


def _unpack_small(block, shapes):
    flat = block.reshape(-1)
    out, off = [], 0
    for shp in shapes:
        n = 1
        for s in shp:
            n *= s
        out.append(flat[off:off + n].reshape(shp))
        off += n + (-n % LANES)
    return out


def kernel(x, ffn1_norm, ffn1_w_gate, ffn1_w_up, ffn1_w_down, mix_norm, w_in, swa_sinks, swa_out_norm, sb_out_norm, w_out, ffn2_norm, ffn2_w_gate, ffn2_w_up, ffn2_w_down, final_norm, loss_target, m_ffn1_norm, m_ffn1_w_gate, m_ffn1_w_up, m_ffn1_w_down, m_mix_norm, m_w_in, m_swa_sinks, m_swa_out_norm, m_sb_out_norm, m_w_out, m_ffn2_norm, m_ffn2_w_gate, m_ffn2_w_up, m_ffn2_w_down, m_final_norm, v_ffn1_norm, v_ffn1_w_gate, v_ffn1_w_up, v_ffn1_w_down, v_mix_norm, v_w_in, v_swa_sinks, v_swa_out_norm, v_sb_out_norm, v_w_out, v_ffn2_norm, v_ffn2_w_gate, v_ffn2_w_up, v_ffn2_w_down, v_final_norm):
    args = dict(locals())
    B, S, D = x.shape
    T = B * S
    weights = {n: args[n] for n in WEIGHT_NAMES}
    mom_m = {n: args["m_" + n] for n in WEIGHT_NAMES}
    mom_v = {n: args["v_" + n] for n in WEIGHT_NAMES}

    shards = {}
    for name in BIG_NAMES:
        arg, transposed = BIG_ARGS[name]
        w2 = weights[arg][0]
        shards[name] = (w2.T if transposed else w2).astype(_WIRE_DTYPE)
    comm = _Comm(shards)
    small = {n: weights[n].reshape(1, -1) for n in SMALL_NAMES}

    loss, gx, d_small = _layer_step(x.reshape(T, D), loss_target.reshape(T, D), B, S, small, comm)

    g_big = {n: _chip_sum(comm.slots[n], comm.rows[n], "chip_sum_" + n) for n in BIG_NAMES}

    small_shapes = [(1, 1)] + [d_small[n].shape for n in SMALL_NAMES]
    reduced = _small_allreduce(_pack_small([loss[:, :1]] + [d_small[n] for n in SMALL_NAMES]))
    red = _unpack_small(reduced, small_shapes)
    loss_out = red[0].reshape(())
    g_small = dict(zip(SMALL_NAMES, red[1:]))

    grads, deltas, new_m, new_v = {}, {}, {}, {}
    for name in BIG_NAMES:
        arg, transposed = BIG_ARGS[name]
        g2 = g_big[name].T if transposed else g_big[name]
        d2, m2, v2 = _adamw(weights[arg][0], g2, mom_m[arg][0], mom_v[arg][0], "adamw_" + name)
        grads[arg], deltas[arg], new_m[arg], new_v[arg] = g2[None], d2[None], m2[None], v2[None]
    shapes1 = [(1, weights[n].size) for n in SMALL_NAMES]
    packed = [_pack_small([t[n].reshape(1, -1) for n in SMALL_NAMES]) for t in (weights, g_small, mom_m, mom_v)]
    upd = _adamw(*packed, "adamw_small")
    for tgt_dict, block in zip((deltas, new_m, new_v), upd):
        for n, val in zip(SMALL_NAMES, _unpack_small(block, shapes1)):
            tgt_dict[n] = val.reshape(weights[n].shape)
    for n in SMALL_NAMES:
        grads[n] = g_small[n].reshape(weights[n].shape)

    return (loss_out, gx.reshape(B, S, D), *[grads[n] for n in WEIGHT_NAMES], *[deltas[n] for n in WEIGHT_NAMES],
            *[new_m[n] for n in WEIGHT_NAMES], *[new_v[n] for n in WEIGHT_NAMES])
```

```python
import functools

import jax
import jax.numpy as jnp
from jax import lax
from jax.experimental import pallas as pl
from jax.experimental.pallas import tpu as pltpu

F32 = jnp.float32
_MXU_DTYPE = jnp.bfloat16
_WIRE_DTYPE = jnp.bfloat16

EPS = 1e-6
HEAD_DIM = 64
N_SWA_HEADS = 16
N_SWA_KV = 4
N_SB_HEADS = 16
WINDOW = 128
SWA_Q = N_SWA_HEADS * HEAD_DIM
SWA_KV = N_SWA_KV * HEAD_DIM
SB_W = N_SB_HEADS * HEAD_DIM
IN_W = SWA_Q + 2 * SWA_KV + 3 * SB_W
LANES = 128
ATT_SCALE = HEAD_DIM ** -0.5

ADAM_LR = 0.001
ADAM_B1 = 0.9
ADAM_B2 = 0.999
ADAM_EPS = 1e-08
ADAM_WD = 0.01
ADAM_STEP = 10

N_DEV = 8
_VMEM_LIMIT_BYTES = 56 * 1024 * 1024
_F_TILE = 512


def _params(*semantics):
    return pltpu.CompilerParams(dimension_semantics=semantics, vmem_limit_bytes=_VMEM_LIMIT_BYTES)


def _tile(n, pref, align):
    t = min(n, pref)
    t -= t % align
    while t >= align:
        if n % t == 0:
            return t
        t -= align
    return n


def _dot(a, b):
    return lax.dot_general(a, b, (((1,), (0,)), ((), ())), preferred_element_type=F32)


def _dot_nt(a, b):
    return lax.dot_general(a, b, (((1,), (1,)), ((), ())), preferred_element_type=F32)


def _dot_tn(a, b):
    return lax.dot_general(a, b, (((0,), (0,)), ((), ())), preferred_element_type=F32)


class _Job:
    def __init__(self, ins, out_shape, scratch, start, finish, mid=None):
        self.ins, self.out_shape, self.scratch = list(ins), list(out_shape), list(scratch)
        self.start, self.mid, self.finish = start, mid, finish


_JOB_MID_FRACTION = 0.8


def _call(body, *, name, grid, in_specs, out_specs, out_shape, args, semantics, scratch_shapes=(), jobs=()):
    single = not isinstance(out_shape, (tuple, list))
    if not jobs:
        res = pl.pallas_call(body, name=name, grid=grid, in_specs=list(in_specs), out_specs=out_specs,
                             out_shape=out_shape, scratch_shapes=list(scratch_shapes),
                             compiler_params=_params(*semantics))(*args)
        return res, []
    base_out = [out_shape] if single else list(out_shape)
    base_out_specs = [out_specs] if single else list(out_specs)
    n_in, n_out, n_scr = len(args), len(base_out), len(scratch_shapes)
    any_spec = pl.BlockSpec(memory_space=pl.ANY)
    total = 1
    for g in grid:
        total *= g
    mid_step = min(total - 1, int(total * _JOB_MID_FRACTION))

    def wrapped(*refs):
        pos = n_in
        job_ins = []
        for job in jobs:
            job_ins.append(refs[pos:pos + len(job.ins)])
            pos += len(job.ins)
        outs = refs[pos:pos + n_out]
        pos += n_out
        job_outs = []
        for job in jobs:
            job_outs.append(refs[pos:pos + len(job.out_shape)])
            pos += len(job.out_shape)
        scr = refs[pos:pos + n_scr]
        pos += n_scr
        job_scr = []
        for job in jobs:
            job_scr.append(refs[pos:pos + len(job.scratch)])
            pos += len(job.scratch)
        step = pl.program_id(0)
        for d in range(1, len(grid)):
            step = step * grid[d] + pl.program_id(d)

        @pl.when(step == 0)
        def _():
            for job, ji, jo, js in zip(jobs, job_ins, job_outs, job_scr):
                job.start(ji, jo, js)

        @pl.when(step == mid_step)
        def _():
            for job, ji, jo, js in zip(jobs, job_ins, job_outs, job_scr):
                if job.mid is not None:
                    job.mid(ji, jo, js)

        body(*refs[:n_in], *outs, *scr)

        @pl.when(step == total - 1)
        def _():
            for job, ji, jo, js in zip(jobs, job_ins, job_outs, job_scr):
                job.finish(ji, jo, js)

    all_args = list(args) + [a for job in jobs for a in job.ins]
    all_in_specs = list(in_specs) + [any_spec for job in jobs for _ in job.ins]
    all_out_shape = base_out + [s for job in jobs for s in job.out_shape]
    all_out_specs = base_out_specs + [any_spec for job in jobs for _ in job.out_shape]
    all_scratch = list(scratch_shapes) + [s for job in jobs for s in job.scratch]
    res = pl.pallas_call(
        wrapped, name=name, grid=grid, in_specs=all_in_specs, out_specs=tuple(all_out_specs),
        out_shape=tuple(all_out_shape), scratch_shapes=all_scratch,
        compiler_params=pltpu.CompilerParams(dimension_semantics=("arbitrary",) * len(grid),
                                             vmem_limit_bytes=_VMEM_LIMIT_BYTES, has_side_effects=True),
    )(*all_args)
    base = res[0] if single else tuple(res[:n_out])
    job_res, pos = [], n_out
    for job in jobs:
        job_res.append(tuple(res[pos:pos + len(job.out_shape)]))
        pos += len(job.out_shape)
    return base, job_res


def _rms_fwd(x, g, name, jobs=()):
    T, D = x.shape
    tm = _tile(T, 512, 16)

    def body(x_ref, g_ref, o_ref):
        xv = x_ref[...]
        r = lax.rsqrt(jnp.mean(xv * xv, axis=-1, keepdims=True) + EPS)
        o_ref[...] = (xv * r * g_ref[...]).astype(o_ref.dtype)

    return _call(
        body, name=name, grid=(T // tm,),
        out_shape=jax.ShapeDtypeStruct((T, D), _MXU_DTYPE),
        in_specs=[pl.BlockSpec((tm, D), lambda i: (i, 0)), pl.BlockSpec((1, D), lambda i: (0, 0))],
        out_specs=pl.BlockSpec((tm, D), lambda i: (i, 0)), args=(x, g), semantics=("parallel",), jobs=jobs)


def _rms_bwd_rows(dh, xv, g):
    r = lax.rsqrt(jnp.mean(xv * xv, axis=-1, keepdims=True) + EPS)
    xhat = xv * r
    u = dh * g
    dx = r * (u - xhat * jnp.mean(u * xhat, axis=-1, keepdims=True))
    return dx, dh * xhat


def _rms_bwd(dh, x, g, dres, name):
    T, D = x.shape
    tm = _tile(T, 256, 16)

    def body(dh_ref, x_ref, g_ref, dres_ref, dx_ref, dxb_ref, dg_ref):
        @pl.when(pl.program_id(0) == 0)
        def _():
            dg_ref[...] = jnp.zeros_like(dg_ref)

        dx, dgr = _rms_bwd_rows(dh_ref[...], x_ref[...], g_ref[...])
        dx = dres_ref[...] + dx
        dx_ref[...] = dx
        dxb_ref[...] = dx.astype(dxb_ref.dtype)
        dg_ref[...] += jnp.sum(dgr, axis=0, keepdims=True)

    row = pl.BlockSpec((tm, D), lambda i: (i, 0))
    vec = pl.BlockSpec((1, D), lambda i: (0, 0))
    return pl.pallas_call(
        body, name=name, grid=(T // tm,),
        out_shape=(jax.ShapeDtypeStruct((T, D), F32), jax.ShapeDtypeStruct((T, D), _MXU_DTYPE),
                   jax.ShapeDtypeStruct((1, D), F32)),
        in_specs=[row, row, vec, row], out_specs=(row, row, vec),
        compiler_params=_params("arbitrary"),
    )(dh, x, g, dres)


def _loss_head(x, g, tgt, name):
    T, D = x.shape
    tm = _tile(T, 256, 16)

    def body(x_ref, g_ref, t_ref, dx_ref, dxb_ref, dg_ref, loss_ref):
        @pl.when(pl.program_id(0) == 0)
        def _():
            dg_ref[...] = jnp.zeros_like(dg_ref)
            loss_ref[...] = jnp.zeros_like(loss_ref)

        xv = x_ref[...]
        gv = g_ref[...]
        r = lax.rsqrt(jnp.mean(xv * xv, axis=-1, keepdims=True) + EPS)
        xhat = xv * r
        diff = xhat * gv - t_ref[...]
        tok = jnp.mean(diff * diff, axis=-1, keepdims=True)
        loss_ref[...] += 0.5 * jnp.sum(tok, axis=0, keepdims=True)
        dy = diff / D
        u = dy * gv
        dx = r * (u - xhat * jnp.mean(u * xhat, axis=-1, keepdims=True))
        dx_ref[...] = dx
        dxb_ref[...] = dx.astype(dxb_ref.dtype)
        dg_ref[...] += jnp.sum(dy * xhat, axis=0, keepdims=True)

    row = pl.BlockSpec((tm, D), lambda i: (i, 0))
    vec = pl.BlockSpec((1, D), lambda i: (0, 0))
    return pl.pallas_call(
        body, name=name, grid=(T // tm,),
        out_shape=(jax.ShapeDtypeStruct((T, D), F32), jax.ShapeDtypeStruct((T, D), _MXU_DTYPE),
                   jax.ShapeDtypeStruct((1, D), F32), jax.ShapeDtypeStruct((1, LANES), F32)),
        in_specs=[row, vec, row],
        out_specs=(row, row, vec, pl.BlockSpec((1, LANES), lambda i: (0, 0))),
        compiler_params=_params("arbitrary"),
    )(x, g, tgt)


def _outnorm_fwd(ya, yb, ga, gb, name):
    T, W = ya.shape
    tm = _tile(T, 512, 16)

    def body(ya_ref, yb_ref, ga_ref, gb_ref, o_ref):
        for k, (y_ref, g_ref) in enumerate(((ya_ref, ga_ref), (yb_ref, gb_ref))):
            yv = y_ref[...]
            r = lax.rsqrt(jnp.mean(yv * yv, axis=-1, keepdims=True) + EPS)
            o_ref[:, k * W:(k + 1) * W] = (yv * r * g_ref[...]).astype(o_ref.dtype)

    row = pl.BlockSpec((tm, W), lambda i: (i, 0))
    vec = pl.BlockSpec((1, W), lambda i: (0, 0))
    return pl.pallas_call(
        body, name=name, grid=(T // tm,),
        out_shape=jax.ShapeDtypeStruct((T, 2 * W), _MXU_DTYPE),
        in_specs=[row, row, vec, vec], out_specs=pl.BlockSpec((tm, 2 * W), lambda i: (i, 0)),
        compiler_params=_params("parallel"),
    )(ya, yb, ga, gb)


def _outnorm_bwd(dyn, ya, yb, ga, gb, name):
    T, W = ya.shape
    tm = _tile(T, 256, 16)

    def body(d_ref, ya_ref, yb_ref, ga_ref, gb_ref, dya_ref, dyb_ref, dga_ref, dgb_ref):
        @pl.when(pl.program_id(0) == 0)
        def _():
            dga_ref[...] = jnp.zeros_like(dga_ref)
            dgb_ref[...] = jnp.zeros_like(dgb_ref)

        for k, (y_ref, g_ref, dy_ref, dg_ref) in enumerate(
                ((ya_ref, ga_ref, dya_ref, dga_ref), (yb_ref, gb_ref, dyb_ref, dgb_ref))):
            dy, dgr = _rms_bwd_rows(d_ref[:, k * W:(k + 1) * W], y_ref[...], g_ref[...])
            dy_ref[...] = dy.astype(dy_ref.dtype)
            dg_ref[...] += jnp.sum(dgr, axis=0, keepdims=True)

    row = pl.BlockSpec((tm, W), lambda i: (i, 0))
    vec = pl.BlockSpec((1, W), lambda i: (0, 0))
    return pl.pallas_call(
        body, name=name, grid=(T // tm,),
        out_shape=(jax.ShapeDtypeStruct((T, W), _MXU_DTYPE), jax.ShapeDtypeStruct((T, W), _MXU_DTYPE),
                   jax.ShapeDtypeStruct((1, W), F32), jax.ShapeDtypeStruct((1, W), F32)),
        in_specs=[pl.BlockSpec((tm, 2 * W), lambda i: (i, 0)), row, row, vec, vec],
        out_specs=(row, row, vec, vec),
        compiler_params=_params("arbitrary"),
    )(dyn, ya, yb, ga, gb)


def _ffn_gu(h, wg_t, wu_t, name, jobs=()):
    T, D = h.shape
    Fp = wg_t.shape[0]
    tm = _tile(T, 1024, 16)
    tn = _tile(Fp, _F_TILE, LANES)

    def body(h_ref, wg_ref, wu_ref, g_ref, u_ref, a_ref):
        hv = h_ref[...]
        g = _dot_nt(hv, wg_ref[...])
        u = _dot_nt(hv, wu_ref[...])
        g_ref[...] = g.astype(g_ref.dtype)
        u_ref[...] = u.astype(u_ref.dtype)
        a_ref[...] = (g * jax.nn.sigmoid(g) * u).astype(a_ref.dtype)

    act = pl.BlockSpec((tm, tn), lambda n, m: (m, n))
    wsp = pl.BlockSpec((tn, D), lambda n, m: (n, 0))
    out = jax.ShapeDtypeStruct((T, Fp), _MXU_DTYPE)
    return _call(
        body, name=name, grid=(Fp // tn, T // tm), out_shape=(out, out, out),
        in_specs=[pl.BlockSpec((tm, D), lambda n, m: (m, 0)), wsp, wsp],
        out_specs=(act, act, act), args=(h, wg_t, wu_t), semantics=("parallel", "parallel"), jobs=jobs)


def _ffn_bwd_act(dxb, wd, G, U, name, jobs=()):
    T, D = dxb.shape
    Fp = wd.shape[0]
    tm = _tile(T, 1024, 16)
    tn = _tile(Fp, _F_TILE, LANES)

    def body(e_ref, wd_ref, g_ref, u_ref, dg_ref, du_ref):
        da = 0.5 * _dot_nt(e_ref[...], wd_ref[...])
        g = g_ref[...].astype(F32)
        u = u_ref[...].astype(F32)
        s = jax.nn.sigmoid(g)
        du_ref[...] = (da * (g * s)).astype(du_ref.dtype)
        dg_ref[...] = (da * u * (s * (1.0 + g * (1.0 - s)))).astype(dg_ref.dtype)

    act = pl.BlockSpec((tm, tn), lambda n, m: (m, n))
    out = jax.ShapeDtypeStruct((T, Fp), _MXU_DTYPE)
    return _call(
        body, name=name, grid=(Fp // tn, T // tm), out_shape=(out, out),
        in_specs=[pl.BlockSpec((tm, D), lambda n, m: (m, 0)), pl.BlockSpec((tn, D), lambda n, m: (n, 0)),
                  act, act],
        out_specs=(act, act), args=(dxb, wd, G, U), semantics=("parallel", "parallel"), jobs=jobs)


def _ffn_gate(h, wg_t, U, name, jobs=()):
    T, D = h.shape
    Fp = wg_t.shape[0]
    tm = _tile(T, 1024, 16)
    tn = _tile(Fp, _F_TILE, LANES)

    def body(h_ref, wg_ref, u_ref, g_ref, a_ref):
        g = _dot_nt(h_ref[...], wg_ref[...])
        g_ref[...] = g.astype(g_ref.dtype)
        a_ref[...] = (g * jax.nn.sigmoid(g) * u_ref[...].astype(F32)).astype(a_ref.dtype)

    act = pl.BlockSpec((tm, tn), lambda n, m: (m, n))
    out = jax.ShapeDtypeStruct((T, Fp), _MXU_DTYPE)
    return _call(
        body, name=name, grid=(Fp // tn, T // tm), out_shape=(out, out),
        in_specs=[pl.BlockSpec((tm, D), lambda n, m: (m, 0)), pl.BlockSpec((tn, D), lambda n, m: (n, 0)), act],
        out_specs=(act, act), args=(h, wg_t, U), semantics=("parallel", "parallel"), jobs=jobs)


def _mm_nt(a, b, out_dtype, name, jobs=()):
    M, K = a.shape
    N = b.shape[0]
    tm = _tile(M, 1024, 16)
    tn = _tile(N, 512, LANES)

    def body(a_ref, b_ref, o_ref):
        o_ref[...] = _dot_nt(a_ref[...], b_ref[...]).astype(o_ref.dtype)

    return _call(
        body, name=name, grid=(N // tn, M // tm), out_shape=jax.ShapeDtypeStruct((M, N), out_dtype),
        in_specs=[pl.BlockSpec((tm, K), lambda n, m: (m, 0)), pl.BlockSpec((tn, K), lambda n, m: (n, 0))],
        out_specs=pl.BlockSpec((tm, tn), lambda n, m: (m, n)), args=(a, b),
        semantics=("parallel", "parallel"), jobs=jobs)


_MM_OPERAND_BYTES = 26 * 1024 * 1024


def _k_tile(K, bytes_per_k, align):
    best = align
    for t in range(align, K + 1, align):
        if K % t == 0 and 2 * t * bytes_per_k <= _MM_OPERAND_BYTES:
            best = t
    return best


def _mm_nn(pairs, res, alpha, out_dtype, name, jobs=()):
    M, K = pairs[0][0].shape
    N = pairs[0][1].shape[1]
    n_pairs = len(pairs)
    tm = _tile(M, 1024, 16)
    tn = _tile(N, 1024, LANES)
    tk = _k_tile(K, n_pairs * (tm + tn) * pairs[0][0].dtype.itemsize, LANES)
    nk = K // tk

    def body(*refs):
        ab = refs[:2 * n_pairs]
        res_ref = refs[2 * n_pairs] if res is not None else None
        o_ref = refs[2 * n_pairs + (res is not None)]

        def finish(acc):
            out = alpha * acc
            if res_ref is not None:
                out = res_ref[...] + out
            o_ref[...] = out.astype(o_ref.dtype)

        part = _dot(ab[0][...], ab[1][...])
        for i in range(1, n_pairs):
            part = part + _dot(ab[2 * i][...], ab[2 * i + 1][...])
        if nk == 1:
            finish(part)
        else:
            acc_ref = refs[-1]
            k = pl.program_id(2)

            @pl.when(k == 0)
            def _():
                acc_ref[...] = part

            @pl.when(k > 0)
            def _():
                acc_ref[...] += part

            @pl.when(k == nk - 1)
            def _():
                finish(acc_ref[...])

    in_specs, args = [], []
    for a, b in pairs:
        in_specs += [pl.BlockSpec((tm, tk), lambda m, n, k: (m, k)), pl.BlockSpec((tk, tn), lambda m, n, k: (k, n))]
        args += [a, b]
    if res is not None:
        in_specs.append(pl.BlockSpec((tm, tn), lambda m, n, k: (m, n)))
        args.append(res)
    return _call(
        body, name=name, grid=(M // tm, N // tn, nk), out_shape=jax.ShapeDtypeStruct((M, N), out_dtype),
        in_specs=in_specs, out_specs=pl.BlockSpec((tm, tn), lambda m, n, k: (m, n)),
        scratch_shapes=[pltpu.VMEM((tm, tn), F32)] if nk > 1 else [], args=args,
        semantics=("parallel", "parallel", "arbitrary"), jobs=jobs)


def _mm_tn(a, b, alpha, out_dtype, name, jobs=()):
    K, M = a.shape
    N = b.shape[1]
    tm = _tile(M, 512, LANES)
    tn = _tile(N, 1024, LANES)

    def body(a_ref, b_ref, o_ref):
        o_ref[...] = (alpha * _dot_tn(a_ref[...], b_ref[...])).astype(o_ref.dtype)

    return _call(
        body, name=name, grid=(N // tn, M // tm), out_shape=jax.ShapeDtypeStruct((M, N), out_dtype),
        in_specs=[pl.BlockSpec((K, tm), lambda n, m: (0, m)), pl.BlockSpec((K, tn), lambda n, m: (0, n))],
        out_specs=pl.BlockSpec((tm, tn), lambda n, m: (m, n)), args=(a, b),
        semantics=("parallel", "parallel"), jobs=jobs)


def _half_masks():
    lane = lax.broadcasted_iota(jnp.int32, (1, LANES), 1)
    return (lane < HEAD_DIM, lane >= HEAD_DIM)


def _swap_halves(v):
    return pltpu.roll(v.astype(F32), HEAD_DIM, 1).astype(v.dtype)


def _swa_geometry(n):
    qi = lax.broadcasted_iota(jnp.int32, (WINDOW, 2 * WINDOW), 0)
    kp = lax.broadcasted_iota(jnp.int32, (WINDOW, 2 * WINDOW), 1)
    dist = (WINDOW + qi) - kp
    valid = (dist >= 0) & (dist < WINDOW) & ((n > 0) | (kp >= WINDOW))
    return dist.astype(F32), valid


def _swa_slope(h):
    return 2.0 ** (-8.0 * (h + 1) / N_SWA_HEADS)


def _swa_probs(qm, kx, sink, slope, distf, valid):
    s = _dot_nt(qm, kx) * ATT_SCALE - slope * distf
    s = jnp.where(valid, s, -1e30)
    m = jnp.maximum(jnp.max(s, axis=1, keepdims=True), sink)
    p = jnp.exp(s - m)
    e_sink = jnp.exp(sink - m)
    den = jnp.sum(p, axis=1, keepdims=True) + e_sink
    return p / den, e_sink / den


def _swa_specs(B, S):
    nb = S // WINDOW
    kcol = SWA_Q // SWA_KV
    cur = lambda b, n: (b * nb + n, kcol)
    prev = lambda b, n: (b * nb + jnp.maximum(n - 1, 0), kcol)
    curv = lambda b, n: (b * nb + n, kcol + 1)
    prevv = lambda b, n: (b * nb + jnp.maximum(n - 1, 0), kcol + 1)
    q_spec = pl.BlockSpec((WINDOW, SWA_Q), lambda b, n: (b * nb + n, 0))
    kv = [pl.BlockSpec((WINDOW, SWA_KV), f) for f in (prev, cur, prevv, curv)]
    sink_spec = pl.BlockSpec(memory_space=pltpu.SMEM)
    return nb, q_spec, kv, sink_spec


def _swa_kv_views(kp_ref, kc_ref, vp_ref, vc_ref, g):
    hm = _half_masks()
    c0 = (g // 2) * LANES
    k_all = jnp.concatenate([kp_ref[:, c0:c0 + LANES], kc_ref[:, c0:c0 + LANES]], axis=0)
    v_all = jnp.concatenate([vp_ref[:, c0:c0 + LANES], vc_ref[:, c0:c0 + LANES]], axis=0)
    b = g % 2
    ks, vs = [None, None], [None, None]
    ks[b], vs[b] = k_all, v_all
    ks[1 - b], vs[1 - b] = _swap_halves(k_all), _swap_halves(v_all)
    ks = [jnp.where(hm[a], ks[a], 0) for a in range(2)]
    vs = [jnp.where(hm[a], vs[a], 0) for a in range(2)]
    return ks, vs


def _swa_fwd(proj, sinks, B, S, name):
    T = B * S
    nb, q_spec, kv_specs, sink_spec = _swa_specs(B, S)

    def body(sink_ref, q_ref, kp_ref, kc_ref, vp_ref, vc_ref, y_ref):
        hm = _half_masks()
        distf, valid = _swa_geometry(pl.program_id(1))
        for g in range(N_SWA_KV):
            ks, vs = _swa_kv_views(kp_ref, kc_ref, vp_ref, vc_ref, g)
            for pp in (2 * g, 2 * g + 1):
                q_pair = q_ref[:, pp * LANES:(pp + 1) * LANES]
                o_pair = jnp.zeros((WINDOW, LANES), F32)
                for a in range(2):
                    h = 2 * pp + a
                    qm = jnp.where(hm[a], q_pair, 0)
                    p, _ = _swa_probs(qm, ks[a], sink_ref[0, h], _swa_slope(h), distf, valid)
                    o_pair = o_pair + _dot(p.astype(_MXU_DTYPE), vs[a])
                y_ref[:, pp * LANES:(pp + 1) * LANES] = o_pair

    return pl.pallas_call(
        body, name=name, grid=(B, nb), out_shape=jax.ShapeDtypeStruct((T, SWA_Q), F32),
        in_specs=[sink_spec, q_spec] + kv_specs,
        out_specs=pl.BlockSpec((WINDOW, SWA_Q), lambda b, n: (b * nb + n, 0)),
        compiler_params=_params("parallel", "parallel"),
    )(sinks, proj, proj, proj, proj, proj)


def _swa_bwd(proj, sinks, dya, B, S, name):
    T = B * S
    nb, q_spec, kv_specs, sink_spec = _swa_specs(B, S)

    def body(sink_ref, q_ref, kp_ref, kc_ref, vp_ref, vc_ref, do_ref,
             dq_ref, dk_ref, dv_ref, dsink_ref, dk_acc, dv_acc):
        b_id, n = pl.program_id(0), pl.program_id(1)
        hm = _half_masks()
        lane = lax.broadcasted_iota(jnp.int32, (1, LANES), 1)

        @pl.when((b_id == 0) & (n == 0))
        def _():
            dsink_ref[...] = jnp.zeros_like(dsink_ref)

        @pl.when(n == 0)
        def _():
            dk_acc[...] = jnp.zeros_like(dk_acc)
            dv_acc[...] = jnp.zeros_like(dv_acc)

        distf, valid = _swa_geometry(n)
        r_prev = pl.multiple_of(jnp.maximum(n - 1, 0) * WINDOW, WINDOW)
        r_cur = pl.multiple_of(n * WINDOW, WINDOW)
        dsink = jnp.zeros((1, LANES), F32)
        for g in range(N_SWA_KV):
            ks, vs = _swa_kv_views(kp_ref, kc_ref, vp_ref, vc_ref, g)
            dk_g = [jnp.zeros((2 * WINDOW, LANES), F32) for _ in range(2)]
            dv_g = [jnp.zeros((2 * WINDOW, LANES), F32) for _ in range(2)]
            for pp in (2 * g, 2 * g + 1):
                q_pair = q_ref[:, pp * LANES:(pp + 1) * LANES]
                do_pair = do_ref[:, pp * LANES:(pp + 1) * LANES]
                dq_pair = jnp.zeros((WINDOW, LANES), F32)
                for a in range(2):
                    h = 2 * pp + a
                    qm = jnp.where(hm[a], q_pair, 0)
                    dom = jnp.where(hm[a], do_pair, 0)
                    p, p_sink = _swa_probs(qm, ks[a], sink_ref[0, h], _swa_slope(h), distf, valid)
                    dp = _dot_nt(dom, vs[a])
                    delta = jnp.sum(p * dp, axis=1, keepdims=True)
                    ds = (p * (dp - delta) * ATT_SCALE).astype(_MXU_DTYPE)
                    dsink = dsink + jnp.where(lane == h, -jnp.sum(p_sink * delta), 0.0)
                    dq_pair = dq_pair + _dot(ds, ks[a])
                    dk_g[a] = dk_g[a] + _dot_tn(ds, qm)
                    dv_g[a] = dv_g[a] + _dot_tn(p.astype(_MXU_DTYPE), dom)
                dq_ref[:, pp * LANES:(pp + 1) * LANES] = dq_pair.astype(dq_ref.dtype)
            bsel = g % 2
            dk_t = dk_g[bsel] + pltpu.roll(dk_g[1 - bsel], HEAD_DIM, 1)
            dv_t = dv_g[bsel] + pltpu.roll(dv_g[1 - bsel], HEAD_DIM, 1)
            c0 = (g // 2) * LANES
            dk_acc[pl.ds(r_prev, WINDOW), c0:c0 + LANES] += dk_t[:WINDOW]
            dk_acc[pl.ds(r_cur, WINDOW), c0:c0 + LANES] += dk_t[WINDOW:]
            dv_acc[pl.ds(r_prev, WINDOW), c0:c0 + LANES] += dv_t[:WINDOW]
            dv_acc[pl.ds(r_cur, WINDOW), c0:c0 + LANES] += dv_t[WINDOW:]
        dsink_ref[...] += dsink

        @pl.when(n == nb - 1)
        def _():
            dk_ref[...] = dk_acc[...].astype(dk_ref.dtype)
            dv_ref[...] = dv_acc[...].astype(dv_ref.dtype)

    seq_kv = pl.BlockSpec((S, SWA_KV), lambda b, n: (b, 0))
    return pl.pallas_call(
        body, name=name, grid=(B, nb),
        out_shape=(jax.ShapeDtypeStruct((T, SWA_Q), _MXU_DTYPE), jax.ShapeDtypeStruct((T, SWA_KV), _MXU_DTYPE),
                   jax.ShapeDtypeStruct((T, SWA_KV), _MXU_DTYPE), jax.ShapeDtypeStruct((1, LANES), F32)),
        in_specs=[sink_spec, q_spec] + kv_specs + [pl.BlockSpec((WINDOW, SWA_Q), lambda b, n: (b * nb + n, 0))],
        out_specs=(pl.BlockSpec((WINDOW, SWA_Q), lambda b, n: (b * nb + n, 0)), seq_kv, seq_kv,
                   pl.BlockSpec((1, LANES), lambda b, n: (0, 0))),
        scratch_shapes=[pltpu.VMEM((S, SWA_KV), F32), pltpu.VMEM((S, SWA_KV), F32)],
        compiler_params=_params("arbitrary", "arbitrary"),
    )(sinks, proj, proj, proj, proj, proj, dya)


SB_TILE = 256
SB_HALF = 128


def _tri2(cond):
    j = lax.broadcasted_iota(jnp.int32, (2 * SB_HALF, SB_HALF), 0) & (SB_HALF - 1)
    s = lax.broadcasted_iota(jnp.int32, (2 * SB_HALF, SB_HALF), 1)
    return cond(j, s).astype(_MXU_DTYPE)


def _half_cumsums(x, tri2):
    out = []
    for h in range(2):
        xh = x[:, h * SB_HALF:(h + 1) * SB_HALF]
        hi = xh.astype(_MXU_DTYPE)
        lo = (xh - hi.astype(F32)).astype(_MXU_DTYPE)
        out.append(_dot(jnp.concatenate([hi, lo], axis=1), tri2))
    return out


def _log_sigmoid(z):
    return jnp.minimum(z, 0.0) - jnp.log(1.0 + jnp.exp(-jnp.abs(z)))


def _sb_specs(B, S):
    qb = (SWA_Q + 2 * SWA_KV) // LANES
    kb = qb + SB_W // LANES
    vb = kb + SB_W // LANES
    return [pl.BlockSpec((S, LANES), functools.partial(lambda b, p, c: (b, c + p), c=c)) for c in (qb, kb, vb)]


def _sb_fwd(proj, B, S, name, jobs=()):
    T = B * S
    tq = SB_TILE
    nq = S // tq

    def body(q_ref, k_ref, v_ref, y_ref, tot_ref):
        hm = _half_masks()
        ji = lax.broadcasted_iota(jnp.int32, (tq, tq), 0)
        si = lax.broadcasted_iota(jnp.int32, (tq, tq), 1)
        tri_after = _tri2(lambda j, s: j > s)
        causal = si < ji

        def q_loop(qi, carry):
            r0 = pl.multiple_of(qi * tq, tq)
            q_pair = q_ref[pl.ds(r0, tq), :] * ATT_SCALE
            qms = [jnp.where(hm[a], q_pair, 0) for a in range(2)]

            def tile(c0, state, diagonal):
                kk = k_ref[pl.ds(c0, tq), :]
                vv = v_ref[pl.ds(c0, tq), :]
                two = range(2)
                z = [_dot_nt(qms[a], kk) for a in two]
                lb = [_log_sigmoid(z[a]) for a in two]
                l1m = [jnp.where(causal, lb[a] - z[a], 0.0) if diagonal else lb[a] - z[a] for a in two]
                cum = [_half_cumsums(l1m[a], tri_after) for a in two]
                tot = [[cum[a][h][:, 0:1] + l1m[a][:, h * SB_HALF:h * SB_HALF + 1] for h in two] for a in two]
                after = [jnp.concatenate([cum[a][0] + (state[a][1] + tot[a][1]), cum[a][1] + state[a][1]], axis=1)
                         for a in two]
                att = [jnp.exp(lb[a] + after[a]) for a in two]
                if diagonal:
                    att = [jnp.where(causal, att[a], 0.0) for a in two]
                acc = [state[a][0] + _dot(att[a].astype(_MXU_DTYPE), jnp.where(hm[a], vv, 0)) for a in two]
                car = [state[a][1] + (tot[a][0] + tot[a][1]) for a in two]
                return tuple((acc[a], car[a]) for a in two)

            zero = (jnp.zeros((tq, LANES), F32), jnp.zeros((tq, 1), F32))
            state = tile(r0, (zero, zero), True)
            state = lax.fori_loop(
                0, qi, lambda it, st: tile(pl.multiple_of((qi - 1 - it) * tq, tq), st, False), state)
            y_ref[pl.ds(r0, tq), :] = state[0][0] + state[1][0]
            tot_ref[pl.ds(r0, tq), :] = jnp.where(hm[0], state[0][1], state[1][1])
            return carry

        lax.fori_loop(0, nq, q_loop, 0)

    out_spec = pl.BlockSpec((S, LANES), lambda b, p: (b, p))
    return _call(
        body, name=name, grid=(B, SB_W // LANES),
        out_shape=(jax.ShapeDtypeStruct((T, SB_W), F32), jax.ShapeDtypeStruct((T, SB_W), F32)),
        in_specs=_sb_specs(B, S), out_specs=(out_spec, out_spec), args=(proj, proj, proj),
        semantics=("parallel", "parallel"), jobs=jobs)


def _sb_bwd(proj, tot, dyb, B, S, name, jobs=()):
    T = B * S
    tq = SB_TILE
    nq = S // tq

    def body(q_ref, k_ref, v_ref, do_ref, tot_ref, dq_ref, dk_ref, dv_ref, dk_acc, dv_acc):
        hm = _half_masks()
        ji = lax.broadcasted_iota(jnp.int32, (tq, tq), 0)
        si = lax.broadcasted_iota(jnp.int32, (tq, tq), 1)
        tri_incl = _tri2(lambda j, s: j <= s)
        tri_excl = _tri2(lambda j, s: j < s)
        causal = si < ji
        dk_acc[...] = jnp.zeros_like(dk_acc)
        dv_acc[...] = jnp.zeros_like(dv_acc)

        def q_loop(qi, carry):
            r0 = pl.multiple_of(qi * tq, tq)
            q_pair = q_ref[pl.ds(r0, tq), :] * ATT_SCALE
            do_pair = do_ref[pl.ds(r0, tq), :]
            tot_pair = tot_ref[pl.ds(r0, tq), :]
            qms = [jnp.where(hm[a], q_pair, 0) for a in range(2)]
            doms = [jnp.where(hm[a], do_pair, 0) for a in range(2)]
            totals = [jnp.max(jnp.where(hm[a], tot_pair, -jnp.inf), axis=1, keepdims=True) for a in range(2)]

            def tile(c0, state, diagonal):
                kk = k_ref[pl.ds(c0, tq), :]
                vv = v_ref[pl.ds(c0, tq), :]
                ks = kk * ATT_SCALE
                two = range(2)
                last = SB_HALF - 1
                z = [_dot_nt(qms[a], kk) for a in two]
                d_att = [_dot_nt(doms[a], vv) for a in two]
                lb = [_log_sigmoid(z[a]) for a in two]
                l1m = [jnp.where(causal, lb[a] - z[a], 0.0) if diagonal else lb[a] - z[a] for a in two]
                cum = [_half_cumsums(l1m[a], tri_incl) for a in two]
                upto = [jnp.concatenate([cum[a][0] + state[a][1],
                                         cum[a][1] + (state[a][1] + cum[a][0][:, last:last + 1])], axis=1) for a in two]
                att = [jnp.exp(lb[a] + (totals[a] - upto[a])) for a in two]
                if diagonal:
                    att = [jnp.where(causal, att[a], 0.0) for a in two]
                d_log = [d_att[a] * att[a] for a in two]
                cumd = [_half_cumsums(d_log[a], tri_excl) for a in two]
                totd = [[cumd[a][h][:, last:last + 1] + d_log[a][:, h * SB_HALF + last:h * SB_HALF + last + 1]
                         for h in two] for a in two]
                before = [jnp.concatenate([cumd[a][0] + state[a][2], cumd[a][1] + (state[a][2] + totd[a][0])], axis=1)
                          for a in two]
                sig = [jnp.exp(lb[a]) for a in two]
                dz = [d_log[a] * (1.0 - sig[a]) - sig[a] * before[a] for a in two]
                if diagonal:
                    dz = [jnp.where(causal, dz[a], 0.0) for a in two]
                dzb = [dz[a].astype(_MXU_DTYPE) for a in two]
                dq = [state[a][0] + _dot(dzb[a], jnp.where(hm[a], ks, 0)) for a in two]
                dk_acc[pl.ds(c0, tq), :] += _dot_tn(dzb[0], qms[0]) + _dot_tn(dzb[1], qms[1])
                dv_acc[pl.ds(c0, tq), :] += (_dot_tn(att[0].astype(_MXU_DTYPE), doms[0])
                                             + _dot_tn(att[1].astype(_MXU_DTYPE), doms[1]))
                cp = [upto[a][:, tq - 1:tq] for a in two]
                cq = [state[a][2] + (totd[a][0] + totd[a][1]) for a in two]
                return tuple((dq[a], cp[a], cq[a]) for a in two)

            zero_col = jnp.zeros((tq, 1), F32)
            zero = (jnp.zeros((tq, LANES), F32), zero_col, zero_col)
            state = lax.fori_loop(0, qi, lambda kj, st: tile(pl.multiple_of(kj * tq, tq), st, False), (zero, zero))
            state = tile(r0, state, True)
            dq_ref[pl.ds(r0, tq), :] = (state[0][0] + state[1][0]).astype(dq_ref.dtype)
            return carry

        lax.fori_loop(0, nq, q_loop, 0)
        dk_ref[...] = dk_acc[...].astype(dk_ref.dtype)
        dv_ref[...] = dv_acc[...].astype(dv_ref.dtype)

    pair = pl.BlockSpec((S, LANES), lambda b, p: (b, p))
    out = jax.ShapeDtypeStruct((T, SB_W), _MXU_DTYPE)
    return _call(
        body, name=name, grid=(B, SB_W // LANES), out_shape=(out, out, out),
        in_specs=_sb_specs(B, S) + [pair, pair], out_specs=(pair, pair, pair),
        scratch_shapes=[pltpu.VMEM((S, LANES), F32), pltpu.VMEM((S, LANES), F32)],
        args=(proj, proj, proj, dyb, tot), semantics=("parallel", "parallel"), jobs=jobs)


def _layer_step(x, tgt, B, S, small, comm):
    run, big, part = comm.run, comm.big, comm.partial
    ffn1_w, ffn2_w = ("ffn1_down", "ffn1_gate", "ffn1_up"), ("ffn2_down", "ffn2_gate", "ffn2_up")

    h1 = run(_rms_fwd, x, small["ffn1_norm"], "ffn1_rms", ag=("ffn1_up",))
    U1 = run(_mm_nt, h1, big["ffn1_up"], _MXU_DTYPE, "ffn1_up", ag=("ffn1_gate",))
    G1, A1 = run(_ffn_gate, h1, big["ffn1_gate"], U1, "ffn1_gate", ag=("ffn1_down",))
    x1 = run(_mm_nn, [(A1, big["ffn1_down"])], x, 0.5, F32, "ffn1_down", ag=("w_in", "w_out"))
    h2 = run(_rms_fwd, x1, small["mix_norm"], "mix_rms")
    proj = run(_mm_nt, h2, big["w_in"], _MXU_DTYPE, "in_proj")
    ya = _swa_fwd(proj, small["swa_sinks"], B, S, "swa_fwd")
    yb, tot = run(_sb_fwd, proj, B, S, "sb_fwd", ag=("ffn2_gate", "ffn2_up", "ffn2_down"))
    yn = _outnorm_fwd(ya, yb, small["swa_out_norm"], small["sb_out_norm"], "out_norm")
    x2 = run(_mm_nn, [(yn, big["w_out"])], x1, 1.0, F32, "out_proj")
    h3 = run(_rms_fwd, x2, small["ffn2_norm"], "ffn2_rms")
    G2, U2, A2 = run(_ffn_gu, h3, big["ffn2_gate"], big["ffn2_up"], "ffn2_gate_up")
    x3 = run(_mm_nn, [(A2, big["ffn2_down"])], x2, 0.5, F32, "ffn2_down")

    dx3, dx3b, d_final, loss = _loss_head(x3, small["final_norm"], tgt, "loss_head")

    dG2, dU2 = run(_ffn_bwd_act, dx3b, big["ffn2_down"], G2, U2, "ffn2_bwd_act")
    part["ffn2_down"] = run(_mm_tn, A2, dx3b, 0.5, _WIRE_DTYPE, "ffn2_dw_down")
    part["ffn2_gate"] = run(_mm_tn, dG2, h3, 1.0, _WIRE_DTYPE, "ffn2_dw_gate")
    part["ffn2_up"] = run(_mm_tn, dU2, h3, 1.0, _WIRE_DTYPE, "ffn2_dw_up")
    dh3 = run(_mm_nn, [(dG2, big["ffn2_gate"]), (dU2, big["ffn2_up"])], None, 1.0, F32, "ffn2_dh", rs1=ffn2_w)
    dx2, dx2b, d_g2 = _rms_bwd(dh3, x2, small["ffn2_norm"], dx3, "ffn2_rms_bwd")

    part["w_out"] = run(_mm_tn, yn, dx2b, 1.0, _WIRE_DTYPE, "dw_out")
    dyn = run(_mm_nt, dx2b, big["w_out"], F32, "out_proj_bwd")
    dya, dyb, d_ga, d_gb = _outnorm_bwd(dyn, ya, yb, small["swa_out_norm"], small["sb_out_norm"], "out_norm_bwd")
    dqa, dka, dva, d_sinks = _swa_bwd(proj, small["swa_sinks"], dya, B, S, "swa_bwd")
    dqb, dkb, dvb = run(_sb_bwd, proj, tot, dyb, B, S, "sb_bwd", rs2=ffn2_w)
    dproj = jnp.concatenate([dqa, dka, dva, dqb, dkb, dvb], axis=1)
    part["w_in"] = run(_mm_tn, dproj, h2, 1.0, _WIRE_DTYPE, "dw_in")
    dh2 = run(_mm_nn, [(dproj, big["w_in"])], None, 1.0, F32, "in_proj_bwd", rs1=("w_in", "w_out"))
    dx1, dx1b, d_gm = _rms_bwd(dh2, x1, small["mix_norm"], dx2, "mix_rms_bwd")

    dG1, dU1 = run(_ffn_bwd_act, dx1b, big["ffn1_down"], G1, U1, "ffn1_bwd_act", rs2=("w_in", "w_out"))
    part["ffn1_down"] = run(_mm_tn, A1, dx1b, 0.5, _WIRE_DTYPE, "ffn1_dw_down")
    part["ffn1_gate"] = run(_mm_tn, dG1, h1, 1.0, _WIRE_DTYPE, "ffn1_dw_gate", rs1=("ffn1_down",))
    part["ffn1_up"] = run(_mm_tn, dU1, h1, 1.0, _WIRE_DTYPE, "ffn1_dw_up", rs1=("ffn1_gate",), rs2=("ffn1_down",))
    dh1 = run(_mm_nn, [(dG1, big["ffn1_gate"])], None, 1.0, F32, "ffn1_dh_gate", rs1=("ffn1_up",), rs2=("ffn1_gate",))
    dh1 = run(_mm_nn, [(dU1, big["ffn1_up"])], dh1, 1.0, F32, "ffn1_dh_up", rs2=("ffn1_up",))
    gx, _, d_g1 = _rms_bwd(dh1, x, small["ffn1_norm"], dx1, "ffn1_rms_bwd")

    d_small = {"ffn1_norm": d_g1, "mix_norm": d_gm, "swa_sinks": d_sinks[:, :N_SWA_HEADS], "swa_out_norm": d_ga,
               "sb_out_norm": d_gb, "ffn2_norm": d_g2, "final_norm": d_final}
    return loss, gx, d_small


MESH = pl.DeviceIdType.MESH
BIG_NAMES = ("ffn1_gate", "ffn1_up", "ffn1_down", "w_in", "w_out", "ffn2_gate", "ffn2_up", "ffn2_down")
_COMM_PARAMS = pltpu.CompilerParams(has_side_effects=True)


def _place():
    x, y, c = lax.axis_index("x"), lax.axis_index("y"), lax.axis_index("c")
    other_chips = [(1 - x, y), (x, 1 - y), (1 - x, 1 - y)]
    return x, y, c, other_chips


def _padded_rows(rows):
    full = N_DEV * rows
    return -(-full // _F_TILE) * _F_TILE


def _ag_job(shards):
    nw = len(shards)
    D = shards[0].shape[1]
    rows_w = [s.shape[0] for s in shards]
    full_w = [_padded_rows(r) for r in rows_w]
    pad_w = [f - N_DEV * r for f, r in zip(full_w, rows_w)]
    max_pad = max(max(pad_w), 16)

    class Plan:
        def __init__(self, ins, outs, scratch):
            zbuf, send_sems, recv_sems, local_sems, zero_sems = scratch
            x, y, c, chips = _place()
            me, sibling = (x, y, c), (x, y, 1 - c)

            def rows(w, px, py, pc):
                start = pl.multiple_of((4 * px + 2 * py + pc) * rows_w[w], 16)
                return outs[w].at[pl.ds(start, rows_w[w]), :]

            def copy(w, k, block, to, src=None):
                return pltpu.make_async_remote_copy(
                    src_ref=rows(w, *block) if src is None else src, dst_ref=rows(w, *block),
                    send_sem=send_sems.at[w, k], recv_sem=recv_sems.at[w, k], device_id=to, device_id_type=MESH)

            self.zbuf = zbuf
            self.local = [pltpu.make_async_copy(zbuf.at[pl.ds(0, pad_w[w]), :],
                                                outs[w].at[pl.ds(N_DEV * rows_w[w], pad_w[w]), :], zero_sems.at[w])
                          for w in range(nw) if pad_w[w]]
            self.local += [pltpu.make_async_copy(ins[w], rows(w, *me), local_sems.at[w]) for w in range(nw)]
            self.first = [[copy(w, 0, me, sibling, src=ins[w])]
                          + [copy(w, 1 + j, me, (*chip, c), src=ins[w]) for j, chip in enumerate(chips)]
                          for w in range(nw)]
            self.arrive = [[copy(w, 1 + j, (*chip, c), me) for j, chip in enumerate(chips)] for w in range(nw)]
            self.passed = [[copy(w, 4 + j, (*chip, c), sibling) for j, chip in enumerate(chips)] for w in range(nw)]
            self.from_sibling = [[copy(w, 0, sibling, me)]
                                 + [copy(w, 4 + j, (*chip, 1 - c), me) for j, chip in enumerate(chips)]
                                 for w in range(nw)]

    def start(ins, outs, scratch):
        plan = Plan(ins, outs, scratch)
        plan.zbuf[...] = jnp.zeros_like(plan.zbuf)
        for cp in plan.local:
            cp.start()
        for w in range(nw):
            for cp in plan.first[w]:
                cp.start()

    def mid(ins, outs, scratch):
        plan = Plan(ins, outs, scratch)
        for w in range(nw):
            for arrived, onward in zip(plan.arrive[w], plan.passed[w]):
                arrived.wait_recv()
                onward.start()

    def finish(ins, outs, scratch):
        plan = Plan(ins, outs, scratch)
        for w in range(nw):
            for cp in plan.from_sibling[w]:
                cp.wait_recv()
        for w in range(nw):
            for cp in plan.first[w] + plan.passed[w]:
                cp.wait_send()
        for cp in plan.local:
            cp.wait()

    return _Job(
        ins=shards, out_shape=[jax.ShapeDtypeStruct((f, D), s.dtype) for f, s in zip(full_w, shards)],
        scratch=[pltpu.VMEM((max_pad, D), shards[0].dtype), pltpu.SemaphoreType.DMA((nw, 7)),
                 pltpu.SemaphoreType.DMA((nw, 7)), pltpu.SemaphoreType.DMA((nw,)), pltpu.SemaphoreType.DMA((nw,))],
        start=start, mid=mid, finish=finish)


def _rs1_job(partials, rows_w):
    nw = len(partials)
    D = partials[0].shape[1]

    def copies(ins, outs, scratch):
        send_sems, recv_sems = scratch
        x, y, c, _ = _place()
        out = []
        for w in range(nw):
            r = rows_w[w]
            for q in range(4):
                src = ins[w].at[pl.ds(pl.multiple_of((2 * q + 1 - c) * r, 16), r), :]
                out.append(pltpu.make_async_remote_copy(
                    src_ref=src, dst_ref=outs[w].at[pl.ds(q * r, r), :], send_sem=send_sems.at[w, q],
                    recv_sem=recv_sems.at[w, q], device_id=(x, y, 1 - c), device_id_type=MESH))
        return out

    def start(ins, outs, scratch):
        for cp in copies(ins, outs, scratch):
            cp.start()

    def finish(ins, outs, scratch):
        for cp in copies(ins, outs, scratch):
            cp.wait()

    return _Job(
        ins=partials, out_shape=[jax.ShapeDtypeStruct((4 * r, D), p.dtype) for r, p in zip(rows_w, partials)],
        scratch=[pltpu.SemaphoreType.DMA((nw, 4)), pltpu.SemaphoreType.DMA((nw, 4))], start=start, finish=finish)


def _pair_sum(partial, from_sibling, rows, core, name):
    D = partial.shape[1]

    def body(core_ref, p_ref, s_ref, o_ref):
        o_ref[...] = (p_ref[...].astype(F32) + s_ref[...].astype(F32)).astype(o_ref.dtype)

    grid_spec = pltpu.PrefetchScalarGridSpec(
        num_scalar_prefetch=1, grid=(4,),
        in_specs=[pl.BlockSpec((rows, D), lambda q, core_ref: (2 * q + core_ref[0], 0)),
                  pl.BlockSpec((rows, D), lambda q, core_ref: (q, 0))],
        out_specs=pl.BlockSpec((rows, D), lambda q, core_ref: (q, 0)))
    return pl.pallas_call(
        body, name=name, grid_spec=grid_spec, out_shape=jax.ShapeDtypeStruct((4 * rows, D), partial.dtype),
        compiler_params=_params("arbitrary"),
    )(core, partial, from_sibling)


def _rs2_job(chip_sums, rows_w):
    nw = len(chip_sums)

    def copies(ins, outs, scratch):
        send_sems, recv_sems, local_sems = scratch
        x, y, c, chips = _place()
        my_chip = 2 * x + y
        out = []
        for w in range(nw):
            r = rows_w[w]
            mine = pl.ds(pl.multiple_of(my_chip * r, 16), r)
            out.append(pltpu.make_async_copy(ins[w].at[mine, :], outs[w].at[mine, :], local_sems.at[w]))
            for j, (qx, qy) in enumerate(chips):
                src = ins[w].at[pl.ds(pl.multiple_of((2 * qx + qy) * r, 16), r), :]
                out.append(pltpu.make_async_remote_copy(
                    src_ref=src, dst_ref=outs[w].at[mine, :], send_sem=send_sems.at[w, j],
                    recv_sem=recv_sems.at[w, j], device_id=(qx, qy, c), device_id_type=MESH))
        return out

    def start(ins, outs, scratch):
        for cp in copies(ins, outs, scratch):
            cp.start()

    def finish(ins, outs, scratch):
        for cp in copies(ins, outs, scratch):
            cp.wait()

    return _Job(
        ins=chip_sums, out_shape=[jax.ShapeDtypeStruct(s.shape, s.dtype) for s in chip_sums],
        scratch=[pltpu.SemaphoreType.DMA((nw, 3)), pltpu.SemaphoreType.DMA((nw, 3)), pltpu.SemaphoreType.DMA((nw,))],
        start=start, finish=finish)


class _Comm:
    def __init__(self, shards):
        self.shards = shards
        self.rows = {n: s.shape[0] for n, s in shards.items()}
        self.core = lax.axis_index("c").astype(jnp.int32).reshape(1)
        self.big, self.partial, self.chip_sums, self.slots = {}, {}, {}, {}

    def run(self, fn, *args, ag=(), rs1=(), rs2=()):
        jobs = []
        if ag:
            jobs.append(_ag_job([self.shards[n] for n in ag]))
        if rs1:
            jobs.append(_rs1_job([self.partial[n] for n in rs1], [self.rows[n] for n in rs1]))
        if rs2:
            jobs.append(_rs2_job([self.chip_sums[n] for n in rs2], [self.rows[n] for n in rs2]))
        out, job_res = fn(*args, jobs=jobs)
        job_res = iter(job_res)
        if ag:
            self.big.update(zip(ag, next(job_res)))
        if rs1:
            for n, got in zip(rs1, next(job_res)):
                self.chip_sums[n] = _pair_sum(self.partial[n], got, self.rows[n], self.core, "pair_sum_" + n)
        if rs2:
            self.slots.update(zip(rs2, next(job_res)))
        return out


def _chip_sum(slots, rows, name):
    D = slots.shape[1]

    def body(s_ref, o_ref):
        acc = s_ref[0:rows, :].astype(F32)
        for q in range(1, 4):
            acc = acc + s_ref[q * rows:(q + 1) * rows, :].astype(F32)
        o_ref[...] = acc

    tc = _tile(D, 512, LANES)
    return pl.pallas_call(
        body, name=name, grid=(D // tc,), out_shape=jax.ShapeDtypeStruct((rows, D), F32),
        in_specs=[pl.BlockSpec((4 * rows, tc), lambda j: (0, j))], out_specs=pl.BlockSpec((rows, tc), lambda j: (0, j)),
        compiler_params=_params("parallel"),
    )(slots)


SMALL_ROWS = 88


def _small_allreduce(vec):
    def body(v_ref, o_ref, gather, send_sems, recv_sems):
        x, y, c, _ = _place()
        my_id = 4 * x + 2 * y + c
        gather[my_id] = v_ref[...]
        copies = []
        for r in range(1, N_DEV):
            peer = (x ^ (r >> 2), y ^ ((r >> 1) & 1), c ^ (r & 1))
            cp = pltpu.make_async_remote_copy(src_ref=v_ref, dst_ref=gather.at[my_id], send_sem=send_sems.at[r - 1],
                                              recv_sem=recv_sems.at[r - 1], device_id=peer, device_id_type=MESH)
            cp.start()
            copies.append(cp)
        for cp in copies:
            cp.wait()
        acc = gather[0]
        for d in range(1, N_DEV):
            acc = acc + gather[d]
        o_ref[...] = acc

    vm = pl.BlockSpec(memory_space=pltpu.VMEM)
    return pl.pallas_call(
        body, name="small_allreduce", out_shape=jax.ShapeDtypeStruct(vec.shape, F32),
        in_specs=[vm], out_specs=vm,
        scratch_shapes=[pltpu.VMEM((N_DEV,) + vec.shape, F32), pltpu.SemaphoreType.DMA((N_DEV - 1,)),
                        pltpu.SemaphoreType.DMA((N_DEV - 1,))],
        compiler_params=_COMM_PARAMS,
    )(vec)


def _adamw(w, g, m, v, name):
    R, C = w.shape
    tr = _tile(R, 256, 8)

    def body(w_ref, g_ref, m_ref, v_ref, d_ref, nm_ref, nv_ref):
        gv = g_ref[...]
        nm = ADAM_B1 * m_ref[...] + (1.0 - ADAM_B1) * gv
        nv = ADAM_B2 * v_ref[...] + (1.0 - ADAM_B2) * jnp.square(gv)
        m_hat = nm / (1.0 - ADAM_B1 ** ADAM_STEP)
        v_hat = nv / (1.0 - ADAM_B2 ** ADAM_STEP)
        d_ref[...] = -ADAM_LR * (m_hat / (jnp.sqrt(v_hat) + ADAM_EPS) + ADAM_WD * w_ref[...])
        nm_ref[...] = nm
        nv_ref[...] = nv

    spec = pl.BlockSpec((tr, C), lambda i: (i, 0))
    out = jax.ShapeDtypeStruct((R, C), F32)
    return pl.pallas_call(
        body, name=name, grid=(R // tr,), out_shape=(out, out, out),
        in_specs=[spec] * 4, out_specs=(spec, spec, spec),
        compiler_params=_params("parallel"),
    )(w, g, m, v)


WEIGHT_NAMES = ("ffn1_norm", "ffn1_w_gate", "ffn1_w_up", "ffn1_w_down", "mix_norm", "w_in", "swa_sinks",
                "swa_out_norm", "sb_out_norm", "w_out", "ffn2_norm", "ffn2_w_gate", "ffn2_w_up", "ffn2_w_down",
                "final_norm")
SMALL_NAMES = ("ffn1_norm", "mix_norm", "swa_sinks", "swa_out_norm", "sb_out_norm", "ffn2_norm", "final_norm")
BIG_ARGS = {"ffn1_gate": ("ffn1_w_gate", True), "ffn1_up": ("ffn1_w_up", True), "ffn1_down": ("ffn1_w_down", False),
            "w_in": ("w_in", True), "w_out": ("w_out", False), "ffn2_gate": ("ffn2_w_gate", True),
            "ffn2_up": ("ffn2_w_up", True), "ffn2_down": ("ffn2_w_down", False)}


def _pack_small(parts):
    padded = [jnp.pad(p.reshape(1, -1), ((0, 0), (0, -p.size % LANES))) for p in parts]
    flat = jnp.concatenate(padded, axis=1)
    flat = jnp.pad(flat, ((0, 0), (0, SMALL_ROWS * LANES - flat.shape[1])))
    return flat.reshape(SMALL_ROWS, LANES)


def _unpack_small(block, shapes):
    flat = block.reshape(-1)
    out, off = [], 0
    for shp in shapes:
        n = 1
        for s in shp:
            n *= s
        out.append(flat[off:off + n].reshape(shp))
        off += n + (-n % LANES)
    return out


def kernel(x, ffn1_norm, ffn1_w_gate, ffn1_w_up, ffn1_w_down, mix_norm, w_in, swa_sinks, swa_out_norm, sb_out_norm, w_out, ffn2_norm, ffn2_w_gate, ffn2_w_up, ffn2_w_down, final_norm, loss_target, m_ffn1_norm, m_ffn1_w_gate, m_ffn1_w_up, m_ffn1_w_down, m_mix_norm, m_w_in, m_swa_sinks, m_swa_out_norm, m_sb_out_norm, m_w_out, m_ffn2_norm, m_ffn2_w_gate, m_ffn2_w_up, m_ffn2_w_down, m_final_norm, v_ffn1_norm, v_ffn1_w_gate, v_ffn1_w_up, v_ffn1_w_down, v_mix_norm, v_w_in, v_swa_sinks, v_swa_out_norm, v_sb_out_norm, v_w_out, v_ffn2_norm, v_ffn2_w_gate, v_ffn2_w_up, v_ffn2_w_down, v_final_norm):
    args = dict(locals())
    B, S, D = x.shape
    T = B * S
    weights = {n: args[n] for n in WEIGHT_NAMES}
    mom_m = {n: args["m_" + n] for n in WEIGHT_NAMES}
    mom_v = {n: args["v_" + n] for n in WEIGHT_NAMES}

    shards = {}
    for name in BIG_NAMES:
        arg, transposed = BIG_ARGS[name]
        w2 = weights[arg][0]
        shards[name] = (w2.T if transposed else w2).astype(_WIRE_DTYPE)
    comm = _Comm(shards)
    small = {n: weights[n].reshape(1, -1) for n in SMALL_NAMES}

    loss, gx, d_small = _layer_step(x.reshape(T, D), loss_target.reshape(T, D), B, S, small, comm)

    g_big = {n: _chip_sum(comm.slots[n], comm.rows[n], "chip_sum_" + n) for n in BIG_NAMES}

    small_shapes = [(1, 1)] + [d_small[n].shape for n in SMALL_NAMES]
    reduced = _small_allreduce(_pack_small([loss[:, :1]] + [d_small[n] for n in SMALL_NAMES]))
    red = _unpack_small(reduced, small_shapes)
    loss_out = red[0].reshape(())
    g_small = dict(zip(SMALL_NAMES, red[1:]))

    grads, deltas, new_m, new_v = {}, {}, {}, {}
    for name in BIG_NAMES:
        arg, transposed = BIG_ARGS[name]
        g2 = g_big[name].T if transposed else g_big[name]
        d2, m2, v2 = _adamw(weights[arg][0], g2, mom_m[arg][0], mom_v[arg][0], "adamw_" + name)
        grads[arg], deltas[arg], new_m[arg], new_v[arg] = g2[None], d2[None], m2[None], v2[None]
    shapes1 = [(1, weights[n].size) for n in SMALL_NAMES]
    packed = [_pack_small([t[n].reshape(1, -1) for n in SMALL_NAMES]) for t in (weights, g_small, mom_m, mom_v)]
    upd = _adamw(*packed, "adamw_small")
    for tgt_dict, block in zip((deltas, new_m, new_v), upd):
        for n, val in zip(SMALL_NAMES, _unpack_small(block, shapes1)):
            tgt_dict[n] = val.reshape(weights[n].shape)
    for n in SMALL_NAMES:
        grads[n] = g_small[n].reshape(weights[n].shape)

    return (loss_out, gx.reshape(B, S, D), *[grads[n] for n in WEIGHT_NAMES], *[deltas[n] for n in WEIGHT_NAMES],
            *[new_m[n] for n in WEIGHT_NAMES], *[new_v[n] for n in WEIGHT_NAMES])
```

```python
import functools

import jax
import jax.numpy as jnp
from jax import lax
from jax.experimental import pallas as pl
from jax.experimental.pallas import tpu as pltpu

F32 = jnp.float32
_MXU_DTYPE = jnp.bfloat16
_WIRE_DTYPE = jnp.bfloat16

EPS = 1e-6
HEAD_DIM = 64
N_SWA_HEADS = 16
N_SWA_KV = 4
N_SB_HEADS = 16
WINDOW = 128
SWA_Q = N_SWA_HEADS * HEAD_DIM
SWA_KV = N_SWA_KV * HEAD_DIM
SB_W = N_SB_HEADS * HEAD_DIM
IN_W = SWA_Q + 2 * SWA_KV + 3 * SB_W
LANES = 128
ATT_SCALE = HEAD_DIM ** -0.5

ADAM_LR = 0.001
ADAM_B1 = 0.9
ADAM_B2 = 0.999
ADAM_EPS = 1e-08
ADAM_WD = 0.01
ADAM_STEP = 10

N_DEV = 8
_VMEM_LIMIT_BYTES = 56 * 1024 * 1024
_F_TILE = 512


def _params(*semantics):
    return pltpu.CompilerParams(dimension_semantics=semantics, vmem_limit_bytes=_VMEM_LIMIT_BYTES)


def _tile(n, pref, align):
    t = min(n, pref)
    t -= t % align
    while t >= align:
        if n % t == 0:
            return t
        t -= align
    return n


def _dot(a, b):
    return lax.dot_general(a, b, (((1,), (0,)), ((), ())), preferred_element_type=F32)


def _dot_nt(a, b):
    return lax.dot_general(a, b, (((1,), (1,)), ((), ())), preferred_element_type=F32)


def _dot_tn(a, b):
    return lax.dot_general(a, b, (((0,), (0,)), ((), ())), preferred_element_type=F32)


class _Job:
    def __init__(self, ins, out_shape, scratch, start, finish, mid=None):
        self.ins, self.out_shape, self.scratch = list(ins), list(out_shape), list(scratch)
        self.start, self.mid, self.finish = start, mid, finish


_JOB_MID_FRACTION = 0.8


def _call(body, *, name, grid, in_specs, out_specs, out_shape, args, semantics, scratch_shapes=(), jobs=()):
    single = not isinstance(out_shape, (tuple, list))
    if not jobs:
        res = pl.pallas_call(body, name=name, grid=grid, in_specs=list(in_specs), out_specs=out_specs,
                             out_shape=out_shape, scratch_shapes=list(scratch_shapes),
                             compiler_params=_params(*semantics))(*args)
        return res, []
    base_out = [out_shape] if single else list(out_shape)
    base_out_specs = [out_specs] if single else list(out_specs)
    n_in, n_out, n_scr = len(args), len(base_out), len(scratch_shapes)
    any_spec = pl.BlockSpec(memory_space=pl.ANY)
    total = 1
    for g in grid:
        total *= g
    mid_step = min(total - 1, int(total * _JOB_MID_FRACTION))

    def wrapped(*refs):
        pos = n_in
        job_ins = []
        for job in jobs:
            job_ins.append(refs[pos:pos + len(job.ins)])
            pos += len(job.ins)
        outs = refs[pos:pos + n_out]
        pos += n_out
        job_outs = []
        for job in jobs:
            job_outs.append(refs[pos:pos + len(job.out_shape)])
            pos += len(job.out_shape)
        scr = refs[pos:pos + n_scr]
        pos += n_scr
        job_scr = []
        for job in jobs:
            job_scr.append(refs[pos:pos + len(job.scratch)])
            pos += len(job.scratch)
        step = pl.program_id(0)
        for d in range(1, len(grid)):
            step = step * grid[d] + pl.program_id(d)

        @pl.when(step == 0)
        def _():
            for job, ji, jo, js in zip(jobs, job_ins, job_outs, job_scr):
                job.start(ji, jo, js)

        @pl.when(step == mid_step)
        def _():
            for job, ji, jo, js in zip(jobs, job_ins, job_outs, job_scr):
                if job.mid is not None:
                    job.mid(ji, jo, js)

        body(*refs[:n_in], *outs, *scr)

        @pl.when(step == total - 1)
        def _():
            for job, ji, jo, js in zip(jobs, job_ins, job_outs, job_scr):
                job.finish(ji, jo, js)

    all_args = list(args) + [a for job in jobs for a in job.ins]
    all_in_specs = list(in_specs) + [any_spec for job in jobs for _ in job.ins]
    all_out_shape = base_out + [s for job in jobs for s in job.out_shape]
    all_out_specs = base_out_specs + [any_spec for job in jobs for _ in job.out_shape]
    all_scratch = list(scratch_shapes) + [s for job in jobs for s in job.scratch]
    res = pl.pallas_call(
        wrapped, name=name, grid=grid, in_specs=all_in_specs, out_specs=tuple(all_out_specs),
        out_shape=tuple(all_out_shape), scratch_shapes=all_scratch,
        compiler_params=pltpu.CompilerParams(dimension_semantics=("arbitrary",) * len(grid),
                                             vmem_limit_bytes=_VMEM_LIMIT_BYTES, has_side_effects=True),
    )(*all_args)
    base = res[0] if single else tuple(res[:n_out])
    job_res, pos = [], n_out
    for job in jobs:
        job_res.append(tuple(res[pos:pos + len(job.out_shape)]))
        pos += len(job.out_shape)
    return base, job_res


def _rms_fwd(x, g, name, jobs=()):
    T, D = x.shape
    tm = _tile(T, 512, 16)

    def body(x_ref, g_ref, o_ref):
        xv = x_ref[...]
        r = lax.rsqrt(jnp.mean(xv * xv, axis=-1, keepdims=True) + EPS)
        o_ref[...] = (xv * r * g_ref[...]).astype(o_ref.dtype)

    return _call(
        body, name=name, grid=(T // tm,),
        out_shape=jax.ShapeDtypeStruct((T, D), _MXU_DTYPE),
        in_specs=[pl.BlockSpec((tm, D), lambda i: (i, 0)), pl.BlockSpec((1, D), lambda i: (0, 0))],
        out_specs=pl.BlockSpec((tm, D), lambda i: (i, 0)), args=(x, g), semantics=("parallel",), jobs=jobs)


def _rms_bwd_rows(dh, xv, g):
    r = lax.rsqrt(jnp.mean(xv * xv, axis=-1, keepdims=True) + EPS)
    xhat = xv * r
    u = dh * g
    dx = r * (u - xhat * jnp.mean(u * xhat, axis=-1, keepdims=True))
    return dx, dh * xhat


def _rms_bwd(dh, x, g, dres, name):
    T, D = x.shape
    tm = _tile(T, 256, 16)

    def body(dh_ref, x_ref, g_ref, dres_ref, dx_ref, dxb_ref, dg_ref):
        @pl.when(pl.program_id(0) == 0)
        def _():
            dg_ref[...] = jnp.zeros_like(dg_ref)

        dx, dgr = _rms_bwd_rows(dh_ref[...], x_ref[...], g_ref[...])
        dx = dres_ref[...] + dx
        dx_ref[...] = dx
        dxb_ref[...] = dx.astype(dxb_ref.dtype)
        dg_ref[...] += jnp.sum(dgr, axis=0, keepdims=True)

    row = pl.BlockSpec((tm, D), lambda i: (i, 0))
    vec = pl.BlockSpec((1, D), lambda i: (0, 0))
    return pl.pallas_call(
        body, name=name, grid=(T // tm,),
        out_shape=(jax.ShapeDtypeStruct((T, D), F32), jax.ShapeDtypeStruct((T, D), _MXU_DTYPE),
                   jax.ShapeDtypeStruct((1, D), F32)),
        in_specs=[row, row, vec, row], out_specs=(row, row, vec),
        compiler_params=_params("arbitrary"),
    )(dh, x, g, dres)


def _loss_head(x, g, tgt, name):
    T, D = x.shape
    tm = _tile(T, 256, 16)

    def body(x_ref, g_ref, t_ref, dx_ref, dxb_ref, dg_ref, loss_ref):
        @pl.when(pl.program_id(0) == 0)
        def _():
            dg_ref[...] = jnp.zeros_like(dg_ref)
            loss_ref[...] = jnp.zeros_like(loss_ref)

        xv = x_ref[...]
        gv = g_ref[...]
        r = lax.rsqrt(jnp.mean(xv * xv, axis=-1, keepdims=True) + EPS)
        xhat = xv * r
        diff = xhat * gv - t_ref[...]
        tok = jnp.mean(diff * diff, axis=-1, keepdims=True)
        loss_ref[...] += 0.5 * jnp.sum(tok, axis=0, keepdims=True)
        dy = diff / D
        u = dy * gv
        dx = r * (u - xhat * jnp.mean(u * xhat, axis=-1, keepdims=True))
        dx_ref[...] = dx
        dxb_ref[...] = dx.astype(dxb_ref.dtype)
        dg_ref[...] += jnp.sum(dy * xhat, axis=0, keepdims=True)

    row = pl.BlockSpec((tm, D), lambda i: (i, 0))
    vec = pl.BlockSpec((1, D), lambda i: (0, 0))
    return pl.pallas_call(
        body, name=name, grid=(T // tm,),
        out_shape=(jax.ShapeDtypeStruct((T, D), F32), jax.ShapeDtypeStruct((T, D), _MXU_DTYPE),
                   jax.ShapeDtypeStruct((1, D), F32), jax.ShapeDtypeStruct((1, LANES), F32)),
        in_specs=[row, vec, row],
        out_specs=(row, row, vec, pl.BlockSpec((1, LANES), lambda i: (0, 0))),
        compiler_params=_params("arbitrary"),
    )(x, g, tgt)


def _outnorm_fwd(ya, yb, ga, gb, name):
    T, W = ya.shape
    tm = _tile(T, 512, 16)

    def body(ya_ref, yb_ref, ga_ref, gb_ref, o_ref):
        for k, (y_ref, g_ref) in enumerate(((ya_ref, ga_ref), (yb_ref, gb_ref))):
            yv = y_ref[...]
            r = lax.rsqrt(jnp.mean(yv * yv, axis=-1, keepdims=True) + EPS)
            o_ref[:, k * W:(k + 1) * W] = (yv * r * g_ref[...]).astype(o_ref.dtype)

    row = pl.BlockSpec((tm, W), lambda i: (i, 0))
    vec = pl.BlockSpec((1, W), lambda i: (0, 0))
    return pl.pallas_call(
        body, name=name, grid=(T // tm,),
        out_shape=jax.ShapeDtypeStruct((T, 2 * W), _MXU_DTYPE),
        in_specs=[row, row, vec, vec], out_specs=pl.BlockSpec((tm, 2 * W), lambda i: (i, 0)),
        compiler_params=_params("parallel"),
    )(ya, yb, ga, gb)


def _outnorm_bwd(dyn, ya, yb, ga, gb, name):
    T, W = ya.shape
    tm = _tile(T, 256, 16)

    def body(d_ref, ya_ref, yb_ref, ga_ref, gb_ref, dya_ref, dyb_ref, dga_ref, dgb_ref):
        @pl.when(pl.program_id(0) == 0)
        def _():
            dga_ref[...] = jnp.zeros_like(dga_ref)
            dgb_ref[...] = jnp.zeros_like(dgb_ref)

        for k, (y_ref, g_ref, dy_ref, dg_ref) in enumerate(
                ((ya_ref, ga_ref, dya_ref, dga_ref), (yb_ref, gb_ref, dyb_ref, dgb_ref))):
            dy, dgr = _rms_bwd_rows(d_ref[:, k * W:(k + 1) * W], y_ref[...], g_ref[...])
            dy_ref[...] = dy.astype(dy_ref.dtype)
            dg_ref[...] += jnp.sum(dgr, axis=0, keepdims=True)

    row = pl.BlockSpec((tm, W), lambda i: (i, 0))
    vec = pl.BlockSpec((1, W), lambda i: (0, 0))
    return pl.pallas_call(
        body, name=name, grid=(T // tm,),
        out_shape=(jax.ShapeDtypeStruct((T, W), _MXU_DTYPE), jax.ShapeDtypeStruct((T, W), _MXU_DTYPE),
                   jax.ShapeDtypeStruct((1, W), F32), jax.ShapeDtypeStruct((1, W), F32)),
        in_specs=[pl.BlockSpec((tm, 2 * W), lambda i: (i, 0)), row, row, vec, vec],
        out_specs=(row, row, vec, vec),
        compiler_params=_params("arbitrary"),
    )(dyn, ya, yb, ga, gb)


def _ffn_gu(h, wg_t, wu_t, name, jobs=()):
    T, D = h.shape
    Fp = wg_t.shape[0]
    tm = _tile(T, 1024, 16)
    tn = _tile(Fp, _F_TILE, LANES)

    def body(h_ref, wg_ref, wu_ref, g_ref, u_ref, a_ref):
        hv = h_ref[...]
        g = _dot_nt(hv, wg_ref[...])
        u = _dot_nt(hv, wu_ref[...])
        g_ref[...] = g.astype(g_ref.dtype)
        u_ref[...] = u.astype(u_ref.dtype)
        a_ref[...] = (g * jax.nn.sigmoid(g) * u).astype(a_ref.dtype)

    act = pl.BlockSpec((tm, tn), lambda n, m: (m, n))
    wsp = pl.BlockSpec((tn, D), lambda n, m: (n, 0))
    out = jax.ShapeDtypeStruct((T, Fp), _MXU_DTYPE)
    return _call(
        body, name=name, grid=(Fp // tn, T // tm), out_shape=(out, out, out),
        in_specs=[pl.BlockSpec((tm, D), lambda n, m: (m, 0)), wsp, wsp],
        out_specs=(act, act, act), args=(h, wg_t, wu_t), semantics=("parallel", "parallel"), jobs=jobs)


def _ffn_bwd_act(dxb, wd, G, U, name, jobs=()):
    T, D = dxb.shape
    Fp = wd.shape[0]
    tm = _tile(T, 1024, 16)
    tn = _tile(Fp, _F_TILE, LANES)

    def body(e_ref, wd_ref, g_ref, u_ref, dg_ref, du_ref):
        da = 0.5 * _dot_nt(e_ref[...], wd_ref[...])
        g = g_ref[...].astype(F32)
        u = u_ref[...].astype(F32)
        s = jax.nn.sigmoid(g)
        du_ref[...] = (da * (g * s)).astype(du_ref.dtype)
        dg_ref[...] = (da * u * (s * (1.0 + g * (1.0 - s)))).astype(dg_ref.dtype)

    act = pl.BlockSpec((tm, tn), lambda n, m: (m, n))
    out = jax.ShapeDtypeStruct((T, Fp), _MXU_DTYPE)
    return _call(
        body, name=name, grid=(Fp // tn, T // tm), out_shape=(out, out),
        in_specs=[pl.BlockSpec((tm, D), lambda n, m: (m, 0)), pl.BlockSpec((tn, D), lambda n, m: (n, 0)),
                  act, act],
        out_specs=(act, act), args=(dxb, wd, G, U), semantics=("parallel", "parallel"), jobs=jobs)


def _ffn_gate(h, wg_t, U, name, jobs=()):
    T, D = h.shape
    Fp = wg_t.shape[0]
    tm = _tile(T, 1024, 16)
    tn = _tile(Fp, _F_TILE, LANES)

    def body(h_ref, wg_ref, u_ref, g_ref, a_ref):
        g = _dot_nt(h_ref[...], wg_ref[...])
        g_ref[...] = g.astype(g_ref.dtype)
        a_ref[...] = (g * jax.nn.sigmoid(g) * u_ref[...].astype(F32)).astype(a_ref.dtype)

    act = pl.BlockSpec((tm, tn), lambda n, m: (m, n))
    out = jax.ShapeDtypeStruct((T, Fp), _MXU_DTYPE)
    return _call(
        body, name=name, grid=(Fp // tn, T // tm), out_shape=(out, out),
        in_specs=[pl.BlockSpec((tm, D), lambda n, m: (m, 0)), pl.BlockSpec((tn, D), lambda n, m: (n, 0)), act],
        out_specs=(act, act), args=(h, wg_t, U), semantics=("parallel", "parallel"), jobs=jobs)


def _mm_nt(a, b, out_dtype, name, jobs=()):
    M, K = a.shape
    N = b.shape[0]
    tm = _tile(M, 1024, 16)
    tn = _tile(N, 512, LANES)

    def body(a_ref, b_ref, o_ref):
        o_ref[...] = _dot_nt(a_ref[...], b_ref[...]).astype(o_ref.dtype)

    return _call(
        body, name=name, grid=(N // tn, M // tm), out_shape=jax.ShapeDtypeStruct((M, N), out_dtype),
        in_specs=[pl.BlockSpec((tm, K), lambda n, m: (m, 0)), pl.BlockSpec((tn, K), lambda n, m: (n, 0))],
        out_specs=pl.BlockSpec((tm, tn), lambda n, m: (m, n)), args=(a, b),
        semantics=("parallel", "parallel"), jobs=jobs)


_MM_OPERAND_BYTES = 26 * 1024 * 1024


def _k_tile(K, bytes_per_k, align):
    best = align
    for t in range(align, K + 1, align):
        if K % t == 0 and 2 * t * bytes_per_k <= _MM_OPERAND_BYTES:
            best = t
    return best


def _mm_nn(pairs, res, alpha, out_dtype, name, jobs=()):
    M, K = pairs[0][0].shape
    N = pairs[0][1].shape[1]
    n_pairs = len(pairs)
    tm = _tile(M, 1024, 16)
    tn = _tile(N, 1024, LANES)
    tk = _k_tile(K, n_pairs * (tm + tn) * pairs[0][0].dtype.itemsize, LANES)
    nk = K // tk

    def body(*refs):
        ab = refs[:2 * n_pairs]
        res_ref = refs[2 * n_pairs] if res is not None else None
        o_ref = refs[2 * n_pairs + (res is not None)]

        def finish(acc):
            out = alpha * acc
            if res_ref is not None:
                out = res_ref[...] + out
            o_ref[...] = out.astype(o_ref.dtype)

        part = _dot(ab[0][...], ab[1][...])
        for i in range(1, n_pairs):
            part = part + _dot(ab[2 * i][...], ab[2 * i + 1][...])
        if nk == 1:
            finish(part)
        else:
            acc_ref = refs[-1]
            k = pl.program_id(2)

            @pl.when(k == 0)
            def _():
                acc_ref[...] = part

            @pl.when(k > 0)
            def _():
                acc_ref[...] += part

            @pl.when(k == nk - 1)
            def _():
                finish(acc_ref[...])

    in_specs, args = [], []
    for a, b in pairs:
        in_specs += [pl.BlockSpec((tm, tk), lambda m, n, k: (m, k)), pl.BlockSpec((tk, tn), lambda m, n, k: (k, n))]
        args += [a, b]
    if res is not None:
        in_specs.append(pl.BlockSpec((tm, tn), lambda m, n, k: (m, n)))
        args.append(res)
    return _call(
        body, name=name, grid=(M // tm, N // tn, nk), out_shape=jax.ShapeDtypeStruct((M, N), out_dtype),
        in_specs=in_specs, out_specs=pl.BlockSpec((tm, tn), lambda m, n, k: (m, n)),
        scratch_shapes=[pltpu.VMEM((tm, tn), F32)] if nk > 1 else [], args=args,
        semantics=("parallel", "parallel", "arbitrary"), jobs=jobs)


def _mm_tn(a, b, alpha, out_dtype, name, jobs=()):
    K, M = a.shape
    N = b.shape[1]
    tm = _tile(M, 512, LANES)
    tn = _tile(N, 1024, LANES)

    def body(a_ref, b_ref, o_ref):
        o_ref[...] = (alpha * _dot_tn(a_ref[...], b_ref[...])).astype(o_ref.dtype)

    return _call(
        body, name=name, grid=(N // tn, M // tm), out_shape=jax.ShapeDtypeStruct((M, N), out_dtype),
        in_specs=[pl.BlockSpec((K, tm), lambda n, m: (0, m)), pl.BlockSpec((K, tn), lambda n, m: (0, n))],
        out_specs=pl.BlockSpec((tm, tn), lambda n, m: (m, n)), args=(a, b),
        semantics=("parallel", "parallel"), jobs=jobs)


def _half_masks():
    lane = lax.broadcasted_iota(jnp.int32, (1, LANES), 1)
    return (lane < HEAD_DIM, lane >= HEAD_DIM)


def _swap_halves(v):
    return pltpu.roll(v.astype(F32), HEAD_DIM, 1).astype(v.dtype)


def _swa_geometry(n):
    qi = lax.broadcasted_iota(jnp.int32, (WINDOW, 2 * WINDOW), 0)
    kp = lax.broadcasted_iota(jnp.int32, (WINDOW, 2 * WINDOW), 1)
    dist = (WINDOW + qi) - kp
    valid = (dist >= 0) & (dist < WINDOW) & ((n > 0) | (kp >= WINDOW))
    return dist.astype(F32), valid


def _swa_slope(h):
    return 2.0 ** (-8.0 * (h + 1) / N_SWA_HEADS)


def _swa_probs(qm, kx, sink, slope, distf, valid):
    s = _dot_nt(qm, kx) * ATT_SCALE - slope * distf
    s = jnp.where(valid, s, -1e30)
    m = jnp.maximum(jnp.max(s, axis=1, keepdims=True), sink)
    p = jnp.exp(s - m)
    e_sink = jnp.exp(sink - m)
    den = jnp.sum(p, axis=1, keepdims=True) + e_sink
    return p / den, e_sink / den


def _swa_specs(B, S):
    nb = S // WINDOW
    kcol = SWA_Q // SWA_KV
    cur = lambda b, n: (b * nb + n, kcol)
    prev = lambda b, n: (b * nb + jnp.maximum(n - 1, 0), kcol)
    curv = lambda b, n: (b * nb + n, kcol + 1)
    prevv = lambda b, n: (b * nb + jnp.maximum(n - 1, 0), kcol + 1)
    q_spec = pl.BlockSpec((WINDOW, SWA_Q), lambda b, n: (b * nb + n, 0))
    kv = [pl.BlockSpec((WINDOW, SWA_KV), f) for f in (prev, cur, prevv, curv)]
    sink_spec = pl.BlockSpec(memory_space=pltpu.SMEM)
    return nb, q_spec, kv, sink_spec


def _swa_kv_views(kp_ref, kc_ref, vp_ref, vc_ref, g):
    hm = _half_masks()
    c0 = (g // 2) * LANES
    k_all = jnp.concatenate([kp_ref[:, c0:c0 + LANES], kc_ref[:, c0:c0 + LANES]], axis=0)
    v_all = jnp.concatenate([vp_ref[:, c0:c0 + LANES], vc_ref[:, c0:c0 + LANES]], axis=0)
    b = g % 2
    ks, vs = [None, None], [None, None]
    ks[b], vs[b] = k_all, v_all
    ks[1 - b], vs[1 - b] = _swap_halves(k_all), _swap_halves(v_all)
    ks = [jnp.where(hm[a], ks[a], 0) for a in range(2)]
    vs = [jnp.where(hm[a], vs[a], 0) for a in range(2)]
    return ks, vs


def _swa_fwd(proj, sinks, B, S, name):
    T = B * S
    nb, q_spec, kv_specs, sink_spec = _swa_specs(B, S)

    def body(sink_ref, q_ref, kp_ref, kc_ref, vp_ref, vc_ref, y_ref):
        hm = _half_masks()
        distf, valid = _swa_geometry(pl.program_id(1))
        for g in range(N_SWA_KV):
            ks, vs = _swa_kv_views(kp_ref, kc_ref, vp_ref, vc_ref, g)
            for pp in (2 * g, 2 * g + 1):
                q_pair = q_ref[:, pp * LANES:(pp + 1) * LANES]
                o_pair = jnp.zeros((WINDOW, LANES), F32)
                for a in range(2):
                    h = 2 * pp + a
                    qm = jnp.where(hm[a], q_pair, 0)
                    p, _ = _swa_probs(qm, ks[a], sink_ref[0, h], _swa_slope(h), distf, valid)
                    o_pair = o_pair + _dot(p.astype(_MXU_DTYPE), vs[a])
                y_ref[:, pp * LANES:(pp + 1) * LANES] = o_pair

    return pl.pallas_call(
        body, name=name, grid=(B, nb), out_shape=jax.ShapeDtypeStruct((T, SWA_Q), F32),
        in_specs=[sink_spec, q_spec] + kv_specs,
        out_specs=pl.BlockSpec((WINDOW, SWA_Q), lambda b, n: (b * nb + n, 0)),
        compiler_params=_params("parallel", "parallel"),
    )(sinks, proj, proj, proj, proj, proj)


def _swa_bwd(proj, sinks, dya, B, S, name):
    T = B * S
    nb, q_spec, kv_specs, sink_spec = _swa_specs(B, S)

    def body(sink_ref, q_ref, kp_ref, kc_ref, vp_ref, vc_ref, do_ref,
             dq_ref, dk_ref, dv_ref, dsink_ref, dk_acc, dv_acc):
        b_id, n = pl.program_id(0), pl.program_id(1)
        hm = _half_masks()
        lane = lax.broadcasted_iota(jnp.int32, (1, LANES), 1)

        @pl.when((b_id == 0) & (n == 0))
        def _():
            dsink_ref[...] = jnp.zeros_like(dsink_ref)

        @pl.when(n == 0)
        def _():
            dk_acc[...] = jnp.zeros_like(dk_acc)
            dv_acc[...] = jnp.zeros_like(dv_acc)

        distf, valid = _swa_geometry(n)
        r_prev = pl.multiple_of(jnp.maximum(n - 1, 0) * WINDOW, WINDOW)
        r_cur = pl.multiple_of(n * WINDOW, WINDOW)
        dsink = jnp.zeros((1, LANES), F32)
        for g in range(N_SWA_KV):
            ks, vs = _swa_kv_views(kp_ref, kc_ref, vp_ref, vc_ref, g)
            dk_g = [jnp.zeros((2 * WINDOW, LANES), F32) for _ in range(2)]
            dv_g = [jnp.zeros((2 * WINDOW, LANES), F32) for _ in range(2)]
            for pp in (2 * g, 2 * g + 1):
                q_pair = q_ref[:, pp * LANES:(pp + 1) * LANES]
                do_pair = do_ref[:, pp * LANES:(pp + 1) * LANES]
                dq_pair = jnp.zeros((WINDOW, LANES), F32)
                for a in range(2):
                    h = 2 * pp + a
                    qm = jnp.where(hm[a], q_pair, 0)
                    dom = jnp.where(hm[a], do_pair, 0)
                    p, p_sink = _swa_probs(qm, ks[a], sink_ref[0, h], _swa_slope(h), distf, valid)
                    dp = _dot_nt(dom, vs[a])
                    delta = jnp.sum(p * dp, axis=1, keepdims=True)
                    ds = (p * (dp - delta) * ATT_SCALE).astype(_MXU_DTYPE)
                    dsink = dsink + jnp.where(lane == h, -jnp.sum(p_sink * delta), 0.0)
                    dq_pair = dq_pair + _dot(ds, ks[a])
                    dk_g[a] = dk_g[a] + _dot_tn(ds, qm)
                    dv_g[a] = dv_g[a] + _dot_tn(p.astype(_MXU_DTYPE), dom)
                dq_ref[:, pp * LANES:(pp + 1) * LANES] = dq_pair.astype(dq_ref.dtype)
            bsel = g % 2
            dk_t = dk_g[bsel] + pltpu.roll(dk_g[1 - bsel], HEAD_DIM, 1)
            dv_t = dv_g[bsel] + pltpu.roll(dv_g[1 - bsel], HEAD_DIM, 1)
            c0 = (g // 2) * LANES
            dk_acc[pl.ds(r_prev, WINDOW), c0:c0 + LANES] += dk_t[:WINDOW]
            dk_acc[pl.ds(r_cur, WINDOW), c0:c0 + LANES] += dk_t[WINDOW:]
            dv_acc[pl.ds(r_prev, WINDOW), c0:c0 + LANES] += dv_t[:WINDOW]
            dv_acc[pl.ds(r_cur, WINDOW), c0:c0 + LANES] += dv_t[WINDOW:]
        dsink_ref[...] += dsink

        @pl.when(n == nb - 1)
        def _():
            dk_ref[...] = dk_acc[...].astype(dk_ref.dtype)
            dv_ref[...] = dv_acc[...].astype(dv_ref.dtype)

    seq_kv = pl.BlockSpec((S, SWA_KV), lambda b, n: (b, 0))
    return pl.pallas_call(
        body, name=name, grid=(B, nb),
        out_shape=(jax.ShapeDtypeStruct((T, SWA_Q), _MXU_DTYPE), jax.ShapeDtypeStruct((T, SWA_KV), _MXU_DTYPE),
                   jax.ShapeDtypeStruct((T, SWA_KV), _MXU_DTYPE), jax.ShapeDtypeStruct((1, LANES), F32)),
        in_specs=[sink_spec, q_spec] + kv_specs + [pl.BlockSpec((WINDOW, SWA_Q), lambda b, n: (b * nb + n, 0))],
        out_specs=(pl.BlockSpec((WINDOW, SWA_Q), lambda b, n: (b * nb + n, 0)), seq_kv, seq_kv,
                   pl.BlockSpec((1, LANES), lambda b, n: (0, 0))),
        scratch_shapes=[pltpu.VMEM((S, SWA_KV), F32), pltpu.VMEM((S, SWA_KV), F32)],
        compiler_params=_params("arbitrary", "arbitrary"),
    )(sinks, proj, proj, proj, proj, proj, dya)


SB_TILE = 256
SB_HALF = 128


def _tri2(cond):
    j = lax.broadcasted_iota(jnp.int32, (2 * SB_HALF, SB_HALF), 0) & (SB_HALF - 1)
    s = lax.broadcasted_iota(jnp.int32, (2 * SB_HALF, SB_HALF), 1)
    return cond(j, s).astype(_MXU_DTYPE)


def _half_cumsums(x, tri2):
    out = []
    for h in range(2):
        xh = x[:, h * SB_HALF:(h + 1) * SB_HALF]
        hi = xh.astype(_MXU_DTYPE)
        lo = (xh - hi.astype(F32)).astype(_MXU_DTYPE)
        out.append(_dot(jnp.concatenate([hi, lo], axis=1), tri2))
    return out


def _log_sigmoid(z):
    return jnp.minimum(z, 0.0) - jnp.log(1.0 + jnp.exp(-jnp.abs(z)))


def _sb_specs(B, S):
    qb = (SWA_Q + 2 * SWA_KV) // LANES
    kb = qb + SB_W // LANES
    vb = kb + SB_W // LANES
    return [pl.BlockSpec((S, LANES), functools.partial(lambda b, p, c: (b, c + p), c=c)) for c in (qb, kb, vb)]


def _sb_fwd(proj, B, S, name, jobs=()):
    T = B * S
    tq = SB_TILE
    nq = S // tq

    def body(q_ref, k_ref, v_ref, y_ref, tot_ref):
        hm = _half_masks()
        ji = lax.broadcasted_iota(jnp.int32, (tq, tq), 0)
        si = lax.broadcasted_iota(jnp.int32, (tq, tq), 1)
        tri_after = _tri2(lambda j, s: j > s)
        causal = si < ji

        def q_loop(qi, carry):
            r0 = pl.multiple_of(qi * tq, tq)
            q_pair = q_ref[pl.ds(r0, tq), :] * ATT_SCALE
            qms = [jnp.where(hm[a], q_pair, 0) for a in range(2)]

            def tile(c0, state, diagonal):
                kk = k_ref[pl.ds(c0, tq), :]
                vv = v_ref[pl.ds(c0, tq), :]
                two = range(2)
                z = [_dot_nt(qms[a], kk) for a in two]
                lb = [_log_sigmoid(z[a]) for a in two]
                l1m = [jnp.where(causal, lb[a] - z[a], 0.0) if diagonal else lb[a] - z[a] for a in two]
                cum = [_half_cumsums(l1m[a], tri_after) for a in two]
                tot = [[cum[a][h][:, 0:1] + l1m[a][:, h * SB_HALF:h * SB_HALF + 1] for h in two] for a in two]
                after = [jnp.concatenate([cum[a][0] + (state[a][1] + tot[a][1]), cum[a][1] + state[a][1]], axis=1)
                         for a in two]
                att = [jnp.exp(lb[a] + after[a]) for a in two]
                if diagonal:
                    att = [jnp.where(causal, att[a], 0.0) for a in two]
                acc = [state[a][0] + _dot(att[a].astype(_MXU_DTYPE), jnp.where(hm[a], vv, 0)) for a in two]
                car = [state[a][1] + (tot[a][0] + tot[a][1]) for a in two]
                return tuple((acc[a], car[a]) for a in two)

            zero = (jnp.zeros((tq, LANES), F32), jnp.zeros((tq, 1), F32))
            state = tile(r0, (zero, zero), True)
            state = lax.fori_loop(
                0, qi, lambda it, st: tile(pl.multiple_of((qi - 1 - it) * tq, tq), st, False), state)
            y_ref[pl.ds(r0, tq), :] = state[0][0] + state[1][0]
            tot_ref[pl.ds(r0, tq), :] = jnp.where(hm[0], state[0][1], state[1][1])
            return carry

        lax.fori_loop(0, nq, q_loop, 0)

    out_spec = pl.BlockSpec((S, LANES), lambda b, p: (b, p))
    return _call(
        body, name=name, grid=(B, SB_W // LANES),
        out_shape=(jax.ShapeDtypeStruct((T, SB_W), F32), jax.ShapeDtypeStruct((T, SB_W), F32)),
        in_specs=_sb_specs(B, S), out_specs=(out_spec, out_spec), args=(proj, proj, proj),
        semantics=("parallel", "parallel"), jobs=jobs)


def _sb_bwd(proj, tot, dyb, B, S, name, jobs=()):
    T = B * S
    tq = SB_TILE
    nq = S // tq

    def body(q_ref, k_ref, v_ref, do_ref, tot_ref, dq_ref, dk_ref, dv_ref, dk_acc, dv_acc):
        hm = _half_masks()
        ji = lax.broadcasted_iota(jnp.int32, (tq, tq), 0)
        si = lax.broadcasted_iota(jnp.int32, (tq, tq), 1)
        tri_incl = _tri2(lambda j, s: j <= s)
        tri_excl = _tri2(lambda j, s: j < s)
        causal = si < ji
        dk_acc[...] = jnp.zeros_like(dk_acc)
        dv_acc[...] = jnp.zeros_like(dv_acc)

        def q_loop(qi, carry):
            r0 = pl.multiple_of(qi * tq, tq)
            q_pair = q_ref[pl.ds(r0, tq), :] * ATT_SCALE
            do_pair = do_ref[pl.ds(r0, tq), :]
            tot_pair = tot_ref[pl.ds(r0, tq), :]
            qms = [jnp.where(hm[a], q_pair, 0) for a in range(2)]
            doms = [jnp.where(hm[a], do_pair, 0) for a in range(2)]
            totals = [jnp.max(jnp.where(hm[a], tot_pair, -jnp.inf), axis=1, keepdims=True) for a in range(2)]

            def tile(c0, state, diagonal):
                kk = k_ref[pl.ds(c0, tq), :]
                vv = v_ref[pl.ds(c0, tq), :]
                ks = kk * ATT_SCALE
                two = range(2)
                last = SB_HALF - 1
                z = [_dot_nt(qms[a], kk) for a in two]
                d_att = [_dot_nt(doms[a], vv) for a in two]
                lb = [_log_sigmoid(z[a]) for a in two]
                l1m = [jnp.where(causal, lb[a] - z[a], 0.0) if diagonal else lb[a] - z[a] for a in two]
                cum = [_half_cumsums(l1m[a], tri_incl) for a in two]
                upto = [jnp.concatenate([cum[a][0] + state[a][1],
                                         cum[a][1] + (state[a][1] + cum[a][0][:, last:last + 1])], axis=1) for a in two]
                att = [jnp.exp(lb[a] + (totals[a] - upto[a])) for a in two]
                if diagonal:
                    att = [jnp.where(causal, att[a], 0.0) for a in two]
                d_log = [d_att[a] * att[a] for a in two]
                cumd = [_half_cumsums(d_log[a], tri_excl) for a in two]
                totd = [[cumd[a][h][:, last:last + 1] + d_log[a][:, h * SB_HALF + last:h * SB_HALF + last + 1]
                         for h in two] for a in two]
                before = [jnp.concatenate([cumd[a][0] + state[a][2], cumd[a][1] + (state[a][2] + totd[a][0])], axis=1)
                          for a in two]
                sig = [jnp.exp(lb[a]) for a in two]
                dz = [d_log[a] * (1.0 - sig[a]) - sig[a] * before[a] for a in two]
                if diagonal:
                    dz = [jnp.where(causal, dz[a], 0.0) for a in two]
                dzb = [dz[a].astype(_MXU_DTYPE) for a in two]
                dq = [state[a][0] + _dot(dzb[a], jnp.where(hm[a], ks, 0)) for a in two]
                dk_acc[pl.ds(c0, tq), :] += _dot_tn(dzb[0], qms[0]) + _dot_tn(dzb[1], qms[1])
                dv_acc[pl.ds(c0, tq), :] += (_dot_tn(att[0].astype(_MXU_DTYPE), doms[0])
                                             + _dot_tn(att[1].astype(_MXU_DTYPE), doms[1]))
                cp = [upto[a][:, tq - 1:tq] for a in two]
                cq = [state[a][2] + (totd[a][0] + totd[a][1]) for a in two]
                return tuple((dq[a], cp[a], cq[a]) for a in two)

            zero_col = jnp.zeros((tq, 1), F32)
            zero = (jnp.zeros((tq, LANES), F32), zero_col, zero_col)
            state = lax.fori_loop(0, qi, lambda kj, st: tile(pl.multiple_of(kj * tq, tq), st, False), (zero, zero))
            state = tile(r0, state, True)
            dq_ref[pl.ds(r0, tq), :] = (state[0][0] + state[1][0]).astype(dq_ref.dtype)
            return carry

        lax.fori_loop(0, nq, q_loop, 0)
        dk_ref[...] = dk_acc[...].astype(dk_ref.dtype)
        dv_ref[...] = dv_acc[...].astype(dv_ref.dtype)

    pair = pl.BlockSpec((S, LANES), lambda b, p: (b, p))
    out = jax.ShapeDtypeStruct((T, SB_W), _MXU_DTYPE)
    return _call(
        body, name=name, grid=(B, SB_W // LANES), out_shape=(out, out, out),
        in_specs=_sb_specs(B, S) + [pair, pair], out_specs=(pair, pair, pair),
        scratch_shapes=[pltpu.VMEM((S, LANES), F32), pltpu.VMEM((S, LANES), F32)],
        args=(proj, proj, proj, dyb, tot), semantics=("parallel", "parallel"), jobs=jobs)


def _layer_step(x, tgt, B, S, small, comm):
    run, big, part = comm.run, comm.big, comm.partial
    ffn1_w, ffn2_w = ("ffn1_down", "ffn1_gate", "ffn1_up"), ("ffn2_down", "ffn2_gate", "ffn2_up")

    h1 = run(_rms_fwd, x, small["ffn1_norm"], "ffn1_rms", ag=("ffn1_up",))
    U1 = run(_mm_nt, h1, big["ffn1_up"], _MXU_DTYPE, "ffn1_up", ag=("ffn1_gate",))
    G1, A1 = run(_ffn_gate, h1, big["ffn1_gate"], U1, "ffn1_gate", ag=("ffn1_down",))
    x1 = run(_mm_nn, [(A1, big["ffn1_down"])], x, 0.5, F32, "ffn1_down", ag=("w_in", "w_out"))
    h2 = run(_rms_fwd, x1, small["mix_norm"], "mix_rms")
    proj = run(_mm_nt, h2, big["w_in"], _MXU_DTYPE, "in_proj")
    ya = _swa_fwd(proj, small["swa_sinks"], B, S, "swa_fwd")
    yb, tot = run(_sb_fwd, proj, B, S, "sb_fwd", ag=("ffn2_gate", "ffn2_up", "ffn2_down"))
    yn = _outnorm_fwd(ya, yb, small["swa_out_norm"], small["sb_out_norm"], "out_norm")
    x2 = run(_mm_nn, [(yn, big["w_out"])], x1, 1.0, F32, "out_proj")
    h3 = run(_rms_fwd, x2, small["ffn2_norm"], "ffn2_rms")
    G2, U2, A2 = run(_ffn_gu, h3, big["ffn2_gate"], big["ffn2_up"], "ffn2_gate_up")
    x3 = run(_mm_nn, [(A2, big["ffn2_down"])], x2, 0.5, F32, "ffn2_down")

    dx3, dx3b, d_final, loss = _loss_head(x3, small["final_norm"], tgt, "loss_head")

    dG2, dU2 = run(_ffn_bwd_act, dx3b, big["ffn2_down"], G2, U2, "ffn2_bwd_act")
    part["ffn2_down"] = run(_mm_tn, A2, dx3b, 0.5, _WIRE_DTYPE, "ffn2_dw_down")
    part["ffn2_gate"] = run(_mm_tn, dG2, h3, 1.0, _WIRE_DTYPE, "ffn2_dw_gate")
    part["ffn2_up"] = run(_mm_tn, dU2, h3, 1.0, _WIRE_DTYPE, "ffn2_dw_up")
    dh3 = run(_mm_nn, [(dG2, big["ffn2_gate"]), (dU2, big["ffn2_up"])], None, 1.0, F32, "ffn2_dh", rs1=ffn2_w)
    dx2, dx2b, d_g2 = _rms_bwd(dh3, x2, small["ffn2_norm"], dx3, "ffn2_rms_bwd")

    part["w_out"] = run(_mm_tn, yn, dx2b, 1.0, _WIRE_DTYPE, "dw_out")
    dyn = run(_mm_nt, dx2b, big["w_out"], F32, "out_proj_bwd")
    dya, dyb, d_ga, d_gb = _outnorm_bwd(dyn, ya, yb, small["swa_out_norm"], small["sb_out_norm"], "out_norm_bwd")
    dqa, dka, dva, d_sinks = _swa_bwd(proj, small["swa_sinks"], dya, B, S, "swa_bwd")
    dqb, dkb, dvb = run(_sb_bwd, proj, tot, dyb, B, S, "sb_bwd", rs2=ffn2_w)
    dproj = jnp.concatenate([dqa, dka, dva, dqb, dkb, dvb], axis=1)
    part["w_in"] = run(_mm_tn, dproj, h2, 1.0, _WIRE_DTYPE, "dw_in")
    dh2 = run(_mm_nn, [(dproj, big["w_in"])], None, 1.0, F32, "in_proj_bwd", rs1=("w_in", "w_out"))
    dx1, dx1b, d_gm = _rms_bwd(dh2, x1, small["mix_norm"], dx2, "mix_rms_bwd")

    dG1, dU1 = run(_ffn_bwd_act, dx1b, big["ffn1_down"], G1, U1, "ffn1_bwd_act", rs2=("w_in", "w_out"))
    part["ffn1_down"] = run(_mm_tn, A1, dx1b, 0.5, _WIRE_DTYPE, "ffn1_dw_down")
    part["ffn1_gate"] = run(_mm_tn, dG1, h1, 1.0, _WIRE_DTYPE, "ffn1_dw_gate", rs1=("ffn1_down",))
    part["ffn1_up"] = run(_mm_tn, dU1, h1, 1.0, _WIRE_DTYPE, "ffn1_dw_up", rs1=("ffn1_gate",), rs2=("ffn1_down",))
    dh1 = run(_mm_nn, [(dG1, big["ffn1_gate"])], None, 1.0, F32, "ffn1_dh_gate", rs1=("ffn1_up",), rs2=("ffn1_gate",))
    dh1 = run(_mm_nn, [(dU1, big["ffn1_up"])], dh1, 1.0, F32, "ffn1_dh_up", rs2=("ffn1_up",))
    gx, _, d_g1 = _rms_bwd(dh1, x, small["ffn1_norm"], dx1, "ffn1_rms_bwd")

    d_small = {"ffn1_norm": d_g1, "mix_norm": d_gm, "swa_sinks": d_sinks[:, :N_SWA_HEADS], "swa_out_norm": d_ga,
               "sb_out_norm": d_gb, "ffn2_norm": d_g2, "final_norm": d_final}
    return loss, gx, d_small


MESH = pl.DeviceIdType.MESH
BIG_NAMES = ("ffn1_gate", "ffn1_up", "ffn1_down", "w_in", "w_out", "ffn2_gate", "ffn2_up", "ffn2_down")
_COMM_PARAMS = pltpu.CompilerParams(has_side_effects=True)


def _place():
    x, y, c = lax.axis_index("x"), lax.axis_index("y"), lax.axis_index("c")
    other_chips = [(1 - x, y), (x, 1 - y), (1 - x, 1 - y)]
    return x, y, c, other_chips


def _padded_rows(rows):
    full = N_DEV * rows
    return -(-full // _F_TILE) * _F_TILE


def _ag_job(shards):
    nw = len(shards)
    D = shards[0].shape[1]
    rows_w = [s.shape[0] for s in shards]
    full_w = [_padded_rows(r) for r in rows_w]
    pad_w = [f - N_DEV * r for f, r in zip(full_w, rows_w)]
    max_pad = max(max(pad_w), 16)

    class Plan:
        def __init__(self, ins, outs, scratch):
            zbuf, send_sems, recv_sems, local_sems, zero_sems = scratch
            x, y, c, chips = _place()
            me, sibling = (x, y, c), (x, y, 1 - c)

            def rows(w, px, py, pc):
                start = pl.multiple_of((4 * px + 2 * py + pc) * rows_w[w], 16)
                return outs[w].at[pl.ds(start, rows_w[w]), :]

            def copy(w, k, block, to, src=None):
                return pltpu.make_async_remote_copy(
                    src_ref=rows(w, *block) if src is None else src, dst_ref=rows(w, *block),
                    send_sem=send_sems.at[w, k], recv_sem=recv_sems.at[w, k], device_id=to, device_id_type=MESH)

            self.zbuf = zbuf
            self.local = [pltpu.make_async_copy(zbuf.at[pl.ds(0, pad_w[w]), :],
                                                outs[w].at[pl.ds(N_DEV * rows_w[w], pad_w[w]), :], zero_sems.at[w])
                          for w in range(nw) if pad_w[w]]
            self.local += [pltpu.make_async_copy(ins[w], rows(w, *me), local_sems.at[w]) for w in range(nw)]
            self.first = [[copy(w, 0, me, sibling, src=ins[w])]
                          + [copy(w, 1 + j, me, (*chip, c), src=ins[w]) for j, chip in enumerate(chips)]
                          for w in range(nw)]
            self.arrive = [[copy(w, 1 + j, (*chip, c), me) for j, chip in enumerate(chips)] for w in range(nw)]
            self.passed = [[copy(w, 4 + j, (*chip, c), sibling) for j, chip in enumerate(chips)] for w in range(nw)]
            self.from_sibling = [[copy(w, 0, sibling, me)]
                                 + [copy(w, 4 + j, (*chip, 1 - c), me) for j, chip in enumerate(chips)]
                                 for w in range(nw)]

    def start(ins, outs, scratch):
        plan = Plan(ins, outs, scratch)
        plan.zbuf[...] = jnp.zeros_like(plan.zbuf)
        for cp in plan.local:
            cp.start()
        for w in range(nw):
            for cp in plan.first[w]:
                cp.start()

    def mid(ins, outs, scratch):
        plan = Plan(ins, outs, scratch)
        for w in range(nw):
            for arrived, onward in zip(plan.arrive[w], plan.passed[w]):
                arrived.wait_recv()
                onward.start()

    def finish(ins, outs, scratch):
        plan = Plan(ins, outs, scratch)
        for w in range(nw):
            for cp in plan.from_sibling[w]:
                cp.wait_recv()
        for w in range(nw):
            for cp in plan.first[w] + plan.passed[w]:
                cp.wait_send()
        for cp in plan.local:
            cp.wait()

    return _Job(
        ins=shards, out_shape=[jax.ShapeDtypeStruct((f, D), s.dtype) for f, s in zip(full_w, shards)],
        scratch=[pltpu.VMEM((max_pad, D), shards[0].dtype), pltpu.SemaphoreType.DMA((nw, 7)),
                 pltpu.SemaphoreType.DMA((nw, 7)), pltpu.SemaphoreType.DMA((nw,)), pltpu.SemaphoreType.DMA((nw,))],
        start=start, mid=mid, finish=finish)


def _rs1_job(partials, rows_w):
    nw = len(partials)
    D = partials[0].shape[1]

    def copies(ins, outs, scratch):
        send_sems, recv_sems = scratch
        x, y, c, _ = _place()
        out = []
        for w in range(nw):
            r = rows_w[w]
            for q in range(4):
                src = ins[w].at[pl.ds(pl.multiple_of((2 * q + 1 - c) * r, 16), r), :]
                out.append(pltpu.make_async_remote_copy(
                    src_ref=src, dst_ref=outs[w].at[pl.ds(q * r, r), :], send_sem=send_sems.at[w, q],
                    recv_sem=recv_sems.at[w, q], device_id=(x, y, 1 - c), device_id_type=MESH))
        return out

    def start(ins, outs, scratch):
        for cp in copies(ins, outs, scratch):
            cp.start()

    def finish(ins, outs, scratch):
        for cp in copies(ins, outs, scratch):
            cp.wait()

    return _Job(
        ins=partials, out_shape=[jax.ShapeDtypeStruct((4 * r, D), p.dtype) for r, p in zip(rows_w, partials)],
        scratch=[pltpu.SemaphoreType.DMA((nw, 4)), pltpu.SemaphoreType.DMA((nw, 4))], start=start, finish=finish)


def _pair_sum(partial, from_sibling, rows, core, name):
    D = partial.shape[1]

    def body(core_ref, p_ref, s_ref, o_ref):
        o_ref[...] = (p_ref[...].astype(F32) + s_ref[...].astype(F32)).astype(o_ref.dtype)

    grid_spec = pltpu.PrefetchScalarGridSpec(
        num_scalar_prefetch=1, grid=(4,),
        in_specs=[pl.BlockSpec((rows, D), lambda q, core_ref: (2 * q + core_ref[0], 0)),
                  pl.BlockSpec((rows, D), lambda q, core_ref: (q, 0))],
        out_specs=pl.BlockSpec((rows, D), lambda q, core_ref: (q, 0)))
    return pl.pallas_call(
        body, name=name, grid_spec=grid_spec, out_shape=jax.ShapeDtypeStruct((4 * rows, D), partial.dtype),
        compiler_params=_params("arbitrary"),
    )(core, partial, from_sibling)


def _rs2_job(chip_sums, rows_w):
    nw = len(chip_sums)

    def copies(ins, outs, scratch):
        send_sems, recv_sems, local_sems = scratch
        x, y, c, chips = _place()
        my_chip = 2 * x + y
        out = []
        for w in range(nw):
            r = rows_w[w]
            mine = pl.ds(pl.multiple_of(my_chip * r, 16), r)
            out.append(pltpu.make_async_copy(ins[w].at[mine, :], outs[w].at[mine, :], local_sems.at[w]))
            for j, (qx, qy) in enumerate(chips):
                src = ins[w].at[pl.ds(pl.multiple_of((2 * qx + qy) * r, 16), r), :]
                out.append(pltpu.make_async_remote_copy(
                    src_ref=src, dst_ref=outs[w].at[mine, :], send_sem=send_sems.at[w, j],
                    recv_sem=recv_sems.at[w, j], device_id=(qx, qy, c), device_id_type=MESH))
        return out

    def start(ins, outs, scratch):
        for cp in copies(ins, outs, scratch):
            cp.start()

    def finish(ins, outs, scratch):
        for cp in copies(ins, outs, scratch):
            cp.wait()

    return _Job(
        ins=chip_sums, out_shape=[jax.ShapeDtypeStruct(s.shape, s.dtype) for s in chip_sums],
        scratch=[pltpu.SemaphoreType.DMA((nw, 3)), pltpu.SemaphoreType.DMA((nw, 3)), pltpu.SemaphoreType.DMA((nw,))],
        start=start, finish=finish)


class _Comm:
    def __init__(self, shards):
        self.shards = shards
        self.rows = {n: s.shape[0] for n, s in shards.items()}
        self.core = lax.axis_index("c").astype(jnp.int32).reshape(1)
        self.big, self.partial, self.chip_sums, self.slots = {}, {}, {}, {}

    def run(self, fn, *args, ag=(), rs1=(), rs2=()):
        jobs = []
        if ag:
            jobs.append(_ag_job([self.shards[n] for n in ag]))
        if rs1:
            jobs.append(_rs1_job([self.partial[n] for n in rs1], [self.rows[n] for n in rs1]))
        if rs2:
            jobs.append(_rs2_job([self.chip_sums[n] for n in rs2], [self.rows[n] for n in rs2]))
        out, job_res = fn(*args, jobs=jobs)
        job_res = iter(job_res)
        if ag:
            self.big.update(zip(ag, next(job_res)))
        if rs1:
            for n, got in zip(rs1, next(job_res)):
                self.chip_sums[n] = _pair_sum(self.partial[n], got, self.rows[n], self.core, "pair_sum_" + n)
        if rs2:
            self.slots.update(zip(rs2, next(job_res)))
        return out


SMALL_ROWS = 88


def _small_allreduce(vec):
    def body(v_ref, o_ref, gather, send_sems, recv_sems):
        x, y, c, _ = _place()
        my_id = 4 * x + 2 * y + c
        gather[my_id] = v_ref[...]
        copies = []
        for r in range(1, N_DEV):
            peer = (x ^ (r >> 2), y ^ ((r >> 1) & 1), c ^ (r & 1))
            cp = pltpu.make_async_remote_copy(src_ref=v_ref, dst_ref=gather.at[my_id], send_sem=send_sems.at[r - 1],
                                              recv_sem=recv_sems.at[r - 1], device_id=peer, device_id_type=MESH)
            cp.start()
            copies.append(cp)
        for cp in copies:
            cp.wait()
        acc = gather[0]
        for d in range(1, N_DEV):
            acc = acc + gather[d]
        o_ref[...] = acc

    vm = pl.BlockSpec(memory_space=pltpu.VMEM)
    return pl.pallas_call(
        body, name="small_allreduce", out_shape=jax.ShapeDtypeStruct(vec.shape, F32),
        in_specs=[vm], out_specs=vm,
        scratch_shapes=[pltpu.VMEM((N_DEV,) + vec.shape, F32), pltpu.SemaphoreType.DMA((N_DEV - 1,)),
                        pltpu.SemaphoreType.DMA((N_DEV - 1,))],
        compiler_params=_COMM_PARAMS,
    )(vec)


def _adamw_update(w, g, m, v):
    nm = ADAM_B1 * m + (1.0 - ADAM_B1) * g
    nv = ADAM_B2 * v + (1.0 - ADAM_B2) * jnp.square(g)
    m_hat = nm / (1.0 - ADAM_B1 ** ADAM_STEP)
    v_hat = nv / (1.0 - ADAM_B2 ** ADAM_STEP)
    return -ADAM_LR * (m_hat / (jnp.sqrt(v_hat) + ADAM_EPS) + ADAM_WD * w), nm, nv


def _adamw(w, g, m, v, name):
    R, C = w.shape
    tr = _tile(R, 256, 8)

    def body(w_ref, g_ref, m_ref, v_ref, d_ref, nm_ref, nv_ref):
        d_ref[...], nm_ref[...], nv_ref[...] = _adamw_update(w_ref[...], g_ref[...], m_ref[...], v_ref[...])

    spec = pl.BlockSpec((tr, C), lambda i: (i, 0))
    out = jax.ShapeDtypeStruct((R, C), F32)
    return pl.pallas_call(
        body, name=name, grid=(R // tr,), out_shape=(out, out, out),
        in_specs=[spec] * 4, out_specs=(spec, spec, spec),
        compiler_params=_params("parallel"),
    )(w, g, m, v)


def _adamw_slots(w, slots, m, v, name):
    R, C = w.shape
    tc = _tile(C, 512, LANES)

    def body(w_ref, s_ref, m_ref, v_ref, g_ref, d_ref, nm_ref, nv_ref):
        g = s_ref[0].astype(F32)
        for q in range(1, 4):
            g = g + s_ref[q].astype(F32)
        g_ref[...] = g
        d_ref[...], nm_ref[...], nv_ref[...] = _adamw_update(w_ref[...], g, m_ref[...], v_ref[...])

    spec = pl.BlockSpec((R, tc), lambda j: (0, j))
    out = jax.ShapeDtypeStruct((R, C), F32)
    return pl.pallas_call(
        body, name=name, grid=(C // tc,), out_shape=(out, out, out, out),
        in_specs=[spec, pl.BlockSpec((4, R, tc), lambda j: (0, 0, j)), spec, spec], out_specs=(spec, spec, spec, spec),
        compiler_params=_params("parallel"),
    )(w, slots, m, v)


WEIGHT_NAMES = ("ffn1_norm", "ffn1_w_gate", "ffn1_w_up", "ffn1_w_down", "mix_norm", "w_in", "swa_sinks",
                "swa_out_norm", "sb_out_norm", "w_out", "ffn2_norm", "ffn2_w_gate", "ffn2_w_up", "ffn2_w_down",
                "final_norm")
SMALL_NAMES = ("ffn1_norm", "mix_norm", "swa_sinks", "swa_out_norm", "sb_out_norm", "ffn2_norm", "final_norm")
BIG_ARGS = {"ffn1_gate": ("ffn1_w_gate", True), "ffn1_up": ("ffn1_w_up", True), "ffn1_down": ("ffn1_w_down", False),
            "w_in": ("w_in", True), "w_out": ("w_out", False), "ffn2_gate": ("ffn2_w_gate", True),
            "ffn2_up": ("ffn2_w_up", True), "ffn2_down": ("ffn2_w_down", False)}


def _pack_small(parts):
    padded = [jnp.pad(p.reshape(1, -1), ((0, 0), (0, -p.size % LANES))) for p in parts]
    flat = jnp.concatenate(padded, axis=1)
    flat = jnp.pad(flat, ((0, 0), (0, SMALL_ROWS * LANES - flat.shape[1])))
    return flat.reshape(SMALL_ROWS, LANES)


def _unpack_small(block, shapes):
    flat = block.reshape(-1)
    out, off = [], 0
    for shp in shapes:
        n = 1
        for s in shp:
            n *= s
        out.append(flat[off:off + n].reshape(shp))
        off += n + (-n % LANES)
    return out


def kernel(x, ffn1_norm, ffn1_w_gate, ffn1_w_up, ffn1_w_down, mix_norm, w_in, swa_sinks, swa_out_norm, sb_out_norm, w_out, ffn2_norm, ffn2_w_gate, ffn2_w_up, ffn2_w_down, final_norm, loss_target, m_ffn1_norm, m_ffn1_w_gate, m_ffn1_w_up, m_ffn1_w_down, m_mix_norm, m_w_in, m_swa_sinks, m_swa_out_norm, m_sb_out_norm, m_w_out, m_ffn2_norm, m_ffn2_w_gate, m_ffn2_w_up, m_ffn2_w_down, m_final_norm, v_ffn1_norm, v_ffn1_w_gate, v_ffn1_w_up, v_ffn1_w_down, v_mix_norm, v_w_in, v_swa_sinks, v_swa_out_norm, v_sb_out_norm, v_w_out, v_ffn2_norm, v_ffn2_w_gate, v_ffn2_w_up, v_ffn2_w_down, v_final_norm):
    args = dict(locals())
    B, S, D = x.shape
    T = B * S
    weights = {n: args[n] for n in WEIGHT_NAMES}
    mom_m = {n: args["m_" + n] for n in WEIGHT_NAMES}
    mom_v = {n: args["v_" + n] for n in WEIGHT_NAMES}

    shards = {}
    for name in BIG_NAMES:
        arg, transposed = BIG_ARGS[name]
        w2 = weights[arg][0]
        shards[name] = (w2.T if transposed else w2).astype(_WIRE_DTYPE)
    comm = _Comm(shards)
    small = {n: weights[n].reshape(1, -1) for n in SMALL_NAMES}

    loss, gx, d_small = _layer_step(x.reshape(T, D), loss_target.reshape(T, D), B, S, small, comm)

    small_shapes = [(1, 1)] + [d_small[n].shape for n in SMALL_NAMES]
    reduced = _small_allreduce(_pack_small([loss[:, :1]] + [d_small[n] for n in SMALL_NAMES]))
    red = _unpack_small(reduced, small_shapes)
    loss_out = red[0].reshape(())
    g_small = dict(zip(SMALL_NAMES, red[1:]))

    grads, deltas, new_m, new_v = {}, {}, {}, {}
    for name in BIG_NAMES:
        arg, transposed = BIG_ARGS[name]
        to_rows = (lambda t: t[0].T) if transposed else (lambda t: t[0])
        back = (lambda t: t.T[None]) if transposed else (lambda t: t[None])
        slots = comm.slots[name].reshape(4, comm.rows[name], D)
        res = _adamw_slots(to_rows(weights[arg]), slots, to_rows(mom_m[arg]), to_rows(mom_v[arg]), "adamw_" + name)
        grads[arg], deltas[arg], new_m[arg], new_v[arg] = [back(t) for t in res]
    shapes1 = [(1, weights[n].size) for n in SMALL_NAMES]
    packed = [_pack_small([t[n].reshape(1, -1) for n in SMALL_NAMES]) for t in (weights, g_small, mom_m, mom_v)]
    upd = _adamw(*packed, "adamw_small")
    for tgt_dict, block in zip((deltas, new_m, new_v), upd):
        for n, val in zip(SMALL_NAMES, _unpack_small(block, shapes1)):
            tgt_dict[n] = val.reshape(weights[n].shape)
    for n in SMALL_NAMES:
        grads[n] = g_small[n].reshape(weights[n].shape)

    return (loss_out, gx.reshape(B, S, D), *[grads[n] for n in WEIGHT_NAMES], *[deltas[n] for n in WEIGHT_NAMES],
            *[new_m[n] for n in WEIGHT_NAMES], *[new_v[n] for n in WEIGHT_NAMES])
```

```python
import functools

import jax
import jax.numpy as jnp
from jax import lax
from jax.experimental import pallas as pl
from jax.experimental.pallas import tpu as pltpu

F32 = jnp.float32
_MXU_DTYPE = jnp.bfloat16
_WIRE_DTYPE = jnp.bfloat16

EPS = 1e-6
HEAD_DIM = 64
N_SWA_HEADS = 16
N_SWA_KV = 4
N_SB_HEADS = 16
WINDOW = 128
SWA_Q = N_SWA_HEADS * HEAD_DIM
SWA_KV = N_SWA_KV * HEAD_DIM
SB_W = N_SB_HEADS * HEAD_DIM
IN_W = SWA_Q + 2 * SWA_KV + 3 * SB_W
LANES = 128
ATT_SCALE = HEAD_DIM ** -0.5

ADAM_LR = 0.001
ADAM_B1 = 0.9
ADAM_B2 = 0.999
ADAM_EPS = 1e-08
ADAM_WD = 0.01
ADAM_STEP = 10

N_DEV = 8
_VMEM_LIMIT_BYTES = 56 * 1024 * 1024
_F_TILE = 512


def _params(*semantics):
    return pltpu.CompilerParams(dimension_semantics=semantics, vmem_limit_bytes=_VMEM_LIMIT_BYTES)


def _tile(n, pref, align):
    t = min(n, pref)
    t -= t % align
    while t >= align:
        if n % t == 0:
            return t
        t -= align
    return n


def _dot(a, b):
    return lax.dot_general(a, b, (((1,), (0,)), ((), ())), preferred_element_type=F32)


def _dot_nt(a, b):
    return lax.dot_general(a, b, (((1,), (1,)), ((), ())), preferred_element_type=F32)


def _dot_tn(a, b):
    return lax.dot_general(a, b, (((0,), (0,)), ((), ())), preferred_element_type=F32)


class _Job:
    def __init__(self, ins, out_shape, scratch, start, finish, mid=None):
        self.ins, self.out_shape, self.scratch = list(ins), list(out_shape), list(scratch)
        self.start, self.mid, self.finish = start, mid, finish


_JOB_MID_FRACTION = 0.8


def _call(body, *, name, grid, in_specs, out_specs, out_shape, args, semantics, scratch_shapes=(), jobs=()):
    single = not isinstance(out_shape, (tuple, list))
    if not jobs:
        res = pl.pallas_call(body, name=name, grid=grid, in_specs=list(in_specs), out_specs=out_specs,
                             out_shape=out_shape, scratch_shapes=list(scratch_shapes),
                             compiler_params=_params(*semantics))(*args)
        return res, []
    base_out = [out_shape] if single else list(out_shape)
    base_out_specs = [out_specs] if single else list(out_specs)
    n_in, n_out, n_scr = len(args), len(base_out), len(scratch_shapes)
    any_spec = pl.BlockSpec(memory_space=pl.ANY)
    total = 1
    for g in grid:
        total *= g
    mid_step = min(total - 1, int(total * _JOB_MID_FRACTION))

    def wrapped(*refs):
        pos = n_in
        job_ins = []
        for job in jobs:
            job_ins.append(refs[pos:pos + len(job.ins)])
            pos += len(job.ins)
        outs = refs[pos:pos + n_out]
        pos += n_out
        job_outs = []
        for job in jobs:
            job_outs.append(refs[pos:pos + len(job.out_shape)])
            pos += len(job.out_shape)
        scr = refs[pos:pos + n_scr]
        pos += n_scr
        job_scr = []
        for job in jobs:
            job_scr.append(refs[pos:pos + len(job.scratch)])
            pos += len(job.scratch)
        step = pl.program_id(0)
        for d in range(1, len(grid)):
            step = step * grid[d] + pl.program_id(d)

        @pl.when(step == 0)
        def _():
            for job, ji, jo, js in zip(jobs, job_ins, job_outs, job_scr):
                job.start(ji, jo, js)

        @pl.when(step == mid_step)
        def _():
            for job, ji, jo, js in zip(jobs, job_ins, job_outs, job_scr):
                if job.mid is not None:
                    job.mid(ji, jo, js)

        body(*refs[:n_in], *outs, *scr)

        @pl.when(step == total - 1)
        def _():
            for job, ji, jo, js in zip(jobs, job_ins, job_outs, job_scr):
                job.finish(ji, jo, js)

    all_args = list(args) + [a for job in jobs for a in job.ins]
    all_in_specs = list(in_specs) + [any_spec for job in jobs for _ in job.ins]
    all_out_shape = base_out + [s for job in jobs for s in job.out_shape]
    all_out_specs = base_out_specs + [any_spec for job in jobs for _ in job.out_shape]
    all_scratch = list(scratch_shapes) + [s for job in jobs for s in job.scratch]
    res = pl.pallas_call(
        wrapped, name=name, grid=grid, in_specs=all_in_specs, out_specs=tuple(all_out_specs),
        out_shape=tuple(all_out_shape), scratch_shapes=all_scratch,
        compiler_params=pltpu.CompilerParams(dimension_semantics=("arbitrary",) * len(grid),
                                             vmem_limit_bytes=_VMEM_LIMIT_BYTES, has_side_effects=True),
    )(*all_args)
    base = res[0] if single else tuple(res[:n_out])
    job_res, pos = [], n_out
    for job in jobs:
        job_res.append(tuple(res[pos:pos + len(job.out_shape)]))
        pos += len(job.out_shape)
    return base, job_res


def _rms_fwd(x, g, name, jobs=()):
    T, D = x.shape
    tm = _tile(T, 512, 16)

    def body(x_ref, g_ref, o_ref):
        xv = x_ref[...]
        r = lax.rsqrt(jnp.mean(xv * xv, axis=-1, keepdims=True) + EPS)
        o_ref[...] = (xv * r * g_ref[...]).astype(o_ref.dtype)

    return _call(
        body, name=name, grid=(T // tm,),
        out_shape=jax.ShapeDtypeStruct((T, D), _MXU_DTYPE),
        in_specs=[pl.BlockSpec((tm, D), lambda i: (i, 0)), pl.BlockSpec((1, D), lambda i: (0, 0))],
        out_specs=pl.BlockSpec((tm, D), lambda i: (i, 0)), args=(x, g), semantics=("parallel",), jobs=jobs)


def _rms_bwd_rows(dh, xv, g):
    r = lax.rsqrt(jnp.mean(xv * xv, axis=-1, keepdims=True) + EPS)
    xhat = xv * r
    u = dh * g
    dx = r * (u - xhat * jnp.mean(u * xhat, axis=-1, keepdims=True))
    return dx, dh * xhat


def _rms_bwd(dh, x, g, dres, name):
    T, D = x.shape
    tm = _tile(T, 256, 16)

    def body(dh_ref, x_ref, g_ref, dres_ref, dx_ref, dxb_ref, dg_ref):
        @pl.when(pl.program_id(0) == 0)
        def _():
            dg_ref[...] = jnp.zeros_like(dg_ref)

        dx, dgr = _rms_bwd_rows(dh_ref[...], x_ref[...], g_ref[...])
        dx = dres_ref[...] + dx
        dx_ref[...] = dx
        dxb_ref[...] = dx.astype(dxb_ref.dtype)
        dg_ref[...] += jnp.sum(dgr, axis=0, keepdims=True)

    row = pl.BlockSpec((tm, D), lambda i: (i, 0))
    vec = pl.BlockSpec((1, D), lambda i: (0, 0))
    return pl.pallas_call(
        body, name=name, grid=(T // tm,),
        out_shape=(jax.ShapeDtypeStruct((T, D), F32), jax.ShapeDtypeStruct((T, D), _MXU_DTYPE),
                   jax.ShapeDtypeStruct((1, D), F32)),
        in_specs=[row, row, vec, row], out_specs=(row, row, vec),
        compiler_params=_params("arbitrary"),
    )(dh, x, g, dres)


def _loss_head(x, g, tgt, name):
    T, D = x.shape
    tm = _tile(T, 256, 16)

    def body(x_ref, g_ref, t_ref, dx_ref, dxb_ref, dg_ref, loss_ref):
        @pl.when(pl.program_id(0) == 0)
        def _():
            dg_ref[...] = jnp.zeros_like(dg_ref)
            loss_ref[...] = jnp.zeros_like(loss_ref)

        xv = x_ref[...]
        gv = g_ref[...]
        r = lax.rsqrt(jnp.mean(xv * xv, axis=-1, keepdims=True) + EPS)
        xhat = xv * r
        diff = xhat * gv - t_ref[...]
        tok = jnp.mean(diff * diff, axis=-1, keepdims=True)
        loss_ref[...] += 0.5 * jnp.sum(tok, axis=0, keepdims=True)
        dy = diff / D
        u = dy * gv
        dx = r * (u - xhat * jnp.mean(u * xhat, axis=-1, keepdims=True))
        dx_ref[...] = dx
        dxb_ref[...] = dx.astype(dxb_ref.dtype)
        dg_ref[...] += jnp.sum(dy * xhat, axis=0, keepdims=True)

    row = pl.BlockSpec((tm, D), lambda i: (i, 0))
    vec = pl.BlockSpec((1, D), lambda i: (0, 0))
    return pl.pallas_call(
        body, name=name, grid=(T // tm,),
        out_shape=(jax.ShapeDtypeStruct((T, D), F32), jax.ShapeDtypeStruct((T, D), _MXU_DTYPE),
                   jax.ShapeDtypeStruct((1, D), F32), jax.ShapeDtypeStruct((1, LANES), F32)),
        in_specs=[row, vec, row],
        out_specs=(row, row, vec, pl.BlockSpec((1, LANES), lambda i: (0, 0))),
        compiler_params=_params("arbitrary"),
    )(x, g, tgt)


def _outnorm_fwd(ya, yb, ga, gb, name):
    T, W = ya.shape
    tm = _tile(T, 512, 16)

    def body(ya_ref, yb_ref, ga_ref, gb_ref, o_ref):
        for k, (y_ref, g_ref) in enumerate(((ya_ref, ga_ref), (yb_ref, gb_ref))):
            yv = y_ref[...]
            r = lax.rsqrt(jnp.mean(yv * yv, axis=-1, keepdims=True) + EPS)
            o_ref[:, k * W:(k + 1) * W] = (yv * r * g_ref[...]).astype(o_ref.dtype)

    row = pl.BlockSpec((tm, W), lambda i: (i, 0))
    vec = pl.BlockSpec((1, W), lambda i: (0, 0))
    return pl.pallas_call(
        body, name=name, grid=(T // tm,),
        out_shape=jax.ShapeDtypeStruct((T, 2 * W), _MXU_DTYPE),
        in_specs=[row, row, vec, vec], out_specs=pl.BlockSpec((tm, 2 * W), lambda i: (i, 0)),
        compiler_params=_params("parallel"),
    )(ya, yb, ga, gb)


def _outnorm_bwd(dyn, ya, yb, ga, gb, name):
    T, W = ya.shape
    tm = _tile(T, 256, 16)

    def body(d_ref, ya_ref, yb_ref, ga_ref, gb_ref, dya_ref, dyb_ref, dga_ref, dgb_ref):
        @pl.when(pl.program_id(0) == 0)
        def _():
            dga_ref[...] = jnp.zeros_like(dga_ref)
            dgb_ref[...] = jnp.zeros_like(dgb_ref)

        for k, (y_ref, g_ref, dy_ref, dg_ref) in enumerate(
                ((ya_ref, ga_ref, dya_ref, dga_ref), (yb_ref, gb_ref, dyb_ref, dgb_ref))):
            dy, dgr = _rms_bwd_rows(d_ref[:, k * W:(k + 1) * W], y_ref[...], g_ref[...])
            dy_ref[...] = dy.astype(dy_ref.dtype)
            dg_ref[...] += jnp.sum(dgr, axis=0, keepdims=True)

    row = pl.BlockSpec((tm, W), lambda i: (i, 0))
    vec = pl.BlockSpec((1, W), lambda i: (0, 0))
    return pl.pallas_call(
        body, name=name, grid=(T // tm,),
        out_shape=(jax.ShapeDtypeStruct((T, W), _MXU_DTYPE), jax.ShapeDtypeStruct((T, W), _MXU_DTYPE),
                   jax.ShapeDtypeStruct((1, W), F32), jax.ShapeDtypeStruct((1, W), F32)),
        in_specs=[pl.BlockSpec((tm, 2 * W), lambda i: (i, 0)), row, row, vec, vec],
        out_specs=(row, row, vec, vec),
        compiler_params=_params("arbitrary"),
    )(dyn, ya, yb, ga, gb)


def _ffn_gu(h, wg_t, wu_t, name, jobs=()):
    T, D = h.shape
    Fp = wg_t.shape[0]
    tm = _tile(T, 1024, 16)
    tn = _tile(Fp, _F_TILE, LANES)

    def body(h_ref, wg_ref, wu_ref, g_ref, u_ref, a_ref):
        hv = h_ref[...]
        g = _dot_nt(hv, wg_ref[...])
        u = _dot_nt(hv, wu_ref[...])
        g_ref[...] = g.astype(g_ref.dtype)
        u_ref[...] = u.astype(u_ref.dtype)
        a_ref[...] = (g * jax.nn.sigmoid(g) * u).astype(a_ref.dtype)

    act = pl.BlockSpec((tm, tn), lambda n, m: (m, n))
    wsp = pl.BlockSpec((tn, D), lambda n, m: (n, 0))
    out = jax.ShapeDtypeStruct((T, Fp), _MXU_DTYPE)
    return _call(
        body, name=name, grid=(Fp // tn, T // tm), out_shape=(out, out, out),
        in_specs=[pl.BlockSpec((tm, D), lambda n, m: (m, 0)), wsp, wsp],
        out_specs=(act, act, act), args=(h, wg_t, wu_t), semantics=("parallel", "parallel"), jobs=jobs)


def _ffn_bwd_act(dxb, wd, G, U, name, jobs=()):
    T, D = dxb.shape
    Fp = wd.shape[0]
    tm = _tile(T, 1024, 16)
    tn = _tile(Fp, _F_TILE, LANES)

    def body(e_ref, wd_ref, g_ref, u_ref, dg_ref, du_ref):
        da = 0.5 * _dot_nt(e_ref[...], wd_ref[...])
        g = g_ref[...].astype(F32)
        u = u_ref[...].astype(F32)
        s = jax.nn.sigmoid(g)
        du_ref[...] = (da * (g * s)).astype(du_ref.dtype)
        dg_ref[...] = (da * u * (s * (1.0 + g * (1.0 - s)))).astype(dg_ref.dtype)

    act = pl.BlockSpec((tm, tn), lambda n, m: (m, n))
    out = jax.ShapeDtypeStruct((T, Fp), _MXU_DTYPE)
    return _call(
        body, name=name, grid=(Fp // tn, T // tm), out_shape=(out, out),
        in_specs=[pl.BlockSpec((tm, D), lambda n, m: (m, 0)), pl.BlockSpec((tn, D), lambda n, m: (n, 0)),
                  act, act],
        out_specs=(act, act), args=(dxb, wd, G, U), semantics=("parallel", "parallel"), jobs=jobs)


def _ffn_gate(h, wg_t, U, name, jobs=()):
    T, D = h.shape
    Fp = wg_t.shape[0]
    tm = _tile(T, 1024, 16)
    tn = _tile(Fp, _F_TILE, LANES)

    def body(h_ref, wg_ref, u_ref, g_ref, a_ref):
        g = _dot_nt(h_ref[...], wg_ref[...])
        g_ref[...] = g.astype(g_ref.dtype)
        a_ref[...] = (g * jax.nn.sigmoid(g) * u_ref[...].astype(F32)).astype(a_ref.dtype)

    act = pl.BlockSpec((tm, tn), lambda n, m: (m, n))
    out = jax.ShapeDtypeStruct((T, Fp), _MXU_DTYPE)
    return _call(
        body, name=name, grid=(Fp // tn, T // tm), out_shape=(out, out),
        in_specs=[pl.BlockSpec((tm, D), lambda n, m: (m, 0)), pl.BlockSpec((tn, D), lambda n, m: (n, 0)), act],
        out_specs=(act, act), args=(h, wg_t, U), semantics=("parallel", "parallel"), jobs=jobs)


def _mm_nt(a, b, out_dtype, name, jobs=()):
    M, K = a.shape
    N = b.shape[0]
    tm = _tile(M, 1024, 16)
    tn = _tile(N, 512, LANES)

    def body(a_ref, b_ref, o_ref):
        o_ref[...] = _dot_nt(a_ref[...], b_ref[...]).astype(o_ref.dtype)

    return _call(
        body, name=name, grid=(N // tn, M // tm), out_shape=jax.ShapeDtypeStruct((M, N), out_dtype),
        in_specs=[pl.BlockSpec((tm, K), lambda n, m: (m, 0)), pl.BlockSpec((tn, K), lambda n, m: (n, 0))],
        out_specs=pl.BlockSpec((tm, tn), lambda n, m: (m, n)), args=(a, b),
        semantics=("parallel", "parallel"), jobs=jobs)


_MM_OPERAND_BYTES = 26 * 1024 * 1024


def _k_tile(K, bytes_per_k, align):
    best = align
    for t in range(align, K + 1, align):
        if K % t == 0 and 2 * t * bytes_per_k <= _MM_OPERAND_BYTES:
            best = t
    return best


def _mm_nn(pairs, res, alpha, out_dtype, name, jobs=()):
    M, K = pairs[0][0].shape
    N = pairs[0][1].shape[1]
    n_pairs = len(pairs)
    tm = _tile(M, 1024, 16)
    tn = _tile(N, 1024, LANES)
    tk = _k_tile(K, n_pairs * (tm + tn) * pairs[0][0].dtype.itemsize, LANES)
    nk = K // tk

    def body(*refs):
        ab = refs[:2 * n_pairs]
        res_ref = refs[2 * n_pairs] if res is not None else None
        o_ref = refs[2 * n_pairs + (res is not None)]

        def finish(acc):
            out = alpha * acc
            if res_ref is not None:
                out = res_ref[...] + out
            o_ref[...] = out.astype(o_ref.dtype)

        part = _dot(ab[0][...], ab[1][...])
        for i in range(1, n_pairs):
            part = part + _dot(ab[2 * i][...], ab[2 * i + 1][...])
        if nk == 1:
            finish(part)
        else:
            acc_ref = refs[-1]
            k = pl.program_id(2)

            @pl.when(k == 0)
            def _():
                acc_ref[...] = part

            @pl.when(k > 0)
            def _():
                acc_ref[...] += part

            @pl.when(k == nk - 1)
            def _():
                finish(acc_ref[...])

    in_specs, args = [], []
    for a, b in pairs:
        in_specs += [pl.BlockSpec((tm, tk), lambda m, n, k: (m, k)), pl.BlockSpec((tk, tn), lambda m, n, k: (k, n))]
        args += [a, b]
    if res is not None:
        in_specs.append(pl.BlockSpec((tm, tn), lambda m, n, k: (m, n)))
        args.append(res)
    return _call(
        body, name=name, grid=(M // tm, N // tn, nk), out_shape=jax.ShapeDtypeStruct((M, N), out_dtype),
        in_specs=in_specs, out_specs=pl.BlockSpec((tm, tn), lambda m, n, k: (m, n)),
        scratch_shapes=[pltpu.VMEM((tm, tn), F32)] if nk > 1 else [], args=args,
        semantics=("parallel", "parallel", "arbitrary"), jobs=jobs)


def _mm_tn(a, b, alpha, out_dtype, name, jobs=()):
    K, M = a.shape
    N = b.shape[1]
    tm = _tile(M, 512, LANES)
    tn = _tile(N, 1024, LANES)

    def body(a_ref, b_ref, o_ref):
        o_ref[...] = (alpha * _dot_tn(a_ref[...], b_ref[...])).astype(o_ref.dtype)

    return _call(
        body, name=name, grid=(N // tn, M // tm), out_shape=jax.ShapeDtypeStruct((M, N), out_dtype),
        in_specs=[pl.BlockSpec((K, tm), lambda n, m: (0, m)), pl.BlockSpec((K, tn), lambda n, m: (0, n))],
        out_specs=pl.BlockSpec((tm, tn), lambda n, m: (m, n)), args=(a, b),
        semantics=("parallel", "parallel"), jobs=jobs)


def _half_masks():
    lane = lax.broadcasted_iota(jnp.int32, (1, LANES), 1)
    return (lane < HEAD_DIM, lane >= HEAD_DIM)


def _swap_halves(v):
    return pltpu.roll(v.astype(F32), HEAD_DIM, 1).astype(v.dtype)


def _swa_geometry(n):
    qi = lax.broadcasted_iota(jnp.int32, (WINDOW, 2 * WINDOW), 0)
    kp = lax.broadcasted_iota(jnp.int32, (WINDOW, 2 * WINDOW), 1)
    dist = (WINDOW + qi) - kp
    valid = (dist >= 0) & (dist < WINDOW) & ((n > 0) | (kp >= WINDOW))
    return dist.astype(F32), valid


def _swa_slope(h):
    return 2.0 ** (-8.0 * (h + 1) / N_SWA_HEADS)


def _swa_probs(qm, kx, sink, slope, distf, valid):
    s = _dot_nt(qm, kx) * ATT_SCALE - slope * distf
    s = jnp.where(valid, s, -1e30)
    m = jnp.maximum(jnp.max(s, axis=1, keepdims=True), sink)
    p = jnp.exp(s - m)
    e_sink = jnp.exp(sink - m)
    den = jnp.sum(p, axis=1, keepdims=True) + e_sink
    return p / den, e_sink / den


def _swa_specs(B, S):
    nb = S // WINDOW
    kcol = SWA_Q // SWA_KV
    cur = lambda b, n: (b * nb + n, kcol)
    prev = lambda b, n: (b * nb + jnp.maximum(n - 1, 0), kcol)
    curv = lambda b, n: (b * nb + n, kcol + 1)
    prevv = lambda b, n: (b * nb + jnp.maximum(n - 1, 0), kcol + 1)
    q_spec = pl.BlockSpec((WINDOW, SWA_Q), lambda b, n: (b * nb + n, 0))
    kv = [pl.BlockSpec((WINDOW, SWA_KV), f) for f in (prev, cur, prevv, curv)]
    sink_spec = pl.BlockSpec(memory_space=pltpu.SMEM)
    return nb, q_spec, kv, sink_spec


def _swa_kv_views(kp_ref, kc_ref, vp_ref, vc_ref, g):
    hm = _half_masks()
    c0 = (g // 2) * LANES
    k_all = jnp.concatenate([kp_ref[:, c0:c0 + LANES], kc_ref[:, c0:c0 + LANES]], axis=0)
    v_all = jnp.concatenate([vp_ref[:, c0:c0 + LANES], vc_ref[:, c0:c0 + LANES]], axis=0)
    b = g % 2
    ks, vs = [None, None], [None, None]
    ks[b], vs[b] = k_all, v_all
    ks[1 - b], vs[1 - b] = _swap_halves(k_all), _swap_halves(v_all)
    ks = [jnp.where(hm[a], ks[a], 0) for a in range(2)]
    vs = [jnp.where(hm[a], vs[a], 0) for a in range(2)]
    return ks, vs


def _swa_fwd(proj, sinks, B, S, name):
    T = B * S
    nb, q_spec, kv_specs, sink_spec = _swa_specs(B, S)

    def body(sink_ref, q_ref, kp_ref, kc_ref, vp_ref, vc_ref, y_ref):
        hm = _half_masks()
        distf, valid = _swa_geometry(pl.program_id(1))
        for g in range(N_SWA_KV):
            ks, vs = _swa_kv_views(kp_ref, kc_ref, vp_ref, vc_ref, g)
            for pp in (2 * g, 2 * g + 1):
                q_pair = q_ref[:, pp * LANES:(pp + 1) * LANES]
                o_pair = jnp.zeros((WINDOW, LANES), F32)
                for a in range(2):
                    h = 2 * pp + a
                    qm = jnp.where(hm[a], q_pair, 0)
                    p, _ = _swa_probs(qm, ks[a], sink_ref[0, h], _swa_slope(h), distf, valid)
                    o_pair = o_pair + _dot(p.astype(_MXU_DTYPE), vs[a])
                y_ref[:, pp * LANES:(pp + 1) * LANES] = o_pair

    return pl.pallas_call(
        body, name=name, grid=(B, nb), out_shape=jax.ShapeDtypeStruct((T, SWA_Q), F32),
        in_specs=[sink_spec, q_spec] + kv_specs,
        out_specs=pl.BlockSpec((WINDOW, SWA_Q), lambda b, n: (b * nb + n, 0)),
        compiler_params=_params("parallel", "parallel"),
    )(sinks, proj, proj, proj, proj, proj)


def _swa_bwd(proj, sinks, dya, B, S, name):
    T = B * S
    nb, q_spec, kv_specs, sink_spec = _swa_specs(B, S)

    def body(sink_ref, q_ref, kp_ref, kc_ref, vp_ref, vc_ref, do_ref,
             dq_ref, dk_ref, dv_ref, dsink_ref, dk_acc, dv_acc):
        b_id, n = pl.program_id(0), pl.program_id(1)
        hm = _half_masks()
        lane = lax.broadcasted_iota(jnp.int32, (1, LANES), 1)

        @pl.when((b_id == 0) & (n == 0))
        def _():
            dsink_ref[...] = jnp.zeros_like(dsink_ref)

        @pl.when(n == 0)
        def _():
            dk_acc[...] = jnp.zeros_like(dk_acc)
            dv_acc[...] = jnp.zeros_like(dv_acc)

        distf, valid = _swa_geometry(n)
        r_prev = pl.multiple_of(jnp.maximum(n - 1, 0) * WINDOW, WINDOW)
        r_cur = pl.multiple_of(n * WINDOW, WINDOW)
        dsink = jnp.zeros((1, LANES), F32)
        for g in range(N_SWA_KV):
            ks, vs = _swa_kv_views(kp_ref, kc_ref, vp_ref, vc_ref, g)
            dk_g = [jnp.zeros((2 * WINDOW, LANES), F32) for _ in range(2)]
            dv_g = [jnp.zeros((2 * WINDOW, LANES), F32) for _ in range(2)]
            for pp in (2 * g, 2 * g + 1):
                q_pair = q_ref[:, pp * LANES:(pp + 1) * LANES]
                do_pair = do_ref[:, pp * LANES:(pp + 1) * LANES]
                dq_pair = jnp.zeros((WINDOW, LANES), F32)
                for a in range(2):
                    h = 2 * pp + a
                    qm = jnp.where(hm[a], q_pair, 0)
                    dom = jnp.where(hm[a], do_pair, 0)
                    p, p_sink = _swa_probs(qm, ks[a], sink_ref[0, h], _swa_slope(h), distf, valid)
                    dp = _dot_nt(dom, vs[a])
                    delta = jnp.sum(p * dp, axis=1, keepdims=True)
                    ds = (p * (dp - delta) * ATT_SCALE).astype(_MXU_DTYPE)
                    dsink = dsink + jnp.where(lane == h, -jnp.sum(p_sink * delta), 0.0)
                    dq_pair = dq_pair + _dot(ds, ks[a])
                    dk_g[a] = dk_g[a] + _dot_tn(ds, qm)
                    dv_g[a] = dv_g[a] + _dot_tn(p.astype(_MXU_DTYPE), dom)
                dq_ref[:, pp * LANES:(pp + 1) * LANES] = dq_pair.astype(dq_ref.dtype)
            bsel = g % 2
            dk_t = dk_g[bsel] + pltpu.roll(dk_g[1 - bsel], HEAD_DIM, 1)
            dv_t = dv_g[bsel] + pltpu.roll(dv_g[1 - bsel], HEAD_DIM, 1)
            c0 = (g // 2) * LANES
            dk_acc[pl.ds(r_prev, WINDOW), c0:c0 + LANES] += dk_t[:WINDOW]
            dk_acc[pl.ds(r_cur, WINDOW), c0:c0 + LANES] += dk_t[WINDOW:]
            dv_acc[pl.ds(r_prev, WINDOW), c0:c0 + LANES] += dv_t[:WINDOW]
            dv_acc[pl.ds(r_cur, WINDOW), c0:c0 + LANES] += dv_t[WINDOW:]
        dsink_ref[...] += dsink

        @pl.when(n == nb - 1)
        def _():
            dk_ref[...] = dk_acc[...].astype(dk_ref.dtype)
            dv_ref[...] = dv_acc[...].astype(dv_ref.dtype)

    seq_kv = pl.BlockSpec((S, SWA_KV), lambda b, n: (b, 0))
    return pl.pallas_call(
        body, name=name, grid=(B, nb),
        out_shape=(jax.ShapeDtypeStruct((T, SWA_Q), _MXU_DTYPE), jax.ShapeDtypeStruct((T, SWA_KV), _MXU_DTYPE),
                   jax.ShapeDtypeStruct((T, SWA_KV), _MXU_DTYPE), jax.ShapeDtypeStruct((1, LANES), F32)),
        in_specs=[sink_spec, q_spec] + kv_specs + [pl.BlockSpec((WINDOW, SWA_Q), lambda b, n: (b * nb + n, 0))],
        out_specs=(pl.BlockSpec((WINDOW, SWA_Q), lambda b, n: (b * nb + n, 0)), seq_kv, seq_kv,
                   pl.BlockSpec((1, LANES), lambda b, n: (0, 0))),
        scratch_shapes=[pltpu.VMEM((S, SWA_KV), F32), pltpu.VMEM((S, SWA_KV), F32)],
        compiler_params=_params("arbitrary", "arbitrary"),
    )(sinks, proj, proj, proj, proj, proj, dya)


SB_TILE = 256
SB_HALF = 128
SB_DEAD = -105.0


def _mark_lanes():
    lane = lax.broadcasted_iota(jnp.int32, (1, LANES), 1)
    return (lane == HEAD_DIM - 1) | (lane == LANES - 1)


def _tri2(cond):
    j = lax.broadcasted_iota(jnp.int32, (2 * SB_HALF, SB_HALF), 0) & (SB_HALF - 1)
    s = lax.broadcasted_iota(jnp.int32, (2 * SB_HALF, SB_HALF), 1)
    return cond(j, s).astype(_MXU_DTYPE)


def _half_cumsums(x, tri2):
    out = []
    for h in range(2):
        xh = x[:, h * SB_HALF:(h + 1) * SB_HALF]
        hi = xh.astype(_MXU_DTYPE)
        lo = (xh - hi.astype(F32)).astype(_MXU_DTYPE)
        out.append(_dot(jnp.concatenate([hi, lo], axis=1), tri2))
    return out


def _log_sigmoid(z):
    return jnp.minimum(z, 0.0) - jnp.log(1.0 + jnp.exp(-jnp.abs(z)))


def _sb_specs(B, S):
    qb = (SWA_Q + 2 * SWA_KV) // LANES
    kb = qb + SB_W // LANES
    vb = kb + SB_W // LANES
    return [pl.BlockSpec((S, LANES), functools.partial(lambda b, p, c: (b, c + p), c=c)) for c in (qb, kb, vb)]


def _sb_fwd(proj, B, S, name, jobs=()):
    T = B * S
    tq = SB_TILE
    nq = S // tq

    def body(q_ref, k_ref, v_ref, y_ref, tot_ref):
        hm = _half_masks()
        ji = lax.broadcasted_iota(jnp.int32, (tq, tq), 0)
        si = lax.broadcasted_iota(jnp.int32, (tq, tq), 1)
        tri_after = _tri2(lambda j, s: j > s)
        causal = si < ji
        mark = _mark_lanes()

        def q_loop(qi, carry):
            r0 = pl.multiple_of(qi * tq, tq)
            q_pair = q_ref[pl.ds(r0, tq), :] * ATT_SCALE
            qms = [jnp.where(hm[a], q_pair, 0) for a in range(2)]

            def tile(c0, state, diagonal):
                kk = k_ref[pl.ds(c0, tq), :]
                vv = v_ref[pl.ds(c0, tq), :]
                two = range(2)
                z = [_dot_nt(qms[a], kk) for a in two]
                lb = [_log_sigmoid(z[a]) for a in two]
                l1m = [jnp.where(causal, lb[a] - z[a], 0.0) if diagonal else lb[a] - z[a] for a in two]
                cum = [_half_cumsums(l1m[a], tri_after) for a in two]
                tot = [[cum[a][h][:, 0:1] + l1m[a][:, h * SB_HALF:h * SB_HALF + 1] for h in two] for a in two]
                after = [jnp.concatenate([cum[a][0] + (state[a][1] + tot[a][1]), cum[a][1] + state[a][1]], axis=1)
                         for a in two]
                att = [jnp.exp(lb[a] + after[a]) for a in two]
                if diagonal:
                    att = [jnp.where(causal, att[a], 0.0) for a in two]
                acc = [state[a][0] + _dot(att[a].astype(_MXU_DTYPE), jnp.where(hm[a], vv, 0)) for a in two]
                car = [state[a][1] + (tot[a][0] + tot[a][1]) for a in two]
                return tuple((acc[a], car[a]) for a in two)

            def live(st):
                return jnp.maximum(jnp.max(st[0][1]), jnp.max(st[1][1])) > SB_DEAD

            def step(c):
                it, _, st = c
                st = tile(pl.multiple_of((qi - 1 - it) * tq, tq), st, False)
                return it + 1, live(st), st

            zero = (jnp.zeros((tq, LANES), F32), jnp.zeros((tq, 1), F32))
            state = tile(r0, (zero, zero), True)
            done, _, state = lax.while_loop(lambda c: (c[0] < qi) & c[1], step, (jnp.int32(0), live(state), state))
            y_ref[pl.ds(r0, tq), :] = state[0][0] + state[1][0]
            first = (qi - done).astype(F32)
            tot_ref[pl.ds(r0, tq), :] = jnp.where(mark, first, jnp.where(hm[0], state[0][1], state[1][1]))
            return carry

        lax.fori_loop(0, nq, q_loop, 0)

    out_spec = pl.BlockSpec((S, LANES), lambda b, p: (b, p))
    return _call(
        body, name=name, grid=(B, SB_W // LANES),
        out_shape=(jax.ShapeDtypeStruct((T, SB_W), F32), jax.ShapeDtypeStruct((T, SB_W), F32)),
        in_specs=_sb_specs(B, S), out_specs=(out_spec, out_spec), args=(proj, proj, proj),
        semantics=("parallel", "parallel"), jobs=jobs)


def _sb_bwd(proj, tot, dyb, B, S, name, jobs=()):
    T = B * S
    tq = SB_TILE
    nq = S // tq

    def body(q_ref, k_ref, v_ref, do_ref, tot_ref, dq_ref, dk_ref, dv_ref, dk_acc, dv_acc):
        hm = _half_masks()
        ji = lax.broadcasted_iota(jnp.int32, (tq, tq), 0)
        si = lax.broadcasted_iota(jnp.int32, (tq, tq), 1)
        tri_incl = _tri2(lambda j, s: j <= s)
        tri_excl = _tri2(lambda j, s: j < s)
        causal = si < ji
        mark = _mark_lanes()
        dk_acc[...] = jnp.zeros_like(dk_acc)
        dv_acc[...] = jnp.zeros_like(dv_acc)

        def q_loop(qi, carry):
            r0 = pl.multiple_of(qi * tq, tq)
            q_pair = q_ref[pl.ds(r0, tq), :] * ATT_SCALE
            do_pair = do_ref[pl.ds(r0, tq), :]
            tot_pair = tot_ref[pl.ds(r0, tq), :]
            qms = [jnp.where(hm[a], q_pair, 0) for a in range(2)]
            doms = [jnp.where(hm[a], do_pair, 0) for a in range(2)]
            totals = [jnp.max(jnp.where(hm[a] & ~mark, tot_pair, -jnp.inf), axis=1, keepdims=True) for a in range(2)]
            first = jnp.max(jnp.where(mark, tot_pair, -jnp.inf))
            first = jnp.where((first >= 0.0) & (first <= qi.astype(F32)), first, 0.0).astype(jnp.int32)

            def tile(c0, state, diagonal):
                kk = k_ref[pl.ds(c0, tq), :]
                vv = v_ref[pl.ds(c0, tq), :]
                ks = kk * ATT_SCALE
                two = range(2)
                last = SB_HALF - 1
                z = [_dot_nt(qms[a], kk) for a in two]
                d_att = [_dot_nt(doms[a], vv) for a in two]
                lb = [_log_sigmoid(z[a]) for a in two]
                l1m = [jnp.where(causal, lb[a] - z[a], 0.0) if diagonal else lb[a] - z[a] for a in two]
                cum = [_half_cumsums(l1m[a], tri_incl) for a in two]
                upto = [jnp.concatenate([cum[a][0] + state[a][1],
                                         cum[a][1] + (state[a][1] + cum[a][0][:, last:last + 1])], axis=1) for a in two]
                att = [jnp.exp(lb[a] + (totals[a] - upto[a])) for a in two]
                if diagonal:
                    att = [jnp.where(causal, att[a], 0.0) for a in two]
                d_log = [d_att[a] * att[a] for a in two]
                cumd = [_half_cumsums(d_log[a], tri_excl) for a in two]
                totd = [[cumd[a][h][:, last:last + 1] + d_log[a][:, h * SB_HALF + last:h * SB_HALF + last + 1]
                         for h in two] for a in two]
                before = [jnp.concatenate([cumd[a][0] + state[a][2], cumd[a][1] + (state[a][2] + totd[a][0])], axis=1)
                          for a in two]
                sig = [jnp.exp(lb[a]) for a in two]
                dz = [d_log[a] * (1.0 - sig[a]) - sig[a] * before[a] for a in two]
                if diagonal:
                    dz = [jnp.where(causal, dz[a], 0.0) for a in two]
                dzb = [dz[a].astype(_MXU_DTYPE) for a in two]
                dq = [state[a][0] + _dot(dzb[a], jnp.where(hm[a], ks, 0)) for a in two]
                dk_acc[pl.ds(c0, tq), :] += _dot_tn(dzb[0], qms[0]) + _dot_tn(dzb[1], qms[1])
                dv_acc[pl.ds(c0, tq), :] += (_dot_tn(att[0].astype(_MXU_DTYPE), doms[0])
                                             + _dot_tn(att[1].astype(_MXU_DTYPE), doms[1]))
                cp = [upto[a][:, tq - 1:tq] for a in two]
                cq = [state[a][2] + (totd[a][0] + totd[a][1]) for a in two]
                return tuple((dq[a], cp[a], cq[a]) for a in two)

            zero_col = jnp.zeros((tq, 1), F32)
            zero = (jnp.zeros((tq, LANES), F32), zero_col, zero_col)
            state = lax.fori_loop(first, qi, lambda kj, st: tile(pl.multiple_of(kj * tq, tq), st, False), (zero, zero))
            state = tile(r0, state, True)
            dq_ref[pl.ds(r0, tq), :] = (state[0][0] + state[1][0]).astype(dq_ref.dtype)
            return carry

        lax.fori_loop(0, nq, q_loop, 0)
        dk_ref[...] = dk_acc[...].astype(dk_ref.dtype)
        dv_ref[...] = dv_acc[...].astype(dv_ref.dtype)

    pair = pl.BlockSpec((S, LANES), lambda b, p: (b, p))
    out = jax.ShapeDtypeStruct((T, SB_W), _MXU_DTYPE)
    return _call(
        body, name=name, grid=(B, SB_W // LANES), out_shape=(out, out, out),
        in_specs=_sb_specs(B, S) + [pair, pair], out_specs=(pair, pair, pair),
        scratch_shapes=[pltpu.VMEM((S, LANES), F32), pltpu.VMEM((S, LANES), F32)],
        args=(proj, proj, proj, dyb, tot), semantics=("parallel", "parallel"), jobs=jobs)


def _layer_step(x, tgt, B, S, small, comm):
    run, big, part = comm.run, comm.big, comm.partial
    ffn1_w, ffn2_w = ("ffn1_down", "ffn1_gate", "ffn1_up"), ("ffn2_down", "ffn2_gate", "ffn2_up")

    h1 = run(_rms_fwd, x, small["ffn1_norm"], "ffn1_rms", ag=("ffn1_up",))
    U1 = run(_mm_nt, h1, big["ffn1_up"], _MXU_DTYPE, "ffn1_up", ag=("ffn1_gate",))
    G1, A1 = run(_ffn_gate, h1, big["ffn1_gate"], U1, "ffn1_gate", ag=("ffn1_down",))
    x1 = run(_mm_nn, [(A1, big["ffn1_down"])], x, 0.5, F32, "ffn1_down", ag=("w_in", "w_out"))
    h2 = run(_rms_fwd, x1, small["mix_norm"], "mix_rms")
    proj = run(_mm_nt, h2, big["w_in"], _MXU_DTYPE, "in_proj")
    ya = _swa_fwd(proj, small["swa_sinks"], B, S, "swa_fwd")
    yb, tot = run(_sb_fwd, proj, B, S, "sb_fwd", ag=("ffn2_gate", "ffn2_up", "ffn2_down"))
    yn = _outnorm_fwd(ya, yb, small["swa_out_norm"], small["sb_out_norm"], "out_norm")
    x2 = run(_mm_nn, [(yn, big["w_out"])], x1, 1.0, F32, "out_proj")
    h3 = run(_rms_fwd, x2, small["ffn2_norm"], "ffn2_rms")
    G2, U2, A2 = run(_ffn_gu, h3, big["ffn2_gate"], big["ffn2_up"], "ffn2_gate_up")
    x3 = run(_mm_nn, [(A2, big["ffn2_down"])], x2, 0.5, F32, "ffn2_down")

    dx3, dx3b, d_final, loss = _loss_head(x3, small["final_norm"], tgt, "loss_head")

    dG2, dU2 = run(_ffn_bwd_act, dx3b, big["ffn2_down"], G2, U2, "ffn2_bwd_act")
    part["ffn2_down"] = run(_mm_tn, A2, dx3b, 0.5, _WIRE_DTYPE, "ffn2_dw_down")
    part["ffn2_gate"] = run(_mm_tn, dG2, h3, 1.0, _WIRE_DTYPE, "ffn2_dw_gate")
    part["ffn2_up"] = run(_mm_tn, dU2, h3, 1.0, _WIRE_DTYPE, "ffn2_dw_up")
    dh3 = run(_mm_nn, [(dG2, big["ffn2_gate"]), (dU2, big["ffn2_up"])], None, 1.0, F32, "ffn2_dh", rs1=ffn2_w)
    dx2, dx2b, d_g2 = _rms_bwd(dh3, x2, small["ffn2_norm"], dx3, "ffn2_rms_bwd")

    part["w_out"] = run(_mm_tn, yn, dx2b, 1.0, _WIRE_DTYPE, "dw_out")
    dyn = run(_mm_nt, dx2b, big["w_out"], F32, "out_proj_bwd")
    dya, dyb, d_ga, d_gb = _outnorm_bwd(dyn, ya, yb, small["swa_out_norm"], small["sb_out_norm"], "out_norm_bwd")
    dqa, dka, dva, d_sinks = _swa_bwd(proj, small["swa_sinks"], dya, B, S, "swa_bwd")
    dqb, dkb, dvb = run(_sb_bwd, proj, tot, dyb, B, S, "sb_bwd", rs2=ffn2_w)
    dproj = jnp.concatenate([dqa, dka, dva, dqb, dkb, dvb], axis=1)
    part["w_in"] = run(_mm_tn, dproj, h2, 1.0, _WIRE_DTYPE, "dw_in")
    dh2 = run(_mm_nn, [(dproj, big["w_in"])], None, 1.0, F32, "in_proj_bwd", rs1=("w_in", "w_out"))
    dx1, dx1b, d_gm = _rms_bwd(dh2, x1, small["mix_norm"], dx2, "mix_rms_bwd")

    dG1, dU1 = run(_ffn_bwd_act, dx1b, big["ffn1_down"], G1, U1, "ffn1_bwd_act", rs2=("w_in", "w_out"))
    part["ffn1_down"] = run(_mm_tn, A1, dx1b, 0.5, _WIRE_DTYPE, "ffn1_dw_down")
    part["ffn1_gate"] = run(_mm_tn, dG1, h1, 1.0, _WIRE_DTYPE, "ffn1_dw_gate", rs1=("ffn1_down",))
    part["ffn1_up"] = run(_mm_tn, dU1, h1, 1.0, _WIRE_DTYPE, "ffn1_dw_up", rs1=("ffn1_gate",), rs2=("ffn1_down",))
    dh1 = run(_mm_nn, [(dG1, big["ffn1_gate"])], None, 1.0, F32, "ffn1_dh_gate", rs1=("ffn1_up",), rs2=("ffn1_gate",))
    dh1 = run(_mm_nn, [(dU1, big["ffn1_up"])], dh1, 1.0, F32, "ffn1_dh_up", rs2=("ffn1_up",))
    gx, _, d_g1 = _rms_bwd(dh1, x, small["ffn1_norm"], dx1, "ffn1_rms_bwd")

    d_small = {"ffn1_norm": d_g1, "mix_norm": d_gm, "swa_sinks": d_sinks[:, :N_SWA_HEADS], "swa_out_norm": d_ga,
               "sb_out_norm": d_gb, "ffn2_norm": d_g2, "final_norm": d_final}
    return loss, gx, d_small


MESH = pl.DeviceIdType.MESH
BIG_NAMES = ("ffn1_gate", "ffn1_up", "ffn1_down", "w_in", "w_out", "ffn2_gate", "ffn2_up", "ffn2_down")
_COMM_PARAMS = pltpu.CompilerParams(has_side_effects=True)


def _place():
    x, y, c = lax.axis_index("x"), lax.axis_index("y"), lax.axis_index("c")
    other_chips = [(1 - x, y), (x, 1 - y), (1 - x, 1 - y)]
    return x, y, c, other_chips


def _padded_rows(rows):
    full = N_DEV * rows
    return -(-full // _F_TILE) * _F_TILE


def _ag_job(shards):
    nw = len(shards)
    D = shards[0].shape[1]
    rows_w = [s.shape[0] for s in shards]
    full_w = [_padded_rows(r) for r in rows_w]
    pad_w = [f - N_DEV * r for f, r in zip(full_w, rows_w)]
    max_pad = max(max(pad_w), 16)

    class Plan:
        def __init__(self, ins, outs, scratch):
            zbuf, send_sems, recv_sems, local_sems, zero_sems = scratch
            x, y, c, chips = _place()
            me, sibling = (x, y, c), (x, y, 1 - c)

            def rows(w, px, py, pc):
                start = pl.multiple_of((4 * px + 2 * py + pc) * rows_w[w], 16)
                return outs[w].at[pl.ds(start, rows_w[w]), :]

            def copy(w, k, block, to, src=None):
                return pltpu.make_async_remote_copy(
                    src_ref=rows(w, *block) if src is None else src, dst_ref=rows(w, *block),
                    send_sem=send_sems.at[w, k], recv_sem=recv_sems.at[w, k], device_id=to, device_id_type=MESH)

            self.zbuf = zbuf
            self.local = [pltpu.make_async_copy(zbuf.at[pl.ds(0, pad_w[w]), :],
                                                outs[w].at[pl.ds(N_DEV * rows_w[w], pad_w[w]), :], zero_sems.at[w])
                          for w in range(nw) if pad_w[w]]
            self.local += [pltpu.make_async_copy(ins[w], rows(w, *me), local_sems.at[w]) for w in range(nw)]
            self.first = [[copy(w, 0, me, sibling, src=ins[w])]
                          + [copy(w, 1 + j, me, (*chip, c), src=ins[w]) for j, chip in enumerate(chips)]
                          for w in range(nw)]
            self.arrive = [[copy(w, 1 + j, (*chip, c), me) for j, chip in enumerate(chips)] for w in range(nw)]
            self.passed = [[copy(w, 4 + j, (*chip, c), sibling) for j, chip in enumerate(chips)] for w in range(nw)]
            self.from_sibling = [[copy(w, 0, sibling, me)]
                                 + [copy(w, 4 + j, (*chip, 1 - c), me) for j, chip in enumerate(chips)]
                                 for w in range(nw)]

    def start(ins, outs, scratch):
        plan = Plan(ins, outs, scratch)
        plan.zbuf[...] = jnp.zeros_like(plan.zbuf)
        for cp in plan.local:
            cp.start()
        for w in range(nw):
            for cp in plan.first[w]:
                cp.start()

    def mid(ins, outs, scratch):
        plan = Plan(ins, outs, scratch)
        for w in range(nw):
            for arrived, onward in zip(plan.arrive[w], plan.passed[w]):
                arrived.wait_recv()
                onward.start()

    def finish(ins, outs, scratch):
        plan = Plan(ins, outs, scratch)
        for w in range(nw):
            for cp in plan.from_sibling[w]:
                cp.wait_recv()
        for w in range(nw):
            for cp in plan.first[w] + plan.passed[w]:
                cp.wait_send()
        for cp in plan.local:
            cp.wait()

    return _Job(
        ins=shards, out_shape=[jax.ShapeDtypeStruct((f, D), s.dtype) for f, s in zip(full_w, shards)],
        scratch=[pltpu.VMEM((max_pad, D), shards[0].dtype), pltpu.SemaphoreType.DMA((nw, 7)),
                 pltpu.SemaphoreType.DMA((nw, 7)), pltpu.SemaphoreType.DMA((nw,)), pltpu.SemaphoreType.DMA((nw,))],
        start=start, mid=mid, finish=finish)


def _rs1_job(partials, rows_w):
    nw = len(partials)
    D = partials[0].shape[1]

    def copies(ins, outs, scratch):
        send_sems, recv_sems = scratch
        x, y, c, _ = _place()
        out = []
        for w in range(nw):
            r = rows_w[w]
            for q in range(4):
                src = ins[w].at[pl.ds(pl.multiple_of((2 * q + 1 - c) * r, 16), r), :]
                out.append(pltpu.make_async_remote_copy(
                    src_ref=src, dst_ref=outs[w].at[pl.ds(q * r, r), :], send_sem=send_sems.at[w, q],
                    recv_sem=recv_sems.at[w, q], device_id=(x, y, 1 - c), device_id_type=MESH))
        return out

    def start(ins, outs, scratch):
        for cp in copies(ins, outs, scratch):
            cp.start()

    def finish(ins, outs, scratch):
        for cp in copies(ins, outs, scratch):
            cp.wait()

    return _Job(
        ins=partials, out_shape=[jax.ShapeDtypeStruct((4 * r, D), p.dtype) for r, p in zip(rows_w, partials)],
        scratch=[pltpu.SemaphoreType.DMA((nw, 4)), pltpu.SemaphoreType.DMA((nw, 4))], start=start, finish=finish)


def _pair_sum(partial, from_sibling, rows, core, name):
    D = partial.shape[1]

    def body(core_ref, p_ref, s_ref, o_ref):
        o_ref[...] = (p_ref[...].astype(F32) + s_ref[...].astype(F32)).astype(o_ref.dtype)

    grid_spec = pltpu.PrefetchScalarGridSpec(
        num_scalar_prefetch=1, grid=(4,),
        in_specs=[pl.BlockSpec((rows, D), lambda q, core_ref: (2 * q + core_ref[0], 0)),
                  pl.BlockSpec((rows, D), lambda q, core_ref: (q, 0))],
        out_specs=pl.BlockSpec((rows, D), lambda q, core_ref: (q, 0)))
    return pl.pallas_call(
        body, name=name, grid_spec=grid_spec, out_shape=jax.ShapeDtypeStruct((4 * rows, D), partial.dtype),
        compiler_params=_params("arbitrary"),
    )(core, partial, from_sibling)


def _rs2_job(chip_sums, rows_w):
    nw = len(chip_sums)

    def copies(ins, outs, scratch):
        send_sems, recv_sems, local_sems = scratch
        x, y, c, chips = _place()
        my_chip = 2 * x + y
        out = []
        for w in range(nw):
            r = rows_w[w]
            mine = pl.ds(pl.multiple_of(my_chip * r, 16), r)
            out.append(pltpu.make_async_copy(ins[w].at[mine, :], outs[w].at[mine, :], local_sems.at[w]))
            for j, (qx, qy) in enumerate(chips):
                src = ins[w].at[pl.ds(pl.multiple_of((2 * qx + qy) * r, 16), r), :]
                out.append(pltpu.make_async_remote_copy(
                    src_ref=src, dst_ref=outs[w].at[mine, :], send_sem=send_sems.at[w, j],
                    recv_sem=recv_sems.at[w, j], device_id=(qx, qy, c), device_id_type=MESH))
        return out

    def start(ins, outs, scratch):
        for cp in copies(ins, outs, scratch):
            cp.start()

    def finish(ins, outs, scratch):
        for cp in copies(ins, outs, scratch):
            cp.wait()

    return _Job(
        ins=chip_sums, out_shape=[jax.ShapeDtypeStruct(s.shape, s.dtype) for s in chip_sums],
        scratch=[pltpu.SemaphoreType.DMA((nw, 3)), pltpu.SemaphoreType.DMA((nw, 3)), pltpu.SemaphoreType.DMA((nw,))],
        start=start, finish=finish)


class _Comm:
    def __init__(self, shards):
        self.shards = shards
        self.rows = {n: s.shape[0] for n, s in shards.items()}
        self.core = lax.axis_index("c").astype(jnp.int32).reshape(1)
        self.big, self.partial, self.chip_sums, self.slots = {}, {}, {}, {}

    def run(self, fn, *args, ag=(), rs1=(), rs2=()):
        jobs = []
        if ag:
            jobs.append(_ag_job([self.shards[n] for n in ag]))
        if rs1:
            jobs.append(_rs1_job([self.partial[n] for n in rs1], [self.rows[n] for n in rs1]))
        if rs2:
            jobs.append(_rs2_job([self.chip_sums[n] for n in rs2], [self.rows[n] for n in rs2]))
        out, job_res = fn(*args, jobs=jobs)
        job_res = iter(job_res)
        if ag:
            self.big.update(zip(ag, next(job_res)))
        if rs1:
            for n, got in zip(rs1, next(job_res)):
                self.chip_sums[n] = _pair_sum(self.partial[n], got, self.rows[n], self.core, "pair_sum_" + n)
        if rs2:
            self.slots.update(zip(rs2, next(job_res)))
        return out


SMALL_ROWS = 88


def _small_allreduce(vec):
    def body(v_ref, o_ref, gather, send_sems, recv_sems):
        x, y, c, _ = _place()
        my_id = 4 * x + 2 * y + c
        gather[my_id] = v_ref[...]
        copies = []
        for r in range(1, N_DEV):
            peer = (x ^ (r >> 2), y ^ ((r >> 1) & 1), c ^ (r & 1))
            cp = pltpu.make_async_remote_copy(src_ref=v_ref, dst_ref=gather.at[my_id], send_sem=send_sems.at[r - 1],
                                              recv_sem=recv_sems.at[r - 1], device_id=peer, device_id_type=MESH)
            cp.start()
            copies.append(cp)
        for cp in copies:
            cp.wait()
        acc = gather[0]
        for d in range(1, N_DEV):
            acc = acc + gather[d]
        o_ref[...] = acc

    vm = pl.BlockSpec(memory_space=pltpu.VMEM)
    return pl.pallas_call(
        body, name="small_allreduce", out_shape=jax.ShapeDtypeStruct(vec.shape, F32),
        in_specs=[vm], out_specs=vm,
        scratch_shapes=[pltpu.VMEM((N_DEV,) + vec.shape, F32), pltpu.SemaphoreType.DMA((N_DEV - 1,)),
                        pltpu.SemaphoreType.DMA((N_DEV - 1,))],
        compiler_params=_COMM_PARAMS,
    )(vec)


def _adamw_update(w, g, m, v):
    nm = ADAM_B1 * m + (1.0 - ADAM_B1) * g
    nv = ADAM_B2 * v + (1.0 - ADAM_B2) * jnp.square(g)
    m_hat = nm / (1.0 - ADAM_B1 ** ADAM_STEP)
    v_hat = nv / (1.0 - ADAM_B2 ** ADAM_STEP)
    return -ADAM_LR * (m_hat / (jnp.sqrt(v_hat) + ADAM_EPS) + ADAM_WD * w), nm, nv


def _adamw(w, g, m, v, name):
    R, C = w.shape
    tr = _tile(R, 256, 8)

    def body(w_ref, g_ref, m_ref, v_ref, d_ref, nm_ref, nv_ref):
        d_ref[...], nm_ref[...], nv_ref[...] = _adamw_update(w_ref[...], g_ref[...], m_ref[...], v_ref[...])

    spec = pl.BlockSpec((tr, C), lambda i: (i, 0))
    out = jax.ShapeDtypeStruct((R, C), F32)
    return pl.pallas_call(
        body, name=name, grid=(R // tr,), out_shape=(out, out, out),
        in_specs=[spec] * 4, out_specs=(spec, spec, spec),
        compiler_params=_params("parallel"),
    )(w, g, m, v)


def _adamw_slots(w, slots, m, v, name):
    R, C = w.shape
    tc = _tile(C, 512, LANES)

    def body(w_ref, s_ref, m_ref, v_ref, g_ref, d_ref, nm_ref, nv_ref):
        g = s_ref[0].astype(F32)
        for q in range(1, 4):
            g = g + s_ref[q].astype(F32)
        g_ref[...] = g
        d_ref[...], nm_ref[...], nv_ref[...] = _adamw_update(w_ref[...], g, m_ref[...], v_ref[...])

    spec = pl.BlockSpec((R, tc), lambda j: (0, j))
    out = jax.ShapeDtypeStruct((R, C), F32)
    return pl.pallas_call(
        body, name=name, grid=(C // tc,), out_shape=(out, out, out, out),
        in_specs=[spec, pl.BlockSpec((4, R, tc), lambda j: (0, 0, j)), spec, spec], out_specs=(spec, spec, spec, spec),
        compiler_params=_params("parallel"),
    )(w, slots, m, v)


WEIGHT_NAMES = ("ffn1_norm", "ffn1_w_gate", "ffn1_w_up", "ffn1_w_down", "mix_norm", "w_in", "swa_sinks",
                "swa_out_norm", "sb_out_norm", "w_out", "ffn2_norm", "ffn2_w_gate", "ffn2_w_up", "ffn2_w_down",
                "final_norm")
SMALL_NAMES = ("ffn1_norm", "mix_norm", "swa_sinks", "swa_out_norm", "sb_out_norm", "ffn2_norm", "final_norm")
BIG_ARGS = {"ffn1_gate": ("ffn1_w_gate", True), "ffn1_up": ("ffn1_w_up", True), "ffn1_down": ("ffn1_w_down", False),
            "w_in": ("w_in", True), "w_out": ("w_out", False), "ffn2_gate": ("ffn2_w_gate", True),
            "ffn2_up": ("ffn2_w_up", True), "ffn2_down": ("ffn2_w_down", False)}


def _pack_small(parts):
    padded = [jnp.pad(p.reshape(1, -1), ((0, 0), (0, -p.size % LANES))) for p in parts]
    flat = jnp.concatenate(padded, axis=1)
    flat = jnp.pad(flat, ((0, 0), (0, SMALL_ROWS * LANES - flat.shape[1])))
    return flat.reshape(SMALL_ROWS, LANES)


def _unpack_small(block, shapes):
    flat = block.reshape(-1)
    out, off = [], 0
    for shp in shapes:
        n = 1
        for s in shp:
            n *= s
        out.append(flat[off:off + n].reshape(shp))
        off += n + (-n % LANES)
    return out


def kernel(x, ffn1_norm, ffn1_w_gate, ffn1_w_up, ffn1_w_down, mix_norm, w_in, swa_sinks, swa_out_norm, sb_out_norm, w_out, ffn2_norm, ffn2_w_gate, ffn2_w_up, ffn2_w_down, final_norm, loss_target, m_ffn1_norm, m_ffn1_w_gate, m_ffn1_w_up, m_ffn1_w_down, m_mix_norm, m_w_in, m_swa_sinks, m_swa_out_norm, m_sb_out_norm, m_w_out, m_ffn2_norm, m_ffn2_w_gate, m_ffn2_w_up, m_ffn2_w_down, m_final_norm, v_ffn1_norm, v_ffn1_w_gate, v_ffn1_w_up, v_ffn1_w_down, v_mix_norm, v_w_in, v_swa_sinks, v_swa_out_norm, v_sb_out_norm, v_w_out, v_ffn2_norm, v_ffn2_w_gate, v_ffn2_w_up, v_ffn2_w_down, v_final_norm):
    args = dict(locals())
    B, S, D = x.shape
    T = B * S
    weights = {n: args[n] for n in WEIGHT_NAMES}
    mom_m = {n: args["m_" + n] for n in WEIGHT_NAMES}
    mom_v = {n: args["v_" + n] for n in WEIGHT_NAMES}

    shards = {}
    for name in BIG_NAMES:
        arg, transposed = BIG_ARGS[name]
        w2 = weights[arg][0]
        shards[name] = (w2.T if transposed else w2).astype(_WIRE_DTYPE)
    comm = _Comm(shards)
    small = {n: weights[n].reshape(1, -1) for n in SMALL_NAMES}

    loss, gx, d_small = _layer_step(x.reshape(T, D), loss_target.reshape(T, D), B, S, small, comm)

    small_shapes = [(1, 1)] + [d_small[n].shape for n in SMALL_NAMES]
    reduced = _small_allreduce(_pack_small([loss[:, :1]] + [d_small[n] for n in SMALL_NAMES]))
    red = _unpack_small(reduced, small_shapes)
    loss_out = red[0].reshape(())
    g_small = dict(zip(SMALL_NAMES, red[1:]))

    grads, deltas, new_m, new_v = {}, {}, {}, {}
    for name in BIG_NAMES:
        arg, transposed = BIG_ARGS[name]
        to_rows = (lambda t: t[0].T) if transposed else (lambda t: t[0])
        back = (lambda t: t.T[None]) if transposed else (lambda t: t[None])
        slots = comm.slots[name].reshape(4, comm.rows[name], D)
        res = _adamw_slots(to_rows(weights[arg]), slots, to_rows(mom_m[arg]), to_rows(mom_v[arg]), "adamw_" + name)
        grads[arg], deltas[arg], new_m[arg], new_v[arg] = [back(t) for t in res]
    shapes1 = [(1, weights[n].size) for n in SMALL_NAMES]
    packed = [_pack_small([t[n].reshape(1, -1) for n in SMALL_NAMES]) for t in (weights, g_small, mom_m, mom_v)]
    upd = _adamw(*packed, "adamw_small")
    for tgt_dict, block in zip((deltas, new_m, new_v), upd):
        for n, val in zip(SMALL_NAMES, _unpack_small(block, shapes1)):
            tgt_dict[n] = val.reshape(weights[n].shape)
    for n in SMALL_NAMES:
        grads[n] = g_small[n].reshape(weights[n].shape)

    return (loss_out, gx.reshape(B, S, D), *[grads[n] for n in WEIGHT_NAMES], *[deltas[n] for n in WEIGHT_NAMES],
            *[new_m[n] for n in WEIGHT_NAMES], *[new_v[n] for n in WEIGHT_NAMES])
```

```python
import functools

import jax
import jax.numpy as jnp
from jax import lax
from jax.experimental import pallas as pl
from jax.experimental.pallas import tpu as pltpu

F32 = jnp.float32
_MXU_DTYPE = jnp.bfloat16
_WIRE_DTYPE = jnp.bfloat16

EPS = 1e-6
HEAD_DIM = 64
N_SWA_HEADS = 16
N_SWA_KV = 4
N_SB_HEADS = 16
WINDOW = 128
SWA_Q = N_SWA_HEADS * HEAD_DIM
SWA_KV = N_SWA_KV * HEAD_DIM
SB_W = N_SB_HEADS * HEAD_DIM
IN_W = SWA_Q + 2 * SWA_KV + 3 * SB_W
LANES = 128
ATT_SCALE = HEAD_DIM ** -0.5

ADAM_LR = 0.001
ADAM_B1 = 0.9
ADAM_B2 = 0.999
ADAM_EPS = 1e-08
ADAM_WD = 0.01
ADAM_STEP = 10

N_DEV = 8
_VMEM_LIMIT_BYTES = 56 * 1024 * 1024
_F_TILE = 512


def _params(*semantics):
    return pltpu.CompilerParams(dimension_semantics=semantics, vmem_limit_bytes=_VMEM_LIMIT_BYTES)


def _tile(n, pref, align):
    t = min(n, pref)
    t -= t % align
    while t >= align:
        if n % t == 0:
            return t
        t -= align
    return n


def _dot(a, b):
    return lax.dot_general(a, b, (((1,), (0,)), ((), ())), preferred_element_type=F32)


def _dot_nt(a, b):
    return lax.dot_general(a, b, (((1,), (1,)), ((), ())), preferred_element_type=F32)


def _dot_tn(a, b):
    return lax.dot_general(a, b, (((0,), (0,)), ((), ())), preferred_element_type=F32)


class _Job:
    def __init__(self, ins, out_shape, scratch, start, finish, mid=None):
        self.ins, self.out_shape, self.scratch = list(ins), list(out_shape), list(scratch)
        self.start, self.mid, self.finish = start, mid, finish


_JOB_MID_FRACTION = 0.8


def _call(body, *, name, grid, in_specs, out_specs, out_shape, args, semantics, scratch_shapes=(), jobs=()):
    single = not isinstance(out_shape, (tuple, list))
    if not jobs:
        res = pl.pallas_call(body, name=name, grid=grid, in_specs=list(in_specs), out_specs=out_specs,
                             out_shape=out_shape, scratch_shapes=list(scratch_shapes),
                             compiler_params=_params(*semantics))(*args)
        return res, []
    base_out = [out_shape] if single else list(out_shape)
    base_out_specs = [out_specs] if single else list(out_specs)
    n_in, n_out, n_scr = len(args), len(base_out), len(scratch_shapes)
    any_spec = pl.BlockSpec(memory_space=pl.ANY)
    total = 1
    for g in grid:
        total *= g
    mid_step = min(total - 1, int(total * _JOB_MID_FRACTION))

    def wrapped(*refs):
        pos = n_in
        job_ins = []
        for job in jobs:
            job_ins.append(refs[pos:pos + len(job.ins)])
            pos += len(job.ins)
        outs = refs[pos:pos + n_out]
        pos += n_out
        job_outs = []
        for job in jobs:
            job_outs.append(refs[pos:pos + len(job.out_shape)])
            pos += len(job.out_shape)
        scr = refs[pos:pos + n_scr]
        pos += n_scr
        job_scr = []
        for job in jobs:
            job_scr.append(refs[pos:pos + len(job.scratch)])
            pos += len(job.scratch)
        step = pl.program_id(0)
        for d in range(1, len(grid)):
            step = step * grid[d] + pl.program_id(d)

        @pl.when(step == 0)
        def _():
            for job, ji, jo, js in zip(jobs, job_ins, job_outs, job_scr):
                job.start(ji, jo, js)

        @pl.when(step == mid_step)
        def _():
            for job, ji, jo, js in zip(jobs, job_ins, job_outs, job_scr):
                if job.mid is not None:
                    job.mid(ji, jo, js)

        body(*refs[:n_in], *outs, *scr)

        @pl.when(step == total - 1)
        def _():
            for job, ji, jo, js in zip(jobs, job_ins, job_outs, job_scr):
                job.finish(ji, jo, js)

    all_args = list(args) + [a for job in jobs for a in job.ins]
    all_in_specs = list(in_specs) + [any_spec for job in jobs for _ in job.ins]
    all_out_shape = base_out + [s for job in jobs for s in job.out_shape]
    all_out_specs = base_out_specs + [any_spec for job in jobs for _ in job.out_shape]
    all_scratch = list(scratch_shapes) + [s for job in jobs for s in job.scratch]
    res = pl.pallas_call(
        wrapped, name=name, grid=grid, in_specs=all_in_specs, out_specs=tuple(all_out_specs),
        out_shape=tuple(all_out_shape), scratch_shapes=all_scratch,
        compiler_params=pltpu.CompilerParams(dimension_semantics=("arbitrary",) * len(grid),
                                             vmem_limit_bytes=_VMEM_LIMIT_BYTES, has_side_effects=True),
    )(*all_args)
    base = res[0] if single else tuple(res[:n_out])
    job_res, pos = [], n_out
    for job in jobs:
        job_res.append(tuple(res[pos:pos + len(job.out_shape)]))
        pos += len(job.out_shape)
    return base, job_res


def _rms_fwd(x, g, name, jobs=()):
    T, D = x.shape
    tm = _tile(T, 512, 16)

    def body(x_ref, g_ref, o_ref):
        xv = x_ref[...]
        r = lax.rsqrt(jnp.mean(xv * xv, axis=-1, keepdims=True) + EPS)
        o_ref[...] = (xv * r * g_ref[...]).astype(o_ref.dtype)

    return _call(
        body, name=name, grid=(T // tm,),
        out_shape=jax.ShapeDtypeStruct((T, D), _MXU_DTYPE),
        in_specs=[pl.BlockSpec((tm, D), lambda i: (i, 0)), pl.BlockSpec((1, D), lambda i: (0, 0))],
        out_specs=pl.BlockSpec((tm, D), lambda i: (i, 0)), args=(x, g), semantics=("parallel",), jobs=jobs)


def _rms_bwd_rows(dh, xv, g):
    r = lax.rsqrt(jnp.mean(xv * xv, axis=-1, keepdims=True) + EPS)
    xhat = xv * r
    u = dh * g
    dx = r * (u - xhat * jnp.mean(u * xhat, axis=-1, keepdims=True))
    return dx, dh * xhat


def _rms_bwd(dh, x, g, dres, name):
    T, D = x.shape
    tm = _tile(T, 256, 16)

    def body(dh_ref, x_ref, g_ref, dres_ref, dx_ref, dxb_ref, dg_ref):
        @pl.when(pl.program_id(0) == 0)
        def _():
            dg_ref[...] = jnp.zeros_like(dg_ref)

        dx, dgr = _rms_bwd_rows(dh_ref[...], x_ref[...], g_ref[...])
        dx = dres_ref[...] + dx
        dx_ref[...] = dx
        dxb_ref[...] = dx.astype(dxb_ref.dtype)
        dg_ref[...] += jnp.sum(dgr, axis=0, keepdims=True)

    row = pl.BlockSpec((tm, D), lambda i: (i, 0))
    vec = pl.BlockSpec((1, D), lambda i: (0, 0))
    return pl.pallas_call(
        body, name=name, grid=(T // tm,),
        out_shape=(jax.ShapeDtypeStruct((T, D), F32), jax.ShapeDtypeStruct((T, D), _MXU_DTYPE),
                   jax.ShapeDtypeStruct((1, D), F32)),
        in_specs=[row, row, vec, row], out_specs=(row, row, vec),
        compiler_params=_params("arbitrary"),
    )(dh, x, g, dres)


def _loss_head(x, g, tgt, name):
    T, D = x.shape
    tm = _tile(T, 256, 16)

    def body(x_ref, g_ref, t_ref, dx_ref, dxb_ref, dg_ref, loss_ref):
        @pl.when(pl.program_id(0) == 0)
        def _():
            dg_ref[...] = jnp.zeros_like(dg_ref)
            loss_ref[...] = jnp.zeros_like(loss_ref)

        xv = x_ref[...]
        gv = g_ref[...]
        r = lax.rsqrt(jnp.mean(xv * xv, axis=-1, keepdims=True) + EPS)
        xhat = xv * r
        diff = xhat * gv - t_ref[...]
        tok = jnp.mean(diff * diff, axis=-1, keepdims=True)
        loss_ref[...] += 0.5 * jnp.sum(tok, axis=0, keepdims=True)
        dy = diff / D
        u = dy * gv
        dx = r * (u - xhat * jnp.mean(u * xhat, axis=-1, keepdims=True))
        dx_ref[...] = dx
        dxb_ref[...] = dx.astype(dxb_ref.dtype)
        dg_ref[...] += jnp.sum(dy * xhat, axis=0, keepdims=True)

    row = pl.BlockSpec((tm, D), lambda i: (i, 0))
    vec = pl.BlockSpec((1, D), lambda i: (0, 0))
    return pl.pallas_call(
        body, name=name, grid=(T // tm,),
        out_shape=(jax.ShapeDtypeStruct((T, D), F32), jax.ShapeDtypeStruct((T, D), _MXU_DTYPE),
                   jax.ShapeDtypeStruct((1, D), F32), jax.ShapeDtypeStruct((1, LANES), F32)),
        in_specs=[row, vec, row],
        out_specs=(row, row, vec, pl.BlockSpec((1, LANES), lambda i: (0, 0))),
        compiler_params=_params("arbitrary"),
    )(x, g, tgt)


def _outnorm_fwd(ya, yb, ga, gb, name):
    T, W = ya.shape
    tm = _tile(T, 512, 16)

    def body(ya_ref, yb_ref, ga_ref, gb_ref, o_ref):
        for k, (y_ref, g_ref) in enumerate(((ya_ref, ga_ref), (yb_ref, gb_ref))):
            yv = y_ref[...]
            r = lax.rsqrt(jnp.mean(yv * yv, axis=-1, keepdims=True) + EPS)
            o_ref[:, k * W:(k + 1) * W] = (yv * r * g_ref[...]).astype(o_ref.dtype)

    row = pl.BlockSpec((tm, W), lambda i: (i, 0))
    vec = pl.BlockSpec((1, W), lambda i: (0, 0))
    return pl.pallas_call(
        body, name=name, grid=(T // tm,),
        out_shape=jax.ShapeDtypeStruct((T, 2 * W), _MXU_DTYPE),
        in_specs=[row, row, vec, vec], out_specs=pl.BlockSpec((tm, 2 * W), lambda i: (i, 0)),
        compiler_params=_params("parallel"),
    )(ya, yb, ga, gb)


def _outnorm_bwd(dyn, ya, yb, ga, gb, name):
    T, W = ya.shape
    tm = _tile(T, 256, 16)

    def body(d_ref, ya_ref, yb_ref, ga_ref, gb_ref, dya_ref, dyb_ref, dga_ref, dgb_ref):
        @pl.when(pl.program_id(0) == 0)
        def _():
            dga_ref[...] = jnp.zeros_like(dga_ref)
            dgb_ref[...] = jnp.zeros_like(dgb_ref)

        for k, (y_ref, g_ref, dy_ref, dg_ref) in enumerate(
                ((ya_ref, ga_ref, dya_ref, dga_ref), (yb_ref, gb_ref, dyb_ref, dgb_ref))):
            dy, dgr = _rms_bwd_rows(d_ref[:, k * W:(k + 1) * W], y_ref[...], g_ref[...])
            dy_ref[...] = dy.astype(dy_ref.dtype)
            dg_ref[...] += jnp.sum(dgr, axis=0, keepdims=True)

    row = pl.BlockSpec((tm, W), lambda i: (i, 0))
    vec = pl.BlockSpec((1, W), lambda i: (0, 0))
    return pl.pallas_call(
        body, name=name, grid=(T // tm,),
        out_shape=(jax.ShapeDtypeStruct((T, W), _MXU_DTYPE), jax.ShapeDtypeStruct((T, W), _MXU_DTYPE),
                   jax.ShapeDtypeStruct((1, W), F32), jax.ShapeDtypeStruct((1, W), F32)),
        in_specs=[pl.BlockSpec((tm, 2 * W), lambda i: (i, 0)), row, row, vec, vec],
        out_specs=(row, row, vec, vec),
        compiler_params=_params("arbitrary"),
    )(dyn, ya, yb, ga, gb)


def _ffn_gu(h, wg_t, wu_t, name, jobs=()):
    T, D = h.shape
    Fp = wg_t.shape[0]
    tm = _tile(T, 1024, 16)
    tn = _tile(Fp, _F_TILE, LANES)

    def body(h_ref, wg_ref, wu_ref, g_ref, u_ref, a_ref):
        hv = h_ref[...]
        g = _dot_nt(hv, wg_ref[...])
        u = _dot_nt(hv, wu_ref[...])
        g_ref[...] = g.astype(g_ref.dtype)
        u_ref[...] = u.astype(u_ref.dtype)
        a_ref[...] = (g * jax.nn.sigmoid(g) * u).astype(a_ref.dtype)

    act = pl.BlockSpec((tm, tn), lambda n, m: (m, n))
    wsp = pl.BlockSpec((tn, D), lambda n, m: (n, 0))
    out = jax.ShapeDtypeStruct((T, Fp), _MXU_DTYPE)
    return _call(
        body, name=name, grid=(Fp // tn, T // tm), out_shape=(out, out, out),
        in_specs=[pl.BlockSpec((tm, D), lambda n, m: (m, 0)), wsp, wsp],
        out_specs=(act, act, act), args=(h, wg_t, wu_t), semantics=("parallel", "parallel"), jobs=jobs)


def _ffn_bwd_act(dxb, wd, G, U, name, jobs=()):
    T, D = dxb.shape
    Fp = wd.shape[0]
    tm = _tile(T, 1024, 16)
    tn = _tile(Fp, _F_TILE, LANES)

    def body(e_ref, wd_ref, g_ref, u_ref, dg_ref, du_ref):
        da = 0.5 * _dot_nt(e_ref[...], wd_ref[...])
        g = g_ref[...].astype(F32)
        u = u_ref[...].astype(F32)
        s = jax.nn.sigmoid(g)
        du_ref[...] = (da * (g * s)).astype(du_ref.dtype)
        dg_ref[...] = (da * u * (s * (1.0 + g * (1.0 - s)))).astype(dg_ref.dtype)

    act = pl.BlockSpec((tm, tn), lambda n, m: (m, n))
    out = jax.ShapeDtypeStruct((T, Fp), _MXU_DTYPE)
    return _call(
        body, name=name, grid=(Fp // tn, T // tm), out_shape=(out, out),
        in_specs=[pl.BlockSpec((tm, D), lambda n, m: (m, 0)), pl.BlockSpec((tn, D), lambda n, m: (n, 0)),
                  act, act],
        out_specs=(act, act), args=(dxb, wd, G, U), semantics=("parallel", "parallel"), jobs=jobs)


def _ffn_gate(h, wg_t, U, name, jobs=()):
    T, D = h.shape
    Fp = wg_t.shape[0]
    tm = _tile(T, 1024, 16)
    tn = _tile(Fp, _F_TILE, LANES)

    def body(h_ref, wg_ref, u_ref, g_ref, a_ref):
        g = _dot_nt(h_ref[...], wg_ref[...])
        g_ref[...] = g.astype(g_ref.dtype)
        a_ref[...] = (g * jax.nn.sigmoid(g) * u_ref[...].astype(F32)).astype(a_ref.dtype)

    act = pl.BlockSpec((tm, tn), lambda n, m: (m, n))
    out = jax.ShapeDtypeStruct((T, Fp), _MXU_DTYPE)
    return _call(
        body, name=name, grid=(Fp // tn, T // tm), out_shape=(out, out),
        in_specs=[pl.BlockSpec((tm, D), lambda n, m: (m, 0)), pl.BlockSpec((tn, D), lambda n, m: (n, 0)), act],
        out_specs=(act, act), args=(h, wg_t, U), semantics=("parallel", "parallel"), jobs=jobs)


def _mm_nt(a, b, out_dtype, name, jobs=()):
    M, K = a.shape
    N = b.shape[0]
    tm = _tile(M, 1024, 16)
    tn = _tile(N, 512, LANES)

    def body(a_ref, b_ref, o_ref):
        o_ref[...] = _dot_nt(a_ref[...], b_ref[...]).astype(o_ref.dtype)

    return _call(
        body, name=name, grid=(N // tn, M // tm), out_shape=jax.ShapeDtypeStruct((M, N), out_dtype),
        in_specs=[pl.BlockSpec((tm, K), lambda n, m: (m, 0)), pl.BlockSpec((tn, K), lambda n, m: (n, 0))],
        out_specs=pl.BlockSpec((tm, tn), lambda n, m: (m, n)), args=(a, b),
        semantics=("parallel", "parallel"), jobs=jobs)


_MM_OPERAND_BYTES = 26 * 1024 * 1024


def _k_tile(K, bytes_per_k, align):
    best = align
    for t in range(align, K + 1, align):
        if K % t == 0 and 2 * t * bytes_per_k <= _MM_OPERAND_BYTES:
            best = t
    return best


def _mm_nn(pairs, res, alpha, out_dtype, name, jobs=()):
    M, K = pairs[0][0].shape
    N = pairs[0][1].shape[1]
    n_pairs = len(pairs)
    tm = _tile(M, 1024, 16)
    tn = _tile(N, 1024, LANES)
    tk = _k_tile(K, n_pairs * (tm + tn) * pairs[0][0].dtype.itemsize, LANES)
    nk = K // tk

    def body(*refs):
        ab = refs[:2 * n_pairs]
        res_ref = refs[2 * n_pairs] if res is not None else None
        o_ref = refs[2 * n_pairs + (res is not None)]

        def finish(acc):
            out = alpha * acc
            if res_ref is not None:
                out = res_ref[...] + out
            o_ref[...] = out.astype(o_ref.dtype)

        part = _dot(ab[0][...], ab[1][...])
        for i in range(1, n_pairs):
            part = part + _dot(ab[2 * i][...], ab[2 * i + 1][...])
        if nk == 1:
            finish(part)
        else:
            acc_ref = refs[-1]
            k = pl.program_id(2)

            @pl.when(k == 0)
            def _():
                acc_ref[...] = part

            @pl.when(k > 0)
            def _():
                acc_ref[...] += part

            @pl.when(k == nk - 1)
            def _():
                finish(acc_ref[...])

    in_specs, args = [], []
    for a, b in pairs:
        in_specs += [pl.BlockSpec((tm, tk), lambda m, n, k: (m, k)), pl.BlockSpec((tk, tn), lambda m, n, k: (k, n))]
        args += [a, b]
    if res is not None:
        in_specs.append(pl.BlockSpec((tm, tn), lambda m, n, k: (m, n)))
        args.append(res)
    return _call(
        body, name=name, grid=(M // tm, N // tn, nk), out_shape=jax.ShapeDtypeStruct((M, N), out_dtype),
        in_specs=in_specs, out_specs=pl.BlockSpec((tm, tn), lambda m, n, k: (m, n)),
        scratch_shapes=[pltpu.VMEM((tm, tn), F32)] if nk > 1 else [], args=args,
        semantics=("parallel", "parallel", "arbitrary"), jobs=jobs)


def _mm_tn(a, b, alpha, out_dtype, name, jobs=()):
    K, M = a.shape
    N = b.shape[1]
    tm = _tile(M, 512, LANES)
    tn = _tile(N, 1024, LANES)

    def body(a_ref, b_ref, o_ref):
        o_ref[...] = (alpha * _dot_tn(a_ref[...], b_ref[...])).astype(o_ref.dtype)

    return _call(
        body, name=name, grid=(N // tn, M // tm), out_shape=jax.ShapeDtypeStruct((M, N), out_dtype),
        in_specs=[pl.BlockSpec((K, tm), lambda n, m: (0, m)), pl.BlockSpec((K, tn), lambda n, m: (0, n))],
        out_specs=pl.BlockSpec((tm, tn), lambda n, m: (m, n)), args=(a, b),
        semantics=("parallel", "parallel"), jobs=jobs)


def _half_masks():
    lane = lax.broadcasted_iota(jnp.int32, (1, LANES), 1)
    return (lane < HEAD_DIM, lane >= HEAD_DIM)


def _swap_halves(v):
    return pltpu.roll(v.astype(F32), HEAD_DIM, 1).astype(v.dtype)


def _swa_geometry(n):
    qi = lax.broadcasted_iota(jnp.int32, (WINDOW, 2 * WINDOW), 0)
    kp = lax.broadcasted_iota(jnp.int32, (WINDOW, 2 * WINDOW), 1)
    dist = (WINDOW + qi) - kp
    valid = (dist >= 0) & (dist < WINDOW) & ((n > 0) | (kp >= WINDOW))
    return dist.astype(F32), valid


def _swa_slope(h):
    return 2.0 ** (-8.0 * (h + 1) / N_SWA_HEADS)


def _swa_softmax(qk, sink, slope, distf, valid):
    s = qk * ATT_SCALE - slope * distf
    s = jnp.where(valid, s, -1e30)
    m = jnp.maximum(jnp.max(s, axis=1, keepdims=True), sink)
    p = jnp.exp(s - m)
    e_sink = jnp.exp(sink - m)
    den = jnp.sum(p, axis=1, keepdims=True) + e_sink
    return p / den, e_sink / den


def _swa_group_heads(g):
    return [(2 * pp + a, pp, a) for pp in (2 * g, 2 * g + 1) for a in range(2)]


def _swa_specs(B, S):
    nb = S // WINDOW
    kcol = SWA_Q // SWA_KV
    cur = lambda b, n: (b * nb + n, kcol)
    prev = lambda b, n: (b * nb + jnp.maximum(n - 1, 0), kcol)
    curv = lambda b, n: (b * nb + n, kcol + 1)
    prevv = lambda b, n: (b * nb + jnp.maximum(n - 1, 0), kcol + 1)
    q_spec = pl.BlockSpec((WINDOW, SWA_Q), lambda b, n: (b * nb + n, 0))
    kv = [pl.BlockSpec((WINDOW, SWA_KV), f) for f in (prev, cur, prevv, curv)]
    sink_spec = pl.BlockSpec(memory_space=pltpu.SMEM)
    return nb, q_spec, kv, sink_spec


def _swa_kv_views(kp_ref, kc_ref, vp_ref, vc_ref, g):
    hm = _half_masks()
    c0 = (g // 2) * LANES
    k_all = jnp.concatenate([kp_ref[:, c0:c0 + LANES], kc_ref[:, c0:c0 + LANES]], axis=0)
    v_all = jnp.concatenate([vp_ref[:, c0:c0 + LANES], vc_ref[:, c0:c0 + LANES]], axis=0)
    b = g % 2
    ks, vs = [None, None], [None, None]
    ks[b], vs[b] = k_all, v_all
    ks[1 - b], vs[1 - b] = _swap_halves(k_all), _swap_halves(v_all)
    ks = [jnp.where(hm[a], ks[a], 0) for a in range(2)]
    vs = [jnp.where(hm[a], vs[a], 0) for a in range(2)]
    return ks, vs


def _swa_fwd(proj, sinks, B, S, name, jobs=()):
    T = B * S
    nb, q_spec, kv_specs, sink_spec = _swa_specs(B, S)

    def body(sink_ref, q_ref, kp_ref, kc_ref, vp_ref, vc_ref, y_ref):
        hm = _half_masks()
        distf, valid = _swa_geometry(pl.program_id(1))
        for g in range(N_SWA_KV):
            ks, vs = _swa_kv_views(kp_ref, kc_ref, vp_ref, vc_ref, g)
            heads = _swa_group_heads(g)
            qk = [_dot_nt(jnp.where(hm[a], q_ref[:, pp * LANES:(pp + 1) * LANES], 0), ks[a]) for _, pp, a in heads]
            p = [_swa_softmax(qk[i], sink_ref[0, h], _swa_slope(h), distf, valid)[0] for i, (h, _, _) in enumerate(heads)]
            o = [_dot(p[i].astype(_MXU_DTYPE), vs[a]) for i, (_, _, a) in enumerate(heads)]
            for j, pp in enumerate((2 * g, 2 * g + 1)):
                y_ref[:, pp * LANES:(pp + 1) * LANES] = o[2 * j] + o[2 * j + 1]

    return _call(
        body, name=name, grid=(B, nb), out_shape=jax.ShapeDtypeStruct((T, SWA_Q), F32),
        in_specs=[sink_spec, q_spec] + kv_specs,
        out_specs=pl.BlockSpec((WINDOW, SWA_Q), lambda b, n: (b * nb + n, 0)),
        args=(sinks, proj, proj, proj, proj, proj), semantics=("parallel", "parallel"), jobs=jobs)


def _swa_bwd(proj, sinks, dya, B, S, name):
    T = B * S
    nb, q_spec, kv_specs, sink_spec = _swa_specs(B, S)

    def body(sink_ref, q_ref, kp_ref, kc_ref, vp_ref, vc_ref, do_ref,
             dq_ref, dk_ref, dv_ref, dsink_ref, dk_acc, dv_acc):
        b_id, n = pl.program_id(0), pl.program_id(1)
        hm = _half_masks()
        lane = lax.broadcasted_iota(jnp.int32, (1, LANES), 1)

        @pl.when((b_id == 0) & (n == 0))
        def _():
            dsink_ref[...] = jnp.zeros_like(dsink_ref)

        @pl.when(n == 0)
        def _():
            dk_acc[...] = jnp.zeros_like(dk_acc)
            dv_acc[...] = jnp.zeros_like(dv_acc)

        distf, valid = _swa_geometry(n)
        r_prev = pl.multiple_of(jnp.maximum(n - 1, 0) * WINDOW, WINDOW)
        r_cur = pl.multiple_of(n * WINDOW, WINDOW)
        dsink = jnp.zeros((1, LANES), F32)
        for g in range(N_SWA_KV):
            ks, vs = _swa_kv_views(kp_ref, kc_ref, vp_ref, vc_ref, g)
            heads = _swa_group_heads(g)
            four = range(len(heads))
            qms = [jnp.where(hm[a], q_ref[:, pp * LANES:(pp + 1) * LANES], 0) for _, pp, a in heads]
            doms = [jnp.where(hm[a], do_ref[:, pp * LANES:(pp + 1) * LANES], 0) for _, pp, a in heads]
            qk = [_dot_nt(qms[i], ks[heads[i][2]]) for i in four]
            dp = [_dot_nt(doms[i], vs[heads[i][2]]) for i in four]
            soft = [_swa_softmax(qk[i], sink_ref[0, heads[i][0]], _swa_slope(heads[i][0]), distf, valid) for i in four]
            p = [soft[i][0] for i in four]
            delta = [jnp.sum(p[i] * dp[i], axis=1, keepdims=True) for i in four]
            ds = [(p[i] * (dp[i] - delta[i]) * ATT_SCALE).astype(_MXU_DTYPE) for i in four]
            for i in four:
                dsink = dsink + jnp.where(lane == heads[i][0], -jnp.sum(soft[i][1] * delta[i]), 0.0)
            dq = [_dot(ds[i], ks[heads[i][2]]) for i in four]
            dk_h = [_dot_tn(ds[i], qms[i]) for i in four]
            dv_h = [_dot_tn(p[i].astype(_MXU_DTYPE), doms[i]) for i in four]
            for j, pp in enumerate((2 * g, 2 * g + 1)):
                dq_ref[:, pp * LANES:(pp + 1) * LANES] = (dq[2 * j] + dq[2 * j + 1]).astype(dq_ref.dtype)
            dk_g = [dk_h[a] + dk_h[2 + a] for a in range(2)]
            dv_g = [dv_h[a] + dv_h[2 + a] for a in range(2)]
            bsel = g % 2
            dk_t = dk_g[bsel] + pltpu.roll(dk_g[1 - bsel], HEAD_DIM, 1)
            dv_t = dv_g[bsel] + pltpu.roll(dv_g[1 - bsel], HEAD_DIM, 1)
            c0 = (g // 2) * LANES
            dk_acc[pl.ds(r_prev, WINDOW), c0:c0 + LANES] += dk_t[:WINDOW]
            dk_acc[pl.ds(r_cur, WINDOW), c0:c0 + LANES] += dk_t[WINDOW:]
            dv_acc[pl.ds(r_prev, WINDOW), c0:c0 + LANES] += dv_t[:WINDOW]
            dv_acc[pl.ds(r_cur, WINDOW), c0:c0 + LANES] += dv_t[WINDOW:]
        dsink_ref[...] += dsink

        @pl.when(n == nb - 1)
        def _():
            dk_ref[...] = dk_acc[...].astype(dk_ref.dtype)
            dv_ref[...] = dv_acc[...].astype(dv_ref.dtype)

    seq_kv = pl.BlockSpec((S, SWA_KV), lambda b, n: (b, 0))
    return pl.pallas_call(
        body, name=name, grid=(B, nb),
        out_shape=(jax.ShapeDtypeStruct((T, SWA_Q), _MXU_DTYPE), jax.ShapeDtypeStruct((T, SWA_KV), _MXU_DTYPE),
                   jax.ShapeDtypeStruct((T, SWA_KV), _MXU_DTYPE), jax.ShapeDtypeStruct((1, LANES), F32)),
        in_specs=[sink_spec, q_spec] + kv_specs + [pl.BlockSpec((WINDOW, SWA_Q), lambda b, n: (b * nb + n, 0))],
        out_specs=(pl.BlockSpec((WINDOW, SWA_Q), lambda b, n: (b * nb + n, 0)), seq_kv, seq_kv,
                   pl.BlockSpec((1, LANES), lambda b, n: (0, 0))),
        scratch_shapes=[pltpu.VMEM((S, SWA_KV), F32), pltpu.VMEM((S, SWA_KV), F32)],
        compiler_params=_params("arbitrary", "arbitrary"),
    )(sinks, proj, proj, proj, proj, proj, dya)


SB_TILE = 256
SB_HALF = 128
SB_DEAD = -105.0


def _mark_lanes():
    lane = lax.broadcasted_iota(jnp.int32, (1, LANES), 1)
    return (lane == HEAD_DIM - 1) | (lane == LANES - 1)


def _tri2(cond):
    j = lax.broadcasted_iota(jnp.int32, (2 * SB_HALF, SB_HALF), 0) & (SB_HALF - 1)
    s = lax.broadcasted_iota(jnp.int32, (2 * SB_HALF, SB_HALF), 1)
    return cond(j, s).astype(_MXU_DTYPE)


def _half_cumsums(x, tri2):
    out = []
    for h in range(2):
        xh = x[:, h * SB_HALF:(h + 1) * SB_HALF]
        hi = xh.astype(_MXU_DTYPE)
        lo = (xh - hi.astype(F32)).astype(_MXU_DTYPE)
        out.append(_dot(jnp.concatenate([hi, lo], axis=1), tri2))
    return out


def _log_sigmoid(z):
    return jnp.minimum(z, 0.0) - jnp.log(1.0 + jnp.exp(-jnp.abs(z)))


def _sb_specs(B, S):
    qb = (SWA_Q + 2 * SWA_KV) // LANES
    kb = qb + SB_W // LANES
    vb = kb + SB_W // LANES
    return [pl.BlockSpec((S, LANES), functools.partial(lambda b, p, c: (b, c + p), c=c)) for c in (qb, kb, vb)]


def _sb_fwd(proj, B, S, name, jobs=()):
    T = B * S
    tq = SB_TILE
    nq = S // tq

    def body(q_ref, k_ref, v_ref, y_ref, tot_ref):
        hm = _half_masks()
        ji = lax.broadcasted_iota(jnp.int32, (tq, tq), 0)
        si = lax.broadcasted_iota(jnp.int32, (tq, tq), 1)
        tri_after = _tri2(lambda j, s: j > s)
        causal = si < ji
        mark = _mark_lanes()

        def q_loop(qi, carry):
            r0 = pl.multiple_of(qi * tq, tq)
            q_pair = q_ref[pl.ds(r0, tq), :] * ATT_SCALE
            qms = [jnp.where(hm[a], q_pair, 0) for a in range(2)]

            def tile(c0, state, diagonal):
                kk = k_ref[pl.ds(c0, tq), :]
                vv = v_ref[pl.ds(c0, tq), :]
                two = range(2)
                z = [_dot_nt(qms[a], kk) for a in two]
                lb = [_log_sigmoid(z[a]) for a in two]
                l1m = [jnp.where(causal, lb[a] - z[a], 0.0) if diagonal else lb[a] - z[a] for a in two]
                cum = [_half_cumsums(l1m[a], tri_after) for a in two]
                tot = [[cum[a][h][:, 0:1] + l1m[a][:, h * SB_HALF:h * SB_HALF + 1] for h in two] for a in two]
                after = [jnp.concatenate([cum[a][0] + (state[a][1] + tot[a][1]), cum[a][1] + state[a][1]], axis=1)
                         for a in two]
                att = [jnp.exp(lb[a] + after[a]) for a in two]
                if diagonal:
                    att = [jnp.where(causal, att[a], 0.0) for a in two]
                acc = [state[a][0] + _dot(att[a].astype(_MXU_DTYPE), jnp.where(hm[a], vv, 0)) for a in two]
                car = [state[a][1] + (tot[a][0] + tot[a][1]) for a in two]
                return tuple((acc[a], car[a]) for a in two)

            def live(st):
                return jnp.maximum(jnp.max(st[0][1]), jnp.max(st[1][1])) > SB_DEAD

            def step(c):
                it, _, st = c
                st = tile(pl.multiple_of((qi - 1 - it) * tq, tq), st, False)
                return it + 1, live(st), st

            zero = (jnp.zeros((tq, LANES), F32), jnp.zeros((tq, 1), F32))
            state = tile(r0, (zero, zero), True)
            done, _, state = lax.while_loop(lambda c: (c[0] < qi) & c[1], step, (jnp.int32(0), live(state), state))
            y_ref[pl.ds(r0, tq), :] = state[0][0] + state[1][0]
            first = (qi - done).astype(F32)
            tot_ref[pl.ds(r0, tq), :] = jnp.where(mark, first, jnp.where(hm[0], state[0][1], state[1][1]))
            return carry

        lax.fori_loop(0, nq, q_loop, 0)

    out_spec = pl.BlockSpec((S, LANES), lambda b, p: (b, p))
    return _call(
        body, name=name, grid=(B, SB_W // LANES),
        out_shape=(jax.ShapeDtypeStruct((T, SB_W), F32), jax.ShapeDtypeStruct((T, SB_W), F32)),
        in_specs=_sb_specs(B, S), out_specs=(out_spec, out_spec), args=(proj, proj, proj),
        semantics=("parallel", "parallel"), jobs=jobs)


def _sb_bwd(proj, tot, dyb, B, S, name, jobs=()):
    T = B * S
    tq = SB_TILE
    nq = S // tq

    def body(q_ref, k_ref, v_ref, do_ref, tot_ref, dq_ref, dk_ref, dv_ref, dk_acc, dv_acc):
        hm = _half_masks()
        ji = lax.broadcasted_iota(jnp.int32, (tq, tq), 0)
        si = lax.broadcasted_iota(jnp.int32, (tq, tq), 1)
        tri_incl = _tri2(lambda j, s: j <= s)
        tri_excl = _tri2(lambda j, s: j < s)
        causal = si < ji
        mark = _mark_lanes()
        dk_acc[...] = jnp.zeros_like(dk_acc)
        dv_acc[...] = jnp.zeros_like(dv_acc)

        def q_loop(qi, carry):
            r0 = pl.multiple_of(qi * tq, tq)
            q_pair = q_ref[pl.ds(r0, tq), :] * ATT_SCALE
            do_pair = do_ref[pl.ds(r0, tq), :]
            tot_pair = tot_ref[pl.ds(r0, tq), :]
            qms = [jnp.where(hm[a], q_pair, 0) for a in range(2)]
            doms = [jnp.where(hm[a], do_pair, 0) for a in range(2)]
            totals = [jnp.max(jnp.where(hm[a] & ~mark, tot_pair, -jnp.inf), axis=1, keepdims=True) for a in range(2)]
            first = jnp.max(jnp.where(mark, tot_pair, -jnp.inf))
            first = jnp.where((first >= 0.0) & (first <= qi.astype(F32)), first, 0.0).astype(jnp.int32)

            def tile(c0, state, diagonal):
                kk = k_ref[pl.ds(c0, tq), :]
                vv = v_ref[pl.ds(c0, tq), :]
                ks = kk * ATT_SCALE
                two = range(2)
                last = SB_HALF - 1
                z = [_dot_nt(qms[a], kk) for a in two]
                d_att = [_dot_nt(doms[a], vv) for a in two]
                lb = [_log_sigmoid(z[a]) for a in two]
                l1m = [jnp.where(causal, lb[a] - z[a], 0.0) if diagonal else lb[a] - z[a] for a in two]
                cum = [_half_cumsums(l1m[a], tri_incl) for a in two]
                upto = [jnp.concatenate([cum[a][0] + state[a][1],
                                         cum[a][1] + (state[a][1] + cum[a][0][:, last:last + 1])], axis=1) for a in two]
                att = [jnp.exp(lb[a] + (totals[a] - upto[a])) for a in two]
                if diagonal:
                    att = [jnp.where(causal, att[a], 0.0) for a in two]
                d_log = [d_att[a] * att[a] for a in two]
                cumd = [_half_cumsums(d_log[a], tri_excl) for a in two]
                totd = [[cumd[a][h][:, last:last + 1] + d_log[a][:, h * SB_HALF + last:h * SB_HALF + last + 1]
                         for h in two] for a in two]
                before = [jnp.concatenate([cumd[a][0] + state[a][2], cumd[a][1] + (state[a][2] + totd[a][0])], axis=1)
                          for a in two]
                sig = [jnp.exp(lb[a]) for a in two]
                dz = [d_log[a] * (1.0 - sig[a]) - sig[a] * before[a] for a in two]
                if diagonal:
                    dz = [jnp.where(causal, dz[a], 0.0) for a in two]
                dzb = [dz[a].astype(_MXU_DTYPE) for a in two]
                dq = [state[a][0] + _dot(dzb[a], jnp.where(hm[a], ks, 0)) for a in two]
                dk_acc[pl.ds(c0, tq), :] += _dot_tn(dzb[0], qms[0]) + _dot_tn(dzb[1], qms[1])
                dv_acc[pl.ds(c0, tq), :] += (_dot_tn(att[0].astype(_MXU_DTYPE), doms[0])
                                             + _dot_tn(att[1].astype(_MXU_DTYPE), doms[1]))
                cp = [upto[a][:, tq - 1:tq] for a in two]
                cq = [state[a][2] + (totd[a][0] + totd[a][1]) for a in two]
                return tuple((dq[a], cp[a], cq[a]) for a in two)

            zero_col = jnp.zeros((tq, 1), F32)
            zero = (jnp.zeros((tq, LANES), F32), zero_col, zero_col)
            state = lax.fori_loop(first, qi, lambda kj, st: tile(pl.multiple_of(kj * tq, tq), st, False), (zero, zero))
            state = tile(r0, state, True)
            dq_ref[pl.ds(r0, tq), :] = (state[0][0] + state[1][0]).astype(dq_ref.dtype)
            return carry

        lax.fori_loop(0, nq, q_loop, 0)
        dk_ref[...] = dk_acc[...].astype(dk_ref.dtype)
        dv_ref[...] = dv_acc[...].astype(dv_ref.dtype)

    pair = pl.BlockSpec((S, LANES), lambda b, p: (b, p))
    out = jax.ShapeDtypeStruct((T, SB_W), _MXU_DTYPE)
    return _call(
        body, name=name, grid=(B, SB_W // LANES), out_shape=(out, out, out),
        in_specs=_sb_specs(B, S) + [pair, pair], out_specs=(pair, pair, pair),
        scratch_shapes=[pltpu.VMEM((S, LANES), F32), pltpu.VMEM((S, LANES), F32)],
        args=(proj, proj, proj, dyb, tot), semantics=("parallel", "parallel"), jobs=jobs)


def _layer_step(x, tgt, B, S, small, comm):
    run, big, part = comm.run, comm.big, comm.partial
    ffn1_w, ffn2_w = ("ffn1_down", "ffn1_gate", "ffn1_up"), ("ffn2_down", "ffn2_gate", "ffn2_up")

    h1 = run(_rms_fwd, x, small["ffn1_norm"], "ffn1_rms", ag=("ffn1_up",))
    U1 = run(_mm_nt, h1, big["ffn1_up"], _MXU_DTYPE, "ffn1_up", ag=("ffn1_gate",))
    G1, A1 = run(_ffn_gate, h1, big["ffn1_gate"], U1, "ffn1_gate", ag=("ffn1_down",))
    x1 = run(_mm_nn, [(A1, big["ffn1_down"])], x, 0.5, F32, "ffn1_down", ag=("w_in",))
    h2 = run(_rms_fwd, x1, small["mix_norm"], "mix_rms")
    proj = run(_mm_nt, h2, big["w_in"], _MXU_DTYPE, "in_proj", ag=("w_out",))
    ya = run(_swa_fwd, proj, small["swa_sinks"], B, S, "swa_fwd", ag=("ffn2_gate",))
    yb, tot = run(_sb_fwd, proj, B, S, "sb_fwd", ag=("ffn2_up",))
    yn = _outnorm_fwd(ya, yb, small["swa_out_norm"], small["sb_out_norm"], "out_norm")
    x2 = run(_mm_nn, [(yn, big["w_out"])], x1, 1.0, F32, "out_proj")
    h3 = run(_rms_fwd, x2, small["ffn2_norm"], "ffn2_rms")
    G2, U2, A2 = run(_ffn_gu, h3, big["ffn2_gate"], big["ffn2_up"], "ffn2_gate_up", ag=("ffn2_down",))
    x3 = run(_mm_nn, [(A2, big["ffn2_down"])], x2, 0.5, F32, "ffn2_down")

    dx3, dx3b, d_final, loss = _loss_head(x3, small["final_norm"], tgt, "loss_head")

    dG2, dU2 = run(_ffn_bwd_act, dx3b, big["ffn2_down"], G2, U2, "ffn2_bwd_act")
    part["ffn2_down"] = run(_mm_tn, A2, dx3b, 0.5, _WIRE_DTYPE, "ffn2_dw_down")
    part["ffn2_gate"] = run(_mm_tn, dG2, h3, 1.0, _WIRE_DTYPE, "ffn2_dw_gate")
    part["ffn2_up"] = run(_mm_tn, dU2, h3, 1.0, _WIRE_DTYPE, "ffn2_dw_up")
    dh3 = run(_mm_nn, [(dG2, big["ffn2_gate"]), (dU2, big["ffn2_up"])], None, 1.0, F32, "ffn2_dh", rs1=ffn2_w)
    dx2, dx2b, d_g2 = _rms_bwd(dh3, x2, small["ffn2_norm"], dx3, "ffn2_rms_bwd")

    part["w_out"] = run(_mm_tn, yn, dx2b, 1.0, _WIRE_DTYPE, "dw_out")
    dyn = run(_mm_nt, dx2b, big["w_out"], F32, "out_proj_bwd")
    dya, dyb, d_ga, d_gb = _outnorm_bwd(dyn, ya, yb, small["swa_out_norm"], small["sb_out_norm"], "out_norm_bwd")
    dqa, dka, dva, d_sinks = _swa_bwd(proj, small["swa_sinks"], dya, B, S, "swa_bwd")
    dqb, dkb, dvb = run(_sb_bwd, proj, tot, dyb, B, S, "sb_bwd", rs2=ffn2_w)
    dproj = jnp.concatenate([dqa, dka, dva, dqb, dkb, dvb], axis=1)
    part["w_in"] = run(_mm_tn, dproj, h2, 1.0, _WIRE_DTYPE, "dw_in")
    dh2 = run(_mm_nn, [(dproj, big["w_in"])], None, 1.0, F32, "in_proj_bwd", rs1=("w_in", "w_out"))
    dx1, dx1b, d_gm = _rms_bwd(dh2, x1, small["mix_norm"], dx2, "mix_rms_bwd")

    dG1, dU1 = run(_ffn_bwd_act, dx1b, big["ffn1_down"], G1, U1, "ffn1_bwd_act", rs2=("w_in", "w_out"))
    part["ffn1_down"] = run(_mm_tn, A1, dx1b, 0.5, _WIRE_DTYPE, "ffn1_dw_down")
    part["ffn1_gate"] = run(_mm_tn, dG1, h1, 1.0, _WIRE_DTYPE, "ffn1_dw_gate", rs1=("ffn1_down",))
    part["ffn1_up"] = run(_mm_tn, dU1, h1, 1.0, _WIRE_DTYPE, "ffn1_dw_up", rs1=("ffn1_gate",), rs2=("ffn1_down",))
    dh1 = run(_mm_nn, [(dG1, big["ffn1_gate"])], None, 1.0, F32, "ffn1_dh_gate", rs1=("ffn1_up",), rs2=("ffn1_gate",))
    dh1 = run(_mm_nn, [(dU1, big["ffn1_up"])], dh1, 1.0, F32, "ffn1_dh_up", rs2=("ffn1_up",))
    gx, _, d_g1 = _rms_bwd(dh1, x, small["ffn1_norm"], dx1, "ffn1_rms_bwd")

    d_small = {"ffn1_norm": d_g1, "mix_norm": d_gm, "swa_sinks": d_sinks[:, :N_SWA_HEADS], "swa_out_norm": d_ga,
               "sb_out_norm": d_gb, "ffn2_norm": d_g2, "final_norm": d_final}
    return loss, gx, d_small


MESH = pl.DeviceIdType.MESH
BIG_NAMES = ("ffn1_gate", "ffn1_up", "ffn1_down", "w_in", "w_out", "ffn2_gate", "ffn2_up", "ffn2_down")
_COMM_PARAMS = pltpu.CompilerParams(has_side_effects=True)


def _place():
    x, y, c = lax.axis_index("x"), lax.axis_index("y"), lax.axis_index("c")
    other_chips = [(1 - x, y), (x, 1 - y), (1 - x, 1 - y)]
    return x, y, c, other_chips


def _padded_rows(rows):
    full = N_DEV * rows
    return -(-full // _F_TILE) * _F_TILE


def _ag_job(shards):
    nw = len(shards)
    D = shards[0].shape[1]
    rows_w = [s.shape[0] for s in shards]
    full_w = [_padded_rows(r) for r in rows_w]
    pad_w = [f - N_DEV * r for f, r in zip(full_w, rows_w)]
    max_pad = max(max(pad_w), 16)

    class Plan:
        def __init__(self, ins, outs, scratch):
            zbuf, send_sems, recv_sems, local_sems, zero_sems = scratch
            x, y, c, chips = _place()
            me, sibling = (x, y, c), (x, y, 1 - c)

            def rows(w, px, py, pc):
                start = pl.multiple_of((4 * px + 2 * py + pc) * rows_w[w], 16)
                return outs[w].at[pl.ds(start, rows_w[w]), :]

            def copy(w, k, block, to, src=None):
                return pltpu.make_async_remote_copy(
                    src_ref=rows(w, *block) if src is None else src, dst_ref=rows(w, *block),
                    send_sem=send_sems.at[w, k], recv_sem=recv_sems.at[w, k], device_id=to, device_id_type=MESH)

            self.zbuf = zbuf
            self.local = [pltpu.make_async_copy(zbuf.at[pl.ds(0, pad_w[w]), :],
                                                outs[w].at[pl.ds(N_DEV * rows_w[w], pad_w[w]), :], zero_sems.at[w])
                          for w in range(nw) if pad_w[w]]
            self.local += [pltpu.make_async_copy(ins[w], rows(w, *me), local_sems.at[w]) for w in range(nw)]
            self.first = [[copy(w, 0, me, sibling, src=ins[w])]
                          + [copy(w, 1 + j, me, (*chip, c), src=ins[w]) for j, chip in enumerate(chips)]
                          for w in range(nw)]
            self.arrive = [[copy(w, 1 + j, (*chip, c), me) for j, chip in enumerate(chips)] for w in range(nw)]
            self.passed = [[copy(w, 4 + j, (*chip, c), sibling) for j, chip in enumerate(chips)] for w in range(nw)]
            self.from_sibling = [[copy(w, 0, sibling, me)]
                                 + [copy(w, 4 + j, (*chip, 1 - c), me) for j, chip in enumerate(chips)]
                                 for w in range(nw)]

    def start(ins, outs, scratch):
        plan = Plan(ins, outs, scratch)
        plan.zbuf[...] = jnp.zeros_like(plan.zbuf)
        for cp in plan.local:
            cp.start()
        for w in range(nw):
            for cp in plan.first[w]:
                cp.start()

    def mid(ins, outs, scratch):
        plan = Plan(ins, outs, scratch)
        for w in range(nw):
            for arrived, onward in zip(plan.arrive[w], plan.passed[w]):
                arrived.wait_recv()
                onward.start()

    def finish(ins, outs, scratch):
        plan = Plan(ins, outs, scratch)
        for w in range(nw):
            for cp in plan.from_sibling[w]:
                cp.wait_recv()
        for w in range(nw):
            for cp in plan.first[w] + plan.passed[w]:
                cp.wait_send()
        for cp in plan.local:
            cp.wait()

    return _Job(
        ins=shards, out_shape=[jax.ShapeDtypeStruct((f, D), s.dtype) for f, s in zip(full_w, shards)],
        scratch=[pltpu.VMEM((max_pad, D), shards[0].dtype), pltpu.SemaphoreType.DMA((nw, 7)),
                 pltpu.SemaphoreType.DMA((nw, 7)), pltpu.SemaphoreType.DMA((nw,)), pltpu.SemaphoreType.DMA((nw,))],
        start=start, mid=mid, finish=finish)


def _rs1_job(partials, rows_w):
    nw = len(partials)
    D = partials[0].shape[1]

    def copies(ins, outs, scratch):
        send_sems, recv_sems = scratch
        x, y, c, _ = _place()
        out = []
        for w in range(nw):
            r = rows_w[w]
            for q in range(4):
                src = ins[w].at[pl.ds(pl.multiple_of((2 * q + 1 - c) * r, 16), r), :]
                out.append(pltpu.make_async_remote_copy(
                    src_ref=src, dst_ref=outs[w].at[pl.ds(q * r, r), :], send_sem=send_sems.at[w, q],
                    recv_sem=recv_sems.at[w, q], device_id=(x, y, 1 - c), device_id_type=MESH))
        return out

    def start(ins, outs, scratch):
        for cp in copies(ins, outs, scratch):
            cp.start()

    def finish(ins, outs, scratch):
        for cp in copies(ins, outs, scratch):
            cp.wait()

    return _Job(
        ins=partials, out_shape=[jax.ShapeDtypeStruct((4 * r, D), p.dtype) for r, p in zip(rows_w, partials)],
        scratch=[pltpu.SemaphoreType.DMA((nw, 4)), pltpu.SemaphoreType.DMA((nw, 4))], start=start, finish=finish)


def _pair_sum(partial, from_sibling, rows, core, name):
    D = partial.shape[1]

    def body(core_ref, p_ref, s_ref, o_ref):
        o_ref[...] = (p_ref[...].astype(F32) + s_ref[...].astype(F32)).astype(o_ref.dtype)

    grid_spec = pltpu.PrefetchScalarGridSpec(
        num_scalar_prefetch=1, grid=(4,),
        in_specs=[pl.BlockSpec((rows, D), lambda q, core_ref: (2 * q + core_ref[0], 0)),
                  pl.BlockSpec((rows, D), lambda q, core_ref: (q, 0))],
        out_specs=pl.BlockSpec((rows, D), lambda q, core_ref: (q, 0)))
    return pl.pallas_call(
        body, name=name, grid_spec=grid_spec, out_shape=jax.ShapeDtypeStruct((4 * rows, D), partial.dtype),
        compiler_params=_params("arbitrary"),
    )(core, partial, from_sibling)


def _rs2_job(chip_sums, rows_w):
    nw = len(chip_sums)

    def copies(ins, outs, scratch):
        send_sems, recv_sems, local_sems = scratch
        x, y, c, chips = _place()
        my_chip = 2 * x + y
        out = []
        for w in range(nw):
            r = rows_w[w]
            mine = pl.ds(pl.multiple_of(my_chip * r, 16), r)
            out.append(pltpu.make_async_copy(ins[w].at[mine, :], outs[w].at[mine, :], local_sems.at[w]))
            for j, (qx, qy) in enumerate(chips):
                src = ins[w].at[pl.ds(pl.multiple_of((2 * qx + qy) * r, 16), r), :]
                out.append(pltpu.make_async_remote_copy(
                    src_ref=src, dst_ref=outs[w].at[mine, :], send_sem=send_sems.at[w, j],
                    recv_sem=recv_sems.at[w, j], device_id=(qx, qy, c), device_id_type=MESH))
        return out

    def start(ins, outs, scratch):
        for cp in copies(ins, outs, scratch):
            cp.start()

    def finish(ins, outs, scratch):
        for cp in copies(ins, outs, scratch):
            cp.wait()

    return _Job(
        ins=chip_sums, out_shape=[jax.ShapeDtypeStruct(s.shape, s.dtype) for s in chip_sums],
        scratch=[pltpu.SemaphoreType.DMA((nw, 3)), pltpu.SemaphoreType.DMA((nw, 3)), pltpu.SemaphoreType.DMA((nw,))],
        start=start, finish=finish)


class _Comm:
    def __init__(self, shards):
        self.shards = shards
        self.rows = {n: s.shape[0] for n, s in shards.items()}
        self.core = lax.axis_index("c").astype(jnp.int32).reshape(1)
        self.big, self.partial, self.chip_sums, self.slots = {}, {}, {}, {}

    def run(self, fn, *args, ag=(), rs1=(), rs2=()):
        jobs = []
        if ag:
            jobs.append(_ag_job([self.shards[n] for n in ag]))
        if rs1:
            jobs.append(_rs1_job([self.partial[n] for n in rs1], [self.rows[n] for n in rs1]))
        if rs2:
            jobs.append(_rs2_job([self.chip_sums[n] for n in rs2], [self.rows[n] for n in rs2]))
        out, job_res = fn(*args, jobs=jobs)
        job_res = iter(job_res)
        if ag:
            self.big.update(zip(ag, next(job_res)))
        if rs1:
            for n, got in zip(rs1, next(job_res)):
                self.chip_sums[n] = _pair_sum(self.partial[n], got, self.rows[n], self.core, "pair_sum_" + n)
        if rs2:
            self.slots.update(zip(rs2, next(job_res)))
        return out


SMALL_ROWS = 88


def _small_allreduce(vec):
    def body(v_ref, o_ref, gather, send_sems, recv_sems):
        x, y, c, _ = _place()
        my_id = 4 * x + 2 * y + c
        gather[my_id] = v_ref[...]
        copies = []
        for r in range(1, N_DEV):
            peer = (x ^ (r >> 2), y ^ ((r >> 1) & 1), c ^ (r & 1))
            cp = pltpu.make_async_remote_copy(src_ref=v_ref, dst_ref=gather.at[my_id], send_sem=send_sems.at[r - 1],
                                              recv_sem=recv_sems.at[r - 1], device_id=peer, device_id_type=MESH)
            cp.start()
            copies.append(cp)
        for cp in copies:
            cp.wait()
        acc = gather[0]
        for d in range(1, N_DEV):
            acc = acc + gather[d]
        o_ref[...] = acc

    vm = pl.BlockSpec(memory_space=pltpu.VMEM)
    return pl.pallas_call(
        body, name="small_allreduce", out_shape=jax.ShapeDtypeStruct(vec.shape, F32),
        in_specs=[vm], out_specs=vm,
        scratch_shapes=[pltpu.VMEM((N_DEV,) + vec.shape, F32), pltpu.SemaphoreType.DMA((N_DEV - 1,)),
                        pltpu.SemaphoreType.DMA((N_DEV - 1,))],
        compiler_params=_COMM_PARAMS,
    )(vec)


def _adamw_update(w, g, m, v):
    nm = ADAM_B1 * m + (1.0 - ADAM_B1) * g
    nv = ADAM_B2 * v + (1.0 - ADAM_B2) * jnp.square(g)
    m_hat = nm / (1.0 - ADAM_B1 ** ADAM_STEP)
    v_hat = nv / (1.0 - ADAM_B2 ** ADAM_STEP)
    return -ADAM_LR * (m_hat / (jnp.sqrt(v_hat) + ADAM_EPS) + ADAM_WD * w), nm, nv


def _adamw(w, g, m, v, name):
    R, C = w.shape
    tr = _tile(R, 256, 8)

    def body(w_ref, g_ref, m_ref, v_ref, d_ref, nm_ref, nv_ref):
        d_ref[...], nm_ref[...], nv_ref[...] = _adamw_update(w_ref[...], g_ref[...], m_ref[...], v_ref[...])

    spec = pl.BlockSpec((tr, C), lambda i: (i, 0))
    out = jax.ShapeDtypeStruct((R, C), F32)
    return pl.pallas_call(
        body, name=name, grid=(R // tr,), out_shape=(out, out, out),
        in_specs=[spec] * 4, out_specs=(spec, spec, spec),
        compiler_params=_params("parallel"),
    )(w, g, m, v)


def _adamw_slots(w, slots, m, v, name):
    R, C = w.shape
    tc = _tile(C, 512, LANES)

    def body(w_ref, s_ref, m_ref, v_ref, g_ref, d_ref, nm_ref, nv_ref):
        g = s_ref[0].astype(F32)
        for q in range(1, 4):
            g = g + s_ref[q].astype(F32)
        g_ref[...] = g
        d_ref[...], nm_ref[...], nv_ref[...] = _adamw_update(w_ref[...], g, m_ref[...], v_ref[...])

    spec = pl.BlockSpec((R, tc), lambda j: (0, j))
    out = jax.ShapeDtypeStruct((R, C), F32)
    return pl.pallas_call(
        body, name=name, grid=(C // tc,), out_shape=(out, out, out, out),
        in_specs=[spec, pl.BlockSpec((4, R, tc), lambda j: (0, 0, j)), spec, spec], out_specs=(spec, spec, spec, spec),
        compiler_params=_params("parallel"),
    )(w, slots, m, v)


WEIGHT_NAMES = ("ffn1_norm", "ffn1_w_gate", "ffn1_w_up", "ffn1_w_down", "mix_norm", "w_in", "swa_sinks",
                "swa_out_norm", "sb_out_norm", "w_out", "ffn2_norm", "ffn2_w_gate", "ffn2_w_up", "ffn2_w_down",
                "final_norm")
SMALL_NAMES = ("ffn1_norm", "mix_norm", "swa_sinks", "swa_out_norm", "sb_out_norm", "ffn2_norm", "final_norm")
BIG_ARGS = {"ffn1_gate": ("ffn1_w_gate", True), "ffn1_up": ("ffn1_w_up", True), "ffn1_down": ("ffn1_w_down", False),
            "w_in": ("w_in", True), "w_out": ("w_out", False), "ffn2_gate": ("ffn2_w_gate", True),
            "ffn2_up": ("ffn2_w_up", True), "ffn2_down": ("ffn2_w_down", False)}


def _pack_small(parts):
    padded = [jnp.pad(p.reshape(1, -1), ((0, 0), (0, -p.size % LANES))) for p in parts]
    flat = jnp.concatenate(padded, axis=1)
    flat = jnp.pad(flat, ((0, 0), (0, SMALL_ROWS * LANES - flat.shape[1])))
    return flat.reshape(SMALL_ROWS, LANES)


def _unpack_small(block, shapes):
    flat = block.reshape(-1)
    out, off = [], 0
    for shp in shapes:
        n = 1
        for s in shp:
            n *= s
        out.append(flat[off:off + n].reshape(shp))
        off += n + (-n % LANES)
    return out


def kernel(x, ffn1_norm, ffn1_w_gate, ffn1_w_up, ffn1_w_down, mix_norm, w_in, swa_sinks, swa_out_norm, sb_out_norm, w_out, ffn2_norm, ffn2_w_gate, ffn2_w_up, ffn2_w_down, final_norm, loss_target, m_ffn1_norm, m_ffn1_w_gate, m_ffn1_w_up, m_ffn1_w_down, m_mix_norm, m_w_in, m_swa_sinks, m_swa_out_norm, m_sb_out_norm, m_w_out, m_ffn2_norm, m_ffn2_w_gate, m_ffn2_w_up, m_ffn2_w_down, m_final_norm, v_ffn1_norm, v_ffn1_w_gate, v_ffn1_w_up, v_ffn1_w_down, v_mix_norm, v_w_in, v_swa_sinks, v_swa_out_norm, v_sb_out_norm, v_w_out, v_ffn2_norm, v_ffn2_w_gate, v_ffn2_w_up, v_ffn2_w_down, v_final_norm):
    args = dict(locals())
    B, S, D = x.shape
    T = B * S
    weights = {n: args[n] for n in WEIGHT_NAMES}
    mom_m = {n: args["m_" + n] for n in WEIGHT_NAMES}
    mom_v = {n: args["v_" + n] for n in WEIGHT_NAMES}

    shards = {}
    for name in BIG_NAMES:
        arg, transposed = BIG_ARGS[name]
        w2 = weights[arg][0]
        shards[name] = (w2.T if transposed else w2).astype(_WIRE_DTYPE)
    comm = _Comm(shards)
    small = {n: weights[n].reshape(1, -1) for n in SMALL_NAMES}

    loss, gx, d_small = _layer_step(x.reshape(T, D), loss_target.reshape(T, D), B, S, small, comm)

    small_shapes = [(1, 1)] + [d_small[n].shape for n in SMALL_NAMES]
    reduced = _small_allreduce(_pack_small([loss[:, :1]] + [d_small[n] for n in SMALL_NAMES]))
    red = _unpack_small(reduced, small_shapes)
    loss_out = red[0].reshape(())
    g_small = dict(zip(SMALL_NAMES, red[1:]))

    grads, deltas, new_m, new_v = {}, {}, {}, {}
    for name in BIG_NAMES:
        arg, transposed = BIG_ARGS[name]
        to_rows = (lambda t: t[0].T) if transposed else (lambda t: t[0])
        back = (lambda t: t.T[None]) if transposed else (lambda t: t[None])
        slots = comm.slots[name].reshape(4, comm.rows[name], D)
        res = _adamw_slots(to_rows(weights[arg]), slots, to_rows(mom_m[arg]), to_rows(mom_v[arg]), "adamw_" + name)
        grads[arg], deltas[arg], new_m[arg], new_v[arg] = [back(t) for t in res]
    shapes1 = [(1, weights[n].size) for n in SMALL_NAMES]
    packed = [_pack_small([t[n].reshape(1, -1) for n in SMALL_NAMES]) for t in (weights, g_small, mom_m, mom_v)]
    upd = _adamw(*packed, "adamw_small")
    for tgt_dict, block in zip((deltas, new_m, new_v), upd):
        for n, val in zip(SMALL_NAMES, _unpack_small(block, shapes1)):
            tgt_dict[n] = val.reshape(weights[n].shape)
    for n in SMALL_NAMES:
        grads[n] = g_small[n].reshape(weights[n].shape)

    return (loss_out, gx.reshape(B, S, D), *[grads[n] for n in WEIGHT_NAMES], *[deltas[n] for n in WEIGHT_NAMES],
            *[new_m[n] for n in WEIGHT_NAMES], *[new_v[n] for n in WEIGHT_NAMES])
```

```python
import functools

import jax
import jax.numpy as jnp
from jax import lax
from jax.experimental import pallas as pl
from jax.experimental.pallas import tpu as pltpu

F32 = jnp.float32
_MXU_DTYPE = jnp.bfloat16
_WIRE_DTYPE = jnp.bfloat16

EPS = 1e-6
HEAD_DIM = 64
N_SWA_HEADS = 16
N_SWA_KV = 4
N_SB_HEADS = 16
WINDOW = 128
SWA_Q = N_SWA_HEADS * HEAD_DIM
SWA_KV = N_SWA_KV * HEAD_DIM
SB_W = N_SB_HEADS * HEAD_DIM
IN_W = SWA_Q + 2 * SWA_KV + 3 * SB_W
LANES = 128
ATT_SCALE = HEAD_DIM ** -0.5

ADAM_LR = 0.001
ADAM_B1 = 0.9
ADAM_B2 = 0.999
ADAM_EPS = 1e-08
ADAM_WD = 0.01
ADAM_STEP = 10

N_DEV = 8
_VMEM_LIMIT_BYTES = 56 * 1024 * 1024
_F_TILE = 512


def _params(*semantics):
    return pltpu.CompilerParams(dimension_semantics=semantics, vmem_limit_bytes=_VMEM_LIMIT_BYTES)


def _tile(n, pref, align):
    t = min(n, pref)
    t -= t % align
    while t >= align:
        if n % t == 0:
            return t
        t -= align
    return n


def _dot(a, b):
    return lax.dot_general(a, b, (((1,), (0,)), ((), ())), preferred_element_type=F32)


def _dot_nt(a, b):
    return lax.dot_general(a, b, (((1,), (1,)), ((), ())), preferred_element_type=F32)


def _dot_tn(a, b):
    return lax.dot_general(a, b, (((0,), (0,)), ((), ())), preferred_element_type=F32)


class _Job:
    def __init__(self, ins, out_shape, scratch, start, finish, mid=None):
        self.ins, self.out_shape, self.scratch = list(ins), list(out_shape), list(scratch)
        self.start, self.mid, self.finish = start, mid, finish


_JOB_MID_FRACTION = 0.8


def _call(body, *, name, grid, in_specs, out_specs, out_shape, args, semantics, scratch_shapes=(), jobs=()):
    single = not isinstance(out_shape, (tuple, list))
    if not jobs:
        res = pl.pallas_call(body, name=name, grid=grid, in_specs=list(in_specs), out_specs=out_specs,
                             out_shape=out_shape, scratch_shapes=list(scratch_shapes),
                             compiler_params=_params(*semantics))(*args)
        return res, []
    base_out = [out_shape] if single else list(out_shape)
    base_out_specs = [out_specs] if single else list(out_specs)
    n_in, n_out, n_scr = len(args), len(base_out), len(scratch_shapes)
    any_spec = pl.BlockSpec(memory_space=pl.ANY)
    total = 1
    for g in grid:
        total *= g
    mid_step = min(total - 1, int(total * _JOB_MID_FRACTION))

    def wrapped(*refs):
        pos = n_in
        job_ins = []
        for job in jobs:
            job_ins.append(refs[pos:pos + len(job.ins)])
            pos += len(job.ins)
        outs = refs[pos:pos + n_out]
        pos += n_out
        job_outs = []
        for job in jobs:
            job_outs.append(refs[pos:pos + len(job.out_shape)])
            pos += len(job.out_shape)
        scr = refs[pos:pos + n_scr]
        pos += n_scr
        job_scr = []
        for job in jobs:
            job_scr.append(refs[pos:pos + len(job.scratch)])
            pos += len(job.scratch)
        step = pl.program_id(0)
        for d in range(1, len(grid)):
            step = step * grid[d] + pl.program_id(d)

        @pl.when(step == 0)
        def _():
            for job, ji, jo, js in zip(jobs, job_ins, job_outs, job_scr):
                job.start(ji, jo, js)

        @pl.when(step == mid_step)
        def _():
            for job, ji, jo, js in zip(jobs, job_ins, job_outs, job_scr):
                if job.mid is not None:
                    job.mid(ji, jo, js)

        body(*refs[:n_in], *outs, *scr)

        @pl.when(step == total - 1)
        def _():
            for job, ji, jo, js in zip(jobs, job_ins, job_outs, job_scr):
                job.finish(ji, jo, js)

    all_args = list(args) + [a for job in jobs for a in job.ins]
    all_in_specs = list(in_specs) + [any_spec for job in jobs for _ in job.ins]
    all_out_shape = base_out + [s for job in jobs for s in job.out_shape]
    all_out_specs = base_out_specs + [any_spec for job in jobs for _ in job.out_shape]
    all_scratch = list(scratch_shapes) + [s for job in jobs for s in job.scratch]
    res = pl.pallas_call(
        wrapped, name=name, grid=grid, in_specs=all_in_specs, out_specs=tuple(all_out_specs),
        out_shape=tuple(all_out_shape), scratch_shapes=all_scratch,
        compiler_params=pltpu.CompilerParams(dimension_semantics=("arbitrary",) * len(grid),
                                             vmem_limit_bytes=_VMEM_LIMIT_BYTES, has_side_effects=True),
    )(*all_args)
    base = res[0] if single else tuple(res[:n_out])
    job_res, pos = [], n_out
    for job in jobs:
        job_res.append(tuple(res[pos:pos + len(job.out_shape)]))
        pos += len(job.out_shape)
    return base, job_res


def _rms_fwd(x, g, name, jobs=()):
    T, D = x.shape
    tm = _tile(T, 512, 16)

    def body(x_ref, g_ref, o_ref):
        xv = x_ref[...]
        r = lax.rsqrt(jnp.mean(xv * xv, axis=-1, keepdims=True) + EPS)
        o_ref[...] = (xv * r * g_ref[...]).astype(o_ref.dtype)

    return _call(
        body, name=name, grid=(T // tm,),
        out_shape=jax.ShapeDtypeStruct((T, D), _MXU_DTYPE),
        in_specs=[pl.BlockSpec((tm, D), lambda i: (i, 0)), pl.BlockSpec((1, D), lambda i: (0, 0))],
        out_specs=pl.BlockSpec((tm, D), lambda i: (i, 0)), args=(x, g), semantics=("parallel",), jobs=jobs)


def _rms_bwd_rows(dh, xv, g):
    r = lax.rsqrt(jnp.mean(xv * xv, axis=-1, keepdims=True) + EPS)
    xhat = xv * r
    u = dh * g
    dx = r * (u - xhat * jnp.mean(u * xhat, axis=-1, keepdims=True))
    return dx, dh * xhat


def _rms_bwd(dh, x, g, dres, name):
    T, D = x.shape
    tm = _tile(T, 256, 16)

    def body(dh_ref, x_ref, g_ref, dres_ref, dx_ref, dxb_ref, dg_ref):
        @pl.when(pl.program_id(0) == 0)
        def _():
            dg_ref[...] = jnp.zeros_like(dg_ref)

        dx, dgr = _rms_bwd_rows(dh_ref[...], x_ref[...], g_ref[...])
        dx = dres_ref[...] + dx
        dx_ref[...] = dx
        dxb_ref[...] = dx.astype(dxb_ref.dtype)
        dg_ref[...] += jnp.sum(dgr, axis=0, keepdims=True)

    row = pl.BlockSpec((tm, D), lambda i: (i, 0))
    vec = pl.BlockSpec((1, D), lambda i: (0, 0))
    return pl.pallas_call(
        body, name=name, grid=(T // tm,),
        out_shape=(jax.ShapeDtypeStruct((T, D), F32), jax.ShapeDtypeStruct((T, D), _MXU_DTYPE),
                   jax.ShapeDtypeStruct((1, D), F32)),
        in_specs=[row, row, vec, row], out_specs=(row, row, vec),
        compiler_params=_params("arbitrary"),
    )(dh, x, g, dres)


def _loss_head(x, g, tgt, name):
    T, D = x.shape
    tm = _tile(T, 256, 16)

    def body(x_ref, g_ref, t_ref, dx_ref, dxb_ref, dg_ref, loss_ref):
        @pl.when(pl.program_id(0) == 0)
        def _():
            dg_ref[...] = jnp.zeros_like(dg_ref)
            loss_ref[...] = jnp.zeros_like(loss_ref)

        xv = x_ref[...]
        gv = g_ref[...]
        r = lax.rsqrt(jnp.mean(xv * xv, axis=-1, keepdims=True) + EPS)
        xhat = xv * r
        diff = xhat * gv - t_ref[...]
        tok = jnp.mean(diff * diff, axis=-1, keepdims=True)
        loss_ref[...] += 0.5 * jnp.sum(tok, axis=0, keepdims=True)
        dy = diff / D
        u = dy * gv
        dx = r * (u - xhat * jnp.mean(u * xhat, axis=-1, keepdims=True))
        dx_ref[...] = dx
        dxb_ref[...] = dx.astype(dxb_ref.dtype)
        dg_ref[...] += jnp.sum(dy * xhat, axis=0, keepdims=True)

    row = pl.BlockSpec((tm, D), lambda i: (i, 0))
    vec = pl.BlockSpec((1, D), lambda i: (0, 0))
    return pl.pallas_call(
        body, name=name, grid=(T // tm,),
        out_shape=(jax.ShapeDtypeStruct((T, D), F32), jax.ShapeDtypeStruct((T, D), _MXU_DTYPE),
                   jax.ShapeDtypeStruct((1, D), F32), jax.ShapeDtypeStruct((1, LANES), F32)),
        in_specs=[row, vec, row],
        out_specs=(row, row, vec, pl.BlockSpec((1, LANES), lambda i: (0, 0))),
        compiler_params=_params("arbitrary"),
    )(x, g, tgt)


def _outnorm_fwd(ya, yb, ga, gb, name):
    T, W = ya.shape
    tm = _tile(T, 512, 16)

    def body(ya_ref, yb_ref, ga_ref, gb_ref, o_ref):
        for k, (y_ref, g_ref) in enumerate(((ya_ref, ga_ref), (yb_ref, gb_ref))):
            yv = y_ref[...]
            r = lax.rsqrt(jnp.mean(yv * yv, axis=-1, keepdims=True) + EPS)
            o_ref[:, k * W:(k + 1) * W] = (yv * r * g_ref[...]).astype(o_ref.dtype)

    row = pl.BlockSpec((tm, W), lambda i: (i, 0))
    vec = pl.BlockSpec((1, W), lambda i: (0, 0))
    return pl.pallas_call(
        body, name=name, grid=(T // tm,),
        out_shape=jax.ShapeDtypeStruct((T, 2 * W), _MXU_DTYPE),
        in_specs=[row, row, vec, vec], out_specs=pl.BlockSpec((tm, 2 * W), lambda i: (i, 0)),
        compiler_params=_params("parallel"),
    )(ya, yb, ga, gb)


def _outnorm_bwd(dyn, ya, yb, ga, gb, name):
    T, W = ya.shape
    tm = _tile(T, 256, 16)

    def body(d_ref, ya_ref, yb_ref, ga_ref, gb_ref, dya_ref, dyb_ref, dga_ref, dgb_ref):
        @pl.when(pl.program_id(0) == 0)
        def _():
            dga_ref[...] = jnp.zeros_like(dga_ref)
            dgb_ref[...] = jnp.zeros_like(dgb_ref)

        for k, (y_ref, g_ref, dy_ref, dg_ref) in enumerate(
                ((ya_ref, ga_ref, dya_ref, dga_ref), (yb_ref, gb_ref, dyb_ref, dgb_ref))):
            dy, dgr = _rms_bwd_rows(d_ref[:, k * W:(k + 1) * W], y_ref[...], g_ref[...])
            dy_ref[...] = dy.astype(dy_ref.dtype)
            dg_ref[...] += jnp.sum(dgr, axis=0, keepdims=True)

    row = pl.BlockSpec((tm, W), lambda i: (i, 0))
    vec = pl.BlockSpec((1, W), lambda i: (0, 0))
    return pl.pallas_call(
        body, name=name, grid=(T // tm,),
        out_shape=(jax.ShapeDtypeStruct((T, W), _MXU_DTYPE), jax.ShapeDtypeStruct((T, W), _MXU_DTYPE),
                   jax.ShapeDtypeStruct((1, W), F32), jax.ShapeDtypeStruct((1, W), F32)),
        in_specs=[pl.BlockSpec((tm, 2 * W), lambda i: (i, 0)), row, row, vec, vec],
        out_specs=(row, row, vec, vec),
        compiler_params=_params("arbitrary"),
    )(dyn, ya, yb, ga, gb)


_STRIP_ROWS = 256


def _strips(rows):
    step = min(rows, _STRIP_ROWS)
    return [slice(r, r + step) for r in range(0, rows, step)]


def _ffn_gu(h, wg_t, wu_t, name, jobs=()):
    T, D = h.shape
    Fp = wg_t.shape[0]
    tm = _tile(T, 1024, 16)
    tn = _tile(Fp, _F_TILE, LANES)

    def body(h_ref, wg_ref, wu_ref, g_ref, u_ref, a_ref):
        for rows in _strips(tm):
            hv = h_ref[rows, :]
            g = _dot_nt(hv, wg_ref[...])
            u = _dot_nt(hv, wu_ref[...])
            g_ref[rows, :] = g.astype(g_ref.dtype)
            u_ref[rows, :] = u.astype(u_ref.dtype)
            a_ref[rows, :] = (g * jax.nn.sigmoid(g) * u).astype(a_ref.dtype)

    act = pl.BlockSpec((tm, tn), lambda n, m: (m, n))
    wsp = pl.BlockSpec((tn, D), lambda n, m: (n, 0))
    out = jax.ShapeDtypeStruct((T, Fp), _MXU_DTYPE)
    return _call(
        body, name=name, grid=(Fp // tn, T // tm), out_shape=(out, out, out),
        in_specs=[pl.BlockSpec((tm, D), lambda n, m: (m, 0)), wsp, wsp],
        out_specs=(act, act, act), args=(h, wg_t, wu_t), semantics=("parallel", "parallel"), jobs=jobs)


def _ffn_bwd_act(dxb, wd, G, U, name, jobs=()):
    T, D = dxb.shape
    Fp = wd.shape[0]
    tm = _tile(T, 1024, 16)
    tn = _tile(Fp, _F_TILE, LANES)

    def body(e_ref, wd_ref, g_ref, u_ref, dg_ref, du_ref):
        for rows in _strips(tm):
            da = 0.5 * _dot_nt(e_ref[rows, :], wd_ref[...])
            g = g_ref[rows, :].astype(F32)
            u = u_ref[rows, :].astype(F32)
            s = jax.nn.sigmoid(g)
            du_ref[rows, :] = (da * (g * s)).astype(du_ref.dtype)
            dg_ref[rows, :] = (da * u * (s * (1.0 + g * (1.0 - s)))).astype(dg_ref.dtype)

    act = pl.BlockSpec((tm, tn), lambda m, n: (m, n))
    out = jax.ShapeDtypeStruct((T, Fp), _MXU_DTYPE)
    return _call(
        body, name=name, grid=(T // tm, Fp // tn), out_shape=(out, out),
        in_specs=[pl.BlockSpec((tm, D), lambda m, n: (m, 0)), pl.BlockSpec((tn, D), lambda m, n: (n, 0)),
                  act, act],
        out_specs=(act, act), args=(dxb, wd, G, U), semantics=("parallel", "parallel"), jobs=jobs)


def _ffn_gate(h, wg_t, U, name, jobs=()):
    T, D = h.shape
    Fp = wg_t.shape[0]
    tm = _tile(T, 1024, 16)
    tn = _tile(Fp, _F_TILE, LANES)

    def body(h_ref, wg_ref, u_ref, g_ref, a_ref):
        for rows in _strips(tm):
            g = _dot_nt(h_ref[rows, :], wg_ref[...])
            g_ref[rows, :] = g.astype(g_ref.dtype)
            a_ref[rows, :] = (g * jax.nn.sigmoid(g) * u_ref[rows, :].astype(F32)).astype(a_ref.dtype)

    act = pl.BlockSpec((tm, tn), lambda m, n: (m, n))
    out = jax.ShapeDtypeStruct((T, Fp), _MXU_DTYPE)
    return _call(
        body, name=name, grid=(T // tm, Fp // tn), out_shape=(out, out),
        in_specs=[pl.BlockSpec((tm, D), lambda m, n: (m, 0)), pl.BlockSpec((tn, D), lambda m, n: (n, 0)), act],
        out_specs=(act, act), args=(h, wg_t, U), semantics=("parallel", "parallel"), jobs=jobs)


def _mm_nt(a, b, out_dtype, name, jobs=()):
    M, K = a.shape
    N = b.shape[0]
    tm = _tile(M, 1024, 16)
    tn = _tile(N, 512, LANES)

    def body(a_ref, b_ref, o_ref):
        o_ref[...] = _dot_nt(a_ref[...], b_ref[...]).astype(o_ref.dtype)

    return _call(
        body, name=name, grid=(M // tm, N // tn), out_shape=jax.ShapeDtypeStruct((M, N), out_dtype),
        in_specs=[pl.BlockSpec((tm, K), lambda m, n: (m, 0)), pl.BlockSpec((tn, K), lambda m, n: (n, 0))],
        out_specs=pl.BlockSpec((tm, tn), lambda m, n: (m, n)), args=(a, b),
        semantics=("parallel", "parallel"), jobs=jobs)


_MM_OPERAND_BYTES = 26 * 1024 * 1024


def _k_tile(K, bytes_per_k, align):
    best = align
    for t in range(align, K + 1, align):
        if K % t == 0 and 2 * t * bytes_per_k <= _MM_OPERAND_BYTES:
            best = t
    return best


def _mm_nn(pairs, res, alpha, out_dtype, name, jobs=()):
    M, K = pairs[0][0].shape
    N = pairs[0][1].shape[1]
    n_pairs = len(pairs)
    tm = _tile(M, 1024, 16)
    tn = _tile(N, 1024, LANES)
    tk = _k_tile(K, n_pairs * (tm + tn) * pairs[0][0].dtype.itemsize, LANES)
    nk = K // tk

    def body(*refs):
        ab = refs[:2 * n_pairs]
        res_ref = refs[2 * n_pairs] if res is not None else None
        o_ref = refs[2 * n_pairs + (res is not None)]

        def finish(acc):
            out = alpha * acc
            if res_ref is not None:
                out = res_ref[...] + out
            o_ref[...] = out.astype(o_ref.dtype)

        part = _dot(ab[0][...], ab[1][...])
        for i in range(1, n_pairs):
            part = part + _dot(ab[2 * i][...], ab[2 * i + 1][...])
        if nk == 1:
            finish(part)
        else:
            acc_ref = refs[-1]
            k = pl.program_id(2)

            @pl.when(k == 0)
            def _():
                acc_ref[...] = part

            @pl.when(k > 0)
            def _():
                acc_ref[...] += part

            @pl.when(k == nk - 1)
            def _():
                finish(acc_ref[...])

    in_specs, args = [], []
    for a, b in pairs:
        in_specs += [pl.BlockSpec((tm, tk), lambda m, n, k: (m, k)), pl.BlockSpec((tk, tn), lambda m, n, k: (k, n))]
        args += [a, b]
    if res is not None:
        in_specs.append(pl.BlockSpec((tm, tn), lambda m, n, k: (m, n)))
        args.append(res)
    return _call(
        body, name=name, grid=(M // tm, N // tn, nk), out_shape=jax.ShapeDtypeStruct((M, N), out_dtype),
        in_specs=in_specs, out_specs=pl.BlockSpec((tm, tn), lambda m, n, k: (m, n)),
        scratch_shapes=[pltpu.VMEM((tm, tn), F32)] if nk > 1 else [], args=args,
        semantics=("parallel", "parallel", "arbitrary"), jobs=jobs)


def _mm_tn(a, b, alpha, out_dtype, name, jobs=()):
    K, M = a.shape
    N = b.shape[1]
    tm = _tile(M, 512, LANES)
    tn = _tile(N, 1024, LANES)

    def body(a_ref, b_ref, o_ref):
        o_ref[...] = (alpha * _dot_tn(a_ref[...], b_ref[...])).astype(o_ref.dtype)

    return _call(
        body, name=name, grid=(N // tn, M // tm), out_shape=jax.ShapeDtypeStruct((M, N), out_dtype),
        in_specs=[pl.BlockSpec((K, tm), lambda n, m: (0, m)), pl.BlockSpec((K, tn), lambda n, m: (0, n))],
        out_specs=pl.BlockSpec((tm, tn), lambda n, m: (m, n)), args=(a, b),
        semantics=("parallel", "parallel"), jobs=jobs)


def _half_masks():
    lane = lax.broadcasted_iota(jnp.int32, (1, LANES), 1)
    return (lane < HEAD_DIM, lane >= HEAD_DIM)


def _swap_halves(v):
    return pltpu.roll(v.astype(F32), HEAD_DIM, 1).astype(v.dtype)


def _swa_geometry(n):
    qi = lax.broadcasted_iota(jnp.int32, (WINDOW, 2 * WINDOW), 0)
    kp = lax.broadcasted_iota(jnp.int32, (WINDOW, 2 * WINDOW), 1)
    dist = (WINDOW + qi) - kp
    valid = (dist >= 0) & (dist < WINDOW) & ((n > 0) | (kp >= WINDOW))
    return dist.astype(F32), valid


def _swa_slope(h):
    return 2.0 ** (-8.0 * (h + 1) / N_SWA_HEADS)


def _swa_softmax(qk, sink, slope, distf, valid):
    s = qk * ATT_SCALE - slope * distf
    s = jnp.where(valid, s, -1e30)
    m = jnp.maximum(jnp.max(s, axis=1, keepdims=True), sink)
    p = jnp.exp(s - m)
    e_sink = jnp.exp(sink - m)
    den = jnp.sum(p, axis=1, keepdims=True) + e_sink
    return p / den, e_sink / den


def _swa_group_heads(g):
    return [(2 * pp + a, pp, a) for pp in (2 * g, 2 * g + 1) for a in range(2)]


def _swa_specs(B, S):
    nb = S // WINDOW
    kcol = SWA_Q // SWA_KV
    cur = lambda b, n: (b * nb + n, kcol)
    prev = lambda b, n: (b * nb + jnp.maximum(n - 1, 0), kcol)
    curv = lambda b, n: (b * nb + n, kcol + 1)
    prevv = lambda b, n: (b * nb + jnp.maximum(n - 1, 0), kcol + 1)
    q_spec = pl.BlockSpec((WINDOW, SWA_Q), lambda b, n: (b * nb + n, 0))
    kv = [pl.BlockSpec((WINDOW, SWA_KV), f) for f in (prev, cur, prevv, curv)]
    sink_spec = pl.BlockSpec(memory_space=pltpu.SMEM)
    return nb, q_spec, kv, sink_spec


def _swa_kv_views(kp_ref, kc_ref, vp_ref, vc_ref, g):
    hm = _half_masks()
    c0 = (g // 2) * LANES
    k_all = jnp.concatenate([kp_ref[:, c0:c0 + LANES], kc_ref[:, c0:c0 + LANES]], axis=0)
    v_all = jnp.concatenate([vp_ref[:, c0:c0 + LANES], vc_ref[:, c0:c0 + LANES]], axis=0)
    b = g % 2
    ks, vs = [None, None], [None, None]
    ks[b], vs[b] = k_all, v_all
    ks[1 - b], vs[1 - b] = _swap_halves(k_all), _swap_halves(v_all)
    ks = [jnp.where(hm[a], ks[a], 0) for a in range(2)]
    vs = [jnp.where(hm[a], vs[a], 0) for a in range(2)]
    return ks, vs


def _swa_fwd(proj, sinks, B, S, name, jobs=()):
    T = B * S
    nb, q_spec, kv_specs, sink_spec = _swa_specs(B, S)

    def body(sink_ref, q_ref, kp_ref, kc_ref, vp_ref, vc_ref, y_ref):
        hm = _half_masks()
        distf, valid = _swa_geometry(pl.program_id(1))
        for g in range(N_SWA_KV):
            ks, vs = _swa_kv_views(kp_ref, kc_ref, vp_ref, vc_ref, g)
            heads = _swa_group_heads(g)
            qk = [_dot_nt(jnp.where(hm[a], q_ref[:, pp * LANES:(pp + 1) * LANES], 0), ks[a]) for _, pp, a in heads]
            p = [_swa_softmax(qk[i], sink_ref[0, h], _swa_slope(h), distf, valid)[0] for i, (h, _, _) in enumerate(heads)]
            o = [_dot(p[i].astype(_MXU_DTYPE), vs[a]) for i, (_, _, a) in enumerate(heads)]
            for j, pp in enumerate((2 * g, 2 * g + 1)):
                y_ref[:, pp * LANES:(pp + 1) * LANES] = o[2 * j] + o[2 * j + 1]

    return _call(
        body, name=name, grid=(B, nb), out_shape=jax.ShapeDtypeStruct((T, SWA_Q), F32),
        in_specs=[sink_spec, q_spec] + kv_specs,
        out_specs=pl.BlockSpec((WINDOW, SWA_Q), lambda b, n: (b * nb + n, 0)),
        args=(sinks, proj, proj, proj, proj, proj), semantics=("parallel", "parallel"), jobs=jobs)


def _swa_bwd(proj, sinks, dya, B, S, name):
    T = B * S
    nb, q_spec, kv_specs, sink_spec = _swa_specs(B, S)

    def body(sink_ref, q_ref, kp_ref, kc_ref, vp_ref, vc_ref, do_ref,
             dq_ref, dk_ref, dv_ref, dsink_ref, dk_acc, dv_acc):
        b_id, n = pl.program_id(0), pl.program_id(1)
        hm = _half_masks()
        lane = lax.broadcasted_iota(jnp.int32, (1, LANES), 1)

        @pl.when((b_id == 0) & (n == 0))
        def _():
            dsink_ref[...] = jnp.zeros_like(dsink_ref)

        @pl.when(n == 0)
        def _():
            dk_acc[...] = jnp.zeros_like(dk_acc)
            dv_acc[...] = jnp.zeros_like(dv_acc)

        distf, valid = _swa_geometry(n)
        r_prev = pl.multiple_of(jnp.maximum(n - 1, 0) * WINDOW, WINDOW)
        r_cur = pl.multiple_of(n * WINDOW, WINDOW)
        dsink = jnp.zeros((1, LANES), F32)
        for g in range(N_SWA_KV):
            ks, vs = _swa_kv_views(kp_ref, kc_ref, vp_ref, vc_ref, g)
            heads = _swa_group_heads(g)
            four = range(len(heads))
            qms = [jnp.where(hm[a], q_ref[:, pp * LANES:(pp + 1) * LANES], 0) for _, pp, a in heads]
            doms = [jnp.where(hm[a], do_ref[:, pp * LANES:(pp + 1) * LANES], 0) for _, pp, a in heads]
            qk = [_dot_nt(qms[i], ks[heads[i][2]]) for i in four]
            dp = [_dot_nt(doms[i], vs[heads[i][2]]) for i in four]
            soft = [_swa_softmax(qk[i], sink_ref[0, heads[i][0]], _swa_slope(heads[i][0]), distf, valid) for i in four]
            p = [soft[i][0] for i in four]
            delta = [jnp.sum(p[i] * dp[i], axis=1, keepdims=True) for i in four]
            ds = [(p[i] * (dp[i] - delta[i]) * ATT_SCALE).astype(_MXU_DTYPE) for i in four]
            for i in four:
                dsink = dsink + jnp.where(lane == heads[i][0], -jnp.sum(soft[i][1] * delta[i]), 0.0)
            dq = [_dot(ds[i], ks[heads[i][2]]) for i in four]
            dk_h = [_dot_tn(ds[i], qms[i]) for i in four]
            dv_h = [_dot_tn(p[i].astype(_MXU_DTYPE), doms[i]) for i in four]
            for j, pp in enumerate((2 * g, 2 * g + 1)):
                dq_ref[:, pp * LANES:(pp + 1) * LANES] = (dq[2 * j] + dq[2 * j + 1]).astype(dq_ref.dtype)
            dk_g = [dk_h[a] + dk_h[2 + a] for a in range(2)]
            dv_g = [dv_h[a] + dv_h[2 + a] for a in range(2)]
            bsel = g % 2
            dk_t = dk_g[bsel] + pltpu.roll(dk_g[1 - bsel], HEAD_DIM, 1)
            dv_t = dv_g[bsel] + pltpu.roll(dv_g[1 - bsel], HEAD_DIM, 1)
            c0 = (g // 2) * LANES
            dk_acc[pl.ds(r_prev, WINDOW), c0:c0 + LANES] += dk_t[:WINDOW]
            dk_acc[pl.ds(r_cur, WINDOW), c0:c0 + LANES] += dk_t[WINDOW:]
            dv_acc[pl.ds(r_prev, WINDOW), c0:c0 + LANES] += dv_t[:WINDOW]
            dv_acc[pl.ds(r_cur, WINDOW), c0:c0 + LANES] += dv_t[WINDOW:]
        dsink_ref[...] += dsink

        @pl.when(n == nb - 1)
        def _():
            dk_ref[...] = dk_acc[...].astype(dk_ref.dtype)
            dv_ref[...] = dv_acc[...].astype(dv_ref.dtype)

    seq_kv = pl.BlockSpec((S, SWA_KV), lambda b, n: (b, 0))
    return pl.pallas_call(
        body, name=name, grid=(B, nb),
        out_shape=(jax.ShapeDtypeStruct((T, SWA_Q), _MXU_DTYPE), jax.ShapeDtypeStruct((T, SWA_KV), _MXU_DTYPE),
                   jax.ShapeDtypeStruct((T, SWA_KV), _MXU_DTYPE), jax.ShapeDtypeStruct((1, LANES), F32)),
        in_specs=[sink_spec, q_spec] + kv_specs + [pl.BlockSpec((WINDOW, SWA_Q), lambda b, n: (b * nb + n, 0))],
        out_specs=(pl.BlockSpec((WINDOW, SWA_Q), lambda b, n: (b * nb + n, 0)), seq_kv, seq_kv,
                   pl.BlockSpec((1, LANES), lambda b, n: (0, 0))),
        scratch_shapes=[pltpu.VMEM((S, SWA_KV), F32), pltpu.VMEM((S, SWA_KV), F32)],
        compiler_params=_params("arbitrary", "arbitrary"),
    )(sinks, proj, proj, proj, proj, proj, dya)


SB_TILE = 256
SB_HALF = 128
SB_DEAD = -105.0


def _mark_lanes():
    lane = lax.broadcasted_iota(jnp.int32, (1, LANES), 1)
    return (lane == HEAD_DIM - 1) | (lane == LANES - 1)


def _tri2(cond):
    j = lax.broadcasted_iota(jnp.int32, (2 * SB_HALF, SB_HALF), 0) & (SB_HALF - 1)
    s = lax.broadcasted_iota(jnp.int32, (2 * SB_HALF, SB_HALF), 1)
    return cond(j, s).astype(_MXU_DTYPE)


def _half_cumsums(x, tri2):
    out = []
    for h in range(2):
        xh = x[:, h * SB_HALF:(h + 1) * SB_HALF]
        hi = xh.astype(_MXU_DTYPE)
        lo = (xh - hi.astype(F32)).astype(_MXU_DTYPE)
        out.append(_dot(jnp.concatenate([hi, lo], axis=1), tri2))
    return out


def _log_sigmoid(z):
    return jnp.minimum(z, 0.0) - jnp.log(1.0 + jnp.exp(-jnp.abs(z)))


def _sb_specs(B, S):
    qb = (SWA_Q + 2 * SWA_KV) // LANES
    kb = qb + SB_W // LANES
    vb = kb + SB_W // LANES
    return [pl.BlockSpec((S, LANES), functools.partial(lambda b, p, c: (b, c + p), c=c)) for c in (qb, kb, vb)]


def _sb_fwd(proj, B, S, name, jobs=()):
    T = B * S
    tq = SB_TILE
    nq = S // tq

    def body(q_ref, k_ref, v_ref, y_ref, tot_ref):
        hm = _half_masks()
        ji = lax.broadcasted_iota(jnp.int32, (tq, tq), 0)
        si = lax.broadcasted_iota(jnp.int32, (tq, tq), 1)
        tri_after = _tri2(lambda j, s: j > s)
        causal = si < ji
        mark = _mark_lanes()

        def q_loop(qi, carry):
            r0 = pl.multiple_of(qi * tq, tq)
            q_pair = q_ref[pl.ds(r0, tq), :] * ATT_SCALE
            qms = [jnp.where(hm[a], q_pair, 0) for a in range(2)]

            def tile(c0, state, diagonal):
                kk = k_ref[pl.ds(c0, tq), :]
                vv = v_ref[pl.ds(c0, tq), :]
                two = range(2)
                z = [_dot_nt(qms[a], kk) for a in two]
                lb = [_log_sigmoid(z[a]) for a in two]
                l1m = [jnp.where(causal, lb[a] - z[a], 0.0) if diagonal else lb[a] - z[a] for a in two]
                cum = [_half_cumsums(l1m[a], tri_after) for a in two]
                tot = [[cum[a][h][:, 0:1] + l1m[a][:, h * SB_HALF:h * SB_HALF + 1] for h in two] for a in two]
                after = [jnp.concatenate([cum[a][0] + (state[a][1] + tot[a][1]), cum[a][1] + state[a][1]], axis=1)
                         for a in two]
                att = [jnp.exp(lb[a] + after[a]) for a in two]
                if diagonal:
                    att = [jnp.where(causal, att[a], 0.0) for a in two]
                acc = [state[a][0] + _dot(att[a].astype(_MXU_DTYPE), jnp.where(hm[a], vv, 0)) for a in two]
                car = [state[a][1] + (tot[a][0] + tot[a][1]) for a in two]
                return tuple((acc[a], car[a]) for a in two)

            def live(st):
                return jnp.maximum(jnp.max(st[0][1]), jnp.max(st[1][1])) > SB_DEAD

            def step(c):
                it, _, st = c
                st = tile(pl.multiple_of((qi - 1 - it) * tq, tq), st, False)
                return it + 1, live(st), st

            zero = (jnp.zeros((tq, LANES), F32), jnp.zeros((tq, 1), F32))
            state = tile(r0, (zero, zero), True)
            done, _, state = lax.while_loop(lambda c: (c[0] < qi) & c[1], step, (jnp.int32(0), live(state), state))
            y_ref[pl.ds(r0, tq), :] = state[0][0] + state[1][0]
            first = (qi - done).astype(F32)
            tot_ref[pl.ds(r0, tq), :] = jnp.where(mark, first, jnp.where(hm[0], state[0][1], state[1][1]))
            return carry

        lax.fori_loop(0, nq, q_loop, 0)

    out_spec = pl.BlockSpec((S, LANES), lambda b, p: (b, p))
    return _call(
        body, name=name, grid=(B, SB_W // LANES),
        out_shape=(jax.ShapeDtypeStruct((T, SB_W), F32), jax.ShapeDtypeStruct((T, SB_W), F32)),
        in_specs=_sb_specs(B, S), out_specs=(out_spec, out_spec), args=(proj, proj, proj),
        semantics=("parallel", "parallel"), jobs=jobs)


def _sb_bwd(proj, tot, dyb, B, S, name, jobs=()):
    T = B * S
    tq = SB_TILE
    nq = S // tq

    def body(q_ref, k_ref, v_ref, do_ref, tot_ref, dq_ref, dk_ref, dv_ref, dk_acc, dv_acc):
        hm = _half_masks()
        ji = lax.broadcasted_iota(jnp.int32, (tq, tq), 0)
        si = lax.broadcasted_iota(jnp.int32, (tq, tq), 1)
        tri_incl = _tri2(lambda j, s: j <= s)
        tri_excl = _tri2(lambda j, s: j < s)
        causal = si < ji
        mark = _mark_lanes()
        dk_acc[...] = jnp.zeros_like(dk_acc)
        dv_acc[...] = jnp.zeros_like(dv_acc)

        def q_loop(qi, carry):
            r0 = pl.multiple_of(qi * tq, tq)
            q_pair = q_ref[pl.ds(r0, tq), :] * ATT_SCALE
            do_pair = do_ref[pl.ds(r0, tq), :]
            tot_pair = tot_ref[pl.ds(r0, tq), :]
            qms = [jnp.where(hm[a], q_pair, 0) for a in range(2)]
            doms = [jnp.where(hm[a], do_pair, 0) for a in range(2)]
            totals = [jnp.max(jnp.where(hm[a] & ~mark, tot_pair, -jnp.inf), axis=1, keepdims=True) for a in range(2)]
            first = jnp.max(jnp.where(mark, tot_pair, -jnp.inf))
            first = jnp.where((first >= 0.0) & (first <= qi.astype(F32)), first, 0.0).astype(jnp.int32)

            def tile(c0, state, diagonal):
                kk = k_ref[pl.ds(c0, tq), :]
                vv = v_ref[pl.ds(c0, tq), :]
                ks = kk * ATT_SCALE
                two = range(2)
                last = SB_HALF - 1
                z = [_dot_nt(qms[a], kk) for a in two]
                d_att = [_dot_nt(doms[a], vv) for a in two]
                lb = [_log_sigmoid(z[a]) for a in two]
                l1m = [jnp.where(causal, lb[a] - z[a], 0.0) if diagonal else lb[a] - z[a] for a in two]
                cum = [_half_cumsums(l1m[a], tri_incl) for a in two]
                upto = [jnp.concatenate([cum[a][0] + state[a][1],
                                         cum[a][1] + (state[a][1] + cum[a][0][:, last:last + 1])], axis=1) for a in two]
                att = [jnp.exp(lb[a] + (totals[a] - upto[a])) for a in two]
                if diagonal:
                    att = [jnp.where(causal, att[a], 0.0) for a in two]
                d_log = [d_att[a] * att[a] for a in two]
                cumd = [_half_cumsums(d_log[a], tri_excl) for a in two]
                totd = [[cumd[a][h][:, last:last + 1] + d_log[a][:, h * SB_HALF + last:h * SB_HALF + last + 1]
                         for h in two] for a in two]
                before = [jnp.concatenate([cumd[a][0] + state[a][2], cumd[a][1] + (state[a][2] + totd[a][0])], axis=1)
                          for a in two]
                sig = [jnp.exp(lb[a]) for a in two]
                dz = [d_log[a] * (1.0 - sig[a]) - sig[a] * before[a] for a in two]
                if diagonal:
                    dz = [jnp.where(causal, dz[a], 0.0) for a in two]
                dzb = [dz[a].astype(_MXU_DTYPE) for a in two]
                dq = [state[a][0] + _dot(dzb[a], jnp.where(hm[a], ks, 0)) for a in two]
                dk_acc[pl.ds(c0, tq), :] += _dot_tn(dzb[0], qms[0]) + _dot_tn(dzb[1], qms[1])
                dv_acc[pl.ds(c0, tq), :] += (_dot_tn(att[0].astype(_MXU_DTYPE), doms[0])
                                             + _dot_tn(att[1].astype(_MXU_DTYPE), doms[1]))
                cp = [upto[a][:, tq - 1:tq] for a in two]
                cq = [state[a][2] + (totd[a][0] + totd[a][1]) for a in two]
                return tuple((dq[a], cp[a], cq[a]) for a in two)

            zero_col = jnp.zeros((tq, 1), F32)
            zero = (jnp.zeros((tq, LANES), F32), zero_col, zero_col)
            state = lax.fori_loop(first, qi, lambda kj, st: tile(pl.multiple_of(kj * tq, tq), st, False), (zero, zero))
            state = tile(r0, state, True)
            dq_ref[pl.ds(r0, tq), :] = (state[0][0] + state[1][0]).astype(dq_ref.dtype)
            return carry

        lax.fori_loop(0, nq, q_loop, 0)
        dk_ref[...] = dk_acc[...].astype(dk_ref.dtype)
        dv_ref[...] = dv_acc[...].astype(dv_ref.dtype)

    pair = pl.BlockSpec((S, LANES), lambda b, p: (b, p))
    out = jax.ShapeDtypeStruct((T, SB_W), _MXU_DTYPE)
    return _call(
        body, name=name, grid=(B, SB_W // LANES), out_shape=(out, out, out),
        in_specs=_sb_specs(B, S) + [pair, pair], out_specs=(pair, pair, pair),
        scratch_shapes=[pltpu.VMEM((S, LANES), F32), pltpu.VMEM((S, LANES), F32)],
        args=(proj, proj, proj, dyb, tot), semantics=("parallel", "parallel"), jobs=jobs)


def _layer_step(x, tgt, B, S, small, comm):
    run, big, part = comm.run, comm.big, comm.partial
    ffn1_w, ffn2_w = ("ffn1_down", "ffn1_gate", "ffn1_up"), ("ffn2_down", "ffn2_gate", "ffn2_up")

    h1 = run(_rms_fwd, x, small["ffn1_norm"], "ffn1_rms", ag=("ffn1_up",))
    U1 = run(_mm_nt, h1, big["ffn1_up"], _MXU_DTYPE, "ffn1_up", ag=("ffn1_gate",))
    G1, A1 = run(_ffn_gate, h1, big["ffn1_gate"], U1, "ffn1_gate", ag=("ffn1_down",))
    x1 = run(_mm_nn, [(A1, big["ffn1_down"])], x, 0.5, F32, "ffn1_down", ag=("w_in",))
    h2 = run(_rms_fwd, x1, small["mix_norm"], "mix_rms")
    proj = run(_mm_nt, h2, big["w_in"], _MXU_DTYPE, "in_proj", ag=("w_out",))
    ya = run(_swa_fwd, proj, small["swa_sinks"], B, S, "swa_fwd", ag=("ffn2_gate",))
    yb, tot = run(_sb_fwd, proj, B, S, "sb_fwd", ag=("ffn2_up",))
    yn = _outnorm_fwd(ya, yb, small["swa_out_norm"], small["sb_out_norm"], "out_norm")
    x2 = run(_mm_nn, [(yn, big["w_out"])], x1, 1.0, F32, "out_proj")
    h3 = run(_rms_fwd, x2, small["ffn2_norm"], "ffn2_rms")
    G2, U2, A2 = run(_ffn_gu, h3, big["ffn2_gate"], big["ffn2_up"], "ffn2_gate_up", ag=("ffn2_down",))
    x3 = run(_mm_nn, [(A2, big["ffn2_down"])], x2, 0.5, F32, "ffn2_down")

    dx3, dx3b, d_final, loss = _loss_head(x3, small["final_norm"], tgt, "loss_head")

    dG2, dU2 = run(_ffn_bwd_act, dx3b, big["ffn2_down"], G2, U2, "ffn2_bwd_act")
    part["ffn2_down"] = run(_mm_tn, A2, dx3b, 0.5, _WIRE_DTYPE, "ffn2_dw_down")
    part["ffn2_gate"] = run(_mm_tn, dG2, h3, 1.0, _WIRE_DTYPE, "ffn2_dw_gate")
    part["ffn2_up"] = run(_mm_tn, dU2, h3, 1.0, _WIRE_DTYPE, "ffn2_dw_up")
    dh3 = run(_mm_nn, [(dG2, big["ffn2_gate"]), (dU2, big["ffn2_up"])], None, 1.0, F32, "ffn2_dh", rs1=ffn2_w)
    dx2, dx2b, d_g2 = _rms_bwd(dh3, x2, small["ffn2_norm"], dx3, "ffn2_rms_bwd")

    part["w_out"] = run(_mm_tn, yn, dx2b, 1.0, _WIRE_DTYPE, "dw_out")
    dyn = run(_mm_nt, dx2b, big["w_out"], F32, "out_proj_bwd")
    dya, dyb, d_ga, d_gb = _outnorm_bwd(dyn, ya, yb, small["swa_out_norm"], small["sb_out_norm"], "out_norm_bwd")
    dqa, dka, dva, d_sinks = _swa_bwd(proj, small["swa_sinks"], dya, B, S, "swa_bwd")
    dqb, dkb, dvb = run(_sb_bwd, proj, tot, dyb, B, S, "sb_bwd", rs2=ffn2_w)
    dproj = jnp.concatenate([dqa, dka, dva, dqb, dkb, dvb], axis=1)
    part["w_in"] = run(_mm_tn, dproj, h2, 1.0, _WIRE_DTYPE, "dw_in")
    dh2 = run(_mm_nn, [(dproj, big["w_in"])], None, 1.0, F32, "in_proj_bwd", rs1=("w_in", "w_out"))
    dx1, dx1b, d_gm = _rms_bwd(dh2, x1, small["mix_norm"], dx2, "mix_rms_bwd")

    dG1, dU1 = run(_ffn_bwd_act, dx1b, big["ffn1_down"], G1, U1, "ffn1_bwd_act", rs2=("w_in", "w_out"))
    part["ffn1_down"] = run(_mm_tn, A1, dx1b, 0.5, _WIRE_DTYPE, "ffn1_dw_down")
    part["ffn1_gate"] = run(_mm_tn, dG1, h1, 1.0, _WIRE_DTYPE, "ffn1_dw_gate", rs1=("ffn1_down",))
    part["ffn1_up"] = run(_mm_tn, dU1, h1, 1.0, _WIRE_DTYPE, "ffn1_dw_up", rs1=("ffn1_gate",), rs2=("ffn1_down",))
    dh1 = run(_mm_nn, [(dG1, big["ffn1_gate"])], None, 1.0, F32, "ffn1_dh_gate", rs1=("ffn1_up",), rs2=("ffn1_gate",))
    dh1 = run(_mm_nn, [(dU1, big["ffn1_up"])], dh1, 1.0, F32, "ffn1_dh_up", rs2=("ffn1_up",))
    gx, _, d_g1 = _rms_bwd(dh1, x, small["ffn1_norm"], dx1, "ffn1_rms_bwd")

    d_small = {"ffn1_norm": d_g1, "mix_norm": d_gm, "swa_sinks": d_sinks[:, :N_SWA_HEADS], "swa_out_norm": d_ga,
               "sb_out_norm": d_gb, "ffn2_norm": d_g2, "final_norm": d_final}
    return loss, gx, d_small


MESH = pl.DeviceIdType.MESH
BIG_NAMES = ("ffn1_gate", "ffn1_up", "ffn1_down", "w_in", "w_out", "ffn2_gate", "ffn2_up", "ffn2_down")
_COMM_PARAMS = pltpu.CompilerParams(has_side_effects=True)


def _place():
    x, y, c = lax.axis_index("x"), lax.axis_index("y"), lax.axis_index("c")
    other_chips = [(1 - x, y), (x, 1 - y), (1 - x, 1 - y)]
    return x, y, c, other_chips


def _padded_rows(rows):
    full = N_DEV * rows
    return -(-full // _F_TILE) * _F_TILE


def _ag_job(shards):
    nw = len(shards)
    D = shards[0].shape[1]
    rows_w = [s.shape[0] for s in shards]
    full_w = [_padded_rows(r) for r in rows_w]
    pad_w = [f - N_DEV * r for f, r in zip(full_w, rows_w)]
    max_pad = max(max(pad_w), 16)

    class Plan:
        def __init__(self, ins, outs, scratch):
            zbuf, send_sems, recv_sems, local_sems, zero_sems = scratch
            x, y, c, chips = _place()
            me, sibling = (x, y, c), (x, y, 1 - c)

            def rows(w, px, py, pc):
                start = pl.multiple_of((4 * px + 2 * py + pc) * rows_w[w], 16)
                return outs[w].at[pl.ds(start, rows_w[w]), :]

            def copy(w, k, block, to, src=None):
                return pltpu.make_async_remote_copy(
                    src_ref=rows(w, *block) if src is None else src, dst_ref=rows(w, *block),
                    send_sem=send_sems.at[w, k], recv_sem=recv_sems.at[w, k], device_id=to, device_id_type=MESH)

            self.zbuf = zbuf
            self.local = [pltpu.make_async_copy(zbuf.at[pl.ds(0, pad_w[w]), :],
                                                outs[w].at[pl.ds(N_DEV * rows_w[w], pad_w[w]), :], zero_sems.at[w])
                          for w in range(nw) if pad_w[w]]
            self.local += [pltpu.make_async_copy(ins[w], rows(w, *me), local_sems.at[w]) for w in range(nw)]
            self.first = [[copy(w, 0, me, sibling, src=ins[w])]
                          + [copy(w, 1 + j, me, (*chip, c), src=ins[w]) for j, chip in enumerate(chips)]
                          for w in range(nw)]
            self.arrive = [[copy(w, 1 + j, (*chip, c), me) for j, chip in enumerate(chips)] for w in range(nw)]
            self.passed = [[copy(w, 4 + j, (*chip, c), sibling) for j, chip in enumerate(chips)] for w in range(nw)]
            self.from_sibling = [[copy(w, 0, sibling, me)]
                                 + [copy(w, 4 + j, (*chip, 1 - c), me) for j, chip in enumerate(chips)]
                                 for w in range(nw)]

    def start(ins, outs, scratch):
        plan = Plan(ins, outs, scratch)
        plan.zbuf[...] = jnp.zeros_like(plan.zbuf)
        for cp in plan.local:
            cp.start()
        for w in range(nw):
            for cp in plan.first[w]:
                cp.start()

    def mid(ins, outs, scratch):
        plan = Plan(ins, outs, scratch)
        for w in range(nw):
            for arrived, onward in zip(plan.arrive[w], plan.passed[w]):
                arrived.wait_recv()
                onward.start()

    def finish(ins, outs, scratch):
        plan = Plan(ins, outs, scratch)
        for w in range(nw):
            for cp in plan.from_sibling[w]:
                cp.wait_recv()
        for w in range(nw):
            for cp in plan.first[w] + plan.passed[w]:
                cp.wait_send()
        for cp in plan.local:
            cp.wait()

    return _Job(
        ins=shards, out_shape=[jax.ShapeDtypeStruct((f, D), s.dtype) for f, s in zip(full_w, shards)],
        scratch=[pltpu.VMEM((max_pad, D), shards[0].dtype), pltpu.SemaphoreType.DMA((nw, 7)),
                 pltpu.SemaphoreType.DMA((nw, 7)), pltpu.SemaphoreType.DMA((nw,)), pltpu.SemaphoreType.DMA((nw,))],
        start=start, mid=mid, finish=finish)


def _rs1_job(partials, rows_w):
    nw = len(partials)
    D = partials[0].shape[1]

    def copies(ins, outs, scratch):
        send_sems, recv_sems = scratch
        x, y, c, _ = _place()
        out = []
        for w in range(nw):
            r = rows_w[w]
            for q in range(4):
                src = ins[w].at[pl.ds(pl.multiple_of((2 * q + 1 - c) * r, 16), r), :]
                out.append(pltpu.make_async_remote_copy(
                    src_ref=src, dst_ref=outs[w].at[pl.ds(q * r, r), :], send_sem=send_sems.at[w, q],
                    recv_sem=recv_sems.at[w, q], device_id=(x, y, 1 - c), device_id_type=MESH))
        return out

    def start(ins, outs, scratch):
        for cp in copies(ins, outs, scratch):
            cp.start()

    def finish(ins, outs, scratch):
        for cp in copies(ins, outs, scratch):
            cp.wait()

    return _Job(
        ins=partials, out_shape=[jax.ShapeDtypeStruct((4 * r, D), p.dtype) for r, p in zip(rows_w, partials)],
        scratch=[pltpu.SemaphoreType.DMA((nw, 4)), pltpu.SemaphoreType.DMA((nw, 4))], start=start, finish=finish)


def _pair_sum(partial, from_sibling, rows, core, name):
    D = partial.shape[1]

    def body(core_ref, p_ref, s_ref, o_ref):
        o_ref[...] = (p_ref[...].astype(F32) + s_ref[...].astype(F32)).astype(o_ref.dtype)

    grid_spec = pltpu.PrefetchScalarGridSpec(
        num_scalar_prefetch=1, grid=(4,),
        in_specs=[pl.BlockSpec((rows, D), lambda q, core_ref: (2 * q + core_ref[0], 0)),
                  pl.BlockSpec((rows, D), lambda q, core_ref: (q, 0))],
        out_specs=pl.BlockSpec((rows, D), lambda q, core_ref: (q, 0)))
    return pl.pallas_call(
        body, name=name, grid_spec=grid_spec, out_shape=jax.ShapeDtypeStruct((4 * rows, D), partial.dtype),
        compiler_params=_params("arbitrary"),
    )(core, partial, from_sibling)


def _rs2_job(chip_sums, rows_w):
    nw = len(chip_sums)

    def copies(ins, outs, scratch):
        send_sems, recv_sems, local_sems = scratch
        x, y, c, chips = _place()
        my_chip = 2 * x + y
        out = []
        for w in range(nw):
            r = rows_w[w]
            mine = pl.ds(pl.multiple_of(my_chip * r, 16), r)
            out.append(pltpu.make_async_copy(ins[w].at[mine, :], outs[w].at[mine, :], local_sems.at[w]))
            for j, (qx, qy) in enumerate(chips):
                src = ins[w].at[pl.ds(pl.multiple_of((2 * qx + qy) * r, 16), r), :]
                out.append(pltpu.make_async_remote_copy(
                    src_ref=src, dst_ref=outs[w].at[mine, :], send_sem=send_sems.at[w, j],
                    recv_sem=recv_sems.at[w, j], device_id=(qx, qy, c), device_id_type=MESH))
        return out

    def start(ins, outs, scratch):
        for cp in copies(ins, outs, scratch):
            cp.start()

    def finish(ins, outs, scratch):
        for cp in copies(ins, outs, scratch):
            cp.wait()

    return _Job(
        ins=chip_sums, out_shape=[jax.ShapeDtypeStruct(s.shape, s.dtype) for s in chip_sums],
        scratch=[pltpu.SemaphoreType.DMA((nw, 3)), pltpu.SemaphoreType.DMA((nw, 3)), pltpu.SemaphoreType.DMA((nw,))],
        start=start, finish=finish)


class _Comm:
    def __init__(self, shards):
        self.shards = shards
        self.rows = {n: s.shape[0] for n, s in shards.items()}
        self.core = lax.axis_index("c").astype(jnp.int32).reshape(1)
        self.big, self.partial, self.chip_sums, self.slots = {}, {}, {}, {}

    def run(self, fn, *args, ag=(), rs1=(), rs2=()):
        jobs = []
        if ag:
            jobs.append(_ag_job([self.shards[n] for n in ag]))
        if rs1:
            jobs.append(_rs1_job([self.partial[n] for n in rs1], [self.rows[n] for n in rs1]))
        if rs2:
            jobs.append(_rs2_job([self.chip_sums[n] for n in rs2], [self.rows[n] for n in rs2]))
        out, job_res = fn(*args, jobs=jobs)
        job_res = iter(job_res)
        if ag:
            self.big.update(zip(ag, next(job_res)))
        if rs1:
            for n, got in zip(rs1, next(job_res)):
                self.chip_sums[n] = _pair_sum(self.partial[n], got, self.rows[n], self.core, "pair_sum_" + n)
        if rs2:
            self.slots.update(zip(rs2, next(job_res)))
        return out


SMALL_ROWS = 88


def _small_allreduce(vec):
    def body(v_ref, o_ref, gather, send_sems, recv_sems):
        x, y, c, _ = _place()
        my_id = 4 * x + 2 * y + c
        gather[my_id] = v_ref[...]
        copies = []
        for r in range(1, N_DEV):
            peer = (x ^ (r >> 2), y ^ ((r >> 1) & 1), c ^ (r & 1))
            cp = pltpu.make_async_remote_copy(src_ref=v_ref, dst_ref=gather.at[my_id], send_sem=send_sems.at[r - 1],
                                              recv_sem=recv_sems.at[r - 1], device_id=peer, device_id_type=MESH)
            cp.start()
            copies.append(cp)
        for cp in copies:
            cp.wait()
        acc = gather[0]
        for d in range(1, N_DEV):
            acc = acc + gather[d]
        o_ref[...] = acc

    vm = pl.BlockSpec(memory_space=pltpu.VMEM)
    return pl.pallas_call(
        body, name="small_allreduce", out_shape=jax.ShapeDtypeStruct(vec.shape, F32),
        in_specs=[vm], out_specs=vm,
        scratch_shapes=[pltpu.VMEM((N_DEV,) + vec.shape, F32), pltpu.SemaphoreType.DMA((N_DEV - 1,)),
                        pltpu.SemaphoreType.DMA((N_DEV - 1,))],
        compiler_params=_COMM_PARAMS,
    )(vec)


def _adamw_update(w, g, m, v):
    nm = ADAM_B1 * m + (1.0 - ADAM_B1) * g
    nv = ADAM_B2 * v + (1.0 - ADAM_B2) * jnp.square(g)
    m_hat = nm / (1.0 - ADAM_B1 ** ADAM_STEP)
    v_hat = nv / (1.0 - ADAM_B2 ** ADAM_STEP)
    return -ADAM_LR * (m_hat / (jnp.sqrt(v_hat) + ADAM_EPS) + ADAM_WD * w), nm, nv


def _adamw(w, g, m, v, name):
    R, C = w.shape
    tr = _tile(R, 256, 8)

    def body(w_ref, g_ref, m_ref, v_ref, d_ref, nm_ref, nv_ref):
        d_ref[...], nm_ref[...], nv_ref[...] = _adamw_update(w_ref[...], g_ref[...], m_ref[...], v_ref[...])

    spec = pl.BlockSpec((tr, C), lambda i: (i, 0))
    out = jax.ShapeDtypeStruct((R, C), F32)
    return pl.pallas_call(
        body, name=name, grid=(R // tr,), out_shape=(out, out, out),
        in_specs=[spec] * 4, out_specs=(spec, spec, spec),
        compiler_params=_params("parallel"),
    )(w, g, m, v)


def _adamw_slots(w, slots, m, v, name):
    R, C = w.shape
    tc = _tile(C, 512, LANES)

    def body(w_ref, s_ref, m_ref, v_ref, g_ref, d_ref, nm_ref, nv_ref):
        g = s_ref[0].astype(F32)
        for q in range(1, 4):
            g = g + s_ref[q].astype(F32)
        g_ref[...] = g
        d_ref[...], nm_ref[...], nv_ref[...] = _adamw_update(w_ref[...], g, m_ref[...], v_ref[...])

    spec = pl.BlockSpec((R, tc), lambda j: (0, j))
    out = jax.ShapeDtypeStruct((R, C), F32)
    return pl.pallas_call(
        body, name=name, grid=(C // tc,), out_shape=(out, out, out, out),
        in_specs=[spec, pl.BlockSpec((4, R, tc), lambda j: (0, 0, j)), spec, spec], out_specs=(spec, spec, spec, spec),
        compiler_params=_params("parallel"),
    )(w, slots, m, v)


WEIGHT_NAMES = ("ffn1_norm", "ffn1_w_gate", "ffn1_w_up", "ffn1_w_down", "mix_norm", "w_in", "swa_sinks",
                "swa_out_norm", "sb_out_norm", "w_out", "ffn2_norm", "ffn2_w_gate", "ffn2_w_up", "ffn2_w_down",
                "final_norm")
SMALL_NAMES = ("ffn1_norm", "mix_norm", "swa_sinks", "swa_out_norm", "sb_out_norm", "ffn2_norm", "final_norm")
BIG_ARGS = {"ffn1_gate": ("ffn1_w_gate", True), "ffn1_up": ("ffn1_w_up", True), "ffn1_down": ("ffn1_w_down", False),
            "w_in": ("w_in", True), "w_out": ("w_out", False), "ffn2_gate": ("ffn2_w_gate", True),
            "ffn2_up": ("ffn2_w_up", True), "ffn2_down": ("ffn2_w_down", False)}


def _pack_small(parts):
    padded = [jnp.pad(p.reshape(1, -1), ((0, 0), (0, -p.size % LANES))) for p in parts]
    flat = jnp.concatenate(padded, axis=1)
    flat = jnp.pad(flat, ((0, 0), (0, SMALL_ROWS * LANES - flat.shape[1])))
    return flat.reshape(SMALL_ROWS, LANES)


def _unpack_small(block, shapes):
    flat = block.reshape(-1)
    out, off = [], 0
    for shp in shapes:
        n = 1
        for s in shp:
            n *= s
        out.append(flat[off:off + n].reshape(shp))
        off += n + (-n % LANES)
    return out


def kernel(x, ffn1_norm, ffn1_w_gate, ffn1_w_up, ffn1_w_down, mix_norm, w_in, swa_sinks, swa_out_norm, sb_out_norm, w_out, ffn2_norm, ffn2_w_gate, ffn2_w_up, ffn2_w_down, final_norm, loss_target, m_ffn1_norm, m_ffn1_w_gate, m_ffn1_w_up, m_ffn1_w_down, m_mix_norm, m_w_in, m_swa_sinks, m_swa_out_norm, m_sb_out_norm, m_w_out, m_ffn2_norm, m_ffn2_w_gate, m_ffn2_w_up, m_ffn2_w_down, m_final_norm, v_ffn1_norm, v_ffn1_w_gate, v_ffn1_w_up, v_ffn1_w_down, v_mix_norm, v_w_in, v_swa_sinks, v_swa_out_norm, v_sb_out_norm, v_w_out, v_ffn2_norm, v_ffn2_w_gate, v_ffn2_w_up, v_ffn2_w_down, v_final_norm):
    args = dict(locals())
    B, S, D = x.shape
    T = B * S
    weights = {n: args[n] for n in WEIGHT_NAMES}
    mom_m = {n: args["m_" + n] for n in WEIGHT_NAMES}
    mom_v = {n: args["v_" + n] for n in WEIGHT_NAMES}

    shards = {}
    for name in BIG_NAMES:
        arg, transposed = BIG_ARGS[name]
        w2 = weights[arg][0]
        shards[name] = (w2.T if transposed else w2).astype(_WIRE_DTYPE)
    comm = _Comm(shards)
    small = {n: weights[n].reshape(1, -1) for n in SMALL_NAMES}

    loss, gx, d_small = _layer_step(x.reshape(T, D), loss_target.reshape(T, D), B, S, small, comm)

    small_shapes = [(1, 1)] + [d_small[n].shape for n in SMALL_NAMES]
    reduced = _small_allreduce(_pack_small([loss[:, :1]] + [d_small[n] for n in SMALL_NAMES]))
    red = _unpack_small(reduced, small_shapes)
    loss_out = red[0].reshape(())
    g_small = dict(zip(SMALL_NAMES, red[1:]))

    grads, deltas, new_m, new_v = {}, {}, {}, {}
    for name in BIG_NAMES:
        arg, transposed = BIG_ARGS[name]
        to_rows = (lambda t: t[0].T) if transposed else (lambda t: t[0])
        back = (lambda t: t.T[None]) if transposed else (lambda t: t[None])
        slots = comm.slots[name].reshape(4, comm.rows[name], D)
        res = _adamw_slots(to_rows(weights[arg]), slots, to_rows(mom_m[arg]), to_rows(mom_v[arg]), "adamw_" + name)
        grads[arg], deltas[arg], new_m[arg], new_v[arg] = [back(t) for t in res]
    shapes1 = [(1, weights[n].size) for n in SMALL_NAMES]
    packed = [_pack_small([t[n].reshape(1, -1) for n in SMALL_NAMES]) for t in (weights, g_small, mom_m, mom_v)]
    upd = _adamw(*packed, "adamw_small")
    for tgt_dict, block in zip((deltas, new_m, new_v), upd):
        for n, val in zip(SMALL_NAMES, _unpack_small(block, shapes1)):
            tgt_dict[n] = val.reshape(weights[n].shape)
    for n in SMALL_NAMES:
        grads[n] = g_small[n].reshape(weights[n].shape)

    return (loss_out, gx.reshape(B, S, D), *[grads[n] for n in WEIGHT_NAMES], *[deltas[n] for n in WEIGHT_NAMES],
            *[new_m[n] for n in WEIGHT_NAMES], *[new_v[n] for n in WEIGHT_NAMES])
```

```python
import functools

import jax
import jax.numpy as jnp
from jax import lax
from jax.experimental import pallas as pl
from jax.experimental.pallas import tpu as pltpu

F32 = jnp.float32
_MXU_DTYPE = jnp.bfloat16
_WIRE_DTYPE = jnp.bfloat16

EPS = 1e-6
HEAD_DIM = 64
N_SWA_HEADS = 16
N_SWA_KV = 4
N_SB_HEADS = 16
WINDOW = 128
SWA_Q = N_SWA_HEADS * HEAD_DIM
SWA_KV = N_SWA_KV * HEAD_DIM
SB_W = N_SB_HEADS * HEAD_DIM
IN_W = SWA_Q + 2 * SWA_KV + 3 * SB_W
LANES = 128
ATT_SCALE = HEAD_DIM ** -0.5

ADAM_LR = 0.001
ADAM_B1 = 0.9
ADAM_B2 = 0.999
ADAM_EPS = 1e-08
ADAM_WD = 0.01
ADAM_STEP = 10

N_DEV = 8
_VMEM_LIMIT_BYTES = 56 * 1024 * 1024
_F_TILE = 512


def _params(*semantics):
    return pltpu.CompilerParams(dimension_semantics=semantics, vmem_limit_bytes=_VMEM_LIMIT_BYTES)


def _tile(n, pref, align):
    t = min(n, pref)
    t -= t % align
    while t >= align:
        if n % t == 0:
            return t
        t -= align
    return n


def _dot(a, b):
    return lax.dot_general(a, b, (((1,), (0,)), ((), ())), preferred_element_type=F32)


def _dot_nt(a, b):
    return lax.dot_general(a, b, (((1,), (1,)), ((), ())), preferred_element_type=F32)


def _dot_tn(a, b):
    return lax.dot_general(a, b, (((0,), (0,)), ((), ())), preferred_element_type=F32)


class _Job:
    def __init__(self, ins, out_shape, scratch, start, finish, mid=None):
        self.ins, self.out_shape, self.scratch = list(ins), list(out_shape), list(scratch)
        self.start, self.mid, self.finish = start, mid, finish


_JOB_MID_FRACTION = 0.8


def _call(body, *, name, grid, in_specs, out_specs, out_shape, args, semantics, scratch_shapes=(), jobs=()):
    single = not isinstance(out_shape, (tuple, list))
    if not jobs:
        res = pl.pallas_call(body, name=name, grid=grid, in_specs=list(in_specs), out_specs=out_specs,
                             out_shape=out_shape, scratch_shapes=list(scratch_shapes),
                             compiler_params=_params(*semantics))(*args)
        return res, []
    base_out = [out_shape] if single else list(out_shape)
    base_out_specs = [out_specs] if single else list(out_specs)
    n_in, n_out, n_scr = len(args), len(base_out), len(scratch_shapes)
    any_spec = pl.BlockSpec(memory_space=pl.ANY)
    total = 1
    for g in grid:
        total *= g
    mid_step = min(total - 1, int(total * _JOB_MID_FRACTION))

    def wrapped(*refs):
        pos = n_in
        job_ins = []
        for job in jobs:
            job_ins.append(refs[pos:pos + len(job.ins)])
            pos += len(job.ins)
        outs = refs[pos:pos + n_out]
        pos += n_out
        job_outs = []
        for job in jobs:
            job_outs.append(refs[pos:pos + len(job.out_shape)])
            pos += len(job.out_shape)
        scr = refs[pos:pos + n_scr]
        pos += n_scr
        job_scr = []
        for job in jobs:
            job_scr.append(refs[pos:pos + len(job.scratch)])
            pos += len(job.scratch)
        step = pl.program_id(0)
        for d in range(1, len(grid)):
            step = step * grid[d] + pl.program_id(d)

        @pl.when(step == 0)
        def _():
            for job, ji, jo, js in zip(jobs, job_ins, job_outs, job_scr):
                job.start(ji, jo, js)

        @pl.when(step == mid_step)
        def _():
            for job, ji, jo, js in zip(jobs, job_ins, job_outs, job_scr):
                if job.mid is not None:
                    job.mid(ji, jo, js)

        body(*refs[:n_in], *outs, *scr)

        @pl.when(step == total - 1)
        def _():
            for job, ji, jo, js in zip(jobs, job_ins, job_outs, job_scr):
                job.finish(ji, jo, js)

    all_args = list(args) + [a for job in jobs for a in job.ins]
    all_in_specs = list(in_specs) + [any_spec for job in jobs for _ in job.ins]
    all_out_shape = base_out + [s for job in jobs for s in job.out_shape]
    all_out_specs = base_out_specs + [any_spec for job in jobs for _ in job.out_shape]
    all_scratch = list(scratch_shapes) + [s for job in jobs for s in job.scratch]
    res = pl.pallas_call(
        wrapped, name=name, grid=grid, in_specs=all_in_specs, out_specs=tuple(all_out_specs),
        out_shape=tuple(all_out_shape), scratch_shapes=all_scratch,
        compiler_params=pltpu.CompilerParams(dimension_semantics=("arbitrary",) * len(grid),
                                             vmem_limit_bytes=_VMEM_LIMIT_BYTES, has_side_effects=True),
    )(*all_args)
    base = res[0] if single else tuple(res[:n_out])
    job_res, pos = [], n_out
    for job in jobs:
        job_res.append(tuple(res[pos:pos + len(job.out_shape)]))
        pos += len(job.out_shape)
    return base, job_res


def _rms_fwd(x, g, name, jobs=()):
    T, D = x.shape
    tm = _tile(T, 512, 16)

    def body(x_ref, g_ref, o_ref):
        xv = x_ref[...]
        r = lax.rsqrt(jnp.mean(xv * xv, axis=-1, keepdims=True) + EPS)
        o_ref[...] = (xv * r * g_ref[...]).astype(o_ref.dtype)

    return _call(
        body, name=name, grid=(T // tm,),
        out_shape=jax.ShapeDtypeStruct((T, D), _MXU_DTYPE),
        in_specs=[pl.BlockSpec((tm, D), lambda i: (i, 0)), pl.BlockSpec((1, D), lambda i: (0, 0))],
        out_specs=pl.BlockSpec((tm, D), lambda i: (i, 0)), args=(x, g), semantics=("parallel",), jobs=jobs)


def _rms_bwd_rows(dh, xv, g):
    r = lax.rsqrt(jnp.mean(xv * xv, axis=-1, keepdims=True) + EPS)
    xhat = xv * r
    u = dh * g
    dx = r * (u - xhat * jnp.mean(u * xhat, axis=-1, keepdims=True))
    return dx, dh * xhat


def _rms_bwd(dh, x, g, dres, name):
    T, D = x.shape
    tm = _tile(T, 256, 16)

    def body(dh_ref, x_ref, g_ref, dres_ref, dx_ref, dxb_ref, dg_ref):
        @pl.when(pl.program_id(0) == 0)
        def _():
            dg_ref[...] = jnp.zeros_like(dg_ref)

        dx, dgr = _rms_bwd_rows(dh_ref[...], x_ref[...], g_ref[...])
        dx = dres_ref[...] + dx
        dx_ref[...] = dx
        dxb_ref[...] = dx.astype(dxb_ref.dtype)
        dg_ref[...] += jnp.sum(dgr, axis=0, keepdims=True)

    row = pl.BlockSpec((tm, D), lambda i: (i, 0))
    vec = pl.BlockSpec((1, D), lambda i: (0, 0))
    return pl.pallas_call(
        body, name=name, grid=(T // tm,),
        out_shape=(jax.ShapeDtypeStruct((T, D), F32), jax.ShapeDtypeStruct((T, D), _MXU_DTYPE),
                   jax.ShapeDtypeStruct((1, D), F32)),
        in_specs=[row, row, vec, row], out_specs=(row, row, vec),
        compiler_params=_params("arbitrary"),
    )(dh, x, g, dres)


def _loss_head(x, g, tgt, name):
    T, D = x.shape
    tm = _tile(T, 256, 16)

    def body(x_ref, g_ref, t_ref, dx_ref, dxb_ref, dg_ref, loss_ref):
        @pl.when(pl.program_id(0) == 0)
        def _():
            dg_ref[...] = jnp.zeros_like(dg_ref)
            loss_ref[...] = jnp.zeros_like(loss_ref)

        xv = x_ref[...]
        gv = g_ref[...]
        r = lax.rsqrt(jnp.mean(xv * xv, axis=-1, keepdims=True) + EPS)
        xhat = xv * r
        diff = xhat * gv - t_ref[...]
        tok = jnp.mean(diff * diff, axis=-1, keepdims=True)
        loss_ref[...] += 0.5 * jnp.sum(tok, axis=0, keepdims=True)
        dy = diff / D
        u = dy * gv
        dx = r * (u - xhat * jnp.mean(u * xhat, axis=-1, keepdims=True))
        dx_ref[...] = dx
        dxb_ref[...] = dx.astype(dxb_ref.dtype)
        dg_ref[...] += jnp.sum(dy * xhat, axis=0, keepdims=True)

    row = pl.BlockSpec((tm, D), lambda i: (i, 0))
    vec = pl.BlockSpec((1, D), lambda i: (0, 0))
    return pl.pallas_call(
        body, name=name, grid=(T // tm,),
        out_shape=(jax.ShapeDtypeStruct((T, D), F32), jax.ShapeDtypeStruct((T, D), _MXU_DTYPE),
                   jax.ShapeDtypeStruct((1, D), F32), jax.ShapeDtypeStruct((1, LANES), F32)),
        in_specs=[row, vec, row],
        out_specs=(row, row, vec, pl.BlockSpec((1, LANES), lambda i: (0, 0))),
        compiler_params=_params("arbitrary"),
    )(x, g, tgt)


def _outnorm_fwd(ya, yb, ga, gb, name):
    T, W = ya.shape
    tm = _tile(T, 512, 16)

    def body(ya_ref, yb_ref, ga_ref, gb_ref, o_ref):
        for k, (y_ref, g_ref) in enumerate(((ya_ref, ga_ref), (yb_ref, gb_ref))):
            yv = y_ref[...]
            r = lax.rsqrt(jnp.mean(yv * yv, axis=-1, keepdims=True) + EPS)
            o_ref[:, k * W:(k + 1) * W] = (yv * r * g_ref[...]).astype(o_ref.dtype)

    row = pl.BlockSpec((tm, W), lambda i: (i, 0))
    vec = pl.BlockSpec((1, W), lambda i: (0, 0))
    return pl.pallas_call(
        body, name=name, grid=(T // tm,),
        out_shape=jax.ShapeDtypeStruct((T, 2 * W), _MXU_DTYPE),
        in_specs=[row, row, vec, vec], out_specs=pl.BlockSpec((tm, 2 * W), lambda i: (i, 0)),
        compiler_params=_params("parallel"),
    )(ya, yb, ga, gb)


def _outnorm_bwd(dyn, ya, yb, ga, gb, name):
    T, W = ya.shape
    tm = _tile(T, 256, 16)

    def body(d_ref, ya_ref, yb_ref, ga_ref, gb_ref, dya_ref, dyb_ref, dga_ref, dgb_ref):
        @pl.when(pl.program_id(0) == 0)
        def _():
            dga_ref[...] = jnp.zeros_like(dga_ref)
            dgb_ref[...] = jnp.zeros_like(dgb_ref)

        for k, (y_ref, g_ref, dy_ref, dg_ref) in enumerate(
                ((ya_ref, ga_ref, dya_ref, dga_ref), (yb_ref, gb_ref, dyb_ref, dgb_ref))):
            dy, dgr = _rms_bwd_rows(d_ref[:, k * W:(k + 1) * W], y_ref[...], g_ref[...])
            dy_ref[...] = dy.astype(dy_ref.dtype)
            dg_ref[...] += jnp.sum(dgr, axis=0, keepdims=True)

    row = pl.BlockSpec((tm, W), lambda i: (i, 0))
    vec = pl.BlockSpec((1, W), lambda i: (0, 0))
    return pl.pallas_call(
        body, name=name, grid=(T // tm,),
        out_shape=(jax.ShapeDtypeStruct((T, W), _MXU_DTYPE), jax.ShapeDtypeStruct((T, W), _MXU_DTYPE),
                   jax.ShapeDtypeStruct((1, W), F32), jax.ShapeDtypeStruct((1, W), F32)),
        in_specs=[pl.BlockSpec((tm, 2 * W), lambda i: (i, 0)), row, row, vec, vec],
        out_specs=(row, row, vec, vec),
        compiler_params=_params("arbitrary"),
    )(dyn, ya, yb, ga, gb)


_STRIP_ROWS = 256


def _strips(rows):
    step = min(rows, _STRIP_ROWS)
    return [slice(r, r + step) for r in range(0, rows, step)]


def _ffn_gu(h, wg_t, wu_t, name, jobs=()):
    T, D = h.shape
    Fp = wg_t.shape[0]
    tm = _tile(T, 1024, 16)
    tn = _tile(Fp, _F_TILE, LANES)

    def body(h_ref, wg_ref, wu_ref, g_ref, u_ref, a_ref):
        for rows in _strips(tm):
            hv = h_ref[rows, :]
            g = _dot_nt(hv, wg_ref[...])
            u = _dot_nt(hv, wu_ref[...])
            g_ref[rows, :] = g.astype(g_ref.dtype)
            u_ref[rows, :] = u.astype(u_ref.dtype)
            a_ref[rows, :] = (g * jax.nn.sigmoid(g) * u).astype(a_ref.dtype)

    act = pl.BlockSpec((tm, tn), lambda n, m: (m, n))
    wsp = pl.BlockSpec((tn, D), lambda n, m: (n, 0))
    out = jax.ShapeDtypeStruct((T, Fp), _MXU_DTYPE)
    return _call(
        body, name=name, grid=(Fp // tn, T // tm), out_shape=(out, out, out),
        in_specs=[pl.BlockSpec((tm, D), lambda n, m: (m, 0)), wsp, wsp],
        out_specs=(act, act, act), args=(h, wg_t, wu_t), semantics=("parallel", "parallel"), jobs=jobs)


def _ffn_bwd_act(dxb, wd, G, U, name, jobs=()):
    T, D = dxb.shape
    Fp = wd.shape[0]
    tm = _tile(T, 1024, 16)
    tn = _tile(Fp, _F_TILE, LANES)

    def body(e_ref, wd_ref, g_ref, u_ref, dg_ref, du_ref):
        for rows in _strips(tm):
            da = 0.5 * _dot_nt(e_ref[rows, :], wd_ref[...])
            g = g_ref[rows, :].astype(F32)
            u = u_ref[rows, :].astype(F32)
            s = jax.nn.sigmoid(g)
            du_ref[rows, :] = (da * (g * s)).astype(du_ref.dtype)
            dg_ref[rows, :] = (da * u * (s * (1.0 + g * (1.0 - s)))).astype(dg_ref.dtype)

    act = pl.BlockSpec((tm, tn), lambda m, n: (m, n))
    out = jax.ShapeDtypeStruct((T, Fp), _MXU_DTYPE)
    return _call(
        body, name=name, grid=(T // tm, Fp // tn), out_shape=(out, out),
        in_specs=[pl.BlockSpec((tm, D), lambda m, n: (m, 0)), pl.BlockSpec((tn, D), lambda m, n: (n, 0)),
                  act, act],
        out_specs=(act, act), args=(dxb, wd, G, U), semantics=("parallel", "parallel"), jobs=jobs)


def _ffn_gate(h, wg_t, U, name, jobs=()):
    T, D = h.shape
    Fp = wg_t.shape[0]
    tm = _tile(T, 1024, 16)
    tn = _tile(Fp, _F_TILE, LANES)

    def body(h_ref, wg_ref, u_ref, g_ref, a_ref):
        for rows in _strips(tm):
            g = _dot_nt(h_ref[rows, :], wg_ref[...])
            g_ref[rows, :] = g.astype(g_ref.dtype)
            a_ref[rows, :] = (g * jax.nn.sigmoid(g) * u_ref[rows, :].astype(F32)).astype(a_ref.dtype)

    act = pl.BlockSpec((tm, tn), lambda m, n: (m, n))
    out = jax.ShapeDtypeStruct((T, Fp), _MXU_DTYPE)
    return _call(
        body, name=name, grid=(T // tm, Fp // tn), out_shape=(out, out),
        in_specs=[pl.BlockSpec((tm, D), lambda m, n: (m, 0)), pl.BlockSpec((tn, D), lambda m, n: (n, 0)), act],
        out_specs=(act, act), args=(h, wg_t, U), semantics=("parallel", "parallel"), jobs=jobs)


def _mm_nt(a, b, out_dtype, name, jobs=()):
    M, K = a.shape
    N = b.shape[0]
    tm = _tile(M, 1024, 16)
    tn = _tile(N, 512, LANES)

    def body(a_ref, b_ref, o_ref):
        o_ref[...] = _dot_nt(a_ref[...], b_ref[...]).astype(o_ref.dtype)

    return _call(
        body, name=name, grid=(M // tm, N // tn), out_shape=jax.ShapeDtypeStruct((M, N), out_dtype),
        in_specs=[pl.BlockSpec((tm, K), lambda m, n: (m, 0)), pl.BlockSpec((tn, K), lambda m, n: (n, 0))],
        out_specs=pl.BlockSpec((tm, tn), lambda m, n: (m, n)), args=(a, b),
        semantics=("parallel", "parallel"), jobs=jobs)


_MM_OPERAND_BYTES = 26 * 1024 * 1024


def _k_tile(K, bytes_per_k, align):
    best = align
    for t in range(align, K + 1, align):
        if K % t == 0 and 2 * t * bytes_per_k <= _MM_OPERAND_BYTES:
            best = t
    return best


def _mm_nn(pairs, res, alpha, out_dtype, name, jobs=()):
    M, K = pairs[0][0].shape
    N = pairs[0][1].shape[1]
    n_pairs = len(pairs)
    tm = _tile(M, 1024, 16)
    tn = _tile(N, 1024, LANES)
    tk = _k_tile(K, n_pairs * (tm + tn) * pairs[0][0].dtype.itemsize, LANES)
    nk = K // tk

    def body(*refs):
        ab = refs[:2 * n_pairs]
        res_ref = refs[2 * n_pairs] if res is not None else None
        o_ref = refs[2 * n_pairs + (res is not None)]

        def finish(acc):
            out = alpha * acc
            if res_ref is not None:
                out = res_ref[...] + out
            o_ref[...] = out.astype(o_ref.dtype)

        part = _dot(ab[0][...], ab[1][...])
        for i in range(1, n_pairs):
            part = part + _dot(ab[2 * i][...], ab[2 * i + 1][...])
        if nk == 1:
            finish(part)
        else:
            acc_ref = refs[-1]
            k = pl.program_id(2)

            @pl.when(k == 0)
            def _():
                acc_ref[...] = part

            @pl.when(k > 0)
            def _():
                acc_ref[...] += part

            @pl.when(k == nk - 1)
            def _():
                finish(acc_ref[...])

    in_specs, args = [], []
    for a, b in pairs:
        in_specs += [pl.BlockSpec((tm, tk), lambda m, n, k: (m, k)), pl.BlockSpec((tk, tn), lambda m, n, k: (k, n))]
        args += [a, b]
    if res is not None:
        in_specs.append(pl.BlockSpec((tm, tn), lambda m, n, k: (m, n)))
        args.append(res)
    return _call(
        body, name=name, grid=(M // tm, N // tn, nk), out_shape=jax.ShapeDtypeStruct((M, N), out_dtype),
        in_specs=in_specs, out_specs=pl.BlockSpec((tm, tn), lambda m, n, k: (m, n)),
        scratch_shapes=[pltpu.VMEM((tm, tn), F32)] if nk > 1 else [], args=args,
        semantics=("parallel", "parallel", "arbitrary"), jobs=jobs)


def _mm_tn(a, b, alpha, out_dtype, name, jobs=()):
    K, M = a.shape
    N = b.shape[1]
    tm = _tile(M, 512, LANES)
    tn = _tile(N, 1024, LANES)

    def body(a_ref, b_ref, o_ref):
        o_ref[...] = (alpha * _dot_tn(a_ref[...], b_ref[...])).astype(o_ref.dtype)

    return _call(
        body, name=name, grid=(N // tn, M // tm), out_shape=jax.ShapeDtypeStruct((M, N), out_dtype),
        in_specs=[pl.BlockSpec((K, tm), lambda n, m: (0, m)), pl.BlockSpec((K, tn), lambda n, m: (0, n))],
        out_specs=pl.BlockSpec((tm, tn), lambda n, m: (m, n)), args=(a, b),
        semantics=("parallel", "parallel"), jobs=jobs)


def _half_masks():
    lane = lax.broadcasted_iota(jnp.int32, (1, LANES), 1)
    return (lane < HEAD_DIM, lane >= HEAD_DIM)


def _swap_halves(v):
    return pltpu.roll(v.astype(F32), HEAD_DIM, 1).astype(v.dtype)


def _swa_geometry(n):
    qi = lax.broadcasted_iota(jnp.int32, (WINDOW, 2 * WINDOW), 0)
    kp = lax.broadcasted_iota(jnp.int32, (WINDOW, 2 * WINDOW), 1)
    dist = (WINDOW + qi) - kp
    valid = (dist >= 0) & (dist < WINDOW) & ((n > 0) | (kp >= WINDOW))
    return dist.astype(F32), valid


def _swa_slope(h):
    return 2.0 ** (-8.0 * (h + 1) / N_SWA_HEADS)


def _swa_softmax(qk, sink, slope, distf, valid):
    s = qk * ATT_SCALE - slope * distf
    s = jnp.where(valid, s, -1e30)
    m = jnp.maximum(jnp.max(s, axis=1, keepdims=True), sink)
    p = jnp.exp(s - m)
    e_sink = jnp.exp(sink - m)
    den = jnp.sum(p, axis=1, keepdims=True) + e_sink
    return p / den, e_sink / den


def _swa_group_heads(g):
    return [(2 * pp + a, pp, a) for pp in (2 * g, 2 * g + 1) for a in range(2)]


def _swa_specs(B, S):
    nb = S // WINDOW
    kcol = SWA_Q // SWA_KV
    cur = lambda b, n: (b * nb + n, kcol)
    prev = lambda b, n: (b * nb + jnp.maximum(n - 1, 0), kcol)
    curv = lambda b, n: (b * nb + n, kcol + 1)
    prevv = lambda b, n: (b * nb + jnp.maximum(n - 1, 0), kcol + 1)
    q_spec = pl.BlockSpec((WINDOW, SWA_Q), lambda b, n: (b * nb + n, 0))
    kv = [pl.BlockSpec((WINDOW, SWA_KV), f) for f in (prev, cur, prevv, curv)]
    sink_spec = pl.BlockSpec(memory_space=pltpu.SMEM)
    return nb, q_spec, kv, sink_spec


def _swa_kv_views(kp_ref, kc_ref, vp_ref, vc_ref, g):
    hm = _half_masks()
    c0 = (g // 2) * LANES
    k_all = jnp.concatenate([kp_ref[:, c0:c0 + LANES], kc_ref[:, c0:c0 + LANES]], axis=0)
    v_all = jnp.concatenate([vp_ref[:, c0:c0 + LANES], vc_ref[:, c0:c0 + LANES]], axis=0)
    b = g % 2
    ks, vs = [None, None], [None, None]
    ks[b], vs[b] = k_all, v_all
    ks[1 - b], vs[1 - b] = _swap_halves(k_all), _swap_halves(v_all)
    ks = [jnp.where(hm[a], ks[a], 0) for a in range(2)]
    vs = [jnp.where(hm[a], vs[a], 0) for a in range(2)]
    return ks, vs


def _swa_fwd(proj, sinks, B, S, name, jobs=()):
    T = B * S
    nb, q_spec, kv_specs, sink_spec = _swa_specs(B, S)

    def body(sink_ref, q_ref, kp_ref, kc_ref, vp_ref, vc_ref, y_ref):
        hm = _half_masks()
        distf, valid = _swa_geometry(pl.program_id(1))
        for g in range(N_SWA_KV):
            ks, vs = _swa_kv_views(kp_ref, kc_ref, vp_ref, vc_ref, g)
            heads = _swa_group_heads(g)
            qk = [_dot_nt(jnp.where(hm[a], q_ref[:, pp * LANES:(pp + 1) * LANES], 0), ks[a]) for _, pp, a in heads]
            p = [_swa_softmax(qk[i], sink_ref[0, h], _swa_slope(h), distf, valid)[0] for i, (h, _, _) in enumerate(heads)]
            o = [_dot(p[i].astype(_MXU_DTYPE), vs[a]) for i, (_, _, a) in enumerate(heads)]
            for j, pp in enumerate((2 * g, 2 * g + 1)):
                y_ref[:, pp * LANES:(pp + 1) * LANES] = o[2 * j] + o[2 * j + 1]

    return _call(
        body, name=name, grid=(B, nb), out_shape=jax.ShapeDtypeStruct((T, SWA_Q), F32),
        in_specs=[sink_spec, q_spec] + kv_specs,
        out_specs=pl.BlockSpec((WINDOW, SWA_Q), lambda b, n: (b * nb + n, 0)),
        args=(sinks, proj, proj, proj, proj, proj), semantics=("parallel", "parallel"), jobs=jobs)


def _swa_bwd(proj, sinks, dya, B, S, name, jobs=()):
    T = B * S
    nb, q_spec, kv_specs, sink_spec = _swa_specs(B, S)

    def body(sink_ref, q_ref, kp_ref, kc_ref, vp_ref, vc_ref, do_ref,
             dq_ref, dk_ref, dv_ref, dsink_ref, dk_acc, dv_acc):
        b_id, n = pl.program_id(0), pl.program_id(1)
        hm = _half_masks()
        lane = lax.broadcasted_iota(jnp.int32, (1, LANES), 1)

        @pl.when((b_id == 0) & (n == 0))
        def _():
            dsink_ref[...] = jnp.zeros_like(dsink_ref)

        @pl.when(n == 0)
        def _():
            dk_acc[...] = jnp.zeros_like(dk_acc)
            dv_acc[...] = jnp.zeros_like(dv_acc)

        distf, valid = _swa_geometry(n)
        r_prev = pl.multiple_of(jnp.maximum(n - 1, 0) * WINDOW, WINDOW)
        r_cur = pl.multiple_of(n * WINDOW, WINDOW)
        dsink = jnp.zeros((1, LANES), F32)
        for g in range(N_SWA_KV):
            ks, vs = _swa_kv_views(kp_ref, kc_ref, vp_ref, vc_ref, g)
            heads = _swa_group_heads(g)
            four = range(len(heads))
            qms = [jnp.where(hm[a], q_ref[:, pp * LANES:(pp + 1) * LANES], 0) for _, pp, a in heads]
            doms = [jnp.where(hm[a], do_ref[:, pp * LANES:(pp + 1) * LANES], 0) for _, pp, a in heads]
            qk = [_dot_nt(qms[i], ks[heads[i][2]]) for i in four]
            dp = [_dot_nt(doms[i], vs[heads[i][2]]) for i in four]
            soft = [_swa_softmax(qk[i], sink_ref[0, heads[i][0]], _swa_slope(heads[i][0]), distf, valid) for i in four]
            p = [soft[i][0] for i in four]
            delta = [jnp.sum(p[i] * dp[i], axis=1, keepdims=True) for i in four]
            ds = [(p[i] * (dp[i] - delta[i]) * ATT_SCALE).astype(_MXU_DTYPE) for i in four]
            for i in four:
                dsink = dsink + jnp.where(lane == heads[i][0], -jnp.sum(soft[i][1] * delta[i]), 0.0)
            dq = [_dot(ds[i], ks[heads[i][2]]) for i in four]
            dk_h = [_dot_tn(ds[i], qms[i]) for i in four]
            dv_h = [_dot_tn(p[i].astype(_MXU_DTYPE), doms[i]) for i in four]
            for j, pp in enumerate((2 * g, 2 * g + 1)):
                dq_ref[:, pp * LANES:(pp + 1) * LANES] = (dq[2 * j] + dq[2 * j + 1]).astype(dq_ref.dtype)
            dk_g = [dk_h[a] + dk_h[2 + a] for a in range(2)]
            dv_g = [dv_h[a] + dv_h[2 + a] for a in range(2)]
            bsel = g % 2
            dk_t = dk_g[bsel] + pltpu.roll(dk_g[1 - bsel], HEAD_DIM, 1)
            dv_t = dv_g[bsel] + pltpu.roll(dv_g[1 - bsel], HEAD_DIM, 1)
            c0 = (g // 2) * LANES
            dk_acc[pl.ds(r_prev, WINDOW), c0:c0 + LANES] += dk_t[:WINDOW]
            dk_acc[pl.ds(r_cur, WINDOW), c0:c0 + LANES] += dk_t[WINDOW:]
            dv_acc[pl.ds(r_prev, WINDOW), c0:c0 + LANES] += dv_t[:WINDOW]
            dv_acc[pl.ds(r_cur, WINDOW), c0:c0 + LANES] += dv_t[WINDOW:]
        dsink_ref[...] += dsink

        @pl.when(n == nb - 1)
        def _():
            dk_ref[...] = dk_acc[...].astype(dk_ref.dtype)
            dv_ref[...] = dv_acc[...].astype(dv_ref.dtype)

    seq_kv = pl.BlockSpec((S, SWA_KV), lambda b, n: (b, 0))
    return _call(
        body, name=name, grid=(B, nb),
        out_shape=(jax.ShapeDtypeStruct((T, SWA_Q), _MXU_DTYPE), jax.ShapeDtypeStruct((T, SWA_KV), _MXU_DTYPE),
                   jax.ShapeDtypeStruct((T, SWA_KV), _MXU_DTYPE), jax.ShapeDtypeStruct((1, LANES), F32)),
        in_specs=[sink_spec, q_spec] + kv_specs + [pl.BlockSpec((WINDOW, SWA_Q), lambda b, n: (b * nb + n, 0))],
        out_specs=(pl.BlockSpec((WINDOW, SWA_Q), lambda b, n: (b * nb + n, 0)), seq_kv, seq_kv,
                   pl.BlockSpec((1, LANES), lambda b, n: (0, 0))),
        scratch_shapes=[pltpu.VMEM((S, SWA_KV), F32), pltpu.VMEM((S, SWA_KV), F32)],
        args=(sinks, proj, proj, proj, proj, proj, dya), semantics=("arbitrary", "arbitrary"), jobs=jobs)


SB_TILE = 256
SB_HALF = 128
SB_DEAD = -105.0


def _mark_lanes():
    lane = lax.broadcasted_iota(jnp.int32, (1, LANES), 1)
    return (lane == HEAD_DIM - 1) | (lane == LANES - 1)


def _tri2(cond):
    j = lax.broadcasted_iota(jnp.int32, (2 * SB_HALF, SB_HALF), 0) & (SB_HALF - 1)
    s = lax.broadcasted_iota(jnp.int32, (2 * SB_HALF, SB_HALF), 1)
    return cond(j, s).astype(_MXU_DTYPE)


def _half_cumsums(x, tri2):
    out = []
    for h in range(2):
        xh = x[:, h * SB_HALF:(h + 1) * SB_HALF]
        hi = xh.astype(_MXU_DTYPE)
        lo = (xh - hi.astype(F32)).astype(_MXU_DTYPE)
        out.append(_dot(jnp.concatenate([hi, lo], axis=1), tri2))
    return out


def _log_sigmoid(z):
    return jnp.minimum(z, 0.0) - jnp.log(1.0 + jnp.exp(-jnp.abs(z)))


def _sb_specs(B, S):
    qb = (SWA_Q + 2 * SWA_KV) // LANES
    kb = qb + SB_W // LANES
    vb = kb + SB_W // LANES
    return [pl.BlockSpec((S, LANES), functools.partial(lambda b, p, c: (b, c + p), c=c)) for c in (qb, kb, vb)]


def _sb_fwd(proj, B, S, name, jobs=()):
    T = B * S
    tq = SB_TILE
    nq = S // tq

    def body(q_ref, k_ref, v_ref, y_ref, tot_ref):
        hm = _half_masks()
        ji = lax.broadcasted_iota(jnp.int32, (tq, tq), 0)
        si = lax.broadcasted_iota(jnp.int32, (tq, tq), 1)
        tri_after = _tri2(lambda j, s: j > s)
        causal = si < ji
        mark = _mark_lanes()

        def q_loop(qi, carry):
            r0 = pl.multiple_of(qi * tq, tq)
            q_pair = q_ref[pl.ds(r0, tq), :] * ATT_SCALE
            qms = [jnp.where(hm[a], q_pair, 0) for a in range(2)]

            def tile(c0, state, diagonal):
                kk = k_ref[pl.ds(c0, tq), :]
                vv = v_ref[pl.ds(c0, tq), :]
                two = range(2)
                z = [_dot_nt(qms[a], kk) for a in two]
                lb = [_log_sigmoid(z[a]) for a in two]
                l1m = [jnp.where(causal, lb[a] - z[a], 0.0) if diagonal else lb[a] - z[a] for a in two]
                cum = [_half_cumsums(l1m[a], tri_after) for a in two]
                tot = [[cum[a][h][:, 0:1] + l1m[a][:, h * SB_HALF:h * SB_HALF + 1] for h in two] for a in two]
                after = [jnp.concatenate([cum[a][0] + (state[a][1] + tot[a][1]), cum[a][1] + state[a][1]], axis=1)
                         for a in two]
                att = [jnp.exp(lb[a] + after[a]) for a in two]
                if diagonal:
                    att = [jnp.where(causal, att[a], 0.0) for a in two]
                acc = [state[a][0] + _dot(att[a].astype(_MXU_DTYPE), jnp.where(hm[a], vv, 0)) for a in two]
                car = [state[a][1] + (tot[a][0] + tot[a][1]) for a in two]
                return tuple((acc[a], car[a]) for a in two)

            def live(st):
                return jnp.maximum(jnp.max(st[0][1]), jnp.max(st[1][1])) > SB_DEAD

            def step(c):
                it, _, st = c
                st = tile(pl.multiple_of((qi - 1 - it) * tq, tq), st, False)
                return it + 1, live(st), st

            zero = (jnp.zeros((tq, LANES), F32), jnp.zeros((tq, 1), F32))
            state = tile(r0, (zero, zero), True)
            done, _, state = lax.while_loop(lambda c: (c[0] < qi) & c[1], step, (jnp.int32(0), live(state), state))
            y_ref[pl.ds(r0, tq), :] = state[0][0] + state[1][0]
            first = (qi - done).astype(F32)
            tot_ref[pl.ds(r0, tq), :] = jnp.where(mark, first, jnp.where(hm[0], state[0][1], state[1][1]))
            return carry

        lax.fori_loop(0, nq, q_loop, 0)

    out_spec = pl.BlockSpec((S, LANES), lambda b, p: (b, p))
    return _call(
        body, name=name, grid=(B, SB_W // LANES),
        out_shape=(jax.ShapeDtypeStruct((T, SB_W), F32), jax.ShapeDtypeStruct((T, SB_W), F32)),
        in_specs=_sb_specs(B, S), out_specs=(out_spec, out_spec), args=(proj, proj, proj),
        semantics=("parallel", "parallel"), jobs=jobs)


def _sb_bwd(proj, tot, dyb, B, S, name, jobs=()):
    T = B * S
    tq = SB_TILE
    nq = S // tq

    def body(q_ref, k_ref, v_ref, do_ref, tot_ref, dq_ref, dk_ref, dv_ref, dk_acc, dv_acc):
        hm = _half_masks()
        ji = lax.broadcasted_iota(jnp.int32, (tq, tq), 0)
        si = lax.broadcasted_iota(jnp.int32, (tq, tq), 1)
        tri_incl = _tri2(lambda j, s: j <= s)
        tri_excl = _tri2(lambda j, s: j < s)
        causal = si < ji
        mark = _mark_lanes()
        dk_acc[...] = jnp.zeros_like(dk_acc)
        dv_acc[...] = jnp.zeros_like(dv_acc)

        def q_loop(qi, carry):
            r0 = pl.multiple_of(qi * tq, tq)
            q_pair = q_ref[pl.ds(r0, tq), :] * ATT_SCALE
            do_pair = do_ref[pl.ds(r0, tq), :]
            tot_pair = tot_ref[pl.ds(r0, tq), :]
            qms = [jnp.where(hm[a], q_pair, 0) for a in range(2)]
            doms = [jnp.where(hm[a], do_pair, 0) for a in range(2)]
            totals = [jnp.max(jnp.where(hm[a] & ~mark, tot_pair, -jnp.inf), axis=1, keepdims=True) for a in range(2)]
            first = jnp.max(jnp.where(mark, tot_pair, -jnp.inf))
            first = jnp.where((first >= 0.0) & (first <= qi.astype(F32)), first, 0.0).astype(jnp.int32)

            def tile(c0, state, diagonal):
                kk = k_ref[pl.ds(c0, tq), :]
                vv = v_ref[pl.ds(c0, tq), :]
                ks = kk * ATT_SCALE
                two = range(2)
                last = SB_HALF - 1
                z = [_dot_nt(qms[a], kk) for a in two]
                d_att = [_dot_nt(doms[a], vv) for a in two]
                lb = [_log_sigmoid(z[a]) for a in two]
                l1m = [jnp.where(causal, lb[a] - z[a], 0.0) if diagonal else lb[a] - z[a] for a in two]
                cum = [_half_cumsums(l1m[a], tri_incl) for a in two]
                upto = [jnp.concatenate([cum[a][0] + state[a][1],
                                         cum[a][1] + (state[a][1] + cum[a][0][:, last:last + 1])], axis=1) for a in two]
                att = [jnp.exp(lb[a] + (totals[a] - upto[a])) for a in two]
                if diagonal:
                    att = [jnp.where(causal, att[a], 0.0) for a in two]
                d_log = [d_att[a] * att[a] for a in two]
                cumd = [_half_cumsums(d_log[a], tri_excl) for a in two]
                totd = [[cumd[a][h][:, last:last + 1] + d_log[a][:, h * SB_HALF + last:h * SB_HALF + last + 1]
                         for h in two] for a in two]
                before = [jnp.concatenate([cumd[a][0] + state[a][2], cumd[a][1] + (state[a][2] + totd[a][0])], axis=1)
                          for a in two]
                sig = [jnp.exp(lb[a]) for a in two]
                dz = [d_log[a] * (1.0 - sig[a]) - sig[a] * before[a] for a in two]
                if diagonal:
                    dz = [jnp.where(causal, dz[a], 0.0) for a in two]
                dzb = [dz[a].astype(_MXU_DTYPE) for a in two]
                dq = [state[a][0] + _dot(dzb[a], jnp.where(hm[a], ks, 0)) for a in two]
                dk_acc[pl.ds(c0, tq), :] += _dot_tn(dzb[0], qms[0]) + _dot_tn(dzb[1], qms[1])
                dv_acc[pl.ds(c0, tq), :] += (_dot_tn(att[0].astype(_MXU_DTYPE), doms[0])
                                             + _dot_tn(att[1].astype(_MXU_DTYPE), doms[1]))
                cp = [upto[a][:, tq - 1:tq] for a in two]
                cq = [state[a][2] + (totd[a][0] + totd[a][1]) for a in two]
                return tuple((dq[a], cp[a], cq[a]) for a in two)

            zero_col = jnp.zeros((tq, 1), F32)
            zero = (jnp.zeros((tq, LANES), F32), zero_col, zero_col)
            state = lax.fori_loop(first, qi, lambda kj, st: tile(pl.multiple_of(kj * tq, tq), st, False), (zero, zero))
            state = tile(r0, state, True)
            dq_ref[pl.ds(r0, tq), :] = (state[0][0] + state[1][0]).astype(dq_ref.dtype)
            return carry

        lax.fori_loop(0, nq, q_loop, 0)
        dk_ref[...] = dk_acc[...].astype(dk_ref.dtype)
        dv_ref[...] = dv_acc[...].astype(dv_ref.dtype)

    pair = pl.BlockSpec((S, LANES), lambda b, p: (b, p))
    out = jax.ShapeDtypeStruct((T, SB_W), _MXU_DTYPE)
    return _call(
        body, name=name, grid=(B, SB_W // LANES), out_shape=(out, out, out),
        in_specs=_sb_specs(B, S) + [pair, pair], out_specs=(pair, pair, pair),
        scratch_shapes=[pltpu.VMEM((S, LANES), F32), pltpu.VMEM((S, LANES), F32)],
        args=(proj, proj, proj, dyb, tot), semantics=("parallel", "parallel"), jobs=jobs)


def _layer_step(x, tgt, B, S, small, comm):
    run, big, part = comm.run, comm.big, comm.partial
    ffn1_w, ffn2_w = ("ffn1_down", "ffn1_gate", "ffn1_up"), ("ffn2_down", "ffn2_gate", "ffn2_up")

    h1 = run(_rms_fwd, x, small["ffn1_norm"], "ffn1_rms", ag=("ffn1_up",))
    U1 = run(_mm_nt, h1, big["ffn1_up"], _MXU_DTYPE, "ffn1_up", ag=("ffn1_gate",))
    G1, A1 = run(_ffn_gate, h1, big["ffn1_gate"], U1, "ffn1_gate", ag=("ffn1_down",))
    x1 = run(_mm_nn, [(A1, big["ffn1_down"])], x, 0.5, F32, "ffn1_down", ag=("w_in",))
    h2 = run(_rms_fwd, x1, small["mix_norm"], "mix_rms")
    proj = run(_mm_nt, h2, big["w_in"], _MXU_DTYPE, "in_proj", ag=("w_out",))
    ya = run(_swa_fwd, proj, small["swa_sinks"], B, S, "swa_fwd", ag=("ffn2_gate",))
    yb, tot = run(_sb_fwd, proj, B, S, "sb_fwd", ag=("ffn2_up",))
    yn = _outnorm_fwd(ya, yb, small["swa_out_norm"], small["sb_out_norm"], "out_norm")
    x2 = run(_mm_nn, [(yn, big["w_out"])], x1, 1.0, F32, "out_proj")
    h3 = run(_rms_fwd, x2, small["ffn2_norm"], "ffn2_rms")
    G2, U2, A2 = run(_ffn_gu, h3, big["ffn2_gate"], big["ffn2_up"], "ffn2_gate_up", ag=("ffn2_down",))
    x3 = run(_mm_nn, [(A2, big["ffn2_down"])], x2, 0.5, F32, "ffn2_down")

    dx3, dx3b, d_final, loss = _loss_head(x3, small["final_norm"], tgt, "loss_head")

    dG2, dU2 = run(_ffn_bwd_act, dx3b, big["ffn2_down"], G2, U2, "ffn2_bwd_act")
    part["ffn2_down"] = run(_mm_tn, A2, dx3b, 0.5, _WIRE_DTYPE, "ffn2_dw_down")
    part["ffn2_gate"] = run(_mm_tn, dG2, h3, 1.0, _WIRE_DTYPE, "ffn2_dw_gate")
    part["ffn2_up"] = run(_mm_tn, dU2, h3, 1.0, _WIRE_DTYPE, "ffn2_dw_up")
    dh3 = run(_mm_nn, [(dG2, big["ffn2_gate"]), (dU2, big["ffn2_up"])], None, 1.0, F32, "ffn2_dh", rs1=ffn2_w)
    dx2, dx2b, d_g2 = _rms_bwd(dh3, x2, small["ffn2_norm"], dx3, "ffn2_rms_bwd")

    part["w_out"] = run(_mm_tn, yn, dx2b, 1.0, _WIRE_DTYPE, "dw_out")
    dyn = run(_mm_nt, dx2b, big["w_out"], F32, "out_proj_bwd")
    dya, dyb, d_ga, d_gb = _outnorm_bwd(dyn, ya, yb, small["swa_out_norm"], small["sb_out_norm"], "out_norm_bwd")
    dqa, dka, dva, d_sinks = run(_swa_bwd, proj, small["swa_sinks"], dya, B, S, "swa_bwd", rs2=ffn2_w[:1])
    dqb, dkb, dvb = run(_sb_bwd, proj, tot, dyb, B, S, "sb_bwd", rs2=ffn2_w[1:])
    dproj = jnp.concatenate([dqa, dka, dva, dqb, dkb, dvb], axis=1)
    part["w_in"] = run(_mm_tn, dproj, h2, 1.0, _WIRE_DTYPE, "dw_in")
    dh2 = run(_mm_nn, [(dproj, big["w_in"])], None, 1.0, F32, "in_proj_bwd", rs1=("w_in", "w_out"))
    dx1, dx1b, d_gm = _rms_bwd(dh2, x1, small["mix_norm"], dx2, "mix_rms_bwd")

    dG1, dU1 = run(_ffn_bwd_act, dx1b, big["ffn1_down"], G1, U1, "ffn1_bwd_act", rs2=("w_in", "w_out"))
    part["ffn1_down"] = run(_mm_tn, A1, dx1b, 0.5, _WIRE_DTYPE, "ffn1_dw_down")
    part["ffn1_gate"] = run(_mm_tn, dG1, h1, 1.0, _WIRE_DTYPE, "ffn1_dw_gate", rs1=("ffn1_down",))
    part["ffn1_up"] = run(_mm_tn, dU1, h1, 1.0, _WIRE_DTYPE, "ffn1_dw_up", rs1=("ffn1_gate",), rs2=("ffn1_down",))
    dh1 = run(_mm_nn, [(dG1, big["ffn1_gate"])], None, 1.0, F32, "ffn1_dh_gate", rs1=("ffn1_up",), rs2=("ffn1_gate",))
    dh1 = run(_mm_nn, [(dU1, big["ffn1_up"])], dh1, 1.0, F32, "ffn1_dh_up", rs2=("ffn1_up",))
    gx, _, d_g1 = _rms_bwd(dh1, x, small["ffn1_norm"], dx1, "ffn1_rms_bwd")

    d_small = {"ffn1_norm": d_g1, "mix_norm": d_gm, "swa_sinks": d_sinks[:, :N_SWA_HEADS], "swa_out_norm": d_ga,
               "sb_out_norm": d_gb, "ffn2_norm": d_g2, "final_norm": d_final}
    return loss, gx, d_small


MESH = pl.DeviceIdType.MESH
BIG_NAMES = ("ffn1_gate", "ffn1_up", "ffn1_down", "w_in", "w_out", "ffn2_gate", "ffn2_up", "ffn2_down")
_COMM_PARAMS = pltpu.CompilerParams(has_side_effects=True)


def _place():
    x, y, c = lax.axis_index("x"), lax.axis_index("y"), lax.axis_index("c")
    other_chips = [(1 - x, y), (x, 1 - y), (1 - x, 1 - y)]
    return x, y, c, other_chips


def _padded_rows(rows):
    full = N_DEV * rows
    return -(-full // _F_TILE) * _F_TILE


AG_PARTS = 4


def _row_parts(rows, n):
    units = rows // 16
    assert units * 16 == rows and units >= n
    out, off = [], 0
    for i in range(n):
        size = (units // n + (1 if i < units % n else 0)) * 16
        out.append((off, size))
        off += size
    return out


def _ag_job(shards):
    nw = len(shards)
    D = shards[0].shape[1]
    rows_w = [s.shape[0] for s in shards]
    full_w = [_padded_rows(r) for r in rows_w]
    pad_w = [f - N_DEV * r for f, r in zip(full_w, rows_w)]
    max_pad = max(max(pad_w), 16)
    n_parts = AG_PARTS
    parts_w = [_row_parts(r, n_parts) for r in rows_w]

    class Plan:
        def __init__(self, ins, outs, scratch):
            zbuf, send_sems, recv_sems, local_sems, zero_sems = scratch
            x, y, c, chips = _place()
            me, sibling = (x, y, c), (x, y, 1 - c)

            def rows(w, block, part=None):
                off, size = (0, rows_w[w]) if part is None else part
                px, py, pc = block
                start = pl.multiple_of((4 * px + 2 * py + pc) * rows_w[w] + off, 16)
                return outs[w].at[pl.ds(start, size), :]

            def copy(w, k, block, to, part=None, own=False):
                src = rows(w, block, part)
                if own:
                    src = ins[w] if part is None else ins[w].at[pl.ds(part[0], part[1]), :]
                return pltpu.make_async_remote_copy(
                    src_ref=src, dst_ref=rows(w, block, part), send_sem=send_sems.at[w, k],
                    recv_sem=recv_sems.at[w, k], device_id=to, device_id_type=MESH)

            def k_ici(j, p):
                return 1 + j * n_parts + p

            def k_on(j, p):
                return 1 + (3 + j) * n_parts + p

            self.zbuf = zbuf
            self.local = [pltpu.make_async_copy(zbuf.at[pl.ds(0, pad_w[w]), :],
                                                outs[w].at[pl.ds(N_DEV * rows_w[w], pad_w[w]), :], zero_sems.at[w])
                          for w in range(nw) if pad_w[w]]
            self.local += [pltpu.make_async_copy(ins[w], rows(w, me), local_sems.at[w]) for w in range(nw)]
            self.first = [[copy(w, 0, me, sibling, own=True)]
                          + [copy(w, k_ici(j, p), me, (*chip, c), part, own=True)
                             for p, part in enumerate(parts_w[w]) for j, chip in enumerate(chips)]
                          for w in range(nw)]
            self.arrive = [[copy(w, k_ici(j, p), (*chip, c), me, part)
                            for p, part in enumerate(parts_w[w]) for j, chip in enumerate(chips)] for w in range(nw)]
            self.passed = [[copy(w, k_on(j, p), (*chip, c), sibling, part)
                            for p, part in enumerate(parts_w[w]) for j, chip in enumerate(chips)] for w in range(nw)]
            self.from_sibling = [[copy(w, 0, sibling, me)]
                                 + [copy(w, k_on(j, p), (*chip, 1 - c), me, part)
                                    for p, part in enumerate(parts_w[w]) for j, chip in enumerate(chips)]
                                 for w in range(nw)]

    def start(ins, outs, scratch):
        plan = Plan(ins, outs, scratch)
        plan.zbuf[...] = jnp.zeros_like(plan.zbuf)
        for cp in plan.local:
            cp.start()
        for w in range(nw):
            for cp in plan.first[w]:
                cp.start()

    def mid(ins, outs, scratch):
        plan = Plan(ins, outs, scratch)
        for w in range(nw):
            for arrived, onward in zip(plan.arrive[w], plan.passed[w]):
                arrived.wait_recv()
                onward.start()

    def finish(ins, outs, scratch):
        plan = Plan(ins, outs, scratch)
        for w in range(nw):
            for cp in plan.from_sibling[w]:
                cp.wait_recv()
        for w in range(nw):
            for cp in plan.first[w] + plan.passed[w]:
                cp.wait_send()
        for cp in plan.local:
            cp.wait()

    return _Job(
        ins=shards, out_shape=[jax.ShapeDtypeStruct((f, D), s.dtype) for f, s in zip(full_w, shards)],
        scratch=[pltpu.VMEM((max_pad, D), shards[0].dtype), pltpu.SemaphoreType.DMA((nw, 1 + 6 * n_parts)),
                 pltpu.SemaphoreType.DMA((nw, 1 + 6 * n_parts)), pltpu.SemaphoreType.DMA((nw,)),
                 pltpu.SemaphoreType.DMA((nw,))],
        start=start, mid=mid, finish=finish)


def _rs1_job(partials, rows_w):
    nw = len(partials)
    D = partials[0].shape[1]

    def copies(ins, outs, scratch):
        send_sems, recv_sems = scratch
        x, y, c, _ = _place()
        out = []
        for w in range(nw):
            r = rows_w[w]
            for q in range(4):
                src = ins[w].at[pl.ds(pl.multiple_of((2 * q + 1 - c) * r, 16), r), :]
                out.append(pltpu.make_async_remote_copy(
                    src_ref=src, dst_ref=outs[w].at[pl.ds(q * r, r), :], send_sem=send_sems.at[w, q],
                    recv_sem=recv_sems.at[w, q], device_id=(x, y, 1 - c), device_id_type=MESH))
        return out

    def start(ins, outs, scratch):
        for cp in copies(ins, outs, scratch):
            cp.start()

    def finish(ins, outs, scratch):
        for cp in copies(ins, outs, scratch):
            cp.wait()

    return _Job(
        ins=partials, out_shape=[jax.ShapeDtypeStruct((4 * r, D), p.dtype) for r, p in zip(rows_w, partials)],
        scratch=[pltpu.SemaphoreType.DMA((nw, 4)), pltpu.SemaphoreType.DMA((nw, 4))], start=start, finish=finish)


def _pair_sum(partial, from_sibling, rows, core, name):
    D = partial.shape[1]

    def body(core_ref, p_ref, s_ref, o_ref):
        o_ref[...] = (p_ref[...].astype(F32) + s_ref[...].astype(F32)).astype(o_ref.dtype)

    grid_spec = pltpu.PrefetchScalarGridSpec(
        num_scalar_prefetch=1, grid=(4,),
        in_specs=[pl.BlockSpec((rows, D), lambda q, core_ref: (2 * q + core_ref[0], 0)),
                  pl.BlockSpec((rows, D), lambda q, core_ref: (q, 0))],
        out_specs=pl.BlockSpec((rows, D), lambda q, core_ref: (q, 0)))
    return pl.pallas_call(
        body, name=name, grid_spec=grid_spec, out_shape=jax.ShapeDtypeStruct((4 * rows, D), partial.dtype),
        compiler_params=_params("arbitrary"),
    )(core, partial, from_sibling)


def _rs2_job(chip_sums, rows_w):
    nw = len(chip_sums)

    def copies(ins, outs, scratch):
        send_sems, recv_sems, local_sems = scratch
        x, y, c, chips = _place()
        my_chip = 2 * x + y
        out = []
        for w in range(nw):
            r = rows_w[w]
            mine = pl.ds(pl.multiple_of(my_chip * r, 16), r)
            out.append(pltpu.make_async_copy(ins[w].at[mine, :], outs[w].at[mine, :], local_sems.at[w]))
            for j, (qx, qy) in enumerate(chips):
                src = ins[w].at[pl.ds(pl.multiple_of((2 * qx + qy) * r, 16), r), :]
                out.append(pltpu.make_async_remote_copy(
                    src_ref=src, dst_ref=outs[w].at[mine, :], send_sem=send_sems.at[w, j],
                    recv_sem=recv_sems.at[w, j], device_id=(qx, qy, c), device_id_type=MESH))
        return out

    def start(ins, outs, scratch):
        for cp in copies(ins, outs, scratch):
            cp.start()

    def finish(ins, outs, scratch):
        for cp in copies(ins, outs, scratch):
            cp.wait()

    return _Job(
        ins=chip_sums, out_shape=[jax.ShapeDtypeStruct(s.shape, s.dtype) for s in chip_sums],
        scratch=[pltpu.SemaphoreType.DMA((nw, 3)), pltpu.SemaphoreType.DMA((nw, 3)), pltpu.SemaphoreType.DMA((nw,))],
        start=start, finish=finish)


class _Comm:
    def __init__(self, shards):
        self.shards = shards
        self.rows = {n: s.shape[0] for n, s in shards.items()}
        self.core = lax.axis_index("c").astype(jnp.int32).reshape(1)
        self.big, self.partial, self.chip_sums, self.slots = {}, {}, {}, {}

    def run(self, fn, *args, ag=(), rs1=(), rs2=()):
        jobs = []
        if ag:
            jobs.append(_ag_job([self.shards[n] for n in ag]))
        if rs1:
            jobs.append(_rs1_job([self.partial[n] for n in rs1], [self.rows[n] for n in rs1]))
        if rs2:
            jobs.append(_rs2_job([self.chip_sums[n] for n in rs2], [self.rows[n] for n in rs2]))
        out, job_res = fn(*args, jobs=jobs)
        job_res = iter(job_res)
        if ag:
            self.big.update(zip(ag, next(job_res)))
        if rs1:
            for n, got in zip(rs1, next(job_res)):
                self.chip_sums[n] = _pair_sum(self.partial[n], got, self.rows[n], self.core, "pair_sum_" + n)
        if rs2:
            self.slots.update(zip(rs2, next(job_res)))
        return out


SMALL_ROWS = 88


def _small_allreduce(vec):
    def body(v_ref, o_ref, gather, send_sems, recv_sems):
        x, y, c, _ = _place()
        my_id = 4 * x + 2 * y + c
        gather[my_id] = v_ref[...]
        copies = []
        for r in range(1, N_DEV):
            peer = (x ^ (r >> 2), y ^ ((r >> 1) & 1), c ^ (r & 1))
            cp = pltpu.make_async_remote_copy(src_ref=v_ref, dst_ref=gather.at[my_id], send_sem=send_sems.at[r - 1],
                                              recv_sem=recv_sems.at[r - 1], device_id=peer, device_id_type=MESH)
            cp.start()
            copies.append(cp)
        for cp in copies:
            cp.wait()
        acc = gather[0]
        for d in range(1, N_DEV):
            acc = acc + gather[d]
        o_ref[...] = acc

    vm = pl.BlockSpec(memory_space=pltpu.VMEM)
    return pl.pallas_call(
        body, name="small_allreduce", out_shape=jax.ShapeDtypeStruct(vec.shape, F32),
        in_specs=[vm], out_specs=vm,
        scratch_shapes=[pltpu.VMEM((N_DEV,) + vec.shape, F32), pltpu.SemaphoreType.DMA((N_DEV - 1,)),
                        pltpu.SemaphoreType.DMA((N_DEV - 1,))],
        compiler_params=_COMM_PARAMS,
    )(vec)


def _adamw_update(w, g, m, v):
    nm = ADAM_B1 * m + (1.0 - ADAM_B1) * g
    nv = ADAM_B2 * v + (1.0 - ADAM_B2) * jnp.square(g)
    m_hat = nm / (1.0 - ADAM_B1 ** ADAM_STEP)
    v_hat = nv / (1.0 - ADAM_B2 ** ADAM_STEP)
    return -ADAM_LR * (m_hat / (jnp.sqrt(v_hat) + ADAM_EPS) + ADAM_WD * w), nm, nv


def _adamw(w, g, m, v, name):
    R, C = w.shape
    tr = _tile(R, 256, 8)

    def body(w_ref, g_ref, m_ref, v_ref, d_ref, nm_ref, nv_ref):
        d_ref[...], nm_ref[...], nv_ref[...] = _adamw_update(w_ref[...], g_ref[...], m_ref[...], v_ref[...])

    spec = pl.BlockSpec((tr, C), lambda i: (i, 0))
    out = jax.ShapeDtypeStruct((R, C), F32)
    return pl.pallas_call(
        body, name=name, grid=(R // tr,), out_shape=(out, out, out),
        in_specs=[spec] * 4, out_specs=(spec, spec, spec),
        compiler_params=_params("parallel"),
    )(w, g, m, v)


def _adamw_slots(w, slots, m, v, name):
    R, C = w.shape
    tc = _tile(C, 512, LANES)

    def body(w_ref, s_ref, m_ref, v_ref, g_ref, d_ref, nm_ref, nv_ref):
        g = s_ref[0].astype(F32)
        for q in range(1, 4):
            g = g + s_ref[q].astype(F32)
        g_ref[...] = g
        d_ref[...], nm_ref[...], nv_ref[...] = _adamw_update(w_ref[...], g, m_ref[...], v_ref[...])

    spec = pl.BlockSpec((R, tc), lambda j: (0, j))
    out = jax.ShapeDtypeStruct((R, C), F32)
    return pl.pallas_call(
        body, name=name, grid=(C // tc,), out_shape=(out, out, out, out),
        in_specs=[spec, pl.BlockSpec((4, R, tc), lambda j: (0, 0, j)), spec, spec], out_specs=(spec, spec, spec, spec),
        compiler_params=_params("parallel"),
    )(w, slots, m, v)


WEIGHT_NAMES = ("ffn1_norm", "ffn1_w_gate", "ffn1_w_up", "ffn1_w_down", "mix_norm", "w_in", "swa_sinks",
                "swa_out_norm", "sb_out_norm", "w_out", "ffn2_norm", "ffn2_w_gate", "ffn2_w_up", "ffn2_w_down",
                "final_norm")
SMALL_NAMES = ("ffn1_norm", "mix_norm", "swa_sinks", "swa_out_norm", "sb_out_norm", "ffn2_norm", "final_norm")
BIG_ARGS = {"ffn1_gate": ("ffn1_w_gate", True), "ffn1_up": ("ffn1_w_up", True), "ffn1_down": ("ffn1_w_down", False),
            "w_in": ("w_in", True), "w_out": ("w_out", False), "ffn2_gate": ("ffn2_w_gate", True),
            "ffn2_up": ("ffn2_w_up", True), "ffn2_down": ("ffn2_w_down", False)}


def _pack_small(parts):
    padded = [jnp.pad(p.reshape(1, -1), ((0, 0), (0, -p.size % LANES))) for p in parts]
    flat = jnp.concatenate(padded, axis=1)
    flat = jnp.pad(flat, ((0, 0), (0, SMALL_ROWS * LANES - flat.shape[1])))
    return flat.reshape(SMALL_ROWS, LANES)


def _unpack_small(block, shapes):
    flat = block.reshape(-1)
    out, off = [], 0
    for shp in shapes:
        n = 1
        for s in shp:
            n *= s
        out.append(flat[off:off + n].reshape(shp))
        off += n + (-n % LANES)
    return out


def kernel(x, ffn1_norm, ffn1_w_gate, ffn1_w_up, ffn1_w_down, mix_norm, w_in, swa_sinks, swa_out_norm, sb_out_norm, w_out, ffn2_norm, ffn2_w_gate, ffn2_w_up, ffn2_w_down, final_norm, loss_target, m_ffn1_norm, m_ffn1_w_gate, m_ffn1_w_up, m_ffn1_w_down, m_mix_norm, m_w_in, m_swa_sinks, m_swa_out_norm, m_sb_out_norm, m_w_out, m_ffn2_norm, m_ffn2_w_gate, m_ffn2_w_up, m_ffn2_w_down, m_final_norm, v_ffn1_norm, v_ffn1_w_gate, v_ffn1_w_up, v_ffn1_w_down, v_mix_norm, v_w_in, v_swa_sinks, v_swa_out_norm, v_sb_out_norm, v_w_out, v_ffn2_norm, v_ffn2_w_gate, v_ffn2_w_up, v_ffn2_w_down, v_final_norm):
    args = dict(locals())
    B, S, D = x.shape
    T = B * S
    weights = {n: args[n] for n in WEIGHT_NAMES}
    mom_m = {n: args["m_" + n] for n in WEIGHT_NAMES}
    mom_v = {n: args["v_" + n] for n in WEIGHT_NAMES}

    shards = {}
    for name in BIG_NAMES:
        arg, transposed = BIG_ARGS[name]
        w2 = weights[arg][0]
        shards[name] = (w2.T if transposed else w2).astype(_WIRE_DTYPE)
    comm = _Comm(shards)
    small = {n: weights[n].reshape(1, -1) for n in SMALL_NAMES}

    loss, gx, d_small = _layer_step(x.reshape(T, D), loss_target.reshape(T, D), B, S, small, comm)

    small_shapes = [(1, 1)] + [d_small[n].shape for n in SMALL_NAMES]
    reduced = _small_allreduce(_pack_small([loss[:, :1]] + [d_small[n] for n in SMALL_NAMES]))
    red = _unpack_small(reduced, small_shapes)
    loss_out = red[0].reshape(())
    g_small = dict(zip(SMALL_NAMES, red[1:]))

    grads, deltas, new_m, new_v = {}, {}, {}, {}
    for name in BIG_NAMES:
        arg, transposed = BIG_ARGS[name]
        to_rows = (lambda t: t[0].T) if transposed else (lambda t: t[0])
        back = (lambda t: t.T[None]) if transposed else (lambda t: t[None])
        slots = comm.slots[name].reshape(4, comm.rows[name], D)
        res = _adamw_slots(to_rows(weights[arg]), slots, to_rows(mom_m[arg]), to_rows(mom_v[arg]), "adamw_" + name)
        grads[arg], deltas[arg], new_m[arg], new_v[arg] = [back(t) for t in res]
    shapes1 = [(1, weights[n].size) for n in SMALL_NAMES]
    packed = [_pack_small([t[n].reshape(1, -1) for n in SMALL_NAMES]) for t in (weights, g_small, mom_m, mom_v)]
    upd = _adamw(*packed, "adamw_small")
    for tgt_dict, block in zip((deltas, new_m, new_v), upd):
        for n, val in zip(SMALL_NAMES, _unpack_small(block, shapes1)):
            tgt_dict[n] = val.reshape(weights[n].shape)
    for n in SMALL_NAMES:
        grads[n] = g_small[n].reshape(weights[n].shape)

    return (loss_out, gx.reshape(B, S, D), *[grads[n] for n in WEIGHT_NAMES], *[deltas[n] for n in WEIGHT_NAMES],
            *[new_m[n] for n in WEIGHT_NAMES], *[new_v[n] for n in WEIGHT_NAMES])
```

```python
import functools

import jax
import jax.numpy as jnp
from jax import lax
from jax.experimental import pallas as pl
from jax.experimental.pallas import tpu as pltpu

F32 = jnp.float32
_MXU_DTYPE = jnp.bfloat16
_WIRE_DTYPE = jnp.bfloat16

EPS = 1e-6
HEAD_DIM = 64
N_SWA_HEADS = 16
N_SWA_KV = 4
N_SB_HEADS = 16
WINDOW = 128
SWA_Q = N_SWA_HEADS * HEAD_DIM
SWA_KV = N_SWA_KV * HEAD_DIM
SB_W = N_SB_HEADS * HEAD_DIM
IN_W = SWA_Q + 2 * SWA_KV + 3 * SB_W
LANES = 128
ATT_SCALE = HEAD_DIM ** -0.5

ADAM_LR = 0.001
ADAM_B1 = 0.9
ADAM_B2 = 0.999
ADAM_EPS = 1e-08
ADAM_WD = 0.01
ADAM_STEP = 10

N_DEV = 8
_VMEM_LIMIT_BYTES = 56 * 1024 * 1024
_F_TILE = 512


def _params(*semantics):
    return pltpu.CompilerParams(dimension_semantics=semantics, vmem_limit_bytes=_VMEM_LIMIT_BYTES)


def _tile(n, pref, align):
    t = min(n, pref)
    t -= t % align
    while t >= align:
        if n % t == 0:
            return t
        t -= align
    return n


def _dot(a, b):
    return lax.dot_general(a, b, (((1,), (0,)), ((), ())), preferred_element_type=F32)


def _dot_nt(a, b):
    return lax.dot_general(a, b, (((1,), (1,)), ((), ())), preferred_element_type=F32)


def _dot_tn(a, b):
    return lax.dot_general(a, b, (((0,), (0,)), ((), ())), preferred_element_type=F32)


class _Job:
    def __init__(self, ins, out_shape, scratch, start, finish, mid=None):
        self.ins, self.out_shape, self.scratch = list(ins), list(out_shape), list(scratch)
        self.start, self.mid, self.finish = start, mid, finish


_JOB_MID_FRACTION = 0.6


def _call(body, *, name, grid, in_specs, out_specs, out_shape, args, semantics, scratch_shapes=(), jobs=()):
    single = not isinstance(out_shape, (tuple, list))
    if not jobs:
        res = pl.pallas_call(body, name=name, grid=grid, in_specs=list(in_specs), out_specs=out_specs,
                             out_shape=out_shape, scratch_shapes=list(scratch_shapes),
                             compiler_params=_params(*semantics))(*args)
        return res, []
    base_out = [out_shape] if single else list(out_shape)
    base_out_specs = [out_specs] if single else list(out_specs)
    n_in, n_out, n_scr = len(args), len(base_out), len(scratch_shapes)
    any_spec = pl.BlockSpec(memory_space=pl.ANY)
    total = 1
    for g in grid:
        total *= g
    mid_step = min(total - 1, int(total * _JOB_MID_FRACTION))

    def wrapped(*refs):
        pos = n_in
        job_ins = []
        for job in jobs:
            job_ins.append(refs[pos:pos + len(job.ins)])
            pos += len(job.ins)
        outs = refs[pos:pos + n_out]
        pos += n_out
        job_outs = []
        for job in jobs:
            job_outs.append(refs[pos:pos + len(job.out_shape)])
            pos += len(job.out_shape)
        scr = refs[pos:pos + n_scr]
        pos += n_scr
        job_scr = []
        for job in jobs:
            job_scr.append(refs[pos:pos + len(job.scratch)])
            pos += len(job.scratch)
        step = pl.program_id(0)
        for d in range(1, len(grid)):
            step = step * grid[d] + pl.program_id(d)

        @pl.when(step == 0)
        def _():
            for job, ji, jo, js in zip(jobs, job_ins, job_outs, job_scr):
                job.start(ji, jo, js)

        @pl.when(step == mid_step)
        def _():
            for job, ji, jo, js in zip(jobs, job_ins, job_outs, job_scr):
                if job.mid is not None:
                    job.mid(ji, jo, js, 0)

        body(*refs[:n_in], *outs, *scr)

        @pl.when(step == total - 1)
        def _():
            for job, ji, jo, js in zip(jobs, job_ins, job_outs, job_scr):
                if job.mid is not None:
                    job.mid(ji, jo, js, 1)
            for job, ji, jo, js in zip(jobs, job_ins, job_outs, job_scr):
                job.finish(ji, jo, js)

    all_args = list(args) + [a for job in jobs for a in job.ins]
    all_in_specs = list(in_specs) + [any_spec for job in jobs for _ in job.ins]
    all_out_shape = base_out + [s for job in jobs for s in job.out_shape]
    all_out_specs = base_out_specs + [any_spec for job in jobs for _ in job.out_shape]
    all_scratch = list(scratch_shapes) + [s for job in jobs for s in job.scratch]
    res = pl.pallas_call(
        wrapped, name=name, grid=grid, in_specs=all_in_specs, out_specs=tuple(all_out_specs),
        out_shape=tuple(all_out_shape), scratch_shapes=all_scratch,
        compiler_params=pltpu.CompilerParams(dimension_semantics=("arbitrary",) * len(grid),
                                             vmem_limit_bytes=_VMEM_LIMIT_BYTES, has_side_effects=True),
    )(*all_args)
    base = res[0] if single else tuple(res[:n_out])
    job_res, pos = [], n_out
    for job in jobs:
        job_res.append(tuple(res[pos:pos + len(job.out_shape)]))
        pos += len(job.out_shape)
    return base, job_res


def _rms_fwd(x, g, name, jobs=()):
    T, D = x.shape
    tm = _tile(T, 512, 16)

    def body(x_ref, g_ref, o_ref):
        xv = x_ref[...]
        r = lax.rsqrt(jnp.mean(xv * xv, axis=-1, keepdims=True) + EPS)
        o_ref[...] = (xv * r * g_ref[...]).astype(o_ref.dtype)

    return _call(
        body, name=name, grid=(T // tm,),
        out_shape=jax.ShapeDtypeStruct((T, D), _MXU_DTYPE),
        in_specs=[pl.BlockSpec((tm, D), lambda i: (i, 0)), pl.BlockSpec((1, D), lambda i: (0, 0))],
        out_specs=pl.BlockSpec((tm, D), lambda i: (i, 0)), args=(x, g), semantics=("parallel",), jobs=jobs)


def _rms_bwd_rows(dh, xv, g):
    r = lax.rsqrt(jnp.mean(xv * xv, axis=-1, keepdims=True) + EPS)
    xhat = xv * r
    u = dh * g
    dx = r * (u - xhat * jnp.mean(u * xhat, axis=-1, keepdims=True))
    return dx, dh * xhat


def _rms_bwd(dh, x, g, dres, name):
    T, D = x.shape
    tm = _tile(T, 256, 16)

    def body(dh_ref, x_ref, g_ref, dres_ref, dx_ref, dxb_ref, dg_ref):
        @pl.when(pl.program_id(0) == 0)
        def _():
            dg_ref[...] = jnp.zeros_like(dg_ref)

        dx, dgr = _rms_bwd_rows(dh_ref[...], x_ref[...], g_ref[...])
        dx = dres_ref[...] + dx
        dx_ref[...] = dx
        dxb_ref[...] = dx.astype(dxb_ref.dtype)
        dg_ref[...] += jnp.sum(dgr, axis=0, keepdims=True)

    row = pl.BlockSpec((tm, D), lambda i: (i, 0))
    vec = pl.BlockSpec((1, D), lambda i: (0, 0))
    return pl.pallas_call(
        body, name=name, grid=(T // tm,),
        out_shape=(jax.ShapeDtypeStruct((T, D), F32), jax.ShapeDtypeStruct((T, D), _MXU_DTYPE),
                   jax.ShapeDtypeStruct((1, D), F32)),
        in_specs=[row, row, vec, row], out_specs=(row, row, vec),
        compiler_params=_params("arbitrary"),
    )(dh, x, g, dres)


def _loss_head(x, g, tgt, name):
    T, D = x.shape
    tm = _tile(T, 256, 16)

    def body(x_ref, g_ref, t_ref, dx_ref, dxb_ref, dg_ref, loss_ref):
        @pl.when(pl.program_id(0) == 0)
        def _():
            dg_ref[...] = jnp.zeros_like(dg_ref)
            loss_ref[...] = jnp.zeros_like(loss_ref)

        xv = x_ref[...]
        gv = g_ref[...]
        r = lax.rsqrt(jnp.mean(xv * xv, axis=-1, keepdims=True) + EPS)
        xhat = xv * r
        diff = xhat * gv - t_ref[...]
        tok = jnp.mean(diff * diff, axis=-1, keepdims=True)
        loss_ref[...] += 0.5 * jnp.sum(tok, axis=0, keepdims=True)
        dy = diff / D
        u = dy * gv
        dx = r * (u - xhat * jnp.mean(u * xhat, axis=-1, keepdims=True))
        dx_ref[...] = dx
        dxb_ref[...] = dx.astype(dxb_ref.dtype)
        dg_ref[...] += jnp.sum(dy * xhat, axis=0, keepdims=True)

    row = pl.BlockSpec((tm, D), lambda i: (i, 0))
    vec = pl.BlockSpec((1, D), lambda i: (0, 0))
    return pl.pallas_call(
        body, name=name, grid=(T // tm,),
        out_shape=(jax.ShapeDtypeStruct((T, D), F32), jax.ShapeDtypeStruct((T, D), _MXU_DTYPE),
                   jax.ShapeDtypeStruct((1, D), F32), jax.ShapeDtypeStruct((1, LANES), F32)),
        in_specs=[row, vec, row],
        out_specs=(row, row, vec, pl.BlockSpec((1, LANES), lambda i: (0, 0))),
        compiler_params=_params("arbitrary"),
    )(x, g, tgt)


def _outnorm_fwd(ya, yb, ga, gb, name):
    T, W = ya.shape
    tm = _tile(T, 512, 16)

    def body(ya_ref, yb_ref, ga_ref, gb_ref, o_ref):
        for k, (y_ref, g_ref) in enumerate(((ya_ref, ga_ref), (yb_ref, gb_ref))):
            yv = y_ref[...]
            r = lax.rsqrt(jnp.mean(yv * yv, axis=-1, keepdims=True) + EPS)
            o_ref[:, k * W:(k + 1) * W] = (yv * r * g_ref[...]).astype(o_ref.dtype)

    row = pl.BlockSpec((tm, W), lambda i: (i, 0))
    vec = pl.BlockSpec((1, W), lambda i: (0, 0))
    return pl.pallas_call(
        body, name=name, grid=(T // tm,),
        out_shape=jax.ShapeDtypeStruct((T, 2 * W), _MXU_DTYPE),
        in_specs=[row, row, vec, vec], out_specs=pl.BlockSpec((tm, 2 * W), lambda i: (i, 0)),
        compiler_params=_params("parallel"),
    )(ya, yb, ga, gb)


def _outnorm_bwd(dyn, ya, yb, ga, gb, name):
    T, W = ya.shape
    tm = _tile(T, 256, 16)

    def body(d_ref, ya_ref, yb_ref, ga_ref, gb_ref, dya_ref, dyb_ref, dga_ref, dgb_ref):
        @pl.when(pl.program_id(0) == 0)
        def _():
            dga_ref[...] = jnp.zeros_like(dga_ref)
            dgb_ref[...] = jnp.zeros_like(dgb_ref)

        for k, (y_ref, g_ref, dy_ref, dg_ref) in enumerate(
                ((ya_ref, ga_ref, dya_ref, dga_ref), (yb_ref, gb_ref, dyb_ref, dgb_ref))):
            dy, dgr = _rms_bwd_rows(d_ref[:, k * W:(k + 1) * W], y_ref[...], g_ref[...])
            dy_ref[...] = dy.astype(dy_ref.dtype)
            dg_ref[...] += jnp.sum(dgr, axis=0, keepdims=True)

    row = pl.BlockSpec((tm, W), lambda i: (i, 0))
    vec = pl.BlockSpec((1, W), lambda i: (0, 0))
    return pl.pallas_call(
        body, name=name, grid=(T // tm,),
        out_shape=(jax.ShapeDtypeStruct((T, W), _MXU_DTYPE), jax.ShapeDtypeStruct((T, W), _MXU_DTYPE),
                   jax.ShapeDtypeStruct((1, W), F32), jax.ShapeDtypeStruct((1, W), F32)),
        in_specs=[pl.BlockSpec((tm, 2 * W), lambda i: (i, 0)), row, row, vec, vec],
        out_specs=(row, row, vec, vec),
        compiler_params=_params("arbitrary"),
    )(dyn, ya, yb, ga, gb)


_STRIP_ROWS = 256


def _strips(rows):
    step = min(rows, _STRIP_ROWS)
    return [slice(r, r + step) for r in range(0, rows, step)]


def _ffn_gu(h, wg_t, wu_t, name, jobs=()):
    T, D = h.shape
    Fp = wg_t.shape[0]
    tm = _tile(T, 1024, 16)
    tn = _tile(Fp, _F_TILE, LANES)

    def body(h_ref, wg_ref, wu_ref, g_ref, u_ref, a_ref):
        for rows in _strips(tm):
            hv = h_ref[rows, :]
            g = _dot_nt(hv, wg_ref[...])
            u = _dot_nt(hv, wu_ref[...])
            g_ref[rows, :] = g.astype(g_ref.dtype)
            u_ref[rows, :] = u.astype(u_ref.dtype)
            a_ref[rows, :] = (g * jax.nn.sigmoid(g) * u).astype(a_ref.dtype)

    act = pl.BlockSpec((tm, tn), lambda n, m: (m, n))
    wsp = pl.BlockSpec((tn, D), lambda n, m: (n, 0))
    out = jax.ShapeDtypeStruct((T, Fp), _MXU_DTYPE)
    return _call(
        body, name=name, grid=(Fp // tn, T // tm), out_shape=(out, out, out),
        in_specs=[pl.BlockSpec((tm, D), lambda n, m: (m, 0)), wsp, wsp],
        out_specs=(act, act, act), args=(h, wg_t, wu_t), semantics=("parallel", "parallel"), jobs=jobs)


def _ffn_bwd_act(dxb, wd, G, U, name, jobs=()):
    T, D = dxb.shape
    Fp = wd.shape[0]
    tm = _tile(T, 1024, 16)
    tn = _tile(Fp, _F_TILE, LANES)

    def body(e_ref, wd_ref, g_ref, u_ref, dg_ref, du_ref):
        for rows in _strips(tm):
            da = 0.5 * _dot_nt(e_ref[rows, :], wd_ref[...])
            g = g_ref[rows, :].astype(F32)
            u = u_ref[rows, :].astype(F32)
            s = jax.nn.sigmoid(g)
            du_ref[rows, :] = (da * (g * s)).astype(du_ref.dtype)
            dg_ref[rows, :] = (da * u * (s * (1.0 + g * (1.0 - s)))).astype(dg_ref.dtype)

    act = pl.BlockSpec((tm, tn), lambda m, n: (m, n))
    out = jax.ShapeDtypeStruct((T, Fp), _MXU_DTYPE)
    return _call(
        body, name=name, grid=(T // tm, Fp // tn), out_shape=(out, out),
        in_specs=[pl.BlockSpec((tm, D), lambda m, n: (m, 0)), pl.BlockSpec((tn, D), lambda m, n: (n, 0)),
                  act, act],
        out_specs=(act, act), args=(dxb, wd, G, U), semantics=("parallel", "parallel"), jobs=jobs)


def _ffn_gate(h, wg_t, U, name, jobs=()):
    T, D = h.shape
    Fp = wg_t.shape[0]
    tm = _tile(T, 1024, 16)
    tn = _tile(Fp, _F_TILE, LANES)

    def body(h_ref, wg_ref, u_ref, g_ref, a_ref):
        for rows in _strips(tm):
            g = _dot_nt(h_ref[rows, :], wg_ref[...])
            g_ref[rows, :] = g.astype(g_ref.dtype)
            a_ref[rows, :] = (g * jax.nn.sigmoid(g) * u_ref[rows, :].astype(F32)).astype(a_ref.dtype)

    act = pl.BlockSpec((tm, tn), lambda m, n: (m, n))
    out = jax.ShapeDtypeStruct((T, Fp), _MXU_DTYPE)
    return _call(
        body, name=name, grid=(T // tm, Fp // tn), out_shape=(out, out),
        in_specs=[pl.BlockSpec((tm, D), lambda m, n: (m, 0)), pl.BlockSpec((tn, D), lambda m, n: (n, 0)), act],
        out_specs=(act, act), args=(h, wg_t, U), semantics=("parallel", "parallel"), jobs=jobs)


def _mm_nt(a, b, out_dtype, name, jobs=()):
    M, K = a.shape
    N = b.shape[0]
    tm = _tile(M, 1024, 16)
    tn = _tile(N, 512, LANES)

    def body(a_ref, b_ref, o_ref):
        o_ref[...] = _dot_nt(a_ref[...], b_ref[...]).astype(o_ref.dtype)

    return _call(
        body, name=name, grid=(M // tm, N // tn), out_shape=jax.ShapeDtypeStruct((M, N), out_dtype),
        in_specs=[pl.BlockSpec((tm, K), lambda m, n: (m, 0)), pl.BlockSpec((tn, K), lambda m, n: (n, 0))],
        out_specs=pl.BlockSpec((tm, tn), lambda m, n: (m, n)), args=(a, b),
        semantics=("parallel", "parallel"), jobs=jobs)


_MM_OPERAND_BYTES = 26 * 1024 * 1024


def _k_tile(K, bytes_per_k, align):
    best = align
    for t in range(align, K + 1, align):
        if K % t == 0 and 2 * t * bytes_per_k <= _MM_OPERAND_BYTES:
            best = t
    return best


def _mm_nn(pairs, res, alpha, out_dtype, name, jobs=()):
    M, K = pairs[0][0].shape
    N = pairs[0][1].shape[1]
    n_pairs = len(pairs)
    tm = _tile(M, 1024, 16)
    tn = _tile(N, 1024, LANES)
    tk = _k_tile(K, n_pairs * (tm + tn) * pairs[0][0].dtype.itemsize, LANES)
    nk = K // tk

    def body(*refs):
        ab = refs[:2 * n_pairs]
        res_ref = refs[2 * n_pairs] if res is not None else None
        o_ref = refs[2 * n_pairs + (res is not None)]

        def finish(acc):
            out = alpha * acc
            if res_ref is not None:
                out = res_ref[...] + out
            o_ref[...] = out.astype(o_ref.dtype)

        part = _dot(ab[0][...], ab[1][...])
        for i in range(1, n_pairs):
            part = part + _dot(ab[2 * i][...], ab[2 * i + 1][...])
        if nk == 1:
            finish(part)
        else:
            acc_ref = refs[-1]
            k = pl.program_id(2)

            @pl.when(k == 0)
            def _():
                acc_ref[...] = part

            @pl.when(k > 0)
            def _():
                acc_ref[...] += part

            @pl.when(k == nk - 1)
            def _():
                finish(acc_ref[...])

    in_specs, args = [], []
    for a, b in pairs:
        in_specs += [pl.BlockSpec((tm, tk), lambda m, n, k: (m, k)), pl.BlockSpec((tk, tn), lambda m, n, k: (k, n))]
        args += [a, b]
    if res is not None:
        in_specs.append(pl.BlockSpec((tm, tn), lambda m, n, k: (m, n)))
        args.append(res)
    return _call(
        body, name=name, grid=(M // tm, N // tn, nk), out_shape=jax.ShapeDtypeStruct((M, N), out_dtype),
        in_specs=in_specs, out_specs=pl.BlockSpec((tm, tn), lambda m, n, k: (m, n)),
        scratch_shapes=[pltpu.VMEM((tm, tn), F32)] if nk > 1 else [], args=args,
        semantics=("parallel", "parallel", "arbitrary"), jobs=jobs)


def _mm_tn(a, b, alpha, out_dtype, name, jobs=()):
    K, M = a.shape
    N = b.shape[1]
    tm = _tile(M, 512, LANES)
    tn = _tile(N, 1024, LANES)

    def body(a_ref, b_ref, o_ref):
        o_ref[...] = (alpha * _dot_tn(a_ref[...], b_ref[...])).astype(o_ref.dtype)

    return _call(
        body, name=name, grid=(N // tn, M // tm), out_shape=jax.ShapeDtypeStruct((M, N), out_dtype),
        in_specs=[pl.BlockSpec((K, tm), lambda n, m: (0, m)), pl.BlockSpec((K, tn), lambda n, m: (0, n))],
        out_specs=pl.BlockSpec((tm, tn), lambda n, m: (m, n)), args=(a, b),
        semantics=("parallel", "parallel"), jobs=jobs)


def _half_masks():
    lane = lax.broadcasted_iota(jnp.int32, (1, LANES), 1)
    return (lane < HEAD_DIM, lane >= HEAD_DIM)


def _swap_halves(v):
    return pltpu.roll(v.astype(F32), HEAD_DIM, 1).astype(v.dtype)


def _swa_geometry(n):
    qi = lax.broadcasted_iota(jnp.int32, (WINDOW, 2 * WINDOW), 0)
    kp = lax.broadcasted_iota(jnp.int32, (WINDOW, 2 * WINDOW), 1)
    dist = (WINDOW + qi) - kp
    valid = (dist >= 0) & (dist < WINDOW) & ((n > 0) | (kp >= WINDOW))
    return dist.astype(F32), valid


def _swa_slope(h):
    return 2.0 ** (-8.0 * (h + 1) / N_SWA_HEADS)


def _swa_softmax(qk, sink, slope, distf, valid):
    s = qk * ATT_SCALE - slope * distf
    s = jnp.where(valid, s, -1e30)
    m = jnp.maximum(jnp.max(s, axis=1, keepdims=True), sink)
    p = jnp.exp(s - m)
    e_sink = jnp.exp(sink - m)
    den = jnp.sum(p, axis=1, keepdims=True) + e_sink
    return p / den, e_sink / den


def _swa_group_heads(g):
    return [(2 * pp + a, pp, a) for pp in (2 * g, 2 * g + 1) for a in range(2)]


def _swa_specs(B, S):
    nb = S // WINDOW
    kcol = SWA_Q // SWA_KV
    cur = lambda b, n: (b * nb + n, kcol)
    prev = lambda b, n: (b * nb + jnp.maximum(n - 1, 0), kcol)
    curv = lambda b, n: (b * nb + n, kcol + 1)
    prevv = lambda b, n: (b * nb + jnp.maximum(n - 1, 0), kcol + 1)
    q_spec = pl.BlockSpec((WINDOW, SWA_Q), lambda b, n: (b * nb + n, 0))
    kv = [pl.BlockSpec((WINDOW, SWA_KV), f) for f in (prev, cur, prevv, curv)]
    sink_spec = pl.BlockSpec(memory_space=pltpu.SMEM)
    return nb, q_spec, kv, sink_spec


def _swa_kv_views(kp_ref, kc_ref, vp_ref, vc_ref, g):
    hm = _half_masks()
    c0 = (g // 2) * LANES
    k_all = jnp.concatenate([kp_ref[:, c0:c0 + LANES], kc_ref[:, c0:c0 + LANES]], axis=0)
    v_all = jnp.concatenate([vp_ref[:, c0:c0 + LANES], vc_ref[:, c0:c0 + LANES]], axis=0)
    b = g % 2
    ks, vs = [None, None], [None, None]
    ks[b], vs[b] = k_all, v_all
    ks[1 - b], vs[1 - b] = _swap_halves(k_all), _swap_halves(v_all)
    ks = [jnp.where(hm[a], ks[a], 0) for a in range(2)]
    vs = [jnp.where(hm[a], vs[a], 0) for a in range(2)]
    return ks, vs


def _swa_fwd(proj, sinks, B, S, name, jobs=()):
    T = B * S
    nb, q_spec, kv_specs, sink_spec = _swa_specs(B, S)

    def body(sink_ref, q_ref, kp_ref, kc_ref, vp_ref, vc_ref, y_ref):
        hm = _half_masks()
        distf, valid = _swa_geometry(pl.program_id(1))
        for g in range(N_SWA_KV):
            ks, vs = _swa_kv_views(kp_ref, kc_ref, vp_ref, vc_ref, g)
            heads = _swa_group_heads(g)
            qk = [_dot_nt(jnp.where(hm[a], q_ref[:, pp * LANES:(pp + 1) * LANES], 0), ks[a]) for _, pp, a in heads]
            p = [_swa_softmax(qk[i], sink_ref[0, h], _swa_slope(h), distf, valid)[0] for i, (h, _, _) in enumerate(heads)]
            o = [_dot(p[i].astype(_MXU_DTYPE), vs[a]) for i, (_, _, a) in enumerate(heads)]
            for j, pp in enumerate((2 * g, 2 * g + 1)):
                y_ref[:, pp * LANES:(pp + 1) * LANES] = o[2 * j] + o[2 * j + 1]

    return _call(
        body, name=name, grid=(B, nb), out_shape=jax.ShapeDtypeStruct((T, SWA_Q), F32),
        in_specs=[sink_spec, q_spec] + kv_specs,
        out_specs=pl.BlockSpec((WINDOW, SWA_Q), lambda b, n: (b * nb + n, 0)),
        args=(sinks, proj, proj, proj, proj, proj), semantics=("parallel", "parallel"), jobs=jobs)


def _swa_bwd(proj, sinks, dya, B, S, name, jobs=()):
    T = B * S
    nb, q_spec, kv_specs, sink_spec = _swa_specs(B, S)

    def body(sink_ref, q_ref, kp_ref, kc_ref, vp_ref, vc_ref, do_ref,
             dq_ref, dk_ref, dv_ref, dsink_ref, dk_acc, dv_acc):
        b_id, n = pl.program_id(0), pl.program_id(1)
        hm = _half_masks()
        lane = lax.broadcasted_iota(jnp.int32, (1, LANES), 1)

        @pl.when((b_id == 0) & (n == 0))
        def _():
            dsink_ref[...] = jnp.zeros_like(dsink_ref)

        @pl.when(n == 0)
        def _():
            dk_acc[...] = jnp.zeros_like(dk_acc)
            dv_acc[...] = jnp.zeros_like(dv_acc)

        distf, valid = _swa_geometry(n)
        r_prev = pl.multiple_of(jnp.maximum(n - 1, 0) * WINDOW, WINDOW)
        r_cur = pl.multiple_of(n * WINDOW, WINDOW)
        dsink = jnp.zeros((1, LANES), F32)
        for g in range(N_SWA_KV):
            ks, vs = _swa_kv_views(kp_ref, kc_ref, vp_ref, vc_ref, g)
            heads = _swa_group_heads(g)
            four = range(len(heads))
            qms = [jnp.where(hm[a], q_ref[:, pp * LANES:(pp + 1) * LANES], 0) for _, pp, a in heads]
            doms = [jnp.where(hm[a], do_ref[:, pp * LANES:(pp + 1) * LANES], 0) for _, pp, a in heads]
            qk = [_dot_nt(qms[i], ks[heads[i][2]]) for i in four]
            dp = [_dot_nt(doms[i], vs[heads[i][2]]) for i in four]
            soft = [_swa_softmax(qk[i], sink_ref[0, heads[i][0]], _swa_slope(heads[i][0]), distf, valid) for i in four]
            p = [soft[i][0] for i in four]
            delta = [jnp.sum(p[i] * dp[i], axis=1, keepdims=True) for i in four]
            ds = [(p[i] * (dp[i] - delta[i]) * ATT_SCALE).astype(_MXU_DTYPE) for i in four]
            for i in four:
                dsink = dsink + jnp.where(lane == heads[i][0], -jnp.sum(soft[i][1] * delta[i]), 0.0)
            dq = [_dot(ds[i], ks[heads[i][2]]) for i in four]
            dk_h = [_dot_tn(ds[i], qms[i]) for i in four]
            dv_h = [_dot_tn(p[i].astype(_MXU_DTYPE), doms[i]) for i in four]
            for j, pp in enumerate((2 * g, 2 * g + 1)):
                dq_ref[:, pp * LANES:(pp + 1) * LANES] = (dq[2 * j] + dq[2 * j + 1]).astype(dq_ref.dtype)
            dk_g = [dk_h[a] + dk_h[2 + a] for a in range(2)]
            dv_g = [dv_h[a] + dv_h[2 + a] for a in range(2)]
            bsel = g % 2
            dk_t = dk_g[bsel] + pltpu.roll(dk_g[1 - bsel], HEAD_DIM, 1)
            dv_t = dv_g[bsel] + pltpu.roll(dv_g[1 - bsel], HEAD_DIM, 1)
            c0 = (g // 2) * LANES
            dk_acc[pl.ds(r_prev, WINDOW), c0:c0 + LANES] += dk_t[:WINDOW]
            dk_acc[pl.ds(r_cur, WINDOW), c0:c0 + LANES] += dk_t[WINDOW:]
            dv_acc[pl.ds(r_prev, WINDOW), c0:c0 + LANES] += dv_t[:WINDOW]
            dv_acc[pl.ds(r_cur, WINDOW), c0:c0 + LANES] += dv_t[WINDOW:]
        dsink_ref[...] += dsink

        @pl.when(n == nb - 1)
        def _():
            dk_ref[...] = dk_acc[...].astype(dk_ref.dtype)
            dv_ref[...] = dv_acc[...].astype(dv_ref.dtype)

    seq_kv = pl.BlockSpec((S, SWA_KV), lambda b, n: (b, 0))
    return _call(
        body, name=name, grid=(B, nb),
        out_shape=(jax.ShapeDtypeStruct((T, SWA_Q), _MXU_DTYPE), jax.ShapeDtypeStruct((T, SWA_KV), _MXU_DTYPE),
                   jax.ShapeDtypeStruct((T, SWA_KV), _MXU_DTYPE), jax.ShapeDtypeStruct((1, LANES), F32)),
        in_specs=[sink_spec, q_spec] + kv_specs + [pl.BlockSpec((WINDOW, SWA_Q), lambda b, n: (b * nb + n, 0))],
        out_specs=(pl.BlockSpec((WINDOW, SWA_Q), lambda b, n: (b * nb + n, 0)), seq_kv, seq_kv,
                   pl.BlockSpec((1, LANES), lambda b, n: (0, 0))),
        scratch_shapes=[pltpu.VMEM((S, SWA_KV), F32), pltpu.VMEM((S, SWA_KV), F32)],
        args=(sinks, proj, proj, proj, proj, proj, dya), semantics=("arbitrary", "arbitrary"), jobs=jobs)


SB_TILE = 256
SB_HALF = 128
SB_DEAD = -105.0


def _mark_lanes():
    lane = lax.broadcasted_iota(jnp.int32, (1, LANES), 1)
    return (lane == HEAD_DIM - 1) | (lane == LANES - 1)


def _tri2(cond):
    j = lax.broadcasted_iota(jnp.int32, (2 * SB_HALF, SB_HALF), 0) & (SB_HALF - 1)
    s = lax.broadcasted_iota(jnp.int32, (2 * SB_HALF, SB_HALF), 1)
    return cond(j, s).astype(_MXU_DTYPE)


def _half_cumsums(x, tri2):
    out = []
    for h in range(2):
        xh = x[:, h * SB_HALF:(h + 1) * SB_HALF]
        hi = xh.astype(_MXU_DTYPE)
        lo = (xh - hi.astype(F32)).astype(_MXU_DTYPE)
        out.append(_dot(jnp.concatenate([hi, lo], axis=1), tri2))
    return out


def _log_sigmoid(z):
    return jnp.minimum(z, 0.0) - jnp.log(1.0 + jnp.exp(-jnp.abs(z)))


def _sb_specs(B, S):
    qb = (SWA_Q + 2 * SWA_KV) // LANES
    kb = qb + SB_W // LANES
    vb = kb + SB_W // LANES
    return [pl.BlockSpec((S, LANES), functools.partial(lambda b, p, c: (b, c + p), c=c)) for c in (qb, kb, vb)]


def _sb_fwd(proj, B, S, name, jobs=()):
    T = B * S
    tq = SB_TILE
    nq = S // tq

    def body(q_ref, k_ref, v_ref, y_ref, tot_ref):
        hm = _half_masks()
        ji = lax.broadcasted_iota(jnp.int32, (tq, tq), 0)
        si = lax.broadcasted_iota(jnp.int32, (tq, tq), 1)
        tri_after = _tri2(lambda j, s: j > s)
        causal = si < ji
        mark = _mark_lanes()

        def q_loop(qi, carry):
            r0 = pl.multiple_of(qi * tq, tq)
            q_pair = q_ref[pl.ds(r0, tq), :] * ATT_SCALE
            qms = [jnp.where(hm[a], q_pair, 0) for a in range(2)]

            def tile(c0, state, diagonal):
                kk = k_ref[pl.ds(c0, tq), :]
                vv = v_ref[pl.ds(c0, tq), :]
                two = range(2)
                z = [_dot_nt(qms[a], kk) for a in two]
                lb = [_log_sigmoid(z[a]) for a in two]
                l1m = [jnp.where(causal, lb[a] - z[a], 0.0) if diagonal else lb[a] - z[a] for a in two]
                cum = [_half_cumsums(l1m[a], tri_after) for a in two]
                tot = [[cum[a][h][:, 0:1] + l1m[a][:, h * SB_HALF:h * SB_HALF + 1] for h in two] for a in two]
                after = [jnp.concatenate([cum[a][0] + (state[a][1] + tot[a][1]), cum[a][1] + state[a][1]], axis=1)
                         for a in two]
                att = [jnp.exp(lb[a] + after[a]) for a in two]
                if diagonal:
                    att = [jnp.where(causal, att[a], 0.0) for a in two]
                acc = [state[a][0] + _dot(att[a].astype(_MXU_DTYPE), jnp.where(hm[a], vv, 0)) for a in two]
                car = [state[a][1] + (tot[a][0] + tot[a][1]) for a in two]
                return tuple((acc[a], car[a]) for a in two)

            def live(st):
                return jnp.maximum(jnp.max(st[0][1]), jnp.max(st[1][1])) > SB_DEAD

            def step(c):
                it, _, st = c
                st = tile(pl.multiple_of((qi - 1 - it) * tq, tq), st, False)
                return it + 1, live(st), st

            zero = (jnp.zeros((tq, LANES), F32), jnp.zeros((tq, 1), F32))
            state = tile(r0, (zero, zero), True)
            done, _, state = lax.while_loop(lambda c: (c[0] < qi) & c[1], step, (jnp.int32(0), live(state), state))
            y_ref[pl.ds(r0, tq), :] = state[0][0] + state[1][0]
            first = (qi - done).astype(F32)
            tot_ref[pl.ds(r0, tq), :] = jnp.where(mark, first, jnp.where(hm[0], state[0][1], state[1][1]))
            return carry

        lax.fori_loop(0, nq, q_loop, 0)

    out_spec = pl.BlockSpec((S, LANES), lambda b, p: (b, p))
    return _call(
        body, name=name, grid=(B, SB_W // LANES),
        out_shape=(jax.ShapeDtypeStruct((T, SB_W), F32), jax.ShapeDtypeStruct((T, SB_W), F32)),
        in_specs=_sb_specs(B, S), out_specs=(out_spec, out_spec), args=(proj, proj, proj),
        semantics=("parallel", "parallel"), jobs=jobs)


def _sb_bwd(proj, tot, dyb, B, S, name, jobs=()):
    T = B * S
    tq = SB_TILE
    nq = S // tq

    def body(q_ref, k_ref, v_ref, do_ref, tot_ref, dq_ref, dk_ref, dv_ref, dk_acc, dv_acc):
        hm = _half_masks()
        ji = lax.broadcasted_iota(jnp.int32, (tq, tq), 0)
        si = lax.broadcasted_iota(jnp.int32, (tq, tq), 1)
        tri_incl = _tri2(lambda j, s: j <= s)
        tri_excl = _tri2(lambda j, s: j < s)
        causal = si < ji
        mark = _mark_lanes()
        dk_acc[...] = jnp.zeros_like(dk_acc)
        dv_acc[...] = jnp.zeros_like(dv_acc)

        def q_loop(qi, carry):
            r0 = pl.multiple_of(qi * tq, tq)
            q_pair = q_ref[pl.ds(r0, tq), :] * ATT_SCALE
            do_pair = do_ref[pl.ds(r0, tq), :]
            tot_pair = tot_ref[pl.ds(r0, tq), :]
            qms = [jnp.where(hm[a], q_pair, 0) for a in range(2)]
            doms = [jnp.where(hm[a], do_pair, 0) for a in range(2)]
            totals = [jnp.max(jnp.where(hm[a] & ~mark, tot_pair, -jnp.inf), axis=1, keepdims=True) for a in range(2)]
            first = jnp.max(jnp.where(mark, tot_pair, -jnp.inf))
            first = jnp.where((first >= 0.0) & (first <= qi.astype(F32)), first, 0.0).astype(jnp.int32)

            def tile(c0, state, diagonal):
                kk = k_ref[pl.ds(c0, tq), :]
                vv = v_ref[pl.ds(c0, tq), :]
                ks = kk * ATT_SCALE
                two = range(2)
                last = SB_HALF - 1
                z = [_dot_nt(qms[a], kk) for a in two]
                d_att = [_dot_nt(doms[a], vv) for a in two]
                lb = [_log_sigmoid(z[a]) for a in two]
                l1m = [jnp.where(causal, lb[a] - z[a], 0.0) if diagonal else lb[a] - z[a] for a in two]
                cum = [_half_cumsums(l1m[a], tri_incl) for a in two]
                upto = [jnp.concatenate([cum[a][0] + state[a][1],
                                         cum[a][1] + (state[a][1] + cum[a][0][:, last:last + 1])], axis=1) for a in two]
                att = [jnp.exp(lb[a] + (totals[a] - upto[a])) for a in two]
                if diagonal:
                    att = [jnp.where(causal, att[a], 0.0) for a in two]
                d_log = [d_att[a] * att[a] for a in two]
                cumd = [_half_cumsums(d_log[a], tri_excl) for a in two]
                totd = [[cumd[a][h][:, last:last + 1] + d_log[a][:, h * SB_HALF + last:h * SB_HALF + last + 1]
                         for h in two] for a in two]
                before = [jnp.concatenate([cumd[a][0] + state[a][2], cumd[a][1] + (state[a][2] + totd[a][0])], axis=1)
                          for a in two]
                sig = [jnp.exp(lb[a]) for a in two]
                dz = [d_log[a] * (1.0 - sig[a]) - sig[a] * before[a] for a in two]
                if diagonal:
                    dz = [jnp.where(causal, dz[a], 0.0) for a in two]
                dzb = [dz[a].astype(_MXU_DTYPE) for a in two]
                dq = [state[a][0] + _dot(dzb[a], jnp.where(hm[a], ks, 0)) for a in two]
                dk_acc[pl.ds(c0, tq), :] += _dot_tn(dzb[0], qms[0]) + _dot_tn(dzb[1], qms[1])
                dv_acc[pl.ds(c0, tq), :] += (_dot_tn(att[0].astype(_MXU_DTYPE), doms[0])
                                             + _dot_tn(att[1].astype(_MXU_DTYPE), doms[1]))
                cp = [upto[a][:, tq - 1:tq] for a in two]
                cq = [state[a][2] + (totd[a][0] + totd[a][1]) for a in two]
                return tuple((dq[a], cp[a], cq[a]) for a in two)

            zero_col = jnp.zeros((tq, 1), F32)
            zero = (jnp.zeros((tq, LANES), F32), zero_col, zero_col)
            state = lax.fori_loop(first, qi, lambda kj, st: tile(pl.multiple_of(kj * tq, tq), st, False), (zero, zero))
            state = tile(r0, state, True)
            dq_ref[pl.ds(r0, tq), :] = (state[0][0] + state[1][0]).astype(dq_ref.dtype)
            return carry

        lax.fori_loop(0, nq, q_loop, 0)
        dk_ref[...] = dk_acc[...].astype(dk_ref.dtype)
        dv_ref[...] = dv_acc[...].astype(dv_ref.dtype)

    pair = pl.BlockSpec((S, LANES), lambda b, p: (b, p))
    out = jax.ShapeDtypeStruct((T, SB_W), _MXU_DTYPE)
    return _call(
        body, name=name, grid=(B, SB_W // LANES), out_shape=(out, out, out),
        in_specs=_sb_specs(B, S) + [pair, pair], out_specs=(pair, pair, pair),
        scratch_shapes=[pltpu.VMEM((S, LANES), F32), pltpu.VMEM((S, LANES), F32)],
        args=(proj, proj, proj, dyb, tot), semantics=("parallel", "parallel"), jobs=jobs)


def _layer_step(x, tgt, B, S, small, comm):
    run, big, part = comm.run, comm.big, comm.partial
    ffn1_w, ffn2_w = ("ffn1_down", "ffn1_gate", "ffn1_up"), ("ffn2_down", "ffn2_gate", "ffn2_up")

    h1 = run(_rms_fwd, x, small["ffn1_norm"], "ffn1_rms", ag=("ffn1_up",))
    U1 = run(_mm_nt, h1, big["ffn1_up"], _MXU_DTYPE, "ffn1_up", ag=("ffn1_gate",))
    G1, A1 = run(_ffn_gate, h1, big["ffn1_gate"], U1, "ffn1_gate", ag=("ffn1_down",))
    x1 = run(_mm_nn, [(A1, big["ffn1_down"])], x, 0.5, F32, "ffn1_down", ag=("w_in",))
    h2 = run(_rms_fwd, x1, small["mix_norm"], "mix_rms")
    proj = run(_mm_nt, h2, big["w_in"], _MXU_DTYPE, "in_proj", ag=("w_out",))
    ya = run(_swa_fwd, proj, small["swa_sinks"], B, S, "swa_fwd", ag=("ffn2_gate",))
    yb, tot = run(_sb_fwd, proj, B, S, "sb_fwd", ag=("ffn2_up",))
    yn = _outnorm_fwd(ya, yb, small["swa_out_norm"], small["sb_out_norm"], "out_norm")
    x2 = run(_mm_nn, [(yn, big["w_out"])], x1, 1.0, F32, "out_proj")
    h3 = run(_rms_fwd, x2, small["ffn2_norm"], "ffn2_rms")
    G2, U2, A2 = run(_ffn_gu, h3, big["ffn2_gate"], big["ffn2_up"], "ffn2_gate_up", ag=("ffn2_down",))
    x3 = run(_mm_nn, [(A2, big["ffn2_down"])], x2, 0.5, F32, "ffn2_down")

    dx3, dx3b, d_final, loss = _loss_head(x3, small["final_norm"], tgt, "loss_head")

    dG2, dU2 = run(_ffn_bwd_act, dx3b, big["ffn2_down"], G2, U2, "ffn2_bwd_act")
    part["ffn2_down"] = run(_mm_tn, A2, dx3b, 0.5, _WIRE_DTYPE, "ffn2_dw_down")
    part["ffn2_gate"] = run(_mm_tn, dG2, h3, 1.0, _WIRE_DTYPE, "ffn2_dw_gate")
    part["ffn2_up"] = run(_mm_tn, dU2, h3, 1.0, _WIRE_DTYPE, "ffn2_dw_up")
    dh3 = run(_mm_nn, [(dG2, big["ffn2_gate"]), (dU2, big["ffn2_up"])], None, 1.0, F32, "ffn2_dh", rs1=ffn2_w)
    dx2, dx2b, d_g2 = _rms_bwd(dh3, x2, small["ffn2_norm"], dx3, "ffn2_rms_bwd")

    part["w_out"] = run(_mm_tn, yn, dx2b, 1.0, _WIRE_DTYPE, "dw_out")
    dyn = run(_mm_nt, dx2b, big["w_out"], F32, "out_proj_bwd")
    dya, dyb, d_ga, d_gb = _outnorm_bwd(dyn, ya, yb, small["swa_out_norm"], small["sb_out_norm"], "out_norm_bwd")
    dqa, dka, dva, d_sinks = run(_swa_bwd, proj, small["swa_sinks"], dya, B, S, "swa_bwd", rs2=ffn2_w[:1])
    dqb, dkb, dvb = run(_sb_bwd, proj, tot, dyb, B, S, "sb_bwd", rs2=ffn2_w[1:])
    dproj = jnp.concatenate([dqa, dka, dva, dqb, dkb, dvb], axis=1)
    part["w_in"] = run(_mm_tn, dproj, h2, 1.0, _WIRE_DTYPE, "dw_in")
    dh2 = run(_mm_nn, [(dproj, big["w_in"])], None, 1.0, F32, "in_proj_bwd", rs1=("w_in", "w_out"))
    dx1, dx1b, d_gm = _rms_bwd(dh2, x1, small["mix_norm"], dx2, "mix_rms_bwd")

    dG1, dU1 = run(_ffn_bwd_act, dx1b, big["ffn1_down"], G1, U1, "ffn1_bwd_act", rs2=("w_in", "w_out"))
    part["ffn1_down"] = run(_mm_tn, A1, dx1b, 0.5, _WIRE_DTYPE, "ffn1_dw_down")
    part["ffn1_gate"] = run(_mm_tn, dG1, h1, 1.0, _WIRE_DTYPE, "ffn1_dw_gate", rs1=("ffn1_down",))
    part["ffn1_up"] = run(_mm_tn, dU1, h1, 1.0, _WIRE_DTYPE, "ffn1_dw_up", rs1=("ffn1_gate",), rs2=("ffn1_down",))
    dh1 = run(_mm_nn, [(dG1, big["ffn1_gate"])], None, 1.0, F32, "ffn1_dh_gate", rs1=("ffn1_up",), rs2=("ffn1_gate",))
    dh1 = run(_mm_nn, [(dU1, big["ffn1_up"])], dh1, 1.0, F32, "ffn1_dh_up", rs2=("ffn1_up",))
    gx, _, d_g1 = _rms_bwd(dh1, x, small["ffn1_norm"], dx1, "ffn1_rms_bwd")

    d_small = {"ffn1_norm": d_g1, "mix_norm": d_gm, "swa_sinks": d_sinks[:, :N_SWA_HEADS], "swa_out_norm": d_ga,
               "sb_out_norm": d_gb, "ffn2_norm": d_g2, "final_norm": d_final}
    return loss, gx, d_small


MESH = pl.DeviceIdType.MESH
BIG_NAMES = ("ffn1_gate", "ffn1_up", "ffn1_down", "w_in", "w_out", "ffn2_gate", "ffn2_up", "ffn2_down")
_COMM_PARAMS = pltpu.CompilerParams(has_side_effects=True)


def _place():
    x, y, c = lax.axis_index("x"), lax.axis_index("y"), lax.axis_index("c")
    other_chips = [(1 - x, y), (x, 1 - y), (1 - x, 1 - y)]
    return x, y, c, other_chips


def _padded_rows(rows):
    full = N_DEV * rows
    return -(-full // _F_TILE) * _F_TILE


AG_PARTS = 4


def _row_parts(rows, n):
    units = rows // 16
    assert units * 16 == rows and units >= n
    out, off = [], 0
    for i in range(n):
        size = (units // n + (1 if i < units % n else 0)) * 16
        out.append((off, size))
        off += size
    return out


def _ag_job(shards):
    nw = len(shards)
    D = shards[0].shape[1]
    rows_w = [s.shape[0] for s in shards]
    full_w = [_padded_rows(r) for r in rows_w]
    pad_w = [f - N_DEV * r for f, r in zip(full_w, rows_w)]
    max_pad = max(max(pad_w), 16)
    n_parts = AG_PARTS
    parts_w = [_row_parts(r, n_parts) for r in rows_w]

    class Plan:
        def __init__(self, ins, outs, scratch):
            zbuf, send_sems, recv_sems, local_sems, zero_sems = scratch
            x, y, c, chips = _place()
            me, sibling = (x, y, c), (x, y, 1 - c)

            def rows(w, block, part=None):
                off, size = (0, rows_w[w]) if part is None else part
                px, py, pc = block
                start = pl.multiple_of((4 * px + 2 * py + pc) * rows_w[w] + off, 16)
                return outs[w].at[pl.ds(start, size), :]

            def copy(w, k, block, to, part=None, own=False):
                src = rows(w, block, part)
                if own:
                    src = ins[w] if part is None else ins[w].at[pl.ds(part[0], part[1]), :]
                return pltpu.make_async_remote_copy(
                    src_ref=src, dst_ref=rows(w, block, part), send_sem=send_sems.at[w, k],
                    recv_sem=recv_sems.at[w, k], device_id=to, device_id_type=MESH)

            def k_ici(j, p):
                return 1 + j * n_parts + p

            def k_on(j, p):
                return 1 + (3 + j) * n_parts + p

            self.zbuf = zbuf
            self.local = [pltpu.make_async_copy(zbuf.at[pl.ds(0, pad_w[w]), :],
                                                outs[w].at[pl.ds(N_DEV * rows_w[w], pad_w[w]), :], zero_sems.at[w])
                          for w in range(nw) if pad_w[w]]
            self.local += [pltpu.make_async_copy(ins[w], rows(w, me), local_sems.at[w]) for w in range(nw)]
            self.first = [[copy(w, 0, me, sibling, own=True)]
                          + [copy(w, k_ici(j, p), me, (*chip, c), part, own=True)
                             for p, part in enumerate(parts_w[w]) for j, chip in enumerate(chips)]
                          for w in range(nw)]
            self.arrive = [[copy(w, k_ici(j, p), (*chip, c), me, part)
                            for p, part in enumerate(parts_w[w]) for j, chip in enumerate(chips)] for w in range(nw)]
            self.passed = [[copy(w, k_on(j, p), (*chip, c), sibling, part)
                            for p, part in enumerate(parts_w[w]) for j, chip in enumerate(chips)] for w in range(nw)]
            self.from_sibling = [[copy(w, 0, sibling, me)]
                                 + [copy(w, k_on(j, p), (*chip, 1 - c), me, part)
                                    for p, part in enumerate(parts_w[w]) for j, chip in enumerate(chips)]
                                 for w in range(nw)]

    def start(ins, outs, scratch):
        plan = Plan(ins, outs, scratch)
        plan.zbuf[...] = jnp.zeros_like(plan.zbuf)
        for cp in plan.local:
            cp.start()
        for w in range(nw):
            for cp in plan.first[w]:
                cp.start()

    def mid(ins, outs, scratch, phase):
        plan = Plan(ins, outs, scratch)
        early = 3 * (n_parts // 2)
        for w in range(nw):
            pairs = list(zip(plan.arrive[w], plan.passed[w]))
            for arrived, onward in (pairs[:early] if phase == 0 else pairs[early:]):
                arrived.wait_recv()
                onward.start()

    def finish(ins, outs, scratch):
        plan = Plan(ins, outs, scratch)
        for w in range(nw):
            for cp in plan.from_sibling[w]:
                cp.wait_recv()
        for w in range(nw):
            for cp in plan.first[w] + plan.passed[w]:
                cp.wait_send()
        for cp in plan.local:
            cp.wait()

    return _Job(
        ins=shards, out_shape=[jax.ShapeDtypeStruct((f, D), s.dtype) for f, s in zip(full_w, shards)],
        scratch=[pltpu.VMEM((max_pad, D), shards[0].dtype), pltpu.SemaphoreType.DMA((nw, 1 + 6 * n_parts)),
                 pltpu.SemaphoreType.DMA((nw, 1 + 6 * n_parts)), pltpu.SemaphoreType.DMA((nw,)),
                 pltpu.SemaphoreType.DMA((nw,))],
        start=start, mid=mid, finish=finish)


def _rs1_job(partials, rows_w):
    nw = len(partials)
    D = partials[0].shape[1]

    def copies(ins, outs, scratch):
        send_sems, recv_sems = scratch
        x, y, c, _ = _place()
        out = []
        for w in range(nw):
            r = rows_w[w]
            for q in range(4):
                src = ins[w].at[pl.ds(pl.multiple_of((2 * q + 1 - c) * r, 16), r), :]
                out.append(pltpu.make_async_remote_copy(
                    src_ref=src, dst_ref=outs[w].at[pl.ds(q * r, r), :], send_sem=send_sems.at[w, q],
                    recv_sem=recv_sems.at[w, q], device_id=(x, y, 1 - c), device_id_type=MESH))
        return out

    def start(ins, outs, scratch):
        for cp in copies(ins, outs, scratch):
            cp.start()

    def finish(ins, outs, scratch):
        for cp in copies(ins, outs, scratch):
            cp.wait()

    return _Job(
        ins=partials, out_shape=[jax.ShapeDtypeStruct((4 * r, D), p.dtype) for r, p in zip(rows_w, partials)],
        scratch=[pltpu.SemaphoreType.DMA((nw, 4)), pltpu.SemaphoreType.DMA((nw, 4))], start=start, finish=finish)


def _pair_sum(partial, from_sibling, rows, core, name):
    D = partial.shape[1]

    def body(core_ref, p_ref, s_ref, o_ref):
        o_ref[...] = (p_ref[...].astype(F32) + s_ref[...].astype(F32)).astype(o_ref.dtype)

    grid_spec = pltpu.PrefetchScalarGridSpec(
        num_scalar_prefetch=1, grid=(4,),
        in_specs=[pl.BlockSpec((rows, D), lambda q, core_ref: (2 * q + core_ref[0], 0)),
                  pl.BlockSpec((rows, D), lambda q, core_ref: (q, 0))],
        out_specs=pl.BlockSpec((rows, D), lambda q, core_ref: (q, 0)))
    return pl.pallas_call(
        body, name=name, grid_spec=grid_spec, out_shape=jax.ShapeDtypeStruct((4 * rows, D), partial.dtype),
        compiler_params=_params("arbitrary"),
    )(core, partial, from_sibling)


def _rs2_job(chip_sums, rows_w):
    nw = len(chip_sums)

    def copies(ins, outs, scratch):
        send_sems, recv_sems, local_sems = scratch
        x, y, c, chips = _place()
        my_chip = 2 * x + y
        out = []
        for w in range(nw):
            r = rows_w[w]
            mine = pl.ds(pl.multiple_of(my_chip * r, 16), r)
            out.append(pltpu.make_async_copy(ins[w].at[mine, :], outs[w].at[mine, :], local_sems.at[w]))
            for j, (qx, qy) in enumerate(chips):
                src = ins[w].at[pl.ds(pl.multiple_of((2 * qx + qy) * r, 16), r), :]
                out.append(pltpu.make_async_remote_copy(
                    src_ref=src, dst_ref=outs[w].at[mine, :], send_sem=send_sems.at[w, j],
                    recv_sem=recv_sems.at[w, j], device_id=(qx, qy, c), device_id_type=MESH))
        return out

    def start(ins, outs, scratch):
        for cp in copies(ins, outs, scratch):
            cp.start()

    def finish(ins, outs, scratch):
        for cp in copies(ins, outs, scratch):
            cp.wait()

    return _Job(
        ins=chip_sums, out_shape=[jax.ShapeDtypeStruct(s.shape, s.dtype) for s in chip_sums],
        scratch=[pltpu.SemaphoreType.DMA((nw, 3)), pltpu.SemaphoreType.DMA((nw, 3)), pltpu.SemaphoreType.DMA((nw,))],
        start=start, finish=finish)


class _Comm:
    def __init__(self, shards):
        self.shards = shards
        self.rows = {n: s.shape[0] for n, s in shards.items()}
        self.core = lax.axis_index("c").astype(jnp.int32).reshape(1)
        self.big, self.partial, self.chip_sums, self.slots = {}, {}, {}, {}

    def run(self, fn, *args, ag=(), rs1=(), rs2=()):
        jobs = []
        if ag:
            jobs.append(_ag_job([self.shards[n] for n in ag]))
        if rs1:
            jobs.append(_rs1_job([self.partial[n] for n in rs1], [self.rows[n] for n in rs1]))
        if rs2:
            jobs.append(_rs2_job([self.chip_sums[n] for n in rs2], [self.rows[n] for n in rs2]))
        out, job_res = fn(*args, jobs=jobs)
        job_res = iter(job_res)
        if ag:
            self.big.update(zip(ag, next(job_res)))
        if rs1:
            for n, got in zip(rs1, next(job_res)):
                self.chip_sums[n] = _pair_sum(self.partial[n], got, self.rows[n], self.core, "pair_sum_" + n)
        if rs2:
            self.slots.update(zip(rs2, next(job_res)))
        return out


SMALL_ROWS = 88


def _small_allreduce(vec):
    def body(v_ref, o_ref, gather, send_sems, recv_sems):
        x, y, c, _ = _place()
        my_id = 4 * x + 2 * y + c
        gather[my_id] = v_ref[...]
        copies = []
        for r in range(1, N_DEV):
            peer = (x ^ (r >> 2), y ^ ((r >> 1) & 1), c ^ (r & 1))
            cp = pltpu.make_async_remote_copy(src_ref=v_ref, dst_ref=gather.at[my_id], send_sem=send_sems.at[r - 1],
                                              recv_sem=recv_sems.at[r - 1], device_id=peer, device_id_type=MESH)
            cp.start()
            copies.append(cp)
        for cp in copies:
            cp.wait()
        acc = gather[0]
        for d in range(1, N_DEV):
            acc = acc + gather[d]
        o_ref[...] = acc

    vm = pl.BlockSpec(memory_space=pltpu.VMEM)
    return pl.pallas_call(
        body, name="small_allreduce", out_shape=jax.ShapeDtypeStruct(vec.shape, F32),
        in_specs=[vm], out_specs=vm,
        scratch_shapes=[pltpu.VMEM((N_DEV,) + vec.shape, F32), pltpu.SemaphoreType.DMA((N_DEV - 1,)),
                        pltpu.SemaphoreType.DMA((N_DEV - 1,))],
        compiler_params=_COMM_PARAMS,
    )(vec)


def _adamw_update(w, g, m, v):
    nm = ADAM_B1 * m + (1.0 - ADAM_B1) * g
    nv = ADAM_B2 * v + (1.0 - ADAM_B2) * jnp.square(g)
    m_hat = nm / (1.0 - ADAM_B1 ** ADAM_STEP)
    v_hat = nv / (1.0 - ADAM_B2 ** ADAM_STEP)
    return -ADAM_LR * (m_hat / (jnp.sqrt(v_hat) + ADAM_EPS) + ADAM_WD * w), nm, nv


def _adamw(w, g, m, v, name):
    R, C = w.shape
    tr = _tile(R, 256, 8)

    def body(w_ref, g_ref, m_ref, v_ref, d_ref, nm_ref, nv_ref):
        d_ref[...], nm_ref[...], nv_ref[...] = _adamw_update(w_ref[...], g_ref[...], m_ref[...], v_ref[...])

    spec = pl.BlockSpec((tr, C), lambda i: (i, 0))
    out = jax.ShapeDtypeStruct((R, C), F32)
    return pl.pallas_call(
        body, name=name, grid=(R // tr,), out_shape=(out, out, out),
        in_specs=[spec] * 4, out_specs=(spec, spec, spec),
        compiler_params=_params("parallel"),
    )(w, g, m, v)


def _adamw_slots(w, slots, m, v, name):
    R, C = w.shape
    tc = _tile(C, 512, LANES)

    def body(w_ref, s_ref, m_ref, v_ref, g_ref, d_ref, nm_ref, nv_ref):
        g = s_ref[0].astype(F32)
        for q in range(1, 4):
            g = g + s_ref[q].astype(F32)
        g_ref[...] = g
        d_ref[...], nm_ref[...], nv_ref[...] = _adamw_update(w_ref[...], g, m_ref[...], v_ref[...])

    spec = pl.BlockSpec((R, tc), lambda j: (0, j))
    out = jax.ShapeDtypeStruct((R, C), F32)
    return pl.pallas_call(
        body, name=name, grid=(C // tc,), out_shape=(out, out, out, out),
        in_specs=[spec, pl.BlockSpec((4, R, tc), lambda j: (0, 0, j)), spec, spec], out_specs=(spec, spec, spec, spec),
        compiler_params=_params("parallel"),
    )(w, slots, m, v)


WEIGHT_NAMES = ("ffn1_norm", "ffn1_w_gate", "ffn1_w_up", "ffn1_w_down", "mix_norm", "w_in", "swa_sinks",
                "swa_out_norm", "sb_out_norm", "w_out", "ffn2_norm", "ffn2_w_gate", "ffn2_w_up", "ffn2_w_down",
                "final_norm")
SMALL_NAMES = ("ffn1_norm", "mix_norm", "swa_sinks", "swa_out_norm", "sb_out_norm", "ffn2_norm", "final_norm")
BIG_ARGS = {"ffn1_gate": ("ffn1_w_gate", True), "ffn1_up": ("ffn1_w_up", True), "ffn1_down": ("ffn1_w_down", False),
            "w_in": ("w_in", True), "w_out": ("w_out", False), "ffn2_gate": ("ffn2_w_gate", True),
            "ffn2_up": ("ffn2_w_up", True), "ffn2_down": ("ffn2_w_down", False)}


def _pack_small(parts):
    padded = [jnp.pad(p.reshape(1, -1), ((0, 0), (0, -p.size % LANES))) for p in parts]
    flat = jnp.concatenate(padded, axis=1)
    flat = jnp.pad(flat, ((0, 0), (0, SMALL_ROWS * LANES - flat.shape[1])))
    return flat.reshape(SMALL_ROWS, LANES)


def _unpack_small(block, shapes):
    flat = block.reshape(-1)
    out, off = [], 0
    for shp in shapes:
        n = 1
        for s in shp:
            n *= s
        out.append(flat[off:off + n].reshape(shp))
        off += n + (-n % LANES)
    return out


def kernel(x, ffn1_norm, ffn1_w_gate, ffn1_w_up, ffn1_w_down, mix_norm, w_in, swa_sinks, swa_out_norm, sb_out_norm, w_out, ffn2_norm, ffn2_w_gate, ffn2_w_up, ffn2_w_down, final_norm, loss_target, m_ffn1_norm, m_ffn1_w_gate, m_ffn1_w_up, m_ffn1_w_down, m_mix_norm, m_w_in, m_swa_sinks, m_swa_out_norm, m_sb_out_norm, m_w_out, m_ffn2_norm, m_ffn2_w_gate, m_ffn2_w_up, m_ffn2_w_down, m_final_norm, v_ffn1_norm, v_ffn1_w_gate, v_ffn1_w_up, v_ffn1_w_down, v_mix_norm, v_w_in, v_swa_sinks, v_swa_out_norm, v_sb_out_norm, v_w_out, v_ffn2_norm, v_ffn2_w_gate, v_ffn2_w_up, v_ffn2_w_down, v_final_norm):
    args = dict(locals())
    B, S, D = x.shape
    T = B * S
    weights = {n: args[n] for n in WEIGHT_NAMES}
    mom_m = {n: args["m_" + n] for n in WEIGHT_NAMES}
    mom_v = {n: args["v_" + n] for n in WEIGHT_NAMES}

    shards = {}
    for name in BIG_NAMES:
        arg, transposed = BIG_ARGS[name]
        w2 = weights[arg][0]
        shards[name] = (w2.T if transposed else w2).astype(_WIRE_DTYPE)
    comm = _Comm(shards)
    small = {n: weights[n].reshape(1, -1) for n in SMALL_NAMES}

    loss, gx, d_small = _layer_step(x.reshape(T, D), loss_target.reshape(T, D), B, S, small, comm)

    small_shapes = [(1, 1)] + [d_small[n].shape for n in SMALL_NAMES]
    reduced = _small_allreduce(_pack_small([loss[:, :1]] + [d_small[n] for n in SMALL_NAMES]))
    red = _unpack_small(reduced, small_shapes)
    loss_out = red[0].reshape(())
    g_small = dict(zip(SMALL_NAMES, red[1:]))

    grads, deltas, new_m, new_v = {}, {}, {}, {}
    for name in BIG_NAMES:
        arg, transposed = BIG_ARGS[name]
        to_rows = (lambda t: t[0].T) if transposed else (lambda t: t[0])
        back = (lambda t: t.T[None]) if transposed else (lambda t: t[None])
        slots = comm.slots[name].reshape(4, comm.rows[name], D)
        res = _adamw_slots(to_rows(weights[arg]), slots, to_rows(mom_m[arg]), to_rows(mom_v[arg]), "adamw_" + name)
        grads[arg], deltas[arg], new_m[arg], new_v[arg] = [back(t) for t in res]
    shapes1 = [(1, weights[n].size) for n in SMALL_NAMES]
    packed = [_pack_small([t[n].reshape(1, -1) for n in SMALL_NAMES]) for t in (weights, g_small, mom_m, mom_v)]
    upd = _adamw(*packed, "adamw_small")
    for tgt_dict, block in zip((deltas, new_m, new_v), upd):
        for n, val in zip(SMALL_NAMES, _unpack_small(block, shapes1)):
            tgt_dict[n] = val.reshape(weights[n].shape)
    for n in SMALL_NAMES:
        grads[n] = g_small[n].reshape(weights[n].shape)

    return (loss_out, gx.reshape(B, S, D), *[grads[n] for n in WEIGHT_NAMES], *[deltas[n] for n in WEIGHT_NAMES],
            *[new_m[n] for n in WEIGHT_NAMES], *[new_v[n] for n in WEIGHT_NAMES])
```

```python
import functools

import jax
import jax.numpy as jnp
from jax import lax
from jax.experimental import pallas as pl
from jax.experimental.pallas import tpu as pltpu

F32 = jnp.float32
_MXU_DTYPE = jnp.bfloat16
_WIRE_DTYPE = jnp.bfloat16

EPS = 1e-6
HEAD_DIM = 64
N_SWA_HEADS = 16
N_SWA_KV = 4
N_SB_HEADS = 16
WINDOW = 128
SWA_Q = N_SWA_HEADS * HEAD_DIM
SWA_KV = N_SWA_KV * HEAD_DIM
SB_W = N_SB_HEADS * HEAD_DIM
IN_W = SWA_Q + 2 * SWA_KV + 3 * SB_W
LANES = 128
ATT_SCALE = HEAD_DIM ** -0.5

ADAM_LR = 0.001
ADAM_B1 = 0.9
ADAM_B2 = 0.999
ADAM_EPS = 1e-08
ADAM_WD = 0.01
ADAM_STEP = 10

N_DEV = 8
_VMEM_LIMIT_BYTES = 56 * 1024 * 1024
_F_TILE = 512


def _params(*semantics):
    return pltpu.CompilerParams(dimension_semantics=semantics, vmem_limit_bytes=_VMEM_LIMIT_BYTES)


def _tile(n, pref, align):
    t = min(n, pref)
    t -= t % align
    while t >= align:
        if n % t == 0:
            return t
        t -= align
    return n


def _dot(a, b):
    return lax.dot_general(a, b, (((1,), (0,)), ((), ())), preferred_element_type=F32)


def _dot_nt(a, b):
    return lax.dot_general(a, b, (((1,), (1,)), ((), ())), preferred_element_type=F32)


def _dot_tn(a, b):
    return lax.dot_general(a, b, (((0,), (0,)), ((), ())), preferred_element_type=F32)


class _Job:
    def __init__(self, ins, out_shape, scratch=(), start=None, finish=None, mid=None, each=None, specs=None):
        self.ins, self.out_shape, self.scratch = list(ins), list(out_shape), list(scratch)
        self.start, self.mid, self.finish, self.each, self.specs = start, mid, finish, each, specs


_JOB_MID_FRACTION = 0.6


def _linear_step(grid, ids):
    step = ids[0]
    for d in range(1, len(grid)):
        step = step * grid[d] + ids[d]
    return step


def _call(body, *, name, grid, in_specs, out_specs, out_shape, args, semantics, scratch_shapes=(), jobs=()):
    single = not isinstance(out_shape, (tuple, list))
    if not jobs:
        res = pl.pallas_call(body, name=name, grid=grid, in_specs=list(in_specs), out_specs=out_specs,
                             out_shape=out_shape, scratch_shapes=list(scratch_shapes),
                             compiler_params=_params(*semantics))(*args)
        return res, []
    base_out = [out_shape] if single else list(out_shape)
    base_out_specs = [out_specs] if single else list(out_specs)
    n_in, n_out, n_scr = len(args), len(base_out), len(scratch_shapes)
    any_spec = pl.BlockSpec(memory_space=pl.ANY)
    total = 1
    for g in grid:
        total *= g
    mid_step = min(total - 1, int(total * _JOB_MID_FRACTION))

    def wrapped(*refs):
        pos = n_in
        job_ins = []
        for job in jobs:
            job_ins.append(refs[pos:pos + len(job.ins)])
            pos += len(job.ins)
        outs = refs[pos:pos + n_out]
        pos += n_out
        job_outs = []
        for job in jobs:
            job_outs.append(refs[pos:pos + len(job.out_shape)])
            pos += len(job.out_shape)
        scr = refs[pos:pos + n_scr]
        pos += n_scr
        job_scr = []
        for job in jobs:
            job_scr.append(refs[pos:pos + len(job.scratch)])
            pos += len(job.scratch)
        bound = list(zip(jobs, job_ins, job_outs, job_scr))
        step = _linear_step(grid, [pl.program_id(d) for d in range(len(grid))])

        @pl.when(step == 0)
        def _():
            for job, ji, jo, js in bound:
                if job.start is not None:
                    job.start(ji, jo, js)

        @pl.when(step == mid_step)
        def _():
            for job, ji, jo, js in bound:
                if job.mid is not None:
                    job.mid(ji, jo, js, 0)

        body(*refs[:n_in], *outs, *scr)
        for job, ji, jo, js in bound:
            if job.each is not None:
                job.each(ji, jo, js, step)

        @pl.when(step == total - 1)
        def _():
            for job, ji, jo, js in bound:
                if job.mid is not None:
                    job.mid(ji, jo, js, 1)
            for job, ji, jo, js in bound:
                if job.finish is not None:
                    job.finish(ji, jo, js)

    all_args, all_in_specs = list(args), list(in_specs)
    all_out_shape, all_out_specs = list(base_out), list(base_out_specs)
    for job in jobs:
        job_in_specs, job_out_specs = (job.specs(grid) if job.specs is not None else
                                       ([any_spec] * len(job.ins), [any_spec] * len(job.out_shape)))
        all_args += job.ins
        all_in_specs += job_in_specs
        all_out_shape += job.out_shape
        all_out_specs += job_out_specs
    all_scratch = list(scratch_shapes) + [s for job in jobs for s in job.scratch]
    res = pl.pallas_call(
        wrapped, name=name, grid=grid, in_specs=all_in_specs, out_specs=tuple(all_out_specs),
        out_shape=tuple(all_out_shape), scratch_shapes=all_scratch,
        compiler_params=pltpu.CompilerParams(dimension_semantics=("arbitrary",) * len(grid),
                                             vmem_limit_bytes=_VMEM_LIMIT_BYTES,
                                             has_side_effects=any(job.start is not None for job in jobs)),
    )(*all_args)
    base = res[0] if single else tuple(res[:n_out])
    job_res, pos = [], n_out
    for job in jobs:
        job_res.append(tuple(res[pos:pos + len(job.out_shape)]))
        pos += len(job.out_shape)
    return base, job_res


def _rms_fwd(x, g, name, jobs=()):
    T, D = x.shape
    tm = _tile(T, 512, 16)

    def body(x_ref, g_ref, o_ref):
        xv = x_ref[...]
        r = lax.rsqrt(jnp.mean(xv * xv, axis=-1, keepdims=True) + EPS)
        o_ref[...] = (xv * r * g_ref[...]).astype(o_ref.dtype)

    return _call(
        body, name=name, grid=(T // tm,),
        out_shape=jax.ShapeDtypeStruct((T, D), _MXU_DTYPE),
        in_specs=[pl.BlockSpec((tm, D), lambda i: (i, 0)), pl.BlockSpec((1, D), lambda i: (0, 0))],
        out_specs=pl.BlockSpec((tm, D), lambda i: (i, 0)), args=(x, g), semantics=("parallel",), jobs=jobs)


def _rms_bwd_rows(dh, xv, g):
    r = lax.rsqrt(jnp.mean(xv * xv, axis=-1, keepdims=True) + EPS)
    xhat = xv * r
    u = dh * g
    dx = r * (u - xhat * jnp.mean(u * xhat, axis=-1, keepdims=True))
    return dx, dh * xhat


def _rms_bwd(dh, x, g, dres, name):
    T, D = x.shape
    tm = _tile(T, 256, 16)

    def body(dh_ref, x_ref, g_ref, dres_ref, dx_ref, dxb_ref, dg_ref):
        @pl.when(pl.program_id(0) == 0)
        def _():
            dg_ref[...] = jnp.zeros_like(dg_ref)

        dx, dgr = _rms_bwd_rows(dh_ref[...], x_ref[...], g_ref[...])
        dx = dres_ref[...] + dx
        dx_ref[...] = dx
        dxb_ref[...] = dx.astype(dxb_ref.dtype)
        dg_ref[...] += jnp.sum(dgr, axis=0, keepdims=True)

    row = pl.BlockSpec((tm, D), lambda i: (i, 0))
    vec = pl.BlockSpec((1, D), lambda i: (0, 0))
    return pl.pallas_call(
        body, name=name, grid=(T // tm,),
        out_shape=(jax.ShapeDtypeStruct((T, D), F32), jax.ShapeDtypeStruct((T, D), _MXU_DTYPE),
                   jax.ShapeDtypeStruct((1, D), F32)),
        in_specs=[row, row, vec, row], out_specs=(row, row, vec),
        compiler_params=_params("arbitrary"),
    )(dh, x, g, dres)


def _loss_head(x, g, tgt, name):
    T, D = x.shape
    tm = _tile(T, 256, 16)

    def body(x_ref, g_ref, t_ref, dx_ref, dxb_ref, dg_ref, loss_ref):
        @pl.when(pl.program_id(0) == 0)
        def _():
            dg_ref[...] = jnp.zeros_like(dg_ref)
            loss_ref[...] = jnp.zeros_like(loss_ref)

        xv = x_ref[...]
        gv = g_ref[...]
        r = lax.rsqrt(jnp.mean(xv * xv, axis=-1, keepdims=True) + EPS)
        xhat = xv * r
        diff = xhat * gv - t_ref[...]
        tok = jnp.mean(diff * diff, axis=-1, keepdims=True)
        loss_ref[...] += 0.5 * jnp.sum(tok, axis=0, keepdims=True)
        dy = diff / D
        u = dy * gv
        dx = r * (u - xhat * jnp.mean(u * xhat, axis=-1, keepdims=True))
        dx_ref[...] = dx
        dxb_ref[...] = dx.astype(dxb_ref.dtype)
        dg_ref[...] += jnp.sum(dy * xhat, axis=0, keepdims=True)

    row = pl.BlockSpec((tm, D), lambda i: (i, 0))
    vec = pl.BlockSpec((1, D), lambda i: (0, 0))
    return pl.pallas_call(
        body, name=name, grid=(T // tm,),
        out_shape=(jax.ShapeDtypeStruct((T, D), F32), jax.ShapeDtypeStruct((T, D), _MXU_DTYPE),
                   jax.ShapeDtypeStruct((1, D), F32), jax.ShapeDtypeStruct((1, LANES), F32)),
        in_specs=[row, vec, row],
        out_specs=(row, row, vec, pl.BlockSpec((1, LANES), lambda i: (0, 0))),
        compiler_params=_params("arbitrary"),
    )(x, g, tgt)


def _outnorm_fwd(ya, yb, ga, gb, name):
    T, W = ya.shape
    tm = _tile(T, 512, 16)

    def body(ya_ref, yb_ref, ga_ref, gb_ref, o_ref):
        for k, (y_ref, g_ref) in enumerate(((ya_ref, ga_ref), (yb_ref, gb_ref))):
            yv = y_ref[...]
            r = lax.rsqrt(jnp.mean(yv * yv, axis=-1, keepdims=True) + EPS)
            o_ref[:, k * W:(k + 1) * W] = (yv * r * g_ref[...]).astype(o_ref.dtype)

    row = pl.BlockSpec((tm, W), lambda i: (i, 0))
    vec = pl.BlockSpec((1, W), lambda i: (0, 0))
    return pl.pallas_call(
        body, name=name, grid=(T // tm,),
        out_shape=jax.ShapeDtypeStruct((T, 2 * W), _MXU_DTYPE),
        in_specs=[row, row, vec, vec], out_specs=pl.BlockSpec((tm, 2 * W), lambda i: (i, 0)),
        compiler_params=_params("parallel"),
    )(ya, yb, ga, gb)


def _outnorm_bwd(dyn, ya, yb, ga, gb, name):
    T, W = ya.shape
    tm = _tile(T, 256, 16)

    def body(d_ref, ya_ref, yb_ref, ga_ref, gb_ref, dya_ref, dyb_ref, dga_ref, dgb_ref):
        @pl.when(pl.program_id(0) == 0)
        def _():
            dga_ref[...] = jnp.zeros_like(dga_ref)
            dgb_ref[...] = jnp.zeros_like(dgb_ref)

        for k, (y_ref, g_ref, dy_ref, dg_ref) in enumerate(
                ((ya_ref, ga_ref, dya_ref, dga_ref), (yb_ref, gb_ref, dyb_ref, dgb_ref))):
            dy, dgr = _rms_bwd_rows(d_ref[:, k * W:(k + 1) * W], y_ref[...], g_ref[...])
            dy_ref[...] = dy.astype(dy_ref.dtype)
            dg_ref[...] += jnp.sum(dgr, axis=0, keepdims=True)

    row = pl.BlockSpec((tm, W), lambda i: (i, 0))
    vec = pl.BlockSpec((1, W), lambda i: (0, 0))
    return pl.pallas_call(
        body, name=name, grid=(T // tm,),
        out_shape=(jax.ShapeDtypeStruct((T, W), _MXU_DTYPE), jax.ShapeDtypeStruct((T, W), _MXU_DTYPE),
                   jax.ShapeDtypeStruct((1, W), F32), jax.ShapeDtypeStruct((1, W), F32)),
        in_specs=[pl.BlockSpec((tm, 2 * W), lambda i: (i, 0)), row, row, vec, vec],
        out_specs=(row, row, vec, vec),
        compiler_params=_params("arbitrary"),
    )(dyn, ya, yb, ga, gb)


_STRIP_ROWS = 256


def _strips(rows):
    step = min(rows, _STRIP_ROWS)
    return [slice(r, r + step) for r in range(0, rows, step)]


def _ffn_gu(h, wg_t, wu_t, name, jobs=()):
    T, D = h.shape
    Fp = wg_t.shape[0]
    tm = _tile(T, 1024, 16)
    tn = _tile(Fp, _F_TILE, LANES)

    def body(h_ref, wg_ref, wu_ref, g_ref, u_ref, a_ref):
        for rows in _strips(tm):
            hv = h_ref[rows, :]
            g = _dot_nt(hv, wg_ref[...])
            u = _dot_nt(hv, wu_ref[...])
            g_ref[rows, :] = g.astype(g_ref.dtype)
            u_ref[rows, :] = u.astype(u_ref.dtype)
            a_ref[rows, :] = (g * jax.nn.sigmoid(g) * u).astype(a_ref.dtype)

    act = pl.BlockSpec((tm, tn), lambda n, m: (m, n))
    wsp = pl.BlockSpec((tn, D), lambda n, m: (n, 0))
    out = jax.ShapeDtypeStruct((T, Fp), _MXU_DTYPE)
    return _call(
        body, name=name, grid=(Fp // tn, T // tm), out_shape=(out, out, out),
        in_specs=[pl.BlockSpec((tm, D), lambda n, m: (m, 0)), wsp, wsp],
        out_specs=(act, act, act), args=(h, wg_t, wu_t), semantics=("parallel", "parallel"), jobs=jobs)


def _ffn_bwd_act(dxb, wd, G, U, name, jobs=()):
    T, D = dxb.shape
    Fp = wd.shape[0]
    tm = _tile(T, 1024, 16)
    tn = _tile(Fp, _F_TILE, LANES)

    def body(e_ref, wd_ref, g_ref, u_ref, dg_ref, du_ref):
        for rows in _strips(tm):
            da = 0.5 * _dot_nt(e_ref[rows, :], wd_ref[...])
            g = g_ref[rows, :].astype(F32)
            u = u_ref[rows, :].astype(F32)
            s = jax.nn.sigmoid(g)
            du_ref[rows, :] = (da * (g * s)).astype(du_ref.dtype)
            dg_ref[rows, :] = (da * u * (s * (1.0 + g * (1.0 - s)))).astype(dg_ref.dtype)

    act = pl.BlockSpec((tm, tn), lambda m, n: (m, n))
    out = jax.ShapeDtypeStruct((T, Fp), _MXU_DTYPE)
    return _call(
        body, name=name, grid=(T // tm, Fp // tn), out_shape=(out, out),
        in_specs=[pl.BlockSpec((tm, D), lambda m, n: (m, 0)), pl.BlockSpec((tn, D), lambda m, n: (n, 0)),
                  act, act],
        out_specs=(act, act), args=(dxb, wd, G, U), semantics=("parallel", "parallel"), jobs=jobs)


def _ffn_gate(h, wg_t, U, name, jobs=()):
    T, D = h.shape
    Fp = wg_t.shape[0]
    tm = _tile(T, 1024, 16)
    tn = _tile(Fp, _F_TILE, LANES)

    def body(h_ref, wg_ref, u_ref, g_ref, a_ref):
        for rows in _strips(tm):
            g = _dot_nt(h_ref[rows, :], wg_ref[...])
            g_ref[rows, :] = g.astype(g_ref.dtype)
            a_ref[rows, :] = (g * jax.nn.sigmoid(g) * u_ref[rows, :].astype(F32)).astype(a_ref.dtype)

    act = pl.BlockSpec((tm, tn), lambda m, n: (m, n))
    out = jax.ShapeDtypeStruct((T, Fp), _MXU_DTYPE)
    return _call(
        body, name=name, grid=(T // tm, Fp // tn), out_shape=(out, out),
        in_specs=[pl.BlockSpec((tm, D), lambda m, n: (m, 0)), pl.BlockSpec((tn, D), lambda m, n: (n, 0)), act],
        out_specs=(act, act), args=(h, wg_t, U), semantics=("parallel", "parallel"), jobs=jobs)


def _mm_nt(a, b, out_dtype, name, jobs=()):
    M, K = a.shape
    N = b.shape[0]
    tm = _tile(M, 1024, 16)
    tn = _tile(N, 512, LANES)

    def body(a_ref, b_ref, o_ref):
        o_ref[...] = _dot_nt(a_ref[...], b_ref[...]).astype(o_ref.dtype)

    return _call(
        body, name=name, grid=(M // tm, N // tn), out_shape=jax.ShapeDtypeStruct((M, N), out_dtype),
        in_specs=[pl.BlockSpec((tm, K), lambda m, n: (m, 0)), pl.BlockSpec((tn, K), lambda m, n: (n, 0))],
        out_specs=pl.BlockSpec((tm, tn), lambda m, n: (m, n)), args=(a, b),
        semantics=("parallel", "parallel"), jobs=jobs)


_MM_OPERAND_BYTES = 26 * 1024 * 1024


def _k_tile(K, bytes_per_k, align):
    best = align
    for t in range(align, K + 1, align):
        if K % t == 0 and 2 * t * bytes_per_k <= _MM_OPERAND_BYTES:
            best = t
    return best


def _mm_nn(pairs, res, alpha, out_dtype, name, jobs=()):
    M, K = pairs[0][0].shape
    N = pairs[0][1].shape[1]
    n_pairs = len(pairs)
    tm = _tile(M, 1024, 16)
    tn = _tile(N, 1024, LANES)
    tk = _k_tile(K, n_pairs * (tm + tn) * pairs[0][0].dtype.itemsize, LANES)
    nk = K // tk

    def body(*refs):
        ab = refs[:2 * n_pairs]
        res_ref = refs[2 * n_pairs] if res is not None else None
        o_ref = refs[2 * n_pairs + (res is not None)]

        def finish(acc):
            out = alpha * acc
            if res_ref is not None:
                out = res_ref[...] + out
            o_ref[...] = out.astype(o_ref.dtype)

        part = _dot(ab[0][...], ab[1][...])
        for i in range(1, n_pairs):
            part = part + _dot(ab[2 * i][...], ab[2 * i + 1][...])
        if nk == 1:
            finish(part)
        else:
            acc_ref = refs[-1]
            k = pl.program_id(2)

            @pl.when(k == 0)
            def _():
                acc_ref[...] = part

            @pl.when(k > 0)
            def _():
                acc_ref[...] += part

            @pl.when(k == nk - 1)
            def _():
                finish(acc_ref[...])

    in_specs, args = [], []
    for a, b in pairs:
        in_specs += [pl.BlockSpec((tm, tk), lambda m, n, k: (m, k)), pl.BlockSpec((tk, tn), lambda m, n, k: (k, n))]
        args += [a, b]
    if res is not None:
        in_specs.append(pl.BlockSpec((tm, tn), lambda m, n, k: (m, n)))
        args.append(res)
    return _call(
        body, name=name, grid=(M // tm, N // tn, nk), out_shape=jax.ShapeDtypeStruct((M, N), out_dtype),
        in_specs=in_specs, out_specs=pl.BlockSpec((tm, tn), lambda m, n, k: (m, n)),
        scratch_shapes=[pltpu.VMEM((tm, tn), F32)] if nk > 1 else [], args=args,
        semantics=("parallel", "parallel", "arbitrary"), jobs=jobs)


def _mm_tn(a, b, alpha, out_dtype, name, jobs=()):
    K, M = a.shape
    N = b.shape[1]
    tm = _tile(M, 512, LANES)
    tn = _tile(N, 1024, LANES)

    def body(a_ref, b_ref, o_ref):
        o_ref[...] = (alpha * _dot_tn(a_ref[...], b_ref[...])).astype(o_ref.dtype)

    return _call(
        body, name=name, grid=(N // tn, M // tm), out_shape=jax.ShapeDtypeStruct((M, N), out_dtype),
        in_specs=[pl.BlockSpec((K, tm), lambda n, m: (0, m)), pl.BlockSpec((K, tn), lambda n, m: (0, n))],
        out_specs=pl.BlockSpec((tm, tn), lambda n, m: (m, n)), args=(a, b),
        semantics=("parallel", "parallel"), jobs=jobs)


def _half_masks():
    lane = lax.broadcasted_iota(jnp.int32, (1, LANES), 1)
    return (lane < HEAD_DIM, lane >= HEAD_DIM)


def _swap_halves(v):
    return pltpu.roll(v.astype(F32), HEAD_DIM, 1).astype(v.dtype)


def _swa_geometry(n):
    qi = lax.broadcasted_iota(jnp.int32, (WINDOW, 2 * WINDOW), 0)
    kp = lax.broadcasted_iota(jnp.int32, (WINDOW, 2 * WINDOW), 1)
    dist = (WINDOW + qi) - kp
    valid = (dist >= 0) & (dist < WINDOW) & ((n > 0) | (kp >= WINDOW))
    return dist.astype(F32), valid


def _swa_slope(h):
    return 2.0 ** (-8.0 * (h + 1) / N_SWA_HEADS)


def _swa_softmax(qk, sink, slope, distf, valid):
    s = qk * ATT_SCALE - slope * distf
    s = jnp.where(valid, s, -1e30)
    m = jnp.maximum(jnp.max(s, axis=1, keepdims=True), sink)
    p = jnp.exp(s - m)
    e_sink = jnp.exp(sink - m)
    den = jnp.sum(p, axis=1, keepdims=True) + e_sink
    return p / den, e_sink / den


def _swa_group_heads(g):
    return [(2 * pp + a, pp, a) for pp in (2 * g, 2 * g + 1) for a in range(2)]


def _swa_specs(B, S):
    nb = S // WINDOW
    kcol = SWA_Q // SWA_KV
    cur = lambda b, n: (b * nb + n, kcol)
    prev = lambda b, n: (b * nb + jnp.maximum(n - 1, 0), kcol)
    curv = lambda b, n: (b * nb + n, kcol + 1)
    prevv = lambda b, n: (b * nb + jnp.maximum(n - 1, 0), kcol + 1)
    q_spec = pl.BlockSpec((WINDOW, SWA_Q), lambda b, n: (b * nb + n, 0))
    kv = [pl.BlockSpec((WINDOW, SWA_KV), f) for f in (prev, cur, prevv, curv)]
    sink_spec = pl.BlockSpec(memory_space=pltpu.SMEM)
    return nb, q_spec, kv, sink_spec


def _swa_kv_views(kp_ref, kc_ref, vp_ref, vc_ref, g):
    hm = _half_masks()
    c0 = (g // 2) * LANES
    k_all = jnp.concatenate([kp_ref[:, c0:c0 + LANES], kc_ref[:, c0:c0 + LANES]], axis=0)
    v_all = jnp.concatenate([vp_ref[:, c0:c0 + LANES], vc_ref[:, c0:c0 + LANES]], axis=0)
    b = g % 2
    ks, vs = [None, None], [None, None]
    ks[b], vs[b] = k_all, v_all
    ks[1 - b], vs[1 - b] = _swap_halves(k_all), _swap_halves(v_all)
    ks = [jnp.where(hm[a], ks[a], 0) for a in range(2)]
    vs = [jnp.where(hm[a], vs[a], 0) for a in range(2)]
    return ks, vs


def _swa_fwd(proj, sinks, B, S, name, jobs=()):
    T = B * S
    nb, q_spec, kv_specs, sink_spec = _swa_specs(B, S)

    def body(sink_ref, q_ref, kp_ref, kc_ref, vp_ref, vc_ref, y_ref):
        hm = _half_masks()
        distf, valid = _swa_geometry(pl.program_id(1))
        for g in range(N_SWA_KV):
            ks, vs = _swa_kv_views(kp_ref, kc_ref, vp_ref, vc_ref, g)
            heads = _swa_group_heads(g)
            qk = [_dot_nt(jnp.where(hm[a], q_ref[:, pp * LANES:(pp + 1) * LANES], 0), ks[a]) for _, pp, a in heads]
            p = [_swa_softmax(qk[i], sink_ref[0, h], _swa_slope(h), distf, valid)[0] for i, (h, _, _) in enumerate(heads)]
            o = [_dot(p[i].astype(_MXU_DTYPE), vs[a]) for i, (_, _, a) in enumerate(heads)]
            for j, pp in enumerate((2 * g, 2 * g + 1)):
                y_ref[:, pp * LANES:(pp + 1) * LANES] = o[2 * j] + o[2 * j + 1]

    return _call(
        body, name=name, grid=(B, nb), out_shape=jax.ShapeDtypeStruct((T, SWA_Q), F32),
        in_specs=[sink_spec, q_spec] + kv_specs,
        out_specs=pl.BlockSpec((WINDOW, SWA_Q), lambda b, n: (b * nb + n, 0)),
        args=(sinks, proj, proj, proj, proj, proj), semantics=("parallel", "parallel"), jobs=jobs)


def _swa_bwd(proj, sinks, dya, B, S, name, jobs=()):
    T = B * S
    nb, q_spec, kv_specs, sink_spec = _swa_specs(B, S)

    def body(sink_ref, q_ref, kp_ref, kc_ref, vp_ref, vc_ref, do_ref,
             dq_ref, dk_ref, dv_ref, dsink_ref, dk_acc, dv_acc):
        b_id, n = pl.program_id(0), pl.program_id(1)
        hm = _half_masks()
        lane = lax.broadcasted_iota(jnp.int32, (1, LANES), 1)

        @pl.when((b_id == 0) & (n == 0))
        def _():
            dsink_ref[...] = jnp.zeros_like(dsink_ref)

        @pl.when(n == 0)
        def _():
            dk_acc[...] = jnp.zeros_like(dk_acc)
            dv_acc[...] = jnp.zeros_like(dv_acc)

        distf, valid = _swa_geometry(n)
        r_prev = pl.multiple_of(jnp.maximum(n - 1, 0) * WINDOW, WINDOW)
        r_cur = pl.multiple_of(n * WINDOW, WINDOW)
        dsink = jnp.zeros((1, LANES), F32)
        for g in range(N_SWA_KV):
            ks, vs = _swa_kv_views(kp_ref, kc_ref, vp_ref, vc_ref, g)
            heads = _swa_group_heads(g)
            four = range(len(heads))
            qms = [jnp.where(hm[a], q_ref[:, pp * LANES:(pp + 1) * LANES], 0) for _, pp, a in heads]
            doms = [jnp.where(hm[a], do_ref[:, pp * LANES:(pp + 1) * LANES], 0) for _, pp, a in heads]
            qk = [_dot_nt(qms[i], ks[heads[i][2]]) for i in four]
            dp = [_dot_nt(doms[i], vs[heads[i][2]]) for i in four]
            soft = [_swa_softmax(qk[i], sink_ref[0, heads[i][0]], _swa_slope(heads[i][0]), distf, valid) for i in four]
            p = [soft[i][0] for i in four]
            delta = [jnp.sum(p[i] * dp[i], axis=1, keepdims=True) for i in four]
            ds = [(p[i] * (dp[i] - delta[i]) * ATT_SCALE).astype(_MXU_DTYPE) for i in four]
            for i in four:
                dsink = dsink + jnp.where(lane == heads[i][0], -jnp.sum(soft[i][1] * delta[i]), 0.0)
            dq = [_dot(ds[i], ks[heads[i][2]]) for i in four]
            dk_h = [_dot_tn(ds[i], qms[i]) for i in four]
            dv_h = [_dot_tn(p[i].astype(_MXU_DTYPE), doms[i]) for i in four]
            for j, pp in enumerate((2 * g, 2 * g + 1)):
                dq_ref[:, pp * LANES:(pp + 1) * LANES] = (dq[2 * j] + dq[2 * j + 1]).astype(dq_ref.dtype)
            dk_g = [dk_h[a] + dk_h[2 + a] for a in range(2)]
            dv_g = [dv_h[a] + dv_h[2 + a] for a in range(2)]
            bsel = g % 2
            dk_t = dk_g[bsel] + pltpu.roll(dk_g[1 - bsel], HEAD_DIM, 1)
            dv_t = dv_g[bsel] + pltpu.roll(dv_g[1 - bsel], HEAD_DIM, 1)
            c0 = (g // 2) * LANES
            dk_acc[pl.ds(r_prev, WINDOW), c0:c0 + LANES] += dk_t[:WINDOW]
            dk_acc[pl.ds(r_cur, WINDOW), c0:c0 + LANES] += dk_t[WINDOW:]
            dv_acc[pl.ds(r_prev, WINDOW), c0:c0 + LANES] += dv_t[:WINDOW]
            dv_acc[pl.ds(r_cur, WINDOW), c0:c0 + LANES] += dv_t[WINDOW:]
        dsink_ref[...] += dsink

        @pl.when(n == nb - 1)
        def _():
            dk_ref[...] = dk_acc[...].astype(dk_ref.dtype)
            dv_ref[...] = dv_acc[...].astype(dv_ref.dtype)

    seq_kv = pl.BlockSpec((S, SWA_KV), lambda b, n: (b, 0))
    return _call(
        body, name=name, grid=(B, nb),
        out_shape=(jax.ShapeDtypeStruct((T, SWA_Q), _MXU_DTYPE), jax.ShapeDtypeStruct((T, SWA_KV), _MXU_DTYPE),
                   jax.ShapeDtypeStruct((T, SWA_KV), _MXU_DTYPE), jax.ShapeDtypeStruct((1, LANES), F32)),
        in_specs=[sink_spec, q_spec] + kv_specs + [pl.BlockSpec((WINDOW, SWA_Q), lambda b, n: (b * nb + n, 0))],
        out_specs=(pl.BlockSpec((WINDOW, SWA_Q), lambda b, n: (b * nb + n, 0)), seq_kv, seq_kv,
                   pl.BlockSpec((1, LANES), lambda b, n: (0, 0))),
        scratch_shapes=[pltpu.VMEM((S, SWA_KV), F32), pltpu.VMEM((S, SWA_KV), F32)],
        args=(sinks, proj, proj, proj, proj, proj, dya), semantics=("arbitrary", "arbitrary"), jobs=jobs)


SB_TILE = 256
SB_HALF = 128
SB_DEAD = -105.0


def _mark_lanes():
    lane = lax.broadcasted_iota(jnp.int32, (1, LANES), 1)
    return (lane == HEAD_DIM - 1) | (lane == LANES - 1)


def _tri2(cond):
    j = lax.broadcasted_iota(jnp.int32, (2 * SB_HALF, SB_HALF), 0) & (SB_HALF - 1)
    s = lax.broadcasted_iota(jnp.int32, (2 * SB_HALF, SB_HALF), 1)
    return cond(j, s).astype(_MXU_DTYPE)


def _half_cumsums(x, tri2):
    out = []
    for h in range(2):
        xh = x[:, h * SB_HALF:(h + 1) * SB_HALF]
        hi = xh.astype(_MXU_DTYPE)
        lo = (xh - hi.astype(F32)).astype(_MXU_DTYPE)
        out.append(_dot(jnp.concatenate([hi, lo], axis=1), tri2))
    return out


def _log_sigmoid(z):
    return jnp.minimum(z, 0.0) - jnp.log(1.0 + jnp.exp(-jnp.abs(z)))


def _sb_specs(B, S):
    qb = (SWA_Q + 2 * SWA_KV) // LANES
    kb = qb + SB_W // LANES
    vb = kb + SB_W // LANES
    return [pl.BlockSpec((S, LANES), functools.partial(lambda b, p, c: (b, c + p), c=c)) for c in (qb, kb, vb)]


def _sb_fwd(proj, B, S, name, jobs=()):
    T = B * S
    tq = SB_TILE
    nq = S // tq

    def body(q_ref, k_ref, v_ref, y_ref, tot_ref):
        hm = _half_masks()
        ji = lax.broadcasted_iota(jnp.int32, (tq, tq), 0)
        si = lax.broadcasted_iota(jnp.int32, (tq, tq), 1)
        tri_after = _tri2(lambda j, s: j > s)
        causal = si < ji
        mark = _mark_lanes()

        def q_loop(qi, carry):
            r0 = pl.multiple_of(qi * tq, tq)
            q_pair = q_ref[pl.ds(r0, tq), :] * ATT_SCALE
            qms = [jnp.where(hm[a], q_pair, 0) for a in range(2)]

            def tile(c0, state, diagonal):
                kk = k_ref[pl.ds(c0, tq), :]
                vv = v_ref[pl.ds(c0, tq), :]
                two = range(2)
                z = [_dot_nt(qms[a], kk) for a in two]
                lb = [_log_sigmoid(z[a]) for a in two]
                l1m = [jnp.where(causal, lb[a] - z[a], 0.0) if diagonal else lb[a] - z[a] for a in two]
                cum = [_half_cumsums(l1m[a], tri_after) for a in two]
                tot = [[cum[a][h][:, 0:1] + l1m[a][:, h * SB_HALF:h * SB_HALF + 1] for h in two] for a in two]
                after = [jnp.concatenate([cum[a][0] + (state[a][1] + tot[a][1]), cum[a][1] + state[a][1]], axis=1)
                         for a in two]
                att = [jnp.exp(lb[a] + after[a]) for a in two]
                if diagonal:
                    att = [jnp.where(causal, att[a], 0.0) for a in two]
                acc = [state[a][0] + _dot(att[a].astype(_MXU_DTYPE), jnp.where(hm[a], vv, 0)) for a in two]
                car = [state[a][1] + (tot[a][0] + tot[a][1]) for a in two]
                return tuple((acc[a], car[a]) for a in two)

            def live(st):
                return jnp.maximum(jnp.max(st[0][1]), jnp.max(st[1][1])) > SB_DEAD

            def step(c):
                it, _, st = c
                st = tile(pl.multiple_of((qi - 1 - it) * tq, tq), st, False)
                return it + 1, live(st), st

            zero = (jnp.zeros((tq, LANES), F32), jnp.zeros((tq, 1), F32))
            state = tile(r0, (zero, zero), True)
            done, _, state = lax.while_loop(lambda c: (c[0] < qi) & c[1], step, (jnp.int32(0), live(state), state))
            y_ref[pl.ds(r0, tq), :] = state[0][0] + state[1][0]
            first = (qi - done).astype(F32)
            tot_ref[pl.ds(r0, tq), :] = jnp.where(mark, first, jnp.where(hm[0], state[0][1], state[1][1]))
            return carry

        lax.fori_loop(0, nq, q_loop, 0)

    out_spec = pl.BlockSpec((S, LANES), lambda b, p: (b, p))
    return _call(
        body, name=name, grid=(B, SB_W // LANES),
        out_shape=(jax.ShapeDtypeStruct((T, SB_W), F32), jax.ShapeDtypeStruct((T, SB_W), F32)),
        in_specs=_sb_specs(B, S), out_specs=(out_spec, out_spec), args=(proj, proj, proj),
        semantics=("parallel", "parallel"), jobs=jobs)


def _sb_bwd(proj, tot, dyb, B, S, name, jobs=()):
    T = B * S
    tq = SB_TILE
    nq = S // tq

    def body(q_ref, k_ref, v_ref, do_ref, tot_ref, dq_ref, dk_ref, dv_ref, dk_acc, dv_acc):
        hm = _half_masks()
        ji = lax.broadcasted_iota(jnp.int32, (tq, tq), 0)
        si = lax.broadcasted_iota(jnp.int32, (tq, tq), 1)
        tri_incl = _tri2(lambda j, s: j <= s)
        tri_excl = _tri2(lambda j, s: j < s)
        causal = si < ji
        mark = _mark_lanes()
        dk_acc[...] = jnp.zeros_like(dk_acc)
        dv_acc[...] = jnp.zeros_like(dv_acc)

        def q_loop(qi, carry):
            r0 = pl.multiple_of(qi * tq, tq)
            q_pair = q_ref[pl.ds(r0, tq), :] * ATT_SCALE
            do_pair = do_ref[pl.ds(r0, tq), :]
            tot_pair = tot_ref[pl.ds(r0, tq), :]
            qms = [jnp.where(hm[a], q_pair, 0) for a in range(2)]
            doms = [jnp.where(hm[a], do_pair, 0) for a in range(2)]
            totals = [jnp.max(jnp.where(hm[a] & ~mark, tot_pair, -jnp.inf), axis=1, keepdims=True) for a in range(2)]
            first = jnp.max(jnp.where(mark, tot_pair, -jnp.inf))
            first = jnp.where((first >= 0.0) & (first <= qi.astype(F32)), first, 0.0).astype(jnp.int32)

            def tile(c0, state, diagonal):
                kk = k_ref[pl.ds(c0, tq), :]
                vv = v_ref[pl.ds(c0, tq), :]
                ks = kk * ATT_SCALE
                two = range(2)
                last = SB_HALF - 1
                z = [_dot_nt(qms[a], kk) for a in two]
                d_att = [_dot_nt(doms[a], vv) for a in two]
                lb = [_log_sigmoid(z[a]) for a in two]
                l1m = [jnp.where(causal, lb[a] - z[a], 0.0) if diagonal else lb[a] - z[a] for a in two]
                cum = [_half_cumsums(l1m[a], tri_incl) for a in two]
                upto = [jnp.concatenate([cum[a][0] + state[a][1],
                                         cum[a][1] + (state[a][1] + cum[a][0][:, last:last + 1])], axis=1) for a in two]
                att = [jnp.exp(lb[a] + (totals[a] - upto[a])) for a in two]
                if diagonal:
                    att = [jnp.where(causal, att[a], 0.0) for a in two]
                d_log = [d_att[a] * att[a] for a in two]
                cumd = [_half_cumsums(d_log[a], tri_excl) for a in two]
                totd = [[cumd[a][h][:, last:last + 1] + d_log[a][:, h * SB_HALF + last:h * SB_HALF + last + 1]
                         for h in two] for a in two]
                before = [jnp.concatenate([cumd[a][0] + state[a][2], cumd[a][1] + (state[a][2] + totd[a][0])], axis=1)
                          for a in two]
                sig = [jnp.exp(lb[a]) for a in two]
                dz = [d_log[a] * (1.0 - sig[a]) - sig[a] * before[a] for a in two]
                if diagonal:
                    dz = [jnp.where(causal, dz[a], 0.0) for a in two]
                dzb = [dz[a].astype(_MXU_DTYPE) for a in two]
                dq = [state[a][0] + _dot(dzb[a], jnp.where(hm[a], ks, 0)) for a in two]
                dk_acc[pl.ds(c0, tq), :] += _dot_tn(dzb[0], qms[0]) + _dot_tn(dzb[1], qms[1])
                dv_acc[pl.ds(c0, tq), :] += (_dot_tn(att[0].astype(_MXU_DTYPE), doms[0])
                                             + _dot_tn(att[1].astype(_MXU_DTYPE), doms[1]))
                cp = [upto[a][:, tq - 1:tq] for a in two]
                cq = [state[a][2] + (totd[a][0] + totd[a][1]) for a in two]
                return tuple((dq[a], cp[a], cq[a]) for a in two)

            zero_col = jnp.zeros((tq, 1), F32)
            zero = (jnp.zeros((tq, LANES), F32), zero_col, zero_col)
            state = lax.fori_loop(first, qi, lambda kj, st: tile(pl.multiple_of(kj * tq, tq), st, False), (zero, zero))
            state = tile(r0, state, True)
            dq_ref[pl.ds(r0, tq), :] = (state[0][0] + state[1][0]).astype(dq_ref.dtype)
            return carry

        lax.fori_loop(0, nq, q_loop, 0)
        dk_ref[...] = dk_acc[...].astype(dk_ref.dtype)
        dv_ref[...] = dv_acc[...].astype(dv_ref.dtype)

    pair = pl.BlockSpec((S, LANES), lambda b, p: (b, p))
    out = jax.ShapeDtypeStruct((T, SB_W), _MXU_DTYPE)
    return _call(
        body, name=name, grid=(B, SB_W // LANES), out_shape=(out, out, out),
        in_specs=_sb_specs(B, S) + [pair, pair], out_specs=(pair, pair, pair),
        scratch_shapes=[pltpu.VMEM((S, LANES), F32), pltpu.VMEM((S, LANES), F32)],
        args=(proj, proj, proj, dyb, tot), semantics=("parallel", "parallel"), jobs=jobs)


def _layer_step(x, tgt, B, S, small, comm):
    run, big, part = comm.run, comm.big, comm.partial
    ffn1_w, ffn2_w = ("ffn1_down", "ffn1_gate", "ffn1_up"), ("ffn2_down", "ffn2_gate", "ffn2_up")

    h1 = run(_rms_fwd, x, small["ffn1_norm"], "ffn1_rms", ag=("ffn1_up",))
    U1 = run(_mm_nt, h1, big["ffn1_up"], _MXU_DTYPE, "ffn1_up", ag=("ffn1_gate",))
    G1, A1 = run(_ffn_gate, h1, big["ffn1_gate"], U1, "ffn1_gate", ag=("ffn1_down",))
    x1 = run(_mm_nn, [(A1, big["ffn1_down"])], x, 0.5, F32, "ffn1_down", ag=("w_in",))
    h2 = run(_rms_fwd, x1, small["mix_norm"], "mix_rms")
    proj = run(_mm_nt, h2, big["w_in"], _MXU_DTYPE, "in_proj", ag=("w_out",))
    ya = run(_swa_fwd, proj, small["swa_sinks"], B, S, "swa_fwd", ag=("ffn2_gate",))
    yb, tot = run(_sb_fwd, proj, B, S, "sb_fwd", ag=("ffn2_up",))
    yn = _outnorm_fwd(ya, yb, small["swa_out_norm"], small["sb_out_norm"], "out_norm")
    x2 = run(_mm_nn, [(yn, big["w_out"])], x1, 1.0, F32, "out_proj")
    h3 = run(_rms_fwd, x2, small["ffn2_norm"], "ffn2_rms")
    G2, U2, A2 = run(_ffn_gu, h3, big["ffn2_gate"], big["ffn2_up"], "ffn2_gate_up", ag=("ffn2_down",))
    x3 = run(_mm_nn, [(A2, big["ffn2_down"])], x2, 0.5, F32, "ffn2_down")

    dx3, dx3b, d_final, loss = _loss_head(x3, small["final_norm"], tgt, "loss_head")

    dG2, dU2 = run(_ffn_bwd_act, dx3b, big["ffn2_down"], G2, U2, "ffn2_bwd_act")
    part["ffn2_down"] = run(_mm_tn, A2, dx3b, 0.5, _WIRE_DTYPE, "ffn2_dw_down")
    part["ffn2_gate"] = run(_mm_tn, dG2, h3, 1.0, _WIRE_DTYPE, "ffn2_dw_gate")
    part["ffn2_up"] = run(_mm_tn, dU2, h3, 1.0, _WIRE_DTYPE, "ffn2_dw_up")
    dh3 = run(_mm_nn, [(dG2, big["ffn2_gate"]), (dU2, big["ffn2_up"])], None, 1.0, F32, "ffn2_dh", rs1=ffn2_w)
    dx2, dx2b, d_g2 = _rms_bwd(dh3, x2, small["ffn2_norm"], dx3, "ffn2_rms_bwd")

    part["w_out"] = run(_mm_tn, yn, dx2b, 1.0, _WIRE_DTYPE, "dw_out")
    dyn = run(_mm_nt, dx2b, big["w_out"], F32, "out_proj_bwd")
    dya, dyb, d_ga, d_gb = _outnorm_bwd(dyn, ya, yb, small["swa_out_norm"], small["sb_out_norm"], "out_norm_bwd")
    dqa, dka, dva, d_sinks = run(_swa_bwd, proj, small["swa_sinks"], dya, B, S, "swa_bwd", rs2=ffn2_w[:1])
    dqb, dkb, dvb = run(_sb_bwd, proj, tot, dyb, B, S, "sb_bwd", rs2=ffn2_w[1:])
    dproj = jnp.concatenate([dqa, dka, dva, dqb, dkb, dvb], axis=1)
    part["w_in"] = run(_mm_tn, dproj, h2, 1.0, _WIRE_DTYPE, "dw_in", adamw=("ffn2_down", "ffn2_gate"))
    dh2 = run(_mm_nn, [(dproj, big["w_in"])], None, 1.0, F32, "in_proj_bwd", rs1=("w_in", "w_out"))
    dx1, dx1b, d_gm = _rms_bwd(dh2, x1, small["mix_norm"], dx2, "mix_rms_bwd")

    dG1, dU1 = run(_ffn_bwd_act, dx1b, big["ffn1_down"], G1, U1, "ffn1_bwd_act", rs2=("w_in", "w_out"))
    part["ffn1_down"] = run(_mm_tn, A1, dx1b, 0.5, _WIRE_DTYPE, "ffn1_dw_down", adamw=("ffn2_up", "w_in", "w_out"))
    part["ffn1_gate"] = run(_mm_tn, dG1, h1, 1.0, _WIRE_DTYPE, "ffn1_dw_gate", rs1=("ffn1_down",))
    part["ffn1_up"] = run(_mm_tn, dU1, h1, 1.0, _WIRE_DTYPE, "ffn1_dw_up", rs1=("ffn1_gate",), rs2=("ffn1_down",))
    dh1 = run(_mm_nn, [(dG1, big["ffn1_gate"])], None, 1.0, F32, "ffn1_dh_gate", rs1=("ffn1_up",), rs2=("ffn1_gate",))
    dh1 = run(_mm_nn, [(dU1, big["ffn1_up"])], dh1, 1.0, F32, "ffn1_dh_up", rs2=("ffn1_up",))
    gx, _, d_g1 = _rms_bwd(dh1, x, small["ffn1_norm"], dx1, "ffn1_rms_bwd")

    d_small = {"ffn1_norm": d_g1, "mix_norm": d_gm, "swa_sinks": d_sinks[:, :N_SWA_HEADS], "swa_out_norm": d_ga,
               "sb_out_norm": d_gb, "ffn2_norm": d_g2, "final_norm": d_final}
    return loss, gx, d_small


MESH = pl.DeviceIdType.MESH
BIG_NAMES = ("ffn1_gate", "ffn1_up", "ffn1_down", "w_in", "w_out", "ffn2_gate", "ffn2_up", "ffn2_down")
_COMM_PARAMS = pltpu.CompilerParams(has_side_effects=True)


def _place():
    x, y, c = lax.axis_index("x"), lax.axis_index("y"), lax.axis_index("c")
    other_chips = [(1 - x, y), (x, 1 - y), (1 - x, 1 - y)]
    return x, y, c, other_chips


def _padded_rows(rows):
    full = N_DEV * rows
    return -(-full // _F_TILE) * _F_TILE


AG_PARTS = 4


def _row_parts(rows, n):
    units = rows // 16
    assert units * 16 == rows and units >= n
    out, off = [], 0
    for i in range(n):
        size = (units // n + (1 if i < units % n else 0)) * 16
        out.append((off, size))
        off += size
    return out


def _ag_job(shards):
    nw = len(shards)
    D = shards[0].shape[1]
    rows_w = [s.shape[0] for s in shards]
    full_w = [_padded_rows(r) for r in rows_w]
    pad_w = [f - N_DEV * r for f, r in zip(full_w, rows_w)]
    max_pad = max(max(pad_w), 16)
    n_parts = AG_PARTS
    parts_w = [_row_parts(r, n_parts) for r in rows_w]

    class Plan:
        def __init__(self, ins, outs, scratch):
            zbuf, send_sems, recv_sems, local_sems, zero_sems = scratch
            x, y, c, chips = _place()
            me, sibling = (x, y, c), (x, y, 1 - c)

            def rows(w, block, part=None):
                off, size = (0, rows_w[w]) if part is None else part
                px, py, pc = block
                start = pl.multiple_of((4 * px + 2 * py + pc) * rows_w[w] + off, 16)
                return outs[w].at[pl.ds(start, size), :]

            def copy(w, k, block, to, part=None, own=False):
                src = rows(w, block, part)
                if own:
                    src = ins[w] if part is None else ins[w].at[pl.ds(part[0], part[1]), :]
                return pltpu.make_async_remote_copy(
                    src_ref=src, dst_ref=rows(w, block, part), send_sem=send_sems.at[w, k],
                    recv_sem=recv_sems.at[w, k], device_id=to, device_id_type=MESH)

            def k_ici(j, p):
                return 1 + j * n_parts + p

            def k_on(j, p):
                return 1 + (3 + j) * n_parts + p

            self.zbuf = zbuf
            self.local = [pltpu.make_async_copy(zbuf.at[pl.ds(0, pad_w[w]), :],
                                                outs[w].at[pl.ds(N_DEV * rows_w[w], pad_w[w]), :], zero_sems.at[w])
                          for w in range(nw) if pad_w[w]]
            self.local += [pltpu.make_async_copy(ins[w], rows(w, me), local_sems.at[w]) for w in range(nw)]
            self.first = [[copy(w, 0, me, sibling, own=True)]
                          + [copy(w, k_ici(j, p), me, (*chip, c), part, own=True)
                             for p, part in enumerate(parts_w[w]) for j, chip in enumerate(chips)]
                          for w in range(nw)]
            self.arrive = [[copy(w, k_ici(j, p), (*chip, c), me, part)
                            for p, part in enumerate(parts_w[w]) for j, chip in enumerate(chips)] for w in range(nw)]
            self.passed = [[copy(w, k_on(j, p), (*chip, c), sibling, part)
                            for p, part in enumerate(parts_w[w]) for j, chip in enumerate(chips)] for w in range(nw)]
            self.from_sibling = [[copy(w, 0, sibling, me)]
                                 + [copy(w, k_on(j, p), (*chip, 1 - c), me, part)
                                    for p, part in enumerate(parts_w[w]) for j, chip in enumerate(chips)]
                                 for w in range(nw)]

    def start(ins, outs, scratch):
        plan = Plan(ins, outs, scratch)
        plan.zbuf[...] = jnp.zeros_like(plan.zbuf)
        for cp in plan.local:
            cp.start()
        for w in range(nw):
            for cp in plan.first[w]:
                cp.start()

    def mid(ins, outs, scratch, phase):
        plan = Plan(ins, outs, scratch)
        early = 3 * (n_parts // 2)
        for w in range(nw):
            pairs = list(zip(plan.arrive[w], plan.passed[w]))
            for arrived, onward in (pairs[:early] if phase == 0 else pairs[early:]):
                arrived.wait_recv()
                onward.start()

    def finish(ins, outs, scratch):
        plan = Plan(ins, outs, scratch)
        for w in range(nw):
            for cp in plan.from_sibling[w]:
                cp.wait_recv()
        for w in range(nw):
            for cp in plan.first[w] + plan.passed[w]:
                cp.wait_send()
        for cp in plan.local:
            cp.wait()

    return _Job(
        ins=shards, out_shape=[jax.ShapeDtypeStruct((f, D), s.dtype) for f, s in zip(full_w, shards)],
        scratch=[pltpu.VMEM((max_pad, D), shards[0].dtype), pltpu.SemaphoreType.DMA((nw, 1 + 6 * n_parts)),
                 pltpu.SemaphoreType.DMA((nw, 1 + 6 * n_parts)), pltpu.SemaphoreType.DMA((nw,)),
                 pltpu.SemaphoreType.DMA((nw,))],
        start=start, mid=mid, finish=finish)


def _rs1_job(partials, rows_w):
    nw = len(partials)
    D = partials[0].shape[1]

    def copies(ins, outs, scratch):
        send_sems, recv_sems = scratch
        x, y, c, _ = _place()
        out = []
        for w in range(nw):
            r = rows_w[w]
            for q in range(4):
                src = ins[w].at[pl.ds(pl.multiple_of((2 * q + 1 - c) * r, 16), r), :]
                out.append(pltpu.make_async_remote_copy(
                    src_ref=src, dst_ref=outs[w].at[pl.ds(q * r, r), :], send_sem=send_sems.at[w, q],
                    recv_sem=recv_sems.at[w, q], device_id=(x, y, 1 - c), device_id_type=MESH))
        return out

    def start(ins, outs, scratch):
        for cp in copies(ins, outs, scratch):
            cp.start()

    def finish(ins, outs, scratch):
        for cp in copies(ins, outs, scratch):
            cp.wait()

    return _Job(
        ins=partials, out_shape=[jax.ShapeDtypeStruct((4 * r, D), p.dtype) for r, p in zip(rows_w, partials)],
        scratch=[pltpu.SemaphoreType.DMA((nw, 4)), pltpu.SemaphoreType.DMA((nw, 4))], start=start, finish=finish)


def _pair_sum(partial, from_sibling, rows, core, name):
    D = partial.shape[1]

    def body(core_ref, p_ref, s_ref, o_ref):
        o_ref[...] = (p_ref[...].astype(F32) + s_ref[...].astype(F32)).astype(o_ref.dtype)

    grid_spec = pltpu.PrefetchScalarGridSpec(
        num_scalar_prefetch=1, grid=(4,),
        in_specs=[pl.BlockSpec((rows, D), lambda q, core_ref: (2 * q + core_ref[0], 0)),
                  pl.BlockSpec((rows, D), lambda q, core_ref: (q, 0))],
        out_specs=pl.BlockSpec((rows, D), lambda q, core_ref: (q, 0)))
    return pl.pallas_call(
        body, name=name, grid_spec=grid_spec, out_shape=jax.ShapeDtypeStruct((4 * rows, D), partial.dtype),
        compiler_params=_params("arbitrary"),
    )(core, partial, from_sibling)


def _rs2_job(chip_sums, rows_w):
    nw = len(chip_sums)

    def copies(ins, outs, scratch):
        send_sems, recv_sems, local_sems = scratch
        x, y, c, chips = _place()
        my_chip = 2 * x + y
        out = []
        for w in range(nw):
            r = rows_w[w]
            mine = pl.ds(pl.multiple_of(my_chip * r, 16), r)
            out.append(pltpu.make_async_copy(ins[w].at[mine, :], outs[w].at[mine, :], local_sems.at[w]))
            for j, (qx, qy) in enumerate(chips):
                src = ins[w].at[pl.ds(pl.multiple_of((2 * qx + qy) * r, 16), r), :]
                out.append(pltpu.make_async_remote_copy(
                    src_ref=src, dst_ref=outs[w].at[mine, :], send_sem=send_sems.at[w, j],
                    recv_sem=recv_sems.at[w, j], device_id=(qx, qy, c), device_id_type=MESH))
        return out

    def start(ins, outs, scratch):
        for cp in copies(ins, outs, scratch):
            cp.start()

    def finish(ins, outs, scratch):
        for cp in copies(ins, outs, scratch):
            cp.wait()

    return _Job(
        ins=chip_sums, out_shape=[jax.ShapeDtypeStruct(s.shape, s.dtype) for s in chip_sums],
        scratch=[pltpu.SemaphoreType.DMA((nw, 3)), pltpu.SemaphoreType.DMA((nw, 3)), pltpu.SemaphoreType.DMA((nw,))],
        start=start, finish=finish)


class _Comm:
    def __init__(self, shards, state):
        self.shards, self.state = shards, state
        self.rows = {n: s.shape[0] for n, s in shards.items()}
        self.core = lax.axis_index("c").astype(jnp.int32).reshape(1)
        self.big, self.partial, self.chip_sums, self.slots, self.updates = {}, {}, {}, {}, {}

    def slots3(self, name):
        return self.slots[name].reshape(4, self.rows[name], -1)

    def run(self, fn, *args, ag=(), rs1=(), rs2=(), adamw=()):
        jobs = []
        if ag:
            jobs.append(_ag_job([self.shards[n] for n in ag]))
        if rs1:
            jobs.append(_rs1_job([self.partial[n] for n in rs1], [self.rows[n] for n in rs1]))
        if rs2:
            jobs.append(_rs2_job([self.chip_sums[n] for n in rs2], [self.rows[n] for n in rs2]))
        for n in adamw:
            w2, m2, v2 = self.state[n]
            jobs.append(_adamw_job(w2, self.slots3(n), m2, v2))
        out, job_res = fn(*args, jobs=jobs)
        job_res = iter(job_res)
        if ag:
            self.big.update(zip(ag, next(job_res)))
        if rs1:
            for n, got in zip(rs1, next(job_res)):
                self.chip_sums[n] = _pair_sum(self.partial[n], got, self.rows[n], self.core, "pair_sum_" + n)
        if rs2:
            self.slots.update(zip(rs2, next(job_res)))
        for n in adamw:
            self.updates[n] = next(job_res)
        return out


SMALL_ROWS = 88


def _small_allreduce(vec):
    def body(v_ref, o_ref, gather, send_sems, recv_sems):
        x, y, c, _ = _place()
        my_id = 4 * x + 2 * y + c
        gather[my_id] = v_ref[...]
        copies = []
        for r in range(1, N_DEV):
            peer = (x ^ (r >> 2), y ^ ((r >> 1) & 1), c ^ (r & 1))
            cp = pltpu.make_async_remote_copy(src_ref=v_ref, dst_ref=gather.at[my_id], send_sem=send_sems.at[r - 1],
                                              recv_sem=recv_sems.at[r - 1], device_id=peer, device_id_type=MESH)
            cp.start()
            copies.append(cp)
        for cp in copies:
            cp.wait()
        acc = gather[0]
        for d in range(1, N_DEV):
            acc = acc + gather[d]
        o_ref[...] = acc

    vm = pl.BlockSpec(memory_space=pltpu.VMEM)
    return pl.pallas_call(
        body, name="small_allreduce", out_shape=jax.ShapeDtypeStruct(vec.shape, F32),
        in_specs=[vm], out_specs=vm,
        scratch_shapes=[pltpu.VMEM((N_DEV,) + vec.shape, F32), pltpu.SemaphoreType.DMA((N_DEV - 1,)),
                        pltpu.SemaphoreType.DMA((N_DEV - 1,))],
        compiler_params=_COMM_PARAMS,
    )(vec)


def _adamw_update(w, g, m, v):
    nm = ADAM_B1 * m + (1.0 - ADAM_B1) * g
    nv = ADAM_B2 * v + (1.0 - ADAM_B2) * jnp.square(g)
    m_hat = nm / (1.0 - ADAM_B1 ** ADAM_STEP)
    v_hat = nv / (1.0 - ADAM_B2 ** ADAM_STEP)
    return -ADAM_LR * (m_hat / (jnp.sqrt(v_hat) + ADAM_EPS) + ADAM_WD * w), nm, nv


def _adamw(w, g, m, v, name):
    R, C = w.shape
    tr = _tile(R, 256, 8)

    def body(w_ref, g_ref, m_ref, v_ref, d_ref, nm_ref, nv_ref):
        d_ref[...], nm_ref[...], nv_ref[...] = _adamw_update(w_ref[...], g_ref[...], m_ref[...], v_ref[...])

    spec = pl.BlockSpec((tr, C), lambda i: (i, 0))
    out = jax.ShapeDtypeStruct((R, C), F32)
    return pl.pallas_call(
        body, name=name, grid=(R // tr,), out_shape=(out, out, out),
        in_specs=[spec] * 4, out_specs=(spec, spec, spec),
        compiler_params=_params("parallel"),
    )(w, g, m, v)


def _adamw_slots(w, slots, m, v, name):
    R, C = w.shape
    tc = _tile(C, 512, LANES)
    spec = pl.BlockSpec((R, tc), lambda j: (0, j))
    out = jax.ShapeDtypeStruct((R, C), F32)
    return pl.pallas_call(
        functools.partial(_adamw_slots_body), name=name, grid=(C // tc,), out_shape=(out, out, out, out),
        in_specs=[spec, pl.BlockSpec((4, R, tc), lambda j: (0, 0, j)), spec, spec], out_specs=(spec, spec, spec, spec),
        compiler_params=_params("parallel"),
    )(w, slots, m, v)


def _adamw_slots_body(w_ref, s_ref, m_ref, v_ref, g_ref, d_ref, nm_ref, nv_ref):
    g = s_ref[0].astype(F32)
    for q in range(1, 4):
        g = g + s_ref[q].astype(F32)
    g_ref[...] = g
    d_ref[...], nm_ref[...], nv_ref[...] = _adamw_update(w_ref[...], g, m_ref[...], v_ref[...])


def _adamw_job(w, slots, m, v):
    R, C = w.shape

    n_slices = C // LANES

    def specs(grid):
        total = 1
        for g in grid:
            total *= g
        assert total >= n_slices, (grid, n_slices)
        col = lambda *ids: jnp.minimum(_linear_step(grid, ids), n_slices - 1)
        blk = pl.BlockSpec((R, LANES), lambda *ids: (0, col(*ids)))
        slot_blk = pl.BlockSpec((4, R, LANES), lambda *ids: (0, 0, col(*ids)))
        return [blk, slot_blk, blk, blk], [blk] * 4

    def each(ins, outs, scratch, step):
        @pl.when(step < n_slices)
        def _():
            _adamw_slots_body(*ins, *outs)

    out = jax.ShapeDtypeStruct((R, C), F32)
    return _Job(ins=[w, slots, m, v], out_shape=[out] * 4, specs=specs, each=each)


WEIGHT_NAMES = ("ffn1_norm", "ffn1_w_gate", "ffn1_w_up", "ffn1_w_down", "mix_norm", "w_in", "swa_sinks",
                "swa_out_norm", "sb_out_norm", "w_out", "ffn2_norm", "ffn2_w_gate", "ffn2_w_up", "ffn2_w_down",
                "final_norm")
SMALL_NAMES = ("ffn1_norm", "mix_norm", "swa_sinks", "swa_out_norm", "sb_out_norm", "ffn2_norm", "final_norm")
BIG_ARGS = {"ffn1_gate": ("ffn1_w_gate", True), "ffn1_up": ("ffn1_w_up", True), "ffn1_down": ("ffn1_w_down", False),
            "w_in": ("w_in", True), "w_out": ("w_out", False), "ffn2_gate": ("ffn2_w_gate", True),
            "ffn2_up": ("ffn2_w_up", True), "ffn2_down": ("ffn2_w_down", False)}


def _pack_small(parts):
    padded = [jnp.pad(p.reshape(1, -1), ((0, 0), (0, -p.size % LANES))) for p in parts]
    flat = jnp.concatenate(padded, axis=1)
    flat = jnp.pad(flat, ((0, 0), (0, SMALL_ROWS * LANES - flat.shape[1])))
    return flat.reshape(SMALL_ROWS, LANES)


def _unpack_small(block, shapes):
    flat = block.reshape(-1)
    out, off = [], 0
    for shp in shapes:
        n = 1
        for s in shp:
            n *= s
        out.append(flat[off:off + n].reshape(shp))
        off += n + (-n % LANES)
    return out


def kernel(x, ffn1_norm, ffn1_w_gate, ffn1_w_up, ffn1_w_down, mix_norm, w_in, swa_sinks, swa_out_norm, sb_out_norm, w_out, ffn2_norm, ffn2_w_gate, ffn2_w_up, ffn2_w_down, final_norm, loss_target, m_ffn1_norm, m_ffn1_w_gate, m_ffn1_w_up, m_ffn1_w_down, m_mix_norm, m_w_in, m_swa_sinks, m_swa_out_norm, m_sb_out_norm, m_w_out, m_ffn2_norm, m_ffn2_w_gate, m_ffn2_w_up, m_ffn2_w_down, m_final_norm, v_ffn1_norm, v_ffn1_w_gate, v_ffn1_w_up, v_ffn1_w_down, v_mix_norm, v_w_in, v_swa_sinks, v_swa_out_norm, v_sb_out_norm, v_w_out, v_ffn2_norm, v_ffn2_w_gate, v_ffn2_w_up, v_ffn2_w_down, v_final_norm):
    args = dict(locals())
    B, S, D = x.shape
    T = B * S
    weights = {n: args[n] for n in WEIGHT_NAMES}
    mom_m = {n: args["m_" + n] for n in WEIGHT_NAMES}
    mom_v = {n: args["v_" + n] for n in WEIGHT_NAMES}

    shards, state = {}, {}
    for name in BIG_NAMES:
        arg, transposed = BIG_ARGS[name]
        to_rows = (lambda t: t[0].T) if transposed else (lambda t: t[0])
        state[name] = tuple(to_rows(t[arg]) for t in (weights, mom_m, mom_v))
        shards[name] = state[name][0].astype(_WIRE_DTYPE)
    comm = _Comm(shards, state)
    small = {n: weights[n].reshape(1, -1) for n in SMALL_NAMES}

    loss, gx, d_small = _layer_step(x.reshape(T, D), loss_target.reshape(T, D), B, S, small, comm)

    small_shapes = [(1, 1)] + [d_small[n].shape for n in SMALL_NAMES]
    reduced = _small_allreduce(_pack_small([loss[:, :1]] + [d_small[n] for n in SMALL_NAMES]))
    red = _unpack_small(reduced, small_shapes)
    loss_out = red[0].reshape(())
    g_small = dict(zip(SMALL_NAMES, red[1:]))

    grads, deltas, new_m, new_v = {}, {}, {}, {}
    for name in BIG_NAMES:
        arg, transposed = BIG_ARGS[name]
        back = (lambda t: t.T[None]) if transposed else (lambda t: t[None])
        res = comm.updates.get(name)
        if res is None:
            w2, m2, v2 = state[name]
            res = _adamw_slots(w2, comm.slots3(name), m2, v2, "adamw_" + name)
        grads[arg], deltas[arg], new_m[arg], new_v[arg] = [back(t) for t in res]
    shapes1 = [(1, weights[n].size) for n in SMALL_NAMES]
    packed = [_pack_small([t[n].reshape(1, -1) for n in SMALL_NAMES]) for t in (weights, g_small, mom_m, mom_v)]
    upd = _adamw(*packed, "adamw_small")
    for tgt_dict, block in zip((deltas, new_m, new_v), upd):
        for n, val in zip(SMALL_NAMES, _unpack_small(block, shapes1)):
            tgt_dict[n] = val.reshape(weights[n].shape)
    for n in SMALL_NAMES:
        grads[n] = g_small[n].reshape(weights[n].shape)

    return (loss_out, gx.reshape(B, S, D), *[grads[n] for n in WEIGHT_NAMES], *[deltas[n] for n in WEIGHT_NAMES],
            *[new_m[n] for n in WEIGHT_NAMES], *[new_v[n] for n in WEIGHT_NAMES])
```

```python
import functools

import jax
import jax.numpy as jnp
from jax import lax
from jax.experimental import pallas as pl
from jax.experimental.pallas import tpu as pltpu

F32 = jnp.float32
_MXU_DTYPE = jnp.bfloat16
_WIRE_DTYPE = jnp.bfloat16

EPS = 1e-6
HEAD_DIM = 64
N_SWA_HEADS = 16
N_SWA_KV = 4
N_SB_HEADS = 16
WINDOW = 128
SWA_Q = N_SWA_HEADS * HEAD_DIM
SWA_KV = N_SWA_KV * HEAD_DIM
SB_W = N_SB_HEADS * HEAD_DIM
IN_W = SWA_Q + 2 * SWA_KV + 3 * SB_W
LANES = 128
ATT_SCALE = HEAD_DIM ** -0.5

ADAM_LR = 0.001
ADAM_B1 = 0.9
ADAM_B2 = 0.999
ADAM_EPS = 1e-08
ADAM_WD = 0.01
ADAM_STEP = 10

N_DEV = 8
_VMEM_LIMIT_BYTES = 56 * 1024 * 1024
_F_TILE = 512


def _params(*semantics):
    return pltpu.CompilerParams(dimension_semantics=semantics, vmem_limit_bytes=_VMEM_LIMIT_BYTES)


def _tile(n, pref, align):
    t = min(n, pref)
    t -= t % align
    while t >= align:
        if n % t == 0:
            return t
        t -= align
    return n


def _dot(a, b):
    return lax.dot_general(a, b, (((1,), (0,)), ((), ())), preferred_element_type=F32)


def _dot_nt(a, b):
    return lax.dot_general(a, b, (((1,), (1,)), ((), ())), preferred_element_type=F32)


def _dot_tn(a, b):
    return lax.dot_general(a, b, (((0,), (0,)), ((), ())), preferred_element_type=F32)


class _Job:
    def __init__(self, ins, out_shape, scratch=(), start=None, finish=None, mid=None, each=None, specs=None):
        self.ins, self.out_shape, self.scratch = list(ins), list(out_shape), list(scratch)
        self.start, self.mid, self.finish, self.each, self.specs = start, mid, finish, each, specs


_JOB_MID_FRACTION = 0.6


def _linear_step(grid, ids):
    step = ids[0]
    for d in range(1, len(grid)):
        step = step * grid[d] + ids[d]
    return step


def _call(body, *, name, grid, in_specs, out_specs, out_shape, args, semantics, scratch_shapes=(), jobs=()):
    single = not isinstance(out_shape, (tuple, list))
    if not jobs:
        res = pl.pallas_call(body, name=name, grid=grid, in_specs=list(in_specs), out_specs=out_specs,
                             out_shape=out_shape, scratch_shapes=list(scratch_shapes),
                             compiler_params=_params(*semantics))(*args)
        return res, []
    base_out = [out_shape] if single else list(out_shape)
    base_out_specs = [out_specs] if single else list(out_specs)
    n_in, n_out, n_scr = len(args), len(base_out), len(scratch_shapes)
    any_spec = pl.BlockSpec(memory_space=pl.ANY)
    total = 1
    for g in grid:
        total *= g
    mid_step = min(total - 1, int(total * _JOB_MID_FRACTION))

    def wrapped(*refs):
        pos = n_in
        job_ins = []
        for job in jobs:
            job_ins.append(refs[pos:pos + len(job.ins)])
            pos += len(job.ins)
        outs = refs[pos:pos + n_out]
        pos += n_out
        job_outs = []
        for job in jobs:
            job_outs.append(refs[pos:pos + len(job.out_shape)])
            pos += len(job.out_shape)
        scr = refs[pos:pos + n_scr]
        pos += n_scr
        job_scr = []
        for job in jobs:
            job_scr.append(refs[pos:pos + len(job.scratch)])
            pos += len(job.scratch)
        bound = list(zip(jobs, job_ins, job_outs, job_scr))
        step = _linear_step(grid, [pl.program_id(d) for d in range(len(grid))])

        @pl.when(step == 0)
        def _():
            for job, ji, jo, js in bound:
                if job.start is not None:
                    job.start(ji, jo, js)

        @pl.when(step == mid_step)
        def _():
            for job, ji, jo, js in bound:
                if job.mid is not None:
                    job.mid(ji, jo, js, 0)

        body(*refs[:n_in], *outs, *scr)
        for job, ji, jo, js in bound:
            if job.each is not None:
                job.each(ji, jo, js, step)

        @pl.when(step == total - 1)
        def _():
            for job, ji, jo, js in bound:
                if job.mid is not None:
                    job.mid(ji, jo, js, 1)
            for job, ji, jo, js in bound:
                if job.finish is not None:
                    job.finish(ji, jo, js)

    all_args, all_in_specs = list(args), list(in_specs)
    all_out_shape, all_out_specs = list(base_out), list(base_out_specs)
    for job in jobs:
        job_in_specs, job_out_specs = (job.specs(grid) if job.specs is not None else
                                       ([any_spec] * len(job.ins), [any_spec] * len(job.out_shape)))
        all_args += job.ins
        all_in_specs += job_in_specs
        all_out_shape += job.out_shape
        all_out_specs += job_out_specs
    all_scratch = list(scratch_shapes) + [s for job in jobs for s in job.scratch]
    res = pl.pallas_call(
        wrapped, name=name, grid=grid, in_specs=all_in_specs, out_specs=tuple(all_out_specs),
        out_shape=tuple(all_out_shape), scratch_shapes=all_scratch,
        compiler_params=pltpu.CompilerParams(dimension_semantics=("arbitrary",) * len(grid),
                                             vmem_limit_bytes=_VMEM_LIMIT_BYTES,
                                             has_side_effects=any(job.start is not None for job in jobs)),
    )(*all_args)
    base = res[0] if single else tuple(res[:n_out])
    job_res, pos = [], n_out
    for job in jobs:
        job_res.append(tuple(res[pos:pos + len(job.out_shape)]))
        pos += len(job.out_shape)
    return base, job_res


def _rms_fwd(x, g, name, jobs=()):
    T, D = x.shape
    tm = _tile(T, 512, 16)

    def body(x_ref, g_ref, o_ref):
        xv = x_ref[...]
        r = lax.rsqrt(jnp.mean(xv * xv, axis=-1, keepdims=True) + EPS)
        o_ref[...] = (xv * r * g_ref[...]).astype(o_ref.dtype)

    return _call(
        body, name=name, grid=(T // tm,),
        out_shape=jax.ShapeDtypeStruct((T, D), _MXU_DTYPE),
        in_specs=[pl.BlockSpec((tm, D), lambda i: (i, 0)), pl.BlockSpec((1, D), lambda i: (0, 0))],
        out_specs=pl.BlockSpec((tm, D), lambda i: (i, 0)), args=(x, g), semantics=("parallel",), jobs=jobs)


def _rms_bwd_rows(dh, xv, g):
    r = lax.rsqrt(jnp.mean(xv * xv, axis=-1, keepdims=True) + EPS)
    xhat = xv * r
    u = dh * g
    dx = r * (u - xhat * jnp.mean(u * xhat, axis=-1, keepdims=True))
    return dx, dh * xhat


def _rms_bwd(dh, x, g, dres, name):
    T, D = x.shape
    tm = _tile(T, 256, 16)

    def body(dh_ref, x_ref, g_ref, dres_ref, dx_ref, dxb_ref, dg_ref):
        @pl.when(pl.program_id(0) == 0)
        def _():
            dg_ref[...] = jnp.zeros_like(dg_ref)

        dx, dgr = _rms_bwd_rows(dh_ref[...], x_ref[...], g_ref[...])
        dx = dres_ref[...] + dx
        dx_ref[...] = dx
        dxb_ref[...] = dx.astype(dxb_ref.dtype)
        dg_ref[...] += jnp.sum(dgr, axis=0, keepdims=True)

    row = pl.BlockSpec((tm, D), lambda i: (i, 0))
    vec = pl.BlockSpec((1, D), lambda i: (0, 0))
    return pl.pallas_call(
        body, name=name, grid=(T // tm,),
        out_shape=(jax.ShapeDtypeStruct((T, D), F32), jax.ShapeDtypeStruct((T, D), _MXU_DTYPE),
                   jax.ShapeDtypeStruct((1, D), F32)),
        in_specs=[row, row, vec, row], out_specs=(row, row, vec),
        compiler_params=_params("arbitrary"),
    )(dh, x, g, dres)


def _loss_head(x, g, tgt, name):
    T, D = x.shape
    tm = _tile(T, 256, 16)

    def body(x_ref, g_ref, t_ref, dx_ref, dxb_ref, dg_ref, loss_ref):
        @pl.when(pl.program_id(0) == 0)
        def _():
            dg_ref[...] = jnp.zeros_like(dg_ref)
            loss_ref[...] = jnp.zeros_like(loss_ref)

        xv = x_ref[...]
        gv = g_ref[...]
        r = lax.rsqrt(jnp.mean(xv * xv, axis=-1, keepdims=True) + EPS)
        xhat = xv * r
        diff = xhat * gv - t_ref[...]
        tok = jnp.mean(diff * diff, axis=-1, keepdims=True)
        loss_ref[...] += 0.5 * jnp.sum(tok, axis=0, keepdims=True)
        dy = diff / D
        u = dy * gv
        dx = r * (u - xhat * jnp.mean(u * xhat, axis=-1, keepdims=True))
        dx_ref[...] = dx
        dxb_ref[...] = dx.astype(dxb_ref.dtype)
        dg_ref[...] += jnp.sum(dy * xhat, axis=0, keepdims=True)

    row = pl.BlockSpec((tm, D), lambda i: (i, 0))
    vec = pl.BlockSpec((1, D), lambda i: (0, 0))
    return pl.pallas_call(
        body, name=name, grid=(T // tm,),
        out_shape=(jax.ShapeDtypeStruct((T, D), F32), jax.ShapeDtypeStruct((T, D), _MXU_DTYPE),
                   jax.ShapeDtypeStruct((1, D), F32), jax.ShapeDtypeStruct((1, LANES), F32)),
        in_specs=[row, vec, row],
        out_specs=(row, row, vec, pl.BlockSpec((1, LANES), lambda i: (0, 0))),
        compiler_params=_params("arbitrary"),
    )(x, g, tgt)


def _outnorm_fwd(ya, yb, ga, gb, name):
    T, W = ya.shape
    tm = _tile(T, 512, 16)

    def body(ya_ref, yb_ref, ga_ref, gb_ref, o_ref):
        for k, (y_ref, g_ref) in enumerate(((ya_ref, ga_ref), (yb_ref, gb_ref))):
            yv = y_ref[...]
            r = lax.rsqrt(jnp.mean(yv * yv, axis=-1, keepdims=True) + EPS)
            o_ref[:, k * W:(k + 1) * W] = (yv * r * g_ref[...]).astype(o_ref.dtype)

    row = pl.BlockSpec((tm, W), lambda i: (i, 0))
    vec = pl.BlockSpec((1, W), lambda i: (0, 0))
    return pl.pallas_call(
        body, name=name, grid=(T // tm,),
        out_shape=jax.ShapeDtypeStruct((T, 2 * W), _MXU_DTYPE),
        in_specs=[row, row, vec, vec], out_specs=pl.BlockSpec((tm, 2 * W), lambda i: (i, 0)),
        compiler_params=_params("parallel"),
    )(ya, yb, ga, gb)


def _outnorm_bwd(dyn, ya, yb, ga, gb, name):
    T, W = ya.shape
    tm = _tile(T, 256, 16)

    def body(d_ref, ya_ref, yb_ref, ga_ref, gb_ref, dya_ref, dyb_ref, dga_ref, dgb_ref):
        @pl.when(pl.program_id(0) == 0)
        def _():
            dga_ref[...] = jnp.zeros_like(dga_ref)
            dgb_ref[...] = jnp.zeros_like(dgb_ref)

        for k, (y_ref, g_ref, dy_ref, dg_ref) in enumerate(
                ((ya_ref, ga_ref, dya_ref, dga_ref), (yb_ref, gb_ref, dyb_ref, dgb_ref))):
            dy, dgr = _rms_bwd_rows(d_ref[:, k * W:(k + 1) * W], y_ref[...], g_ref[...])
            dy_ref[...] = dy.astype(dy_ref.dtype)
            dg_ref[...] += jnp.sum(dgr, axis=0, keepdims=True)

    row = pl.BlockSpec((tm, W), lambda i: (i, 0))
    vec = pl.BlockSpec((1, W), lambda i: (0, 0))
    return pl.pallas_call(
        body, name=name, grid=(T // tm,),
        out_shape=(jax.ShapeDtypeStruct((T, W), _MXU_DTYPE), jax.ShapeDtypeStruct((T, W), _MXU_DTYPE),
                   jax.ShapeDtypeStruct((1, W), F32), jax.ShapeDtypeStruct((1, W), F32)),
        in_specs=[pl.BlockSpec((tm, 2 * W), lambda i: (i, 0)), row, row, vec, vec],
        out_specs=(row, row, vec, vec),
        compiler_params=_params("arbitrary"),
    )(dyn, ya, yb, ga, gb)


_STRIP_ROWS = 256


def _strips(rows):
    step = min(rows, _STRIP_ROWS)
    return [slice(r, r + step) for r in range(0, rows, step)]


def _ffn_gu(h, wg_t, wu_t, name, jobs=()):
    T, D = h.shape
    Fp = wg_t.shape[0]
    tm = _tile(T, 1024, 16)
    tn = _tile(Fp, _F_TILE, LANES)

    def body(h_ref, wg_ref, wu_ref, g_ref, u_ref, a_ref):
        for rows in _strips(tm):
            hv = h_ref[rows, :]
            g = _dot_nt(hv, wg_ref[...])
            u = _dot_nt(hv, wu_ref[...])
            g_ref[rows, :] = g.astype(g_ref.dtype)
            u_ref[rows, :] = u.astype(u_ref.dtype)
            a_ref[rows, :] = (g * jax.nn.sigmoid(g) * u).astype(a_ref.dtype)

    act = pl.BlockSpec((tm, tn), lambda n, m: (m, n))
    wsp = pl.BlockSpec((tn, D), lambda n, m: (n, 0))
    out = jax.ShapeDtypeStruct((T, Fp), _MXU_DTYPE)
    return _call(
        body, name=name, grid=(Fp // tn, T // tm), out_shape=(out, out, out),
        in_specs=[pl.BlockSpec((tm, D), lambda n, m: (m, 0)), wsp, wsp],
        out_specs=(act, act, act), args=(h, wg_t, wu_t), semantics=("parallel", "parallel"), jobs=jobs)


def _ffn_bwd_act(dxb, wd, G, U, name, jobs=()):
    T, D = dxb.shape
    Fp = wd.shape[0]
    tm = _tile(T, 1024, 16)
    tn = _tile(Fp, _F_TILE, LANES)

    def body(e_ref, wd_ref, g_ref, u_ref, dg_ref, du_ref):
        for rows in _strips(tm):
            da = 0.5 * _dot_nt(e_ref[rows, :], wd_ref[...])
            g = g_ref[rows, :].astype(F32)
            u = u_ref[rows, :].astype(F32)
            s = jax.nn.sigmoid(g)
            du_ref[rows, :] = (da * (g * s)).astype(du_ref.dtype)
            dg_ref[rows, :] = (da * u * (s * (1.0 + g * (1.0 - s)))).astype(dg_ref.dtype)

    act = pl.BlockSpec((tm, tn), lambda m, n: (m, n))
    out = jax.ShapeDtypeStruct((T, Fp), _MXU_DTYPE)
    return _call(
        body, name=name, grid=(T // tm, Fp // tn), out_shape=(out, out),
        in_specs=[pl.BlockSpec((tm, D), lambda m, n: (m, 0)), pl.BlockSpec((tn, D), lambda m, n: (n, 0)),
                  act, act],
        out_specs=(act, act), args=(dxb, wd, G, U), semantics=("parallel", "parallel"), jobs=jobs)


def _ffn_gate(h, wg_t, U, name, jobs=()):
    T, D = h.shape
    Fp = wg_t.shape[0]
    tm = _tile(T, 1024, 16)
    tn = _tile(Fp, _F_TILE, LANES)

    def body(h_ref, wg_ref, u_ref, g_ref, a_ref):
        for rows in _strips(tm):
            g = _dot_nt(h_ref[rows, :], wg_ref[...])
            g_ref[rows, :] = g.astype(g_ref.dtype)
            a_ref[rows, :] = (g * jax.nn.sigmoid(g) * u_ref[rows, :].astype(F32)).astype(a_ref.dtype)

    act = pl.BlockSpec((tm, tn), lambda m, n: (m, n))
    out = jax.ShapeDtypeStruct((T, Fp), _MXU_DTYPE)
    return _call(
        body, name=name, grid=(T // tm, Fp // tn), out_shape=(out, out),
        in_specs=[pl.BlockSpec((tm, D), lambda m, n: (m, 0)), pl.BlockSpec((tn, D), lambda m, n: (n, 0)), act],
        out_specs=(act, act), args=(h, wg_t, U), semantics=("parallel", "parallel"), jobs=jobs)


def _mm_nt(a, b, out_dtype, name, jobs=()):
    M, K = a.shape
    N = b.shape[0]
    tm = _tile(M, 1024, 16)
    tn = _tile(N, 512, LANES)

    def body(a_ref, b_ref, o_ref):
        o_ref[...] = _dot_nt(a_ref[...], b_ref[...]).astype(o_ref.dtype)

    return _call(
        body, name=name, grid=(M // tm, N // tn), out_shape=jax.ShapeDtypeStruct((M, N), out_dtype),
        in_specs=[pl.BlockSpec((tm, K), lambda m, n: (m, 0)), pl.BlockSpec((tn, K), lambda m, n: (n, 0))],
        out_specs=pl.BlockSpec((tm, tn), lambda m, n: (m, n)), args=(a, b),
        semantics=("parallel", "parallel"), jobs=jobs)


_MM_OPERAND_BYTES = 26 * 1024 * 1024


def _k_tile(K, bytes_per_k, align):
    best = align
    for t in range(align, K + 1, align):
        if K % t == 0 and 2 * t * bytes_per_k <= _MM_OPERAND_BYTES:
            best = t
    return best


def _mm_nn(pairs, res, alpha, out_dtype, name, jobs=()):
    M, K = pairs[0][0].shape
    N = pairs[0][1].shape[1]
    n_pairs = len(pairs)
    tm = _tile(M, 1024, 16)
    tn = _tile(N, 1024, LANES)
    tk = _k_tile(K, n_pairs * (tm + tn) * pairs[0][0].dtype.itemsize, LANES)
    nk = K // tk

    def body(*refs):
        ab = refs[:2 * n_pairs]
        res_ref = refs[2 * n_pairs] if res is not None else None
        o_ref = refs[2 * n_pairs + (res is not None)]

        def finish(acc):
            out = alpha * acc
            if res_ref is not None:
                out = res_ref[...] + out
            o_ref[...] = out.astype(o_ref.dtype)

        part = _dot(ab[0][...], ab[1][...])
        for i in range(1, n_pairs):
            part = part + _dot(ab[2 * i][...], ab[2 * i + 1][...])
        if nk == 1:
            finish(part)
        else:
            acc_ref = refs[-1]
            k = pl.program_id(2)

            @pl.when(k == 0)
            def _():
                acc_ref[...] = part

            @pl.when(k > 0)
            def _():
                acc_ref[...] += part

            @pl.when(k == nk - 1)
            def _():
                finish(acc_ref[...])

    in_specs, args = [], []
    for a, b in pairs:
        in_specs += [pl.BlockSpec((tm, tk), lambda m, n, k: (m, k)), pl.BlockSpec((tk, tn), lambda m, n, k: (k, n))]
        args += [a, b]
    if res is not None:
        in_specs.append(pl.BlockSpec((tm, tn), lambda m, n, k: (m, n)))
        args.append(res)
    return _call(
        body, name=name, grid=(M // tm, N // tn, nk), out_shape=jax.ShapeDtypeStruct((M, N), out_dtype),
        in_specs=in_specs, out_specs=pl.BlockSpec((tm, tn), lambda m, n, k: (m, n)),
        scratch_shapes=[pltpu.VMEM((tm, tn), F32)] if nk > 1 else [], args=args,
        semantics=("parallel", "parallel", "arbitrary"), jobs=jobs)


def _mm_tn(a, b, alpha, out_dtype, name, jobs=()):
    K, M = a.shape
    N = b.shape[1]
    tm = _tile(M, 512, LANES)
    tn = _tile(N, 1024, LANES)

    def body(a_ref, b_ref, o_ref):
        o_ref[...] = (alpha * _dot_tn(a_ref[...], b_ref[...])).astype(o_ref.dtype)

    return _call(
        body, name=name, grid=(N // tn, M // tm), out_shape=jax.ShapeDtypeStruct((M, N), out_dtype),
        in_specs=[pl.BlockSpec((K, tm), lambda n, m: (0, m)), pl.BlockSpec((K, tn), lambda n, m: (0, n))],
        out_specs=pl.BlockSpec((tm, tn), lambda n, m: (m, n)), args=(a, b),
        semantics=("parallel", "parallel"), jobs=jobs)


def _half_masks():
    lane = lax.broadcasted_iota(jnp.int32, (1, LANES), 1)
    return (lane < HEAD_DIM, lane >= HEAD_DIM)


def _swap_halves(v):
    return pltpu.roll(v.astype(F32), HEAD_DIM, 1).astype(v.dtype)


def _swa_geometry(n):
    qi = lax.broadcasted_iota(jnp.int32, (WINDOW, 2 * WINDOW), 0)
    kp = lax.broadcasted_iota(jnp.int32, (WINDOW, 2 * WINDOW), 1)
    dist = (WINDOW + qi) - kp
    valid = (dist >= 0) & (dist < WINDOW) & ((n > 0) | (kp >= WINDOW))
    return dist.astype(F32), valid


def _swa_slope(h):
    return 2.0 ** (-8.0 * (h + 1) / N_SWA_HEADS)


def _swa_softmax(qk, sink, slope, distf, valid):
    s = qk * ATT_SCALE - slope * distf
    s = jnp.where(valid, s, -1e30)
    m = jnp.maximum(jnp.max(s, axis=1, keepdims=True), sink)
    p = jnp.exp(s - m)
    e_sink = jnp.exp(sink - m)
    den = jnp.sum(p, axis=1, keepdims=True) + e_sink
    return p / den, e_sink / den


def _swa_group_heads(g):
    return [(2 * pp + a, pp, a) for pp in (2 * g, 2 * g + 1) for a in range(2)]


def _swa_specs(B, S):
    nb = S // WINDOW
    kcol = SWA_Q // SWA_KV
    cur = lambda b, n: (b * nb + n, kcol)
    prev = lambda b, n: (b * nb + jnp.maximum(n - 1, 0), kcol)
    curv = lambda b, n: (b * nb + n, kcol + 1)
    prevv = lambda b, n: (b * nb + jnp.maximum(n - 1, 0), kcol + 1)
    q_spec = pl.BlockSpec((WINDOW, SWA_Q), lambda b, n: (b * nb + n, 0))
    kv = [pl.BlockSpec((WINDOW, SWA_KV), f) for f in (prev, cur, prevv, curv)]
    sink_spec = pl.BlockSpec(memory_space=pltpu.SMEM)
    return nb, q_spec, kv, sink_spec


def _swa_kv_views(kp_ref, kc_ref, vp_ref, vc_ref, g):
    hm = _half_masks()
    c0 = (g // 2) * LANES
    k_all = jnp.concatenate([kp_ref[:, c0:c0 + LANES], kc_ref[:, c0:c0 + LANES]], axis=0)
    v_all = jnp.concatenate([vp_ref[:, c0:c0 + LANES], vc_ref[:, c0:c0 + LANES]], axis=0)
    b = g % 2
    ks, vs = [None, None], [None, None]
    ks[b], vs[b] = k_all, v_all
    ks[1 - b], vs[1 - b] = _swap_halves(k_all), _swap_halves(v_all)
    ks = [jnp.where(hm[a], ks[a], 0) for a in range(2)]
    vs = [jnp.where(hm[a], vs[a], 0) for a in range(2)]
    return ks, vs


def _swa_fwd(proj, sinks, B, S, name, jobs=()):
    T = B * S
    nb, q_spec, kv_specs, sink_spec = _swa_specs(B, S)

    def body(sink_ref, q_ref, kp_ref, kc_ref, vp_ref, vc_ref, y_ref):
        hm = _half_masks()
        distf, valid = _swa_geometry(pl.program_id(1))
        for g in range(N_SWA_KV):
            ks, vs = _swa_kv_views(kp_ref, kc_ref, vp_ref, vc_ref, g)
            heads = _swa_group_heads(g)
            qk = [_dot_nt(jnp.where(hm[a], q_ref[:, pp * LANES:(pp + 1) * LANES], 0), ks[a]) for _, pp, a in heads]
            p = [_swa_softmax(qk[i], sink_ref[0, h], _swa_slope(h), distf, valid)[0] for i, (h, _, _) in enumerate(heads)]
            o = [_dot(p[i].astype(_MXU_DTYPE), vs[a]) for i, (_, _, a) in enumerate(heads)]
            for j, pp in enumerate((2 * g, 2 * g + 1)):
                y_ref[:, pp * LANES:(pp + 1) * LANES] = o[2 * j] + o[2 * j + 1]

    return _call(
        body, name=name, grid=(B, nb), out_shape=jax.ShapeDtypeStruct((T, SWA_Q), F32),
        in_specs=[sink_spec, q_spec] + kv_specs,
        out_specs=pl.BlockSpec((WINDOW, SWA_Q), lambda b, n: (b * nb + n, 0)),
        args=(sinks, proj, proj, proj, proj, proj), semantics=("parallel", "parallel"), jobs=jobs)


def _swa_bwd(proj, sinks, dya, B, S, name, jobs=()):
    T = B * S
    nb, q_spec, kv_specs, sink_spec = _swa_specs(B, S)

    def body(sink_ref, q_ref, kp_ref, kc_ref, vp_ref, vc_ref, do_ref,
             dq_ref, dk_ref, dv_ref, dsink_ref, dk_acc, dv_acc):
        b_id, n = pl.program_id(0), pl.program_id(1)
        hm = _half_masks()
        lane = lax.broadcasted_iota(jnp.int32, (1, LANES), 1)

        @pl.when((b_id == 0) & (n == 0))
        def _():
            dsink_ref[...] = jnp.zeros_like(dsink_ref)

        @pl.when(n == 0)
        def _():
            dk_acc[...] = jnp.zeros_like(dk_acc)
            dv_acc[...] = jnp.zeros_like(dv_acc)

        distf, valid = _swa_geometry(n)
        r_prev = pl.multiple_of(jnp.maximum(n - 1, 0) * WINDOW, WINDOW)
        r_cur = pl.multiple_of(n * WINDOW, WINDOW)
        dsink = jnp.zeros((1, LANES), F32)
        for g in range(N_SWA_KV):
            ks, vs = _swa_kv_views(kp_ref, kc_ref, vp_ref, vc_ref, g)
            heads = _swa_group_heads(g)
            four = range(len(heads))
            qms = [jnp.where(hm[a], q_ref[:, pp * LANES:(pp + 1) * LANES], 0) for _, pp, a in heads]
            doms = [jnp.where(hm[a], do_ref[:, pp * LANES:(pp + 1) * LANES], 0) for _, pp, a in heads]
            qk = [_dot_nt(qms[i], ks[heads[i][2]]) for i in four]
            dp = [_dot_nt(doms[i], vs[heads[i][2]]) for i in four]
            soft = [_swa_softmax(qk[i], sink_ref[0, heads[i][0]], _swa_slope(heads[i][0]), distf, valid) for i in four]
            p = [soft[i][0] for i in four]
            delta = [jnp.sum(p[i] * dp[i], axis=1, keepdims=True) for i in four]
            ds = [(p[i] * (dp[i] - delta[i]) * ATT_SCALE).astype(_MXU_DTYPE) for i in four]
            for i in four:
                dsink = dsink + jnp.where(lane == heads[i][0], -jnp.sum(soft[i][1] * delta[i]), 0.0)
            dq = [_dot(ds[i], ks[heads[i][2]]) for i in four]
            dk_h = [_dot_tn(ds[i], qms[i]) for i in four]
            dv_h = [_dot_tn(p[i].astype(_MXU_DTYPE), doms[i]) for i in four]
            for j, pp in enumerate((2 * g, 2 * g + 1)):
                dq_ref[:, pp * LANES:(pp + 1) * LANES] = (dq[2 * j] + dq[2 * j + 1]).astype(dq_ref.dtype)
            dk_g = [dk_h[a] + dk_h[2 + a] for a in range(2)]
            dv_g = [dv_h[a] + dv_h[2 + a] for a in range(2)]
            bsel = g % 2
            dk_t = dk_g[bsel] + pltpu.roll(dk_g[1 - bsel], HEAD_DIM, 1)
            dv_t = dv_g[bsel] + pltpu.roll(dv_g[1 - bsel], HEAD_DIM, 1)
            c0 = (g // 2) * LANES
            dk_acc[pl.ds(r_prev, WINDOW), c0:c0 + LANES] += dk_t[:WINDOW]
            dk_acc[pl.ds(r_cur, WINDOW), c0:c0 + LANES] += dk_t[WINDOW:]
            dv_acc[pl.ds(r_prev, WINDOW), c0:c0 + LANES] += dv_t[:WINDOW]
            dv_acc[pl.ds(r_cur, WINDOW), c0:c0 + LANES] += dv_t[WINDOW:]
        dsink_ref[...] += dsink

        @pl.when(n == nb - 1)
        def _():
            dk_ref[...] = dk_acc[...].astype(dk_ref.dtype)
            dv_ref[...] = dv_acc[...].astype(dv_ref.dtype)

    seq_kv = pl.BlockSpec((S, SWA_KV), lambda b, n: (b, 0))
    return _call(
        body, name=name, grid=(B, nb),
        out_shape=(jax.ShapeDtypeStruct((T, SWA_Q), _MXU_DTYPE), jax.ShapeDtypeStruct((T, SWA_KV), _MXU_DTYPE),
                   jax.ShapeDtypeStruct((T, SWA_KV), _MXU_DTYPE), jax.ShapeDtypeStruct((1, LANES), F32)),
        in_specs=[sink_spec, q_spec] + kv_specs + [pl.BlockSpec((WINDOW, SWA_Q), lambda b, n: (b * nb + n, 0))],
        out_specs=(pl.BlockSpec((WINDOW, SWA_Q), lambda b, n: (b * nb + n, 0)), seq_kv, seq_kv,
                   pl.BlockSpec((1, LANES), lambda b, n: (0, 0))),
        scratch_shapes=[pltpu.VMEM((S, SWA_KV), F32), pltpu.VMEM((S, SWA_KV), F32)],
        args=(sinks, proj, proj, proj, proj, proj, dya), semantics=("arbitrary", "arbitrary"), jobs=jobs)


SB_TILE = 256
SB_HALF = 128
SB_DEAD = -105.0


def _mark_lanes():
    lane = lax.broadcasted_iota(jnp.int32, (1, LANES), 1)
    return (lane == HEAD_DIM - 1) | (lane == LANES - 1)


def _tri2(cond):
    j = lax.broadcasted_iota(jnp.int32, (2 * SB_HALF, SB_HALF), 0) & (SB_HALF - 1)
    s = lax.broadcasted_iota(jnp.int32, (2 * SB_HALF, SB_HALF), 1)
    return cond(j, s).astype(_MXU_DTYPE)


def _half_cumsums(x, tri2):
    out = []
    for h in range(2):
        xh = x[:, h * SB_HALF:(h + 1) * SB_HALF]
        hi = xh.astype(_MXU_DTYPE)
        lo = (xh - hi.astype(F32)).astype(_MXU_DTYPE)
        out.append(_dot(jnp.concatenate([hi, lo], axis=1), tri2))
    return out


def _log_sigmoid(z):
    return jnp.minimum(z, 0.0) - jnp.log(1.0 + jnp.exp(-jnp.abs(z)))


def _sb_specs(B, S):
    qb = (SWA_Q + 2 * SWA_KV) // LANES
    kb = qb + SB_W // LANES
    vb = kb + SB_W // LANES
    return [pl.BlockSpec((S, LANES), functools.partial(lambda b, p, c: (b, c + p), c=c)) for c in (qb, kb, vb)]


def _sb_fwd(proj, B, S, name, jobs=()):
    T = B * S
    tq = SB_TILE
    nq = S // tq

    def body(q_ref, k_ref, v_ref, y_ref, tot_ref):
        hm = _half_masks()
        ji = lax.broadcasted_iota(jnp.int32, (tq, tq), 0)
        si = lax.broadcasted_iota(jnp.int32, (tq, tq), 1)
        tri_after = _tri2(lambda j, s: j > s)
        causal = si < ji
        mark = _mark_lanes()

        def q_loop(qi, carry):
            r0 = pl.multiple_of(qi * tq, tq)
            q_pair = q_ref[pl.ds(r0, tq), :] * ATT_SCALE
            qms = [jnp.where(hm[a], q_pair, 0) for a in range(2)]

            def tile(c0, state, diagonal):
                kk = k_ref[pl.ds(c0, tq), :]
                vv = v_ref[pl.ds(c0, tq), :]
                two = range(2)
                z = [_dot_nt(qms[a], kk) for a in two]
                lb = [_log_sigmoid(z[a]) for a in two]
                l1m = [jnp.where(causal, lb[a] - z[a], 0.0) if diagonal else lb[a] - z[a] for a in two]
                cum = [_half_cumsums(l1m[a], tri_after) for a in two]
                tot = [[cum[a][h][:, 0:1] + l1m[a][:, h * SB_HALF:h * SB_HALF + 1] for h in two] for a in two]
                after = [jnp.concatenate([cum[a][0] + (state[a][1] + tot[a][1]), cum[a][1] + state[a][1]], axis=1)
                         for a in two]
                att = [jnp.exp(lb[a] + after[a]) for a in two]
                if diagonal:
                    att = [jnp.where(causal, att[a], 0.0) for a in two]
                acc = [state[a][0] + _dot(att[a].astype(_MXU_DTYPE), jnp.where(hm[a], vv, 0)) for a in two]
                car = [state[a][1] + (tot[a][0] + tot[a][1]) for a in two]
                return tuple((acc[a], car[a]) for a in two)

            def live(st):
                return jnp.maximum(jnp.max(st[0][1]), jnp.max(st[1][1])) > SB_DEAD

            def step(c):
                it, _, st = c
                st = tile(pl.multiple_of((qi - 1 - it) * tq, tq), st, False)
                return it + 1, live(st), st

            zero = (jnp.zeros((tq, LANES), F32), jnp.zeros((tq, 1), F32))
            state = tile(r0, (zero, zero), True)
            done, _, state = lax.while_loop(lambda c: (c[0] < qi) & c[1], step, (jnp.int32(0), live(state), state))
            y_ref[pl.ds(r0, tq), :] = state[0][0] + state[1][0]
            first = (qi - done).astype(F32)
            tot_ref[pl.ds(r0, tq), :] = jnp.where(mark, first, jnp.where(hm[0], state[0][1], state[1][1]))
            return carry

        lax.fori_loop(0, nq, q_loop, 0)

    out_spec = pl.BlockSpec((S, LANES), lambda b, p: (b, p))
    return _call(
        body, name=name, grid=(B, SB_W // LANES),
        out_shape=(jax.ShapeDtypeStruct((T, SB_W), F32), jax.ShapeDtypeStruct((T, SB_W), F32)),
        in_specs=_sb_specs(B, S), out_specs=(out_spec, out_spec), args=(proj, proj, proj),
        semantics=("parallel", "parallel"), jobs=jobs)


def _sb_bwd(proj, tot, dyb, B, S, name, jobs=()):
    T = B * S
    tq = SB_TILE
    nq = S // tq

    def body(q_ref, k_ref, v_ref, do_ref, tot_ref, dq_ref, dk_ref, dv_ref, dk_acc, dv_acc):
        hm = _half_masks()
        ji = lax.broadcasted_iota(jnp.int32, (tq, tq), 0)
        si = lax.broadcasted_iota(jnp.int32, (tq, tq), 1)
        tri_incl = _tri2(lambda j, s: j <= s)
        tri_excl = _tri2(lambda j, s: j < s)
        causal = si < ji
        mark = _mark_lanes()
        dk_acc[...] = jnp.zeros_like(dk_acc)
        dv_acc[...] = jnp.zeros_like(dv_acc)

        def q_loop(qi, carry):
            r0 = pl.multiple_of(qi * tq, tq)
            q_pair = q_ref[pl.ds(r0, tq), :] * ATT_SCALE
            do_pair = do_ref[pl.ds(r0, tq), :]
            tot_pair = tot_ref[pl.ds(r0, tq), :]
            qms = [jnp.where(hm[a], q_pair, 0) for a in range(2)]
            doms = [jnp.where(hm[a], do_pair, 0) for a in range(2)]
            totals = [jnp.max(jnp.where(hm[a] & ~mark, tot_pair, -jnp.inf), axis=1, keepdims=True) for a in range(2)]
            first = jnp.max(jnp.where(mark, tot_pair, -jnp.inf))
            first = jnp.where((first >= 0.0) & (first <= qi.astype(F32)), first, 0.0).astype(jnp.int32)

            def tile(c0, state, diagonal):
                kk = k_ref[pl.ds(c0, tq), :]
                vv = v_ref[pl.ds(c0, tq), :]
                ks = kk * ATT_SCALE
                two = range(2)
                last = SB_HALF - 1
                z = [_dot_nt(qms[a], kk) for a in two]
                d_att = [_dot_nt(doms[a], vv) for a in two]
                lb = [_log_sigmoid(z[a]) for a in two]
                l1m = [jnp.where(causal, lb[a] - z[a], 0.0) if diagonal else lb[a] - z[a] for a in two]
                cum = [_half_cumsums(l1m[a], tri_incl) for a in two]
                upto = [jnp.concatenate([cum[a][0] + state[a][1],
                                         cum[a][1] + (state[a][1] + cum[a][0][:, last:last + 1])], axis=1) for a in two]
                att = [jnp.exp(lb[a] + (totals[a] - upto[a])) for a in two]
                if diagonal:
                    att = [jnp.where(causal, att[a], 0.0) for a in two]
                d_log = [d_att[a] * att[a] for a in two]
                cumd = [_half_cumsums(d_log[a], tri_excl) for a in two]
                totd = [[cumd[a][h][:, last:last + 1] + d_log[a][:, h * SB_HALF + last:h * SB_HALF + last + 1]
                         for h in two] for a in two]
                before = [jnp.concatenate([cumd[a][0] + state[a][2], cumd[a][1] + (state[a][2] + totd[a][0])], axis=1)
                          for a in two]
                sig = [jnp.exp(lb[a]) for a in two]
                dz = [d_log[a] * (1.0 - sig[a]) - sig[a] * before[a] for a in two]
                if diagonal:
                    dz = [jnp.where(causal, dz[a], 0.0) for a in two]
                dzb = [dz[a].astype(_MXU_DTYPE) for a in two]
                dq = [state[a][0] + _dot(dzb[a], jnp.where(hm[a], ks, 0)) for a in two]
                dk_acc[pl.ds(c0, tq), :] += _dot_tn(dzb[0], qms[0]) + _dot_tn(dzb[1], qms[1])
                dv_acc[pl.ds(c0, tq), :] += (_dot_tn(att[0].astype(_MXU_DTYPE), doms[0])
                                             + _dot_tn(att[1].astype(_MXU_DTYPE), doms[1]))
                cp = [upto[a][:, tq - 1:tq] for a in two]
                cq = [state[a][2] + (totd[a][0] + totd[a][1]) for a in two]
                return tuple((dq[a], cp[a], cq[a]) for a in two)

            zero_col = jnp.zeros((tq, 1), F32)
            zero = (jnp.zeros((tq, LANES), F32), zero_col, zero_col)
            state = lax.fori_loop(first, qi, lambda kj, st: tile(pl.multiple_of(kj * tq, tq), st, False), (zero, zero))
            state = tile(r0, state, True)
            dq_ref[pl.ds(r0, tq), :] = (state[0][0] + state[1][0]).astype(dq_ref.dtype)
            return carry

        lax.fori_loop(0, nq, q_loop, 0)
        dk_ref[...] = dk_acc[...].astype(dk_ref.dtype)
        dv_ref[...] = dv_acc[...].astype(dv_ref.dtype)

    pair = pl.BlockSpec((S, LANES), lambda b, p: (b, p))
    out = jax.ShapeDtypeStruct((T, SB_W), _MXU_DTYPE)
    return _call(
        body, name=name, grid=(B, SB_W // LANES), out_shape=(out, out, out),
        in_specs=_sb_specs(B, S) + [pair, pair], out_specs=(pair, pair, pair),
        scratch_shapes=[pltpu.VMEM((S, LANES), F32), pltpu.VMEM((S, LANES), F32)],
        args=(proj, proj, proj, dyb, tot), semantics=("parallel", "parallel"), jobs=jobs)


def _layer_step(x, tgt, B, S, small, comm):
    run, big, part = comm.run, comm.big, comm.partial
    ffn1_w, ffn2_w = ("ffn1_down", "ffn1_gate", "ffn1_up"), ("ffn2_down", "ffn2_gate", "ffn2_up")

    h1 = run(_rms_fwd, x, small["ffn1_norm"], "ffn1_rms", ag=(FIRST_GATHERED,),
             cast=tuple(n for n in BIG_NAMES if n != FIRST_GATHERED))
    U1 = run(_mm_nt, h1, big["ffn1_up"], _MXU_DTYPE, "ffn1_up", ag=("ffn1_gate",))
    G1, A1 = run(_ffn_gate, h1, big["ffn1_gate"], U1, "ffn1_gate", ag=("ffn1_down",))
    x1 = run(_mm_nn, [(A1, big["ffn1_down"])], x, 0.5, F32, "ffn1_down", ag=("w_in",))
    h2 = run(_rms_fwd, x1, small["mix_norm"], "mix_rms")
    proj = run(_mm_nt, h2, big["w_in"], _MXU_DTYPE, "in_proj", ag=("w_out",))
    ya = run(_swa_fwd, proj, small["swa_sinks"], B, S, "swa_fwd", ag=("ffn2_gate",))
    yb, tot = run(_sb_fwd, proj, B, S, "sb_fwd", ag=("ffn2_up",))
    yn = _outnorm_fwd(ya, yb, small["swa_out_norm"], small["sb_out_norm"], "out_norm")
    x2 = run(_mm_nn, [(yn, big["w_out"])], x1, 1.0, F32, "out_proj")
    h3 = run(_rms_fwd, x2, small["ffn2_norm"], "ffn2_rms")
    G2, U2, A2 = run(_ffn_gu, h3, big["ffn2_gate"], big["ffn2_up"], "ffn2_gate_up", ag=("ffn2_down",))
    x3 = run(_mm_nn, [(A2, big["ffn2_down"])], x2, 0.5, F32, "ffn2_down")

    dx3, dx3b, d_final, loss = _loss_head(x3, small["final_norm"], tgt, "loss_head")

    dG2, dU2 = run(_ffn_bwd_act, dx3b, big["ffn2_down"], G2, U2, "ffn2_bwd_act")
    part["ffn2_down"] = run(_mm_tn, A2, dx3b, 0.5, _WIRE_DTYPE, "ffn2_dw_down")
    part["ffn2_gate"] = run(_mm_tn, dG2, h3, 1.0, _WIRE_DTYPE, "ffn2_dw_gate")
    part["ffn2_up"] = run(_mm_tn, dU2, h3, 1.0, _WIRE_DTYPE, "ffn2_dw_up")
    dh3 = run(_mm_nn, [(dG2, big["ffn2_gate"])], None, 1.0, F32, "ffn2_dh_gate", rs1=ffn2_w)
    dh3 = run(_mm_nn, [(dU2, big["ffn2_up"])], dh3, 1.0, F32, "ffn2_dh_up")
    dx2, dx2b, d_g2 = _rms_bwd(dh3, x2, small["ffn2_norm"], dx3, "ffn2_rms_bwd")

    part["w_out"] = run(_mm_tn, yn, dx2b, 1.0, _WIRE_DTYPE, "dw_out")
    dyn = run(_mm_nt, dx2b, big["w_out"], F32, "out_proj_bwd")
    dya, dyb, d_ga, d_gb = _outnorm_bwd(dyn, ya, yb, small["swa_out_norm"], small["sb_out_norm"], "out_norm_bwd")
    dqa, dka, dva, d_sinks = run(_swa_bwd, proj, small["swa_sinks"], dya, B, S, "swa_bwd", rs2=ffn2_w[:1])
    dqb, dkb, dvb = run(_sb_bwd, proj, tot, dyb, B, S, "sb_bwd", rs2=ffn2_w[1:])
    dproj = jnp.concatenate([dqa, dka, dva, dqb, dkb, dvb], axis=1)
    part["w_in"] = run(_mm_tn, dproj, h2, 1.0, _WIRE_DTYPE, "dw_in", adamw=("ffn2_down", "ffn2_gate"))
    dh2 = run(_mm_nn, [(dproj, big["w_in"])], None, 1.0, F32, "in_proj_bwd", rs1=("w_in", "w_out"))
    dx1, dx1b, d_gm = _rms_bwd(dh2, x1, small["mix_norm"], dx2, "mix_rms_bwd")

    dG1, dU1 = run(_ffn_bwd_act, dx1b, big["ffn1_down"], G1, U1, "ffn1_bwd_act", rs2=("w_in", "w_out"))
    part["ffn1_down"] = run(_mm_tn, A1, dx1b, 0.5, _WIRE_DTYPE, "ffn1_dw_down", adamw=("ffn2_up", "w_in", "w_out"))
    part["ffn1_gate"] = run(_mm_tn, dG1, h1, 1.0, _WIRE_DTYPE, "ffn1_dw_gate", rs1=("ffn1_down",))
    part["ffn1_up"] = run(_mm_tn, dU1, h1, 1.0, _WIRE_DTYPE, "ffn1_dw_up", rs1=("ffn1_gate",), rs2=("ffn1_down",))
    dh1 = run(_mm_nn, [(dG1, big["ffn1_gate"])], None, 1.0, F32, "ffn1_dh_gate", rs1=("ffn1_up",), rs2=("ffn1_gate",))
    dh1 = run(_mm_nn, [(dU1, big["ffn1_up"])], dh1, 1.0, F32, "ffn1_dh_up", rs2=("ffn1_up",))
    gx, _, d_g1 = _rms_bwd(dh1, x, small["ffn1_norm"], dx1, "ffn1_rms_bwd")

    d_small = {"ffn1_norm": d_g1, "mix_norm": d_gm, "swa_sinks": d_sinks[:, :N_SWA_HEADS], "swa_out_norm": d_ga,
               "sb_out_norm": d_gb, "ffn2_norm": d_g2, "final_norm": d_final}
    return loss, gx, d_small


MESH = pl.DeviceIdType.MESH
BIG_NAMES = ("ffn1_gate", "ffn1_up", "ffn1_down", "w_in", "w_out", "ffn2_gate", "ffn2_up", "ffn2_down")
FIRST_GATHERED = "ffn1_up"
_COMM_PARAMS = pltpu.CompilerParams(has_side_effects=True)


def _place():
    x, y, c = lax.axis_index("x"), lax.axis_index("y"), lax.axis_index("c")
    other_chips = [(1 - x, y), (x, 1 - y), (1 - x, 1 - y)]
    return x, y, c, other_chips


def _padded_rows(rows):
    full = N_DEV * rows
    return -(-full // _F_TILE) * _F_TILE


AG_PARTS = 4


def _row_parts(rows, n):
    units = rows // 16
    assert units * 16 == rows and units >= n
    out, off = [], 0
    for i in range(n):
        size = (units // n + (1 if i < units % n else 0)) * 16
        out.append((off, size))
        off += size
    return out


def _ag_job(shards):
    nw = len(shards)
    D = shards[0].shape[1]
    rows_w = [s.shape[0] for s in shards]
    full_w = [_padded_rows(r) for r in rows_w]
    pad_w = [f - N_DEV * r for f, r in zip(full_w, rows_w)]
    max_pad = max(max(pad_w), 16)
    n_parts = AG_PARTS
    parts_w = [_row_parts(r, n_parts) for r in rows_w]

    class Plan:
        def __init__(self, ins, outs, scratch):
            zbuf, send_sems, recv_sems, local_sems, zero_sems = scratch
            x, y, c, chips = _place()
            me, sibling = (x, y, c), (x, y, 1 - c)

            def rows(w, block, part=None):
                off, size = (0, rows_w[w]) if part is None else part
                px, py, pc = block
                start = pl.multiple_of((4 * px + 2 * py + pc) * rows_w[w] + off, 16)
                return outs[w].at[pl.ds(start, size), :]

            def copy(w, k, block, to, part=None, own=False):
                src = rows(w, block, part)
                if own:
                    src = ins[w] if part is None else ins[w].at[pl.ds(part[0], part[1]), :]
                return pltpu.make_async_remote_copy(
                    src_ref=src, dst_ref=rows(w, block, part), send_sem=send_sems.at[w, k],
                    recv_sem=recv_sems.at[w, k], device_id=to, device_id_type=MESH)

            def k_ici(j, p):
                return 1 + j * n_parts + p

            def k_on(j, p):
                return 1 + (3 + j) * n_parts + p

            self.zbuf = zbuf
            self.local = [pltpu.make_async_copy(zbuf.at[pl.ds(0, pad_w[w]), :],
                                                outs[w].at[pl.ds(N_DEV * rows_w[w], pad_w[w]), :], zero_sems.at[w])
                          for w in range(nw) if pad_w[w]]
            self.local += [pltpu.make_async_copy(ins[w], rows(w, me), local_sems.at[w]) for w in range(nw)]
            self.first = [[copy(w, 0, me, sibling, own=True)]
                          + [copy(w, k_ici(j, p), me, (*chip, c), part, own=True)
                             for p, part in enumerate(parts_w[w]) for j, chip in enumerate(chips)]
                          for w in range(nw)]
            self.arrive = [[copy(w, k_ici(j, p), (*chip, c), me, part)
                            for p, part in enumerate(parts_w[w]) for j, chip in enumerate(chips)] for w in range(nw)]
            self.passed = [[copy(w, k_on(j, p), (*chip, c), sibling, part)
                            for p, part in enumerate(parts_w[w]) for j, chip in enumerate(chips)] for w in range(nw)]
            self.from_sibling = [[copy(w, 0, sibling, me)]
                                 + [copy(w, k_on(j, p), (*chip, 1 - c), me, part)
                                    for p, part in enumerate(parts_w[w]) for j, chip in enumerate(chips)]
                                 for w in range(nw)]

    def start(ins, outs, scratch):
        plan = Plan(ins, outs, scratch)
        plan.zbuf[...] = jnp.zeros_like(plan.zbuf)
        for cp in plan.local:
            cp.start()
        for w in range(nw):
            for cp in plan.first[w]:
                cp.start()

    def mid(ins, outs, scratch, phase):
        plan = Plan(ins, outs, scratch)
        early = 3 * (n_parts // 2)
        for w in range(nw):
            pairs = list(zip(plan.arrive[w], plan.passed[w]))
            for arrived, onward in (pairs[:early] if phase == 0 else pairs[early:]):
                arrived.wait_recv()
                onward.start()

    def finish(ins, outs, scratch):
        plan = Plan(ins, outs, scratch)
        for w in range(nw):
            for cp in plan.from_sibling[w]:
                cp.wait_recv()
        for w in range(nw):
            for cp in plan.first[w] + plan.passed[w]:
                cp.wait_send()
        for cp in plan.local:
            cp.wait()

    return _Job(
        ins=shards, out_shape=[jax.ShapeDtypeStruct((f, D), s.dtype) for f, s in zip(full_w, shards)],
        scratch=[pltpu.VMEM((max_pad, D), shards[0].dtype), pltpu.SemaphoreType.DMA((nw, 1 + 6 * n_parts)),
                 pltpu.SemaphoreType.DMA((nw, 1 + 6 * n_parts)), pltpu.SemaphoreType.DMA((nw,)),
                 pltpu.SemaphoreType.DMA((nw,))],
        start=start, mid=mid, finish=finish)


def _rs1_job(partials, rows_w):
    nw = len(partials)
    D = partials[0].shape[1]

    def copies(ins, outs, scratch):
        send_sems, recv_sems = scratch
        x, y, c, _ = _place()
        out = []
        for w in range(nw):
            r = rows_w[w]
            for q in range(4):
                src = ins[w].at[pl.ds(pl.multiple_of((2 * q + 1 - c) * r, 16), r), :]
                out.append(pltpu.make_async_remote_copy(
                    src_ref=src, dst_ref=outs[w].at[pl.ds(q * r, r), :], send_sem=send_sems.at[w, q],
                    recv_sem=recv_sems.at[w, q], device_id=(x, y, 1 - c), device_id_type=MESH))
        return out

    def start(ins, outs, scratch):
        for cp in copies(ins, outs, scratch):
            cp.start()

    def finish(ins, outs, scratch):
        for cp in copies(ins, outs, scratch):
            cp.wait()

    return _Job(
        ins=partials, out_shape=[jax.ShapeDtypeStruct((4 * r, D), p.dtype) for r, p in zip(rows_w, partials)],
        scratch=[pltpu.SemaphoreType.DMA((nw, 4)), pltpu.SemaphoreType.DMA((nw, 4))], start=start, finish=finish)


def _pair_sum(partial, from_sibling, rows, core, name):
    D = partial.shape[1]

    def body(core_ref, p_ref, s_ref, o_ref):
        o_ref[...] = (p_ref[...].astype(F32) + s_ref[...].astype(F32)).astype(o_ref.dtype)

    grid_spec = pltpu.PrefetchScalarGridSpec(
        num_scalar_prefetch=1, grid=(4,),
        in_specs=[pl.BlockSpec((rows, D), lambda q, core_ref: (2 * q + core_ref[0], 0)),
                  pl.BlockSpec((rows, D), lambda q, core_ref: (q, 0))],
        out_specs=pl.BlockSpec((rows, D), lambda q, core_ref: (q, 0)))
    return pl.pallas_call(
        body, name=name, grid_spec=grid_spec, out_shape=jax.ShapeDtypeStruct((4 * rows, D), partial.dtype),
        compiler_params=_params("arbitrary"),
    )(core, partial, from_sibling)


def _rs2_job(chip_sums, rows_w):
    nw = len(chip_sums)

    def copies(ins, outs, scratch):
        send_sems, recv_sems, local_sems = scratch
        x, y, c, chips = _place()
        my_chip = 2 * x + y
        out = []
        for w in range(nw):
            r = rows_w[w]
            mine = pl.ds(pl.multiple_of(my_chip * r, 16), r)
            out.append(pltpu.make_async_copy(ins[w].at[mine, :], outs[w].at[mine, :], local_sems.at[w]))
            for j, (qx, qy) in enumerate(chips):
                src = ins[w].at[pl.ds(pl.multiple_of((2 * qx + qy) * r, 16), r), :]
                out.append(pltpu.make_async_remote_copy(
                    src_ref=src, dst_ref=outs[w].at[mine, :], send_sem=send_sems.at[w, j],
                    recv_sem=recv_sems.at[w, j], device_id=(qx, qy, c), device_id_type=MESH))
        return out

    def start(ins, outs, scratch):
        for cp in copies(ins, outs, scratch):
            cp.start()

    def finish(ins, outs, scratch):
        for cp in copies(ins, outs, scratch):
            cp.wait()

    return _Job(
        ins=chip_sums, out_shape=[jax.ShapeDtypeStruct(s.shape, s.dtype) for s in chip_sums],
        scratch=[pltpu.SemaphoreType.DMA((nw, 3)), pltpu.SemaphoreType.DMA((nw, 3)), pltpu.SemaphoreType.DMA((nw,))],
        start=start, finish=finish)


class _Comm:
    def __init__(self, shards, state):
        self.shards, self.state = dict(shards), state
        self.rows = {n: st[0].shape[0] for n, st in state.items()}
        self.core = lax.axis_index("c").astype(jnp.int32).reshape(1)
        self.big, self.partial, self.chip_sums, self.slots, self.updates = {}, {}, {}, {}, {}

    def slots3(self, name):
        return self.slots[name].reshape(4, self.rows[name], -1)

    def run(self, fn, *args, ag=(), rs1=(), rs2=(), adamw=(), cast=()):
        jobs = []
        if cast:
            jobs.append(_cast_job([self.state[n][0] for n in cast], _WIRE_DTYPE))
        if ag:
            jobs.append(_ag_job([self.shards[n] for n in ag]))
        if rs1:
            jobs.append(_rs1_job([self.partial[n] for n in rs1], [self.rows[n] for n in rs1]))
        if rs2:
            jobs.append(_rs2_job([self.chip_sums[n] for n in rs2], [self.rows[n] for n in rs2]))
        for n in adamw:
            w2, m2, v2 = self.state[n]
            jobs.append(_adamw_job(w2, self.slots3(n), m2, v2))
        out, job_res = fn(*args, jobs=jobs)
        job_res = iter(job_res)
        if cast:
            self.shards.update(zip(cast, next(job_res)))
        if ag:
            self.big.update(zip(ag, next(job_res)))
        if rs1:
            for n, got in zip(rs1, next(job_res)):
                self.chip_sums[n] = _pair_sum(self.partial[n], got, self.rows[n], self.core, "pair_sum_" + n)
        if rs2:
            self.slots.update(zip(rs2, next(job_res)))
        for n in adamw:
            self.updates[n] = next(job_res)
        return out


SMALL_ROWS = 88


def _small_allreduce(vec):
    def body(v_ref, o_ref, gather, send_sems, recv_sems):
        x, y, c, _ = _place()
        my_id = 4 * x + 2 * y + c
        gather[my_id] = v_ref[...]
        copies = []
        for r in range(1, N_DEV):
            peer = (x ^ (r >> 2), y ^ ((r >> 1) & 1), c ^ (r & 1))
            cp = pltpu.make_async_remote_copy(src_ref=v_ref, dst_ref=gather.at[my_id], send_sem=send_sems.at[r - 1],
                                              recv_sem=recv_sems.at[r - 1], device_id=peer, device_id_type=MESH)
            cp.start()
            copies.append(cp)
        for cp in copies:
            cp.wait()
        acc = gather[0]
        for d in range(1, N_DEV):
            acc = acc + gather[d]
        o_ref[...] = acc

    vm = pl.BlockSpec(memory_space=pltpu.VMEM)
    return pl.pallas_call(
        body, name="small_allreduce", out_shape=jax.ShapeDtypeStruct(vec.shape, F32),
        in_specs=[vm], out_specs=vm,
        scratch_shapes=[pltpu.VMEM((N_DEV,) + vec.shape, F32), pltpu.SemaphoreType.DMA((N_DEV - 1,)),
                        pltpu.SemaphoreType.DMA((N_DEV - 1,))],
        compiler_params=_COMM_PARAMS,
    )(vec)


def _adamw_update(w, g, m, v):
    nm = ADAM_B1 * m + (1.0 - ADAM_B1) * g
    nv = ADAM_B2 * v + (1.0 - ADAM_B2) * jnp.square(g)
    m_hat = nm / (1.0 - ADAM_B1 ** ADAM_STEP)
    v_hat = nv / (1.0 - ADAM_B2 ** ADAM_STEP)
    return -ADAM_LR * (m_hat / (jnp.sqrt(v_hat) + ADAM_EPS) + ADAM_WD * w), nm, nv


def _adamw(w, g, m, v, name):
    R, C = w.shape
    tr = _tile(R, 256, 8)

    def body(w_ref, g_ref, m_ref, v_ref, d_ref, nm_ref, nv_ref):
        d_ref[...], nm_ref[...], nv_ref[...] = _adamw_update(w_ref[...], g_ref[...], m_ref[...], v_ref[...])

    spec = pl.BlockSpec((tr, C), lambda i: (i, 0))
    out = jax.ShapeDtypeStruct((R, C), F32)
    return pl.pallas_call(
        body, name=name, grid=(R // tr,), out_shape=(out, out, out),
        in_specs=[spec] * 4, out_specs=(spec, spec, spec),
        compiler_params=_params("parallel"),
    )(w, g, m, v)


def _adamw_slots(w, slots, m, v, name):
    R, C = w.shape
    tc = _tile(C, 512, LANES)
    spec = pl.BlockSpec((R, tc), lambda j: (0, j))
    out = jax.ShapeDtypeStruct((R, C), F32)
    return pl.pallas_call(
        functools.partial(_adamw_slots_body), name=name, grid=(C // tc,), out_shape=(out, out, out, out),
        in_specs=[spec, pl.BlockSpec((4, R, tc), lambda j: (0, 0, j)), spec, spec], out_specs=(spec, spec, spec, spec),
        compiler_params=_params("parallel"),
    )(w, slots, m, v)


def _adamw_slots_body(w_ref, s_ref, m_ref, v_ref, g_ref, d_ref, nm_ref, nv_ref):
    g = s_ref[0].astype(F32)
    for q in range(1, 4):
        g = g + s_ref[q].astype(F32)
    g_ref[...] = g
    d_ref[...], nm_ref[...], nv_ref[...] = _adamw_update(w_ref[...], g, m_ref[...], v_ref[...])


def _cast_job(arrays, dtype):
    C = arrays[0].shape[1]

    def specs(grid):
        total = 1
        for g in grid:
            total *= g
        tc = C // total
        assert tc * total == C and tc % LANES == 0, (C, grid)
        blocks = [pl.BlockSpec((a.shape[0], tc), lambda *ids: (0, _linear_step(grid, ids))) for a in arrays]
        return blocks, list(blocks)

    def each(ins, outs, scratch, step):
        for i_ref, o_ref in zip(ins, outs):
            o_ref[...] = i_ref[...].astype(o_ref.dtype)

    return _Job(ins=arrays, out_shape=[jax.ShapeDtypeStruct(a.shape, dtype) for a in arrays], specs=specs, each=each)


def _adamw_job(w, slots, m, v):
    R, C = w.shape

    n_slices = C // LANES

    def specs(grid):
        total = 1
        for g in grid:
            total *= g
        assert total >= n_slices, (grid, n_slices)
        col = lambda *ids: jnp.minimum(_linear_step(grid, ids), n_slices - 1)
        blk = pl.BlockSpec((R, LANES), lambda *ids: (0, col(*ids)))
        slot_blk = pl.BlockSpec((4, R, LANES), lambda *ids: (0, 0, col(*ids)))
        return [blk, slot_blk, blk, blk], [blk] * 4

    def each(ins, outs, scratch, step):
        @pl.when(step < n_slices)
        def _():
            _adamw_slots_body(*ins, *outs)

    out = jax.ShapeDtypeStruct((R, C), F32)
    return _Job(ins=[w, slots, m, v], out_shape=[out] * 4, specs=specs, each=each)


WEIGHT_NAMES = ("ffn1_norm", "ffn1_w_gate", "ffn1_w_up", "ffn1_w_down", "mix_norm", "w_in", "swa_sinks",
                "swa_out_norm", "sb_out_norm", "w_out", "ffn2_norm", "ffn2_w_gate", "ffn2_w_up", "ffn2_w_down",
                "final_norm")
SMALL_NAMES = ("ffn1_norm", "mix_norm", "swa_sinks", "swa_out_norm", "sb_out_norm", "ffn2_norm", "final_norm")
BIG_ARGS = {"ffn1_gate": ("ffn1_w_gate", True), "ffn1_up": ("ffn1_w_up", True), "ffn1_down": ("ffn1_w_down", False),
            "w_in": ("w_in", True), "w_out": ("w_out", False), "ffn2_gate": ("ffn2_w_gate", True),
            "ffn2_up": ("ffn2_w_up", True), "ffn2_down": ("ffn2_w_down", False)}


def _pack_small(parts):
    padded = [jnp.pad(p.reshape(1, -1), ((0, 0), (0, -p.size % LANES))) for p in parts]
    flat = jnp.concatenate(padded, axis=1)
    flat = jnp.pad(flat, ((0, 0), (0, SMALL_ROWS * LANES - flat.shape[1])))
    return flat.reshape(SMALL_ROWS, LANES)


def _unpack_small(block, shapes):
    flat = block.reshape(-1)
    out, off = [], 0
    for shp in shapes:
        n = 1
        for s in shp:
            n *= s
        out.append(flat[off:off + n].reshape(shp))
        off += n + (-n % LANES)
    return out


def kernel(x, ffn1_norm, ffn1_w_gate, ffn1_w_up, ffn1_w_down, mix_norm, w_in, swa_sinks, swa_out_norm, sb_out_norm, w_out, ffn2_norm, ffn2_w_gate, ffn2_w_up, ffn2_w_down, final_norm, loss_target, m_ffn1_norm, m_ffn1_w_gate, m_ffn1_w_up, m_ffn1_w_down, m_mix_norm, m_w_in, m_swa_sinks, m_swa_out_norm, m_sb_out_norm, m_w_out, m_ffn2_norm, m_ffn2_w_gate, m_ffn2_w_up, m_ffn2_w_down, m_final_norm, v_ffn1_norm, v_ffn1_w_gate, v_ffn1_w_up, v_ffn1_w_down, v_mix_norm, v_w_in, v_swa_sinks, v_swa_out_norm, v_sb_out_norm, v_w_out, v_ffn2_norm, v_ffn2_w_gate, v_ffn2_w_up, v_ffn2_w_down, v_final_norm):
    args = dict(locals())
    B, S, D = x.shape
    T = B * S
    weights = {n: args[n] for n in WEIGHT_NAMES}
    mom_m = {n: args["m_" + n] for n in WEIGHT_NAMES}
    mom_v = {n: args["v_" + n] for n in WEIGHT_NAMES}

    state = {}
    for name in BIG_NAMES:
        arg, transposed = BIG_ARGS[name]
        to_rows = (lambda t: t[0].T) if transposed else (lambda t: t[0])
        state[name] = tuple(to_rows(t[arg]) for t in (weights, mom_m, mom_v))
    comm = _Comm({FIRST_GATHERED: state[FIRST_GATHERED][0].astype(_WIRE_DTYPE)}, state)
    small = {n: weights[n].reshape(1, -1) for n in SMALL_NAMES}

    loss, gx, d_small = _layer_step(x.reshape(T, D), loss_target.reshape(T, D), B, S, small, comm)

    small_shapes = [(1, 1)] + [d_small[n].shape for n in SMALL_NAMES]
    reduced = _small_allreduce(_pack_small([loss[:, :1]] + [d_small[n] for n in SMALL_NAMES]))
    red = _unpack_small(reduced, small_shapes)
    loss_out = red[0].reshape(())
    g_small = dict(zip(SMALL_NAMES, red[1:]))

    grads, deltas, new_m, new_v = {}, {}, {}, {}
    for name in BIG_NAMES:
        arg, transposed = BIG_ARGS[name]
        back = (lambda t: t.T[None]) if transposed else (lambda t: t[None])
        res = comm.updates.get(name)
        if res is None:
            w2, m2, v2 = state[name]
            res = _adamw_slots(w2, comm.slots3(name), m2, v2, "adamw_" + name)
        grads[arg], deltas[arg], new_m[arg], new_v[arg] = [back(t) for t in res]
    shapes1 = [(1, weights[n].size) for n in SMALL_NAMES]
    packed = [_pack_small([t[n].reshape(1, -1) for n in SMALL_NAMES]) for t in (weights, g_small, mom_m, mom_v)]
    upd = _adamw(*packed, "adamw_small")
    for tgt_dict, block in zip((deltas, new_m, new_v), upd):
        for n, val in zip(SMALL_NAMES, _unpack_small(block, shapes1)):
            tgt_dict[n] = val.reshape(weights[n].shape)
    for n in SMALL_NAMES:
        grads[n] = g_small[n].reshape(weights[n].shape)

    return (loss_out, gx.reshape(B, S, D), *[grads[n] for n in WEIGHT_NAMES], *[deltas[n] for n in WEIGHT_NAMES],
            *[new_m[n] for n in WEIGHT_NAMES], *[new_v[n] for n in WEIGHT_NAMES])
```

```python
import functools

import jax
import jax.numpy as jnp
from jax import lax
from jax.experimental import pallas as pl
from jax.experimental.pallas import tpu as pltpu

F32 = jnp.float32
_MXU_DTYPE = jnp.bfloat16
_WIRE_DTYPE = jnp.bfloat16

EPS = 1e-6
HEAD_DIM = 64
N_SWA_HEADS = 16
N_SWA_KV = 4
N_SB_HEADS = 16
WINDOW = 128
SWA_Q = N_SWA_HEADS * HEAD_DIM
SWA_KV = N_SWA_KV * HEAD_DIM
SB_W = N_SB_HEADS * HEAD_DIM
IN_W = SWA_Q + 2 * SWA_KV + 3 * SB_W
LANES = 128
ATT_SCALE = HEAD_DIM ** -0.5

ADAM_LR = 0.001
ADAM_B1 = 0.9
ADAM_B2 = 0.999
ADAM_EPS = 1e-08
ADAM_WD = 0.01
ADAM_STEP = 10

N_DEV = 8
_VMEM_LIMIT_BYTES = 56 * 1024 * 1024
_F_TILE = 512


def _params(*semantics):
    return pltpu.CompilerParams(dimension_semantics=semantics, vmem_limit_bytes=_VMEM_LIMIT_BYTES)


def _tile(n, pref, align):
    t = min(n, pref)
    t -= t % align
    while t >= align:
        if n % t == 0:
            return t
        t -= align
    return n


def _dot(a, b):
    return lax.dot_general(a, b, (((1,), (0,)), ((), ())), preferred_element_type=F32)


def _dot_nt(a, b):
    return lax.dot_general(a, b, (((1,), (1,)), ((), ())), preferred_element_type=F32)


def _dot_tn(a, b):
    return lax.dot_general(a, b, (((0,), (0,)), ((), ())), preferred_element_type=F32)


class _Job:
    def __init__(self, ins, out_shape, scratch=(), start=None, finish=None, mid=None, each=None, specs=None):
        self.ins, self.out_shape, self.scratch = list(ins), list(out_shape), list(scratch)
        self.start, self.mid, self.finish, self.each, self.specs = start, mid, finish, each, specs


_JOB_MID_FRACTION = 0.6


def _linear_step(grid, ids):
    step = ids[0]
    for d in range(1, len(grid)):
        step = step * grid[d] + ids[d]
    return step


def _call(body, *, name, grid, in_specs, out_specs, out_shape, args, semantics, scratch_shapes=(), jobs=()):
    single = not isinstance(out_shape, (tuple, list))
    if not jobs:
        res = pl.pallas_call(body, name=name, grid=grid, in_specs=list(in_specs), out_specs=out_specs,
                             out_shape=out_shape, scratch_shapes=list(scratch_shapes),
                             compiler_params=_params(*semantics))(*args)
        return res, []
    base_out = [out_shape] if single else list(out_shape)
    base_out_specs = [out_specs] if single else list(out_specs)
    n_in, n_out, n_scr = len(args), len(base_out), len(scratch_shapes)
    any_spec = pl.BlockSpec(memory_space=pl.ANY)
    total = 1
    for g in grid:
        total *= g
    mid_step = min(total - 1, int(total * _JOB_MID_FRACTION))

    def wrapped(*refs):
        pos = n_in
        job_ins = []
        for job in jobs:
            job_ins.append(refs[pos:pos + len(job.ins)])
            pos += len(job.ins)
        outs = refs[pos:pos + n_out]
        pos += n_out
        job_outs = []
        for job in jobs:
            job_outs.append(refs[pos:pos + len(job.out_shape)])
            pos += len(job.out_shape)
        scr = refs[pos:pos + n_scr]
        pos += n_scr
        job_scr = []
        for job in jobs:
            job_scr.append(refs[pos:pos + len(job.scratch)])
            pos += len(job.scratch)
        bound = list(zip(jobs, job_ins, job_outs, job_scr))
        step = _linear_step(grid, [pl.program_id(d) for d in range(len(grid))])

        @pl.when(step == 0)
        def _():
            for job, ji, jo, js in bound:
                if job.start is not None:
                    job.start(ji, jo, js)

        @pl.when(step == mid_step)
        def _():
            for job, ji, jo, js in bound:
                if job.mid is not None:
                    job.mid(ji, jo, js, 0)

        body(*refs[:n_in], *outs, *scr)
        for job, ji, jo, js in bound:
            if job.each is not None:
                job.each(ji, jo, js, step)

        @pl.when(step == total - 1)
        def _():
            for job, ji, jo, js in bound:
                if job.mid is not None:
                    job.mid(ji, jo, js, 1)
            for job, ji, jo, js in bound:
                if job.finish is not None:
                    job.finish(ji, jo, js)

    all_args, all_in_specs = list(args), list(in_specs)
    all_out_shape, all_out_specs = list(base_out), list(base_out_specs)
    for job in jobs:
        job_in_specs, job_out_specs = (job.specs(grid) if job.specs is not None else
                                       ([any_spec] * len(job.ins), [any_spec] * len(job.out_shape)))
        all_args += job.ins
        all_in_specs += job_in_specs
        all_out_shape += job.out_shape
        all_out_specs += job_out_specs
    all_scratch = list(scratch_shapes) + [s for job in jobs for s in job.scratch]
    res = pl.pallas_call(
        wrapped, name=name, grid=grid, in_specs=all_in_specs, out_specs=tuple(all_out_specs),
        out_shape=tuple(all_out_shape), scratch_shapes=all_scratch,
        compiler_params=pltpu.CompilerParams(dimension_semantics=("arbitrary",) * len(grid),
                                             vmem_limit_bytes=_VMEM_LIMIT_BYTES,
                                             has_side_effects=any(job.start is not None for job in jobs)),
    )(*all_args)
    base = res[0] if single else tuple(res[:n_out])
    job_res, pos = [], n_out
    for job in jobs:
        job_res.append(tuple(res[pos:pos + len(job.out_shape)]))
        pos += len(job.out_shape)
    return base, job_res


def _rms_fwd(x, g, name, jobs=()):
    T, D = x.shape
    tm = _tile(T, 512, 16)

    def body(x_ref, g_ref, o_ref):
        xv = x_ref[...]
        r = lax.rsqrt(jnp.mean(xv * xv, axis=-1, keepdims=True) + EPS)
        o_ref[...] = (xv * r * g_ref[...]).astype(o_ref.dtype)

    return _call(
        body, name=name, grid=(T // tm,),
        out_shape=jax.ShapeDtypeStruct((T, D), _MXU_DTYPE),
        in_specs=[pl.BlockSpec((tm, D), lambda i: (i, 0)), pl.BlockSpec((1, D), lambda i: (0, 0))],
        out_specs=pl.BlockSpec((tm, D), lambda i: (i, 0)), args=(x, g), semantics=("parallel",), jobs=jobs)


def _rms_bwd_rows(dh, xv, g):
    r = lax.rsqrt(jnp.mean(xv * xv, axis=-1, keepdims=True) + EPS)
    xhat = xv * r
    u = dh * g
    dx = r * (u - xhat * jnp.mean(u * xhat, axis=-1, keepdims=True))
    return dx, dh * xhat


def _rms_bwd(dh, x, g, dres, name):
    T, D = x.shape
    tm = _tile(T, 256, 16)

    def body(dh_ref, x_ref, g_ref, dres_ref, dx_ref, dxb_ref, dg_ref):
        @pl.when(pl.program_id(0) == 0)
        def _():
            dg_ref[...] = jnp.zeros_like(dg_ref)

        dx, dgr = _rms_bwd_rows(dh_ref[...], x_ref[...], g_ref[...])
        dx = dres_ref[...] + dx
        dx_ref[...] = dx
        dxb_ref[...] = dx.astype(dxb_ref.dtype)
        dg_ref[...] += jnp.sum(dgr, axis=0, keepdims=True)

    row = pl.BlockSpec((tm, D), lambda i: (i, 0))
    vec = pl.BlockSpec((1, D), lambda i: (0, 0))
    return pl.pallas_call(
        body, name=name, grid=(T // tm,),
        out_shape=(jax.ShapeDtypeStruct((T, D), F32), jax.ShapeDtypeStruct((T, D), _MXU_DTYPE),
                   jax.ShapeDtypeStruct((1, D), F32)),
        in_specs=[row, row, vec, row], out_specs=(row, row, vec),
        compiler_params=_params("arbitrary"),
    )(dh, x, g, dres)


def _loss_head(x, g, tgt, name):
    T, D = x.shape
    tm = _tile(T, 256, 16)

    def body(x_ref, g_ref, t_ref, dx_ref, dxb_ref, dg_ref, loss_ref):
        @pl.when(pl.program_id(0) == 0)
        def _():
            dg_ref[...] = jnp.zeros_like(dg_ref)
            loss_ref[...] = jnp.zeros_like(loss_ref)

        xv = x_ref[...]
        gv = g_ref[...]
        r = lax.rsqrt(jnp.mean(xv * xv, axis=-1, keepdims=True) + EPS)
        xhat = xv * r
        diff = xhat * gv - t_ref[...]
        tok = jnp.mean(diff * diff, axis=-1, keepdims=True)
        loss_ref[...] += 0.5 * jnp.sum(tok, axis=0, keepdims=True)
        dy = diff / D
        u = dy * gv
        dx = r * (u - xhat * jnp.mean(u * xhat, axis=-1, keepdims=True))
        dx_ref[...] = dx
        dxb_ref[...] = dx.astype(dxb_ref.dtype)
        dg_ref[...] += jnp.sum(dy * xhat, axis=0, keepdims=True)

    row = pl.BlockSpec((tm, D), lambda i: (i, 0))
    vec = pl.BlockSpec((1, D), lambda i: (0, 0))
    return pl.pallas_call(
        body, name=name, grid=(T // tm,),
        out_shape=(jax.ShapeDtypeStruct((T, D), F32), jax.ShapeDtypeStruct((T, D), _MXU_DTYPE),
                   jax.ShapeDtypeStruct((1, D), F32), jax.ShapeDtypeStruct((1, LANES), F32)),
        in_specs=[row, vec, row],
        out_specs=(row, row, vec, pl.BlockSpec((1, LANES), lambda i: (0, 0))),
        compiler_params=_params("arbitrary"),
    )(x, g, tgt)


def _outnorm_fwd(ya, yb, ga, gb, name):
    T, W = ya.shape
    tm = _tile(T, 512, 16)

    def body(ya_ref, yb_ref, ga_ref, gb_ref, o_ref):
        for k, (y_ref, g_ref) in enumerate(((ya_ref, ga_ref), (yb_ref, gb_ref))):
            yv = y_ref[...]
            r = lax.rsqrt(jnp.mean(yv * yv, axis=-1, keepdims=True) + EPS)
            o_ref[:, k * W:(k + 1) * W] = (yv * r * g_ref[...]).astype(o_ref.dtype)

    row = pl.BlockSpec((tm, W), lambda i: (i, 0))
    vec = pl.BlockSpec((1, W), lambda i: (0, 0))
    return pl.pallas_call(
        body, name=name, grid=(T // tm,),
        out_shape=jax.ShapeDtypeStruct((T, 2 * W), _MXU_DTYPE),
        in_specs=[row, row, vec, vec], out_specs=pl.BlockSpec((tm, 2 * W), lambda i: (i, 0)),
        compiler_params=_params("parallel"),
    )(ya, yb, ga, gb)


def _outnorm_bwd(dyn, ya, yb, ga, gb, name):
    T, W = ya.shape
    tm = _tile(T, 256, 16)

    def body(d_ref, ya_ref, yb_ref, ga_ref, gb_ref, dya_ref, dyb_ref, dga_ref, dgb_ref):
        @pl.when(pl.program_id(0) == 0)
        def _():
            dga_ref[...] = jnp.zeros_like(dga_ref)
            dgb_ref[...] = jnp.zeros_like(dgb_ref)

        for k, (y_ref, g_ref, dy_ref, dg_ref) in enumerate(
                ((ya_ref, ga_ref, dya_ref, dga_ref), (yb_ref, gb_ref, dyb_ref, dgb_ref))):
            dy, dgr = _rms_bwd_rows(d_ref[:, k * W:(k + 1) * W], y_ref[...], g_ref[...])
            dy_ref[...] = dy.astype(dy_ref.dtype)
            dg_ref[...] += jnp.sum(dgr, axis=0, keepdims=True)

    row = pl.BlockSpec((tm, W), lambda i: (i, 0))
    vec = pl.BlockSpec((1, W), lambda i: (0, 0))
    return pl.pallas_call(
        body, name=name, grid=(T // tm,),
        out_shape=(jax.ShapeDtypeStruct((T, W), _MXU_DTYPE), jax.ShapeDtypeStruct((T, W), _MXU_DTYPE),
                   jax.ShapeDtypeStruct((1, W), F32), jax.ShapeDtypeStruct((1, W), F32)),
        in_specs=[pl.BlockSpec((tm, 2 * W), lambda i: (i, 0)), row, row, vec, vec],
        out_specs=(row, row, vec, vec),
        compiler_params=_params("arbitrary"),
    )(dyn, ya, yb, ga, gb)


_STRIP_ROWS = 256


def _strips(rows):
    step = min(rows, _STRIP_ROWS)
    return [slice(r, r + step) for r in range(0, rows, step)]


def _ffn_gu(h, wg_t, wu_t, name, jobs=()):
    T, D = h.shape
    Fp = wg_t.shape[0]
    tm = _tile(T, 1024, 16)
    tn = _tile(Fp, _F_TILE, LANES)

    def body(h_ref, wg_ref, wu_ref, g_ref, u_ref, a_ref):
        for rows in _strips(tm):
            hv = h_ref[rows, :]
            g = _dot_nt(hv, wg_ref[...])
            u = _dot_nt(hv, wu_ref[...])
            g_ref[rows, :] = g.astype(g_ref.dtype)
            u_ref[rows, :] = u.astype(u_ref.dtype)
            a_ref[rows, :] = (g * jax.nn.sigmoid(g) * u).astype(a_ref.dtype)

    act = pl.BlockSpec((tm, tn), lambda n, m: (m, n))
    wsp = pl.BlockSpec((tn, D), lambda n, m: (n, 0))
    out = jax.ShapeDtypeStruct((T, Fp), _MXU_DTYPE)
    return _call(
        body, name=name, grid=(Fp // tn, T // tm), out_shape=(out, out, out),
        in_specs=[pl.BlockSpec((tm, D), lambda n, m: (m, 0)), wsp, wsp],
        out_specs=(act, act, act), args=(h, wg_t, wu_t), semantics=("parallel", "parallel"), jobs=jobs)


def _ffn_bwd_act(dxb, wd, G, U, name, jobs=()):
    T, D = dxb.shape
    Fp = wd.shape[0]
    tm = _tile(T, 1024, 16)
    tn = _tile(Fp, _F_TILE, LANES)

    def body(e_ref, wd_ref, g_ref, u_ref, dg_ref, du_ref):
        for rows in _strips(tm):
            da = 0.5 * _dot_nt(e_ref[rows, :], wd_ref[...])
            g = g_ref[rows, :].astype(F32)
            u = u_ref[rows, :].astype(F32)
            s = jax.nn.sigmoid(g)
            du_ref[rows, :] = (da * (g * s)).astype(du_ref.dtype)
            dg_ref[rows, :] = (da * u * (s * (1.0 + g * (1.0 - s)))).astype(dg_ref.dtype)

    act = pl.BlockSpec((tm, tn), lambda m, n: (m, n))
    out = jax.ShapeDtypeStruct((T, Fp), _MXU_DTYPE)
    return _call(
        body, name=name, grid=(T // tm, Fp // tn), out_shape=(out, out),
        in_specs=[pl.BlockSpec((tm, D), lambda m, n: (m, 0)), pl.BlockSpec((tn, D), lambda m, n: (n, 0)),
                  act, act],
        out_specs=(act, act), args=(dxb, wd, G, U), semantics=("parallel", "parallel"), jobs=jobs)


def _ffn_gate(h, wg_t, U, name, jobs=()):
    T, D = h.shape
    Fp = wg_t.shape[0]
    tm = _tile(T, 1024, 16)
    tn = _tile(Fp, _F_TILE, LANES)

    def body(h_ref, wg_ref, u_ref, g_ref, a_ref):
        for rows in _strips(tm):
            g = _dot_nt(h_ref[rows, :], wg_ref[...])
            g_ref[rows, :] = g.astype(g_ref.dtype)
            a_ref[rows, :] = (g * jax.nn.sigmoid(g) * u_ref[rows, :].astype(F32)).astype(a_ref.dtype)

    act = pl.BlockSpec((tm, tn), lambda m, n: (m, n))
    out = jax.ShapeDtypeStruct((T, Fp), _MXU_DTYPE)
    return _call(
        body, name=name, grid=(T // tm, Fp // tn), out_shape=(out, out),
        in_specs=[pl.BlockSpec((tm, D), lambda m, n: (m, 0)), pl.BlockSpec((tn, D), lambda m, n: (n, 0)), act],
        out_specs=(act, act), args=(h, wg_t, U), semantics=("parallel", "parallel"), jobs=jobs)


def _mm_nt(a, b, out_dtype, name, jobs=()):
    M, K = a.shape
    N = b.shape[0]
    tm = _tile(M, 1024, 16)
    tn = _tile(N, 512, LANES)

    def body(a_ref, b_ref, o_ref):
        o_ref[...] = _dot_nt(a_ref[...], b_ref[...]).astype(o_ref.dtype)

    return _call(
        body, name=name, grid=(M // tm, N // tn), out_shape=jax.ShapeDtypeStruct((M, N), out_dtype),
        in_specs=[pl.BlockSpec((tm, K), lambda m, n: (m, 0)), pl.BlockSpec((tn, K), lambda m, n: (n, 0))],
        out_specs=pl.BlockSpec((tm, tn), lambda m, n: (m, n)), args=(a, b),
        semantics=("parallel", "parallel"), jobs=jobs)


_MM_OPERAND_BYTES = 26 * 1024 * 1024


def _k_tile(K, bytes_per_k, align):
    best = align
    for t in range(align, K + 1, align):
        if K % t == 0 and 2 * t * bytes_per_k <= _MM_OPERAND_BYTES:
            best = t
    return best


def _mm_nn(pairs, res, alpha, out_dtype, name, jobs=()):
    M, K = pairs[0][0].shape
    N = pairs[0][1].shape[1]
    n_pairs = len(pairs)
    tm = _tile(M, 1024, 16)
    tn = _tile(N, 1024, LANES)
    tk = _k_tile(K, n_pairs * (tm + tn) * pairs[0][0].dtype.itemsize, LANES)
    nk = K // tk

    def body(*refs):
        ab = refs[:2 * n_pairs]
        res_ref = refs[2 * n_pairs] if res is not None else None
        o_ref = refs[2 * n_pairs + (res is not None)]

        def finish(acc):
            out = alpha * acc
            if res_ref is not None:
                out = res_ref[...] + out
            o_ref[...] = out.astype(o_ref.dtype)

        part = _dot(ab[0][...], ab[1][...])
        for i in range(1, n_pairs):
            part = part + _dot(ab[2 * i][...], ab[2 * i + 1][...])
        if nk == 1:
            finish(part)
        else:
            acc_ref = refs[-1]
            k = pl.program_id(2)

            @pl.when(k == 0)
            def _():
                acc_ref[...] = part

            @pl.when(k > 0)
            def _():
                acc_ref[...] += part

            @pl.when(k == nk - 1)
            def _():
                finish(acc_ref[...])

    in_specs, args = [], []
    for a, b in pairs:
        in_specs += [pl.BlockSpec((tm, tk), lambda m, n, k: (m, k)), pl.BlockSpec((tk, tn), lambda m, n, k: (k, n))]
        args += [a, b]
    if res is not None:
        in_specs.append(pl.BlockSpec((tm, tn), lambda m, n, k: (m, n)))
        args.append(res)
    return _call(
        body, name=name, grid=(M // tm, N // tn, nk), out_shape=jax.ShapeDtypeStruct((M, N), out_dtype),
        in_specs=in_specs, out_specs=pl.BlockSpec((tm, tn), lambda m, n, k: (m, n)),
        scratch_shapes=[pltpu.VMEM((tm, tn), F32)] if nk > 1 else [], args=args,
        semantics=("parallel", "parallel", "arbitrary"), jobs=jobs)


def _mm_tn(a, b, alpha, out_dtype, name, jobs=()):
    K, M = a.shape
    N = b.shape[1]
    tm = _tile(M, 512, LANES)
    tn = _tile(N, 1024, LANES)

    def body(a_ref, b_ref, o_ref):
        o_ref[...] = (alpha * _dot_tn(a_ref[...], b_ref[...])).astype(o_ref.dtype)

    return _call(
        body, name=name, grid=(N // tn, M // tm), out_shape=jax.ShapeDtypeStruct((M, N), out_dtype),
        in_specs=[pl.BlockSpec((K, tm), lambda n, m: (0, m)), pl.BlockSpec((K, tn), lambda n, m: (0, n))],
        out_specs=pl.BlockSpec((tm, tn), lambda n, m: (m, n)), args=(a, b),
        semantics=("parallel", "parallel"), jobs=jobs)


def _half_masks():
    lane = lax.broadcasted_iota(jnp.int32, (1, LANES), 1)
    return (lane < HEAD_DIM, lane >= HEAD_DIM)


def _swap_halves(v):
    return pltpu.roll(v.astype(F32), HEAD_DIM, 1).astype(v.dtype)


def _swa_geometry(n):
    qi = lax.broadcasted_iota(jnp.int32, (WINDOW, 2 * WINDOW), 0)
    kp = lax.broadcasted_iota(jnp.int32, (WINDOW, 2 * WINDOW), 1)
    dist = (WINDOW + qi) - kp
    valid = (dist >= 0) & (dist < WINDOW) & ((n > 0) | (kp >= WINDOW))
    return dist.astype(F32), valid


def _swa_slope(h):
    return 2.0 ** (-8.0 * (h + 1) / N_SWA_HEADS)


def _swa_softmax(qk, sink, slope, distf, valid):
    s = qk * ATT_SCALE - slope * distf
    s = jnp.where(valid, s, -1e30)
    m = jnp.maximum(jnp.max(s, axis=1, keepdims=True), sink)
    p = jnp.exp(s - m)
    e_sink = jnp.exp(sink - m)
    den = jnp.sum(p, axis=1, keepdims=True) + e_sink
    return p / den, e_sink / den


def _swa_group_heads(g):
    return [(2 * pp + a, pp, a) for pp in (2 * g, 2 * g + 1) for a in range(2)]


def _swa_specs(B, S):
    nb = S // WINDOW
    kcol = SWA_Q // SWA_KV
    cur = lambda b, n: (b * nb + n, kcol)
    prev = lambda b, n: (b * nb + jnp.maximum(n - 1, 0), kcol)
    curv = lambda b, n: (b * nb + n, kcol + 1)
    prevv = lambda b, n: (b * nb + jnp.maximum(n - 1, 0), kcol + 1)
    q_spec = pl.BlockSpec((WINDOW, SWA_Q), lambda b, n: (b * nb + n, 0))
    kv = [pl.BlockSpec((WINDOW, SWA_KV), f) for f in (prev, cur, prevv, curv)]
    sink_spec = pl.BlockSpec(memory_space=pltpu.SMEM)
    return nb, q_spec, kv, sink_spec


def _swa_kv_views(kp_ref, kc_ref, vp_ref, vc_ref, g):
    hm = _half_masks()
    c0 = (g // 2) * LANES
    k_all = jnp.concatenate([kp_ref[:, c0:c0 + LANES], kc_ref[:, c0:c0 + LANES]], axis=0)
    v_all = jnp.concatenate([vp_ref[:, c0:c0 + LANES], vc_ref[:, c0:c0 + LANES]], axis=0)
    b = g % 2
    ks, vs = [None, None], [None, None]
    ks[b], vs[b] = k_all, v_all
    ks[1 - b], vs[1 - b] = _swap_halves(k_all), _swap_halves(v_all)
    ks = [jnp.where(hm[a], ks[a], 0) for a in range(2)]
    vs = [jnp.where(hm[a], vs[a], 0) for a in range(2)]
    return ks, vs


def _swa_fwd(proj, sinks, B, S, name, jobs=()):
    T = B * S
    nb, q_spec, kv_specs, sink_spec = _swa_specs(B, S)

    def body(sink_ref, q_ref, kp_ref, kc_ref, vp_ref, vc_ref, y_ref):
        hm = _half_masks()
        distf, valid = _swa_geometry(pl.program_id(1))
        for g in range(N_SWA_KV):
            ks, vs = _swa_kv_views(kp_ref, kc_ref, vp_ref, vc_ref, g)
            heads = _swa_group_heads(g)
            qk = [_dot_nt(jnp.where(hm[a], q_ref[:, pp * LANES:(pp + 1) * LANES], 0), ks[a]) for _, pp, a in heads]
            p = [_swa_softmax(qk[i], sink_ref[0, h], _swa_slope(h), distf, valid)[0] for i, (h, _, _) in enumerate(heads)]
            o = [_dot(p[i].astype(_MXU_DTYPE), vs[a]) for i, (_, _, a) in enumerate(heads)]
            for j, pp in enumerate((2 * g, 2 * g + 1)):
                y_ref[:, pp * LANES:(pp + 1) * LANES] = o[2 * j] + o[2 * j + 1]

    return _call(
        body, name=name, grid=(B, nb), out_shape=jax.ShapeDtypeStruct((T, SWA_Q), F32),
        in_specs=[sink_spec, q_spec] + kv_specs,
        out_specs=pl.BlockSpec((WINDOW, SWA_Q), lambda b, n: (b * nb + n, 0)),
        args=(sinks, proj, proj, proj, proj, proj), semantics=("parallel", "parallel"), jobs=jobs)


def _swa_bwd(proj, sinks, dya, B, S, name, jobs=()):
    T = B * S
    nb, q_spec, kv_specs, sink_spec = _swa_specs(B, S)

    def body(sink_ref, q_ref, kp_ref, kc_ref, vp_ref, vc_ref, do_ref,
             dq_ref, dk_ref, dv_ref, dsink_ref, dk_acc, dv_acc):
        b_id, n = pl.program_id(0), pl.program_id(1)
        hm = _half_masks()
        lane = lax.broadcasted_iota(jnp.int32, (1, LANES), 1)

        @pl.when((b_id == 0) & (n == 0))
        def _():
            dsink_ref[...] = jnp.zeros_like(dsink_ref)

        @pl.when(n == 0)
        def _():
            dk_acc[...] = jnp.zeros_like(dk_acc)
            dv_acc[...] = jnp.zeros_like(dv_acc)

        distf, valid = _swa_geometry(n)
        r_prev = pl.multiple_of(jnp.maximum(n - 1, 0) * WINDOW, WINDOW)
        r_cur = pl.multiple_of(n * WINDOW, WINDOW)
        dsink = jnp.zeros((1, LANES), F32)
        for g in range(N_SWA_KV):
            ks, vs = _swa_kv_views(kp_ref, kc_ref, vp_ref, vc_ref, g)
            heads = _swa_group_heads(g)
            four = range(len(heads))
            qms = [jnp.where(hm[a], q_ref[:, pp * LANES:(pp + 1) * LANES], 0) for _, pp, a in heads]
            doms = [jnp.where(hm[a], do_ref[:, pp * LANES:(pp + 1) * LANES], 0) for _, pp, a in heads]
            qk = [_dot_nt(qms[i], ks[heads[i][2]]) for i in four]
            dp = [_dot_nt(doms[i], vs[heads[i][2]]) for i in four]
            soft = [_swa_softmax(qk[i], sink_ref[0, heads[i][0]], _swa_slope(heads[i][0]), distf, valid) for i in four]
            p = [soft[i][0] for i in four]
            delta = [jnp.sum(p[i] * dp[i], axis=1, keepdims=True) for i in four]
            ds = [(p[i] * (dp[i] - delta[i]) * ATT_SCALE).astype(_MXU_DTYPE) for i in four]
            for i in four:
                dsink = dsink + jnp.where(lane == heads[i][0], -jnp.sum(soft[i][1] * delta[i]), 0.0)
            dq = [_dot(ds[i], ks[heads[i][2]]) for i in four]
            dk_h = [_dot_tn(ds[i], qms[i]) for i in four]
            dv_h = [_dot_tn(p[i].astype(_MXU_DTYPE), doms[i]) for i in four]
            for j, pp in enumerate((2 * g, 2 * g + 1)):
                dq_ref[:, pp * LANES:(pp + 1) * LANES] = (dq[2 * j] + dq[2 * j + 1]).astype(dq_ref.dtype)
            dk_g = [dk_h[a] + dk_h[2 + a] for a in range(2)]
            dv_g = [dv_h[a] + dv_h[2 + a] for a in range(2)]
            bsel = g % 2
            dk_t = dk_g[bsel] + pltpu.roll(dk_g[1 - bsel], HEAD_DIM, 1)
            dv_t = dv_g[bsel] + pltpu.roll(dv_g[1 - bsel], HEAD_DIM, 1)
            c0 = (g // 2) * LANES
            dk_acc[pl.ds(r_prev, WINDOW), c0:c0 + LANES] += dk_t[:WINDOW]
            dk_acc[pl.ds(r_cur, WINDOW), c0:c0 + LANES] += dk_t[WINDOW:]
            dv_acc[pl.ds(r_prev, WINDOW), c0:c0 + LANES] += dv_t[:WINDOW]
            dv_acc[pl.ds(r_cur, WINDOW), c0:c0 + LANES] += dv_t[WINDOW:]
        dsink_ref[...] += dsink

        @pl.when(n == nb - 1)
        def _():
            dk_ref[...] = dk_acc[...].astype(dk_ref.dtype)
            dv_ref[...] = dv_acc[...].astype(dv_ref.dtype)

    seq_kv = pl.BlockSpec((S, SWA_KV), lambda b, n: (b, 0))
    return _call(
        body, name=name, grid=(B, nb),
        out_shape=(jax.ShapeDtypeStruct((T, SWA_Q), _MXU_DTYPE), jax.ShapeDtypeStruct((T, SWA_KV), _MXU_DTYPE),
                   jax.ShapeDtypeStruct((T, SWA_KV), _MXU_DTYPE), jax.ShapeDtypeStruct((1, LANES), F32)),
        in_specs=[sink_spec, q_spec] + kv_specs + [pl.BlockSpec((WINDOW, SWA_Q), lambda b, n: (b * nb + n, 0))],
        out_specs=(pl.BlockSpec((WINDOW, SWA_Q), lambda b, n: (b * nb + n, 0)), seq_kv, seq_kv,
                   pl.BlockSpec((1, LANES), lambda b, n: (0, 0))),
        scratch_shapes=[pltpu.VMEM((S, SWA_KV), F32), pltpu.VMEM((S, SWA_KV), F32)],
        args=(sinks, proj, proj, proj, proj, proj, dya), semantics=("arbitrary", "arbitrary"), jobs=jobs)


SB_TILE = 256
SB_HALF = 128
SB_DEAD = -105.0


def _mark_lanes():
    lane = lax.broadcasted_iota(jnp.int32, (1, LANES), 1)
    return (lane == HEAD_DIM - 1) | (lane == LANES - 1)


def _tri2(cond):
    j = lax.broadcasted_iota(jnp.int32, (2 * SB_HALF, SB_HALF), 0) & (SB_HALF - 1)
    s = lax.broadcasted_iota(jnp.int32, (2 * SB_HALF, SB_HALF), 1)
    return cond(j, s).astype(_MXU_DTYPE)


def _half_cumsums(x, tri2):
    out = []
    for h in range(2):
        xh = x[:, h * SB_HALF:(h + 1) * SB_HALF]
        hi = xh.astype(_MXU_DTYPE)
        lo = (xh - hi.astype(F32)).astype(_MXU_DTYPE)
        out.append(_dot(jnp.concatenate([hi, lo], axis=1), tri2))
    return out


def _log_sigmoid(z):
    return jnp.minimum(z, 0.0) - jnp.log(1.0 + jnp.exp(-jnp.abs(z)))


def _sb_specs(B, S):
    qb = (SWA_Q + 2 * SWA_KV) // LANES
    kb = qb + SB_W // LANES
    vb = kb + SB_W // LANES
    return [pl.BlockSpec((S, LANES), functools.partial(lambda b, p, c: (b, c + p), c=c)) for c in (qb, kb, vb)]


def _sb_fwd(proj, B, S, name, jobs=()):
    T = B * S
    tq = SB_TILE
    nq = S // tq

    def body(q_ref, k_ref, v_ref, y_ref, tot_ref):
        hm = _half_masks()
        ji = lax.broadcasted_iota(jnp.int32, (tq, tq), 0)
        si = lax.broadcasted_iota(jnp.int32, (tq, tq), 1)
        tri_after = _tri2(lambda j, s: j > s)
        causal = si < ji
        mark = _mark_lanes()

        def q_loop(qi, carry):
            r0 = pl.multiple_of(qi * tq, tq)
            q_pair = q_ref[pl.ds(r0, tq), :] * ATT_SCALE
            qms = [jnp.where(hm[a], q_pair, 0) for a in range(2)]

            def tile(c0, state, diagonal):
                kk = k_ref[pl.ds(c0, tq), :]
                vv = v_ref[pl.ds(c0, tq), :]
                two = range(2)
                z = [_dot_nt(qms[a], kk) for a in two]
                lb = [_log_sigmoid(z[a]) for a in two]
                l1m = [jnp.where(causal, lb[a] - z[a], 0.0) if diagonal else lb[a] - z[a] for a in two]
                cum = [_half_cumsums(l1m[a], tri_after) for a in two]
                tot = [[cum[a][h][:, 0:1] + l1m[a][:, h * SB_HALF:h * SB_HALF + 1] for h in two] for a in two]
                after = [jnp.concatenate([cum[a][0] + (state[a][1] + tot[a][1]), cum[a][1] + state[a][1]], axis=1)
                         for a in two]
                att = [jnp.exp(lb[a] + after[a]) for a in two]
                if diagonal:
                    att = [jnp.where(causal, att[a], 0.0) for a in two]
                acc = [state[a][0] + _dot(att[a].astype(_MXU_DTYPE), jnp.where(hm[a], vv, 0)) for a in two]
                car = [state[a][1] + (tot[a][0] + tot[a][1]) for a in two]
                return tuple((acc[a], car[a]) for a in two)

            def live(st):
                return jnp.maximum(jnp.max(st[0][1]), jnp.max(st[1][1])) > SB_DEAD

            def step(c):
                it, _, st = c
                st = tile(pl.multiple_of((qi - 1 - it) * tq, tq), st, False)
                return it + 1, live(st), st

            zero = (jnp.zeros((tq, LANES), F32), jnp.zeros((tq, 1), F32))
            state = tile(r0, (zero, zero), True)
            done, _, state = lax.while_loop(lambda c: (c[0] < qi) & c[1], step, (jnp.int32(0), live(state), state))
            y_ref[pl.ds(r0, tq), :] = state[0][0] + state[1][0]
            first = (qi - done).astype(F32)
            tot_ref[pl.ds(r0, tq), :] = jnp.where(mark, first, jnp.where(hm[0], state[0][1], state[1][1]))
            return carry

        lax.fori_loop(0, nq, q_loop, 0)

    out_spec = pl.BlockSpec((S, LANES), lambda b, p: (b, p))
    return _call(
        body, name=name, grid=(B, SB_W // LANES),
        out_shape=(jax.ShapeDtypeStruct((T, SB_W), F32), jax.ShapeDtypeStruct((T, SB_W), F32)),
        in_specs=_sb_specs(B, S), out_specs=(out_spec, out_spec), args=(proj, proj, proj),
        semantics=("parallel", "parallel"), jobs=jobs)


def _sb_bwd(proj, tot, dyb, B, S, name, jobs=()):
    T = B * S
    tq = SB_TILE
    nq = S // tq

    def body(q_ref, k_ref, v_ref, do_ref, tot_ref, dq_ref, dk_ref, dv_ref, dk_acc, dv_acc):
        hm = _half_masks()
        ji = lax.broadcasted_iota(jnp.int32, (tq, tq), 0)
        si = lax.broadcasted_iota(jnp.int32, (tq, tq), 1)
        tri_incl = _tri2(lambda j, s: j <= s)
        tri_excl = _tri2(lambda j, s: j < s)
        causal = si < ji
        mark = _mark_lanes()
        dk_acc[...] = jnp.zeros_like(dk_acc)
        dv_acc[...] = jnp.zeros_like(dv_acc)

        def q_loop(qi, carry):
            r0 = pl.multiple_of(qi * tq, tq)
            q_pair = q_ref[pl.ds(r0, tq), :] * ATT_SCALE
            do_pair = do_ref[pl.ds(r0, tq), :]
            tot_pair = tot_ref[pl.ds(r0, tq), :]
            qms = [jnp.where(hm[a], q_pair, 0) for a in range(2)]
            doms = [jnp.where(hm[a], do_pair, 0) for a in range(2)]
            totals = [jnp.max(jnp.where(hm[a] & ~mark, tot_pair, -jnp.inf), axis=1, keepdims=True) for a in range(2)]
            first = jnp.max(jnp.where(mark, tot_pair, -jnp.inf))
            first = jnp.where((first >= 0.0) & (first <= qi.astype(F32)), first, 0.0).astype(jnp.int32)

            def tile(c0, state, diagonal):
                kk = k_ref[pl.ds(c0, tq), :]
                vv = v_ref[pl.ds(c0, tq), :]
                ks = kk * ATT_SCALE
                two = range(2)
                last = SB_HALF - 1
                z = [_dot_nt(qms[a], kk) for a in two]
                d_att = [_dot_nt(doms[a], vv) for a in two]
                lb = [_log_sigmoid(z[a]) for a in two]
                l1m = [jnp.where(causal, lb[a] - z[a], 0.0) if diagonal else lb[a] - z[a] for a in two]
                cum = [_half_cumsums(l1m[a], tri_incl) for a in two]
                upto = [jnp.concatenate([cum[a][0] + state[a][1],
                                         cum[a][1] + (state[a][1] + cum[a][0][:, last:last + 1])], axis=1) for a in two]
                att = [jnp.exp(lb[a] + (totals[a] - upto[a])) for a in two]
                if diagonal:
                    att = [jnp.where(causal, att[a], 0.0) for a in two]
                d_log = [d_att[a] * att[a] for a in two]
                cumd = [_half_cumsums(d_log[a], tri_excl) for a in two]
                totd = [[cumd[a][h][:, last:last + 1] + d_log[a][:, h * SB_HALF + last:h * SB_HALF + last + 1]
                         for h in two] for a in two]
                before = [jnp.concatenate([cumd[a][0] + state[a][2], cumd[a][1] + (state[a][2] + totd[a][0])], axis=1)
                          for a in two]
                sig = [jnp.exp(lb[a]) for a in two]
                dz = [d_log[a] * (1.0 - sig[a]) - sig[a] * before[a] for a in two]
                if diagonal:
                    dz = [jnp.where(causal, dz[a], 0.0) for a in two]
                dzb = [dz[a].astype(_MXU_DTYPE) for a in two]
                dq = [state[a][0] + _dot(dzb[a], jnp.where(hm[a], ks, 0)) for a in two]
                dk_acc[pl.ds(c0, tq), :] += _dot_tn(dzb[0], qms[0]) + _dot_tn(dzb[1], qms[1])
                dv_acc[pl.ds(c0, tq), :] += (_dot_tn(att[0].astype(_MXU_DTYPE), doms[0])
                                             + _dot_tn(att[1].astype(_MXU_DTYPE), doms[1]))
                cp = [upto[a][:, tq - 1:tq] for a in two]
                cq = [state[a][2] + (totd[a][0] + totd[a][1]) for a in two]
                return tuple((dq[a], cp[a], cq[a]) for a in two)

            zero_col = jnp.zeros((tq, 1), F32)
            zero = (jnp.zeros((tq, LANES), F32), zero_col, zero_col)
            state = lax.fori_loop(first, qi, lambda kj, st: tile(pl.multiple_of(kj * tq, tq), st, False), (zero, zero))
            state = tile(r0, state, True)
            dq_ref[pl.ds(r0, tq), :] = (state[0][0] + state[1][0]).astype(dq_ref.dtype)
            return carry

        lax.fori_loop(0, nq, q_loop, 0)
        dk_ref[...] = dk_acc[...].astype(dk_ref.dtype)
        dv_ref[...] = dv_acc[...].astype(dv_ref.dtype)

    pair = pl.BlockSpec((S, LANES), lambda b, p: (b, p))
    out = jax.ShapeDtypeStruct((T, SB_W), _MXU_DTYPE)
    return _call(
        body, name=name, grid=(B, SB_W // LANES), out_shape=(out, out, out),
        in_specs=_sb_specs(B, S) + [pair, pair], out_specs=(pair, pair, pair),
        scratch_shapes=[pltpu.VMEM((S, LANES), F32), pltpu.VMEM((S, LANES), F32)],
        args=(proj, proj, proj, dyb, tot), semantics=("parallel", "parallel"), jobs=jobs)


def _layer_step(x, tgt, B, S, small, comm):
    run, big, part = comm.run, comm.big, comm.partial
    ffn1_w, ffn2_w = ("ffn1_down", "ffn1_gate", "ffn1_up"), ("ffn2_down", "ffn2_gate", "ffn2_up")

    h1 = run(_rms_fwd, x, small["ffn1_norm"], "ffn1_rms", ag=(FIRST_GATHERED,),
             cast=tuple(n for n in BIG_NAMES if n != FIRST_GATHERED))
    U1 = run(_mm_nt, h1, big["ffn1_up"], _MXU_DTYPE, "ffn1_up", ag=("ffn1_gate",))
    G1, A1 = run(_ffn_gate, h1, big["ffn1_gate"], U1, "ffn1_gate", ag=("ffn1_down",))
    x1 = run(_mm_nn, [(A1, big["ffn1_down"])], x, 0.5, F32, "ffn1_down", ag=("w_in",))
    h2 = run(_rms_fwd, x1, small["mix_norm"], "mix_rms")
    proj = run(_mm_nt, h2, big["w_in"], _MXU_DTYPE, "in_proj", ag=("w_out",), ag_early=AG_PARTS)
    ya = run(_swa_fwd, proj, small["swa_sinks"], B, S, "swa_fwd", ag=("ffn2_gate",))
    yb, tot = run(_sb_fwd, proj, B, S, "sb_fwd", ag=("ffn2_up",), ag_early=AG_PARTS)
    yn = _outnorm_fwd(ya, yb, small["swa_out_norm"], small["sb_out_norm"], "out_norm")
    x2 = run(_mm_nn, [(yn, big["w_out"])], x1, 1.0, F32, "out_proj")
    h3 = run(_rms_fwd, x2, small["ffn2_norm"], "ffn2_rms")
    G2, U2, A2 = run(_ffn_gu, h3, big["ffn2_gate"], big["ffn2_up"], "ffn2_gate_up", ag=("ffn2_down",),
                     ag_early=AG_PARTS)
    x3 = run(_mm_nn, [(A2, big["ffn2_down"])], x2, 0.5, F32, "ffn2_down")

    dx3, dx3b, d_final, loss = _loss_head(x3, small["final_norm"], tgt, "loss_head")

    dG2, dU2 = run(_ffn_bwd_act, dx3b, big["ffn2_down"], G2, U2, "ffn2_bwd_act")
    part["ffn2_down"] = run(_mm_tn, A2, dx3b, 0.5, _WIRE_DTYPE, "ffn2_dw_down")
    part["ffn2_gate"] = run(_mm_tn, dG2, h3, 1.0, _WIRE_DTYPE, "ffn2_dw_gate")
    part["ffn2_up"] = run(_mm_tn, dU2, h3, 1.0, _WIRE_DTYPE, "ffn2_dw_up")
    dh3 = run(_mm_nn, [(dG2, big["ffn2_gate"])], None, 1.0, F32, "ffn2_dh_gate", rs1=ffn2_w)
    dh3 = run(_mm_nn, [(dU2, big["ffn2_up"])], dh3, 1.0, F32, "ffn2_dh_up")
    dx2, dx2b, d_g2 = _rms_bwd(dh3, x2, small["ffn2_norm"], dx3, "ffn2_rms_bwd")

    part["w_out"] = run(_mm_tn, yn, dx2b, 1.0, _WIRE_DTYPE, "dw_out")
    dyn = run(_mm_nt, dx2b, big["w_out"], F32, "out_proj_bwd")
    dya, dyb, d_ga, d_gb = _outnorm_bwd(dyn, ya, yb, small["swa_out_norm"], small["sb_out_norm"], "out_norm_bwd")
    dqa, dka, dva, d_sinks = run(_swa_bwd, proj, small["swa_sinks"], dya, B, S, "swa_bwd", rs2=ffn2_w[:1])
    dqb, dkb, dvb = run(_sb_bwd, proj, tot, dyb, B, S, "sb_bwd", rs2=ffn2_w[1:])
    dproj = jnp.concatenate([dqa, dka, dva, dqb, dkb, dvb], axis=1)
    part["w_in"] = run(_mm_tn, dproj, h2, 1.0, _WIRE_DTYPE, "dw_in", adamw=("ffn2_down", "ffn2_gate"))
    dh2 = run(_mm_nn, [(dproj, big["w_in"])], None, 1.0, F32, "in_proj_bwd", rs1=("w_in", "w_out"))
    dx1, dx1b, d_gm = _rms_bwd(dh2, x1, small["mix_norm"], dx2, "mix_rms_bwd")

    dG1, dU1 = run(_ffn_bwd_act, dx1b, big["ffn1_down"], G1, U1, "ffn1_bwd_act", rs2=("w_in", "w_out"))
    part["ffn1_down"] = run(_mm_tn, A1, dx1b, 0.5, _WIRE_DTYPE, "ffn1_dw_down", adamw=("ffn2_up", "w_in", "w_out"))
    part["ffn1_gate"] = run(_mm_tn, dG1, h1, 1.0, _WIRE_DTYPE, "ffn1_dw_gate", rs1=("ffn1_down",))
    part["ffn1_up"] = run(_mm_tn, dU1, h1, 1.0, _WIRE_DTYPE, "ffn1_dw_up", rs1=("ffn1_gate",), rs2=("ffn1_down",))
    dh1 = run(_mm_nn, [(dG1, big["ffn1_gate"])], None, 1.0, F32, "ffn1_dh_gate", rs1=("ffn1_up",), rs2=("ffn1_gate",))
    dh1 = run(_mm_nn, [(dU1, big["ffn1_up"])], dh1, 1.0, F32, "ffn1_dh_up", rs2=("ffn1_up",))
    gx, _, d_g1 = _rms_bwd(dh1, x, small["ffn1_norm"], dx1, "ffn1_rms_bwd")

    d_small = {"ffn1_norm": d_g1, "mix_norm": d_gm, "swa_sinks": d_sinks[:, :N_SWA_HEADS], "swa_out_norm": d_ga,
               "sb_out_norm": d_gb, "ffn2_norm": d_g2, "final_norm": d_final}
    return loss, gx, d_small


MESH = pl.DeviceIdType.MESH
BIG_NAMES = ("ffn1_gate", "ffn1_up", "ffn1_down", "w_in", "w_out", "ffn2_gate", "ffn2_up", "ffn2_down")
FIRST_GATHERED = "ffn1_up"
_COMM_PARAMS = pltpu.CompilerParams(has_side_effects=True)


def _place():
    x, y, c = lax.axis_index("x"), lax.axis_index("y"), lax.axis_index("c")
    other_chips = [(1 - x, y), (x, 1 - y), (1 - x, 1 - y)]
    return x, y, c, other_chips


def _padded_rows(rows):
    full = N_DEV * rows
    return -(-full // _F_TILE) * _F_TILE


AG_PARTS = 4


def _row_parts(rows, n):
    units = rows // 16
    assert units * 16 == rows and units >= n
    out, off = [], 0
    for i in range(n):
        size = (units // n + (1 if i < units % n else 0)) * 16
        out.append((off, size))
        off += size
    return out


def _ag_job(shards, early_parts=AG_PARTS // 2):
    nw = len(shards)
    D = shards[0].shape[1]
    rows_w = [s.shape[0] for s in shards]
    full_w = [_padded_rows(r) for r in rows_w]
    pad_w = [f - N_DEV * r for f, r in zip(full_w, rows_w)]
    max_pad = max(max(pad_w), 16)
    n_parts = AG_PARTS
    parts_w = [_row_parts(r, n_parts) for r in rows_w]

    class Plan:
        def __init__(self, ins, outs, scratch):
            zbuf, send_sems, recv_sems, local_sems, zero_sems = scratch
            x, y, c, chips = _place()
            me, sibling = (x, y, c), (x, y, 1 - c)

            def rows(w, block, part=None):
                off, size = (0, rows_w[w]) if part is None else part
                px, py, pc = block
                start = pl.multiple_of((4 * px + 2 * py + pc) * rows_w[w] + off, 16)
                return outs[w].at[pl.ds(start, size), :]

            def copy(w, k, block, to, part=None, own=False):
                src = rows(w, block, part)
                if own:
                    src = ins[w] if part is None else ins[w].at[pl.ds(part[0], part[1]), :]
                return pltpu.make_async_remote_copy(
                    src_ref=src, dst_ref=rows(w, block, part), send_sem=send_sems.at[w, k],
                    recv_sem=recv_sems.at[w, k], device_id=to, device_id_type=MESH)

            def k_ici(j, p):
                return 1 + j * n_parts + p

            def k_on(j, p):
                return 1 + (3 + j) * n_parts + p

            self.zbuf = zbuf
            self.local = [pltpu.make_async_copy(zbuf.at[pl.ds(0, pad_w[w]), :],
                                                outs[w].at[pl.ds(N_DEV * rows_w[w], pad_w[w]), :], zero_sems.at[w])
                          for w in range(nw) if pad_w[w]]
            self.local += [pltpu.make_async_copy(ins[w], rows(w, me), local_sems.at[w]) for w in range(nw)]
            self.first = [[copy(w, 0, me, sibling, own=True)]
                          + [copy(w, k_ici(j, p), me, (*chip, c), part, own=True)
                             for p, part in enumerate(parts_w[w]) for j, chip in enumerate(chips)]
                          for w in range(nw)]
            self.arrive = [[copy(w, k_ici(j, p), (*chip, c), me, part)
                            for p, part in enumerate(parts_w[w]) for j, chip in enumerate(chips)] for w in range(nw)]
            self.passed = [[copy(w, k_on(j, p), (*chip, c), sibling, part)
                            for p, part in enumerate(parts_w[w]) for j, chip in enumerate(chips)] for w in range(nw)]
            self.from_sibling = [[copy(w, 0, sibling, me)]
                                 + [copy(w, k_on(j, p), (*chip, 1 - c), me, part)
                                    for p, part in enumerate(parts_w[w]) for j, chip in enumerate(chips)]
                                 for w in range(nw)]

    def start(ins, outs, scratch):
        plan = Plan(ins, outs, scratch)
        plan.zbuf[...] = jnp.zeros_like(plan.zbuf)
        for cp in plan.local:
            cp.start()
        for w in range(nw):
            for cp in plan.first[w]:
                cp.start()

    def mid(ins, outs, scratch, phase):
        plan = Plan(ins, outs, scratch)
        early = 3 * early_parts
        for w in range(nw):
            pairs = list(zip(plan.arrive[w], plan.passed[w]))
            for arrived, onward in (pairs[:early] if phase == 0 else pairs[early:]):
                arrived.wait_recv()
                onward.start()

    def finish(ins, outs, scratch):
        plan = Plan(ins, outs, scratch)
        for w in range(nw):
            for cp in plan.from_sibling[w]:
                cp.wait_recv()
        for w in range(nw):
            for cp in plan.first[w] + plan.passed[w]:
                cp.wait_send()
        for cp in plan.local:
            cp.wait()

    return _Job(
        ins=shards, out_shape=[jax.ShapeDtypeStruct((f, D), s.dtype) for f, s in zip(full_w, shards)],
        scratch=[pltpu.VMEM((max_pad, D), shards[0].dtype), pltpu.SemaphoreType.DMA((nw, 1 + 6 * n_parts)),
                 pltpu.SemaphoreType.DMA((nw, 1 + 6 * n_parts)), pltpu.SemaphoreType.DMA((nw,)),
                 pltpu.SemaphoreType.DMA((nw,))],
        start=start, mid=mid, finish=finish)


def _rs1_job(partials, rows_w):
    nw = len(partials)
    D = partials[0].shape[1]

    def copies(ins, outs, scratch):
        send_sems, recv_sems = scratch
        x, y, c, _ = _place()
        out = []
        for w in range(nw):
            r = rows_w[w]
            for q in range(4):
                src = ins[w].at[pl.ds(pl.multiple_of((2 * q + 1 - c) * r, 16), r), :]
                out.append(pltpu.make_async_remote_copy(
                    src_ref=src, dst_ref=outs[w].at[pl.ds(q * r, r), :], send_sem=send_sems.at[w, q],
                    recv_sem=recv_sems.at[w, q], device_id=(x, y, 1 - c), device_id_type=MESH))
        return out

    def start(ins, outs, scratch):
        for cp in copies(ins, outs, scratch):
            cp.start()

    def finish(ins, outs, scratch):
        for cp in copies(ins, outs, scratch):
            cp.wait()

    return _Job(
        ins=partials, out_shape=[jax.ShapeDtypeStruct((4 * r, D), p.dtype) for r, p in zip(rows_w, partials)],
        scratch=[pltpu.SemaphoreType.DMA((nw, 4)), pltpu.SemaphoreType.DMA((nw, 4))], start=start, finish=finish)


def _pair_sum(partial, from_sibling, rows, core, name):
    D = partial.shape[1]

    def body(core_ref, p_ref, s_ref, o_ref):
        o_ref[...] = (p_ref[...].astype(F32) + s_ref[...].astype(F32)).astype(o_ref.dtype)

    grid_spec = pltpu.PrefetchScalarGridSpec(
        num_scalar_prefetch=1, grid=(4,),
        in_specs=[pl.BlockSpec((rows, D), lambda q, core_ref: (2 * q + core_ref[0], 0)),
                  pl.BlockSpec((rows, D), lambda q, core_ref: (q, 0))],
        out_specs=pl.BlockSpec((rows, D), lambda q, core_ref: (q, 0)))
    return pl.pallas_call(
        body, name=name, grid_spec=grid_spec, out_shape=jax.ShapeDtypeStruct((4 * rows, D), partial.dtype),
        compiler_params=_params("arbitrary"),
    )(core, partial, from_sibling)


def _rs2_job(chip_sums, rows_w):
    nw = len(chip_sums)

    def copies(ins, outs, scratch):
        send_sems, recv_sems, local_sems = scratch
        x, y, c, chips = _place()
        my_chip = 2 * x + y
        out = []
        for w in range(nw):
            r = rows_w[w]
            mine = pl.ds(pl.multiple_of(my_chip * r, 16), r)
            out.append(pltpu.make_async_copy(ins[w].at[mine, :], outs[w].at[mine, :], local_sems.at[w]))
            for j, (qx, qy) in enumerate(chips):
                src = ins[w].at[pl.ds(pl.multiple_of((2 * qx + qy) * r, 16), r), :]
                out.append(pltpu.make_async_remote_copy(
                    src_ref=src, dst_ref=outs[w].at[mine, :], send_sem=send_sems.at[w, j],
                    recv_sem=recv_sems.at[w, j], device_id=(qx, qy, c), device_id_type=MESH))
        return out

    def start(ins, outs, scratch):
        for cp in copies(ins, outs, scratch):
            cp.start()

    def finish(ins, outs, scratch):
        for cp in copies(ins, outs, scratch):
            cp.wait()

    return _Job(
        ins=chip_sums, out_shape=[jax.ShapeDtypeStruct(s.shape, s.dtype) for s in chip_sums],
        scratch=[pltpu.SemaphoreType.DMA((nw, 3)), pltpu.SemaphoreType.DMA((nw, 3)), pltpu.SemaphoreType.DMA((nw,))],
        start=start, finish=finish)


class _Comm:
    def __init__(self, shards, state):
        self.shards, self.state = dict(shards), state
        self.rows = {n: st[0].shape[0] for n, st in state.items()}
        self.core = lax.axis_index("c").astype(jnp.int32).reshape(1)
        self.big, self.partial, self.chip_sums, self.slots, self.updates = {}, {}, {}, {}, {}

    def slots3(self, name):
        return self.slots[name].reshape(4, self.rows[name], -1)

    def run(self, fn, *args, ag=(), rs1=(), rs2=(), adamw=(), cast=(), ag_early=AG_PARTS // 2):
        jobs = []
        if cast:
            jobs.append(_cast_job([self.state[n][0] for n in cast], _WIRE_DTYPE))
        if ag:
            jobs.append(_ag_job([self.shards[n] for n in ag], ag_early))
        if rs1:
            jobs.append(_rs1_job([self.partial[n] for n in rs1], [self.rows[n] for n in rs1]))
        if rs2:
            jobs.append(_rs2_job([self.chip_sums[n] for n in rs2], [self.rows[n] for n in rs2]))
        for n in adamw:
            w2, m2, v2 = self.state[n]
            jobs.append(_adamw_job(w2, self.slots3(n), m2, v2))
        out, job_res = fn(*args, jobs=jobs)
        job_res = iter(job_res)
        if cast:
            self.shards.update(zip(cast, next(job_res)))
        if ag:
            self.big.update(zip(ag, next(job_res)))
        if rs1:
            for n, got in zip(rs1, next(job_res)):
                self.chip_sums[n] = _pair_sum(self.partial[n], got, self.rows[n], self.core, "pair_sum_" + n)
        if rs2:
            self.slots.update(zip(rs2, next(job_res)))
        for n in adamw:
            self.updates[n] = next(job_res)
        return out


SMALL_ROWS = 88


def _small_allreduce(vec):
    def body(v_ref, o_ref, gather, send_sems, recv_sems):
        x, y, c, _ = _place()
        my_id = 4 * x + 2 * y + c
        gather[my_id] = v_ref[...]
        copies = []
        for r in range(1, N_DEV):
            peer = (x ^ (r >> 2), y ^ ((r >> 1) & 1), c ^ (r & 1))
            cp = pltpu.make_async_remote_copy(src_ref=v_ref, dst_ref=gather.at[my_id], send_sem=send_sems.at[r - 1],
                                              recv_sem=recv_sems.at[r - 1], device_id=peer, device_id_type=MESH)
            cp.start()
            copies.append(cp)
        for cp in copies:
            cp.wait()
        acc = gather[0]
        for d in range(1, N_DEV):
            acc = acc + gather[d]
        o_ref[...] = acc

    vm = pl.BlockSpec(memory_space=pltpu.VMEM)
    return pl.pallas_call(
        body, name="small_allreduce", out_shape=jax.ShapeDtypeStruct(vec.shape, F32),
        in_specs=[vm], out_specs=vm,
        scratch_shapes=[pltpu.VMEM((N_DEV,) + vec.shape, F32), pltpu.SemaphoreType.DMA((N_DEV - 1,)),
                        pltpu.SemaphoreType.DMA((N_DEV - 1,))],
        compiler_params=_COMM_PARAMS,
    )(vec)


def _adamw_update(w, g, m, v):
    nm = ADAM_B1 * m + (1.0 - ADAM_B1) * g
    nv = ADAM_B2 * v + (1.0 - ADAM_B2) * jnp.square(g)
    m_hat = nm / (1.0 - ADAM_B1 ** ADAM_STEP)
    v_hat = nv / (1.0 - ADAM_B2 ** ADAM_STEP)
    return -ADAM_LR * (m_hat / (jnp.sqrt(v_hat) + ADAM_EPS) + ADAM_WD * w), nm, nv


def _adamw(w, g, m, v, name):
    R, C = w.shape
    tr = _tile(R, 256, 8)

    def body(w_ref, g_ref, m_ref, v_ref, d_ref, nm_ref, nv_ref):
        d_ref[...], nm_ref[...], nv_ref[...] = _adamw_update(w_ref[...], g_ref[...], m_ref[...], v_ref[...])

    spec = pl.BlockSpec((tr, C), lambda i: (i, 0))
    out = jax.ShapeDtypeStruct((R, C), F32)
    return pl.pallas_call(
        body, name=name, grid=(R // tr,), out_shape=(out, out, out),
        in_specs=[spec] * 4, out_specs=(spec, spec, spec),
        compiler_params=_params("parallel"),
    )(w, g, m, v)


def _adamw_slots(w, slots, m, v, name):
    R, C = w.shape
    tc = _tile(C, 512, LANES)
    spec = pl.BlockSpec((R, tc), lambda j: (0, j))
    out = jax.ShapeDtypeStruct((R, C), F32)
    return pl.pallas_call(
        functools.partial(_adamw_slots_body), name=name, grid=(C // tc,), out_shape=(out, out, out, out),
        in_specs=[spec, pl.BlockSpec((4, R, tc), lambda j: (0, 0, j)), spec, spec], out_specs=(spec, spec, spec, spec),
        compiler_params=_params("parallel"),
    )(w, slots, m, v)


def _adamw_slots_body(w_ref, s_ref, m_ref, v_ref, g_ref, d_ref, nm_ref, nv_ref):
    g = s_ref[0].astype(F32)
    for q in range(1, 4):
        g = g + s_ref[q].astype(F32)
    g_ref[...] = g
    d_ref[...], nm_ref[...], nv_ref[...] = _adamw_update(w_ref[...], g, m_ref[...], v_ref[...])


def _cast_job(arrays, dtype):
    C = arrays[0].shape[1]

    def specs(grid):
        total = 1
        for g in grid:
            total *= g
        tc = C // total
        assert tc * total == C and tc % LANES == 0, (C, grid)
        blocks = [pl.BlockSpec((a.shape[0], tc), lambda *ids: (0, _linear_step(grid, ids))) for a in arrays]
        return blocks, list(blocks)

    def each(ins, outs, scratch, step):
        for i_ref, o_ref in zip(ins, outs):
            o_ref[...] = i_ref[...].astype(o_ref.dtype)

    return _Job(ins=arrays, out_shape=[jax.ShapeDtypeStruct(a.shape, dtype) for a in arrays], specs=specs, each=each)


def _adamw_job(w, slots, m, v):
    R, C = w.shape

    n_slices = C // LANES

    def specs(grid):
        total = 1
        for g in grid:
            total *= g
        assert total >= n_slices, (grid, n_slices)
        col = lambda *ids: jnp.minimum(_linear_step(grid, ids), n_slices - 1)
        blk = pl.BlockSpec((R, LANES), lambda *ids: (0, col(*ids)))
        slot_blk = pl.BlockSpec((4, R, LANES), lambda *ids: (0, 0, col(*ids)))
        return [blk, slot_blk, blk, blk], [blk] * 4

    def each(ins, outs, scratch, step):
        @pl.when(step < n_slices)
        def _():
            _adamw_slots_body(*ins, *outs)

    out = jax.ShapeDtypeStruct((R, C), F32)
    return _Job(ins=[w, slots, m, v], out_shape=[out] * 4, specs=specs, each=each)


WEIGHT_NAMES = ("ffn1_norm", "ffn1_w_gate", "ffn1_w_up", "ffn1_w_down", "mix_norm", "w_in", "swa_sinks",
                "swa_out_norm", "sb_out_norm", "w_out", "ffn2_norm", "ffn2_w_gate", "ffn2_w_up", "ffn2_w_down",
                "final_norm")
SMALL_NAMES = ("ffn1_norm", "mix_norm", "swa_sinks", "swa_out_norm", "sb_out_norm", "ffn2_norm", "final_norm")
BIG_ARGS = {"ffn1_gate": ("ffn1_w_gate", True), "ffn1_up": ("ffn1_w_up", True), "ffn1_down": ("ffn1_w_down", False),
            "w_in": ("w_in", True), "w_out": ("w_out", False), "ffn2_gate": ("ffn2_w_gate", True),
            "ffn2_up": ("ffn2_w_up", True), "ffn2_down": ("ffn2_w_down", False)}


def _pack_small(parts):
    padded = [jnp.pad(p.reshape(1, -1), ((0, 0), (0, -p.size % LANES))) for p in parts]
    flat = jnp.concatenate(padded, axis=1)
    flat = jnp.pad(flat, ((0, 0), (0, SMALL_ROWS * LANES - flat.shape[1])))
    return flat.reshape(SMALL_ROWS, LANES)


def _unpack_small(block, shapes):
    flat = block.reshape(-1)
    out, off = [], 0
    for shp in shapes:
        n = 1
        for s in shp:
            n *= s
        out.append(flat[off:off + n].reshape(shp))
        off += n + (-n % LANES)
    return out


def kernel(x, ffn1_norm, ffn1_w_gate, ffn1_w_up, ffn1_w_down, mix_norm, w_in, swa_sinks, swa_out_norm, sb_out_norm, w_out, ffn2_norm, ffn2_w_gate, ffn2_w_up, ffn2_w_down, final_norm, loss_target, m_ffn1_norm, m_ffn1_w_gate, m_ffn1_w_up, m_ffn1_w_down, m_mix_norm, m_w_in, m_swa_sinks, m_swa_out_norm, m_sb_out_norm, m_w_out, m_ffn2_norm, m_ffn2_w_gate, m_ffn2_w_up, m_ffn2_w_down, m_final_norm, v_ffn1_norm, v_ffn1_w_gate, v_ffn1_w_up, v_ffn1_w_down, v_mix_norm, v_w_in, v_swa_sinks, v_swa_out_norm, v_sb_out_norm, v_w_out, v_ffn2_norm, v_ffn2_w_gate, v_ffn2_w_up, v_ffn2_w_down, v_final_norm):
    args = dict(locals())
    B, S, D = x.shape
    T = B * S
    weights = {n: args[n] for n in WEIGHT_NAMES}
    mom_m = {n: args["m_" + n] for n in WEIGHT_NAMES}
    mom_v = {n: args["v_" + n] for n in WEIGHT_NAMES}

    state = {}
    for name in BIG_NAMES:
        arg, transposed = BIG_ARGS[name]
        to_rows = (lambda t: t[0].T) if transposed else (lambda t: t[0])
        state[name] = tuple(to_rows(t[arg]) for t in (weights, mom_m, mom_v))
    comm = _Comm({FIRST_GATHERED: state[FIRST_GATHERED][0].astype(_WIRE_DTYPE)}, state)
    small = {n: weights[n].reshape(1, -1) for n in SMALL_NAMES}

    loss, gx, d_small = _layer_step(x.reshape(T, D), loss_target.reshape(T, D), B, S, small, comm)

    small_shapes = [(1, 1)] + [d_small[n].shape for n in SMALL_NAMES]
    reduced = _small_allreduce(_pack_small([loss[:, :1]] + [d_small[n] for n in SMALL_NAMES]))
    red = _unpack_small(reduced, small_shapes)
    loss_out = red[0].reshape(())
    g_small = dict(zip(SMALL_NAMES, red[1:]))

    grads, deltas, new_m, new_v = {}, {}, {}, {}
    for name in BIG_NAMES:
        arg, transposed = BIG_ARGS[name]
        back = (lambda t: t.T[None]) if transposed else (lambda t: t[None])
        res = comm.updates.get(name)
        if res is None:
            w2, m2, v2 = state[name]
            res = _adamw_slots(w2, comm.slots3(name), m2, v2, "adamw_" + name)
        grads[arg], deltas[arg], new_m[arg], new_v[arg] = [back(t) for t in res]
    shapes1 = [(1, weights[n].size) for n in SMALL_NAMES]
    packed = [_pack_small([t[n].reshape(1, -1) for n in SMALL_NAMES]) for t in (weights, g_small, mom_m, mom_v)]
    upd = _adamw(*packed, "adamw_small")
    for tgt_dict, block in zip((deltas, new_m, new_v), upd):
        for n, val in zip(SMALL_NAMES, _unpack_small(block, shapes1)):
            tgt_dict[n] = val.reshape(weights[n].shape)
    for n in SMALL_NAMES:
        grads[n] = g_small[n].reshape(weights[n].shape)

    return (loss_out, gx.reshape(B, S, D), *[grads[n] for n in WEIGHT_NAMES], *[deltas[n] for n in WEIGHT_NAMES],
            *[new_m[n] for n in WEIGHT_NAMES], *[new_v[n] for n in WEIGHT_NAMES])
```

```python
import functools

import jax
import jax.numpy as jnp
from jax import lax
from jax.experimental import pallas as pl
from jax.experimental.pallas import tpu as pltpu

F32 = jnp.float32
_MXU_DTYPE = jnp.bfloat16
_WIRE_DTYPE = jnp.bfloat16

EPS = 1e-6
HEAD_DIM = 64
N_SWA_HEADS = 16
N_SWA_KV = 4
N_SB_HEADS = 16
WINDOW = 128
SWA_Q = N_SWA_HEADS * HEAD_DIM
SWA_KV = N_SWA_KV * HEAD_DIM
SB_W = N_SB_HEADS * HEAD_DIM
IN_W = SWA_Q + 2 * SWA_KV + 3 * SB_W
LANES = 128
ATT_SCALE = HEAD_DIM ** -0.5

ADAM_LR = 0.001
ADAM_B1 = 0.9
ADAM_B2 = 0.999
ADAM_EPS = 1e-08
ADAM_WD = 0.01
ADAM_STEP = 10

N_DEV = 8
_VMEM_LIMIT_BYTES = 56 * 1024 * 1024
_F_TILE = 512


def _params(*semantics):
    return pltpu.CompilerParams(dimension_semantics=semantics, vmem_limit_bytes=_VMEM_LIMIT_BYTES)


def _tile(n, pref, align):
    t = min(n, pref)
    t -= t % align
    while t >= align:
        if n % t == 0:
            return t
        t -= align
    return n


def _dot(a, b):
    return lax.dot_general(a, b, (((1,), (0,)), ((), ())), preferred_element_type=F32)


def _dot_nt(a, b):
    return lax.dot_general(a, b, (((1,), (1,)), ((), ())), preferred_element_type=F32)


def _dot_tn(a, b):
    return lax.dot_general(a, b, (((0,), (0,)), ((), ())), preferred_element_type=F32)


class _Job:
    def __init__(self, ins, out_shape, scratch=(), start=None, finish=None, mid=None, each=None, specs=None):
        self.ins, self.out_shape, self.scratch = list(ins), list(out_shape), list(scratch)
        self.start, self.mid, self.finish, self.each, self.specs = start, mid, finish, each, specs


_JOB_MID_FRACTION = 0.6


def _linear_step(grid, ids):
    step = ids[0]
    for d in range(1, len(grid)):
        step = step * grid[d] + ids[d]
    return step


def _call(body, *, name, grid, in_specs, out_specs, out_shape, args, semantics, scratch_shapes=(), jobs=()):
    single = not isinstance(out_shape, (tuple, list))
    if not jobs:
        res = pl.pallas_call(body, name=name, grid=grid, in_specs=list(in_specs), out_specs=out_specs,
                             out_shape=out_shape, scratch_shapes=list(scratch_shapes),
                             compiler_params=_params(*semantics))(*args)
        return res, []
    base_out = [out_shape] if single else list(out_shape)
    base_out_specs = [out_specs] if single else list(out_specs)
    n_in, n_out, n_scr = len(args), len(base_out), len(scratch_shapes)
    any_spec = pl.BlockSpec(memory_space=pl.ANY)
    total = 1
    for g in grid:
        total *= g
    mid_step = min(total - 1, int(total * _JOB_MID_FRACTION))

    def wrapped(*refs):
        pos = n_in
        job_ins = []
        for job in jobs:
            job_ins.append(refs[pos:pos + len(job.ins)])
            pos += len(job.ins)
        outs = refs[pos:pos + n_out]
        pos += n_out
        job_outs = []
        for job in jobs:
            job_outs.append(refs[pos:pos + len(job.out_shape)])
            pos += len(job.out_shape)
        scr = refs[pos:pos + n_scr]
        pos += n_scr
        job_scr = []
        for job in jobs:
            job_scr.append(refs[pos:pos + len(job.scratch)])
            pos += len(job.scratch)
        bound = list(zip(jobs, job_ins, job_outs, job_scr))
        step = _linear_step(grid, [pl.program_id(d) for d in range(len(grid))])

        @pl.when(step == 0)
        def _():
            for job, ji, jo, js in bound:
                if job.start is not None:
                    job.start(ji, jo, js)

        @pl.when(step == mid_step)
        def _():
            for job, ji, jo, js in bound:
                if job.mid is not None:
                    job.mid(ji, jo, js, 0)

        body(*refs[:n_in], *outs, *scr)
        for job, ji, jo, js in bound:
            if job.each is not None:
                job.each(ji, jo, js, step)

        @pl.when(step == total - 1)
        def _():
            for job, ji, jo, js in bound:
                if job.mid is not None:
                    job.mid(ji, jo, js, 1)
            for job, ji, jo, js in bound:
                if job.finish is not None:
                    job.finish(ji, jo, js)

    all_args, all_in_specs = list(args), list(in_specs)
    all_out_shape, all_out_specs = list(base_out), list(base_out_specs)
    for job in jobs:
        job_in_specs, job_out_specs = (job.specs(grid) if job.specs is not None else
                                       ([any_spec] * len(job.ins), [any_spec] * len(job.out_shape)))
        all_args += job.ins
        all_in_specs += job_in_specs
        all_out_shape += job.out_shape
        all_out_specs += job_out_specs
    all_scratch = list(scratch_shapes) + [s for job in jobs for s in job.scratch]
    res = pl.pallas_call(
        wrapped, name=name, grid=grid, in_specs=all_in_specs, out_specs=tuple(all_out_specs),
        out_shape=tuple(all_out_shape), scratch_shapes=all_scratch,
        compiler_params=pltpu.CompilerParams(dimension_semantics=("arbitrary",) * len(grid),
                                             vmem_limit_bytes=_VMEM_LIMIT_BYTES,
                                             has_side_effects=any(job.start is not None for job in jobs)),
    )(*all_args)
    base = res[0] if single else tuple(res[:n_out])
    job_res, pos = [], n_out
    for job in jobs:
        job_res.append(tuple(res[pos:pos + len(job.out_shape)]))
        pos += len(job.out_shape)
    return base, job_res


def _rms_fwd(x, g, name, jobs=()):
    T, D = x.shape
    tm = _tile(T, 512, 16)

    def body(x_ref, g_ref, o_ref):
        xv = x_ref[...]
        r = lax.rsqrt(jnp.mean(xv * xv, axis=-1, keepdims=True) + EPS)
        o_ref[...] = (xv * r * g_ref[...]).astype(o_ref.dtype)

    return _call(
        body, name=name, grid=(T // tm,),
        out_shape=jax.ShapeDtypeStruct((T, D), _MXU_DTYPE),
        in_specs=[pl.BlockSpec((tm, D), lambda i: (i, 0)), pl.BlockSpec((1, D), lambda i: (0, 0))],
        out_specs=pl.BlockSpec((tm, D), lambda i: (i, 0)), args=(x, g), semantics=("parallel",), jobs=jobs)


def _rms_bwd_rows(dh, xv, g):
    r = lax.rsqrt(jnp.mean(xv * xv, axis=-1, keepdims=True) + EPS)
    xhat = xv * r
    u = dh * g
    dx = r * (u - xhat * jnp.mean(u * xhat, axis=-1, keepdims=True))
    return dx, dh * xhat


def _rms_bwd(dh, x, g, dres, name):
    T, D = x.shape
    tm = _tile(T, 256, 16)

    def body(dh_ref, x_ref, g_ref, dres_ref, dx_ref, dxb_ref, dg_ref):
        @pl.when(pl.program_id(0) == 0)
        def _():
            dg_ref[...] = jnp.zeros_like(dg_ref)

        dx, dgr = _rms_bwd_rows(dh_ref[...], x_ref[...], g_ref[...])
        dx = dres_ref[...] + dx
        dx_ref[...] = dx
        dxb_ref[...] = dx.astype(dxb_ref.dtype)
        dg_ref[...] += jnp.sum(dgr, axis=0, keepdims=True)

    row = pl.BlockSpec((tm, D), lambda i: (i, 0))
    vec = pl.BlockSpec((1, D), lambda i: (0, 0))
    return pl.pallas_call(
        body, name=name, grid=(T // tm,),
        out_shape=(jax.ShapeDtypeStruct((T, D), F32), jax.ShapeDtypeStruct((T, D), _MXU_DTYPE),
                   jax.ShapeDtypeStruct((1, D), F32)),
        in_specs=[row, row, vec, row], out_specs=(row, row, vec),
        compiler_params=_params("arbitrary"),
    )(dh, x, g, dres)


def _loss_head(x, g, tgt, name):
    T, D = x.shape
    tm = _tile(T, 256, 16)

    def body(x_ref, g_ref, t_ref, dx_ref, dxb_ref, dg_ref, loss_ref):
        @pl.when(pl.program_id(0) == 0)
        def _():
            dg_ref[...] = jnp.zeros_like(dg_ref)
            loss_ref[...] = jnp.zeros_like(loss_ref)

        xv = x_ref[...]
        gv = g_ref[...]
        r = lax.rsqrt(jnp.mean(xv * xv, axis=-1, keepdims=True) + EPS)
        xhat = xv * r
        diff = xhat * gv - t_ref[...]
        tok = jnp.mean(diff * diff, axis=-1, keepdims=True)
        loss_ref[...] += 0.5 * jnp.sum(tok, axis=0, keepdims=True)
        dy = diff / D
        u = dy * gv
        dx = r * (u - xhat * jnp.mean(u * xhat, axis=-1, keepdims=True))
        dx_ref[...] = dx
        dxb_ref[...] = dx.astype(dxb_ref.dtype)
        dg_ref[...] += jnp.sum(dy * xhat, axis=0, keepdims=True)

    row = pl.BlockSpec((tm, D), lambda i: (i, 0))
    vec = pl.BlockSpec((1, D), lambda i: (0, 0))
    return pl.pallas_call(
        body, name=name, grid=(T // tm,),
        out_shape=(jax.ShapeDtypeStruct((T, D), F32), jax.ShapeDtypeStruct((T, D), _MXU_DTYPE),
                   jax.ShapeDtypeStruct((1, D), F32), jax.ShapeDtypeStruct((1, LANES), F32)),
        in_specs=[row, vec, row],
        out_specs=(row, row, vec, pl.BlockSpec((1, LANES), lambda i: (0, 0))),
        compiler_params=_params("arbitrary"),
    )(x, g, tgt)


def _outnorm_fwd(ya, yb, ga, gb, name):
    T, W = ya.shape
    tm = _tile(T, 512, 16)

    def body(ya_ref, yb_ref, ga_ref, gb_ref, o_ref):
        for k, (y_ref, g_ref) in enumerate(((ya_ref, ga_ref), (yb_ref, gb_ref))):
            yv = y_ref[...]
            r = lax.rsqrt(jnp.mean(yv * yv, axis=-1, keepdims=True) + EPS)
            o_ref[:, k * W:(k + 1) * W] = (yv * r * g_ref[...]).astype(o_ref.dtype)

    row = pl.BlockSpec((tm, W), lambda i: (i, 0))
    vec = pl.BlockSpec((1, W), lambda i: (0, 0))
    return pl.pallas_call(
        body, name=name, grid=(T // tm,),
        out_shape=jax.ShapeDtypeStruct((T, 2 * W), _MXU_DTYPE),
        in_specs=[row, row, vec, vec], out_specs=pl.BlockSpec((tm, 2 * W), lambda i: (i, 0)),
        compiler_params=_params("parallel"),
    )(ya, yb, ga, gb)


def _outnorm_bwd(dyn, ya, yb, ga, gb, name):
    T, W = ya.shape
    tm = _tile(T, 256, 16)

    def body(d_ref, ya_ref, yb_ref, ga_ref, gb_ref, dya_ref, dyb_ref, dga_ref, dgb_ref):
        @pl.when(pl.program_id(0) == 0)
        def _():
            dga_ref[...] = jnp.zeros_like(dga_ref)
            dgb_ref[...] = jnp.zeros_like(dgb_ref)

        for k, (y_ref, g_ref, dy_ref, dg_ref) in enumerate(
                ((ya_ref, ga_ref, dya_ref, dga_ref), (yb_ref, gb_ref, dyb_ref, dgb_ref))):
            dy, dgr = _rms_bwd_rows(d_ref[:, k * W:(k + 1) * W], y_ref[...], g_ref[...])
            dy_ref[...] = dy.astype(dy_ref.dtype)
            dg_ref[...] += jnp.sum(dgr, axis=0, keepdims=True)

    row = pl.BlockSpec((tm, W), lambda i: (i, 0))
    vec = pl.BlockSpec((1, W), lambda i: (0, 0))
    return pl.pallas_call(
        body, name=name, grid=(T // tm,),
        out_shape=(jax.ShapeDtypeStruct((T, W), _MXU_DTYPE), jax.ShapeDtypeStruct((T, W), _MXU_DTYPE),
                   jax.ShapeDtypeStruct((1, W), F32), jax.ShapeDtypeStruct((1, W), F32)),
        in_specs=[pl.BlockSpec((tm, 2 * W), lambda i: (i, 0)), row, row, vec, vec],
        out_specs=(row, row, vec, vec),
        compiler_params=_params("arbitrary"),
    )(dyn, ya, yb, ga, gb)


_STRIP_ROWS = 256


def _strips(rows):
    step = min(rows, _STRIP_ROWS)
    return [slice(r, r + step) for r in range(0, rows, step)]


def _ffn_gu(h, wg_t, wu_t, name, jobs=()):
    T, D = h.shape
    Fp = wg_t.shape[0]
    tm = _tile(T, 1024, 16)
    tn = _tile(Fp, _F_TILE, LANES)

    def body(h_ref, wg_ref, wu_ref, g_ref, u_ref, a_ref):
        for rows in _strips(tm):
            hv = h_ref[rows, :]
            g = _dot_nt(hv, wg_ref[...])
            u = _dot_nt(hv, wu_ref[...])
            g_ref[rows, :] = g.astype(g_ref.dtype)
            u_ref[rows, :] = u.astype(u_ref.dtype)
            a_ref[rows, :] = (g * jax.nn.sigmoid(g) * u).astype(a_ref.dtype)

    act = pl.BlockSpec((tm, tn), lambda n, m: (m, n))
    wsp = pl.BlockSpec((tn, D), lambda n, m: (n, 0))
    out = jax.ShapeDtypeStruct((T, Fp), _MXU_DTYPE)
    return _call(
        body, name=name, grid=(Fp // tn, T // tm), out_shape=(out, out, out),
        in_specs=[pl.BlockSpec((tm, D), lambda n, m: (m, 0)), wsp, wsp],
        out_specs=(act, act, act), args=(h, wg_t, wu_t), semantics=("parallel", "parallel"), jobs=jobs)


def _ffn_bwd_act(dxb, wd, G, U, name, jobs=()):
    T, D = dxb.shape
    Fp = wd.shape[0]
    tm = _tile(T, 1024, 16)
    tn = _tile(Fp, _F_TILE, LANES)

    def body(e_ref, wd_ref, g_ref, u_ref, dg_ref, du_ref):
        for rows in _strips(tm):
            da = 0.5 * _dot_nt(e_ref[rows, :], wd_ref[...])
            g = g_ref[rows, :].astype(F32)
            u = u_ref[rows, :].astype(F32)
            s = jax.nn.sigmoid(g)
            du_ref[rows, :] = (da * (g * s)).astype(du_ref.dtype)
            dg_ref[rows, :] = (da * u * (s * (1.0 + g * (1.0 - s)))).astype(dg_ref.dtype)

    act = pl.BlockSpec((tm, tn), lambda m, n: (m, n))
    out = jax.ShapeDtypeStruct((T, Fp), _MXU_DTYPE)
    return _call(
        body, name=name, grid=(T // tm, Fp // tn), out_shape=(out, out),
        in_specs=[pl.BlockSpec((tm, D), lambda m, n: (m, 0)), pl.BlockSpec((tn, D), lambda m, n: (n, 0)),
                  act, act],
        out_specs=(act, act), args=(dxb, wd, G, U), semantics=("parallel", "parallel"), jobs=jobs)


def _ffn_gate(h, wg_t, U, name, jobs=()):
    T, D = h.shape
    Fp = wg_t.shape[0]
    tm = _tile(T, 1024, 16)
    tn = _tile(Fp, _F_TILE, LANES)

    def body(h_ref, wg_ref, u_ref, g_ref, a_ref):
        for rows in _strips(tm):
            g = _dot_nt(h_ref[rows, :], wg_ref[...])
            g_ref[rows, :] = g.astype(g_ref.dtype)
            a_ref[rows, :] = (g * jax.nn.sigmoid(g) * u_ref[rows, :].astype(F32)).astype(a_ref.dtype)

    act = pl.BlockSpec((tm, tn), lambda m, n: (m, n))
    out = jax.ShapeDtypeStruct((T, Fp), _MXU_DTYPE)
    return _call(
        body, name=name, grid=(T // tm, Fp // tn), out_shape=(out, out),
        in_specs=[pl.BlockSpec((tm, D), lambda m, n: (m, 0)), pl.BlockSpec((tn, D), lambda m, n: (n, 0)), act],
        out_specs=(act, act), args=(h, wg_t, U), semantics=("parallel", "parallel"), jobs=jobs)


def _mm_nt(a, b, out_dtype, name, jobs=()):
    M, K = a.shape
    N = b.shape[0]
    tm = _tile(M, 1024, 16)
    tn = _tile(N, 512, LANES)

    def body(a_ref, b_ref, o_ref):
        o_ref[...] = _dot_nt(a_ref[...], b_ref[...]).astype(o_ref.dtype)

    return _call(
        body, name=name, grid=(M // tm, N // tn), out_shape=jax.ShapeDtypeStruct((M, N), out_dtype),
        in_specs=[pl.BlockSpec((tm, K), lambda m, n: (m, 0)), pl.BlockSpec((tn, K), lambda m, n: (n, 0))],
        out_specs=pl.BlockSpec((tm, tn), lambda m, n: (m, n)), args=(a, b),
        semantics=("parallel", "parallel"), jobs=jobs)


_MM_OPERAND_BYTES = 26 * 1024 * 1024


def _k_tile(K, bytes_per_k, align):
    best = align
    for t in range(align, K + 1, align):
        if K % t == 0 and 2 * t * bytes_per_k <= _MM_OPERAND_BYTES:
            best = t
    return best


def _mm_nn(pairs, res, alpha, out_dtype, name, jobs=()):
    M, K = pairs[0][0].shape
    N = pairs[0][1].shape[1]
    n_pairs = len(pairs)
    tm = _tile(M, 1024, 16)
    tn = _tile(N, 1024, LANES)
    tk = _k_tile(K, n_pairs * (tm + tn) * pairs[0][0].dtype.itemsize, LANES)
    nk = K // tk

    def body(*refs):
        ab = refs[:2 * n_pairs]
        res_ref = refs[2 * n_pairs] if res is not None else None
        o_ref = refs[2 * n_pairs + (res is not None)]

        def finish(acc):
            out = alpha * acc
            if res_ref is not None:
                out = res_ref[...] + out
            o_ref[...] = out.astype(o_ref.dtype)

        part = _dot(ab[0][...], ab[1][...])
        for i in range(1, n_pairs):
            part = part + _dot(ab[2 * i][...], ab[2 * i + 1][...])
        if nk == 1:
            finish(part)
        else:
            acc_ref = refs[-1]
            k = pl.program_id(2)

            @pl.when(k == 0)
            def _():
                acc_ref[...] = part

            @pl.when(k > 0)
            def _():
                acc_ref[...] += part

            @pl.when(k == nk - 1)
            def _():
                finish(acc_ref[...])

    in_specs, args = [], []
    for a, b in pairs:
        in_specs += [pl.BlockSpec((tm, tk), lambda m, n, k: (m, k)), pl.BlockSpec((tk, tn), lambda m, n, k: (k, n))]
        args += [a, b]
    if res is not None:
        in_specs.append(pl.BlockSpec((tm, tn), lambda m, n, k: (m, n)))
        args.append(res)
    return _call(
        body, name=name, grid=(M // tm, N // tn, nk), out_shape=jax.ShapeDtypeStruct((M, N), out_dtype),
        in_specs=in_specs, out_specs=pl.BlockSpec((tm, tn), lambda m, n, k: (m, n)),
        scratch_shapes=[pltpu.VMEM((tm, tn), F32)] if nk > 1 else [], args=args,
        semantics=("parallel", "parallel", "arbitrary"), jobs=jobs)


def _mm_tn(a, b, alpha, out_dtype, name, jobs=()):
    K, M = a.shape
    N = b.shape[1]
    tm = _tile(M, 512, LANES)
    tn = _tile(N, 1024, LANES)

    def body(a_ref, b_ref, o_ref):
        o_ref[...] = (alpha * _dot_tn(a_ref[...], b_ref[...])).astype(o_ref.dtype)

    return _call(
        body, name=name, grid=(N // tn, M // tm), out_shape=jax.ShapeDtypeStruct((M, N), out_dtype),
        in_specs=[pl.BlockSpec((K, tm), lambda n, m: (0, m)), pl.BlockSpec((K, tn), lambda n, m: (0, n))],
        out_specs=pl.BlockSpec((tm, tn), lambda n, m: (m, n)), args=(a, b),
        semantics=("parallel", "parallel"), jobs=jobs)


def _half_masks():
    lane = lax.broadcasted_iota(jnp.int32, (1, LANES), 1)
    return (lane < HEAD_DIM, lane >= HEAD_DIM)


def _swap_halves(v):
    return pltpu.roll(v.astype(F32), HEAD_DIM, 1).astype(v.dtype)


def _swa_geometry(n):
    qi = lax.broadcasted_iota(jnp.int32, (WINDOW, 2 * WINDOW), 0)
    kp = lax.broadcasted_iota(jnp.int32, (WINDOW, 2 * WINDOW), 1)
    dist = (WINDOW + qi) - kp
    valid = (dist >= 0) & (dist < WINDOW) & ((n > 0) | (kp >= WINDOW))
    return dist.astype(F32), valid


def _swa_slope(h):
    return 2.0 ** (-8.0 * (h + 1) / N_SWA_HEADS)


def _swa_softmax(qk, sink, slope, distf, valid):
    s = qk * ATT_SCALE - slope * distf
    s = jnp.where(valid, s, -1e30)
    m = jnp.maximum(jnp.max(s, axis=1, keepdims=True), sink)
    p = jnp.exp(s - m)
    e_sink = jnp.exp(sink - m)
    den = jnp.sum(p, axis=1, keepdims=True) + e_sink
    return p / den, e_sink / den


def _swa_group_heads(g):
    return [(2 * pp + a, pp, a) for pp in (2 * g, 2 * g + 1) for a in range(2)]


def _swa_specs(B, S):
    nb = S // WINDOW
    kcol = SWA_Q // SWA_KV
    cur = lambda b, n: (b * nb + n, kcol)
    prev = lambda b, n: (b * nb + jnp.maximum(n - 1, 0), kcol)
    curv = lambda b, n: (b * nb + n, kcol + 1)
    prevv = lambda b, n: (b * nb + jnp.maximum(n - 1, 0), kcol + 1)
    q_spec = pl.BlockSpec((WINDOW, SWA_Q), lambda b, n: (b * nb + n, 0))
    kv = [pl.BlockSpec((WINDOW, SWA_KV), f) for f in (prev, cur, prevv, curv)]
    sink_spec = pl.BlockSpec(memory_space=pltpu.SMEM)
    return nb, q_spec, kv, sink_spec


def _swa_kv_views(kp_ref, kc_ref, vp_ref, vc_ref, g):
    hm = _half_masks()
    c0 = (g // 2) * LANES
    k_all = jnp.concatenate([kp_ref[:, c0:c0 + LANES], kc_ref[:, c0:c0 + LANES]], axis=0)
    v_all = jnp.concatenate([vp_ref[:, c0:c0 + LANES], vc_ref[:, c0:c0 + LANES]], axis=0)
    b = g % 2
    ks, vs = [None, None], [None, None]
    ks[b], vs[b] = k_all, v_all
    ks[1 - b], vs[1 - b] = _swap_halves(k_all), _swap_halves(v_all)
    ks = [jnp.where(hm[a], ks[a], 0) for a in range(2)]
    vs = [jnp.where(hm[a], vs[a], 0) for a in range(2)]
    return ks, vs


def _swa_fwd(proj, sinks, B, S, name, jobs=()):
    T = B * S
    nb, q_spec, kv_specs, sink_spec = _swa_specs(B, S)

    def body(sink_ref, q_ref, kp_ref, kc_ref, vp_ref, vc_ref, y_ref):
        hm = _half_masks()
        distf, valid = _swa_geometry(pl.program_id(1))
        for g in range(N_SWA_KV):
            ks, vs = _swa_kv_views(kp_ref, kc_ref, vp_ref, vc_ref, g)
            heads = _swa_group_heads(g)
            qk = [_dot_nt(jnp.where(hm[a], q_ref[:, pp * LANES:(pp + 1) * LANES], 0), ks[a]) for _, pp, a in heads]
            p = [_swa_softmax(qk[i], sink_ref[0, h], _swa_slope(h), distf, valid)[0] for i, (h, _, _) in enumerate(heads)]
            o = [_dot(p[i].astype(_MXU_DTYPE), vs[a]) for i, (_, _, a) in enumerate(heads)]
            for j, pp in enumerate((2 * g, 2 * g + 1)):
                y_ref[:, pp * LANES:(pp + 1) * LANES] = o[2 * j] + o[2 * j + 1]

    return _call(
        body, name=name, grid=(B, nb), out_shape=jax.ShapeDtypeStruct((T, SWA_Q), F32),
        in_specs=[sink_spec, q_spec] + kv_specs,
        out_specs=pl.BlockSpec((WINDOW, SWA_Q), lambda b, n: (b * nb + n, 0)),
        args=(sinks, proj, proj, proj, proj, proj), semantics=("parallel", "parallel"), jobs=jobs)


def _swa_bwd(proj, sinks, dya, B, S, name, jobs=()):
    T = B * S
    nb, q_spec, kv_specs, sink_spec = _swa_specs(B, S)

    def body(sink_ref, q_ref, kp_ref, kc_ref, vp_ref, vc_ref, do_ref,
             dq_ref, dk_ref, dv_ref, dsink_ref, dk_acc, dv_acc):
        b_id, n = pl.program_id(0), pl.program_id(1)
        hm = _half_masks()
        lane = lax.broadcasted_iota(jnp.int32, (1, LANES), 1)

        @pl.when((b_id == 0) & (n == 0))
        def _():
            dsink_ref[...] = jnp.zeros_like(dsink_ref)

        @pl.when(n == 0)
        def _():
            dk_acc[...] = jnp.zeros_like(dk_acc)
            dv_acc[...] = jnp.zeros_like(dv_acc)

        distf, valid = _swa_geometry(n)
        r_prev = pl.multiple_of(jnp.maximum(n - 1, 0) * WINDOW, WINDOW)
        r_cur = pl.multiple_of(n * WINDOW, WINDOW)
        dsink = jnp.zeros((1, LANES), F32)
        for g in range(N_SWA_KV):
            ks, vs = _swa_kv_views(kp_ref, kc_ref, vp_ref, vc_ref, g)
            heads = _swa_group_heads(g)
            four = range(len(heads))
            qms = [jnp.where(hm[a], q_ref[:, pp * LANES:(pp + 1) * LANES], 0) for _, pp, a in heads]
            doms = [jnp.where(hm[a], do_ref[:, pp * LANES:(pp + 1) * LANES], 0) for _, pp, a in heads]
            qk = [_dot_nt(qms[i], ks[heads[i][2]]) for i in four]
            dp = [_dot_nt(doms[i], vs[heads[i][2]]) for i in four]
            soft = [_swa_softmax(qk[i], sink_ref[0, heads[i][0]], _swa_slope(heads[i][0]), distf, valid) for i in four]
            p = [soft[i][0] for i in four]
            delta = [jnp.sum(p[i] * dp[i], axis=1, keepdims=True) for i in four]
            ds = [(p[i] * (dp[i] - delta[i]) * ATT_SCALE).astype(_MXU_DTYPE) for i in four]
            for i in four:
                dsink = dsink + jnp.where(lane == heads[i][0], -jnp.sum(soft[i][1] * delta[i]), 0.0)
            dq = [_dot(ds[i], ks[heads[i][2]]) for i in four]
            dk_h = [_dot_tn(ds[i], qms[i]) for i in four]
            dv_h = [_dot_tn(p[i].astype(_MXU_DTYPE), doms[i]) for i in four]
            for j, pp in enumerate((2 * g, 2 * g + 1)):
                dq_ref[:, pp * LANES:(pp + 1) * LANES] = (dq[2 * j] + dq[2 * j + 1]).astype(dq_ref.dtype)
            dk_g = [dk_h[a] + dk_h[2 + a] for a in range(2)]
            dv_g = [dv_h[a] + dv_h[2 + a] for a in range(2)]
            bsel = g % 2
            dk_t = dk_g[bsel] + pltpu.roll(dk_g[1 - bsel], HEAD_DIM, 1)
            dv_t = dv_g[bsel] + pltpu.roll(dv_g[1 - bsel], HEAD_DIM, 1)
            c0 = (g // 2) * LANES
            dk_acc[pl.ds(r_prev, WINDOW), c0:c0 + LANES] += dk_t[:WINDOW]
            dk_acc[pl.ds(r_cur, WINDOW), c0:c0 + LANES] += dk_t[WINDOW:]
            dv_acc[pl.ds(r_prev, WINDOW), c0:c0 + LANES] += dv_t[:WINDOW]
            dv_acc[pl.ds(r_cur, WINDOW), c0:c0 + LANES] += dv_t[WINDOW:]
        dsink_ref[...] += dsink

        @pl.when(n == nb - 1)
        def _():
            dk_ref[...] = dk_acc[...].astype(dk_ref.dtype)
            dv_ref[...] = dv_acc[...].astype(dv_ref.dtype)

    seq_kv = pl.BlockSpec((S, SWA_KV), lambda b, n: (b, 0))
    return _call(
        body, name=name, grid=(B, nb),
        out_shape=(jax.ShapeDtypeStruct((T, SWA_Q), _MXU_DTYPE), jax.ShapeDtypeStruct((T, SWA_KV), _MXU_DTYPE),
                   jax.ShapeDtypeStruct((T, SWA_KV), _MXU_DTYPE), jax.ShapeDtypeStruct((1, LANES), F32)),
        in_specs=[sink_spec, q_spec] + kv_specs + [pl.BlockSpec((WINDOW, SWA_Q), lambda b, n: (b * nb + n, 0))],
        out_specs=(pl.BlockSpec((WINDOW, SWA_Q), lambda b, n: (b * nb + n, 0)), seq_kv, seq_kv,
                   pl.BlockSpec((1, LANES), lambda b, n: (0, 0))),
        scratch_shapes=[pltpu.VMEM((S, SWA_KV), F32), pltpu.VMEM((S, SWA_KV), F32)],
        args=(sinks, proj, proj, proj, proj, proj, dya), semantics=("arbitrary", "arbitrary"), jobs=jobs)


SB_TILE = 256
SB_HALF = 128
SB_DEAD = -105.0


def _mark_lanes():
    lane = lax.broadcasted_iota(jnp.int32, (1, LANES), 1)
    return (lane == HEAD_DIM - 1) | (lane == LANES - 1)


def _tri2(cond):
    j = lax.broadcasted_iota(jnp.int32, (2 * SB_HALF, SB_HALF), 0) & (SB_HALF - 1)
    s = lax.broadcasted_iota(jnp.int32, (2 * SB_HALF, SB_HALF), 1)
    return cond(j, s).astype(_MXU_DTYPE)


def _half_cumsums(x, tri2):
    out = []
    for h in range(2):
        xh = x[:, h * SB_HALF:(h + 1) * SB_HALF]
        hi = xh.astype(_MXU_DTYPE)
        lo = (xh - hi.astype(F32)).astype(_MXU_DTYPE)
        out.append(_dot(jnp.concatenate([hi, lo], axis=1), tri2))
    return out


def _log_sigmoid(z):
    return jnp.minimum(z, 0.0) - jnp.log(1.0 + jnp.exp(-jnp.abs(z)))


def _sb_specs(B, S):
    qb = (SWA_Q + 2 * SWA_KV) // LANES
    kb = qb + SB_W // LANES
    vb = kb + SB_W // LANES
    return [pl.BlockSpec((S, LANES), functools.partial(lambda b, p, c: (b, c + p), c=c)) for c in (qb, kb, vb)]


def _sb_fwd(proj, B, S, name, jobs=()):
    T = B * S
    tq = SB_TILE
    nq = S // tq

    def body(q_ref, k_ref, v_ref, y_ref, tot_ref):
        hm = _half_masks()
        ji = lax.broadcasted_iota(jnp.int32, (tq, tq), 0)
        si = lax.broadcasted_iota(jnp.int32, (tq, tq), 1)
        tri_after = _tri2(lambda j, s: j > s)
        causal = si < ji
        mark = _mark_lanes()

        def q_block(qi, with_previous):
            r0 = qi * tq if isinstance(qi, int) else pl.multiple_of(qi * tq, tq)
            q_pair = q_ref[pl.ds(r0, tq), :] * ATT_SCALE
            qms = [jnp.where(hm[a], q_pair, 0) for a in range(2)]

            def tiles(blocks, state):
                two, nb = range(2), range(len(blocks))
                kk = [k_ref[pl.ds(c0, tq), :] for c0, _ in blocks]
                vv = [v_ref[pl.ds(c0, tq), :] for c0, _ in blocks]
                z = [[_dot_nt(qms[a], kk[b]) for a in two] for b in nb]
                lb = [[_log_sigmoid(z[b][a]) for a in two] for b in nb]
                l1m = [[jnp.where(causal, lb[b][a] - z[b][a], 0.0) if blocks[b][1] else lb[b][a] - z[b][a]
                        for a in two] for b in nb]
                cum = [[_half_cumsums(l1m[b][a], tri_after) for a in two] for b in nb]
                tot = [[[cum[b][a][h][:, 0:1] + l1m[b][a][:, h * SB_HALF:h * SB_HALF + 1] for h in two]
                        for a in two] for b in nb]
                car = [[state[a][1] for a in two]]
                for b in nb:
                    car.append([car[b][a] + (tot[b][a][0] + tot[b][a][1]) for a in two])
                after = [[jnp.concatenate([cum[b][a][0] + (car[b][a] + tot[b][a][1]), cum[b][a][1] + car[b][a]], axis=1)
                          for a in two] for b in nb]
                att = [[jnp.exp(lb[b][a] + after[b][a]) for a in two] for b in nb]
                att = [[jnp.where(causal, att[b][a], 0.0) if blocks[b][1] else att[b][a] for a in two] for b in nb]
                acc = [state[a][0] for a in two]
                for b in nb:
                    acc = [acc[a] + _dot(att[b][a].astype(_MXU_DTYPE), jnp.where(hm[a], vv[b], 0)) for a in two]
                return tuple((acc[a], car[-1][a]) for a in two)

            def live(st):
                return jnp.maximum(jnp.max(st[0][1]), jnp.max(st[1][1])) > SB_DEAD

            def step(c):
                it, _, st = c
                st = tiles([(pl.multiple_of((qi - 1 - it) * tq, tq), False)], st)
                return it + 1, live(st), st

            zero = (jnp.zeros((tq, LANES), F32), jnp.zeros((tq, 1), F32))
            if with_previous:
                state = tiles([(r0, True), (pl.multiple_of(r0 - tq, tq), False)], (zero, zero))
                done, _, state = lax.while_loop(lambda c: (c[0] < qi) & c[1], step, (jnp.int32(1), live(state), state))
            else:
                state, done = tiles([(r0, True)], (zero, zero)), 0
            y_ref[pl.ds(r0, tq), :] = state[0][0] + state[1][0]
            first = jnp.asarray(qi - done, F32)
            tot_ref[pl.ds(r0, tq), :] = jnp.where(mark, first, jnp.where(hm[0], state[0][1], state[1][1]))

        q_block(0, False)

        def q_loop(qi, carry):
            q_block(qi, True)
            return carry

        lax.fori_loop(1, nq, q_loop, 0)

    out_spec = pl.BlockSpec((S, LANES), lambda b, p: (b, p))
    return _call(
        body, name=name, grid=(B, SB_W // LANES),
        out_shape=(jax.ShapeDtypeStruct((T, SB_W), F32), jax.ShapeDtypeStruct((T, SB_W), F32)),
        in_specs=_sb_specs(B, S), out_specs=(out_spec, out_spec), args=(proj, proj, proj),
        semantics=("parallel", "parallel"), jobs=jobs)


def _sb_bwd(proj, tot, dyb, B, S, name, jobs=()):
    T = B * S
    tq = SB_TILE
    nq = S // tq

    def body(q_ref, k_ref, v_ref, do_ref, tot_ref, dq_ref, dk_ref, dv_ref, dk_acc, dv_acc):
        hm = _half_masks()
        ji = lax.broadcasted_iota(jnp.int32, (tq, tq), 0)
        si = lax.broadcasted_iota(jnp.int32, (tq, tq), 1)
        tri_incl = _tri2(lambda j, s: j <= s)
        tri_excl = _tri2(lambda j, s: j < s)
        causal = si < ji
        mark = _mark_lanes()
        dk_acc[...] = jnp.zeros_like(dk_acc)
        dv_acc[...] = jnp.zeros_like(dv_acc)

        def q_block(qi, with_previous):
            r0 = qi * tq if isinstance(qi, int) else pl.multiple_of(qi * tq, tq)
            q_pair = q_ref[pl.ds(r0, tq), :] * ATT_SCALE
            do_pair = do_ref[pl.ds(r0, tq), :]
            tot_pair = tot_ref[pl.ds(r0, tq), :]
            qms = [jnp.where(hm[a], q_pair, 0) for a in range(2)]
            doms = [jnp.where(hm[a], do_pair, 0) for a in range(2)]
            totals = [jnp.max(jnp.where(hm[a] & ~mark, tot_pair, -jnp.inf), axis=1, keepdims=True) for a in range(2)]

            def tiles(blocks, state):
                two, nb = range(2), range(len(blocks))
                last = SB_HALF - 1
                kk = [k_ref[pl.ds(c0, tq), :] for c0, _ in blocks]
                vv = [v_ref[pl.ds(c0, tq), :] for c0, _ in blocks]
                z = [[_dot_nt(qms[a], kk[b]) for a in two] for b in nb]
                d_att = [[_dot_nt(doms[a], vv[b]) for a in two] for b in nb]
                lb = [[_log_sigmoid(z[b][a]) for a in two] for b in nb]
                l1m = [[jnp.where(causal, lb[b][a] - z[b][a], 0.0) if blocks[b][1] else lb[b][a] - z[b][a]
                        for a in two] for b in nb]
                cum = [[_half_cumsums(l1m[b][a], tri_incl) for a in two] for b in nb]
                cp, upto = [[state[a][1] for a in two]], []
                for b in nb:
                    upto.append([jnp.concatenate(
                        [cum[b][a][0] + cp[b][a], cum[b][a][1] + (cp[b][a] + cum[b][a][0][:, last:last + 1])], axis=1)
                        for a in two])
                    cp.append([upto[b][a][:, tq - 1:tq] for a in two])
                att = [[jnp.exp(lb[b][a] + (totals[a] - upto[b][a])) for a in two] for b in nb]
                att = [[jnp.where(causal, att[b][a], 0.0) if blocks[b][1] else att[b][a] for a in two] for b in nb]
                d_log = [[d_att[b][a] * att[b][a] for a in two] for b in nb]
                cumd = [[_half_cumsums(d_log[b][a], tri_excl) for a in two] for b in nb]
                totd = [[[cumd[b][a][h][:, last:last + 1] + d_log[b][a][:, h * SB_HALF + last:h * SB_HALF + last + 1]
                          for h in two] for a in two] for b in nb]
                cq = [[state[a][2] for a in two]]
                for b in nb:
                    cq.append([cq[b][a] + (totd[b][a][0] + totd[b][a][1]) for a in two])
                before = [[jnp.concatenate([cumd[b][a][0] + cq[b][a], cumd[b][a][1] + (cq[b][a] + totd[b][a][0])], axis=1)
                           for a in two] for b in nb]
                sig = [[jnp.exp(lb[b][a]) for a in two] for b in nb]
                dz = [[d_log[b][a] * (1.0 - sig[b][a]) - sig[b][a] * before[b][a] for a in two] for b in nb]
                dz = [[jnp.where(causal, dz[b][a], 0.0) if blocks[b][1] else dz[b][a] for a in two] for b in nb]
                dzb = [[dz[b][a].astype(_MXU_DTYPE) for a in two] for b in nb]
                dq = [state[a][0] for a in two]
                for b, (c0, _) in enumerate(blocks):
                    ks = kk[b] * ATT_SCALE
                    dq = [dq[a] + _dot(dzb[b][a], jnp.where(hm[a], ks, 0)) for a in two]
                    dk_acc[pl.ds(c0, tq), :] += _dot_tn(dzb[b][0], qms[0]) + _dot_tn(dzb[b][1], qms[1])
                    dv_acc[pl.ds(c0, tq), :] += (_dot_tn(att[b][0].astype(_MXU_DTYPE), doms[0])
                                                 + _dot_tn(att[b][1].astype(_MXU_DTYPE), doms[1]))
                return tuple((dq[a], cp[-1][a], cq[-1][a]) for a in two)

            zero_col = jnp.zeros((tq, 1), F32)
            zero = (jnp.zeros((tq, LANES), F32), zero_col, zero_col)
            if with_previous:
                first = jnp.max(jnp.where(mark, tot_pair, -jnp.inf))
                first = jnp.where((first >= 0.0) & (first <= (qi - 1).astype(F32)), first, 0.0).astype(jnp.int32)
                state = lax.fori_loop(first, qi - 1,
                                      lambda kj, st: tiles([(pl.multiple_of(kj * tq, tq), False)], st), (zero, zero))
                state = tiles([(pl.multiple_of(r0 - tq, tq), False), (r0, True)], state)
            else:
                state = tiles([(r0, True)], (zero, zero))
            dq_ref[pl.ds(r0, tq), :] = (state[0][0] + state[1][0]).astype(dq_ref.dtype)

        q_block(0, False)

        def q_loop(qi, carry):
            q_block(qi, True)
            return carry

        lax.fori_loop(1, nq, q_loop, 0)
        dk_ref[...] = dk_acc[...].astype(dk_ref.dtype)
        dv_ref[...] = dv_acc[...].astype(dv_ref.dtype)

    pair = pl.BlockSpec((S, LANES), lambda b, p: (b, p))
    out = jax.ShapeDtypeStruct((T, SB_W), _MXU_DTYPE)
    return _call(
        body, name=name, grid=(B, SB_W // LANES), out_shape=(out, out, out),
        in_specs=_sb_specs(B, S) + [pair, pair], out_specs=(pair, pair, pair),
        scratch_shapes=[pltpu.VMEM((S, LANES), F32), pltpu.VMEM((S, LANES), F32)],
        args=(proj, proj, proj, dyb, tot), semantics=("parallel", "parallel"), jobs=jobs)


def _layer_step(x, tgt, B, S, small, comm):
    run, big, part = comm.run, comm.big, comm.partial
    ffn1_w, ffn2_w = ("ffn1_down", "ffn1_gate", "ffn1_up"), ("ffn2_down", "ffn2_gate", "ffn2_up")

    h1 = run(_rms_fwd, x, small["ffn1_norm"], "ffn1_rms", ag=(FIRST_GATHERED,),
             cast=tuple(n for n in BIG_NAMES if n != FIRST_GATHERED))
    U1 = run(_mm_nt, h1, big["ffn1_up"], _MXU_DTYPE, "ffn1_up", ag=("ffn1_gate",))
    G1, A1 = run(_ffn_gate, h1, big["ffn1_gate"], U1, "ffn1_gate", ag=("ffn1_down",))
    x1 = run(_mm_nn, [(A1, big["ffn1_down"])], x, 0.5, F32, "ffn1_down", ag=("w_in",))
    h2 = run(_rms_fwd, x1, small["mix_norm"], "mix_rms")
    proj = run(_mm_nt, h2, big["w_in"], _MXU_DTYPE, "in_proj", ag=("w_out",), ag_early=AG_PARTS)
    ya = run(_swa_fwd, proj, small["swa_sinks"], B, S, "swa_fwd", ag=("ffn2_gate",))
    yb, tot = run(_sb_fwd, proj, B, S, "sb_fwd", ag=("ffn2_up",), ag_early=AG_PARTS)
    yn = _outnorm_fwd(ya, yb, small["swa_out_norm"], small["sb_out_norm"], "out_norm")
    x2 = run(_mm_nn, [(yn, big["w_out"])], x1, 1.0, F32, "out_proj")
    h3 = run(_rms_fwd, x2, small["ffn2_norm"], "ffn2_rms")
    G2, U2, A2 = run(_ffn_gu, h3, big["ffn2_gate"], big["ffn2_up"], "ffn2_gate_up", ag=("ffn2_down",),
                     ag_early=AG_PARTS)
    x3 = run(_mm_nn, [(A2, big["ffn2_down"])], x2, 0.5, F32, "ffn2_down")

    dx3, dx3b, d_final, loss = _loss_head(x3, small["final_norm"], tgt, "loss_head")

    dG2, dU2 = run(_ffn_bwd_act, dx3b, big["ffn2_down"], G2, U2, "ffn2_bwd_act")
    part["ffn2_down"] = run(_mm_tn, A2, dx3b, 0.5, _WIRE_DTYPE, "ffn2_dw_down")
    part["ffn2_gate"] = run(_mm_tn, dG2, h3, 1.0, _WIRE_DTYPE, "ffn2_dw_gate")
    part["ffn2_up"] = run(_mm_tn, dU2, h3, 1.0, _WIRE_DTYPE, "ffn2_dw_up")
    dh3 = run(_mm_nn, [(dG2, big["ffn2_gate"])], None, 1.0, F32, "ffn2_dh_gate", rs1=ffn2_w)
    dh3 = run(_mm_nn, [(dU2, big["ffn2_up"])], dh3, 1.0, F32, "ffn2_dh_up")
    dx2, dx2b, d_g2 = _rms_bwd(dh3, x2, small["ffn2_norm"], dx3, "ffn2_rms_bwd")

    part["w_out"] = run(_mm_tn, yn, dx2b, 1.0, _WIRE_DTYPE, "dw_out")
    dyn = run(_mm_nt, dx2b, big["w_out"], F32, "out_proj_bwd")
    dya, dyb, d_ga, d_gb = _outnorm_bwd(dyn, ya, yb, small["swa_out_norm"], small["sb_out_norm"], "out_norm_bwd")
    dqa, dka, dva, d_sinks = run(_swa_bwd, proj, small["swa_sinks"], dya, B, S, "swa_bwd", rs2=ffn2_w[:1])
    dqb, dkb, dvb = run(_sb_bwd, proj, tot, dyb, B, S, "sb_bwd", rs2=ffn2_w[1:])
    dproj = jnp.concatenate([dqa, dka, dva, dqb, dkb, dvb], axis=1)
    part["w_in"] = run(_mm_tn, dproj, h2, 1.0, _WIRE_DTYPE, "dw_in", adamw=("ffn2_down", "ffn2_gate"))
    dh2 = run(_mm_nn, [(dproj, big["w_in"])], None, 1.0, F32, "in_proj_bwd", rs1=("w_in", "w_out"))
    dx1, dx1b, d_gm = _rms_bwd(dh2, x1, small["mix_norm"], dx2, "mix_rms_bwd")

    dG1, dU1 = run(_ffn_bwd_act, dx1b, big["ffn1_down"], G1, U1, "ffn1_bwd_act", rs2=("w_in", "w_out"))
    part["ffn1_down"] = run(_mm_tn, A1, dx1b, 0.5, _WIRE_DTYPE, "ffn1_dw_down", adamw=("ffn2_up", "w_in", "w_out"))
    part["ffn1_gate"] = run(_mm_tn, dG1, h1, 1.0, _WIRE_DTYPE, "ffn1_dw_gate", rs1=("ffn1_down",))
    part["ffn1_up"] = run(_mm_tn, dU1, h1, 1.0, _WIRE_DTYPE, "ffn1_dw_up", rs1=("ffn1_gate",), rs2=("ffn1_down",))
    dh1 = run(_mm_nn, [(dG1, big["ffn1_gate"])], None, 1.0, F32, "ffn1_dh_gate", rs1=("ffn1_up",), rs2=("ffn1_gate",))
    dh1 = run(_mm_nn, [(dU1, big["ffn1_up"])], dh1, 1.0, F32, "ffn1_dh_up", rs2=("ffn1_up",))
    gx, _, d_g1 = _rms_bwd(dh1, x, small["ffn1_norm"], dx1, "ffn1_rms_bwd")

    d_small = {"ffn1_norm": d_g1, "mix_norm": d_gm, "swa_sinks": d_sinks[:, :N_SWA_HEADS], "swa_out_norm": d_ga,
               "sb_out_norm": d_gb, "ffn2_norm": d_g2, "final_norm": d_final}
    return loss, gx, d_small


MESH = pl.DeviceIdType.MESH
BIG_NAMES = ("ffn1_gate", "ffn1_up", "ffn1_down", "w_in", "w_out", "ffn2_gate", "ffn2_up", "ffn2_down")
FIRST_GATHERED = "ffn1_up"
_COMM_PARAMS = pltpu.CompilerParams(has_side_effects=True)


def _place():
    x, y, c = lax.axis_index("x"), lax.axis_index("y"), lax.axis_index("c")
    other_chips = [(1 - x, y), (x, 1 - y), (1 - x, 1 - y)]
    return x, y, c, other_chips


def _padded_rows(rows):
    full = N_DEV * rows
    return -(-full // _F_TILE) * _F_TILE


AG_PARTS = 4


def _row_parts(rows, n):
    units = rows // 16
    assert units * 16 == rows and units >= n
    out, off = [], 0
    for i in range(n):
        size = (units // n + (1 if i < units % n else 0)) * 16
        out.append((off, size))
        off += size
    return out


def _ag_job(shards, early_parts=AG_PARTS // 2):
    nw = len(shards)
    D = shards[0].shape[1]
    rows_w = [s.shape[0] for s in shards]
    full_w = [_padded_rows(r) for r in rows_w]
    pad_w = [f - N_DEV * r for f, r in zip(full_w, rows_w)]
    max_pad = max(max(pad_w), 16)
    n_parts = AG_PARTS
    parts_w = [_row_parts(r, n_parts) for r in rows_w]

    class Plan:
        def __init__(self, ins, outs, scratch):
            zbuf, send_sems, recv_sems, local_sems, zero_sems = scratch
            x, y, c, chips = _place()
            me, sibling = (x, y, c), (x, y, 1 - c)

            def rows(w, block, part=None):
                off, size = (0, rows_w[w]) if part is None else part
                px, py, pc = block
                start = pl.multiple_of((4 * px + 2 * py + pc) * rows_w[w] + off, 16)
                return outs[w].at[pl.ds(start, size), :]

            def copy(w, k, block, to, part=None, own=False):
                src = rows(w, block, part)
                if own:
                    src = ins[w] if part is None else ins[w].at[pl.ds(part[0], part[1]), :]
                return pltpu.make_async_remote_copy(
                    src_ref=src, dst_ref=rows(w, block, part), send_sem=send_sems.at[w, k],
                    recv_sem=recv_sems.at[w, k], device_id=to, device_id_type=MESH)

            def k_ici(j, p):
                return 1 + j * n_parts + p

            def k_on(j, p):
                return 1 + (3 + j) * n_parts + p

            self.zbuf = zbuf
            self.local = [pltpu.make_async_copy(zbuf.at[pl.ds(0, pad_w[w]), :],
                                                outs[w].at[pl.ds(N_DEV * rows_w[w], pad_w[w]), :], zero_sems.at[w])
                          for w in range(nw) if pad_w[w]]
            self.local += [pltpu.make_async_copy(ins[w], rows(w, me), local_sems.at[w]) for w in range(nw)]
            self.first = [[copy(w, 0, me, sibling, own=True)]
                          + [copy(w, k_ici(j, p), me, (*chip, c), part, own=True)
                             for p, part in enumerate(parts_w[w]) for j, chip in enumerate(chips)]
                          for w in range(nw)]
            self.arrive = [[copy(w, k_ici(j, p), (*chip, c), me, part)
                            for p, part in enumerate(parts_w[w]) for j, chip in enumerate(chips)] for w in range(nw)]
            self.passed = [[copy(w, k_on(j, p), (*chip, c), sibling, part)
                            for p, part in enumerate(parts_w[w]) for j, chip in enumerate(chips)] for w in range(nw)]
            self.from_sibling = [[copy(w, 0, sibling, me)]
                                 + [copy(w, k_on(j, p), (*chip, 1 - c), me, part)
                                    for p, part in enumerate(parts_w[w]) for j, chip in enumerate(chips)]
                                 for w in range(nw)]

    def start(ins, outs, scratch):
        plan = Plan(ins, outs, scratch)
        plan.zbuf[...] = jnp.zeros_like(plan.zbuf)
        for cp in plan.local:
            cp.start()
        for w in range(nw):
            for cp in plan.first[w]:
                cp.start()

    def mid(ins, outs, scratch, phase):
        plan = Plan(ins, outs, scratch)
        early = 3 * early_parts
        for w in range(nw):
            pairs = list(zip(plan.arrive[w], plan.passed[w]))
            for arrived, onward in (pairs[:early] if phase == 0 else pairs[early:]):
                arrived.wait_recv()
                onward.start()

    def finish(ins, outs, scratch):
        plan = Plan(ins, outs, scratch)
        for w in range(nw):
            for cp in plan.from_sibling[w]:
                cp.wait_recv()
        for w in range(nw):
            for cp in plan.first[w] + plan.passed[w]:
                cp.wait_send()
        for cp in plan.local:
            cp.wait()

    return _Job(
        ins=shards, out_shape=[jax.ShapeDtypeStruct((f, D), s.dtype) for f, s in zip(full_w, shards)],
        scratch=[pltpu.VMEM((max_pad, D), shards[0].dtype), pltpu.SemaphoreType.DMA((nw, 1 + 6 * n_parts)),
                 pltpu.SemaphoreType.DMA((nw, 1 + 6 * n_parts)), pltpu.SemaphoreType.DMA((nw,)),
                 pltpu.SemaphoreType.DMA((nw,))],
        start=start, mid=mid, finish=finish)


def _rs1_job(partials, rows_w):
    nw = len(partials)
    D = partials[0].shape[1]

    def copies(ins, outs, scratch):
        send_sems, recv_sems = scratch
        x, y, c, _ = _place()
        out = []
        for w in range(nw):
            r = rows_w[w]
            for q in range(4):
                src = ins[w].at[pl.ds(pl.multiple_of((2 * q + 1 - c) * r, 16), r), :]
                out.append(pltpu.make_async_remote_copy(
                    src_ref=src, dst_ref=outs[w].at[pl.ds(q * r, r), :], send_sem=send_sems.at[w, q],
                    recv_sem=recv_sems.at[w, q], device_id=(x, y, 1 - c), device_id_type=MESH))
        return out

    def start(ins, outs, scratch):
        for cp in copies(ins, outs, scratch):
            cp.start()

    def finish(ins, outs, scratch):
        for cp in copies(ins, outs, scratch):
            cp.wait()

    return _Job(
        ins=partials, out_shape=[jax.ShapeDtypeStruct((4 * r, D), p.dtype) for r, p in zip(rows_w, partials)],
        scratch=[pltpu.SemaphoreType.DMA((nw, 4)), pltpu.SemaphoreType.DMA((nw, 4))], start=start, finish=finish)


def _pair_sum(partial, from_sibling, rows, core, name):
    D = partial.shape[1]

    def body(core_ref, p_ref, s_ref, o_ref):
        o_ref[...] = (p_ref[...].astype(F32) + s_ref[...].astype(F32)).astype(o_ref.dtype)

    grid_spec = pltpu.PrefetchScalarGridSpec(
        num_scalar_prefetch=1, grid=(4,),
        in_specs=[pl.BlockSpec((rows, D), lambda q, core_ref: (2 * q + core_ref[0], 0)),
                  pl.BlockSpec((rows, D), lambda q, core_ref: (q, 0))],
        out_specs=pl.BlockSpec((rows, D), lambda q, core_ref: (q, 0)))
    return pl.pallas_call(
        body, name=name, grid_spec=grid_spec, out_shape=jax.ShapeDtypeStruct((4 * rows, D), partial.dtype),
        compiler_params=_params("arbitrary"),
    )(core, partial, from_sibling)


def _rs2_job(chip_sums, rows_w):
    nw = len(chip_sums)

    def copies(ins, outs, scratch):
        send_sems, recv_sems, local_sems = scratch
        x, y, c, chips = _place()
        my_chip = 2 * x + y
        out = []
        for w in range(nw):
            r = rows_w[w]
            mine = pl.ds(pl.multiple_of(my_chip * r, 16), r)
            out.append(pltpu.make_async_copy(ins[w].at[mine, :], outs[w].at[mine, :], local_sems.at[w]))
            for j, (qx, qy) in enumerate(chips):
                src = ins[w].at[pl.ds(pl.multiple_of((2 * qx + qy) * r, 16), r), :]
                out.append(pltpu.make_async_remote_copy(
                    src_ref=src, dst_ref=outs[w].at[mine, :], send_sem=send_sems.at[w, j],
                    recv_sem=recv_sems.at[w, j], device_id=(qx, qy, c), device_id_type=MESH))
        return out

    def start(ins, outs, scratch):
        for cp in copies(ins, outs, scratch):
            cp.start()

    def finish(ins, outs, scratch):
        for cp in copies(ins, outs, scratch):
            cp.wait()

    return _Job(
        ins=chip_sums, out_shape=[jax.ShapeDtypeStruct(s.shape, s.dtype) for s in chip_sums],
        scratch=[pltpu.SemaphoreType.DMA((nw, 3)), pltpu.SemaphoreType.DMA((nw, 3)), pltpu.SemaphoreType.DMA((nw,))],
        start=start, finish=finish)


class _Comm:
    def __init__(self, shards, state):
        self.shards, self.state = dict(shards), state
        self.rows = {n: st[0].shape[0] for n, st in state.items()}
        self.core = lax.axis_index("c").astype(jnp.int32).reshape(1)
        self.big, self.partial, self.chip_sums, self.slots, self.updates = {}, {}, {}, {}, {}

    def slots3(self, name):
        return self.slots[name].reshape(4, self.rows[name], -1)

    def run(self, fn, *args, ag=(), rs1=(), rs2=(), adamw=(), cast=(), ag_early=AG_PARTS // 2):
        jobs = []
        if cast:
            jobs.append(_cast_job([self.state[n][0] for n in cast], _WIRE_DTYPE))
        if ag:
            jobs.append(_ag_job([self.shards[n] for n in ag], ag_early))
        if rs1:
            jobs.append(_rs1_job([self.partial[n] for n in rs1], [self.rows[n] for n in rs1]))
        if rs2:
            jobs.append(_rs2_job([self.chip_sums[n] for n in rs2], [self.rows[n] for n in rs2]))
        for n in adamw:
            w2, m2, v2 = self.state[n]
            jobs.append(_adamw_job(w2, self.slots3(n), m2, v2))
        out, job_res = fn(*args, jobs=jobs)
        job_res = iter(job_res)
        if cast:
            self.shards.update(zip(cast, next(job_res)))
        if ag:
            self.big.update(zip(ag, next(job_res)))
        if rs1:
            for n, got in zip(rs1, next(job_res)):
                self.chip_sums[n] = _pair_sum(self.partial[n], got, self.rows[n], self.core, "pair_sum_" + n)
        if rs2:
            self.slots.update(zip(rs2, next(job_res)))
        for n in adamw:
            self.updates[n] = next(job_res)
        return out


SMALL_ROWS = 88


def _small_allreduce(vec):
    def body(v_ref, o_ref, gather, send_sems, recv_sems):
        x, y, c, _ = _place()
        my_id = 4 * x + 2 * y + c
        gather[my_id] = v_ref[...]
        copies = []
        for r in range(1, N_DEV):
            peer = (x ^ (r >> 2), y ^ ((r >> 1) & 1), c ^ (r & 1))
            cp = pltpu.make_async_remote_copy(src_ref=v_ref, dst_ref=gather.at[my_id], send_sem=send_sems.at[r - 1],
                                              recv_sem=recv_sems.at[r - 1], device_id=peer, device_id_type=MESH)
            cp.start()
            copies.append(cp)
        for cp in copies:
            cp.wait()
        acc = gather[0]
        for d in range(1, N_DEV):
            acc = acc + gather[d]
        o_ref[...] = acc

    vm = pl.BlockSpec(memory_space=pltpu.VMEM)
    return pl.pallas_call(
        body, name="small_allreduce", out_shape=jax.ShapeDtypeStruct(vec.shape, F32),
        in_specs=[vm], out_specs=vm,
        scratch_shapes=[pltpu.VMEM((N_DEV,) + vec.shape, F32), pltpu.SemaphoreType.DMA((N_DEV - 1,)),
                        pltpu.SemaphoreType.DMA((N_DEV - 1,))],
        compiler_params=_COMM_PARAMS,
    )(vec)


def _adamw_update(w, g, m, v):
    nm = ADAM_B1 * m + (1.0 - ADAM_B1) * g
    nv = ADAM_B2 * v + (1.0 - ADAM_B2) * jnp.square(g)
    m_hat = nm / (1.0 - ADAM_B1 ** ADAM_STEP)
    v_hat = nv / (1.0 - ADAM_B2 ** ADAM_STEP)
    return -ADAM_LR * (m_hat / (jnp.sqrt(v_hat) + ADAM_EPS) + ADAM_WD * w), nm, nv


def _adamw(w, g, m, v, name):
    R, C = w.shape
    tr = _tile(R, 256, 8)

    def body(w_ref, g_ref, m_ref, v_ref, d_ref, nm_ref, nv_ref):
        d_ref[...], nm_ref[...], nv_ref[...] = _adamw_update(w_ref[...], g_ref[...], m_ref[...], v_ref[...])

    spec = pl.BlockSpec((tr, C), lambda i: (i, 0))
    out = jax.ShapeDtypeStruct((R, C), F32)
    return pl.pallas_call(
        body, name=name, grid=(R // tr,), out_shape=(out, out, out),
        in_specs=[spec] * 4, out_specs=(spec, spec, spec),
        compiler_params=_params("parallel"),
    )(w, g, m, v)


def _adamw_slots(w, slots, m, v, name):
    R, C = w.shape
    tc = _tile(C, 512, LANES)
    spec = pl.BlockSpec((R, tc), lambda j: (0, j))
    out = jax.ShapeDtypeStruct((R, C), F32)
    return pl.pallas_call(
        functools.partial(_adamw_slots_body), name=name, grid=(C // tc,), out_shape=(out, out, out, out),
        in_specs=[spec, pl.BlockSpec((4, R, tc), lambda j: (0, 0, j)), spec, spec], out_specs=(spec, spec, spec, spec),
        compiler_params=_params("parallel"),
    )(w, slots, m, v)


def _adamw_slots_body(w_ref, s_ref, m_ref, v_ref, g_ref, d_ref, nm_ref, nv_ref):
    g = s_ref[0].astype(F32)
    for q in range(1, 4):
        g = g + s_ref[q].astype(F32)
    g_ref[...] = g
    d_ref[...], nm_ref[...], nv_ref[...] = _adamw_update(w_ref[...], g, m_ref[...], v_ref[...])


def _cast_job(arrays, dtype):
    C = arrays[0].shape[1]

    def specs(grid):
        total = 1
        for g in grid:
            total *= g
        tc = C // total
        assert tc * total == C and tc % LANES == 0, (C, grid)
        blocks = [pl.BlockSpec((a.shape[0], tc), lambda *ids: (0, _linear_step(grid, ids))) for a in arrays]
        return blocks, list(blocks)

    def each(ins, outs, scratch, step):
        for i_ref, o_ref in zip(ins, outs):
            o_ref[...] = i_ref[...].astype(o_ref.dtype)

    return _Job(ins=arrays, out_shape=[jax.ShapeDtypeStruct(a.shape, dtype) for a in arrays], specs=specs, each=each)


def _adamw_job(w, slots, m, v):
    R, C = w.shape

    n_slices = C // LANES

    def specs(grid):
        total = 1
        for g in grid:
            total *= g
        assert total >= n_slices, (grid, n_slices)
        col = lambda *ids: jnp.minimum(_linear_step(grid, ids), n_slices - 1)
        blk = pl.BlockSpec((R, LANES), lambda *ids: (0, col(*ids)))
        slot_blk = pl.BlockSpec((4, R, LANES), lambda *ids: (0, 0, col(*ids)))
        return [blk, slot_blk, blk, blk], [blk] * 4

    def each(ins, outs, scratch, step):
        @pl.when(step < n_slices)
        def _():
            _adamw_slots_body(*ins, *outs)

    out = jax.ShapeDtypeStruct((R, C), F32)
    return _Job(ins=[w, slots, m, v], out_shape=[out] * 4, specs=specs, each=each)


WEIGHT_NAMES = ("ffn1_norm", "ffn1_w_gate", "ffn1_w_up", "ffn1_w_down", "mix_norm", "w_in", "swa_sinks",
                "swa_out_norm", "sb_out_norm", "w_out", "ffn2_norm", "ffn2_w_gate", "ffn2_w_up", "ffn2_w_down",
                "final_norm")
SMALL_NAMES = ("ffn1_norm", "mix_norm", "swa_sinks", "swa_out_norm", "sb_out_norm", "ffn2_norm", "final_norm")
BIG_ARGS = {"ffn1_gate": ("ffn1_w_gate", True), "ffn1_up": ("ffn1_w_up", True), "ffn1_down": ("ffn1_w_down", False),
            "w_in": ("w_in", True), "w_out": ("w_out", False), "ffn2_gate": ("ffn2_w_gate", True),
            "ffn2_up": ("ffn2_w_up", True), "ffn2_down": ("ffn2_w_down", False)}


def _pack_small(parts):
    padded = [jnp.pad(p.reshape(1, -1), ((0, 0), (0, -p.size % LANES))) for p in parts]
    flat = jnp.concatenate(padded, axis=1)
    flat = jnp.pad(flat, ((0, 0), (0, SMALL_ROWS * LANES - flat.shape[1])))
    return flat.reshape(SMALL_ROWS, LANES)


def _unpack_small(block, shapes):
    flat = block.reshape(-1)
    out, off = [], 0
    for shp in shapes:
        n = 1
        for s in shp:
            n *= s
        out.append(flat[off:off + n].reshape(shp))
        off += n + (-n % LANES)
    return out


def kernel(x, ffn1_norm, ffn1_w_gate, ffn1_w_up, ffn1_w_down, mix_norm, w_in, swa_sinks, swa_out_norm, sb_out_norm, w_out, ffn2_norm, ffn2_w_gate, ffn2_w_up, ffn2_w_down, final_norm, loss_target, m_ffn1_norm, m_ffn1_w_gate, m_ffn1_w_up, m_ffn1_w_down, m_mix_norm, m_w_in, m_swa_sinks, m_swa_out_norm, m_sb_out_norm, m_w_out, m_ffn2_norm, m_ffn2_w_gate, m_ffn2_w_up, m_ffn2_w_down, m_final_norm, v_ffn1_norm, v_ffn1_w_gate, v_ffn1_w_up, v_ffn1_w_down, v_mix_norm, v_w_in, v_swa_sinks, v_swa_out_norm, v_sb_out_norm, v_w_out, v_ffn2_norm, v_ffn2_w_gate, v_ffn2_w_up, v_ffn2_w_down, v_final_norm):
    args = dict(locals())
    B, S, D = x.shape
    T = B * S
    weights = {n: args[n] for n in WEIGHT_NAMES}
    mom_m = {n: args["m_" + n] for n in WEIGHT_NAMES}
    mom_v = {n: args["v_" + n] for n in WEIGHT_NAMES}

    state = {}
    for name in BIG_NAMES:
        arg, transposed = BIG_ARGS[name]
        to_rows = (lambda t: t[0].T) if transposed else (lambda t: t[0])
        state[name] = tuple(to_rows(t[arg]) for t in (weights, mom_m, mom_v))
    comm = _Comm({FIRST_GATHERED: state[FIRST_GATHERED][0].astype(_WIRE_DTYPE)}, state)
    small = {n: weights[n].reshape(1, -1) for n in SMALL_NAMES}

    loss, gx, d_small = _layer_step(x.reshape(T, D), loss_target.reshape(T, D), B, S, small, comm)

    small_shapes = [(1, 1)] + [d_small[n].shape for n in SMALL_NAMES]
    reduced = _small_allreduce(_pack_small([loss[:, :1]] + [d_small[n] for n in SMALL_NAMES]))
    red = _unpack_small(reduced, small_shapes)
    loss_out = red[0].reshape(())
    g_small = dict(zip(SMALL_NAMES, red[1:]))

    grads, deltas, new_m, new_v = {}, {}, {}, {}
    for name in BIG_NAMES:
        arg, transposed = BIG_ARGS[name]
        back = (lambda t: t.T[None]) if transposed else (lambda t: t[None])
        res = comm.updates.get(name)
        if res is None:
            w2, m2, v2 = state[name]
            res = _adamw_slots(w2, comm.slots3(name), m2, v2, "adamw_" + name)
        grads[arg], deltas[arg], new_m[arg], new_v[arg] = [back(t) for t in res]
    shapes1 = [(1, weights[n].size) for n in SMALL_NAMES]
    packed = [_pack_small([t[n].reshape(1, -1) for n in SMALL_NAMES]) for t in (weights, g_small, mom_m, mom_v)]
    upd = _adamw(*packed, "adamw_small")
    for tgt_dict, block in zip((deltas, new_m, new_v), upd):
        for n, val in zip(SMALL_NAMES, _unpack_small(block, shapes1)):
            tgt_dict[n] = val.reshape(weights[n].shape)
    for n in SMALL_NAMES:
        grads[n] = g_small[n].reshape(weights[n].shape)

    return (loss_out, gx.reshape(B, S, D), *[grads[n] for n in WEIGHT_NAMES], *[deltas[n] for n in WEIGHT_NAMES],
            *[new_m[n] for n in WEIGHT_NAMES], *[new_v[n] for n in WEIGHT_NAMES])
```

```python
import functools

import jax
import jax.numpy as jnp
from jax import lax
from jax.experimental import pallas as pl
from jax.experimental.pallas import tpu as pltpu

F32 = jnp.float32
_MXU_DTYPE = jnp.bfloat16
_WIRE_DTYPE = jnp.bfloat16

EPS = 1e-6
HEAD_DIM = 64
N_SWA_HEADS = 16
N_SWA_KV = 4
N_SB_HEADS = 16
WINDOW = 128
SWA_Q = N_SWA_HEADS * HEAD_DIM
SWA_KV = N_SWA_KV * HEAD_DIM
SB_W = N_SB_HEADS * HEAD_DIM
IN_W = SWA_Q + 2 * SWA_KV + 3 * SB_W
LANES = 128
ATT_SCALE = HEAD_DIM ** -0.5

ADAM_LR = 0.001
ADAM_B1 = 0.9
ADAM_B2 = 0.999
ADAM_EPS = 1e-08
ADAM_WD = 0.01
ADAM_STEP = 10

N_DEV = 8
_VMEM_LIMIT_BYTES = 56 * 1024 * 1024
_F_TILE = 512


def _params(*semantics):
    return pltpu.CompilerParams(dimension_semantics=semantics, vmem_limit_bytes=_VMEM_LIMIT_BYTES)


def _tile(n, pref, align):
    t = min(n, pref)
    t -= t % align
    while t >= align:
        if n % t == 0:
            return t
        t -= align
    return n


def _dot(a, b):
    return lax.dot_general(a, b, (((1,), (0,)), ((), ())), preferred_element_type=F32)


def _dot_nt(a, b):
    return lax.dot_general(a, b, (((1,), (1,)), ((), ())), preferred_element_type=F32)


def _dot_tn(a, b):
    return lax.dot_general(a, b, (((0,), (0,)), ((), ())), preferred_element_type=F32)


class _Job:
    def __init__(self, ins, out_shape, scratch=(), start=None, finish=None, mid=None, each=None, specs=None):
        self.ins, self.out_shape, self.scratch = list(ins), list(out_shape), list(scratch)
        self.start, self.mid, self.finish, self.each, self.specs = start, mid, finish, each, specs


_JOB_MID_FRACTION = 0.6


def _linear_step(grid, ids):
    step = ids[0]
    for d in range(1, len(grid)):
        step = step * grid[d] + ids[d]
    return step


def _call(body, *, name, grid, in_specs, out_specs, out_shape, args, semantics, scratch_shapes=(), jobs=()):
    single = not isinstance(out_shape, (tuple, list))
    if not jobs:
        res = pl.pallas_call(body, name=name, grid=grid, in_specs=list(in_specs), out_specs=out_specs,
                             out_shape=out_shape, scratch_shapes=list(scratch_shapes),
                             compiler_params=_params(*semantics))(*args)
        return res, []
    base_out = [out_shape] if single else list(out_shape)
    base_out_specs = [out_specs] if single else list(out_specs)
    n_in, n_out, n_scr = len(args), len(base_out), len(scratch_shapes)
    any_spec = pl.BlockSpec(memory_space=pl.ANY)
    total = 1
    for g in grid:
        total *= g
    mid_step = min(total - 1, int(total * _JOB_MID_FRACTION))

    def wrapped(*refs):
        pos = n_in
        job_ins = []
        for job in jobs:
            job_ins.append(refs[pos:pos + len(job.ins)])
            pos += len(job.ins)
        outs = refs[pos:pos + n_out]
        pos += n_out
        job_outs = []
        for job in jobs:
            job_outs.append(refs[pos:pos + len(job.out_shape)])
            pos += len(job.out_shape)
        scr = refs[pos:pos + n_scr]
        pos += n_scr
        job_scr = []
        for job in jobs:
            job_scr.append(refs[pos:pos + len(job.scratch)])
            pos += len(job.scratch)
        bound = list(zip(jobs, job_ins, job_outs, job_scr))
        step = _linear_step(grid, [pl.program_id(d) for d in range(len(grid))])

        @pl.when(step == 0)
        def _():
            for job, ji, jo, js in bound:
                if job.start is not None:
                    job.start(ji, jo, js)

        @pl.when(step == mid_step)
        def _():
            for job, ji, jo, js in bound:
                if job.mid is not None:
                    job.mid(ji, jo, js, 0)

        body(*refs[:n_in], *outs, *scr)
        for job, ji, jo, js in bound:
            if job.each is not None:
                job.each(ji, jo, js, step)

        @pl.when(step == total - 1)
        def _():
            for job, ji, jo, js in bound:
                if job.mid is not None:
                    job.mid(ji, jo, js, 1)
            for job, ji, jo, js in bound:
                if job.finish is not None:
                    job.finish(ji, jo, js)

    all_args, all_in_specs = list(args), list(in_specs)
    all_out_shape, all_out_specs = list(base_out), list(base_out_specs)
    for job in jobs:
        job_in_specs, job_out_specs = (job.specs(grid) if job.specs is not None else
                                       ([any_spec] * len(job.ins), [any_spec] * len(job.out_shape)))
        all_args += job.ins
        all_in_specs += job_in_specs
        all_out_shape += job.out_shape
        all_out_specs += job_out_specs
    all_scratch = list(scratch_shapes) + [s for job in jobs for s in job.scratch]
    res = pl.pallas_call(
        wrapped, name=name, grid=grid, in_specs=all_in_specs, out_specs=tuple(all_out_specs),
        out_shape=tuple(all_out_shape), scratch_shapes=all_scratch,
        compiler_params=pltpu.CompilerParams(dimension_semantics=("arbitrary",) * len(grid),
                                             vmem_limit_bytes=_VMEM_LIMIT_BYTES,
                                             has_side_effects=any(job.start is not None for job in jobs)),
    )(*all_args)
    base = res[0] if single else tuple(res[:n_out])
    job_res, pos = [], n_out
    for job in jobs:
        job_res.append(tuple(res[pos:pos + len(job.out_shape)]))
        pos += len(job.out_shape)
    return base, job_res


def _rms_fwd(x, g, name, jobs=()):
    T, D = x.shape
    tm = _tile(T, 512, 16)

    def body(x_ref, g_ref, o_ref):
        xv = x_ref[...]
        r = lax.rsqrt(jnp.mean(xv * xv, axis=-1, keepdims=True) + EPS)
        o_ref[...] = (xv * r * g_ref[...]).astype(o_ref.dtype)

    return _call(
        body, name=name, grid=(T // tm,),
        out_shape=jax.ShapeDtypeStruct((T, D), _MXU_DTYPE),
        in_specs=[pl.BlockSpec((tm, D), lambda i: (i, 0)), pl.BlockSpec((1, D), lambda i: (0, 0))],
        out_specs=pl.BlockSpec((tm, D), lambda i: (i, 0)), args=(x, g), semantics=("parallel",), jobs=jobs)


def _rms_bwd_rows(dh, xv, g):
    r = lax.rsqrt(jnp.mean(xv * xv, axis=-1, keepdims=True) + EPS)
    xhat = xv * r
    u = dh * g
    dx = r * (u - xhat * jnp.mean(u * xhat, axis=-1, keepdims=True))
    return dx, dh * xhat


def _rms_bwd(dh, x, g, dres, name):
    T, D = x.shape
    tm = _tile(T, 256, 16)

    def body(dh_ref, x_ref, g_ref, dres_ref, dx_ref, dxb_ref, dg_ref):
        @pl.when(pl.program_id(0) == 0)
        def _():
            dg_ref[...] = jnp.zeros_like(dg_ref)

        dx, dgr = _rms_bwd_rows(dh_ref[...], x_ref[...], g_ref[...])
        dx = dres_ref[...] + dx
        dx_ref[...] = dx
        dxb_ref[...] = dx.astype(dxb_ref.dtype)
        dg_ref[...] += jnp.sum(dgr, axis=0, keepdims=True)

    row = pl.BlockSpec((tm, D), lambda i: (i, 0))
    vec = pl.BlockSpec((1, D), lambda i: (0, 0))
    return pl.pallas_call(
        body, name=name, grid=(T // tm,),
        out_shape=(jax.ShapeDtypeStruct((T, D), F32), jax.ShapeDtypeStruct((T, D), _MXU_DTYPE),
                   jax.ShapeDtypeStruct((1, D), F32)),
        in_specs=[row, row, vec, row], out_specs=(row, row, vec),
        compiler_params=_params("arbitrary"),
    )(dh, x, g, dres)


def _loss_head(x, g, tgt, name):
    T, D = x.shape
    tm = _tile(T, 256, 16)

    def body(x_ref, g_ref, t_ref, dx_ref, dxb_ref, dg_ref, loss_ref):
        @pl.when(pl.program_id(0) == 0)
        def _():
            dg_ref[...] = jnp.zeros_like(dg_ref)
            loss_ref[...] = jnp.zeros_like(loss_ref)

        xv = x_ref[...]
        gv = g_ref[...]
        r = lax.rsqrt(jnp.mean(xv * xv, axis=-1, keepdims=True) + EPS)
        xhat = xv * r
        diff = xhat * gv - t_ref[...]
        tok = jnp.mean(diff * diff, axis=-1, keepdims=True)
        loss_ref[...] += 0.5 * jnp.sum(tok, axis=0, keepdims=True)
        dy = diff / D
        u = dy * gv
        dx = r * (u - xhat * jnp.mean(u * xhat, axis=-1, keepdims=True))
        dx_ref[...] = dx
        dxb_ref[...] = dx.astype(dxb_ref.dtype)
        dg_ref[...] += jnp.sum(dy * xhat, axis=0, keepdims=True)

    row = pl.BlockSpec((tm, D), lambda i: (i, 0))
    vec = pl.BlockSpec((1, D), lambda i: (0, 0))
    return pl.pallas_call(
        body, name=name, grid=(T // tm,),
        out_shape=(jax.ShapeDtypeStruct((T, D), F32), jax.ShapeDtypeStruct((T, D), _MXU_DTYPE),
                   jax.ShapeDtypeStruct((1, D), F32), jax.ShapeDtypeStruct((1, LANES), F32)),
        in_specs=[row, vec, row],
        out_specs=(row, row, vec, pl.BlockSpec((1, LANES), lambda i: (0, 0))),
        compiler_params=_params("arbitrary"),
    )(x, g, tgt)


def _outnorm_fwd(ya, yb, ga, gb, name):
    T, W = ya.shape
    tm = _tile(T, 512, 16)

    def body(ya_ref, yb_ref, ga_ref, gb_ref, o_ref):
        for k, (y_ref, g_ref) in enumerate(((ya_ref, ga_ref), (yb_ref, gb_ref))):
            yv = y_ref[...]
            r = lax.rsqrt(jnp.mean(yv * yv, axis=-1, keepdims=True) + EPS)
            o_ref[:, k * W:(k + 1) * W] = (yv * r * g_ref[...]).astype(o_ref.dtype)

    row = pl.BlockSpec((tm, W), lambda i: (i, 0))
    vec = pl.BlockSpec((1, W), lambda i: (0, 0))
    return pl.pallas_call(
        body, name=name, grid=(T // tm,),
        out_shape=jax.ShapeDtypeStruct((T, 2 * W), _MXU_DTYPE),
        in_specs=[row, row, vec, vec], out_specs=pl.BlockSpec((tm, 2 * W), lambda i: (i, 0)),
        compiler_params=_params("parallel"),
    )(ya, yb, ga, gb)


def _outnorm_bwd(dyn, ya, yb, ga, gb, name):
    T, W = ya.shape
    tm = _tile(T, 256, 16)

    def body(d_ref, ya_ref, yb_ref, ga_ref, gb_ref, dya_ref, dyb_ref, dga_ref, dgb_ref):
        @pl.when(pl.program_id(0) == 0)
        def _():
            dga_ref[...] = jnp.zeros_like(dga_ref)
            dgb_ref[...] = jnp.zeros_like(dgb_ref)

        for k, (y_ref, g_ref, dy_ref, dg_ref) in enumerate(
                ((ya_ref, ga_ref, dya_ref, dga_ref), (yb_ref, gb_ref, dyb_ref, dgb_ref))):
            dy, dgr = _rms_bwd_rows(d_ref[:, k * W:(k + 1) * W], y_ref[...], g_ref[...])
            dy_ref[...] = dy.astype(dy_ref.dtype)
            dg_ref[...] += jnp.sum(dgr, axis=0, keepdims=True)

    row = pl.BlockSpec((tm, W), lambda i: (i, 0))
    vec = pl.BlockSpec((1, W), lambda i: (0, 0))
    return pl.pallas_call(
        body, name=name, grid=(T // tm,),
        out_shape=(jax.ShapeDtypeStruct((T, W), _MXU_DTYPE), jax.ShapeDtypeStruct((T, W), _MXU_DTYPE),
                   jax.ShapeDtypeStruct((1, W), F32), jax.ShapeDtypeStruct((1, W), F32)),
        in_specs=[pl.BlockSpec((tm, 2 * W), lambda i: (i, 0)), row, row, vec, vec],
        out_specs=(row, row, vec, vec),
        compiler_params=_params("arbitrary"),
    )(dyn, ya, yb, ga, gb)


_STRIP_ROWS = 256


def _strips(rows):
    step = min(rows, _STRIP_ROWS)
    return [slice(r, r + step) for r in range(0, rows, step)]


def _ffn_gu(h, wg_t, wu_t, name, jobs=()):
    T, D = h.shape
    Fp = wg_t.shape[0]
    tm = _tile(T, 1024, 16)
    tn = _tile(Fp, _F_TILE, LANES)

    def body(h_ref, wg_ref, wu_ref, g_ref, u_ref, a_ref):
        for rows in _strips(tm):
            hv = h_ref[rows, :]
            g = _dot_nt(hv, wg_ref[...])
            u = _dot_nt(hv, wu_ref[...])
            g_ref[rows, :] = g.astype(g_ref.dtype)
            u_ref[rows, :] = u.astype(u_ref.dtype)
            a_ref[rows, :] = (g * jax.nn.sigmoid(g) * u).astype(a_ref.dtype)

    act = pl.BlockSpec((tm, tn), lambda n, m: (m, n))
    wsp = pl.BlockSpec((tn, D), lambda n, m: (n, 0))
    out = jax.ShapeDtypeStruct((T, Fp), _MXU_DTYPE)
    return _call(
        body, name=name, grid=(Fp // tn, T // tm), out_shape=(out, out, out),
        in_specs=[pl.BlockSpec((tm, D), lambda n, m: (m, 0)), wsp, wsp],
        out_specs=(act, act, act), args=(h, wg_t, wu_t), semantics=("parallel", "parallel"), jobs=jobs)


def _ffn_bwd_act(dxb, wd, G, U, name, jobs=()):
    T, D = dxb.shape
    Fp = wd.shape[0]
    tm = _tile(T, 1024, 16)
    tn = _tile(Fp, _F_TILE, LANES)

    def body(e_ref, wd_ref, g_ref, u_ref, dg_ref, du_ref):
        for rows in _strips(tm):
            da = 0.5 * _dot_nt(e_ref[rows, :], wd_ref[...])
            g = g_ref[rows, :].astype(F32)
            u = u_ref[rows, :].astype(F32)
            s = jax.nn.sigmoid(g)
            du_ref[rows, :] = (da * (g * s)).astype(du_ref.dtype)
            dg_ref[rows, :] = (da * u * (s * (1.0 + g * (1.0 - s)))).astype(dg_ref.dtype)

    act = pl.BlockSpec((tm, tn), lambda m, n: (m, n))
    out = jax.ShapeDtypeStruct((T, Fp), _MXU_DTYPE)
    return _call(
        body, name=name, grid=(T // tm, Fp // tn), out_shape=(out, out),
        in_specs=[pl.BlockSpec((tm, D), lambda m, n: (m, 0)), pl.BlockSpec((tn, D), lambda m, n: (n, 0)),
                  act, act],
        out_specs=(act, act), args=(dxb, wd, G, U), semantics=("parallel", "parallel"), jobs=jobs)


def _ffn_gate(h, wg_t, U, name, jobs=()):
    T, D = h.shape
    Fp = wg_t.shape[0]
    tm = _tile(T, 1024, 16)
    tn = _tile(Fp, _F_TILE, LANES)

    def body(h_ref, wg_ref, u_ref, g_ref, a_ref):
        for rows in _strips(tm):
            g = _dot_nt(h_ref[rows, :], wg_ref[...])
            g_ref[rows, :] = g.astype(g_ref.dtype)
            a_ref[rows, :] = (g * jax.nn.sigmoid(g) * u_ref[rows, :].astype(F32)).astype(a_ref.dtype)

    act = pl.BlockSpec((tm, tn), lambda m, n: (m, n))
    out = jax.ShapeDtypeStruct((T, Fp), _MXU_DTYPE)
    return _call(
        body, name=name, grid=(T // tm, Fp // tn), out_shape=(out, out),
        in_specs=[pl.BlockSpec((tm, D), lambda m, n: (m, 0)), pl.BlockSpec((tn, D), lambda m, n: (n, 0)), act],
        out_specs=(act, act), args=(h, wg_t, U), semantics=("parallel", "parallel"), jobs=jobs)


def _mm_nt(a, b, out_dtype, name, jobs=()):
    M, K = a.shape
    N = b.shape[0]
    tm = _tile(M, 1024, 16)
    tn = _tile(N, 512, LANES)

    def body(a_ref, b_ref, o_ref):
        o_ref[...] = _dot_nt(a_ref[...], b_ref[...]).astype(o_ref.dtype)

    return _call(
        body, name=name, grid=(M // tm, N // tn), out_shape=jax.ShapeDtypeStruct((M, N), out_dtype),
        in_specs=[pl.BlockSpec((tm, K), lambda m, n: (m, 0)), pl.BlockSpec((tn, K), lambda m, n: (n, 0))],
        out_specs=pl.BlockSpec((tm, tn), lambda m, n: (m, n)), args=(a, b),
        semantics=("parallel", "parallel"), jobs=jobs)


_MM_OPERAND_BYTES = 26 * 1024 * 1024


def _k_tile(K, bytes_per_k, align):
    best = align
    for t in range(align, K + 1, align):
        if K % t == 0 and 2 * t * bytes_per_k <= _MM_OPERAND_BYTES:
            best = t
    return best


def _mm_nn(pairs, res, alpha, out_dtype, name, jobs=()):
    M, K = pairs[0][0].shape
    N = pairs[0][1].shape[1]
    n_pairs = len(pairs)
    tm = _tile(M, 1024, 16)
    tn = _tile(N, 1024, LANES)
    tk = _k_tile(K, n_pairs * (tm + tn) * pairs[0][0].dtype.itemsize, LANES)
    nk = K // tk

    def body(*refs):
        ab = refs[:2 * n_pairs]
        res_ref = refs[2 * n_pairs] if res is not None else None
        o_ref = refs[2 * n_pairs + (res is not None)]

        def finish(acc):
            out = alpha * acc
            if res_ref is not None:
                out = res_ref[...] + out
            o_ref[...] = out.astype(o_ref.dtype)

        part = _dot(ab[0][...], ab[1][...])
        for i in range(1, n_pairs):
            part = part + _dot(ab[2 * i][...], ab[2 * i + 1][...])
        if nk == 1:
            finish(part)
        else:
            acc_ref = refs[-1]
            k = pl.program_id(2)

            @pl.when(k == 0)
            def _():
                acc_ref[...] = part

            @pl.when(k > 0)
            def _():
                acc_ref[...] += part

            @pl.when(k == nk - 1)
            def _():
                finish(acc_ref[...])

    in_specs, args = [], []
    for a, b in pairs:
        in_specs += [pl.BlockSpec((tm, tk), lambda m, n, k: (m, k)), pl.BlockSpec((tk, tn), lambda m, n, k: (k, n))]
        args += [a, b]
    if res is not None:
        in_specs.append(pl.BlockSpec((tm, tn), lambda m, n, k: (m, n)))
        args.append(res)
    return _call(
        body, name=name, grid=(M // tm, N // tn, nk), out_shape=jax.ShapeDtypeStruct((M, N), out_dtype),
        in_specs=in_specs, out_specs=pl.BlockSpec((tm, tn), lambda m, n, k: (m, n)),
        scratch_shapes=[pltpu.VMEM((tm, tn), F32)] if nk > 1 else [], args=args,
        semantics=("parallel", "parallel", "arbitrary"), jobs=jobs)


def _mm_tn(a, b, alpha, out_dtype, name, jobs=()):
    K, M = a.shape
    N = b.shape[1]
    tm = _tile(M, 512, LANES)
    tn = _tile(N, 1024, LANES)

    def body(a_ref, b_ref, o_ref):
        o_ref[...] = (alpha * _dot_tn(a_ref[...], b_ref[...])).astype(o_ref.dtype)

    return _call(
        body, name=name, grid=(N // tn, M // tm), out_shape=jax.ShapeDtypeStruct((M, N), out_dtype),
        in_specs=[pl.BlockSpec((K, tm), lambda n, m: (0, m)), pl.BlockSpec((K, tn), lambda n, m: (0, n))],
        out_specs=pl.BlockSpec((tm, tn), lambda n, m: (m, n)), args=(a, b),
        semantics=("parallel", "parallel"), jobs=jobs)


def _half_masks():
    lane = lax.broadcasted_iota(jnp.int32, (1, LANES), 1)
    return (lane < HEAD_DIM, lane >= HEAD_DIM)


def _swap_halves(v):
    return pltpu.roll(v.astype(F32), HEAD_DIM, 1).astype(v.dtype)


def _swa_geometry(n):
    qi = lax.broadcasted_iota(jnp.int32, (WINDOW, 2 * WINDOW), 0)
    kp = lax.broadcasted_iota(jnp.int32, (WINDOW, 2 * WINDOW), 1)
    dist = (WINDOW + qi) - kp
    valid = (dist >= 0) & (dist < WINDOW) & ((n > 0) | (kp >= WINDOW))
    return dist.astype(F32), valid


def _swa_slope(h):
    return 2.0 ** (-8.0 * (h + 1) / N_SWA_HEADS)


def _swa_softmax(qk, sink, slope, distf, valid):
    s = qk * ATT_SCALE - slope * distf
    s = jnp.where(valid, s, -1e30)
    m = jnp.maximum(jnp.max(s, axis=1, keepdims=True), sink)
    p = jnp.exp(s - m)
    e_sink = jnp.exp(sink - m)
    den = jnp.sum(p, axis=1, keepdims=True) + e_sink
    return p / den, e_sink / den


def _swa_group_heads(g):
    return [(2 * pp + a, pp, a) for pp in (2 * g, 2 * g + 1) for a in range(2)]


def _swa_specs(B, S):
    nb = S // WINDOW
    kcol = SWA_Q // SWA_KV
    cur = lambda b, n: (b * nb + n, kcol)
    prev = lambda b, n: (b * nb + jnp.maximum(n - 1, 0), kcol)
    curv = lambda b, n: (b * nb + n, kcol + 1)
    prevv = lambda b, n: (b * nb + jnp.maximum(n - 1, 0), kcol + 1)
    q_spec = pl.BlockSpec((WINDOW, SWA_Q), lambda b, n: (b * nb + n, 0))
    kv = [pl.BlockSpec((WINDOW, SWA_KV), f) for f in (prev, cur, prevv, curv)]
    sink_spec = pl.BlockSpec(memory_space=pltpu.SMEM)
    return nb, q_spec, kv, sink_spec


def _swa_kv_views(kp_ref, kc_ref, vp_ref, vc_ref, g):
    hm = _half_masks()
    c0 = (g // 2) * LANES
    k_all = jnp.concatenate([kp_ref[:, c0:c0 + LANES], kc_ref[:, c0:c0 + LANES]], axis=0)
    v_all = jnp.concatenate([vp_ref[:, c0:c0 + LANES], vc_ref[:, c0:c0 + LANES]], axis=0)
    b = g % 2
    ks, vs = [None, None], [None, None]
    ks[b], vs[b] = k_all, v_all
    ks[1 - b], vs[1 - b] = _swap_halves(k_all), _swap_halves(v_all)
    ks = [jnp.where(hm[a], ks[a], 0) for a in range(2)]
    vs = [jnp.where(hm[a], vs[a], 0) for a in range(2)]
    return ks, vs


def _swa_fwd(proj, sinks, B, S, name, jobs=()):
    T = B * S
    nb, q_spec, kv_specs, sink_spec = _swa_specs(B, S)

    def body(sink_ref, q_ref, kp_ref, kc_ref, vp_ref, vc_ref, y_ref):
        hm = _half_masks()
        distf, valid = _swa_geometry(pl.program_id(1))
        for g in range(N_SWA_KV):
            ks, vs = _swa_kv_views(kp_ref, kc_ref, vp_ref, vc_ref, g)
            heads = _swa_group_heads(g)
            qk = [_dot_nt(jnp.where(hm[a], q_ref[:, pp * LANES:(pp + 1) * LANES], 0), ks[a]) for _, pp, a in heads]
            p = [_swa_softmax(qk[i], sink_ref[0, h], _swa_slope(h), distf, valid)[0] for i, (h, _, _) in enumerate(heads)]
            o = [_dot(p[i].astype(_MXU_DTYPE), vs[a]) for i, (_, _, a) in enumerate(heads)]
            for j, pp in enumerate((2 * g, 2 * g + 1)):
                y_ref[:, pp * LANES:(pp + 1) * LANES] = o[2 * j] + o[2 * j + 1]

    return _call(
        body, name=name, grid=(B, nb), out_shape=jax.ShapeDtypeStruct((T, SWA_Q), F32),
        in_specs=[sink_spec, q_spec] + kv_specs,
        out_specs=pl.BlockSpec((WINDOW, SWA_Q), lambda b, n: (b * nb + n, 0)),
        args=(sinks, proj, proj, proj, proj, proj), semantics=("parallel", "parallel"), jobs=jobs)


def _swa_bwd(proj, sinks, dya, B, S, name, jobs=()):
    T = B * S
    nb, q_spec, kv_specs, sink_spec = _swa_specs(B, S)

    def body(sink_ref, q_ref, kp_ref, kc_ref, vp_ref, vc_ref, do_ref,
             dq_ref, dk_ref, dv_ref, dsink_ref, dk_acc, dv_acc):
        b_id, n = pl.program_id(0), pl.program_id(1)
        hm = _half_masks()
        lane = lax.broadcasted_iota(jnp.int32, (1, LANES), 1)

        @pl.when((b_id == 0) & (n == 0))
        def _():
            dsink_ref[...] = jnp.zeros_like(dsink_ref)

        @pl.when(n == 0)
        def _():
            dk_acc[...] = jnp.zeros_like(dk_acc)
            dv_acc[...] = jnp.zeros_like(dv_acc)

        distf, valid = _swa_geometry(n)
        r_prev = pl.multiple_of(jnp.maximum(n - 1, 0) * WINDOW, WINDOW)
        r_cur = pl.multiple_of(n * WINDOW, WINDOW)
        dsink = jnp.zeros((1, LANES), F32)
        for g in range(N_SWA_KV):
            ks, vs = _swa_kv_views(kp_ref, kc_ref, vp_ref, vc_ref, g)
            heads = _swa_group_heads(g)
            four = range(len(heads))
            qms = [jnp.where(hm[a], q_ref[:, pp * LANES:(pp + 1) * LANES], 0) for _, pp, a in heads]
            doms = [jnp.where(hm[a], do_ref[:, pp * LANES:(pp + 1) * LANES], 0) for _, pp, a in heads]
            qk = [_dot_nt(qms[i], ks[heads[i][2]]) for i in four]
            dp = [_dot_nt(doms[i], vs[heads[i][2]]) for i in four]
            soft = [_swa_softmax(qk[i], sink_ref[0, heads[i][0]], _swa_slope(heads[i][0]), distf, valid) for i in four]
            p = [soft[i][0] for i in four]
            delta = [jnp.sum(p[i] * dp[i], axis=1, keepdims=True) for i in four]
            ds = [(p[i] * (dp[i] - delta[i]) * ATT_SCALE).astype(_MXU_DTYPE) for i in four]
            for i in four:
                dsink = dsink + jnp.where(lane == heads[i][0], -jnp.sum(soft[i][1] * delta[i]), 0.0)
            dq = [_dot(ds[i], ks[heads[i][2]]) for i in four]
            dk_h = [_dot_tn(ds[i], qms[i]) for i in four]
            dv_h = [_dot_tn(p[i].astype(_MXU_DTYPE), doms[i]) for i in four]
            for j, pp in enumerate((2 * g, 2 * g + 1)):
                dq_ref[:, pp * LANES:(pp + 1) * LANES] = (dq[2 * j] + dq[2 * j + 1]).astype(dq_ref.dtype)
            dk_g = [dk_h[a] + dk_h[2 + a] for a in range(2)]
            dv_g = [dv_h[a] + dv_h[2 + a] for a in range(2)]
            bsel = g % 2
            dk_t = dk_g[bsel] + pltpu.roll(dk_g[1 - bsel], HEAD_DIM, 1)
            dv_t = dv_g[bsel] + pltpu.roll(dv_g[1 - bsel], HEAD_DIM, 1)
            c0 = (g // 2) * LANES
            dk_acc[pl.ds(r_prev, WINDOW), c0:c0 + LANES] += dk_t[:WINDOW]
            dk_acc[pl.ds(r_cur, WINDOW), c0:c0 + LANES] += dk_t[WINDOW:]
            dv_acc[pl.ds(r_prev, WINDOW), c0:c0 + LANES] += dv_t[:WINDOW]
            dv_acc[pl.ds(r_cur, WINDOW), c0:c0 + LANES] += dv_t[WINDOW:]
        dsink_ref[...] += dsink

        @pl.when(n == nb - 1)
        def _():
            dk_ref[...] = dk_acc[...].astype(dk_ref.dtype)
            dv_ref[...] = dv_acc[...].astype(dv_ref.dtype)

    seq_kv = pl.BlockSpec((S, SWA_KV), lambda b, n: (b, 0))
    return _call(
        body, name=name, grid=(B, nb),
        out_shape=(jax.ShapeDtypeStruct((T, SWA_Q), _MXU_DTYPE), jax.ShapeDtypeStruct((T, SWA_KV), _MXU_DTYPE),
                   jax.ShapeDtypeStruct((T, SWA_KV), _MXU_DTYPE), jax.ShapeDtypeStruct((1, LANES), F32)),
        in_specs=[sink_spec, q_spec] + kv_specs + [pl.BlockSpec((WINDOW, SWA_Q), lambda b, n: (b * nb + n, 0))],
        out_specs=(pl.BlockSpec((WINDOW, SWA_Q), lambda b, n: (b * nb + n, 0)), seq_kv, seq_kv,
                   pl.BlockSpec((1, LANES), lambda b, n: (0, 0))),
        scratch_shapes=[pltpu.VMEM((S, SWA_KV), F32), pltpu.VMEM((S, SWA_KV), F32)],
        args=(sinks, proj, proj, proj, proj, proj, dya), semantics=("arbitrary", "arbitrary"), jobs=jobs)


SB_TILE = 256
SB_HALF = 128
SB_DEAD = -105.0


def _mark_lanes():
    lane = lax.broadcasted_iota(jnp.int32, (1, LANES), 1)
    return (lane == HEAD_DIM - 1) | (lane == LANES - 1)


def _tri2(cond):
    j = lax.broadcasted_iota(jnp.int32, (2 * SB_HALF, SB_HALF), 0) & (SB_HALF - 1)
    s = lax.broadcasted_iota(jnp.int32, (2 * SB_HALF, SB_HALF), 1)
    return cond(j, s).astype(_MXU_DTYPE)


def _half_cumsums(x, tri2):
    out = []
    for h in range(2):
        xh = x[:, h * SB_HALF:(h + 1) * SB_HALF]
        hi = xh.astype(_MXU_DTYPE)
        lo = (xh - hi.astype(F32)).astype(_MXU_DTYPE)
        out.append(_dot(jnp.concatenate([hi, lo], axis=1), tri2))
    return out


def _log_sigmoid(z):
    return jnp.minimum(z, 0.0) - jnp.log(1.0 + jnp.exp(-jnp.abs(z)))


def _sb_specs(B, S):
    qb = (SWA_Q + 2 * SWA_KV) // LANES
    kb = qb + SB_W // LANES
    vb = kb + SB_W // LANES
    return [pl.BlockSpec((S, LANES), functools.partial(lambda b, p, c: (b, c + p), c=c)) for c in (qb, kb, vb)]


def _sb_fwd(proj, B, S, name, jobs=()):
    T = B * S
    tq = SB_TILE
    nq = S // tq

    def body(q_ref, k_ref, v_ref, y_ref, tot_ref):
        hm = _half_masks()
        ji = lax.broadcasted_iota(jnp.int32, (tq, tq), 0)
        si = lax.broadcasted_iota(jnp.int32, (tq, tq), 1)
        tri_after = _tri2(lambda j, s: j > s)
        causal = si < ji
        mark = _mark_lanes()

        def q_block(qi, with_previous):
            r0 = qi * tq if isinstance(qi, int) else pl.multiple_of(qi * tq, tq)
            q_pair = q_ref[pl.ds(r0, tq), :] * ATT_SCALE
            qms = [jnp.where(hm[a], q_pair, 0) for a in range(2)]

            def tiles(blocks, state):
                two, nb = range(2), range(len(blocks))
                kk = [k_ref[pl.ds(c0, tq), :] for c0, _ in blocks]
                vv = [v_ref[pl.ds(c0, tq), :] for c0, _ in blocks]
                z = [[_dot_nt(qms[a], kk[b]) for a in two] for b in nb]
                lb = [[_log_sigmoid(z[b][a]) for a in two] for b in nb]
                l1m = [[jnp.where(causal, lb[b][a] - z[b][a], 0.0) if blocks[b][1] else lb[b][a] - z[b][a]
                        for a in two] for b in nb]
                cum = [[_half_cumsums(l1m[b][a], tri_after) for a in two] for b in nb]
                tot = [[[cum[b][a][h][:, 0:1] + l1m[b][a][:, h * SB_HALF:h * SB_HALF + 1] for h in two]
                        for a in two] for b in nb]
                car = [[state[a][1] for a in two]]
                for b in nb:
                    car.append([car[b][a] + (tot[b][a][0] + tot[b][a][1]) for a in two])
                after = [[jnp.concatenate([cum[b][a][0] + (car[b][a] + tot[b][a][1]), cum[b][a][1] + car[b][a]], axis=1)
                          for a in two] for b in nb]
                att = [[jnp.exp(lb[b][a] + after[b][a]) for a in two] for b in nb]
                att = [[jnp.where(causal, att[b][a], 0.0) if blocks[b][1] else att[b][a] for a in two] for b in nb]
                acc = [state[a][0] for a in two]
                for b in nb:
                    acc = [acc[a] + _dot(att[b][a].astype(_MXU_DTYPE), jnp.where(hm[a], vv[b], 0)) for a in two]
                return tuple((acc[a], car[-1][a]) for a in two)

            def live(st):
                return jnp.maximum(jnp.max(st[0][1]), jnp.max(st[1][1])) > SB_DEAD

            def step(c):
                it, _, st = c
                st = tiles([(pl.multiple_of((qi - 1 - it) * tq, tq), False)], st)
                return it + 1, live(st), st

            zero = (jnp.zeros((tq, LANES), F32), jnp.zeros((tq, 1), F32))
            if with_previous:
                state = tiles([(r0, True), (pl.multiple_of(r0 - tq, tq), False)], (zero, zero))
                done, _, state = lax.while_loop(lambda c: (c[0] < qi) & c[1], step, (jnp.int32(1), live(state), state))
            else:
                state, done = tiles([(r0, True)], (zero, zero)), 0
            y_ref[pl.ds(r0, tq), :] = state[0][0] + state[1][0]
            first = jnp.asarray(qi - done, F32)
            tot_ref[pl.ds(r0, tq), :] = jnp.where(mark, first, jnp.where(hm[0], state[0][1], state[1][1]))

        q_block(0, False)

        def q_loop(qi, carry):
            q_block(qi, True)
            return carry

        lax.fori_loop(1, nq, q_loop, 0)

    out_spec = pl.BlockSpec((S, LANES), lambda b, p: (b, p))
    return _call(
        body, name=name, grid=(B, SB_W // LANES),
        out_shape=(jax.ShapeDtypeStruct((T, SB_W), F32), jax.ShapeDtypeStruct((T, SB_W), F32)),
        in_specs=_sb_specs(B, S), out_specs=(out_spec, out_spec), args=(proj, proj, proj),
        semantics=("parallel", "parallel"), jobs=jobs)


def _sb_bwd(proj, tot, dyb, B, S, name, jobs=()):
    T = B * S
    tq = SB_TILE
    nq = S // tq

    def body(q_ref, k_ref, v_ref, do_ref, tot_ref, dq_ref, dk_ref, dv_ref, dk_acc, dv_acc):
        hm = _half_masks()
        ji = lax.broadcasted_iota(jnp.int32, (tq, tq), 0)
        si = lax.broadcasted_iota(jnp.int32, (tq, tq), 1)
        tri_incl = _tri2(lambda j, s: j <= s)
        tri_excl = _tri2(lambda j, s: j < s)
        causal = si < ji
        mark = _mark_lanes()
        dk_acc[...] = jnp.zeros_like(dk_acc)
        dv_acc[...] = jnp.zeros_like(dv_acc)

        def q_block(qi, with_previous):
            r0 = qi * tq if isinstance(qi, int) else pl.multiple_of(qi * tq, tq)
            q_pair = q_ref[pl.ds(r0, tq), :] * ATT_SCALE
            do_pair = do_ref[pl.ds(r0, tq), :]
            tot_pair = tot_ref[pl.ds(r0, tq), :]
            qms = [jnp.where(hm[a], q_pair, 0) for a in range(2)]
            doms = [jnp.where(hm[a], do_pair, 0) for a in range(2)]
            totals = [jnp.max(jnp.where(hm[a] & ~mark, tot_pair, -jnp.inf), axis=1, keepdims=True) for a in range(2)]

            def tiles(blocks, state):
                two, nb = range(2), range(len(blocks))
                last = SB_HALF - 1
                kk = [k_ref[pl.ds(c0, tq), :] for c0, _ in blocks]
                vv = [v_ref[pl.ds(c0, tq), :] for c0, _ in blocks]
                z = [[_dot_nt(qms[a], kk[b]) for a in two] for b in nb]
                d_att = [[_dot_nt(doms[a], vv[b]) for a in two] for b in nb]
                lb = [[_log_sigmoid(z[b][a]) for a in two] for b in nb]
                l1m = [[jnp.where(causal, lb[b][a] - z[b][a], 0.0) if blocks[b][1] else lb[b][a] - z[b][a]
                        for a in two] for b in nb]
                cum = [[_half_cumsums(l1m[b][a], tri_incl) for a in two] for b in nb]
                cp, upto = [[state[a][1] for a in two]], []
                for b in nb:
                    upto.append([jnp.concatenate(
                        [cum[b][a][0] + cp[b][a], cum[b][a][1] + (cp[b][a] + cum[b][a][0][:, last:last + 1])], axis=1)
                        for a in two])
                    cp.append([upto[b][a][:, tq - 1:tq] for a in two])
                att = [[jnp.exp(lb[b][a] + (totals[a] - upto[b][a])) for a in two] for b in nb]
                att = [[jnp.where(causal, att[b][a], 0.0) if blocks[b][1] else att[b][a] for a in two] for b in nb]
                d_log = [[d_att[b][a] * att[b][a] for a in two] for b in nb]
                cumd = [[_half_cumsums(d_log[b][a], tri_excl) for a in two] for b in nb]
                totd = [[[cumd[b][a][h][:, last:last + 1] + d_log[b][a][:, h * SB_HALF + last:h * SB_HALF + last + 1]
                          for h in two] for a in two] for b in nb]
                cq = [[state[a][2] for a in two]]
                for b in nb:
                    cq.append([cq[b][a] + (totd[b][a][0] + totd[b][a][1]) for a in two])
                before = [[jnp.concatenate([cumd[b][a][0] + cq[b][a], cumd[b][a][1] + (cq[b][a] + totd[b][a][0])], axis=1)
                           for a in two] for b in nb]
                sig = [[jnp.exp(lb[b][a]) for a in two] for b in nb]
                dz = [[d_log[b][a] * (1.0 - sig[b][a]) - sig[b][a] * before[b][a] for a in two] for b in nb]
                dz = [[jnp.where(causal, dz[b][a], 0.0) if blocks[b][1] else dz[b][a] for a in two] for b in nb]
                dzb = [[dz[b][a].astype(_MXU_DTYPE) for a in two] for b in nb]
                dq = [state[a][0] for a in two]
                for b, (c0, _) in enumerate(blocks):
                    ks = kk[b] * ATT_SCALE
                    dq = [dq[a] + _dot(dzb[b][a], jnp.where(hm[a], ks, 0)) for a in two]
                    dk_acc[pl.ds(c0, tq), :] += _dot_tn(dzb[b][0], qms[0]) + _dot_tn(dzb[b][1], qms[1])
                    dv_acc[pl.ds(c0, tq), :] += (_dot_tn(att[b][0].astype(_MXU_DTYPE), doms[0])
                                                 + _dot_tn(att[b][1].astype(_MXU_DTYPE), doms[1]))
                return tuple((dq[a], cp[-1][a], cq[-1][a]) for a in two)

            zero_col = jnp.zeros((tq, 1), F32)
            zero = (jnp.zeros((tq, LANES), F32), zero_col, zero_col)
            if with_previous:
                first = jnp.max(jnp.where(mark, tot_pair, -jnp.inf))
                first = jnp.where((first >= 0.0) & (first <= (qi - 1).astype(F32)), first, 0.0).astype(jnp.int32)
                state = lax.fori_loop(first, qi - 1,
                                      lambda kj, st: tiles([(pl.multiple_of(kj * tq, tq), False)], st), (zero, zero))
                state = tiles([(pl.multiple_of(r0 - tq, tq), False), (r0, True)], state)
            else:
                state = tiles([(r0, True)], (zero, zero))
            dq_ref[pl.ds(r0, tq), :] = (state[0][0] + state[1][0]).astype(dq_ref.dtype)

        q_block(0, False)

        def q_loop(qi, carry):
            q_block(qi, True)
            return carry

        lax.fori_loop(1, nq, q_loop, 0)
        dk_ref[...] = dk_acc[...].astype(dk_ref.dtype)
        dv_ref[...] = dv_acc[...].astype(dv_ref.dtype)

    pair = pl.BlockSpec((S, LANES), lambda b, p: (b, p))
    out = jax.ShapeDtypeStruct((T, SB_W), _MXU_DTYPE)
    return _call(
        body, name=name, grid=(B, SB_W // LANES), out_shape=(out, out, out),
        in_specs=_sb_specs(B, S) + [pair, pair], out_specs=(pair, pair, pair),
        scratch_shapes=[pltpu.VMEM((S, LANES), F32), pltpu.VMEM((S, LANES), F32)],
        args=(proj, proj, proj, dyb, tot), semantics=("parallel", "parallel"), jobs=jobs)


def _layer_step(x, tgt, B, S, small, comm):
    run, big, part = comm.run, comm.big, comm.partial
    ffn1_w, ffn2_w = ("ffn1_down", "ffn1_gate", "ffn1_up"), ("ffn2_down", "ffn2_gate", "ffn2_up")

    h1 = run(_rms_fwd, x, small["ffn1_norm"], "ffn1_rms", ag=(FIRST_GATHERED,),
             cast=tuple(n for n in BIG_NAMES if n != FIRST_GATHERED))
    U1 = run(_mm_nt, h1, big["ffn1_up"], _MXU_DTYPE, "ffn1_up", ag=("ffn1_gate",))
    G1, A1 = run(_ffn_gate, h1, big["ffn1_gate"], U1, "ffn1_gate", ag=("ffn1_down",))
    x1 = run(_mm_nn, [(A1, big["ffn1_down"])], x, 0.5, F32, "ffn1_down", ag=("w_in",))
    h2 = run(_rms_fwd, x1, small["mix_norm"], "mix_rms")
    proj = run(_mm_nt, h2, big["w_in"], _MXU_DTYPE, "in_proj", ag=("w_out",), ag_early=AG_PARTS)
    ya = run(_swa_fwd, proj, small["swa_sinks"], B, S, "swa_fwd", ag=("ffn2_gate",))
    yb, tot = run(_sb_fwd, proj, B, S, "sb_fwd", ag=("ffn2_up",))
    yn = _outnorm_fwd(ya, yb, small["swa_out_norm"], small["sb_out_norm"], "out_norm")
    x2 = run(_mm_nn, [(yn, big["w_out"])], x1, 1.0, F32, "out_proj")
    h3 = run(_rms_fwd, x2, small["ffn2_norm"], "ffn2_rms")
    G2, U2, A2 = run(_ffn_gu, h3, big["ffn2_gate"], big["ffn2_up"], "ffn2_gate_up", ag=("ffn2_down",),
                     ag_early=AG_PARTS)
    x3 = run(_mm_nn, [(A2, big["ffn2_down"])], x2, 0.5, F32, "ffn2_down")

    dx3, dx3b, d_final, loss = _loss_head(x3, small["final_norm"], tgt, "loss_head")

    dG2, dU2 = run(_ffn_bwd_act, dx3b, big["ffn2_down"], G2, U2, "ffn2_bwd_act")
    part["ffn2_down"] = run(_mm_tn, A2, dx3b, 0.5, _WIRE_DTYPE, "ffn2_dw_down")
    part["ffn2_gate"] = run(_mm_tn, dG2, h3, 1.0, _WIRE_DTYPE, "ffn2_dw_gate")
    part["ffn2_up"] = run(_mm_tn, dU2, h3, 1.0, _WIRE_DTYPE, "ffn2_dw_up")
    dh3 = run(_mm_nn, [(dG2, big["ffn2_gate"])], None, 1.0, F32, "ffn2_dh_gate", rs1=ffn2_w)
    dh3 = run(_mm_nn, [(dU2, big["ffn2_up"])], dh3, 1.0, F32, "ffn2_dh_up")
    dx2, dx2b, d_g2 = _rms_bwd(dh3, x2, small["ffn2_norm"], dx3, "ffn2_rms_bwd")

    part["w_out"] = run(_mm_tn, yn, dx2b, 1.0, _WIRE_DTYPE, "dw_out")
    dyn = run(_mm_nt, dx2b, big["w_out"], F32, "out_proj_bwd")
    dya, dyb, d_ga, d_gb = _outnorm_bwd(dyn, ya, yb, small["swa_out_norm"], small["sb_out_norm"], "out_norm_bwd")
    dqa, dka, dva, d_sinks = run(_swa_bwd, proj, small["swa_sinks"], dya, B, S, "swa_bwd", rs2=ffn2_w[:1])
    dqb, dkb, dvb = run(_sb_bwd, proj, tot, dyb, B, S, "sb_bwd", rs2=ffn2_w[1:])
    dproj = jnp.concatenate([dqa, dka, dva, dqb, dkb, dvb], axis=1)
    part["w_in"] = run(_mm_tn, dproj, h2, 1.0, _WIRE_DTYPE, "dw_in", adamw=("ffn2_down", "ffn2_gate"))
    dh2 = run(_mm_nn, [(dproj, big["w_in"])], None, 1.0, F32, "in_proj_bwd", rs1=("w_in", "w_out"))
    dx1, dx1b, d_gm = _rms_bwd(dh2, x1, small["mix_norm"], dx2, "mix_rms_bwd")

    dG1, dU1 = run(_ffn_bwd_act, dx1b, big["ffn1_down"], G1, U1, "ffn1_bwd_act", rs2=("w_in", "w_out"))
    part["ffn1_down"] = run(_mm_tn, A1, dx1b, 0.5, _WIRE_DTYPE, "ffn1_dw_down", adamw=("ffn2_up", "w_in", "w_out"))
    part["ffn1_gate"] = run(_mm_tn, dG1, h1, 1.0, _WIRE_DTYPE, "ffn1_dw_gate", rs1=("ffn1_down",))
    part["ffn1_up"] = run(_mm_tn, dU1, h1, 1.0, _WIRE_DTYPE, "ffn1_dw_up", rs1=("ffn1_gate",), rs2=("ffn1_down",))
    dh1 = run(_mm_nn, [(dG1, big["ffn1_gate"])], None, 1.0, F32, "ffn1_dh_gate", rs1=("ffn1_up",), rs2=("ffn1_gate",))
    dh1 = run(_mm_nn, [(dU1, big["ffn1_up"])], dh1, 1.0, F32, "ffn1_dh_up", rs2=("ffn1_up",))
    gx, _, d_g1 = _rms_bwd(dh1, x, small["ffn1_norm"], dx1, "ffn1_rms_bwd")

    d_small = {"ffn1_norm": d_g1, "mix_norm": d_gm, "swa_sinks": d_sinks[:, :N_SWA_HEADS], "swa_out_norm": d_ga,
               "sb_out_norm": d_gb, "ffn2_norm": d_g2, "final_norm": d_final}
    return loss, gx, d_small


MESH = pl.DeviceIdType.MESH
BIG_NAMES = ("ffn1_gate", "ffn1_up", "ffn1_down", "w_in", "w_out", "ffn2_gate", "ffn2_up", "ffn2_down")
FIRST_GATHERED = "ffn1_up"
_COMM_PARAMS = pltpu.CompilerParams(has_side_effects=True)


def _place():
    x, y, c = lax.axis_index("x"), lax.axis_index("y"), lax.axis_index("c")
    other_chips = [(1 - x, y), (x, 1 - y), (1 - x, 1 - y)]
    return x, y, c, other_chips


def _padded_rows(rows):
    full = N_DEV * rows
    return -(-full // _F_TILE) * _F_TILE


AG_PARTS = 4


def _row_parts(rows, n):
    units = rows // 16
    assert units * 16 == rows and units >= n
    out, off = [], 0
    for i in range(n):
        size = (units // n + (1 if i < units % n else 0)) * 16
        out.append((off, size))
        off += size
    return out


def _ag_job(shards, early_parts=AG_PARTS // 2):
    nw = len(shards)
    D = shards[0].shape[1]
    rows_w = [s.shape[0] for s in shards]
    full_w = [_padded_rows(r) for r in rows_w]
    pad_w = [f - N_DEV * r for f, r in zip(full_w, rows_w)]
    max_pad = max(max(pad_w), 16)
    n_parts = AG_PARTS
    parts_w = [_row_parts(r, n_parts) for r in rows_w]

    class Plan:
        def __init__(self, ins, outs, scratch):
            zbuf, send_sems, recv_sems, local_sems, zero_sems = scratch
            x, y, c, chips = _place()
            me, sibling = (x, y, c), (x, y, 1 - c)

            def rows(w, block, part=None):
                off, size = (0, rows_w[w]) if part is None else part
                px, py, pc = block
                start = pl.multiple_of((4 * px + 2 * py + pc) * rows_w[w] + off, 16)
                return outs[w].at[pl.ds(start, size), :]

            def copy(w, k, block, to, part=None, own=False):
                src = rows(w, block, part)
                if own:
                    src = ins[w] if part is None else ins[w].at[pl.ds(part[0], part[1]), :]
                return pltpu.make_async_remote_copy(
                    src_ref=src, dst_ref=rows(w, block, part), send_sem=send_sems.at[w, k],
                    recv_sem=recv_sems.at[w, k], device_id=to, device_id_type=MESH)

            def k_ici(j, p):
                return 1 + j * n_parts + p

            def k_on(j, p):
                return 1 + (3 + j) * n_parts + p

            self.zbuf = zbuf
            self.local = [pltpu.make_async_copy(zbuf.at[pl.ds(0, pad_w[w]), :],
                                                outs[w].at[pl.ds(N_DEV * rows_w[w], pad_w[w]), :], zero_sems.at[w])
                          for w in range(nw) if pad_w[w]]
            self.local += [pltpu.make_async_copy(ins[w], rows(w, me), local_sems.at[w]) for w in range(nw)]
            self.first = [[copy(w, 0, me, sibling, own=True)]
                          + [copy(w, k_ici(j, p), me, (*chip, c), part, own=True)
                             for p, part in enumerate(parts_w[w]) for j, chip in enumerate(chips)]
                          for w in range(nw)]
            self.arrive = [[copy(w, k_ici(j, p), (*chip, c), me, part)
                            for p, part in enumerate(parts_w[w]) for j, chip in enumerate(chips)] for w in range(nw)]
            self.passed = [[copy(w, k_on(j, p), (*chip, c), sibling, part)
                            for p, part in enumerate(parts_w[w]) for j, chip in enumerate(chips)] for w in range(nw)]
            self.from_sibling = [[copy(w, 0, sibling, me)]
                                 + [copy(w, k_on(j, p), (*chip, 1 - c), me, part)
                                    for p, part in enumerate(parts_w[w]) for j, chip in enumerate(chips)]
                                 for w in range(nw)]

    def start(ins, outs, scratch):
        plan = Plan(ins, outs, scratch)
        plan.zbuf[...] = jnp.zeros_like(plan.zbuf)
        for cp in plan.local:
            cp.start()
        for w in range(nw):
            for cp in plan.first[w]:
                cp.start()

    def mid(ins, outs, scratch, phase):
        plan = Plan(ins, outs, scratch)
        early = 3 * early_parts
        for w in range(nw):
            pairs = list(zip(plan.arrive[w], plan.passed[w]))
            for arrived, onward in (pairs[:early] if phase == 0 else pairs[early:]):
                arrived.wait_recv()
                onward.start()

    def finish(ins, outs, scratch):
        plan = Plan(ins, outs, scratch)
        for w in range(nw):
            for cp in plan.from_sibling[w]:
                cp.wait_recv()
        for w in range(nw):
            for cp in plan.first[w] + plan.passed[w]:
                cp.wait_send()
        for cp in plan.local:
            cp.wait()

    return _Job(
        ins=shards, out_shape=[jax.ShapeDtypeStruct((f, D), s.dtype) for f, s in zip(full_w, shards)],
        scratch=[pltpu.VMEM((max_pad, D), shards[0].dtype), pltpu.SemaphoreType.DMA((nw, 1 + 6 * n_parts)),
                 pltpu.SemaphoreType.DMA((nw, 1 + 6 * n_parts)), pltpu.SemaphoreType.DMA((nw,)),
                 pltpu.SemaphoreType.DMA((nw,))],
        start=start, mid=mid, finish=finish)


def _rs1_job(partials, rows_w):
    nw = len(partials)
    D = partials[0].shape[1]

    def copies(ins, outs, scratch):
        send_sems, recv_sems = scratch
        x, y, c, _ = _place()
        out = []
        for w in range(nw):
            r = rows_w[w]
            for q in range(4):
                src = ins[w].at[pl.ds(pl.multiple_of((2 * q + 1 - c) * r, 16), r), :]
                out.append(pltpu.make_async_remote_copy(
                    src_ref=src, dst_ref=outs[w].at[pl.ds(q * r, r), :], send_sem=send_sems.at[w, q],
                    recv_sem=recv_sems.at[w, q], device_id=(x, y, 1 - c), device_id_type=MESH))
        return out

    def start(ins, outs, scratch):
        for cp in copies(ins, outs, scratch):
            cp.start()

    def finish(ins, outs, scratch):
        for cp in copies(ins, outs, scratch):
            cp.wait()

    return _Job(
        ins=partials, out_shape=[jax.ShapeDtypeStruct((4 * r, D), p.dtype) for r, p in zip(rows_w, partials)],
        scratch=[pltpu.SemaphoreType.DMA((nw, 4)), pltpu.SemaphoreType.DMA((nw, 4))], start=start, finish=finish)


def _pair_sum(partial, from_sibling, rows, core, name):
    D = partial.shape[1]

    def body(core_ref, p_ref, s_ref, o_ref):
        o_ref[...] = (p_ref[...].astype(F32) + s_ref[...].astype(F32)).astype(o_ref.dtype)

    grid_spec = pltpu.PrefetchScalarGridSpec(
        num_scalar_prefetch=1, grid=(4,),
        in_specs=[pl.BlockSpec((rows, D), lambda q, core_ref: (2 * q + core_ref[0], 0)),
                  pl.BlockSpec((rows, D), lambda q, core_ref: (q, 0))],
        out_specs=pl.BlockSpec((rows, D), lambda q, core_ref: (q, 0)))
    return pl.pallas_call(
        body, name=name, grid_spec=grid_spec, out_shape=jax.ShapeDtypeStruct((4 * rows, D), partial.dtype),
        compiler_params=_params("arbitrary"),
    )(core, partial, from_sibling)


def _rs2_job(chip_sums, rows_w):
    nw = len(chip_sums)

    def copies(ins, outs, scratch):
        send_sems, recv_sems, local_sems = scratch
        x, y, c, chips = _place()
        my_chip = 2 * x + y
        out = []
        for w in range(nw):
            r = rows_w[w]
            mine = pl.ds(pl.multiple_of(my_chip * r, 16), r)
            out.append(pltpu.make_async_copy(ins[w].at[mine, :], outs[w].at[mine, :], local_sems.at[w]))
            for j, (qx, qy) in enumerate(chips):
                src = ins[w].at[pl.ds(pl.multiple_of((2 * qx + qy) * r, 16), r), :]
                out.append(pltpu.make_async_remote_copy(
                    src_ref=src, dst_ref=outs[w].at[mine, :], send_sem=send_sems.at[w, j],
                    recv_sem=recv_sems.at[w, j], device_id=(qx, qy, c), device_id_type=MESH))
        return out

    def start(ins, outs, scratch):
        for cp in copies(ins, outs, scratch):
            cp.start()

    def finish(ins, outs, scratch):
        for cp in copies(ins, outs, scratch):
            cp.wait()

    return _Job(
        ins=chip_sums, out_shape=[jax.ShapeDtypeStruct(s.shape, s.dtype) for s in chip_sums],
        scratch=[pltpu.SemaphoreType.DMA((nw, 3)), pltpu.SemaphoreType.DMA((nw, 3)), pltpu.SemaphoreType.DMA((nw,))],
        start=start, finish=finish)


class _Comm:
    def __init__(self, shards, state):
        self.shards, self.state = dict(shards), state
        self.rows = {n: st[0].shape[0] for n, st in state.items()}
        self.core = lax.axis_index("c").astype(jnp.int32).reshape(1)
        self.big, self.partial, self.chip_sums, self.slots, self.updates = {}, {}, {}, {}, {}

    def slots3(self, name):
        return self.slots[name].reshape(4, self.rows[name], -1)

    def run(self, fn, *args, ag=(), rs1=(), rs2=(), adamw=(), cast=(), ag_early=AG_PARTS // 2):
        jobs = []
        if cast:
            jobs.append(_cast_job([self.state[n][0] for n in cast], _WIRE_DTYPE))
        if ag:
            jobs.append(_ag_job([self.shards[n] for n in ag], ag_early))
        if rs1:
            jobs.append(_rs1_job([self.partial[n] for n in rs1], [self.rows[n] for n in rs1]))
        if rs2:
            jobs.append(_rs2_job([self.chip_sums[n] for n in rs2], [self.rows[n] for n in rs2]))
        for n in adamw:
            w2, m2, v2 = self.state[n]
            jobs.append(_adamw_job(w2, self.slots3(n), m2, v2))
        out, job_res = fn(*args, jobs=jobs)
        job_res = iter(job_res)
        if cast:
            self.shards.update(zip(cast, next(job_res)))
        if ag:
            self.big.update(zip(ag, next(job_res)))
        if rs1:
            for n, got in zip(rs1, next(job_res)):
                self.chip_sums[n] = _pair_sum(self.partial[n], got, self.rows[n], self.core, "pair_sum_" + n)
        if rs2:
            self.slots.update(zip(rs2, next(job_res)))
        for n in adamw:
            self.updates[n] = next(job_res)
        return out


SMALL_ROWS = 88


def _small_allreduce(vec):
    def body(v_ref, o_ref, gather, send_sems, recv_sems):
        x, y, c, _ = _place()
        my_id = 4 * x + 2 * y + c
        gather[my_id] = v_ref[...]
        copies = []
        for r in range(1, N_DEV):
            peer = (x ^ (r >> 2), y ^ ((r >> 1) & 1), c ^ (r & 1))
            cp = pltpu.make_async_remote_copy(src_ref=v_ref, dst_ref=gather.at[my_id], send_sem=send_sems.at[r - 1],
                                              recv_sem=recv_sems.at[r - 1], device_id=peer, device_id_type=MESH)
            cp.start()
            copies.append(cp)
        for cp in copies:
            cp.wait()
        acc = gather[0]
        for d in range(1, N_DEV):
            acc = acc + gather[d]
        o_ref[...] = acc

    vm = pl.BlockSpec(memory_space=pltpu.VMEM)
    return pl.pallas_call(
        body, name="small_allreduce", out_shape=jax.ShapeDtypeStruct(vec.shape, F32),
        in_specs=[vm], out_specs=vm,
        scratch_shapes=[pltpu.VMEM((N_DEV,) + vec.shape, F32), pltpu.SemaphoreType.DMA((N_DEV - 1,)),
                        pltpu.SemaphoreType.DMA((N_DEV - 1,))],
        compiler_params=_COMM_PARAMS,
    )(vec)


def _adamw_update(w, g, m, v):
    nm = ADAM_B1 * m + (1.0 - ADAM_B1) * g
    nv = ADAM_B2 * v + (1.0 - ADAM_B2) * jnp.square(g)
    m_hat = nm / (1.0 - ADAM_B1 ** ADAM_STEP)
    v_hat = nv / (1.0 - ADAM_B2 ** ADAM_STEP)
    return -ADAM_LR * (m_hat / (jnp.sqrt(v_hat) + ADAM_EPS) + ADAM_WD * w), nm, nv


def _adamw(w, g, m, v, name):
    R, C = w.shape
    tr = _tile(R, 256, 8)

    def body(w_ref, g_ref, m_ref, v_ref, d_ref, nm_ref, nv_ref):
        d_ref[...], nm_ref[...], nv_ref[...] = _adamw_update(w_ref[...], g_ref[...], m_ref[...], v_ref[...])

    spec = pl.BlockSpec((tr, C), lambda i: (i, 0))
    out = jax.ShapeDtypeStruct((R, C), F32)
    return pl.pallas_call(
        body, name=name, grid=(R // tr,), out_shape=(out, out, out),
        in_specs=[spec] * 4, out_specs=(spec, spec, spec),
        compiler_params=_params("parallel"),
    )(w, g, m, v)


def _adamw_slots(w, slots, m, v, name):
    R, C = w.shape
    tc = _tile(C, 512, LANES)
    spec = pl.BlockSpec((R, tc), lambda j: (0, j))
    out = jax.ShapeDtypeStruct((R, C), F32)
    return pl.pallas_call(
        functools.partial(_adamw_slots_body), name=name, grid=(C // tc,), out_shape=(out, out, out, out),
        in_specs=[spec, pl.BlockSpec((4, R, tc), lambda j: (0, 0, j)), spec, spec], out_specs=(spec, spec, spec, spec),
        compiler_params=_params("parallel"),
    )(w, slots, m, v)


def _adamw_slots_body(w_ref, s_ref, m_ref, v_ref, g_ref, d_ref, nm_ref, nv_ref):
    g = s_ref[0].astype(F32)
    for q in range(1, 4):
        g = g + s_ref[q].astype(F32)
    g_ref[...] = g
    d_ref[...], nm_ref[...], nv_ref[...] = _adamw_update(w_ref[...], g, m_ref[...], v_ref[...])


def _cast_job(arrays, dtype):
    C = arrays[0].shape[1]

    def specs(grid):
        total = 1
        for g in grid:
            total *= g
        tc = C // total
        assert tc * total == C and tc % LANES == 0, (C, grid)
        blocks = [pl.BlockSpec((a.shape[0], tc), lambda *ids: (0, _linear_step(grid, ids))) for a in arrays]
        return blocks, list(blocks)

    def each(ins, outs, scratch, step):
        for i_ref, o_ref in zip(ins, outs):
            o_ref[...] = i_ref[...].astype(o_ref.dtype)

    return _Job(ins=arrays, out_shape=[jax.ShapeDtypeStruct(a.shape, dtype) for a in arrays], specs=specs, each=each)


def _adamw_job(w, slots, m, v):
    R, C = w.shape

    n_slices = C // LANES

    def specs(grid):
        total = 1
        for g in grid:
            total *= g
        assert total >= n_slices, (grid, n_slices)
        col = lambda *ids: jnp.minimum(_linear_step(grid, ids), n_slices - 1)
        blk = pl.BlockSpec((R, LANES), lambda *ids: (0, col(*ids)))
        slot_blk = pl.BlockSpec((4, R, LANES), lambda *ids: (0, 0, col(*ids)))
        return [blk, slot_blk, blk, blk], [blk] * 4

    def each(ins, outs, scratch, step):
        @pl.when(step < n_slices)
        def _():
            _adamw_slots_body(*ins, *outs)

    out = jax.ShapeDtypeStruct((R, C), F32)
    return _Job(ins=[w, slots, m, v], out_shape=[out] * 4, specs=specs, each=each)


WEIGHT_NAMES = ("ffn1_norm", "ffn1_w_gate", "ffn1_w_up", "ffn1_w_down", "mix_norm", "w_in", "swa_sinks",
                "swa_out_norm", "sb_out_norm", "w_out", "ffn2_norm", "ffn2_w_gate", "ffn2_w_up", "ffn2_w_down",
                "final_norm")
SMALL_NAMES = ("ffn1_norm", "mix_norm", "swa_sinks", "swa_out_norm", "sb_out_norm", "ffn2_norm", "final_norm")
BIG_ARGS = {"ffn1_gate": ("ffn1_w_gate", True), "ffn1_up": ("ffn1_w_up", True), "ffn1_down": ("ffn1_w_down", False),
            "w_in": ("w_in", True), "w_out": ("w_out", False), "ffn2_gate": ("ffn2_w_gate", True),
            "ffn2_up": ("ffn2_w_up", True), "ffn2_down": ("ffn2_w_down", False)}


def _pack_small(parts):
    padded = [jnp.pad(p.reshape(1, -1), ((0, 0), (0, -p.size % LANES))) for p in parts]
    flat = jnp.concatenate(padded, axis=1)
    flat = jnp.pad(flat, ((0, 0), (0, SMALL_ROWS * LANES - flat.shape[1])))
    return flat.reshape(SMALL_ROWS, LANES)


def _unpack_small(block, shapes):
    flat = block.reshape(-1)
    out, off = [], 0
    for shp in shapes:
        n = 1
        for s in shp:
            n *= s
        out.append(flat[off:off + n].reshape(shp))
        off += n + (-n % LANES)
    return out


def kernel(x, ffn1_norm, ffn1_w_gate, ffn1_w_up, ffn1_w_down, mix_norm, w_in, swa_sinks, swa_out_norm, sb_out_norm, w_out, ffn2_norm, ffn2_w_gate, ffn2_w_up, ffn2_w_down, final_norm, loss_target, m_ffn1_norm, m_ffn1_w_gate, m_ffn1_w_up, m_ffn1_w_down, m_mix_norm, m_w_in, m_swa_sinks, m_swa_out_norm, m_sb_out_norm, m_w_out, m_ffn2_norm, m_ffn2_w_gate, m_ffn2_w_up, m_ffn2_w_down, m_final_norm, v_ffn1_norm, v_ffn1_w_gate, v_ffn1_w_up, v_ffn1_w_down, v_mix_norm, v_w_in, v_swa_sinks, v_swa_out_norm, v_sb_out_norm, v_w_out, v_ffn2_norm, v_ffn2_w_gate, v_ffn2_w_up, v_ffn2_w_down, v_final_norm):
    args = dict(locals())
    B, S, D = x.shape
    T = B * S
    weights = {n: args[n] for n in WEIGHT_NAMES}
    mom_m = {n: args["m_" + n] for n in WEIGHT_NAMES}
    mom_v = {n: args["v_" + n] for n in WEIGHT_NAMES}

    state = {}
    for name in BIG_NAMES:
        arg, transposed = BIG_ARGS[name]
        to_rows = (lambda t: t[0].T) if transposed else (lambda t: t[0])
        state[name] = tuple(to_rows(t[arg]) for t in (weights, mom_m, mom_v))
    comm = _Comm({FIRST_GATHERED: state[FIRST_GATHERED][0].astype(_WIRE_DTYPE)}, state)
    small = {n: weights[n].reshape(1, -1) for n in SMALL_NAMES}

    loss, gx, d_small = _layer_step(x.reshape(T, D), loss_target.reshape(T, D), B, S, small, comm)

    small_shapes = [(1, 1)] + [d_small[n].shape for n in SMALL_NAMES]
    reduced = _small_allreduce(_pack_small([loss[:, :1]] + [d_small[n] for n in SMALL_NAMES]))
    red = _unpack_small(reduced, small_shapes)
    loss_out = red[0].reshape(())
    g_small = dict(zip(SMALL_NAMES, red[1:]))

    grads, deltas, new_m, new_v = {}, {}, {}, {}
    for name in BIG_NAMES:
        arg, transposed = BIG_ARGS[name]
        back = (lambda t: t.T[None]) if transposed else (lambda t: t[None])
        res = comm.updates.get(name)
        if res is None:
            w2, m2, v2 = state[name]
            res = _adamw_slots(w2, comm.slots3(name), m2, v2, "adamw_" + name)
        grads[arg], deltas[arg], new_m[arg], new_v[arg] = [back(t) for t in res]
    shapes1 = [(1, weights[n].size) for n in SMALL_NAMES]
    packed = [_pack_small([t[n].reshape(1, -1) for n in SMALL_NAMES]) for t in (weights, g_small, mom_m, mom_v)]
    upd = _adamw(*packed, "adamw_small")
    for tgt_dict, block in zip((deltas, new_m, new_v), upd):
        for n, val in zip(SMALL_NAMES, _unpack_small(block, shapes1)):
            tgt_dict[n] = val.reshape(weights[n].shape)
    for n in SMALL_NAMES:
        grads[n] = g_small[n].reshape(weights[n].shape)

    return (loss_out, gx.reshape(B, S, D), *[grads[n] for n in WEIGHT_NAMES], *[deltas[n] for n in WEIGHT_NAMES],
            *[new_m[n] for n in WEIGHT_NAMES], *[new_v[n] for n in WEIGHT_NAMES])
```

```python
import functools

import jax
import jax.numpy as jnp
from jax import lax
from jax.experimental import pallas as pl
from jax.experimental.pallas import tpu as pltpu

F32 = jnp.float32
_MXU_DTYPE = jnp.bfloat16
_WIRE_DTYPE = jnp.bfloat16

EPS = 1e-6
HEAD_DIM = 64
N_SWA_HEADS = 16
N_SWA_KV = 4
N_SB_HEADS = 16
WINDOW = 128
SWA_Q = N_SWA_HEADS * HEAD_DIM
SWA_KV = N_SWA_KV * HEAD_DIM
SB_W = N_SB_HEADS * HEAD_DIM
IN_W = SWA_Q + 2 * SWA_KV + 3 * SB_W
LANES = 128
ATT_SCALE = HEAD_DIM ** -0.5

ADAM_LR = 0.001
ADAM_B1 = 0.9
ADAM_B2 = 0.999
ADAM_EPS = 1e-08
ADAM_WD = 0.01
ADAM_STEP = 10

N_DEV = 8
_VMEM_LIMIT_BYTES = 56 * 1024 * 1024
_F_TILE = 512


def _params(*semantics):
    return pltpu.CompilerParams(dimension_semantics=semantics, vmem_limit_bytes=_VMEM_LIMIT_BYTES)


def _tile(n, pref, align):
    t = min(n, pref)
    t -= t % align
    while t >= align:
        if n % t == 0:
            return t
        t -= align
    return n


def _dot(a, b):
    return lax.dot_general(a, b, (((1,), (0,)), ((), ())), preferred_element_type=F32)


def _dot_nt(a, b):
    return lax.dot_general(a, b, (((1,), (1,)), ((), ())), preferred_element_type=F32)


def _dot_tn(a, b):
    return lax.dot_general(a, b, (((0,), (0,)), ((), ())), preferred_element_type=F32)


class _Job:
    def __init__(self, ins, out_shape, scratch=(), start=None, finish=None, mid=None, each=None, specs=None):
        self.ins, self.out_shape, self.scratch = list(ins), list(out_shape), list(scratch)
        self.start, self.mid, self.finish, self.each, self.specs = start, mid, finish, each, specs


_JOB_MID_FRACTION = 0.6


def _linear_step(grid, ids):
    step = ids[0]
    for d in range(1, len(grid)):
        step = step * grid[d] + ids[d]
    return step


def _call(body, *, name, grid, in_specs, out_specs, out_shape, args, semantics, scratch_shapes=(), jobs=()):
    single = not isinstance(out_shape, (tuple, list))
    if not jobs:
        res = pl.pallas_call(body, name=name, grid=grid, in_specs=list(in_specs), out_specs=out_specs,
                             out_shape=out_shape, scratch_shapes=list(scratch_shapes),
                             compiler_params=_params(*semantics))(*args)
        return res, []
    base_out = [out_shape] if single else list(out_shape)
    base_out_specs = [out_specs] if single else list(out_specs)
    n_in, n_out, n_scr = len(args), len(base_out), len(scratch_shapes)
    any_spec = pl.BlockSpec(memory_space=pl.ANY)
    total = 1
    for g in grid:
        total *= g
    mid_step = min(total - 1, int(total * _JOB_MID_FRACTION))

    def wrapped(*refs):
        pos = n_in
        job_ins = []
        for job in jobs:
            job_ins.append(refs[pos:pos + len(job.ins)])
            pos += len(job.ins)
        outs = refs[pos:pos + n_out]
        pos += n_out
        job_outs = []
        for job in jobs:
            job_outs.append(refs[pos:pos + len(job.out_shape)])
            pos += len(job.out_shape)
        scr = refs[pos:pos + n_scr]
        pos += n_scr
        job_scr = []
        for job in jobs:
            job_scr.append(refs[pos:pos + len(job.scratch)])
            pos += len(job.scratch)
        bound = list(zip(jobs, job_ins, job_outs, job_scr))
        step = _linear_step(grid, [pl.program_id(d) for d in range(len(grid))])

        @pl.when(step == 0)
        def _():
            for job, ji, jo, js in bound:
                if job.start is not None:
                    job.start(ji, jo, js)

        @pl.when(step == mid_step)
        def _():
            for job, ji, jo, js in bound:
                if job.mid is not None:
                    job.mid(ji, jo, js, 0)

        body(*refs[:n_in], *outs, *scr)
        for job, ji, jo, js in bound:
            if job.each is not None:
                job.each(ji, jo, js, step)

        @pl.when(step == total - 1)
        def _():
            for job, ji, jo, js in bound:
                if job.mid is not None:
                    job.mid(ji, jo, js, 1)
            for job, ji, jo, js in bound:
                if job.finish is not None:
                    job.finish(ji, jo, js)

    all_args, all_in_specs = list(args), list(in_specs)
    all_out_shape, all_out_specs = list(base_out), list(base_out_specs)
    for job in jobs:
        job_in_specs, job_out_specs = (job.specs(grid) if job.specs is not None else
                                       ([any_spec] * len(job.ins), [any_spec] * len(job.out_shape)))
        all_args += job.ins
        all_in_specs += job_in_specs
        all_out_shape += job.out_shape
        all_out_specs += job_out_specs
    all_scratch = list(scratch_shapes) + [s for job in jobs for s in job.scratch]
    res = pl.pallas_call(
        wrapped, name=name, grid=grid, in_specs=all_in_specs, out_specs=tuple(all_out_specs),
        out_shape=tuple(all_out_shape), scratch_shapes=all_scratch,
        compiler_params=pltpu.CompilerParams(dimension_semantics=("arbitrary",) * len(grid),
                                             vmem_limit_bytes=_VMEM_LIMIT_BYTES,
                                             has_side_effects=any(job.start is not None for job in jobs)),
    )(*all_args)
    base = res[0] if single else tuple(res[:n_out])
    job_res, pos = [], n_out
    for job in jobs:
        job_res.append(tuple(res[pos:pos + len(job.out_shape)]))
        pos += len(job.out_shape)
    return base, job_res


def _rms_fwd(x, g, name, jobs=()):
    T, D = x.shape
    tm = _tile(T, 512, 16)

    def body(x_ref, g_ref, o_ref):
        xv = x_ref[...]
        r = lax.rsqrt(jnp.mean(xv * xv, axis=-1, keepdims=True) + EPS)
        o_ref[...] = (xv * r * g_ref[...]).astype(o_ref.dtype)

    return _call(
        body, name=name, grid=(T // tm,),
        out_shape=jax.ShapeDtypeStruct((T, D), _MXU_DTYPE),
        in_specs=[pl.BlockSpec((tm, D), lambda i: (i, 0)), pl.BlockSpec((1, D), lambda i: (0, 0))],
        out_specs=pl.BlockSpec((tm, D), lambda i: (i, 0)), args=(x, g), semantics=("parallel",), jobs=jobs)


def _rms_bwd_rows(dh, xv, g):
    r = lax.rsqrt(jnp.mean(xv * xv, axis=-1, keepdims=True) + EPS)
    xhat = xv * r
    u = dh * g
    dx = r * (u - xhat * jnp.mean(u * xhat, axis=-1, keepdims=True))
    return dx, dh * xhat


def _rms_bwd(dh, x, g, dres, name):
    T, D = x.shape
    tm = _tile(T, 256, 16)

    def body(dh_ref, x_ref, g_ref, dres_ref, dx_ref, dxb_ref, dg_ref):
        @pl.when(pl.program_id(0) == 0)
        def _():
            dg_ref[...] = jnp.zeros_like(dg_ref)

        dx, dgr = _rms_bwd_rows(dh_ref[...], x_ref[...], g_ref[...])
        dx = dres_ref[...] + dx
        dx_ref[...] = dx
        dxb_ref[...] = dx.astype(dxb_ref.dtype)
        dg_ref[...] += jnp.sum(dgr, axis=0, keepdims=True)

    row = pl.BlockSpec((tm, D), lambda i: (i, 0))
    vec = pl.BlockSpec((1, D), lambda i: (0, 0))
    return pl.pallas_call(
        body, name=name, grid=(T // tm,),
        out_shape=(jax.ShapeDtypeStruct((T, D), F32), jax.ShapeDtypeStruct((T, D), _MXU_DTYPE),
                   jax.ShapeDtypeStruct((1, D), F32)),
        in_specs=[row, row, vec, row], out_specs=(row, row, vec),
        compiler_params=_params("arbitrary"),
    )(dh, x, g, dres)


def _loss_head(x, g, tgt, name):
    T, D = x.shape
    tm = _tile(T, 256, 16)

    def body(x_ref, g_ref, t_ref, dx_ref, dxb_ref, dg_ref, loss_ref):
        @pl.when(pl.program_id(0) == 0)
        def _():
            dg_ref[...] = jnp.zeros_like(dg_ref)
            loss_ref[...] = jnp.zeros_like(loss_ref)

        xv = x_ref[...]
        gv = g_ref[...]
        r = lax.rsqrt(jnp.mean(xv * xv, axis=-1, keepdims=True) + EPS)
        xhat = xv * r
        diff = xhat * gv - t_ref[...]
        tok = jnp.mean(diff * diff, axis=-1, keepdims=True)
        loss_ref[...] += 0.5 * jnp.sum(tok, axis=0, keepdims=True)
        dy = diff / D
        u = dy * gv
        dx = r * (u - xhat * jnp.mean(u * xhat, axis=-1, keepdims=True))
        dx_ref[...] = dx
        dxb_ref[...] = dx.astype(dxb_ref.dtype)
        dg_ref[...] += jnp.sum(dy * xhat, axis=0, keepdims=True)

    row = pl.BlockSpec((tm, D), lambda i: (i, 0))
    vec = pl.BlockSpec((1, D), lambda i: (0, 0))
    return pl.pallas_call(
        body, name=name, grid=(T // tm,),
        out_shape=(jax.ShapeDtypeStruct((T, D), F32), jax.ShapeDtypeStruct((T, D), _MXU_DTYPE),
                   jax.ShapeDtypeStruct((1, D), F32), jax.ShapeDtypeStruct((1, LANES), F32)),
        in_specs=[row, vec, row],
        out_specs=(row, row, vec, pl.BlockSpec((1, LANES), lambda i: (0, 0))),
        compiler_params=_params("arbitrary"),
    )(x, g, tgt)


def _outnorm_fwd(ya, yb, ga, gb, name):
    T, W = ya.shape
    tm = _tile(T, 512, 16)

    def body(ya_ref, yb_ref, ga_ref, gb_ref, o_ref):
        for k, (y_ref, g_ref) in enumerate(((ya_ref, ga_ref), (yb_ref, gb_ref))):
            yv = y_ref[...]
            r = lax.rsqrt(jnp.mean(yv * yv, axis=-1, keepdims=True) + EPS)
            o_ref[:, k * W:(k + 1) * W] = (yv * r * g_ref[...]).astype(o_ref.dtype)

    row = pl.BlockSpec((tm, W), lambda i: (i, 0))
    vec = pl.BlockSpec((1, W), lambda i: (0, 0))
    return pl.pallas_call(
        body, name=name, grid=(T // tm,),
        out_shape=jax.ShapeDtypeStruct((T, 2 * W), _MXU_DTYPE),
        in_specs=[row, row, vec, vec], out_specs=pl.BlockSpec((tm, 2 * W), lambda i: (i, 0)),
        compiler_params=_params("parallel"),
    )(ya, yb, ga, gb)


def _outnorm_bwd(dyn, ya, yb, ga, gb, name):
    T, W = ya.shape
    tm = _tile(T, 256, 16)

    def body(d_ref, ya_ref, yb_ref, ga_ref, gb_ref, dya_ref, dyb_ref, dga_ref, dgb_ref):
        @pl.when(pl.program_id(0) == 0)
        def _():
            dga_ref[...] = jnp.zeros_like(dga_ref)
            dgb_ref[...] = jnp.zeros_like(dgb_ref)

        for k, (y_ref, g_ref, dy_ref, dg_ref) in enumerate(
                ((ya_ref, ga_ref, dya_ref, dga_ref), (yb_ref, gb_ref, dyb_ref, dgb_ref))):
            dy, dgr = _rms_bwd_rows(d_ref[:, k * W:(k + 1) * W], y_ref[...], g_ref[...])
            dy_ref[...] = dy.astype(dy_ref.dtype)
            dg_ref[...] += jnp.sum(dgr, axis=0, keepdims=True)

    row = pl.BlockSpec((tm, W), lambda i: (i, 0))
    vec = pl.BlockSpec((1, W), lambda i: (0, 0))
    return pl.pallas_call(
        body, name=name, grid=(T // tm,),
        out_shape=(jax.ShapeDtypeStruct((T, W), _MXU_DTYPE), jax.ShapeDtypeStruct((T, W), _MXU_DTYPE),
                   jax.ShapeDtypeStruct((1, W), F32), jax.ShapeDtypeStruct((1, W), F32)),
        in_specs=[pl.BlockSpec((tm, 2 * W), lambda i: (i, 0)), row, row, vec, vec],
        out_specs=(row, row, vec, vec),
        compiler_params=_params("arbitrary"),
    )(dyn, ya, yb, ga, gb)


_STRIP_ROWS = 256


def _strips(rows):
    step = min(rows, _STRIP_ROWS)
    return [slice(r, r + step) for r in range(0, rows, step)]


def _ffn_gu(h, wg_t, wu_t, name, jobs=()):
    T, D = h.shape
    Fp = wg_t.shape[0]
    tm = _tile(T, 1024, 16)
    tn = _tile(Fp, _F_TILE, LANES)

    def body(h_ref, wg_ref, wu_ref, g_ref, u_ref, a_ref):
        for rows in _strips(tm):
            hv = h_ref[rows, :]
            g = _dot_nt(hv, wg_ref[...])
            u = _dot_nt(hv, wu_ref[...])
            g_ref[rows, :] = g.astype(g_ref.dtype)
            u_ref[rows, :] = u.astype(u_ref.dtype)
            a_ref[rows, :] = (g * jax.nn.sigmoid(g) * u).astype(a_ref.dtype)

    act = pl.BlockSpec((tm, tn), lambda n, m: (m, n))
    wsp = pl.BlockSpec((tn, D), lambda n, m: (n, 0))
    out = jax.ShapeDtypeStruct((T, Fp), _MXU_DTYPE)
    return _call(
        body, name=name, grid=(Fp // tn, T // tm), out_shape=(out, out, out),
        in_specs=[pl.BlockSpec((tm, D), lambda n, m: (m, 0)), wsp, wsp],
        out_specs=(act, act, act), args=(h, wg_t, wu_t), semantics=("parallel", "parallel"), jobs=jobs)


def _ffn_bwd_act(dxb, wd, G, U, name, jobs=()):
    T, D = dxb.shape
    Fp = wd.shape[0]
    tm = _tile(T, 1024, 16)
    tn = _tile(Fp, _F_TILE, LANES)

    def body(e_ref, wd_ref, g_ref, u_ref, dg_ref, du_ref):
        for rows in _strips(tm):
            da = 0.5 * _dot_nt(e_ref[rows, :], wd_ref[...])
            g = g_ref[rows, :].astype(F32)
            u = u_ref[rows, :].astype(F32)
            s = jax.nn.sigmoid(g)
            du_ref[rows, :] = (da * (g * s)).astype(du_ref.dtype)
            dg_ref[rows, :] = (da * u * (s * (1.0 + g * (1.0 - s)))).astype(dg_ref.dtype)

    act = pl.BlockSpec((tm, tn), lambda m, n: (m, n))
    out = jax.ShapeDtypeStruct((T, Fp), _MXU_DTYPE)
    return _call(
        body, name=name, grid=(T // tm, Fp // tn), out_shape=(out, out),
        in_specs=[pl.BlockSpec((tm, D), lambda m, n: (m, 0)), pl.BlockSpec((tn, D), lambda m, n: (n, 0)),
                  act, act],
        out_specs=(act, act), args=(dxb, wd, G, U), semantics=("parallel", "parallel"), jobs=jobs)


def _ffn_gate(h, wg_t, U, name, jobs=()):
    T, D = h.shape
    Fp = wg_t.shape[0]
    tm = _tile(T, 1024, 16)
    tn = _tile(Fp, _F_TILE, LANES)

    def body(h_ref, wg_ref, u_ref, g_ref, a_ref):
        for rows in _strips(tm):
            g = _dot_nt(h_ref[rows, :], wg_ref[...])
            g_ref[rows, :] = g.astype(g_ref.dtype)
            a_ref[rows, :] = (g * jax.nn.sigmoid(g) * u_ref[rows, :].astype(F32)).astype(a_ref.dtype)

    act = pl.BlockSpec((tm, tn), lambda m, n: (m, n))
    out = jax.ShapeDtypeStruct((T, Fp), _MXU_DTYPE)
    return _call(
        body, name=name, grid=(T // tm, Fp // tn), out_shape=(out, out),
        in_specs=[pl.BlockSpec((tm, D), lambda m, n: (m, 0)), pl.BlockSpec((tn, D), lambda m, n: (n, 0)), act],
        out_specs=(act, act), args=(h, wg_t, U), semantics=("parallel", "parallel"), jobs=jobs)


def _mm_nt(a, b, out_dtype, name, jobs=()):
    M, K = a.shape
    N = b.shape[0]
    tm = _tile(M, 1024, 16)
    tn = _tile(N, 1536, LANES)

    def body(a_ref, b_ref, o_ref):
        o_ref[...] = _dot_nt(a_ref[...], b_ref[...]).astype(o_ref.dtype)

    return _call(
        body, name=name, grid=(M // tm, N // tn), out_shape=jax.ShapeDtypeStruct((M, N), out_dtype),
        in_specs=[pl.BlockSpec((tm, K), lambda m, n: (m, 0)), pl.BlockSpec((tn, K), lambda m, n: (n, 0))],
        out_specs=pl.BlockSpec((tm, tn), lambda m, n: (m, n)), args=(a, b),
        semantics=("parallel", "parallel"), jobs=jobs)


_MM_OPERAND_BYTES = 26 * 1024 * 1024


def _k_tile(K, bytes_per_k, align):
    best = align
    for t in range(align, K + 1, align):
        if K % t == 0 and 2 * t * bytes_per_k <= _MM_OPERAND_BYTES:
            best = t
    return best


def _mm_nn(pairs, res, alpha, out_dtype, name, jobs=()):
    M, K = pairs[0][0].shape
    N = pairs[0][1].shape[1]
    n_pairs = len(pairs)
    tm = _tile(M, 1024, 16)
    tn = _tile(N, 1024, LANES)
    tk = _k_tile(K, n_pairs * (tm + tn) * pairs[0][0].dtype.itemsize, LANES)
    nk = K // tk

    def body(*refs):
        ab = refs[:2 * n_pairs]
        res_ref = refs[2 * n_pairs] if res is not None else None
        o_ref = refs[2 * n_pairs + (res is not None)]

        def finish(acc):
            out = alpha * acc
            if res_ref is not None:
                out = res_ref[...] + out
            o_ref[...] = out.astype(o_ref.dtype)

        part = _dot(ab[0][...], ab[1][...])
        for i in range(1, n_pairs):
            part = part + _dot(ab[2 * i][...], ab[2 * i + 1][...])
        if nk == 1:
            finish(part)
        else:
            acc_ref = refs[-1]
            k = pl.program_id(2)

            @pl.when(k == 0)
            def _():
                acc_ref[...] = part

            @pl.when(k > 0)
            def _():
                acc_ref[...] += part

            @pl.when(k == nk - 1)
            def _():
                finish(acc_ref[...])

    in_specs, args = [], []
    for a, b in pairs:
        in_specs += [pl.BlockSpec((tm, tk), lambda m, n, k: (m, k)), pl.BlockSpec((tk, tn), lambda m, n, k: (k, n))]
        args += [a, b]
    if res is not None:
        in_specs.append(pl.BlockSpec((tm, tn), lambda m, n, k: (m, n)))
        args.append(res)
    return _call(
        body, name=name, grid=(M // tm, N // tn, nk), out_shape=jax.ShapeDtypeStruct((M, N), out_dtype),
        in_specs=in_specs, out_specs=pl.BlockSpec((tm, tn), lambda m, n, k: (m, n)),
        scratch_shapes=[pltpu.VMEM((tm, tn), F32)] if nk > 1 else [], args=args,
        semantics=("parallel", "parallel", "arbitrary"), jobs=jobs)


def _mm_tn(a, b, alpha, out_dtype, name, jobs=()):
    K, M = a.shape
    N = b.shape[1]
    tm = _tile(M, 512, LANES)
    tn = _tile(N, 1024, LANES)

    def body(a_ref, b_ref, o_ref):
        o_ref[...] = (alpha * _dot_tn(a_ref[...], b_ref[...])).astype(o_ref.dtype)

    return _call(
        body, name=name, grid=(N // tn, M // tm), out_shape=jax.ShapeDtypeStruct((M, N), out_dtype),
        in_specs=[pl.BlockSpec((K, tm), lambda n, m: (0, m)), pl.BlockSpec((K, tn), lambda n, m: (0, n))],
        out_specs=pl.BlockSpec((tm, tn), lambda n, m: (m, n)), args=(a, b),
        semantics=("parallel", "parallel"), jobs=jobs)


def _half_masks():
    lane = lax.broadcasted_iota(jnp.int32, (1, LANES), 1)
    return (lane < HEAD_DIM, lane >= HEAD_DIM)


def _swap_halves(v):
    return pltpu.roll(v.astype(F32), HEAD_DIM, 1).astype(v.dtype)


def _swa_geometry(n):
    qi = lax.broadcasted_iota(jnp.int32, (WINDOW, 2 * WINDOW), 0)
    kp = lax.broadcasted_iota(jnp.int32, (WINDOW, 2 * WINDOW), 1)
    dist = (WINDOW + qi) - kp
    valid = (dist >= 0) & (dist < WINDOW) & ((n > 0) | (kp >= WINDOW))
    return dist.astype(F32), valid


def _swa_slope(h):
    return 2.0 ** (-8.0 * (h + 1) / N_SWA_HEADS)


def _swa_softmax(qk, sink, slope, distf, valid):
    s = qk * ATT_SCALE - slope * distf
    s = jnp.where(valid, s, -1e30)
    m = jnp.maximum(jnp.max(s, axis=1, keepdims=True), sink)
    p = jnp.exp(s - m)
    e_sink = jnp.exp(sink - m)
    den = jnp.sum(p, axis=1, keepdims=True) + e_sink
    return p / den, e_sink / den


def _swa_group_heads(g):
    return [(2 * pp + a, pp, a) for pp in (2 * g, 2 * g + 1) for a in range(2)]


def _swa_specs(B, S):
    nb = S // WINDOW
    kcol = SWA_Q // SWA_KV
    cur = lambda b, n: (b * nb + n, kcol)
    prev = lambda b, n: (b * nb + jnp.maximum(n - 1, 0), kcol)
    curv = lambda b, n: (b * nb + n, kcol + 1)
    prevv = lambda b, n: (b * nb + jnp.maximum(n - 1, 0), kcol + 1)
    q_spec = pl.BlockSpec((WINDOW, SWA_Q), lambda b, n: (b * nb + n, 0))
    kv = [pl.BlockSpec((WINDOW, SWA_KV), f) for f in (prev, cur, prevv, curv)]
    sink_spec = pl.BlockSpec(memory_space=pltpu.SMEM)
    return nb, q_spec, kv, sink_spec


def _swa_kv_views(kp_ref, kc_ref, vp_ref, vc_ref, g):
    hm = _half_masks()
    c0 = (g // 2) * LANES
    k_all = jnp.concatenate([kp_ref[:, c0:c0 + LANES], kc_ref[:, c0:c0 + LANES]], axis=0)
    v_all = jnp.concatenate([vp_ref[:, c0:c0 + LANES], vc_ref[:, c0:c0 + LANES]], axis=0)
    b = g % 2
    ks, vs = [None, None], [None, None]
    ks[b], vs[b] = k_all, v_all
    ks[1 - b], vs[1 - b] = _swap_halves(k_all), _swap_halves(v_all)
    ks = [jnp.where(hm[a], ks[a], 0) for a in range(2)]
    vs = [jnp.where(hm[a], vs[a], 0) for a in range(2)]
    return ks, vs


def _swa_fwd(proj, sinks, B, S, name, jobs=()):
    T = B * S
    nb, q_spec, kv_specs, sink_spec = _swa_specs(B, S)

    def body(sink_ref, q_ref, kp_ref, kc_ref, vp_ref, vc_ref, y_ref):
        hm = _half_masks()
        distf, valid = _swa_geometry(pl.program_id(1))
        for g in range(N_SWA_KV):
            ks, vs = _swa_kv_views(kp_ref, kc_ref, vp_ref, vc_ref, g)
            heads = _swa_group_heads(g)
            qk = [_dot_nt(jnp.where(hm[a], q_ref[:, pp * LANES:(pp + 1) * LANES], 0), ks[a]) for _, pp, a in heads]
            p = [_swa_softmax(qk[i], sink_ref[0, h], _swa_slope(h), distf, valid)[0] for i, (h, _, _) in enumerate(heads)]
            o = [_dot(p[i].astype(_MXU_DTYPE), vs[a]) for i, (_, _, a) in enumerate(heads)]
            for j, pp in enumerate((2 * g, 2 * g + 1)):
                y_ref[:, pp * LANES:(pp + 1) * LANES] = o[2 * j] + o[2 * j + 1]

    return _call(
        body, name=name, grid=(B, nb), out_shape=jax.ShapeDtypeStruct((T, SWA_Q), F32),
        in_specs=[sink_spec, q_spec] + kv_specs,
        out_specs=pl.BlockSpec((WINDOW, SWA_Q), lambda b, n: (b * nb + n, 0)),
        args=(sinks, proj, proj, proj, proj, proj), semantics=("parallel", "parallel"), jobs=jobs)


def _swa_bwd(proj, sinks, dya, B, S, name, jobs=()):
    T = B * S
    nb, q_spec, kv_specs, sink_spec = _swa_specs(B, S)

    def body(sink_ref, q_ref, kp_ref, kc_ref, vp_ref, vc_ref, do_ref,
             dq_ref, dk_ref, dv_ref, dsink_ref, dk_acc, dv_acc):
        b_id, n = pl.program_id(0), pl.program_id(1)
        hm = _half_masks()
        lane = lax.broadcasted_iota(jnp.int32, (1, LANES), 1)

        @pl.when((b_id == 0) & (n == 0))
        def _():
            dsink_ref[...] = jnp.zeros_like(dsink_ref)

        @pl.when(n == 0)
        def _():
            dk_acc[...] = jnp.zeros_like(dk_acc)
            dv_acc[...] = jnp.zeros_like(dv_acc)

        distf, valid = _swa_geometry(n)
        r_prev = pl.multiple_of(jnp.maximum(n - 1, 0) * WINDOW, WINDOW)
        r_cur = pl.multiple_of(n * WINDOW, WINDOW)
        dsink = jnp.zeros((1, LANES), F32)
        for g in range(N_SWA_KV):
            ks, vs = _swa_kv_views(kp_ref, kc_ref, vp_ref, vc_ref, g)
            heads = _swa_group_heads(g)
            four = range(len(heads))
            qms = [jnp.where(hm[a], q_ref[:, pp * LANES:(pp + 1) * LANES], 0) for _, pp, a in heads]
            doms = [jnp.where(hm[a], do_ref[:, pp * LANES:(pp + 1) * LANES], 0) for _, pp, a in heads]
            qk = [_dot_nt(qms[i], ks[heads[i][2]]) for i in four]
            dp = [_dot_nt(doms[i], vs[heads[i][2]]) for i in four]
            soft = [_swa_softmax(qk[i], sink_ref[0, heads[i][0]], _swa_slope(heads[i][0]), distf, valid) for i in four]
            p = [soft[i][0] for i in four]
            delta = [jnp.sum(p[i] * dp[i], axis=1, keepdims=True) for i in four]
            ds = [(p[i] * (dp[i] - delta[i]) * ATT_SCALE).astype(_MXU_DTYPE) for i in four]
            for i in four:
                dsink = dsink + jnp.where(lane == heads[i][0], -jnp.sum(soft[i][1] * delta[i]), 0.0)
            dq = [_dot(ds[i], ks[heads[i][2]]) for i in four]
            dk_h = [_dot_tn(ds[i], qms[i]) for i in four]
            dv_h = [_dot_tn(p[i].astype(_MXU_DTYPE), doms[i]) for i in four]
            for j, pp in enumerate((2 * g, 2 * g + 1)):
                dq_ref[:, pp * LANES:(pp + 1) * LANES] = (dq[2 * j] + dq[2 * j + 1]).astype(dq_ref.dtype)
            dk_g = [dk_h[a] + dk_h[2 + a] for a in range(2)]
            dv_g = [dv_h[a] + dv_h[2 + a] for a in range(2)]
            bsel = g % 2
            dk_t = dk_g[bsel] + pltpu.roll(dk_g[1 - bsel], HEAD_DIM, 1)
            dv_t = dv_g[bsel] + pltpu.roll(dv_g[1 - bsel], HEAD_DIM, 1)
            c0 = (g // 2) * LANES
            dk_acc[pl.ds(r_prev, WINDOW), c0:c0 + LANES] += dk_t[:WINDOW]
            dk_acc[pl.ds(r_cur, WINDOW), c0:c0 + LANES] += dk_t[WINDOW:]
            dv_acc[pl.ds(r_prev, WINDOW), c0:c0 + LANES] += dv_t[:WINDOW]
            dv_acc[pl.ds(r_cur, WINDOW), c0:c0 + LANES] += dv_t[WINDOW:]
        dsink_ref[...] += dsink

        @pl.when(n == nb - 1)
        def _():
            dk_ref[...] = dk_acc[...].astype(dk_ref.dtype)
            dv_ref[...] = dv_acc[...].astype(dv_ref.dtype)

    seq_kv = pl.BlockSpec((S, SWA_KV), lambda b, n: (b, 0))
    return _call(
        body, name=name, grid=(B, nb),
        out_shape=(jax.ShapeDtypeStruct((T, SWA_Q), _MXU_DTYPE), jax.ShapeDtypeStruct((T, SWA_KV), _MXU_DTYPE),
                   jax.ShapeDtypeStruct((T, SWA_KV), _MXU_DTYPE), jax.ShapeDtypeStruct((1, LANES), F32)),
        in_specs=[sink_spec, q_spec] + kv_specs + [pl.BlockSpec((WINDOW, SWA_Q), lambda b, n: (b * nb + n, 0))],
        out_specs=(pl.BlockSpec((WINDOW, SWA_Q), lambda b, n: (b * nb + n, 0)), seq_kv, seq_kv,
                   pl.BlockSpec((1, LANES), lambda b, n: (0, 0))),
        scratch_shapes=[pltpu.VMEM((S, SWA_KV), F32), pltpu.VMEM((S, SWA_KV), F32)],
        args=(sinks, proj, proj, proj, proj, proj, dya), semantics=("arbitrary", "arbitrary"), jobs=jobs)


SB_TILE = 256
SB_HALF = 128
SB_DEAD = -105.0


def _mark_lanes():
    lane = lax.broadcasted_iota(jnp.int32, (1, LANES), 1)
    return (lane == HEAD_DIM - 1) | (lane == LANES - 1)


def _tri2(cond):
    j = lax.broadcasted_iota(jnp.int32, (2 * SB_HALF, SB_HALF), 0) & (SB_HALF - 1)
    s = lax.broadcasted_iota(jnp.int32, (2 * SB_HALF, SB_HALF), 1)
    return cond(j, s).astype(_MXU_DTYPE)


def _half_cumsums(x, tri2):
    out = []
    for h in range(2):
        xh = x[:, h * SB_HALF:(h + 1) * SB_HALF]
        hi = xh.astype(_MXU_DTYPE)
        lo = (xh - hi.astype(F32)).astype(_MXU_DTYPE)
        out.append(_dot(jnp.concatenate([hi, lo], axis=1), tri2))
    return out


def _log_sigmoid(z):
    return jnp.minimum(z, 0.0) - jnp.log(1.0 + jnp.exp(-jnp.abs(z)))


def _sb_specs(B, S):
    qb = (SWA_Q + 2 * SWA_KV) // LANES
    kb = qb + SB_W // LANES
    vb = kb + SB_W // LANES
    return [pl.BlockSpec((S, LANES), functools.partial(lambda b, p, c: (b, c + p), c=c)) for c in (qb, kb, vb)]


def _sb_fwd(proj, B, S, name, jobs=()):
    T = B * S
    tq = SB_TILE
    nq = S // tq

    def body(q_ref, k_ref, v_ref, y_ref, tot_ref):
        hm = _half_masks()
        ji = lax.broadcasted_iota(jnp.int32, (tq, tq), 0)
        si = lax.broadcasted_iota(jnp.int32, (tq, tq), 1)
        tri_after = _tri2(lambda j, s: j > s)
        causal = si < ji
        mark = _mark_lanes()

        def q_block(qi, with_previous):
            r0 = qi * tq if isinstance(qi, int) else pl.multiple_of(qi * tq, tq)
            q_pair = q_ref[pl.ds(r0, tq), :] * ATT_SCALE
            qms = [jnp.where(hm[a], q_pair, 0) for a in range(2)]

            def tiles(blocks, state):
                two, nb = range(2), range(len(blocks))
                kk = [k_ref[pl.ds(c0, tq), :] for c0, _ in blocks]
                vv = [v_ref[pl.ds(c0, tq), :] for c0, _ in blocks]
                z = [[_dot_nt(qms[a], kk[b]) for a in two] for b in nb]
                lb = [[_log_sigmoid(z[b][a]) for a in two] for b in nb]
                l1m = [[jnp.where(causal, lb[b][a] - z[b][a], 0.0) if blocks[b][1] else lb[b][a] - z[b][a]
                        for a in two] for b in nb]
                cum = [[_half_cumsums(l1m[b][a], tri_after) for a in two] for b in nb]
                tot = [[[cum[b][a][h][:, 0:1] + l1m[b][a][:, h * SB_HALF:h * SB_HALF + 1] for h in two]
                        for a in two] for b in nb]
                car = [[state[a][1] for a in two]]
                for b in nb:
                    car.append([car[b][a] + (tot[b][a][0] + tot[b][a][1]) for a in two])
                after = [[jnp.concatenate([cum[b][a][0] + (car[b][a] + tot[b][a][1]), cum[b][a][1] + car[b][a]], axis=1)
                          for a in two] for b in nb]
                att = [[jnp.exp(lb[b][a] + after[b][a]) for a in two] for b in nb]
                att = [[jnp.where(causal, att[b][a], 0.0) if blocks[b][1] else att[b][a] for a in two] for b in nb]
                acc = [state[a][0] for a in two]
                for b in nb:
                    acc = [acc[a] + _dot(att[b][a].astype(_MXU_DTYPE), jnp.where(hm[a], vv[b], 0)) for a in two]
                return tuple((acc[a], car[-1][a]) for a in two)

            def live(st):
                return jnp.maximum(jnp.max(st[0][1]), jnp.max(st[1][1])) > SB_DEAD

            def step(c):
                it, _, st = c
                st = tiles([(pl.multiple_of((qi - 1 - it) * tq, tq), False)], st)
                return it + 1, live(st), st

            zero = (jnp.zeros((tq, LANES), F32), jnp.zeros((tq, 1), F32))
            if with_previous:
                state = tiles([(r0, True), (pl.multiple_of(r0 - tq, tq), False)], (zero, zero))
                done, _, state = lax.while_loop(lambda c: (c[0] < qi) & c[1], step, (jnp.int32(1), live(state), state))
            else:
                state, done = tiles([(r0, True)], (zero, zero)), 0
            y_ref[pl.ds(r0, tq), :] = state[0][0] + state[1][0]
            first = jnp.asarray(qi - done, F32)
            tot_ref[pl.ds(r0, tq), :] = jnp.where(mark, first, jnp.where(hm[0], state[0][1], state[1][1]))

        q_block(0, False)

        def q_loop(qi, carry):
            q_block(qi, True)
            return carry

        lax.fori_loop(1, nq, q_loop, 0)

    out_spec = pl.BlockSpec((S, LANES), lambda b, p: (b, p))
    return _call(
        body, name=name, grid=(B, SB_W // LANES),
        out_shape=(jax.ShapeDtypeStruct((T, SB_W), F32), jax.ShapeDtypeStruct((T, SB_W), F32)),
        in_specs=_sb_specs(B, S), out_specs=(out_spec, out_spec), args=(proj, proj, proj),
        semantics=("parallel", "parallel"), jobs=jobs)


def _sb_bwd(proj, tot, dyb, B, S, name, jobs=()):
    T = B * S
    tq = SB_TILE
    nq = S // tq

    def body(q_ref, k_ref, v_ref, do_ref, tot_ref, dq_ref, dk_ref, dv_ref, dk_acc, dv_acc):
        hm = _half_masks()
        ji = lax.broadcasted_iota(jnp.int32, (tq, tq), 0)
        si = lax.broadcasted_iota(jnp.int32, (tq, tq), 1)
        tri_incl = _tri2(lambda j, s: j <= s)
        tri_excl = _tri2(lambda j, s: j < s)
        causal = si < ji
        mark = _mark_lanes()
        dk_acc[...] = jnp.zeros_like(dk_acc)
        dv_acc[...] = jnp.zeros_like(dv_acc)

        def q_block(qi, with_previous):
            r0 = qi * tq if isinstance(qi, int) else pl.multiple_of(qi * tq, tq)
            q_pair = q_ref[pl.ds(r0, tq), :] * ATT_SCALE
            do_pair = do_ref[pl.ds(r0, tq), :]
            tot_pair = tot_ref[pl.ds(r0, tq), :]
            qms = [jnp.where(hm[a], q_pair, 0) for a in range(2)]
            doms = [jnp.where(hm[a], do_pair, 0) for a in range(2)]
            totals = [jnp.max(jnp.where(hm[a] & ~mark, tot_pair, -jnp.inf), axis=1, keepdims=True) for a in range(2)]

            def tiles(blocks, state):
                two, nb = range(2), range(len(blocks))
                last = SB_HALF - 1
                kk = [k_ref[pl.ds(c0, tq), :] for c0, _ in blocks]
                vv = [v_ref[pl.ds(c0, tq), :] for c0, _ in blocks]
                z = [[_dot_nt(qms[a], kk[b]) for a in two] for b in nb]
                d_att = [[_dot_nt(doms[a], vv[b]) for a in two] for b in nb]
                lb = [[_log_sigmoid(z[b][a]) for a in two] for b in nb]
                l1m = [[jnp.where(causal, lb[b][a] - z[b][a], 0.0) if blocks[b][1] else lb[b][a] - z[b][a]
                        for a in two] for b in nb]
                cum = [[_half_cumsums(l1m[b][a], tri_incl) for a in two] for b in nb]
                cp, upto = [[state[a][1] for a in two]], []
                for b in nb:
                    upto.append([jnp.concatenate(
                        [cum[b][a][0] + cp[b][a], cum[b][a][1] + (cp[b][a] + cum[b][a][0][:, last:last + 1])], axis=1)
                        for a in two])
                    cp.append([upto[b][a][:, tq - 1:tq] for a in two])
                att = [[jnp.exp(lb[b][a] + (totals[a] - upto[b][a])) for a in two] for b in nb]
                att = [[jnp.where(causal, att[b][a], 0.0) if blocks[b][1] else att[b][a] for a in two] for b in nb]
                d_log = [[d_att[b][a] * att[b][a] for a in two] for b in nb]
                cumd = [[_half_cumsums(d_log[b][a], tri_excl) for a in two] for b in nb]
                totd = [[[cumd[b][a][h][:, last:last + 1] + d_log[b][a][:, h * SB_HALF + last:h * SB_HALF + last + 1]
                          for h in two] for a in two] for b in nb]
                cq = [[state[a][2] for a in two]]
                for b in nb:
                    cq.append([cq[b][a] + (totd[b][a][0] + totd[b][a][1]) for a in two])
                before = [[jnp.concatenate([cumd[b][a][0] + cq[b][a], cumd[b][a][1] + (cq[b][a] + totd[b][a][0])], axis=1)
                           for a in two] for b in nb]
                sig = [[jnp.exp(lb[b][a]) for a in two] for b in nb]
                dz = [[d_log[b][a] * (1.0 - sig[b][a]) - sig[b][a] * before[b][a] for a in two] for b in nb]
                dz = [[jnp.where(causal, dz[b][a], 0.0) if blocks[b][1] else dz[b][a] for a in two] for b in nb]
                dzb = [[dz[b][a].astype(_MXU_DTYPE) for a in two] for b in nb]
                dq = [state[a][0] for a in two]
                for b, (c0, _) in enumerate(blocks):
                    ks = kk[b] * ATT_SCALE
                    dq = [dq[a] + _dot(dzb[b][a], jnp.where(hm[a], ks, 0)) for a in two]
                    dk_acc[pl.ds(c0, tq), :] += _dot_tn(dzb[b][0], qms[0]) + _dot_tn(dzb[b][1], qms[1])
                    dv_acc[pl.ds(c0, tq), :] += (_dot_tn(att[b][0].astype(_MXU_DTYPE), doms[0])
                                                 + _dot_tn(att[b][1].astype(_MXU_DTYPE), doms[1]))
                return tuple((dq[a], cp[-1][a], cq[-1][a]) for a in two)

            zero_col = jnp.zeros((tq, 1), F32)
            zero = (jnp.zeros((tq, LANES), F32), zero_col, zero_col)
            if with_previous:
                first = jnp.max(jnp.where(mark, tot_pair, -jnp.inf))
                first = jnp.where((first >= 0.0) & (first <= (qi - 1).astype(F32)), first, 0.0).astype(jnp.int32)
                state = lax.fori_loop(first, qi - 1,
                                      lambda kj, st: tiles([(pl.multiple_of(kj * tq, tq), False)], st), (zero, zero))
                state = tiles([(pl.multiple_of(r0 - tq, tq), False), (r0, True)], state)
            else:
                state = tiles([(r0, True)], (zero, zero))
            dq_ref[pl.ds(r0, tq), :] = (state[0][0] + state[1][0]).astype(dq_ref.dtype)

        q_block(0, False)

        def q_loop(qi, carry):
            q_block(qi, True)
            return carry

        lax.fori_loop(1, nq, q_loop, 0)
        dk_ref[...] = dk_acc[...].astype(dk_ref.dtype)
        dv_ref[...] = dv_acc[...].astype(dv_ref.dtype)

    pair = pl.BlockSpec((S, LANES), lambda b, p: (b, p))
    out = jax.ShapeDtypeStruct((T, SB_W), _MXU_DTYPE)
    return _call(
        body, name=name, grid=(B, SB_W // LANES), out_shape=(out, out, out),
        in_specs=_sb_specs(B, S) + [pair, pair], out_specs=(pair, pair, pair),
        scratch_shapes=[pltpu.VMEM((S, LANES), F32), pltpu.VMEM((S, LANES), F32)],
        args=(proj, proj, proj, dyb, tot), semantics=("parallel", "parallel"), jobs=jobs)


def _layer_step(x, tgt, B, S, small, comm):
    run, big, part = comm.run, comm.big, comm.partial
    ffn1_w, ffn2_w = ("ffn1_down", "ffn1_gate", "ffn1_up"), ("ffn2_down", "ffn2_gate", "ffn2_up")

    h1 = run(_rms_fwd, x, small["ffn1_norm"], "ffn1_rms", ag=(FIRST_GATHERED,),
             cast=tuple(n for n in BIG_NAMES if n != FIRST_GATHERED))
    U1 = run(_mm_nt, h1, big["ffn1_up"], _MXU_DTYPE, "ffn1_up", ag=("ffn1_gate",))
    G1, A1 = run(_ffn_gate, h1, big["ffn1_gate"], U1, "ffn1_gate", ag=("ffn1_down",))
    x1 = run(_mm_nn, [(A1, big["ffn1_down"])], x, 0.5, F32, "ffn1_down", ag=("w_in",))
    h2 = run(_rms_fwd, x1, small["mix_norm"], "mix_rms")
    proj = run(_mm_nt, h2, big["w_in"], _MXU_DTYPE, "in_proj", ag=("w_out",), ag_early=AG_PARTS)
    ya = run(_swa_fwd, proj, small["swa_sinks"], B, S, "swa_fwd", ag=("ffn2_gate",))
    yb, tot = run(_sb_fwd, proj, B, S, "sb_fwd", ag=("ffn2_up",))
    yn = _outnorm_fwd(ya, yb, small["swa_out_norm"], small["sb_out_norm"], "out_norm")
    x2 = run(_mm_nn, [(yn, big["w_out"])], x1, 1.0, F32, "out_proj")
    h3 = run(_rms_fwd, x2, small["ffn2_norm"], "ffn2_rms")
    G2, U2, A2 = run(_ffn_gu, h3, big["ffn2_gate"], big["ffn2_up"], "ffn2_gate_up", ag=("ffn2_down",),
                     ag_early=AG_PARTS)
    x3 = run(_mm_nn, [(A2, big["ffn2_down"])], x2, 0.5, F32, "ffn2_down")

    dx3, dx3b, d_final, loss = _loss_head(x3, small["final_norm"], tgt, "loss_head")

    dG2, dU2 = run(_ffn_bwd_act, dx3b, big["ffn2_down"], G2, U2, "ffn2_bwd_act")
    part["ffn2_down"] = run(_mm_tn, A2, dx3b, 0.5, _WIRE_DTYPE, "ffn2_dw_down")
    part["ffn2_gate"] = run(_mm_tn, dG2, h3, 1.0, _WIRE_DTYPE, "ffn2_dw_gate")
    part["ffn2_up"] = run(_mm_tn, dU2, h3, 1.0, _WIRE_DTYPE, "ffn2_dw_up")
    dh3 = run(_mm_nn, [(dG2, big["ffn2_gate"])], None, 1.0, F32, "ffn2_dh_gate", rs1=ffn2_w)
    dh3 = run(_mm_nn, [(dU2, big["ffn2_up"])], dh3, 1.0, F32, "ffn2_dh_up")
    dx2, dx2b, d_g2 = _rms_bwd(dh3, x2, small["ffn2_norm"], dx3, "ffn2_rms_bwd")

    part["w_out"] = run(_mm_tn, yn, dx2b, 1.0, _WIRE_DTYPE, "dw_out")
    dyn = run(_mm_nt, dx2b, big["w_out"], F32, "out_proj_bwd")
    dya, dyb, d_ga, d_gb = _outnorm_bwd(dyn, ya, yb, small["swa_out_norm"], small["sb_out_norm"], "out_norm_bwd")
    dqa, dka, dva, d_sinks = run(_swa_bwd, proj, small["swa_sinks"], dya, B, S, "swa_bwd", rs2=ffn2_w[:1])
    dqb, dkb, dvb = run(_sb_bwd, proj, tot, dyb, B, S, "sb_bwd", rs2=ffn2_w[1:])
    dproj = jnp.concatenate([dqa, dka, dva, dqb, dkb, dvb], axis=1)
    part["w_in"] = run(_mm_tn, dproj, h2, 1.0, _WIRE_DTYPE, "dw_in", adamw=("ffn2_down", "ffn2_gate"))
    dh2 = run(_mm_nn, [(dproj, big["w_in"])], None, 1.0, F32, "in_proj_bwd", rs1=("w_in", "w_out"))
    dx1, dx1b, d_gm = _rms_bwd(dh2, x1, small["mix_norm"], dx2, "mix_rms_bwd")

    dG1, dU1 = run(_ffn_bwd_act, dx1b, big["ffn1_down"], G1, U1, "ffn1_bwd_act", rs2=("w_in", "w_out"))
    part["ffn1_down"] = run(_mm_tn, A1, dx1b, 0.5, _WIRE_DTYPE, "ffn1_dw_down", adamw=("ffn2_up", "w_in", "w_out"))
    part["ffn1_gate"] = run(_mm_tn, dG1, h1, 1.0, _WIRE_DTYPE, "ffn1_dw_gate", rs1=("ffn1_down",))
    part["ffn1_up"] = run(_mm_tn, dU1, h1, 1.0, _WIRE_DTYPE, "ffn1_dw_up", rs1=("ffn1_gate",), rs2=("ffn1_down",))
    dh1 = run(_mm_nn, [(dG1, big["ffn1_gate"])], None, 1.0, F32, "ffn1_dh_gate", rs1=("ffn1_up",), rs2=("ffn1_gate",))
    dh1 = run(_mm_nn, [(dU1, big["ffn1_up"])], dh1, 1.0, F32, "ffn1_dh_up", rs2=("ffn1_up",))
    gx, _, d_g1 = _rms_bwd(dh1, x, small["ffn1_norm"], dx1, "ffn1_rms_bwd")

    d_small = {"ffn1_norm": d_g1, "mix_norm": d_gm, "swa_sinks": d_sinks[:, :N_SWA_HEADS], "swa_out_norm": d_ga,
               "sb_out_norm": d_gb, "ffn2_norm": d_g2, "final_norm": d_final}
    return loss, gx, d_small


MESH = pl.DeviceIdType.MESH
BIG_NAMES = ("ffn1_gate", "ffn1_up", "ffn1_down", "w_in", "w_out", "ffn2_gate", "ffn2_up", "ffn2_down")
FIRST_GATHERED = "ffn1_up"
_COMM_PARAMS = pltpu.CompilerParams(has_side_effects=True)


def _place():
    x, y, c = lax.axis_index("x"), lax.axis_index("y"), lax.axis_index("c")
    other_chips = [(1 - x, y), (x, 1 - y), (1 - x, 1 - y)]
    return x, y, c, other_chips


def _padded_rows(rows):
    full = N_DEV * rows
    return -(-full // _F_TILE) * _F_TILE


AG_PARTS = 4


def _row_parts(rows, n):
    units = rows // 16
    assert units * 16 == rows and units >= n
    out, off = [], 0
    for i in range(n):
        size = (units // n + (1 if i < units % n else 0)) * 16
        out.append((off, size))
        off += size
    return out


def _ag_job(shards, early_parts=AG_PARTS // 2):
    nw = len(shards)
    D = shards[0].shape[1]
    rows_w = [s.shape[0] for s in shards]
    full_w = [_padded_rows(r) for r in rows_w]
    pad_w = [f - N_DEV * r for f, r in zip(full_w, rows_w)]
    max_pad = max(max(pad_w), 16)
    n_parts = AG_PARTS
    parts_w = [_row_parts(r, n_parts) for r in rows_w]

    class Plan:
        def __init__(self, ins, outs, scratch):
            zbuf, send_sems, recv_sems, local_sems, zero_sems = scratch
            x, y, c, chips = _place()
            me, sibling = (x, y, c), (x, y, 1 - c)

            def rows(w, block, part=None):
                off, size = (0, rows_w[w]) if part is None else part
                px, py, pc = block
                start = pl.multiple_of((4 * px + 2 * py + pc) * rows_w[w] + off, 16)
                return outs[w].at[pl.ds(start, size), :]

            def copy(w, k, block, to, part=None, own=False):
                src = rows(w, block, part)
                if own:
                    src = ins[w] if part is None else ins[w].at[pl.ds(part[0], part[1]), :]
                return pltpu.make_async_remote_copy(
                    src_ref=src, dst_ref=rows(w, block, part), send_sem=send_sems.at[w, k],
                    recv_sem=recv_sems.at[w, k], device_id=to, device_id_type=MESH)

            def k_ici(j, p):
                return 1 + j * n_parts + p

            def k_on(j, p):
                return 1 + (3 + j) * n_parts + p

            self.zbuf = zbuf
            self.local = [pltpu.make_async_copy(zbuf.at[pl.ds(0, pad_w[w]), :],
                                                outs[w].at[pl.ds(N_DEV * rows_w[w], pad_w[w]), :], zero_sems.at[w])
                          for w in range(nw) if pad_w[w]]
            self.local += [pltpu.make_async_copy(ins[w], rows(w, me), local_sems.at[w]) for w in range(nw)]
            self.first = [[copy(w, 0, me, sibling, own=True)]
                          + [copy(w, k_ici(j, p), me, (*chip, c), part, own=True)
                             for p, part in enumerate(parts_w[w]) for j, chip in enumerate(chips)]
                          for w in range(nw)]
            self.arrive = [[copy(w, k_ici(j, p), (*chip, c), me, part)
                            for p, part in enumerate(parts_w[w]) for j, chip in enumerate(chips)] for w in range(nw)]
            self.passed = [[copy(w, k_on(j, p), (*chip, c), sibling, part)
                            for p, part in enumerate(parts_w[w]) for j, chip in enumerate(chips)] for w in range(nw)]
            self.from_sibling = [[copy(w, 0, sibling, me)]
                                 + [copy(w, k_on(j, p), (*chip, 1 - c), me, part)
                                    for p, part in enumerate(parts_w[w]) for j, chip in enumerate(chips)]
                                 for w in range(nw)]

    def start(ins, outs, scratch):
        plan = Plan(ins, outs, scratch)
        plan.zbuf[...] = jnp.zeros_like(plan.zbuf)
        for cp in plan.local:
            cp.start()
        for w in range(nw):
            for cp in plan.first[w]:
                cp.start()

    def mid(ins, outs, scratch, phase):
        plan = Plan(ins, outs, scratch)
        early = 3 * early_parts
        for w in range(nw):
            pairs = list(zip(plan.arrive[w], plan.passed[w]))
            for arrived, onward in (pairs[:early] if phase == 0 else pairs[early:]):
                arrived.wait_recv()
                onward.start()

    def finish(ins, outs, scratch):
        plan = Plan(ins, outs, scratch)
        for w in range(nw):
            for cp in plan.from_sibling[w]:
                cp.wait_recv()
        for w in range(nw):
            for cp in plan.first[w] + plan.passed[w]:
                cp.wait_send()
        for cp in plan.local:
            cp.wait()

    return _Job(
        ins=shards, out_shape=[jax.ShapeDtypeStruct((f, D), s.dtype) for f, s in zip(full_w, shards)],
        scratch=[pltpu.VMEM((max_pad, D), shards[0].dtype), pltpu.SemaphoreType.DMA((nw, 1 + 6 * n_parts)),
                 pltpu.SemaphoreType.DMA((nw, 1 + 6 * n_parts)), pltpu.SemaphoreType.DMA((nw,)),
                 pltpu.SemaphoreType.DMA((nw,))],
        start=start, mid=mid, finish=finish)


def _rs1_job(partials, rows_w):
    nw = len(partials)
    D = partials[0].shape[1]

    def copies(ins, outs, scratch):
        send_sems, recv_sems = scratch
        x, y, c, _ = _place()
        out = []
        for w in range(nw):
            r = rows_w[w]
            for q in range(4):
                src = ins[w].at[pl.ds(pl.multiple_of((2 * q + 1 - c) * r, 16), r), :]
                out.append(pltpu.make_async_remote_copy(
                    src_ref=src, dst_ref=outs[w].at[pl.ds(q * r, r), :], send_sem=send_sems.at[w, q],
                    recv_sem=recv_sems.at[w, q], device_id=(x, y, 1 - c), device_id_type=MESH))
        return out

    def start(ins, outs, scratch):
        for cp in copies(ins, outs, scratch):
            cp.start()

    def finish(ins, outs, scratch):
        for cp in copies(ins, outs, scratch):
            cp.wait()

    return _Job(
        ins=partials, out_shape=[jax.ShapeDtypeStruct((4 * r, D), p.dtype) for r, p in zip(rows_w, partials)],
        scratch=[pltpu.SemaphoreType.DMA((nw, 4)), pltpu.SemaphoreType.DMA((nw, 4))], start=start, finish=finish)


def _pair_sum(partial, from_sibling, rows, core, name):
    D = partial.shape[1]

    def body(core_ref, p_ref, s_ref, o_ref):
        o_ref[...] = (p_ref[...].astype(F32) + s_ref[...].astype(F32)).astype(o_ref.dtype)

    grid_spec = pltpu.PrefetchScalarGridSpec(
        num_scalar_prefetch=1, grid=(4,),
        in_specs=[pl.BlockSpec((rows, D), lambda q, core_ref: (2 * q + core_ref[0], 0)),
                  pl.BlockSpec((rows, D), lambda q, core_ref: (q, 0))],
        out_specs=pl.BlockSpec((rows, D), lambda q, core_ref: (q, 0)))
    return pl.pallas_call(
        body, name=name, grid_spec=grid_spec, out_shape=jax.ShapeDtypeStruct((4 * rows, D), partial.dtype),
        compiler_params=_params("arbitrary"),
    )(core, partial, from_sibling)


def _rs2_job(chip_sums, rows_w):
    nw = len(chip_sums)

    def copies(ins, outs, scratch):
        send_sems, recv_sems, local_sems = scratch
        x, y, c, chips = _place()
        my_chip = 2 * x + y
        out = []
        for w in range(nw):
            r = rows_w[w]
            mine = pl.ds(pl.multiple_of(my_chip * r, 16), r)
            out.append(pltpu.make_async_copy(ins[w].at[mine, :], outs[w].at[mine, :], local_sems.at[w]))
            for j, (qx, qy) in enumerate(chips):
                src = ins[w].at[pl.ds(pl.multiple_of((2 * qx + qy) * r, 16), r), :]
                out.append(pltpu.make_async_remote_copy(
                    src_ref=src, dst_ref=outs[w].at[mine, :], send_sem=send_sems.at[w, j],
                    recv_sem=recv_sems.at[w, j], device_id=(qx, qy, c), device_id_type=MESH))
        return out

    def start(ins, outs, scratch):
        for cp in copies(ins, outs, scratch):
            cp.start()

    def finish(ins, outs, scratch):
        for cp in copies(ins, outs, scratch):
            cp.wait()

    return _Job(
        ins=chip_sums, out_shape=[jax.ShapeDtypeStruct(s.shape, s.dtype) for s in chip_sums],
        scratch=[pltpu.SemaphoreType.DMA((nw, 3)), pltpu.SemaphoreType.DMA((nw, 3)), pltpu.SemaphoreType.DMA((nw,))],
        start=start, finish=finish)


class _Comm:
    def __init__(self, shards, state):
        self.shards, self.state = dict(shards), state
        self.rows = {n: st[0].shape[0] for n, st in state.items()}
        self.core = lax.axis_index("c").astype(jnp.int32).reshape(1)
        self.big, self.partial, self.chip_sums, self.slots, self.updates = {}, {}, {}, {}, {}

    def slots3(self, name):
        return self.slots[name].reshape(4, self.rows[name], -1)

    def run(self, fn, *args, ag=(), rs1=(), rs2=(), adamw=(), cast=(), ag_early=AG_PARTS // 2):
        jobs = []
        if cast:
            jobs.append(_cast_job([self.state[n][0] for n in cast], _WIRE_DTYPE))
        if ag:
            jobs.append(_ag_job([self.shards[n] for n in ag], ag_early))
        if rs1:
            jobs.append(_rs1_job([self.partial[n] for n in rs1], [self.rows[n] for n in rs1]))
        if rs2:
            jobs.append(_rs2_job([self.chip_sums[n] for n in rs2], [self.rows[n] for n in rs2]))
        for n in adamw:
            w2, m2, v2 = self.state[n]
            jobs.append(_adamw_job(w2, self.slots3(n), m2, v2))
        out, job_res = fn(*args, jobs=jobs)
        job_res = iter(job_res)
        if cast:
            self.shards.update(zip(cast, next(job_res)))
        if ag:
            self.big.update(zip(ag, next(job_res)))
        if rs1:
            for n, got in zip(rs1, next(job_res)):
                self.chip_sums[n] = _pair_sum(self.partial[n], got, self.rows[n], self.core, "pair_sum_" + n)
        if rs2:
            self.slots.update(zip(rs2, next(job_res)))
        for n in adamw:
            self.updates[n] = next(job_res)
        return out


SMALL_ROWS = 88


def _small_allreduce(vec):
    def body(v_ref, o_ref, gather, send_sems, recv_sems):
        x, y, c, _ = _place()
        my_id = 4 * x + 2 * y + c
        gather[my_id] = v_ref[...]
        copies = []
        for r in range(1, N_DEV):
            peer = (x ^ (r >> 2), y ^ ((r >> 1) & 1), c ^ (r & 1))
            cp = pltpu.make_async_remote_copy(src_ref=v_ref, dst_ref=gather.at[my_id], send_sem=send_sems.at[r - 1],
                                              recv_sem=recv_sems.at[r - 1], device_id=peer, device_id_type=MESH)
            cp.start()
            copies.append(cp)
        for cp in copies:
            cp.wait()
        acc = gather[0]
        for d in range(1, N_DEV):
            acc = acc + gather[d]
        o_ref[...] = acc

    vm = pl.BlockSpec(memory_space=pltpu.VMEM)
    return pl.pallas_call(
        body, name="small_allreduce", out_shape=jax.ShapeDtypeStruct(vec.shape, F32),
        in_specs=[vm], out_specs=vm,
        scratch_shapes=[pltpu.VMEM((N_DEV,) + vec.shape, F32), pltpu.SemaphoreType.DMA((N_DEV - 1,)),
                        pltpu.SemaphoreType.DMA((N_DEV - 1,))],
        compiler_params=_COMM_PARAMS,
    )(vec)


def _adamw_update(w, g, m, v):
    nm = ADAM_B1 * m + (1.0 - ADAM_B1) * g
    nv = ADAM_B2 * v + (1.0 - ADAM_B2) * jnp.square(g)
    m_hat = nm / (1.0 - ADAM_B1 ** ADAM_STEP)
    v_hat = nv / (1.0 - ADAM_B2 ** ADAM_STEP)
    return -ADAM_LR * (m_hat / (jnp.sqrt(v_hat) + ADAM_EPS) + ADAM_WD * w), nm, nv


def _adamw(w, g, m, v, name):
    R, C = w.shape
    tr = _tile(R, 256, 8)

    def body(w_ref, g_ref, m_ref, v_ref, d_ref, nm_ref, nv_ref):
        d_ref[...], nm_ref[...], nv_ref[...] = _adamw_update(w_ref[...], g_ref[...], m_ref[...], v_ref[...])

    spec = pl.BlockSpec((tr, C), lambda i: (i, 0))
    out = jax.ShapeDtypeStruct((R, C), F32)
    return pl.pallas_call(
        body, name=name, grid=(R // tr,), out_shape=(out, out, out),
        in_specs=[spec] * 4, out_specs=(spec, spec, spec),
        compiler_params=_params("parallel"),
    )(w, g, m, v)


def _adamw_slots(w, slots, m, v, name):
    R, C = w.shape
    tc = _tile(C, 512, LANES)
    spec = pl.BlockSpec((R, tc), lambda j: (0, j))
    out = jax.ShapeDtypeStruct((R, C), F32)
    return pl.pallas_call(
        functools.partial(_adamw_slots_body), name=name, grid=(C // tc,), out_shape=(out, out, out, out),
        in_specs=[spec, pl.BlockSpec((4, R, tc), lambda j: (0, 0, j)), spec, spec], out_specs=(spec, spec, spec, spec),
        compiler_params=_params("parallel"),
    )(w, slots, m, v)


def _adamw_slots_body(w_ref, s_ref, m_ref, v_ref, g_ref, d_ref, nm_ref, nv_ref):
    g = s_ref[0].astype(F32)
    for q in range(1, 4):
        g = g + s_ref[q].astype(F32)
    g_ref[...] = g
    d_ref[...], nm_ref[...], nv_ref[...] = _adamw_update(w_ref[...], g, m_ref[...], v_ref[...])


def _cast_job(arrays, dtype):
    C = arrays[0].shape[1]

    def specs(grid):
        total = 1
        for g in grid:
            total *= g
        tc = C // total
        assert tc * total == C and tc % LANES == 0, (C, grid)
        blocks = [pl.BlockSpec((a.shape[0], tc), lambda *ids: (0, _linear_step(grid, ids))) for a in arrays]
        return blocks, list(blocks)

    def each(ins, outs, scratch, step):
        for i_ref, o_ref in zip(ins, outs):
            o_ref[...] = i_ref[...].astype(o_ref.dtype)

    return _Job(ins=arrays, out_shape=[jax.ShapeDtypeStruct(a.shape, dtype) for a in arrays], specs=specs, each=each)


def _adamw_job(w, slots, m, v):
    R, C = w.shape

    n_slices = C // LANES

    def specs(grid):
        total = 1
        for g in grid:
            total *= g
        assert total >= n_slices, (grid, n_slices)
        col = lambda *ids: jnp.minimum(_linear_step(grid, ids), n_slices - 1)
        blk = pl.BlockSpec((R, LANES), lambda *ids: (0, col(*ids)))
        slot_blk = pl.BlockSpec((4, R, LANES), lambda *ids: (0, 0, col(*ids)))
        return [blk, slot_blk, blk, blk], [blk] * 4

    def each(ins, outs, scratch, step):
        @pl.when(step < n_slices)
        def _():
            _adamw_slots_body(*ins, *outs)

    out = jax.ShapeDtypeStruct((R, C), F32)
    return _Job(ins=[w, slots, m, v], out_shape=[out] * 4, specs=specs, each=each)


WEIGHT_NAMES = ("ffn1_norm", "ffn1_w_gate", "ffn1_w_up", "ffn1_w_down", "mix_norm", "w_in", "swa_sinks",
                "swa_out_norm", "sb_out_norm", "w_out", "ffn2_norm", "ffn2_w_gate", "ffn2_w_up", "ffn2_w_down",
                "final_norm")
SMALL_NAMES = ("ffn1_norm", "mix_norm", "swa_sinks", "swa_out_norm", "sb_out_norm", "ffn2_norm", "final_norm")
BIG_ARGS = {"ffn1_gate": ("ffn1_w_gate", True), "ffn1_up": ("ffn1_w_up", True), "ffn1_down": ("ffn1_w_down", False),
            "w_in": ("w_in", True), "w_out": ("w_out", False), "ffn2_gate": ("ffn2_w_gate", True),
            "ffn2_up": ("ffn2_w_up", True), "ffn2_down": ("ffn2_w_down", False)}


def _pack_small(parts):
    padded = [jnp.pad(p.reshape(1, -1), ((0, 0), (0, -p.size % LANES))) for p in parts]
    flat = jnp.concatenate(padded, axis=1)
    flat = jnp.pad(flat, ((0, 0), (0, SMALL_ROWS * LANES - flat.shape[1])))
    return flat.reshape(SMALL_ROWS, LANES)


def _unpack_small(block, shapes):
    flat = block.reshape(-1)
    out, off = [], 0
    for shp in shapes:
        n = 1
        for s in shp:
            n *= s
        out.append(flat[off:off + n].reshape(shp))
        off += n + (-n % LANES)
    return out


def kernel(x, ffn1_norm, ffn1_w_gate, ffn1_w_up, ffn1_w_down, mix_norm, w_in, swa_sinks, swa_out_norm, sb_out_norm, w_out, ffn2_norm, ffn2_w_gate, ffn2_w_up, ffn2_w_down, final_norm, loss_target, m_ffn1_norm, m_ffn1_w_gate, m_ffn1_w_up, m_ffn1_w_down, m_mix_norm, m_w_in, m_swa_sinks, m_swa_out_norm, m_sb_out_norm, m_w_out, m_ffn2_norm, m_ffn2_w_gate, m_ffn2_w_up, m_ffn2_w_down, m_final_norm, v_ffn1_norm, v_ffn1_w_gate, v_ffn1_w_up, v_ffn1_w_down, v_mix_norm, v_w_in, v_swa_sinks, v_swa_out_norm, v_sb_out_norm, v_w_out, v_ffn2_norm, v_ffn2_w_gate, v_ffn2_w_up, v_ffn2_w_down, v_final_norm):
    args = dict(locals())
    B, S, D = x.shape
    T = B * S
    weights = {n: args[n] for n in WEIGHT_NAMES}
    mom_m = {n: args["m_" + n] for n in WEIGHT_NAMES}
    mom_v = {n: args["v_" + n] for n in WEIGHT_NAMES}

    state = {}
    for name in BIG_NAMES:
        arg, transposed = BIG_ARGS[name]
        to_rows = (lambda t: t[0].T) if transposed else (lambda t: t[0])
        state[name] = tuple(to_rows(t[arg]) for t in (weights, mom_m, mom_v))
    comm = _Comm({FIRST_GATHERED: state[FIRST_GATHERED][0].astype(_WIRE_DTYPE)}, state)
    small = {n: weights[n].reshape(1, -1) for n in SMALL_NAMES}

    loss, gx, d_small = _layer_step(x.reshape(T, D), loss_target.reshape(T, D), B, S, small, comm)

    small_shapes = [(1, 1)] + [d_small[n].shape for n in SMALL_NAMES]
    reduced = _small_allreduce(_pack_small([loss[:, :1]] + [d_small[n] for n in SMALL_NAMES]))
    red = _unpack_small(reduced, small_shapes)
    loss_out = red[0].reshape(())
    g_small = dict(zip(SMALL_NAMES, red[1:]))

    grads, deltas, new_m, new_v = {}, {}, {}, {}
    for name in BIG_NAMES:
        arg, transposed = BIG_ARGS[name]
        back = (lambda t: t.T[None]) if transposed else (lambda t: t[None])
        res = comm.updates.get(name)
        if res is None:
            w2, m2, v2 = state[name]
            res = _adamw_slots(w2, comm.slots3(name), m2, v2, "adamw_" + name)
        grads[arg], deltas[arg], new_m[arg], new_v[arg] = [back(t) for t in res]
    shapes1 = [(1, weights[n].size) for n in SMALL_NAMES]
    packed = [_pack_small([t[n].reshape(1, -1) for n in SMALL_NAMES]) for t in (weights, g_small, mom_m, mom_v)]
    upd = _adamw(*packed, "adamw_small")
    for tgt_dict, block in zip((deltas, new_m, new_v), upd):
        for n, val in zip(SMALL_NAMES, _unpack_small(block, shapes1)):
            tgt_dict[n] = val.reshape(weights[n].shape)
    for n in SMALL_NAMES:
        grads[n] = g_small[n].reshape(weights[n].shape)

    return (loss_out, gx.reshape(B, S, D), *[grads[n] for n in WEIGHT_NAMES], *[deltas[n] for n in WEIGHT_NAMES],
            *[new_m[n] for n in WEIGHT_NAMES], *[new_v[n] for n in WEIGHT_NAMES])
```

```python
import functools

import jax
import jax.numpy as jnp
from jax import lax
from jax.experimental import pallas as pl
from jax.experimental.pallas import tpu as pltpu

F32 = jnp.float32
_MXU_DTYPE = jnp.bfloat16
_WIRE_DTYPE = jnp.bfloat16

EPS = 1e-6
HEAD_DIM = 64
N_SWA_HEADS = 16
N_SWA_KV = 4
N_SB_HEADS = 16
WINDOW = 128
SWA_Q = N_SWA_HEADS * HEAD_DIM
SWA_KV = N_SWA_KV * HEAD_DIM
SB_W = N_SB_HEADS * HEAD_DIM
IN_W = SWA_Q + 2 * SWA_KV + 3 * SB_W
LANES = 128
ATT_SCALE = HEAD_DIM ** -0.5

ADAM_LR = 0.001
ADAM_B1 = 0.9
ADAM_B2 = 0.999
ADAM_EPS = 1e-08
ADAM_WD = 0.01
ADAM_STEP = 10

N_DEV = 8
_VMEM_LIMIT_BYTES = 56 * 1024 * 1024
_F_TILE = 512


def _params(*semantics):
    return pltpu.CompilerParams(dimension_semantics=semantics, vmem_limit_bytes=_VMEM_LIMIT_BYTES)


def _tile(n, pref, align):
    t = min(n, pref)
    t -= t % align
    while t >= align:
        if n % t == 0:
            return t
        t -= align
    return n


def _dot(a, b):
    return lax.dot_general(a, b, (((1,), (0,)), ((), ())), preferred_element_type=F32)


def _dot_nt(a, b):
    return lax.dot_general(a, b, (((1,), (1,)), ((), ())), preferred_element_type=F32)


def _dot_tn(a, b):
    return lax.dot_general(a, b, (((0,), (0,)), ((), ())), preferred_element_type=F32)


class _Job:
    def __init__(self, ins, out_shape, scratch=(), start=None, finish=None, mid=None, each=None, specs=None):
        self.ins, self.out_shape, self.scratch = list(ins), list(out_shape), list(scratch)
        self.start, self.mid, self.finish, self.each, self.specs = start, mid, finish, each, specs


_JOB_MID_FRACTION = 0.6


def _linear_step(grid, ids):
    step = ids[0]
    for d in range(1, len(grid)):
        step = step * grid[d] + ids[d]
    return step


def _call(body, *, name, grid, in_specs, out_specs, out_shape, args, semantics, scratch_shapes=(), jobs=()):
    single = not isinstance(out_shape, (tuple, list))
    if not jobs:
        res = pl.pallas_call(body, name=name, grid=grid, in_specs=list(in_specs), out_specs=out_specs,
                             out_shape=out_shape, scratch_shapes=list(scratch_shapes),
                             compiler_params=_params(*semantics))(*args)
        return res, []
    base_out = [out_shape] if single else list(out_shape)
    base_out_specs = [out_specs] if single else list(out_specs)
    n_in, n_out, n_scr = len(args), len(base_out), len(scratch_shapes)
    any_spec = pl.BlockSpec(memory_space=pl.ANY)
    total = 1
    for g in grid:
        total *= g
    mid_step = min(total - 1, int(total * _JOB_MID_FRACTION))

    def wrapped(*refs):
        pos = n_in
        job_ins = []
        for job in jobs:
            job_ins.append(refs[pos:pos + len(job.ins)])
            pos += len(job.ins)
        outs = refs[pos:pos + n_out]
        pos += n_out
        job_outs = []
        for job in jobs:
            job_outs.append(refs[pos:pos + len(job.out_shape)])
            pos += len(job.out_shape)
        scr = refs[pos:pos + n_scr]
        pos += n_scr
        job_scr = []
        for job in jobs:
            job_scr.append(refs[pos:pos + len(job.scratch)])
            pos += len(job.scratch)
        bound = list(zip(jobs, job_ins, job_outs, job_scr))
        step = _linear_step(grid, [pl.program_id(d) for d in range(len(grid))])

        @pl.when(step == 0)
        def _():
            for job, ji, jo, js in bound:
                if job.start is not None:
                    job.start(ji, jo, js)

        @pl.when(step == mid_step)
        def _():
            for job, ji, jo, js in bound:
                if job.mid is not None:
                    job.mid(ji, jo, js, 0)

        body(*refs[:n_in], *outs, *scr)
        for job, ji, jo, js in bound:
            if job.each is not None:
                job.each(ji, jo, js, step)

        @pl.when(step == total - 1)
        def _():
            for job, ji, jo, js in bound:
                if job.mid is not None:
                    job.mid(ji, jo, js, 1)
            for job, ji, jo, js in bound:
                if job.finish is not None:
                    job.finish(ji, jo, js)

    all_args, all_in_specs = list(args), list(in_specs)
    all_out_shape, all_out_specs = list(base_out), list(base_out_specs)
    for job in jobs:
        job_in_specs, job_out_specs = (job.specs(grid) if job.specs is not None else
                                       ([any_spec] * len(job.ins), [any_spec] * len(job.out_shape)))
        all_args += job.ins
        all_in_specs += job_in_specs
        all_out_shape += job.out_shape
        all_out_specs += job_out_specs
    all_scratch = list(scratch_shapes) + [s for job in jobs for s in job.scratch]
    res = pl.pallas_call(
        wrapped, name=name, grid=grid, in_specs=all_in_specs, out_specs=tuple(all_out_specs),
        out_shape=tuple(all_out_shape), scratch_shapes=all_scratch,
        compiler_params=pltpu.CompilerParams(dimension_semantics=("arbitrary",) * len(grid),
                                             vmem_limit_bytes=_VMEM_LIMIT_BYTES,
                                             has_side_effects=any(job.start is not None for job in jobs)),
    )(*all_args)
    base = res[0] if single else tuple(res[:n_out])
    job_res, pos = [], n_out
    for job in jobs:
        job_res.append(tuple(res[pos:pos + len(job.out_shape)]))
        pos += len(job.out_shape)
    return base, job_res


def _rms_fwd(x, g, name, jobs=()):
    T, D = x.shape
    tm = _tile(T, 512, 16)

    def body(x_ref, g_ref, o_ref):
        xv = x_ref[...]
        r = lax.rsqrt(jnp.mean(xv * xv, axis=-1, keepdims=True) + EPS)
        o_ref[...] = (xv * r * g_ref[...]).astype(o_ref.dtype)

    return _call(
        body, name=name, grid=(T // tm,),
        out_shape=jax.ShapeDtypeStruct((T, D), _MXU_DTYPE),
        in_specs=[pl.BlockSpec((tm, D), lambda i: (i, 0)), pl.BlockSpec((1, D), lambda i: (0, 0))],
        out_specs=pl.BlockSpec((tm, D), lambda i: (i, 0)), args=(x, g), semantics=("parallel",), jobs=jobs)


def _rms_bwd_rows(dh, xv, g):
    r = lax.rsqrt(jnp.mean(xv * xv, axis=-1, keepdims=True) + EPS)
    xhat = xv * r
    u = dh * g
    dx = r * (u - xhat * jnp.mean(u * xhat, axis=-1, keepdims=True))
    return dx, dh * xhat


def _rms_bwd(dh, x, g, dres, name):
    T, D = x.shape
    tm = _tile(T, 256, 16)

    def body(dh_ref, x_ref, g_ref, dres_ref, dx_ref, dxb_ref, dg_ref):
        @pl.when(pl.program_id(0) == 0)
        def _():
            dg_ref[...] = jnp.zeros_like(dg_ref)

        dx, dgr = _rms_bwd_rows(dh_ref[...], x_ref[...], g_ref[...])
        dx = dres_ref[...] + dx
        dx_ref[...] = dx
        dxb_ref[...] = dx.astype(dxb_ref.dtype)
        dg_ref[...] += jnp.sum(dgr, axis=0, keepdims=True)

    row = pl.BlockSpec((tm, D), lambda i: (i, 0))
    vec = pl.BlockSpec((1, D), lambda i: (0, 0))
    return pl.pallas_call(
        body, name=name, grid=(T // tm,),
        out_shape=(jax.ShapeDtypeStruct((T, D), F32), jax.ShapeDtypeStruct((T, D), _MXU_DTYPE),
                   jax.ShapeDtypeStruct((1, D), F32)),
        in_specs=[row, row, vec, row], out_specs=(row, row, vec),
        compiler_params=_params("arbitrary"),
    )(dh, x, g, dres)


def _loss_head(x, g, tgt, name):
    T, D = x.shape
    tm = _tile(T, 256, 16)

    def body(x_ref, g_ref, t_ref, dx_ref, dxb_ref, dg_ref, loss_ref):
        @pl.when(pl.program_id(0) == 0)
        def _():
            dg_ref[...] = jnp.zeros_like(dg_ref)
            loss_ref[...] = jnp.zeros_like(loss_ref)

        xv = x_ref[...]
        gv = g_ref[...]
        r = lax.rsqrt(jnp.mean(xv * xv, axis=-1, keepdims=True) + EPS)
        xhat = xv * r
        diff = xhat * gv - t_ref[...]
        tok = jnp.mean(diff * diff, axis=-1, keepdims=True)
        loss_ref[...] += 0.5 * jnp.sum(tok, axis=0, keepdims=True)
        dy = diff / D
        u = dy * gv
        dx = r * (u - xhat * jnp.mean(u * xhat, axis=-1, keepdims=True))
        dx_ref[...] = dx
        dxb_ref[...] = dx.astype(dxb_ref.dtype)
        dg_ref[...] += jnp.sum(dy * xhat, axis=0, keepdims=True)

    row = pl.BlockSpec((tm, D), lambda i: (i, 0))
    vec = pl.BlockSpec((1, D), lambda i: (0, 0))
    return pl.pallas_call(
        body, name=name, grid=(T // tm,),
        out_shape=(jax.ShapeDtypeStruct((T, D), F32), jax.ShapeDtypeStruct((T, D), _MXU_DTYPE),
                   jax.ShapeDtypeStruct((1, D), F32), jax.ShapeDtypeStruct((1, LANES), F32)),
        in_specs=[row, vec, row],
        out_specs=(row, row, vec, pl.BlockSpec((1, LANES), lambda i: (0, 0))),
        compiler_params=_params("arbitrary"),
    )(x, g, tgt)


def _outnorm_fwd(ya, yb, ga, gb, name):
    T, W = ya.shape
    tm = _tile(T, 512, 16)

    def body(ya_ref, yb_ref, ga_ref, gb_ref, o_ref):
        for k, (y_ref, g_ref) in enumerate(((ya_ref, ga_ref), (yb_ref, gb_ref))):
            yv = y_ref[...]
            r = lax.rsqrt(jnp.mean(yv * yv, axis=-1, keepdims=True) + EPS)
            o_ref[:, k * W:(k + 1) * W] = (yv * r * g_ref[...]).astype(o_ref.dtype)

    row = pl.BlockSpec((tm, W), lambda i: (i, 0))
    vec = pl.BlockSpec((1, W), lambda i: (0, 0))
    return pl.pallas_call(
        body, name=name, grid=(T // tm,),
        out_shape=jax.ShapeDtypeStruct((T, 2 * W), _MXU_DTYPE),
        in_specs=[row, row, vec, vec], out_specs=pl.BlockSpec((tm, 2 * W), lambda i: (i, 0)),
        compiler_params=_params("parallel"),
    )(ya, yb, ga, gb)


def _outnorm_bwd(dyn, ya, yb, ga, gb, name):
    T, W = ya.shape
    tm = _tile(T, 256, 16)

    def body(d_ref, ya_ref, yb_ref, ga_ref, gb_ref, dya_ref, dyb_ref, dga_ref, dgb_ref):
        @pl.when(pl.program_id(0) == 0)
        def _():
            dga_ref[...] = jnp.zeros_like(dga_ref)
            dgb_ref[...] = jnp.zeros_like(dgb_ref)

        for k, (y_ref, g_ref, dy_ref, dg_ref) in enumerate(
                ((ya_ref, ga_ref, dya_ref, dga_ref), (yb_ref, gb_ref, dyb_ref, dgb_ref))):
            dy, dgr = _rms_bwd_rows(d_ref[:, k * W:(k + 1) * W], y_ref[...], g_ref[...])
            dy_ref[...] = dy.astype(dy_ref.dtype)
            dg_ref[...] += jnp.sum(dgr, axis=0, keepdims=True)

    row = pl.BlockSpec((tm, W), lambda i: (i, 0))
    vec = pl.BlockSpec((1, W), lambda i: (0, 0))
    return pl.pallas_call(
        body, name=name, grid=(T // tm,),
        out_shape=(jax.ShapeDtypeStruct((T, W), _MXU_DTYPE), jax.ShapeDtypeStruct((T, W), _MXU_DTYPE),
                   jax.ShapeDtypeStruct((1, W), F32), jax.ShapeDtypeStruct((1, W), F32)),
        in_specs=[pl.BlockSpec((tm, 2 * W), lambda i: (i, 0)), row, row, vec, vec],
        out_specs=(row, row, vec, vec),
        compiler_params=_params("arbitrary"),
    )(dyn, ya, yb, ga, gb)


_STRIP_ROWS = 256


def _strips(rows):
    step = min(rows, _STRIP_ROWS)
    return [slice(r, r + step) for r in range(0, rows, step)]


def _ffn_gu(h, wg_t, wu_t, name, jobs=()):
    T, D = h.shape
    Fp = wg_t.shape[0]
    tm = _tile(T, 1024, 16)
    tn = _tile(Fp, _F_TILE, LANES)

    def body(h_ref, wg_ref, wu_ref, g_ref, u_ref, a_ref):
        for rows in _strips(tm):
            hv = h_ref[rows, :]
            g = _dot_nt(hv, wg_ref[...])
            u = _dot_nt(hv, wu_ref[...])
            g_ref[rows, :] = g.astype(g_ref.dtype)
            u_ref[rows, :] = u.astype(u_ref.dtype)
            a_ref[rows, :] = (g * jax.nn.sigmoid(g) * u).astype(a_ref.dtype)

    act = pl.BlockSpec((tm, tn), lambda n, m: (m, n))
    wsp = pl.BlockSpec((tn, D), lambda n, m: (n, 0))
    out = jax.ShapeDtypeStruct((T, Fp), _MXU_DTYPE)
    return _call(
        body, name=name, grid=(Fp // tn, T // tm), out_shape=(out, out, out),
        in_specs=[pl.BlockSpec((tm, D), lambda n, m: (m, 0)), wsp, wsp],
        out_specs=(act, act, act), args=(h, wg_t, wu_t), semantics=("parallel", "parallel"), jobs=jobs)


def _ffn_bwd_act(dxb, wd, G, U, name, jobs=()):
    T, D = dxb.shape
    Fp = wd.shape[0]
    tm = _tile(T, 1024, 16)
    tn = _tile(Fp, _F_TILE, LANES)

    def body(e_ref, wd_ref, g_ref, u_ref, dg_ref, du_ref):
        for rows in _strips(tm):
            da = 0.5 * _dot_nt(e_ref[rows, :], wd_ref[...])
            g = g_ref[rows, :].astype(F32)
            u = u_ref[rows, :].astype(F32)
            s = jax.nn.sigmoid(g)
            du_ref[rows, :] = (da * (g * s)).astype(du_ref.dtype)
            dg_ref[rows, :] = (da * u * (s * (1.0 + g * (1.0 - s)))).astype(dg_ref.dtype)

    act = pl.BlockSpec((tm, tn), lambda m, n: (m, n))
    out = jax.ShapeDtypeStruct((T, Fp), _MXU_DTYPE)
    return _call(
        body, name=name, grid=(T // tm, Fp // tn), out_shape=(out, out),
        in_specs=[pl.BlockSpec((tm, D), lambda m, n: (m, 0)), pl.BlockSpec((tn, D), lambda m, n: (n, 0)),
                  act, act],
        out_specs=(act, act), args=(dxb, wd, G, U), semantics=("parallel", "parallel"), jobs=jobs)


def _ffn_gate(h, wg_t, U, name, jobs=()):
    T, D = h.shape
    Fp = wg_t.shape[0]
    tm = _tile(T, 1024, 16)
    tn = _tile(Fp, _F_TILE, LANES)

    def body(h_ref, wg_ref, u_ref, g_ref, a_ref):
        for rows in _strips(tm):
            g = _dot_nt(h_ref[rows, :], wg_ref[...])
            g_ref[rows, :] = g.astype(g_ref.dtype)
            a_ref[rows, :] = (g * jax.nn.sigmoid(g) * u_ref[rows, :].astype(F32)).astype(a_ref.dtype)

    act = pl.BlockSpec((tm, tn), lambda m, n: (m, n))
    out = jax.ShapeDtypeStruct((T, Fp), _MXU_DTYPE)
    return _call(
        body, name=name, grid=(T // tm, Fp // tn), out_shape=(out, out),
        in_specs=[pl.BlockSpec((tm, D), lambda m, n: (m, 0)), pl.BlockSpec((tn, D), lambda m, n: (n, 0)), act],
        out_specs=(act, act), args=(h, wg_t, U), semantics=("parallel", "parallel"), jobs=jobs)


def _mm_nt(a, b, out_dtype, name, jobs=()):
    M, K = a.shape
    N = b.shape[0]
    tm = _tile(M, 1024, 16)
    tn = _tile(N, 1536, LANES)

    def body(a_ref, b_ref, o_ref):
        o_ref[...] = _dot_nt(a_ref[...], b_ref[...]).astype(o_ref.dtype)

    return _call(
        body, name=name, grid=(M // tm, N // tn), out_shape=jax.ShapeDtypeStruct((M, N), out_dtype),
        in_specs=[pl.BlockSpec((tm, K), lambda m, n: (m, 0)), pl.BlockSpec((tn, K), lambda m, n: (n, 0))],
        out_specs=pl.BlockSpec((tm, tn), lambda m, n: (m, n)), args=(a, b),
        semantics=("parallel", "parallel"), jobs=jobs)


_MM_OPERAND_BYTES = 26 * 1024 * 1024


def _k_tile(K, bytes_per_k, align):
    best = align
    for t in range(align, K + 1, align):
        if K % t == 0 and 2 * t * bytes_per_k <= _MM_OPERAND_BYTES:
            best = t
    return best


def _mm_nn(pairs, res, alpha, out_dtype, name, jobs=()):
    M, K = pairs[0][0].shape
    N = pairs[0][1].shape[1]
    n_pairs = len(pairs)
    tm = _tile(M, 1024, 16)
    tn = _tile(N, 1024, LANES)
    tk = _k_tile(K, n_pairs * (tm + tn) * pairs[0][0].dtype.itemsize, LANES)
    nk = K // tk

    def body(*refs):
        ab = refs[:2 * n_pairs]
        res_ref = refs[2 * n_pairs] if res is not None else None
        o_ref = refs[2 * n_pairs + (res is not None)]

        def finish(acc):
            out = alpha * acc
            if res_ref is not None:
                out = res_ref[...] + out
            o_ref[...] = out.astype(o_ref.dtype)

        part = _dot(ab[0][...], ab[1][...])
        for i in range(1, n_pairs):
            part = part + _dot(ab[2 * i][...], ab[2 * i + 1][...])
        if nk == 1:
            finish(part)
        else:
            acc_ref = refs[-1]
            k = pl.program_id(2)

            @pl.when(k == 0)
            def _():
                acc_ref[...] = part

            @pl.when(k > 0)
            def _():
                acc_ref[...] += part

            @pl.when(k == nk - 1)
            def _():
                finish(acc_ref[...])

    in_specs, args = [], []
    for a, b in pairs:
        in_specs += [pl.BlockSpec((tm, tk), lambda m, n, k: (m, k)), pl.BlockSpec((tk, tn), lambda m, n, k: (k, n))]
        args += [a, b]
    if res is not None:
        in_specs.append(pl.BlockSpec((tm, tn), lambda m, n, k: (m, n)))
        args.append(res)
    return _call(
        body, name=name, grid=(M // tm, N // tn, nk), out_shape=jax.ShapeDtypeStruct((M, N), out_dtype),
        in_specs=in_specs, out_specs=pl.BlockSpec((tm, tn), lambda m, n, k: (m, n)),
        scratch_shapes=[pltpu.VMEM((tm, tn), F32)] if nk > 1 else [], args=args,
        semantics=("parallel", "parallel", "arbitrary"), jobs=jobs)


def _mm_nn_norm(a, b, res, g, name, jobs=()):
    M, K = a.shape
    N = b.shape[1]
    tm = _tile(M, 512, 16)

    def body(a_ref, b_ref, res_ref, g_ref, y_ref, h_ref):
        y = res_ref[...] + _dot(a_ref[...], b_ref[...])
        y_ref[...] = y
        r = lax.rsqrt(jnp.mean(y * y, axis=-1, keepdims=True) + EPS)
        h_ref[...] = (y * r * g_ref[...]).astype(h_ref.dtype)

    row = pl.BlockSpec((tm, N), lambda m: (m, 0))
    return _call(
        body, name=name, grid=(M // tm,),
        out_shape=(jax.ShapeDtypeStruct((M, N), F32), jax.ShapeDtypeStruct((M, N), _MXU_DTYPE)),
        in_specs=[pl.BlockSpec((tm, K), lambda m: (m, 0)), pl.BlockSpec((K, N), lambda m: (0, 0)), row,
                  pl.BlockSpec((1, N), lambda m: (0, 0))],
        out_specs=(row, row), args=(a, b, res, g), semantics=("parallel",), jobs=jobs)


def _mm_tn(a, b, alpha, out_dtype, name, jobs=()):
    K, M = a.shape
    N = b.shape[1]
    tm = _tile(M, 512, LANES)
    tn = _tile(N, 1024, LANES)

    def body(a_ref, b_ref, o_ref):
        o_ref[...] = (alpha * _dot_tn(a_ref[...], b_ref[...])).astype(o_ref.dtype)

    return _call(
        body, name=name, grid=(N // tn, M // tm), out_shape=jax.ShapeDtypeStruct((M, N), out_dtype),
        in_specs=[pl.BlockSpec((K, tm), lambda n, m: (0, m)), pl.BlockSpec((K, tn), lambda n, m: (0, n))],
        out_specs=pl.BlockSpec((tm, tn), lambda n, m: (m, n)), args=(a, b),
        semantics=("parallel", "parallel"), jobs=jobs)


def _half_masks():
    lane = lax.broadcasted_iota(jnp.int32, (1, LANES), 1)
    return (lane < HEAD_DIM, lane >= HEAD_DIM)


def _swap_halves(v):
    return pltpu.roll(v.astype(F32), HEAD_DIM, 1).astype(v.dtype)


def _swa_geometry(n):
    qi = lax.broadcasted_iota(jnp.int32, (WINDOW, 2 * WINDOW), 0)
    kp = lax.broadcasted_iota(jnp.int32, (WINDOW, 2 * WINDOW), 1)
    dist = (WINDOW + qi) - kp
    valid = (dist >= 0) & (dist < WINDOW) & ((n > 0) | (kp >= WINDOW))
    return dist.astype(F32), valid


def _swa_slope(h):
    return 2.0 ** (-8.0 * (h + 1) / N_SWA_HEADS)


def _swa_softmax(qk, sink, slope, distf, valid):
    s = qk * ATT_SCALE - slope * distf
    s = jnp.where(valid, s, -1e30)
    m = jnp.maximum(jnp.max(s, axis=1, keepdims=True), sink)
    p = jnp.exp(s - m)
    e_sink = jnp.exp(sink - m)
    den = jnp.sum(p, axis=1, keepdims=True) + e_sink
    return p / den, e_sink / den


def _swa_group_heads(g):
    return [(2 * pp + a, pp, a) for pp in (2 * g, 2 * g + 1) for a in range(2)]


def _swa_specs(B, S):
    nb = S // WINDOW
    kcol = SWA_Q // SWA_KV
    cur = lambda b, n: (b * nb + n, kcol)
    prev = lambda b, n: (b * nb + jnp.maximum(n - 1, 0), kcol)
    curv = lambda b, n: (b * nb + n, kcol + 1)
    prevv = lambda b, n: (b * nb + jnp.maximum(n - 1, 0), kcol + 1)
    q_spec = pl.BlockSpec((WINDOW, SWA_Q), lambda b, n: (b * nb + n, 0))
    kv = [pl.BlockSpec((WINDOW, SWA_KV), f) for f in (prev, cur, prevv, curv)]
    sink_spec = pl.BlockSpec(memory_space=pltpu.SMEM)
    return nb, q_spec, kv, sink_spec


def _swa_kv_views(kp_ref, kc_ref, vp_ref, vc_ref, g):
    hm = _half_masks()
    c0 = (g // 2) * LANES
    k_all = jnp.concatenate([kp_ref[:, c0:c0 + LANES], kc_ref[:, c0:c0 + LANES]], axis=0)
    v_all = jnp.concatenate([vp_ref[:, c0:c0 + LANES], vc_ref[:, c0:c0 + LANES]], axis=0)
    b = g % 2
    ks, vs = [None, None], [None, None]
    ks[b], vs[b] = k_all, v_all
    ks[1 - b], vs[1 - b] = _swap_halves(k_all), _swap_halves(v_all)
    ks = [jnp.where(hm[a], ks[a], 0) for a in range(2)]
    vs = [jnp.where(hm[a], vs[a], 0) for a in range(2)]
    return ks, vs


def _swa_fwd(proj, sinks, B, S, name, jobs=()):
    T = B * S
    nb, q_spec, kv_specs, sink_spec = _swa_specs(B, S)

    def body(sink_ref, q_ref, kp_ref, kc_ref, vp_ref, vc_ref, y_ref):
        hm = _half_masks()
        distf, valid = _swa_geometry(pl.program_id(1))
        for g in range(N_SWA_KV):
            ks, vs = _swa_kv_views(kp_ref, kc_ref, vp_ref, vc_ref, g)
            heads = _swa_group_heads(g)
            qk = [_dot_nt(jnp.where(hm[a], q_ref[:, pp * LANES:(pp + 1) * LANES], 0), ks[a]) for _, pp, a in heads]
            p = [_swa_softmax(qk[i], sink_ref[0, h], _swa_slope(h), distf, valid)[0] for i, (h, _, _) in enumerate(heads)]
            o = [_dot(p[i].astype(_MXU_DTYPE), vs[a]) for i, (_, _, a) in enumerate(heads)]
            for j, pp in enumerate((2 * g, 2 * g + 1)):
                y_ref[:, pp * LANES:(pp + 1) * LANES] = o[2 * j] + o[2 * j + 1]

    return _call(
        body, name=name, grid=(B, nb), out_shape=jax.ShapeDtypeStruct((T, SWA_Q), F32),
        in_specs=[sink_spec, q_spec] + kv_specs,
        out_specs=pl.BlockSpec((WINDOW, SWA_Q), lambda b, n: (b * nb + n, 0)),
        args=(sinks, proj, proj, proj, proj, proj), semantics=("parallel", "parallel"), jobs=jobs)


def _swa_bwd(proj, sinks, dya, B, S, name, jobs=()):
    T = B * S
    nb, q_spec, kv_specs, sink_spec = _swa_specs(B, S)

    def body(sink_ref, q_ref, kp_ref, kc_ref, vp_ref, vc_ref, do_ref,
             dq_ref, dk_ref, dv_ref, dsink_ref, dk_acc, dv_acc):
        b_id, n = pl.program_id(0), pl.program_id(1)
        hm = _half_masks()
        lane = lax.broadcasted_iota(jnp.int32, (1, LANES), 1)

        @pl.when((b_id == 0) & (n == 0))
        def _():
            dsink_ref[...] = jnp.zeros_like(dsink_ref)

        @pl.when(n == 0)
        def _():
            dk_acc[...] = jnp.zeros_like(dk_acc)
            dv_acc[...] = jnp.zeros_like(dv_acc)

        distf, valid = _swa_geometry(n)
        r_prev = pl.multiple_of(jnp.maximum(n - 1, 0) * WINDOW, WINDOW)
        r_cur = pl.multiple_of(n * WINDOW, WINDOW)
        dsink = jnp.zeros((1, LANES), F32)
        for g in range(N_SWA_KV):
            ks, vs = _swa_kv_views(kp_ref, kc_ref, vp_ref, vc_ref, g)
            heads = _swa_group_heads(g)
            four = range(len(heads))
            qms = [jnp.where(hm[a], q_ref[:, pp * LANES:(pp + 1) * LANES], 0) for _, pp, a in heads]
            doms = [jnp.where(hm[a], do_ref[:, pp * LANES:(pp + 1) * LANES], 0) for _, pp, a in heads]
            qk = [_dot_nt(qms[i], ks[heads[i][2]]) for i in four]
            dp = [_dot_nt(doms[i], vs[heads[i][2]]) for i in four]
            soft = [_swa_softmax(qk[i], sink_ref[0, heads[i][0]], _swa_slope(heads[i][0]), distf, valid) for i in four]
            p = [soft[i][0] for i in four]
            delta = [jnp.sum(p[i] * dp[i], axis=1, keepdims=True) for i in four]
            ds = [(p[i] * (dp[i] - delta[i]) * ATT_SCALE).astype(_MXU_DTYPE) for i in four]
            for i in four:
                dsink = dsink + jnp.where(lane == heads[i][0], -jnp.sum(soft[i][1] * delta[i]), 0.0)
            dq = [_dot(ds[i], ks[heads[i][2]]) for i in four]
            dk_h = [_dot_tn(ds[i], qms[i]) for i in four]
            dv_h = [_dot_tn(p[i].astype(_MXU_DTYPE), doms[i]) for i in four]
            for j, pp in enumerate((2 * g, 2 * g + 1)):
                dq_ref[:, pp * LANES:(pp + 1) * LANES] = (dq[2 * j] + dq[2 * j + 1]).astype(dq_ref.dtype)
            dk_g = [dk_h[a] + dk_h[2 + a] for a in range(2)]
            dv_g = [dv_h[a] + dv_h[2 + a] for a in range(2)]
            bsel = g % 2
            dk_t = dk_g[bsel] + pltpu.roll(dk_g[1 - bsel], HEAD_DIM, 1)
            dv_t = dv_g[bsel] + pltpu.roll(dv_g[1 - bsel], HEAD_DIM, 1)
            c0 = (g // 2) * LANES
            dk_acc[pl.ds(r_prev, WINDOW), c0:c0 + LANES] += dk_t[:WINDOW]
            dk_acc[pl.ds(r_cur, WINDOW), c0:c0 + LANES] += dk_t[WINDOW:]
            dv_acc[pl.ds(r_prev, WINDOW), c0:c0 + LANES] += dv_t[:WINDOW]
            dv_acc[pl.ds(r_cur, WINDOW), c0:c0 + LANES] += dv_t[WINDOW:]
        dsink_ref[...] += dsink

        @pl.when(n == nb - 1)
        def _():
            dk_ref[...] = dk_acc[...].astype(dk_ref.dtype)
            dv_ref[...] = dv_acc[...].astype(dv_ref.dtype)

    seq_kv = pl.BlockSpec((S, SWA_KV), lambda b, n: (b, 0))
    return _call(
        body, name=name, grid=(B, nb),
        out_shape=(jax.ShapeDtypeStruct((T, SWA_Q), _MXU_DTYPE), jax.ShapeDtypeStruct((T, SWA_KV), _MXU_DTYPE),
                   jax.ShapeDtypeStruct((T, SWA_KV), _MXU_DTYPE), jax.ShapeDtypeStruct((1, LANES), F32)),
        in_specs=[sink_spec, q_spec] + kv_specs + [pl.BlockSpec((WINDOW, SWA_Q), lambda b, n: (b * nb + n, 0))],
        out_specs=(pl.BlockSpec((WINDOW, SWA_Q), lambda b, n: (b * nb + n, 0)), seq_kv, seq_kv,
                   pl.BlockSpec((1, LANES), lambda b, n: (0, 0))),
        scratch_shapes=[pltpu.VMEM((S, SWA_KV), F32), pltpu.VMEM((S, SWA_KV), F32)],
        args=(sinks, proj, proj, proj, proj, proj, dya), semantics=("arbitrary", "arbitrary"), jobs=jobs)


SB_TILE = 256
SB_HALF = 128
SB_DEAD = -105.0


def _mark_lanes():
    lane = lax.broadcasted_iota(jnp.int32, (1, LANES), 1)
    return (lane == HEAD_DIM - 1) | (lane == LANES - 1)


def _tri2(cond):
    j = lax.broadcasted_iota(jnp.int32, (2 * SB_HALF, SB_HALF), 0) & (SB_HALF - 1)
    s = lax.broadcasted_iota(jnp.int32, (2 * SB_HALF, SB_HALF), 1)
    return cond(j, s).astype(_MXU_DTYPE)


def _half_cumsums(x, tri2):
    out = []
    for h in range(2):
        xh = x[:, h * SB_HALF:(h + 1) * SB_HALF]
        hi = xh.astype(_MXU_DTYPE)
        lo = (xh - hi.astype(F32)).astype(_MXU_DTYPE)
        out.append(_dot(jnp.concatenate([hi, lo], axis=1), tri2))
    return out


def _log_sigmoid(z):
    return jnp.minimum(z, 0.0) - jnp.log(1.0 + jnp.exp(-jnp.abs(z)))


def _sb_specs(B, S):
    qb = (SWA_Q + 2 * SWA_KV) // LANES
    kb = qb + SB_W // LANES
    vb = kb + SB_W // LANES
    return [pl.BlockSpec((S, LANES), functools.partial(lambda b, p, c: (b, c + p), c=c)) for c in (qb, kb, vb)]


def _sb_fwd(proj, B, S, name, jobs=()):
    T = B * S
    tq = SB_TILE
    nq = S // tq

    def body(q_ref, k_ref, v_ref, y_ref, tot_ref):
        hm = _half_masks()
        ji = lax.broadcasted_iota(jnp.int32, (tq, tq), 0)
        si = lax.broadcasted_iota(jnp.int32, (tq, tq), 1)
        tri_after = _tri2(lambda j, s: j > s)
        causal = si < ji
        mark = _mark_lanes()

        def q_block(qi, with_previous):
            r0 = qi * tq if isinstance(qi, int) else pl.multiple_of(qi * tq, tq)
            q_pair = q_ref[pl.ds(r0, tq), :] * ATT_SCALE
            qms = [jnp.where(hm[a], q_pair, 0) for a in range(2)]

            def tiles(blocks, state):
                two, nb = range(2), range(len(blocks))
                kk = [k_ref[pl.ds(c0, tq), :] for c0, _ in blocks]
                vv = [v_ref[pl.ds(c0, tq), :] for c0, _ in blocks]
                z = [[_dot_nt(qms[a], kk[b]) for a in two] for b in nb]
                lb = [[_log_sigmoid(z[b][a]) for a in two] for b in nb]
                l1m = [[jnp.where(causal, lb[b][a] - z[b][a], 0.0) if blocks[b][1] else lb[b][a] - z[b][a]
                        for a in two] for b in nb]
                cum = [[_half_cumsums(l1m[b][a], tri_after) for a in two] for b in nb]
                tot = [[[cum[b][a][h][:, 0:1] + l1m[b][a][:, h * SB_HALF:h * SB_HALF + 1] for h in two]
                        for a in two] for b in nb]
                car = [[state[a][1] for a in two]]
                for b in nb:
                    car.append([car[b][a] + (tot[b][a][0] + tot[b][a][1]) for a in two])
                after = [[jnp.concatenate([cum[b][a][0] + (car[b][a] + tot[b][a][1]), cum[b][a][1] + car[b][a]], axis=1)
                          for a in two] for b in nb]
                att = [[jnp.exp(lb[b][a] + after[b][a]) for a in two] for b in nb]
                att = [[jnp.where(causal, att[b][a], 0.0) if blocks[b][1] else att[b][a] for a in two] for b in nb]
                acc = [state[a][0] for a in two]
                for b in nb:
                    acc = [acc[a] + _dot(att[b][a].astype(_MXU_DTYPE), jnp.where(hm[a], vv[b], 0)) for a in two]
                return tuple((acc[a], car[-1][a]) for a in two)

            def live(st):
                return jnp.maximum(jnp.max(st[0][1]), jnp.max(st[1][1])) > SB_DEAD

            def step(c):
                it, _, st = c
                st = tiles([(pl.multiple_of((qi - 1 - it) * tq, tq), False)], st)
                return it + 1, live(st), st

            zero = (jnp.zeros((tq, LANES), F32), jnp.zeros((tq, 1), F32))
            if with_previous:
                state = tiles([(r0, True), (pl.multiple_of(r0 - tq, tq), False)], (zero, zero))
                done, _, state = lax.while_loop(lambda c: (c[0] < qi) & c[1], step, (jnp.int32(1), live(state), state))
            else:
                state, done = tiles([(r0, True)], (zero, zero)), 0
            y_ref[pl.ds(r0, tq), :] = state[0][0] + state[1][0]
            first = jnp.asarray(qi - done, F32)
            tot_ref[pl.ds(r0, tq), :] = jnp.where(mark, first, jnp.where(hm[0], state[0][1], state[1][1]))

        q_block(0, False)

        def q_loop(qi, carry):
            q_block(qi, True)
            return carry

        lax.fori_loop(1, nq, q_loop, 0)

    out_spec = pl.BlockSpec((S, LANES), lambda b, p: (b, p))
    return _call(
        body, name=name, grid=(B, SB_W // LANES),
        out_shape=(jax.ShapeDtypeStruct((T, SB_W), F32), jax.ShapeDtypeStruct((T, SB_W), F32)),
        in_specs=_sb_specs(B, S), out_specs=(out_spec, out_spec), args=(proj, proj, proj),
        semantics=("parallel", "parallel"), jobs=jobs)


def _sb_bwd(proj, tot, dyb, B, S, name, jobs=()):
    T = B * S
    tq = SB_TILE
    nq = S // tq

    def body(q_ref, k_ref, v_ref, do_ref, tot_ref, dq_ref, dk_ref, dv_ref, dk_acc, dv_acc):
        hm = _half_masks()
        ji = lax.broadcasted_iota(jnp.int32, (tq, tq), 0)
        si = lax.broadcasted_iota(jnp.int32, (tq, tq), 1)
        tri_incl = _tri2(lambda j, s: j <= s)
        tri_excl = _tri2(lambda j, s: j < s)
        causal = si < ji
        mark = _mark_lanes()
        dk_acc[...] = jnp.zeros_like(dk_acc)
        dv_acc[...] = jnp.zeros_like(dv_acc)

        def q_block(qi, with_previous):
            r0 = qi * tq if isinstance(qi, int) else pl.multiple_of(qi * tq, tq)
            q_pair = q_ref[pl.ds(r0, tq), :] * ATT_SCALE
            do_pair = do_ref[pl.ds(r0, tq), :]
            tot_pair = tot_ref[pl.ds(r0, tq), :]
            qms = [jnp.where(hm[a], q_pair, 0) for a in range(2)]
            doms = [jnp.where(hm[a], do_pair, 0) for a in range(2)]
            totals = [jnp.max(jnp.where(hm[a] & ~mark, tot_pair, -jnp.inf), axis=1, keepdims=True) for a in range(2)]

            def tiles(blocks, state):
                two, nb = range(2), range(len(blocks))
                last = SB_HALF - 1
                kk = [k_ref[pl.ds(c0, tq), :] for c0, _ in blocks]
                vv = [v_ref[pl.ds(c0, tq), :] for c0, _ in blocks]
                z = [[_dot_nt(qms[a], kk[b]) for a in two] for b in nb]
                d_att = [[_dot_nt(doms[a], vv[b]) for a in two] for b in nb]
                lb = [[_log_sigmoid(z[b][a]) for a in two] for b in nb]
                l1m = [[jnp.where(causal, lb[b][a] - z[b][a], 0.0) if blocks[b][1] else lb[b][a] - z[b][a]
                        for a in two] for b in nb]
                cum = [[_half_cumsums(l1m[b][a], tri_incl) for a in two] for b in nb]
                cp, upto = [[state[a][1] for a in two]], []
                for b in nb:
                    upto.append([jnp.concatenate(
                        [cum[b][a][0] + cp[b][a], cum[b][a][1] + (cp[b][a] + cum[b][a][0][:, last:last + 1])], axis=1)
                        for a in two])
                    cp.append([upto[b][a][:, tq - 1:tq] for a in two])
                att = [[jnp.exp(lb[b][a] + (totals[a] - upto[b][a])) for a in two] for b in nb]
                att = [[jnp.where(causal, att[b][a], 0.0) if blocks[b][1] else att[b][a] for a in two] for b in nb]
                d_log = [[d_att[b][a] * att[b][a] for a in two] for b in nb]
                cumd = [[_half_cumsums(d_log[b][a], tri_excl) for a in two] for b in nb]
                totd = [[[cumd[b][a][h][:, last:last + 1] + d_log[b][a][:, h * SB_HALF + last:h * SB_HALF + last + 1]
                          for h in two] for a in two] for b in nb]
                cq = [[state[a][2] for a in two]]
                for b in nb:
                    cq.append([cq[b][a] + (totd[b][a][0] + totd[b][a][1]) for a in two])
                before = [[jnp.concatenate([cumd[b][a][0] + cq[b][a], cumd[b][a][1] + (cq[b][a] + totd[b][a][0])], axis=1)
                           for a in two] for b in nb]
                sig = [[jnp.exp(lb[b][a]) for a in two] for b in nb]
                dz = [[d_log[b][a] * (1.0 - sig[b][a]) - sig[b][a] * before[b][a] for a in two] for b in nb]
                dz = [[jnp.where(causal, dz[b][a], 0.0) if blocks[b][1] else dz[b][a] for a in two] for b in nb]
                dzb = [[dz[b][a].astype(_MXU_DTYPE) for a in two] for b in nb]
                dq = [state[a][0] for a in two]
                for b, (c0, _) in enumerate(blocks):
                    ks = kk[b] * ATT_SCALE
                    dq = [dq[a] + _dot(dzb[b][a], jnp.where(hm[a], ks, 0)) for a in two]
                    dk_acc[pl.ds(c0, tq), :] += _dot_tn(dzb[b][0], qms[0]) + _dot_tn(dzb[b][1], qms[1])
                    dv_acc[pl.ds(c0, tq), :] += (_dot_tn(att[b][0].astype(_MXU_DTYPE), doms[0])
                                                 + _dot_tn(att[b][1].astype(_MXU_DTYPE), doms[1]))
                return tuple((dq[a], cp[-1][a], cq[-1][a]) for a in two)

            zero_col = jnp.zeros((tq, 1), F32)
            zero = (jnp.zeros((tq, LANES), F32), zero_col, zero_col)
            if with_previous:
                first = jnp.max(jnp.where(mark, tot_pair, -jnp.inf))
                first = jnp.where((first >= 0.0) & (first <= (qi - 1).astype(F32)), first, 0.0).astype(jnp.int32)
                state = lax.fori_loop(first, qi - 1,
                                      lambda kj, st: tiles([(pl.multiple_of(kj * tq, tq), False)], st), (zero, zero))
                state = tiles([(pl.multiple_of(r0 - tq, tq), False), (r0, True)], state)
            else:
                state = tiles([(r0, True)], (zero, zero))
            dq_ref[pl.ds(r0, tq), :] = (state[0][0] + state[1][0]).astype(dq_ref.dtype)

        q_block(0, False)

        def q_loop(qi, carry):
            q_block(qi, True)
            return carry

        lax.fori_loop(1, nq, q_loop, 0)
        dk_ref[...] = dk_acc[...].astype(dk_ref.dtype)
        dv_ref[...] = dv_acc[...].astype(dv_ref.dtype)

    pair = pl.BlockSpec((S, LANES), lambda b, p: (b, p))
    out = jax.ShapeDtypeStruct((T, SB_W), _MXU_DTYPE)
    return _call(
        body, name=name, grid=(B, SB_W // LANES), out_shape=(out, out, out),
        in_specs=_sb_specs(B, S) + [pair, pair], out_specs=(pair, pair, pair),
        scratch_shapes=[pltpu.VMEM((S, LANES), F32), pltpu.VMEM((S, LANES), F32)],
        args=(proj, proj, proj, dyb, tot), semantics=("parallel", "parallel"), jobs=jobs)


def _layer_step(x, tgt, B, S, small, comm):
    run, big, part = comm.run, comm.big, comm.partial
    ffn1_w, ffn2_w = ("ffn1_down", "ffn1_gate", "ffn1_up"), ("ffn2_down", "ffn2_gate", "ffn2_up")

    h1 = run(_rms_fwd, x, small["ffn1_norm"], "ffn1_rms", ag=(FIRST_GATHERED,),
             cast=tuple(n for n in BIG_NAMES if n != FIRST_GATHERED))
    U1 = run(_mm_nt, h1, big["ffn1_up"], _MXU_DTYPE, "ffn1_up", ag=("ffn1_gate",))
    G1, A1 = run(_ffn_gate, h1, big["ffn1_gate"], U1, "ffn1_gate", ag=("ffn1_down",))
    x1 = run(_mm_nn, [(A1, big["ffn1_down"])], x, 0.5, F32, "ffn1_down", ag=("w_in",))
    h2 = run(_rms_fwd, x1, small["mix_norm"], "mix_rms")
    proj = run(_mm_nt, h2, big["w_in"], _MXU_DTYPE, "in_proj", ag=("w_out",), ag_early=AG_PARTS)
    ya = run(_swa_fwd, proj, small["swa_sinks"], B, S, "swa_fwd", ag=("ffn2_gate",))
    yb, tot = run(_sb_fwd, proj, B, S, "sb_fwd", ag=("ffn2_up",))
    yn = _outnorm_fwd(ya, yb, small["swa_out_norm"], small["sb_out_norm"], "out_norm")
    x2, h3 = run(_mm_nn_norm, yn, big["w_out"], x1, small["ffn2_norm"], "out_proj_norm")
    G2, U2, A2 = run(_ffn_gu, h3, big["ffn2_gate"], big["ffn2_up"], "ffn2_gate_up", ag=("ffn2_down",),
                     ag_early=AG_PARTS)
    x3 = run(_mm_nn, [(A2, big["ffn2_down"])], x2, 0.5, F32, "ffn2_down")

    dx3, dx3b, d_final, loss = _loss_head(x3, small["final_norm"], tgt, "loss_head")

    dG2, dU2 = run(_ffn_bwd_act, dx3b, big["ffn2_down"], G2, U2, "ffn2_bwd_act")
    part["ffn2_down"] = run(_mm_tn, A2, dx3b, 0.5, _WIRE_DTYPE, "ffn2_dw_down")
    part["ffn2_gate"] = run(_mm_tn, dG2, h3, 1.0, _WIRE_DTYPE, "ffn2_dw_gate")
    part["ffn2_up"] = run(_mm_tn, dU2, h3, 1.0, _WIRE_DTYPE, "ffn2_dw_up")
    dh3 = run(_mm_nn, [(dG2, big["ffn2_gate"])], None, 1.0, F32, "ffn2_dh_gate", rs1=ffn2_w)
    dh3 = run(_mm_nn, [(dU2, big["ffn2_up"])], dh3, 1.0, F32, "ffn2_dh_up")
    dx2, dx2b, d_g2 = _rms_bwd(dh3, x2, small["ffn2_norm"], dx3, "ffn2_rms_bwd")

    part["w_out"] = run(_mm_tn, yn, dx2b, 1.0, _WIRE_DTYPE, "dw_out")
    dyn = run(_mm_nt, dx2b, big["w_out"], F32, "out_proj_bwd")
    dya, dyb, d_ga, d_gb = _outnorm_bwd(dyn, ya, yb, small["swa_out_norm"], small["sb_out_norm"], "out_norm_bwd")
    dqa, dka, dva, d_sinks = run(_swa_bwd, proj, small["swa_sinks"], dya, B, S, "swa_bwd", rs2=ffn2_w[:1])
    dqb, dkb, dvb = run(_sb_bwd, proj, tot, dyb, B, S, "sb_bwd", rs2=ffn2_w[1:])
    dproj = jnp.concatenate([dqa, dka, dva, dqb, dkb, dvb], axis=1)
    part["w_in"] = run(_mm_tn, dproj, h2, 1.0, _WIRE_DTYPE, "dw_in", adamw=("ffn2_down", "ffn2_gate"))
    dh2 = run(_mm_nn, [(dproj, big["w_in"])], None, 1.0, F32, "in_proj_bwd", rs1=("w_in", "w_out"))
    dx1, dx1b, d_gm = _rms_bwd(dh2, x1, small["mix_norm"], dx2, "mix_rms_bwd")

    dG1, dU1 = run(_ffn_bwd_act, dx1b, big["ffn1_down"], G1, U1, "ffn1_bwd_act", rs2=("w_in", "w_out"))
    part["ffn1_down"] = run(_mm_tn, A1, dx1b, 0.5, _WIRE_DTYPE, "ffn1_dw_down", adamw=("ffn2_up", "w_in", "w_out"))
    part["ffn1_gate"] = run(_mm_tn, dG1, h1, 1.0, _WIRE_DTYPE, "ffn1_dw_gate", rs1=("ffn1_down",))
    part["ffn1_up"] = run(_mm_tn, dU1, h1, 1.0, _WIRE_DTYPE, "ffn1_dw_up", rs1=("ffn1_gate",), rs2=("ffn1_down",))
    dh1 = run(_mm_nn, [(dG1, big["ffn1_gate"])], None, 1.0, F32, "ffn1_dh_gate", rs1=("ffn1_up",), rs2=("ffn1_gate",))
    dh1 = run(_mm_nn, [(dU1, big["ffn1_up"])], dh1, 1.0, F32, "ffn1_dh_up", rs2=("ffn1_up",))
    gx, _, d_g1 = _rms_bwd(dh1, x, small["ffn1_norm"], dx1, "ffn1_rms_bwd")

    d_small = {"ffn1_norm": d_g1, "mix_norm": d_gm, "swa_sinks": d_sinks[:, :N_SWA_HEADS], "swa_out_norm": d_ga,
               "sb_out_norm": d_gb, "ffn2_norm": d_g2, "final_norm": d_final}
    return loss, gx, d_small


MESH = pl.DeviceIdType.MESH
BIG_NAMES = ("ffn1_gate", "ffn1_up", "ffn1_down", "w_in", "w_out", "ffn2_gate", "ffn2_up", "ffn2_down")
FIRST_GATHERED = "ffn1_up"
_COMM_PARAMS = pltpu.CompilerParams(has_side_effects=True)


def _place():
    x, y, c = lax.axis_index("x"), lax.axis_index("y"), lax.axis_index("c")
    other_chips = [(1 - x, y), (x, 1 - y), (1 - x, 1 - y)]
    return x, y, c, other_chips


def _padded_rows(rows):
    full = N_DEV * rows
    return -(-full // _F_TILE) * _F_TILE


AG_PARTS = 4


def _row_parts(rows, n):
    units = rows // 16
    assert units * 16 == rows and units >= n
    out, off = [], 0
    for i in range(n):
        size = (units // n + (1 if i < units % n else 0)) * 16
        out.append((off, size))
        off += size
    return out


def _ag_job(shards, early_parts=AG_PARTS // 2):
    nw = len(shards)
    D = shards[0].shape[1]
    rows_w = [s.shape[0] for s in shards]
    full_w = [_padded_rows(r) for r in rows_w]
    pad_w = [f - N_DEV * r for f, r in zip(full_w, rows_w)]
    max_pad = max(max(pad_w), 16)
    n_parts = AG_PARTS
    parts_w = [_row_parts(r, n_parts) for r in rows_w]

    class Plan:
        def __init__(self, ins, outs, scratch):
            zbuf, send_sems, recv_sems, local_sems, zero_sems = scratch
            x, y, c, chips = _place()
            me, sibling = (x, y, c), (x, y, 1 - c)

            def rows(w, block, part=None):
                off, size = (0, rows_w[w]) if part is None else part
                px, py, pc = block
                start = pl.multiple_of((4 * px + 2 * py + pc) * rows_w[w] + off, 16)
                return outs[w].at[pl.ds(start, size), :]

            def copy(w, k, block, to, part=None, own=False):
                src = rows(w, block, part)
                if own:
                    src = ins[w] if part is None else ins[w].at[pl.ds(part[0], part[1]), :]
                return pltpu.make_async_remote_copy(
                    src_ref=src, dst_ref=rows(w, block, part), send_sem=send_sems.at[w, k],
                    recv_sem=recv_sems.at[w, k], device_id=to, device_id_type=MESH)

            def k_ici(j, p):
                return 1 + j * n_parts + p

            def k_on(j, p):
                return 1 + (3 + j) * n_parts + p

            self.zbuf = zbuf
            self.local = [pltpu.make_async_copy(zbuf.at[pl.ds(0, pad_w[w]), :],
                                                outs[w].at[pl.ds(N_DEV * rows_w[w], pad_w[w]), :], zero_sems.at[w])
                          for w in range(nw) if pad_w[w]]
            self.local += [pltpu.make_async_copy(ins[w], rows(w, me), local_sems.at[w]) for w in range(nw)]
            self.first = [[copy(w, 0, me, sibling, own=True)]
                          + [copy(w, k_ici(j, p), me, (*chip, c), part, own=True)
                             for p, part in enumerate(parts_w[w]) for j, chip in enumerate(chips)]
                          for w in range(nw)]
            self.arrive = [[copy(w, k_ici(j, p), (*chip, c), me, part)
                            for p, part in enumerate(parts_w[w]) for j, chip in enumerate(chips)] for w in range(nw)]
            self.passed = [[copy(w, k_on(j, p), (*chip, c), sibling, part)
                            for p, part in enumerate(parts_w[w]) for j, chip in enumerate(chips)] for w in range(nw)]
            self.from_sibling = [[copy(w, 0, sibling, me)]
                                 + [copy(w, k_on(j, p), (*chip, 1 - c), me, part)
                                    for p, part in enumerate(parts_w[w]) for j, chip in enumerate(chips)]
                                 for w in range(nw)]

    def start(ins, outs, scratch):
        plan = Plan(ins, outs, scratch)
        plan.zbuf[...] = jnp.zeros_like(plan.zbuf)
        for cp in plan.local:
            cp.start()
        for w in range(nw):
            for cp in plan.first[w]:
                cp.start()

    def mid(ins, outs, scratch, phase):
        plan = Plan(ins, outs, scratch)
        early = 3 * early_parts
        for w in range(nw):
            pairs = list(zip(plan.arrive[w], plan.passed[w]))
            for arrived, onward in (pairs[:early] if phase == 0 else pairs[early:]):
                arrived.wait_recv()
                onward.start()

    def finish(ins, outs, scratch):
        plan = Plan(ins, outs, scratch)
        for w in range(nw):
            for cp in plan.from_sibling[w]:
                cp.wait_recv()
        for w in range(nw):
            for cp in plan.first[w] + plan.passed[w]:
                cp.wait_send()
        for cp in plan.local:
            cp.wait()

    return _Job(
        ins=shards, out_shape=[jax.ShapeDtypeStruct((f, D), s.dtype) for f, s in zip(full_w, shards)],
        scratch=[pltpu.VMEM((max_pad, D), shards[0].dtype), pltpu.SemaphoreType.DMA((nw, 1 + 6 * n_parts)),
                 pltpu.SemaphoreType.DMA((nw, 1 + 6 * n_parts)), pltpu.SemaphoreType.DMA((nw,)),
                 pltpu.SemaphoreType.DMA((nw,))],
        start=start, mid=mid, finish=finish)


def _rs1_job(partials, rows_w):
    nw = len(partials)
    D = partials[0].shape[1]

    def copies(ins, outs, scratch):
        send_sems, recv_sems = scratch
        x, y, c, _ = _place()
        out = []
        for w in range(nw):
            r = rows_w[w]
            for q in range(4):
                src = ins[w].at[pl.ds(pl.multiple_of((2 * q + 1 - c) * r, 16), r), :]
                out.append(pltpu.make_async_remote_copy(
                    src_ref=src, dst_ref=outs[w].at[pl.ds(q * r, r), :], send_sem=send_sems.at[w, q],
                    recv_sem=recv_sems.at[w, q], device_id=(x, y, 1 - c), device_id_type=MESH))
        return out

    def start(ins, outs, scratch):
        for cp in copies(ins, outs, scratch):
            cp.start()

    def finish(ins, outs, scratch):
        for cp in copies(ins, outs, scratch):
            cp.wait()

    return _Job(
        ins=partials, out_shape=[jax.ShapeDtypeStruct((4 * r, D), p.dtype) for r, p in zip(rows_w, partials)],
        scratch=[pltpu.SemaphoreType.DMA((nw, 4)), pltpu.SemaphoreType.DMA((nw, 4))], start=start, finish=finish)


def _pair_sum(partial, from_sibling, rows, core, name):
    D = partial.shape[1]

    def body(core_ref, p_ref, s_ref, o_ref):
        o_ref[...] = (p_ref[...].astype(F32) + s_ref[...].astype(F32)).astype(o_ref.dtype)

    grid_spec = pltpu.PrefetchScalarGridSpec(
        num_scalar_prefetch=1, grid=(4,),
        in_specs=[pl.BlockSpec((rows, D), lambda q, core_ref: (2 * q + core_ref[0], 0)),
                  pl.BlockSpec((rows, D), lambda q, core_ref: (q, 0))],
        out_specs=pl.BlockSpec((rows, D), lambda q, core_ref: (q, 0)))
    return pl.pallas_call(
        body, name=name, grid_spec=grid_spec, out_shape=jax.ShapeDtypeStruct((4 * rows, D), partial.dtype),
        compiler_params=_params("arbitrary"),
    )(core, partial, from_sibling)


def _rs2_job(chip_sums, rows_w):
    nw = len(chip_sums)

    def copies(ins, outs, scratch):
        send_sems, recv_sems, local_sems = scratch
        x, y, c, chips = _place()
        my_chip = 2 * x + y
        out = []
        for w in range(nw):
            r = rows_w[w]
            mine = pl.ds(pl.multiple_of(my_chip * r, 16), r)
            out.append(pltpu.make_async_copy(ins[w].at[mine, :], outs[w].at[mine, :], local_sems.at[w]))
            for j, (qx, qy) in enumerate(chips):
                src = ins[w].at[pl.ds(pl.multiple_of((2 * qx + qy) * r, 16), r), :]
                out.append(pltpu.make_async_remote_copy(
                    src_ref=src, dst_ref=outs[w].at[mine, :], send_sem=send_sems.at[w, j],
                    recv_sem=recv_sems.at[w, j], device_id=(qx, qy, c), device_id_type=MESH))
        return out

    def start(ins, outs, scratch):
        for cp in copies(ins, outs, scratch):
            cp.start()

    def finish(ins, outs, scratch):
        for cp in copies(ins, outs, scratch):
            cp.wait()

    return _Job(
        ins=chip_sums, out_shape=[jax.ShapeDtypeStruct(s.shape, s.dtype) for s in chip_sums],
        scratch=[pltpu.SemaphoreType.DMA((nw, 3)), pltpu.SemaphoreType.DMA((nw, 3)), pltpu.SemaphoreType.DMA((nw,))],
        start=start, finish=finish)


class _Comm:
    def __init__(self, shards, state):
        self.shards, self.state = dict(shards), state
        self.rows = {n: st[0].shape[0] for n, st in state.items()}
        self.core = lax.axis_index("c").astype(jnp.int32).reshape(1)
        self.big, self.partial, self.chip_sums, self.slots, self.updates = {}, {}, {}, {}, {}

    def slots3(self, name):
        return self.slots[name].reshape(4, self.rows[name], -1)

    def run(self, fn, *args, ag=(), rs1=(), rs2=(), adamw=(), cast=(), ag_early=AG_PARTS // 2):
        jobs = []
        if cast:
            jobs.append(_cast_job([self.state[n][0] for n in cast], _WIRE_DTYPE))
        if ag:
            jobs.append(_ag_job([self.shards[n] for n in ag], ag_early))
        if rs1:
            jobs.append(_rs1_job([self.partial[n] for n in rs1], [self.rows[n] for n in rs1]))
        if rs2:
            jobs.append(_rs2_job([self.chip_sums[n] for n in rs2], [self.rows[n] for n in rs2]))
        for n in adamw:
            w2, m2, v2 = self.state[n]
            jobs.append(_adamw_job(w2, self.slots3(n), m2, v2))
        out, job_res = fn(*args, jobs=jobs)
        job_res = iter(job_res)
        if cast:
            self.shards.update(zip(cast, next(job_res)))
        if ag:
            self.big.update(zip(ag, next(job_res)))
        if rs1:
            for n, got in zip(rs1, next(job_res)):
                self.chip_sums[n] = _pair_sum(self.partial[n], got, self.rows[n], self.core, "pair_sum_" + n)
        if rs2:
            self.slots.update(zip(rs2, next(job_res)))
        for n in adamw:
            self.updates[n] = next(job_res)
        return out


SMALL_ROWS = 88


def _small_allreduce(vec):
    def body(v_ref, o_ref, gather, send_sems, recv_sems):
        x, y, c, _ = _place()
        my_id = 4 * x + 2 * y + c
        gather[my_id] = v_ref[...]
        copies = []
        for r in range(1, N_DEV):
            peer = (x ^ (r >> 2), y ^ ((r >> 1) & 1), c ^ (r & 1))
            cp = pltpu.make_async_remote_copy(src_ref=v_ref, dst_ref=gather.at[my_id], send_sem=send_sems.at[r - 1],
                                              recv_sem=recv_sems.at[r - 1], device_id=peer, device_id_type=MESH)
            cp.start()
            copies.append(cp)
        for cp in copies:
            cp.wait()
        acc = gather[0]
        for d in range(1, N_DEV):
            acc = acc + gather[d]
        o_ref[...] = acc

    vm = pl.BlockSpec(memory_space=pltpu.VMEM)
    return pl.pallas_call(
        body, name="small_allreduce", out_shape=jax.ShapeDtypeStruct(vec.shape, F32),
        in_specs=[vm], out_specs=vm,
        scratch_shapes=[pltpu.VMEM((N_DEV,) + vec.shape, F32), pltpu.SemaphoreType.DMA((N_DEV - 1,)),
                        pltpu.SemaphoreType.DMA((N_DEV - 1,))],
        compiler_params=_COMM_PARAMS,
    )(vec)


def _adamw_update(w, g, m, v):
    nm = ADAM_B1 * m + (1.0 - ADAM_B1) * g
    nv = ADAM_B2 * v + (1.0 - ADAM_B2) * jnp.square(g)
    m_hat = nm / (1.0 - ADAM_B1 ** ADAM_STEP)
    v_hat = nv / (1.0 - ADAM_B2 ** ADAM_STEP)
    return -ADAM_LR * (m_hat / (jnp.sqrt(v_hat) + ADAM_EPS) + ADAM_WD * w), nm, nv


def _adamw(w, g, m, v, name):
    R, C = w.shape
    tr = _tile(R, 256, 8)

    def body(w_ref, g_ref, m_ref, v_ref, d_ref, nm_ref, nv_ref):
        d_ref[...], nm_ref[...], nv_ref[...] = _adamw_update(w_ref[...], g_ref[...], m_ref[...], v_ref[...])

    spec = pl.BlockSpec((tr, C), lambda i: (i, 0))
    out = jax.ShapeDtypeStruct((R, C), F32)
    return pl.pallas_call(
        body, name=name, grid=(R // tr,), out_shape=(out, out, out),
        in_specs=[spec] * 4, out_specs=(spec, spec, spec),
        compiler_params=_params("parallel"),
    )(w, g, m, v)


def _adamw_slots(w, slots, m, v, name):
    R, C = w.shape
    tc = _tile(C, 512, LANES)
    spec = pl.BlockSpec((R, tc), lambda j: (0, j))
    out = jax.ShapeDtypeStruct((R, C), F32)
    return pl.pallas_call(
        functools.partial(_adamw_slots_body), name=name, grid=(C // tc,), out_shape=(out, out, out, out),
        in_specs=[spec, pl.BlockSpec((4, R, tc), lambda j: (0, 0, j)), spec, spec], out_specs=(spec, spec, spec, spec),
        compiler_params=_params("parallel"),
    )(w, slots, m, v)


def _adamw_slots_body(w_ref, s_ref, m_ref, v_ref, g_ref, d_ref, nm_ref, nv_ref):
    g = s_ref[0].astype(F32)
    for q in range(1, 4):
        g = g + s_ref[q].astype(F32)
    g_ref[...] = g
    d_ref[...], nm_ref[...], nv_ref[...] = _adamw_update(w_ref[...], g, m_ref[...], v_ref[...])


def _cast_job(arrays, dtype):
    C = arrays[0].shape[1]

    def specs(grid):
        total = 1
        for g in grid:
            total *= g
        tc = C // total
        assert tc * total == C and tc % LANES == 0, (C, grid)
        blocks = [pl.BlockSpec((a.shape[0], tc), lambda *ids: (0, _linear_step(grid, ids))) for a in arrays]
        return blocks, list(blocks)

    def each(ins, outs, scratch, step):
        for i_ref, o_ref in zip(ins, outs):
            o_ref[...] = i_ref[...].astype(o_ref.dtype)

    return _Job(ins=arrays, out_shape=[jax.ShapeDtypeStruct(a.shape, dtype) for a in arrays], specs=specs, each=each)


def _adamw_job(w, slots, m, v):
    R, C = w.shape

    n_slices = C // LANES

    def specs(grid):
        total = 1
        for g in grid:
            total *= g
        assert total >= n_slices, (grid, n_slices)
        col = lambda *ids: jnp.minimum(_linear_step(grid, ids), n_slices - 1)
        blk = pl.BlockSpec((R, LANES), lambda *ids: (0, col(*ids)))
        slot_blk = pl.BlockSpec((4, R, LANES), lambda *ids: (0, 0, col(*ids)))
        return [blk, slot_blk, blk, blk], [blk] * 4

    def each(ins, outs, scratch, step):
        @pl.when(step < n_slices)
        def _():
            _adamw_slots_body(*ins, *outs)

    out = jax.ShapeDtypeStruct((R, C), F32)
    return _Job(ins=[w, slots, m, v], out_shape=[out] * 4, specs=specs, each=each)


WEIGHT_NAMES = ("ffn1_norm", "ffn1_w_gate", "ffn1_w_up", "ffn1_w_down", "mix_norm", "w_in", "swa_sinks",
                "swa_out_norm", "sb_out_norm", "w_out", "ffn2_norm", "ffn2_w_gate", "ffn2_w_up", "ffn2_w_down",
                "final_norm")
SMALL_NAMES = ("ffn1_norm", "mix_norm", "swa_sinks", "swa_out_norm", "sb_out_norm", "ffn2_norm", "final_norm")
BIG_ARGS = {"ffn1_gate": ("ffn1_w_gate", True), "ffn1_up": ("ffn1_w_up", True), "ffn1_down": ("ffn1_w_down", False),
            "w_in": ("w_in", True), "w_out": ("w_out", False), "ffn2_gate": ("ffn2_w_gate", True),
            "ffn2_up": ("ffn2_w_up", True), "ffn2_down": ("ffn2_w_down", False)}


def _pack_small(parts):
    padded = [jnp.pad(p.reshape(1, -1), ((0, 0), (0, -p.size % LANES))) for p in parts]
    flat = jnp.concatenate(padded, axis=1)
    flat = jnp.pad(flat, ((0, 0), (0, SMALL_ROWS * LANES - flat.shape[1])))
    return flat.reshape(SMALL_ROWS, LANES)


def _unpack_small(block, shapes):
    flat = block.reshape(-1)
    out, off = [], 0
    for shp in shapes:
        n = 1
        for s in shp:
            n *= s
        out.append(flat[off:off + n].reshape(shp))
        off += n + (-n % LANES)
    return out


def kernel(x, ffn1_norm, ffn1_w_gate, ffn1_w_up, ffn1_w_down, mix_norm, w_in, swa_sinks, swa_out_norm, sb_out_norm, w_out, ffn2_norm, ffn2_w_gate, ffn2_w_up, ffn2_w_down, final_norm, loss_target, m_ffn1_norm, m_ffn1_w_gate, m_ffn1_w_up, m_ffn1_w_down, m_mix_norm, m_w_in, m_swa_sinks, m_swa_out_norm, m_sb_out_norm, m_w_out, m_ffn2_norm, m_ffn2_w_gate, m_ffn2_w_up, m_ffn2_w_down, m_final_norm, v_ffn1_norm, v_ffn1_w_gate, v_ffn1_w_up, v_ffn1_w_down, v_mix_norm, v_w_in, v_swa_sinks, v_swa_out_norm, v_sb_out_norm, v_w_out, v_ffn2_norm, v_ffn2_w_gate, v_ffn2_w_up, v_ffn2_w_down, v_final_norm):
    args = dict(locals())
    B, S, D = x.shape
    T = B * S
    weights = {n: args[n] for n in WEIGHT_NAMES}
    mom_m = {n: args["m_" + n] for n in WEIGHT_NAMES}
    mom_v = {n: args["v_" + n] for n in WEIGHT_NAMES}

    state = {}
    for name in BIG_NAMES:
        arg, transposed = BIG_ARGS[name]
        to_rows = (lambda t: t[0].T) if transposed else (lambda t: t[0])
        state[name] = tuple(to_rows(t[arg]) for t in (weights, mom_m, mom_v))
    comm = _Comm({FIRST_GATHERED: state[FIRST_GATHERED][0].astype(_WIRE_DTYPE)}, state)
    small = {n: weights[n].reshape(1, -1) for n in SMALL_NAMES}

    loss, gx, d_small = _layer_step(x.reshape(T, D), loss_target.reshape(T, D), B, S, small, comm)

    small_shapes = [(1, 1)] + [d_small[n].shape for n in SMALL_NAMES]
    reduced = _small_allreduce(_pack_small([loss[:, :1]] + [d_small[n] for n in SMALL_NAMES]))
    red = _unpack_small(reduced, small_shapes)
    loss_out = red[0].reshape(())
    g_small = dict(zip(SMALL_NAMES, red[1:]))

    grads, deltas, new_m, new_v = {}, {}, {}, {}
    for name in BIG_NAMES:
        arg, transposed = BIG_ARGS[name]
        back = (lambda t: t.T[None]) if transposed else (lambda t: t[None])
        res = comm.updates.get(name)
        if res is None:
            w2, m2, v2 = state[name]
            res = _adamw_slots(w2, comm.slots3(name), m2, v2, "adamw_" + name)
        grads[arg], deltas[arg], new_m[arg], new_v[arg] = [back(t) for t in res]
    shapes1 = [(1, weights[n].size) for n in SMALL_NAMES]
    packed = [_pack_small([t[n].reshape(1, -1) for n in SMALL_NAMES]) for t in (weights, g_small, mom_m, mom_v)]
    upd = _adamw(*packed, "adamw_small")
    for tgt_dict, block in zip((deltas, new_m, new_v), upd):
        for n, val in zip(SMALL_NAMES, _unpack_small(block, shapes1)):
            tgt_dict[n] = val.reshape(weights[n].shape)
    for n in SMALL_NAMES:
        grads[n] = g_small[n].reshape(weights[n].shape)

    return (loss_out, gx.reshape(B, S, D), *[grads[n] for n in WEIGHT_NAMES], *[deltas[n] for n in WEIGHT_NAMES],
            *[new_m[n] for n in WEIGHT_NAMES], *[new_v[n] for n in WEIGHT_NAMES])
```
